```python
import jax, jax.numpy as jnp
from jax import lax
import numpy as np

D_MODEL = 1024
BATCH = 8
SEQ = 8192
DEPTH = 1

N_MEM = 256
HEAD_DIM = 128
HEADS_PER_GROUP = 4
DIL_GROUPS = ((128, 1), (512, 4), (2048, 16))
N_GROUPS = len(DIL_GROUPS)
ATTN_HEADS = N_GROUPS * HEADS_PER_GROUP
ATTN_WIDTH = ATTN_HEADS * HEAD_DIM
ATTN_OUT = HEADS_PER_GROUP * HEAD_DIM
ROT_DIM = HEAD_DIM // 4
ROPE_THETA = 500000.0
CONV_CH = 3 * D_MODEL // 4
CONV_K = 31
N_BRANCH = 2
IN_SPLITS = (ATTN_WIDTH, ATTN_WIDTH, ATTN_WIDTH, CONV_CH, CONV_CH, D_MODEL, D_MODEL)
IN_WIDTH = sum(IN_SPLITS)
CROSS_HEADS = 4
CROSS_HEAD_DIM = D_MODEL // CROSS_HEADS
D_FF = 4 * D_MODEL
EPS = 1e-6

kernel_name = 'hybrid_dilated_attn_conformer_conv_gated'


def rmsnorm(x, g):
    xf = x.astype(jnp.float32)
    y = xf * lax.rsqrt(jnp.mean(xf * xf, axis=-1, keepdims=True) + EPS) * g.astype(jnp.float32)
    return y.astype(x.dtype)


def layernorm(x, g, b):
    xf = x.astype(jnp.float32)
    mu = jnp.mean(xf, axis=-1, keepdims=True)
    var = jnp.mean(jnp.square(xf - mu), axis=-1, keepdims=True)
    y = (xf - mu) * lax.rsqrt(var + EPS) * g.astype(jnp.float32) + b.astype(jnp.float32)
    return y.astype(x.dtype)


def rope_partial(t, pos):
    half = ROT_DIM // 2
    inv_freq = ROPE_THETA ** (-jnp.arange(0, ROT_DIM, 2, dtype=jnp.float32) / ROT_DIM)
    ang = pos[:, None] * inv_freq[None, :]
    cos = jnp.cos(ang)[None, :, None, :]
    sin = jnp.sin(ang)[None, :, None, :]
    tf = t.astype(jnp.float32)
    x1, x2, rest = tf[..., :half], tf[..., half:ROT_DIM], tf[..., ROT_DIM:]
    out = jnp.concatenate([x1 * cos - x2 * sin, x2 * cos + x1 * sin, rest], axis=-1)
    return out.astype(t.dtype)


def dilated_window_attention(q, k, v, window, dilation):
    B, S, H, Dh = q.shape
    L = window // dilation
    span = dilation * L
    Sp = -(-S // span) * span
    M = Sp // dilation
    nb = M // L
    pad = ((0, 0), (0, Sp - S), (0, 0), (0, 0))

    def to_blocks(t):
        t = jnp.pad(t, pad).reshape(B, M, dilation, H, Dh)
        t = t.transpose(0, 3, 2, 1, 4)
        return t.reshape(B, H, dilation, nb, L, Dh)

    def with_prev(t):
        prev = jnp.pad(t, ((0, 0), (0, 0), (0, 0), (1, 0), (0, 0), (0, 0)))[:, :, :, :-1]
        return jnp.concatenate([prev, t], axis=-2)

    qb = to_blocks(q)
    kw = with_prev(to_blocks(k))
    vw = with_prev(to_blocks(v))
    s = jnp.einsum('bhrnqe,bhrnke->bhrnqk', qb, kw).astype(jnp.float32) * (Dh ** -0.5)
    qi = jnp.arange(L)[:, None]
    kj = jnp.arange(2 * L)[None, :]
    band = (kj >= qi) & (kj <= qi + L)
    first = (jnp.arange(nb)[:, None, None] == 0) & (kj[None] < L)
    mask = band[None] & jnp.logical_not(first)
    s = jnp.where(mask, s, -jnp.inf)
    lse = jax.nn.logsumexp(s, axis=-1)
    p = jnp.exp(s - lse[..., None])
    o = jnp.einsum('bhrnqk,bhrnke->bhrnqe', p.astype(v.dtype), vw)
    o = o.reshape(B, H, dilation, M, Dh).transpose(0, 3, 2, 1, 4).reshape(B, Sp, H, Dh)[:, :S]
    lse = lse.reshape(B, H, dilation, M).transpose(0, 3, 2, 1).reshape(B, Sp, H)[:, :S]
    return o, lse


def _fwd_setup_inputs(seed: int = 0) -> dict:
    key = jax.random.key(seed)
    ks = jax.random.split(key, 24)
    f32 = jnp.float32
    nrm = lambda k, shape, fan: jax.random.normal(k, shape, f32) * (fan ** -0.5)
    gain = lambda k, shape: 1.0 + 0.01 * jax.random.normal(k, shape, f32)
    small = lambda k, shape: 0.01 * jax.random.normal(k, shape, f32)
    return {
        'x': jax.random.normal(ks[0], (BATCH, SEQ, D_MODEL), f32),
        'mem': jax.random.normal(ks[1], (BATCH, N_MEM, D_MODEL), f32),
        'g_mix': gain(ks[2], (DEPTH, D_MODEL)),
        'w_in': nrm(ks[3], (DEPTH, D_MODEL, IN_WIDTH), D_MODEL),
        'b_gate': small(ks[4], (DEPTH, N_BRANCH * D_MODEL)),
        'conv_w': nrm(ks[5], (DEPTH, CONV_K, CONV_CH), CONV_K),
        'conv_b': small(ks[6], (DEPTH, CONV_CH)),
        'conv_ln_g': gain(ks[7], (DEPTH, CONV_CH)),
        'conv_ln_b': small(ks[8], (DEPTH, CONV_CH)),
        'w_attn_proj': nrm(ks[9], (DEPTH, ATTN_OUT, D_MODEL), ATTN_OUT),
        'w_conv_proj': nrm(ks[10], (DEPTH, CONV_CH, D_MODEL), CONV_CH),
        'w_out': nrm(ks[11], (DEPTH, D_MODEL, D_MODEL), D_MODEL),
        'g_cross': gain(ks[12], (DEPTH, D_MODEL)),
        'g_mem': gain(ks[13], (DEPTH, D_MODEL)),
        'w_cq': nrm(ks[14], (DEPTH, D_MODEL, D_MODEL), D_MODEL),
        'w_ckv': nrm(ks[15], (DEPTH, D_MODEL, 2 * D_MODEL), D_MODEL),
        'w_co': nrm(ks[16], (DEPTH, D_MODEL, D_MODEL), D_MODEL),
        'g_mlp': gain(ks[17], (DEPTH, D_MODEL)),
        'w_up': nrm(ks[18], (DEPTH, D_MODEL, D_FF), D_MODEL),
        'w_down': nrm(ks[19], (DEPTH, D_FF, D_MODEL), D_FF),
        'g_final': gain(ks[20], (D_MODEL,)),
    }


def _fwd_reference(x, mem, g_mix, w_in, b_gate, conv_w, conv_b, conv_ln_g, conv_ln_b, w_attn_proj,
              w_conv_proj, w_out, g_cross, g_mem, w_cq, w_ckv, w_co, g_mlp, w_up, w_down, g_final):
    B, S, _ = x.shape
    pos = jnp.arange(S, dtype=jnp.float32)
    split_pts = [int(v) for v in np.cumsum(IN_SPLITS)[:-1]]
    for l in range(DEPTH):
        u = rmsnorm(x, g_mix[l])
        z = u @ w_in[l]
        q, k, v, glu_a, glu_b, gate_a, gate_b = jnp.split(z, split_pts, axis=-1)
        q = rope_partial(q.reshape(B, S, ATTN_HEADS, HEAD_DIM), pos)
        k = rope_partial(k.reshape(B, S, ATTN_HEADS, HEAD_DIM), pos)
        v = v.reshape(B, S, ATTN_HEADS, HEAD_DIM)
        q = q.reshape(B, S, N_GROUPS, HEADS_PER_GROUP, HEAD_DIM)
        k = k.reshape(B, S, N_GROUPS, HEADS_PER_GROUP, HEAD_DIM)
        v = v.reshape(B, S, N_GROUPS, HEADS_PER_GROUP, HEAD_DIM)
        outs, lses = [], []
        for g, (win, dil) in enumerate(DIL_GROUPS):
            o_g, lse_g = dilated_window_attention(q[:, :, g], k[:, :, g], v[:, :, g], win, dil)
            outs.append(o_g)
            lses.append(lse_g)
        wts = jax.nn.softmax(jnp.stack(lses, axis=0), axis=0)
        attn = jnp.sum(wts[..., None] * jnp.stack(outs, axis=0).astype(jnp.float32), axis=0)
        y_attn = attn.astype(x.dtype).reshape(B, S, ATTN_OUT) @ w_attn_proj[l]

        c = glu_a * jax.nn.sigmoid(glu_b)
        c = lax.conv_general_dilated(c, conv_w[l].astype(c.dtype)[:, None, :], window_strides=(1,),
                                     padding=[(CONV_K - 1, 0)], dimension_numbers=('NWC', 'WIO', 'NWC'),
                                     feature_group_count=CONV_CH) + conv_b[l]
        c = jax.nn.silu(layernorm(c, conv_ln_g[l], conv_ln_b[l]))
        y_conv = c @ w_conv_proj[l]

        bg_a, bg_b = jnp.split(b_gate[l], 2)
        merged = jax.nn.sigmoid(gate_a + bg_a) * y_attn + jax.nn.sigmoid(gate_b + bg_b) * y_conv
        x = x + merged @ w_out[l]

        uq = rmsnorm(x, g_cross[l])
        m = rmsnorm(mem, g_mem[l])
        cq = (uq @ w_cq[l]).reshape(B, S, CROSS_HEADS, CROSS_HEAD_DIM)
        ck, cv = jnp.split(m @ w_ckv[l], 2, axis=-1)
        ck = ck.reshape(B, N_MEM, CROSS_HEADS, CROSS_HEAD_DIM)
        cv = cv.reshape(B, N_MEM, CROSS_HEADS, CROSS_HEAD_DIM)
        sc = jnp.einsum('bshe,bmhe->bhsm', cq, ck).astype(jnp.float32) * (CROSS_HEAD_DIM ** -0.5)
        pc = jax.nn.softmax(sc, axis=-1).astype(cv.dtype)
        co = jnp.einsum('bhsm,bmhe->bshe', pc, cv).reshape(B, S, D_MODEL)
        x = x + co @ w_co[l]

        h = jnp.square(jax.nn.relu(rmsnorm(x, g_mlp[l]) @ w_up[l]))
        x = x + h @ w_down[l]
    return rmsnorm(x, g_final)


import jax as _jax
import jax.numpy as _jnp

TWIN_FORMAT = 'train_step'
FWD_PARAMS = ['x', 'mem', 'g_mix', 'w_in', 'b_gate', 'conv_w', 'conv_b', 'conv_ln_g', 'conv_ln_b', 'w_attn_proj', 'w_conv_proj', 'w_out', 'g_cross', 'g_mem', 'w_cq', 'w_ckv', 'w_co', 'g_mlp', 'w_up', 'w_down', 'g_final']
TWIN_WEIGHTS = ['g_mix', 'w_in', 'b_gate', 'conv_w', 'conv_b', 'conv_ln_g', 'conv_ln_b', 'w_attn_proj', 'w_conv_proj', 'w_out', 'g_cross', 'g_mem', 'w_cq', 'w_ckv', 'w_co', 'g_mlp', 'w_up', 'w_down', 'g_final']
TWIN_DIFF_INPUT = 'x'
TWIN_INPUTS = ['x', 'mem', 'g_mix', 'w_in', 'b_gate', 'conv_w', 'conv_b', 'conv_ln_g', 'conv_ln_b', 'w_attn_proj', 'w_conv_proj', 'w_out', 'g_cross', 'g_mem', 'w_cq', 'w_ckv', 'w_co', 'g_mlp', 'w_up', 'w_down', 'g_final', 'loss_target', 'm_g_mix', 'm_w_in', 'm_b_gate', 'm_conv_w', 'm_conv_b', 'm_conv_ln_g', 'm_conv_ln_b', 'm_w_attn_proj', 'm_w_conv_proj', 'm_w_out', 'm_g_cross', 'm_g_mem', 'm_w_cq', 'm_w_ckv', 'm_w_co', 'm_g_mlp', 'm_w_up', 'm_w_down', 'm_g_final', 'v_g_mix', 'v_w_in', 'v_b_gate', 'v_conv_w', 'v_conv_b', 'v_conv_ln_g', 'v_conv_ln_b', 'v_w_attn_proj', 'v_w_conv_proj', 'v_w_out', 'v_g_cross', 'v_g_mem', 'v_w_cq', 'v_w_ckv', 'v_w_co', 'v_g_mlp', 'v_w_up', 'v_w_down', 'v_g_final']
TWIN_OUTPUTS = ['loss', 'grad_x', 'grad_g_mix', 'grad_w_in', 'grad_b_gate', 'grad_conv_w', 'grad_conv_b', 'grad_conv_ln_g', 'grad_conv_ln_b', 'grad_w_attn_proj', 'grad_w_conv_proj', 'grad_w_out', 'grad_g_cross', 'grad_g_mem', 'grad_w_cq', 'grad_w_ckv', 'grad_w_co', 'grad_g_mlp', 'grad_w_up', 'grad_w_down', 'grad_g_final', 'delta_g_mix', 'delta_w_in', 'delta_b_gate', 'delta_conv_w', 'delta_conv_b', 'delta_conv_ln_g', 'delta_conv_ln_b', 'delta_w_attn_proj', 'delta_w_conv_proj', 'delta_w_out', 'delta_g_cross', 'delta_g_mem', 'delta_w_cq', 'delta_w_ckv', 'delta_w_co', 'delta_g_mlp', 'delta_w_up', 'delta_w_down', 'delta_g_final', 'new_m_g_mix', 'new_m_w_in', 'new_m_b_gate', 'new_m_conv_w', 'new_m_conv_b', 'new_m_conv_ln_g', 'new_m_conv_ln_b', 'new_m_w_attn_proj', 'new_m_w_conv_proj', 'new_m_w_out', 'new_m_g_cross', 'new_m_g_mem', 'new_m_w_cq', 'new_m_w_ckv', 'new_m_w_co', 'new_m_g_mlp', 'new_m_w_up', 'new_m_w_down', 'new_m_g_final', 'new_v_g_mix', 'new_v_w_in', 'new_v_b_gate', 'new_v_conv_w', 'new_v_conv_b', 'new_v_conv_ln_g', 'new_v_conv_ln_b', 'new_v_w_attn_proj', 'new_v_w_conv_proj', 'new_v_w_out', 'new_v_g_cross', 'new_v_g_mem', 'new_v_w_cq', 'new_v_w_ckv', 'new_v_w_co', 'new_v_g_mlp', 'new_v_w_up', 'new_v_w_down', 'new_v_g_final']
TWIN_LEAF_KINDS = {'loss': 'loss', 'grad_x': 'grad_x', 'grad_g_mix': 'grad_w', 'grad_w_in': 'grad_w', 'grad_b_gate': 'grad_w', 'grad_conv_w': 'grad_w', 'grad_conv_b': 'grad_w', 'grad_conv_ln_g': 'grad_w', 'grad_conv_ln_b': 'grad_w', 'grad_w_attn_proj': 'grad_w', 'grad_w_conv_proj': 'grad_w', 'grad_w_out': 'grad_w', 'grad_g_cross': 'grad_w', 'grad_g_mem': 'grad_w', 'grad_w_cq': 'grad_w', 'grad_w_ckv': 'grad_w', 'grad_w_co': 'grad_w', 'grad_g_mlp': 'grad_w', 'grad_w_up': 'grad_w', 'grad_w_down': 'grad_w', 'grad_g_final': 'grad_w', 'delta_g_mix': 'delta_w', 'delta_w_in': 'delta_w', 'delta_b_gate': 'delta_w', 'delta_conv_w': 'delta_w', 'delta_conv_b': 'delta_w', 'delta_conv_ln_g': 'delta_w', 'delta_conv_ln_b': 'delta_w', 'delta_w_attn_proj': 'delta_w', 'delta_w_conv_proj': 'delta_w', 'delta_w_out': 'delta_w', 'delta_g_cross': 'delta_w', 'delta_g_mem': 'delta_w', 'delta_w_cq': 'delta_w', 'delta_w_ckv': 'delta_w', 'delta_w_co': 'delta_w', 'delta_g_mlp': 'delta_w', 'delta_w_up': 'delta_w', 'delta_w_down': 'delta_w', 'delta_g_final': 'delta_w', 'new_m_g_mix': 'new_m', 'new_m_w_in': 'new_m', 'new_m_b_gate': 'new_m', 'new_m_conv_w': 'new_m', 'new_m_conv_b': 'new_m', 'new_m_conv_ln_g': 'new_m', 'new_m_conv_ln_b': 'new_m', 'new_m_w_attn_proj': 'new_m', 'new_m_w_conv_proj': 'new_m', 'new_m_w_out': 'new_m', 'new_m_g_cross': 'new_m', 'new_m_g_mem': 'new_m', 'new_m_w_cq': 'new_m', 'new_m_w_ckv': 'new_m', 'new_m_w_co': 'new_m', 'new_m_g_mlp': 'new_m', 'new_m_w_up': 'new_m', 'new_m_w_down': 'new_m', 'new_m_g_final': 'new_m', 'new_v_g_mix': 'new_v', 'new_v_w_in': 'new_v', 'new_v_b_gate': 'new_v', 'new_v_conv_w': 'new_v', 'new_v_conv_b': 'new_v', 'new_v_conv_ln_g': 'new_v', 'new_v_conv_ln_b': 'new_v', 'new_v_w_attn_proj': 'new_v', 'new_v_w_conv_proj': 'new_v', 'new_v_w_out': 'new_v', 'new_v_g_cross': 'new_v', 'new_v_g_mem': 'new_v', 'new_v_w_cq': 'new_v', 'new_v_w_ckv': 'new_v', 'new_v_w_co': 'new_v', 'new_v_g_mlp': 'new_v', 'new_v_w_up': 'new_v', 'new_v_w_down': 'new_v', 'new_v_g_final': 'new_v'}


def _forward(args):
    return _fwd_reference(*[args[k] for k in FWD_PARAMS])


def _output_shape():
    def fwd():
        inp = _fwd_setup_inputs(0)
        return _fwd_reference(*[inp[k] for k in FWD_PARAMS])
    out = _jax.eval_shape(fwd)
    return out.shape, out.dtype

N_MICROBATCH = 1
ADAM_LR = 0.001
ADAM_B1 = 0.9
ADAM_B2 = 0.999
ADAM_EPS = 1e-08
ADAM_WD = 0.01
ADAM_STEP = 10
PER_EXAMPLE_BATCH_AXIS = {'x': 0, 'mem': 0, 'loss_target': 0}
SHARED_INPUTS = []
_WEIGHT_DTYPES = {'g_mix': _jnp.float32, 'w_in': _jnp.float32, 'b_gate': _jnp.float32, 'conv_w': _jnp.float32, 'conv_b': _jnp.float32, 'conv_ln_g': _jnp.float32, 'conv_ln_b': _jnp.float32, 'w_attn_proj': _jnp.float32, 'w_conv_proj': _jnp.float32, 'w_out': _jnp.float32, 'g_cross': _jnp.float32, 'g_mem': _jnp.float32, 'w_cq': _jnp.float32, 'w_ckv': _jnp.float32, 'w_co': _jnp.float32, 'g_mlp': _jnp.float32, 'w_up': _jnp.float32, 'w_down': _jnp.float32, 'g_final': _jnp.float32}
MOMENT_SCALE = {'g_mix': 1.091570e-01, 'w_in': 3.811205e-02, 'b_gate': 2.491368e-02, 'conv_w': 1.023987e-01, 'conv_b': 2.144413e-01, 'conv_ln_g': 1.281266e-01, 'conv_ln_b': 1.216033e-01, 'w_attn_proj': 2.603885e-02, 'w_conv_proj': 8.671880e-02, 'w_out': 8.797972e-02, 'g_cross': 3.110095e-02, 'g_mem': 4.150095e-02, 'w_cq': 2.719157e-02, 'w_ckv': 2.748850e-02, 'w_co': 2.802530e-02, 'g_mlp': 2.249781e-01, 'w_up': 1.100060e-01, 'w_down': 2.044241e-01, 'g_final': 6.464140e+01}


def _to_microbatches(a, axis):
    t = _jnp.moveaxis(a, axis, 0)
    t = t.reshape((N_MICROBATCH, t.shape[0] // N_MICROBATCH) + t.shape[1:])
    return _jnp.moveaxis(t, 1, axis + 1)


def setup_inputs(seed: int = 0) -> dict:
    inp = _fwd_setup_inputs(seed)
    key = _jax.random.fold_in(_jax.random.key(seed), 7919)
    shape, _ = _output_shape()
    out = dict(inp)
    out["loss_target"] = _jax.random.normal(_jax.random.fold_in(key, 0), shape, _jnp.float32)
    for i, name in enumerate(TWIN_WEIGHTS):
        w = inp[name].astype(_jnp.float32)
        if MOMENT_SCALE is None:
            s = _jnp.sqrt(_jnp.mean(_jnp.square(w)) + 1e-30)
        else:
            s = MOMENT_SCALE[name]
        km, kv = _jax.random.split(_jax.random.fold_in(key, i + 1))
        out[name] = w
        out["m_" + name] = s * _jax.random.normal(km, w.shape, _jnp.float32)
        out["v_" + name] = (s * s) * _jax.random.uniform(kv, w.shape, _jnp.float32, 0.5, 1.5)
    if N_MICROBATCH > 1:
        for name, axis in PER_EXAMPLE_BATCH_AXIS.items():
            out[name] = _to_microbatches(out[name], axis)
    return {'x': out['x'], 'mem': out['mem'], 'g_mix': out['g_mix'], 'w_in': out['w_in'], 'b_gate': out['b_gate'], 'conv_w': out['conv_w'], 'conv_b': out['conv_b'], 'conv_ln_g': out['conv_ln_g'], 'conv_ln_b': out['conv_ln_b'], 'w_attn_proj': out['w_attn_proj'], 'w_conv_proj': out['w_conv_proj'], 'w_out': out['w_out'], 'g_cross': out['g_cross'], 'g_mem': out['g_mem'], 'w_cq': out['w_cq'], 'w_ckv': out['w_ckv'], 'w_co': out['w_co'], 'g_mlp': out['g_mlp'], 'w_up': out['w_up'], 'w_down': out['w_down'], 'g_final': out['g_final'], 'loss_target': out['loss_target'], 'm_g_mix': out['m_g_mix'], 'm_w_in': out['m_w_in'], 'm_b_gate': out['m_b_gate'], 'm_conv_w': out['m_conv_w'], 'm_conv_b': out['m_conv_b'], 'm_conv_ln_g': out['m_conv_ln_g'], 'm_conv_ln_b': out['m_conv_ln_b'], 'm_w_attn_proj': out['m_w_attn_proj'], 'm_w_conv_proj': out['m_w_conv_proj'], 'm_w_out': out['m_w_out'], 'm_g_cross': out['m_g_cross'], 'm_g_mem': out['m_g_mem'], 'm_w_cq': out['m_w_cq'], 'm_w_ckv': out['m_w_ckv'], 'm_w_co': out['m_w_co'], 'm_g_mlp': out['m_g_mlp'], 'm_w_up': out['m_w_up'], 'm_w_down': out['m_w_down'], 'm_g_final': out['m_g_final'], 'v_g_mix': out['v_g_mix'], 'v_w_in': out['v_w_in'], 'v_b_gate': out['v_b_gate'], 'v_conv_w': out['v_conv_w'], 'v_conv_b': out['v_conv_b'], 'v_conv_ln_g': out['v_conv_ln_g'], 'v_conv_ln_b': out['v_conv_ln_b'], 'v_w_attn_proj': out['v_w_attn_proj'], 'v_w_conv_proj': out['v_w_conv_proj'], 'v_w_out': out['v_w_out'], 'v_g_cross': out['v_g_cross'], 'v_g_mem': out['v_g_mem'], 'v_w_cq': out['v_w_cq'], 'v_w_ckv': out['v_w_ckv'], 'v_w_co': out['v_w_co'], 'v_g_mlp': out['v_g_mlp'], 'v_w_up': out['v_w_up'], 'v_w_down': out['v_w_down'], 'v_g_final': out['v_g_final']}


def _loss(weights, diff, rest, loss_target):
    with _jax.named_scope("forward"):
        args = {**rest, TWIN_DIFF_INPUT: diff, **{k: w.astype(_WEIGHT_DTYPES[k]) for k, w in weights.items()}}
        y = _forward(args)
    with _jax.named_scope("loss_head"):
        err = _jnp.square(y.astype(_jnp.float32) - loss_target)
        return 0.5 * _jnp.sum(_jnp.mean(err, axis=-1)) if err.ndim else 0.5 * err


def _adamw(w, g, m, v):
    m = ADAM_B1 * m + (1.0 - ADAM_B1) * g
    v = ADAM_B2 * v + (1.0 - ADAM_B2) * _jnp.square(g)
    m_hat = m / (1.0 - ADAM_B1 ** ADAM_STEP)
    v_hat = v / (1.0 - ADAM_B2 ** ADAM_STEP)
    delta = -ADAM_LR * (m_hat / (_jnp.sqrt(v_hat) + ADAM_EPS) + ADAM_WD * w)
    return delta, m, v


def reference(x, mem, g_mix, w_in, b_gate, conv_w, conv_b, conv_ln_g, conv_ln_b, w_attn_proj, w_conv_proj, w_out, g_cross, g_mem, w_cq, w_ckv, w_co, g_mlp, w_up, w_down, g_final, loss_target, m_g_mix, m_w_in, m_b_gate, m_conv_w, m_conv_b, m_conv_ln_g, m_conv_ln_b, m_w_attn_proj, m_w_conv_proj, m_w_out, m_g_cross, m_g_mem, m_w_cq, m_w_ckv, m_w_co, m_g_mlp, m_w_up, m_w_down, m_g_final, v_g_mix, v_w_in, v_b_gate, v_conv_w, v_conv_b, v_conv_ln_g, v_conv_ln_b, v_w_attn_proj, v_w_conv_proj, v_w_out, v_g_cross, v_g_mem, v_w_cq, v_w_ckv, v_w_co, v_g_mlp, v_w_up, v_w_down, v_g_final):
    given = dict(x=x, mem=mem, g_mix=g_mix, w_in=w_in, b_gate=b_gate, conv_w=conv_w, conv_b=conv_b, conv_ln_g=conv_ln_g, conv_ln_b=conv_ln_b, w_attn_proj=w_attn_proj, w_conv_proj=w_conv_proj, w_out=w_out, g_cross=g_cross, g_mem=g_mem, w_cq=w_cq, w_ckv=w_ckv, w_co=w_co, g_mlp=g_mlp, w_up=w_up, w_down=w_down, g_final=g_final, loss_target=loss_target, m_g_mix=m_g_mix, m_w_in=m_w_in, m_b_gate=m_b_gate, m_conv_w=m_conv_w, m_conv_b=m_conv_b, m_conv_ln_g=m_conv_ln_g, m_conv_ln_b=m_conv_ln_b, m_w_attn_proj=m_w_attn_proj, m_w_conv_proj=m_w_conv_proj, m_w_out=m_w_out, m_g_cross=m_g_cross, m_g_mem=m_g_mem, m_w_cq=m_w_cq, m_w_ckv=m_w_ckv, m_w_co=m_w_co, m_g_mlp=m_g_mlp, m_w_up=m_w_up, m_w_down=m_w_down, m_g_final=m_g_final, v_g_mix=v_g_mix, v_w_in=v_w_in, v_b_gate=v_b_gate, v_conv_w=v_conv_w, v_conv_b=v_conv_b, v_conv_ln_g=v_conv_ln_g, v_conv_ln_b=v_conv_ln_b, v_w_attn_proj=v_w_attn_proj, v_w_conv_proj=v_w_conv_proj, v_w_out=v_w_out, v_g_cross=v_g_cross, v_g_mem=v_g_mem, v_w_cq=v_w_cq, v_w_ckv=v_w_ckv, v_w_co=v_w_co, v_g_mlp=v_g_mlp, v_w_up=v_w_up, v_w_down=v_w_down, v_g_final=v_g_final)
    weights = {n: given[n] for n in TWIN_WEIGHTS}
    shared = {n: given[n] for n in SHARED_INPUTS}
    per_example = {n: given[n] for n in ['x', 'mem']}
    grad_fn = _jax.value_and_grad(_loss, argnums=(0, 1))

    def one_microbatch(ex, loss_target):
        ex = dict(ex)
        diff = ex.pop(TWIN_DIFF_INPUT)
        return grad_fn(weights, diff, {**shared, **ex}, loss_target)

    if N_MICROBATCH == 1:
        loss, (grad_w, grad_x) = one_microbatch(per_example, given["loss_target"])
    else:
        def body(carry, xs):
            loss_sum, grad_sum = carry
            l_k, (gw_k, gx_k) = one_microbatch(xs[0], xs[1])
            with _jax.named_scope("update"):
                return (loss_sum + l_k, _jax.tree.map(_jnp.add, grad_sum, gw_k)), gx_k

        init = (_jnp.zeros((), _jnp.float32), _jax.tree.map(_jnp.zeros_like, weights))
        (loss, grad_w), grad_x = _jax.lax.scan(body, init, (per_example, given["loss_target"]))
    with _jax.named_scope("update"):
        delta_w, new_m, new_v = {}, {}, {}
        for n in TWIN_WEIGHTS:
            delta_w[n], new_m[n], new_v[n] = _adamw(weights[n], grad_w[n], given["m_" + n], given["v_" + n])
    return (loss, grad_x, *[grad_w[n] for n in TWIN_WEIGHTS], *[delta_w[n] for n in TWIN_WEIGHTS],
            *[new_m[n] for n in TWIN_WEIGHTS], *[new_v[n] for n in TWIN_WEIGHTS])
```

```python
import functools

import jax
import jax.numpy as jnp
from jax import lax
from jax.experimental import pallas as pl
from jax.experimental.pallas import tpu as pltpu

F32 = jnp.float32
BF = jnp.bfloat16

D_MODEL = 1024
N_MEM = 256
HEAD_DIM = 128
HEADS_PER_GROUP = 4
DILATIONS = (1, 4, 16)
BLK = 128
GROUP_W = HEADS_PER_GROUP * HEAD_DIM
ATTN_WIDTH = 3 * GROUP_W
ROT_DIM = 32
ROPE_THETA = 500000.0
CONV_CH = 768
CONV_K = 31
CONV_KP = 32
IN_WIDTH = 8192
CROSS_HEADS = 4
CROSS_HEAD_DIM = 256
D_FF = 4096
EPS = 1e-6
ATTN_SCALE = HEAD_DIM ** -0.5
CROSS_SCALE = CROSS_HEAD_DIM ** -0.5
NEG = -1e30

ADAM_LR = 0.001
ADAM_B1 = 0.9
ADAM_B2 = 0.999
ADAM_EPS = 1e-08
ADAM_WD = 0.01
ADAM_STEP = 10

LANES = 128
VMEM_LIMIT = 56 * 1024 * 1024
MESH = pl.DeviceIdType.MESH
ANY = pl.BlockSpec(memory_space=pl.ANY)

GLU_A_COL = 3 * ATTN_WIDTH
GLU_B_COL = GLU_A_COL + CONV_CH
GATE_A_COL = GLU_B_COL + CONV_CH
GATE_B_COL = GATE_A_COL + D_MODEL


def _params(sem=None):
    return pltpu.CompilerParams(dimension_semantics=sem, vmem_limit_bytes=VMEM_LIMIT)


def _dot(a, b):
    return lax.dot_general(a, b, (((1,), (0,)), ((), ())), preferred_element_type=F32)


def _dot_nt(a, b):
    return lax.dot_general(a, b, (((1,), (1,)), ((), ())), preferred_element_type=F32)


def _dot_tn(a, b):
    return lax.dot_general(a, b, (((0,), (0,)), ((), ())), preferred_element_type=F32)


def _sig(x):
    return 1.0 / (1.0 + jnp.exp(-x))


def _rowcall(name, fn, n_rows, tile, ins, params, outs, accs=()):
    tile = min(tile, n_rows)
    ni, npar, no, na = len(ins), len(params), len(outs), len(accs)

    def kern(*refs):
        in_refs = refs[:ni + npar]
        o_refs = refs[ni + npar:ni + npar + no]
        a_refs = refs[ni + npar + no:]
        vals = fn(*[r[...] for r in in_refs])
        for r, v in zip(o_refs, vals[:no]):
            r[...] = v.astype(r.dtype)
        if na:
            @pl.when(pl.program_id(0) == 0)
            def _():
                for r in a_refs:
                    r[...] = jnp.zeros_like(r)
            for r, v in zip(a_refs, vals[no:]):
                r[...] += v

    in_specs = []
    arrays = []
    for spec in ins:
        arr, width, cb = spec[0], spec[1], spec[2]
        rb = spec[3] if len(spec) > 3 else 0
        in_specs.append(pl.BlockSpec((tile, width), functools.partial(lambda i, cb, rb: (i + rb, cb), cb=cb, rb=rb)))
        arrays.append(arr)
    for p in params:
        in_specs.append(pl.BlockSpec(p.shape, lambda i: (0, 0)))
        arrays.append(p)
    out_specs = [pl.BlockSpec((tile, w), lambda i: (i, 0)) for w, _ in outs]
    out_specs += [pl.BlockSpec(s, lambda i: (0, 0)) for s in accs]
    out_shape = [jax.ShapeDtypeStruct((n_rows, w), dt) for w, dt in outs]
    out_shape += [jax.ShapeDtypeStruct(s, F32) for s in accs]
    res = pl.pallas_call(
        kern, name=name, grid=(n_rows // tile,), in_specs=in_specs, out_specs=out_specs, out_shape=out_shape,
        compiler_params=_params(("arbitrary",) if na else ("parallel",)),
    )(*arrays)
    return res


def _mm(name, a, w3, *, nt=False, extras=(), epi=None, out_dtypes=(F32,), tm=None, tn=None, tk=None):
    m, ka = a.shape
    ns, r, cs = w3.shape
    if not nt:
        k_dim, n = r, ns * cs
        tn = tn or min(cs, 1024)
        tk = tk or min(k_dim, 1024)
        nbs = cs // tn
        w_spec = pl.BlockSpec((None, tk, tn), lambda i, j, k: (j // nbs, k, j % nbs))
    else:
        k_dim, n = ns * cs, r
        tn = tn or min(r, 1024)
        tk = tk or min(cs, 1024)
        kbs = cs // tk
        w_spec = pl.BlockSpec((None, tn, tk), lambda i, j, k: (k // kbs, j, k % kbs))
    assert ka == k_dim, (name, a.shape, w3.shape)
    tm = tm or min(m, 1024)
    nk = k_dim // tk
    ne, no = len(extras), len(out_dtypes)

    def kern(a_ref, w_ref, *rest):
        e_refs = rest[:ne]
        o_refs = rest[ne:ne + no]
        acc = rest[ne + no]
        k = pl.program_id(2)

        @pl.when(k == 0)
        def _():
            acc[...] = jnp.zeros_like(acc)

        av = a_ref[...].astype(BF)
        acc[...] += _dot_nt(av, w_ref[...]) if nt else _dot(av, w_ref[...])

        @pl.when(k == nk - 1)
        def _():
            res = acc[...]
            vals = epi(res, *[e[...] for e in e_refs]) if epi else (res,)
            for o, v in zip(o_refs, vals):
                o[...] = v.astype(o.dtype)

    in_specs = [pl.BlockSpec((tm, tk), lambda i, j, k: (i, k)), w_spec]
    in_specs += [pl.BlockSpec((tm, tn), lambda i, j, k: (i, j)) for _ in extras]
    res = pl.pallas_call(
        kern, name=name, grid=(m // tm, n // tn, nk), in_specs=in_specs,
        out_specs=[pl.BlockSpec((tm, tn), lambda i, j, k: (i, j)) for _ in out_dtypes],
        out_shape=[jax.ShapeDtypeStruct((m, n), dt) for dt in out_dtypes],
        scratch_shapes=[pltpu.VMEM((tm, tn), F32)],
        compiler_params=_params(("parallel", "parallel", "arbitrary")),
    )(a, w3, *extras)
    return res[0] if no == 1 else res


def _mm_tn(name, a, b, tm=None, tn=None, tk=None):
    t, ka = a.shape
    _, n = b.shape
    tm = tm or min(ka, 1024)
    tn = tn or min(n, 1024)
    tk = tk or min(t, 512)

    def kern(a_ref, b_ref, o_ref):
        @pl.when(pl.program_id(2) == 0)
        def _():
            o_ref[...] = jnp.zeros_like(o_ref)
        o_ref[...] += _dot_tn(a_ref[...].astype(BF), b_ref[...].astype(BF))

    return pl.pallas_call(
        kern, name=name, grid=(ka // tm, n // tn, t // tk),
        in_specs=[pl.BlockSpec((tk, tm), lambda i, j, k: (k, i)), pl.BlockSpec((tk, tn), lambda i, j, k: (k, j))],
        out_specs=pl.BlockSpec((tm, tn), lambda i, j, k: (i, j)),
        out_shape=jax.ShapeDtypeStruct((ka, n), F32),
        compiler_params=_params(("parallel", "parallel", "arbitrary")),
    )(a, b)


def _rms(x, g):
    return x * lax.rsqrt(jnp.mean(x * x, axis=-1, keepdims=True) + EPS) * g


def _rms_bwd(x, g, dy):
    r = lax.rsqrt(jnp.mean(x * x, axis=-1, keepdims=True) + EPS)
    xh = x * r
    dxh = dy * g
    dx = r * (dxh - xh * jnp.mean(dxh * xh, axis=-1, keepdims=True))
    return dx, jnp.sum(dy * xh, axis=0, keepdims=True)


def _rot(t, c, s):
    lane = lax.broadcasted_iota(jnp.int32, t.shape, 1)
    swapped = jnp.where(lane < ROT_DIM // 2, pltpu.roll(t, HEAD_DIM - ROT_DIM // 2, 1), pltpu.roll(t, ROT_DIM // 2, 1))
    return t * c + swapped * s


def _rope_tables(t):
    half = ROT_DIM // 2
    pos = jnp.arange(t, dtype=F32)
    inv_freq = ROPE_THETA ** (-jnp.arange(0, ROT_DIM, 2, dtype=F32) / ROT_DIM)
    ang = pos[:, None] * inv_freq[None, :]
    cos, sin = jnp.cos(ang), jnp.sin(ang)
    ones = jnp.ones((t, HEAD_DIM - ROT_DIM), F32)
    c_tab = jnp.concatenate([cos, cos, ones], axis=1)
    s_tab = jnp.concatenate([-sin, sin, 0.0 * ones], axis=1)
    return c_tab, s_tab


def _heads(t):
    return [t[:, h * HEAD_DIM:(h + 1) * HEAD_DIM] for h in range(t.shape[1] // HEAD_DIM)]


def _rope_fwd_fn(zq, zk, zv, c, s):
    q = jnp.concatenate([_rot(th, c, s) for th in _heads(zq)], axis=1)
    k = jnp.concatenate([_rot(th, c, s) for th in _heads(zk)], axis=1)
    return q, k, zv


def _rope_bwd_fn(dq0, dq1, dq2, dk0, dk1, dk2, dv0, dv1, dv2, c, s):
    parts = []
    for grp in (dq0, dq1, dq2, dk0, dk1, dk2):
        parts += [_rot(th, c, -s) for th in _heads(grp)]
    parts += [dv0, dv1, dv2]
    return (jnp.concatenate(parts, axis=1),)


def _merge_fn(o0, o1, o2, l0, l1, l2):
    m = jnp.maximum(jnp.maximum(l0, l1), l2)
    e0, e1, e2 = jnp.exp(l0 - m), jnp.exp(l1 - m), jnp.exp(l2 - m)
    tot = e0 + e1 + e2
    return (e0 * o0 + e1 * o1 + e2 * o2) / tot, m + jnp.log(tot)


def _ln_parts(c1):
    mu = jnp.mean(c1, axis=-1, keepdims=True)
    xc = c1 - mu
    r = lax.rsqrt(jnp.mean(xc * xc, axis=-1, keepdims=True) + EPS)
    return xc * r, r


def _ln_silu_fn(c1, g, b):
    xh, _ = _ln_parts(c1)
    yl = xh * g + b
    return (yl * _sig(yl),)


def _ln_silu_bwd_fn(c1, dout, g, b):
    xh, r = _ln_parts(c1)
    yl = xh * g + b
    s = _sig(yl)
    dyl = dout * (s + yl * s * (1.0 - s))
    dxh = dyl * g
    dx = r * (dxh - jnp.mean(dxh, axis=-1, keepdims=True) - xh * jnp.mean(dxh * xh, axis=-1, keepdims=True))
    return dx, jnp.sum(dyl * xh, axis=0, keepdims=True), jnp.sum(dyl, axis=0, keepdims=True)


def _gate_fn(ya, yc, ga, gb, bg):
    sa = _sig(ga + bg[:, :D_MODEL])
    sb = _sig(gb + bg[:, D_MODEL:])
    return (sa * ya + sb * yc,)


def _gate_bwd_fn(dm, ya, yc, ga, gb, bg):
    sa = _sig(ga + bg[:, :D_MODEL])
    sb = _sig(gb + bg[:, D_MODEL:])
    dga = dm * ya * sa * (1.0 - sa)
    dgb = dm * yc * sb * (1.0 - sb)
    dgate = jnp.concatenate([dga, dgb], axis=1)
    return dm * sa, dm * sb, dgate, jnp.sum(dgate, axis=0, keepdims=True)


def _final_fn(x3, tgt, g):
    err = _rms(x3, g) - tgt
    lrow = jnp.sum(err * err, axis=-1, keepdims=True) * (0.5 / D_MODEL)
    lsum = jnp.sum(lrow, axis=0, keepdims=True)
    dx, dg = _rms_bwd(x3, g, err * (1.0 / D_MODEL))
    return dx, jnp.broadcast_to(lsum, (1, LANES)), dg


def _attn_geometry(t, dil):
    cls = t // dil
    rows = min(4 * BLK, cls)
    return rows, rows // BLK, cls // rows


def _band_masks():
    row = lax.broadcasted_iota(jnp.int32, (BLK, BLK), 0)
    col = lax.broadcasted_iota(jnp.int32, (BLK, BLK), 1)
    return row, col


def _attn_fwd(name, q, k, v, t, dil):
    rows, nbk, spc = _attn_geometry(t, dil)
    nblk = t // BLK

    def kern(q_ref, k_ref, kh_ref, v_ref, vh_ref, o_ref, l_ref):
        i = pl.program_id(0)
        first_shift = jnp.where(i % spc == 0, BLK, 0)
        row, col = _band_masks()
        for h in range(HEADS_PER_GROUP):
            hs = slice(h * HEAD_DIM, (h + 1) * HEAD_DIM)
            for b in range(nbk):
                rs = slice(b * BLK, (b + 1) * BLK)
                qb, kc, vc = q_ref[rs, hs], k_ref[rs, hs], v_ref[rs, hs]
                if b == 0:
                    kp, vp, shift = kh_ref[:, hs], vh_ref[:, hs], first_shift
                else:
                    ps = slice((b - 1) * BLK, b * BLK)
                    kp, vp, shift = k_ref[ps, hs], v_ref[ps, hs], 0
                sc = jnp.where(col <= row, _dot_nt(qb, kc) * ATTN_SCALE, NEG)
                sp = jnp.where(col >= row + shift, _dot_nt(qb, kp) * ATTN_SCALE, NEG)
                m = jnp.maximum(jnp.max(sc, axis=1, keepdims=True), jnp.max(sp, axis=1, keepdims=True))
                pc, pp = jnp.exp(sc - m), jnp.exp(sp - m)
                tot = jnp.sum(pc, axis=1, keepdims=True) + jnp.sum(pp, axis=1, keepdims=True)
                o_ref[rs, hs] = (_dot(pc.astype(BF), vc) + _dot(pp.astype(BF), vp)) / tot
                l_ref[rs, hs] = jnp.broadcast_to(m + jnp.log(tot), (BLK, HEAD_DIM))

    def cur(cb):
        return pl.BlockSpec((rows, GROUP_W), lambda i: (i, cb))

    def halo(cb):
        return pl.BlockSpec((BLK, GROUP_W), lambda i: (jnp.maximum(i * nbk - 1, 0), cb))

    (qa, qc), (ka, kc_), (va, vc_) = q, k, v
    return pl.pallas_call(
        kern, name=name, grid=(t // rows,),
        in_specs=[cur(qc), cur(kc_), halo(kc_), cur(vc_), halo(vc_)],
        out_specs=[pl.BlockSpec((rows, GROUP_W), lambda i: (i, 0))] * 2,
        out_shape=[jax.ShapeDtypeStruct((t, GROUP_W), F32)] * 2,
        compiler_params=_params(("parallel",)),
    )(qa, ka, ka, va, va)


def _attn_bwd(name, q, k, v, da, at, lse, t, dil):
    rows, nbk, spc = _attn_geometry(t, dil)
    nblk = t // BLK

    def kern(q_ref, qn_ref, k_ref, kh_ref, v_ref, vh_ref, da_ref, dan_ref, at_ref, atn_ref, ls_ref, lsn_ref,
             dq_ref, dk_ref, dv_ref):
        i = pl.program_id(0)
        first_shift = jnp.where(i % spc == 0, BLK, 0)
        next_shift = jnp.where((i + 1) % spc == 0, BLK, 0)
        row, col = _band_masks()
        for h in range(HEADS_PER_GROUP):
            hs = slice(h * HEAD_DIM, (h + 1) * HEAD_DIM)
            dq = [None] * nbk
            dk = [None] * nbk
            dv = [None] * nbk
            for b in range(nbk + 1):
                rs = slice(b * BLK, (b + 1) * BLK)
                if b < nbk:
                    qb, dab, atb, ls = q_ref[rs, hs], da_ref[rs, hs], at_ref[rs, hs], ls_ref[rs, hs]
                else:
                    qb, dab, atb, ls = qn_ref[:, hs], dan_ref[:, hs], atn_ref[:, hs], lsn_ref[:, hs]
                delta = jnp.sum(dab * atb, axis=1, keepdims=True)
                dab = dab.astype(BF)
                if b < nbk:
                    kc, vc = k_ref[rs, hs], v_ref[rs, hs]
                    p = jnp.where(col <= row, jnp.exp(_dot_nt(qb, kc) * ATTN_SCALE - ls), 0.0)
                    ds = (p * (_dot_nt(dab, vc) - delta) * ATTN_SCALE).astype(BF)
                    dv[b] = _dot_tn(p.astype(BF), dab)
                    dk[b] = _dot_tn(ds, qb)
                    dq[b] = _dot(ds, kc)
                if b == 0:
                    kp, vp, shift = kh_ref[:, hs], vh_ref[:, hs], first_shift
                else:
                    ps = slice((b - 1) * BLK, b * BLK)
                    kp, vp, shift = k_ref[ps, hs], v_ref[ps, hs], (next_shift if b == nbk else 0)
                p = jnp.where(col >= row + shift, jnp.exp(_dot_nt(qb, kp) * ATTN_SCALE - ls), 0.0)
                ds = (p * (_dot_nt(dab, vp) - delta) * ATTN_SCALE).astype(BF)
                if b < nbk:
                    dq[b] = dq[b] + _dot(ds, kp)
                if b >= 1:
                    dv[b - 1] = dv[b - 1] + _dot_tn(p.astype(BF), dab)
                    dk[b - 1] = dk[b - 1] + _dot_tn(ds, qb)
            for b in range(nbk):
                rs = slice(b * BLK, (b + 1) * BLK)
                dq_ref[rs, hs] = dq[b]
                dk_ref[rs, hs] = dk[b]
                dv_ref[rs, hs] = dv[b]

    def cur(cb):
        return pl.BlockSpec((rows, GROUP_W), lambda i: (i, cb))

    def prev(cb):
        return pl.BlockSpec((BLK, GROUP_W), lambda i: (jnp.maximum(i * nbk - 1, 0), cb))

    def nxt(cb):
        return pl.BlockSpec((BLK, GROUP_W), lambda i: (jnp.minimum((i + 1) * nbk, nblk - 1), cb))

    (qa, qc), (ka, kc_), (va, vc_) = q, k, v
    return pl.pallas_call(
        kern, name=name, grid=(t // rows,),
        in_specs=[cur(qc), nxt(qc), cur(kc_), prev(kc_), cur(vc_), prev(vc_),
                  cur(0), nxt(0), cur(0), nxt(0), cur(0), nxt(0)],
        out_specs=[pl.BlockSpec((rows, GROUP_W), lambda i: (i, 0))] * 3,
        out_shape=[jax.ShapeDtypeStruct((t, GROUP_W), F32)] * 3,
        compiler_params=_params(("parallel",)),
    )(qa, qa, ka, ka, va, va, da, da, at, at, lse, lse)


def _to_classes(a, dil):
    if dil == 1:
        return a
    t, w = a.shape
    return a.reshape(t // dil, dil, w).transpose(1, 0, 2).reshape(t, w)


def _from_classes(a, dil):
    if dil == 1:
        return a
    t, w = a.shape
    return a.reshape(dil, t // dil, w).transpose(1, 0, 2).reshape(t, w)


def _cross_probs(qh, kh):
    s = _dot_nt(qh, kh) * CROSS_SCALE
    e = jnp.exp(s - jnp.max(s, axis=1, keepdims=True))
    return e, jnp.sum(e, axis=1, keepdims=True)


def _cross_fwd(cq, ckv, t):
    rows = min(512, t)

    def kern(q_ref, kv_ref, o_ref):
        for h in range(CROSS_HEADS):
            hs = slice(h * CROSS_HEAD_DIM, (h + 1) * CROSS_HEAD_DIM)
            vs = slice(D_MODEL + h * CROSS_HEAD_DIM, D_MODEL + (h + 1) * CROSS_HEAD_DIM)
            e, tot = _cross_probs(q_ref[:, hs], kv_ref[:, hs])
            o_ref[:, hs] = (_dot(e.astype(BF), kv_ref[:, vs]) / tot).astype(BF)

    return pl.pallas_call(
        kern, name="cross_fwd", grid=(t // rows,),
        in_specs=[pl.BlockSpec((rows, D_MODEL), lambda i: (i, 0)), pl.BlockSpec((N_MEM, 2 * D_MODEL), lambda i: (0, 0))],
        out_specs=pl.BlockSpec((rows, D_MODEL), lambda i: (i, 0)),
        out_shape=jax.ShapeDtypeStruct((t, D_MODEL), BF),
        compiler_params=_params(("parallel",)),
    )(cq, ckv)


def _cross_bwd(cq, ckv, dco, t):
    rows = min(512, t)

    def kern(q_ref, kv_ref, do_ref, dq_ref, dkv_ref):
        @pl.when(pl.program_id(0) == 0)
        def _():
            dkv_ref[...] = jnp.zeros_like(dkv_ref)
        for h in range(CROSS_HEADS):
            hs = slice(h * CROSS_HEAD_DIM, (h + 1) * CROSS_HEAD_DIM)
            vs = slice(D_MODEL + h * CROSS_HEAD_DIM, D_MODEL + (h + 1) * CROSS_HEAD_DIM)
            qh, kh, vh, doh = q_ref[:, hs], kv_ref[:, hs], kv_ref[:, vs], do_ref[:, hs]
            e, tot = _cross_probs(qh, kh)
            p = e / tot
            dp = _dot_nt(doh, vh)
            ds = (p * (dp - jnp.sum(p * dp, axis=1, keepdims=True)) * CROSS_SCALE).astype(BF)
            dq_ref[:, hs] = _dot(ds, kh).astype(BF)
            dkv_ref[:, hs] += _dot_tn(ds, qh)
            dkv_ref[:, vs] += _dot_tn(p.astype(BF), doh)

    return pl.pallas_call(
        kern, name="cross_bwd", grid=(t // rows,),
        in_specs=[pl.BlockSpec((rows, D_MODEL), lambda i: (i, 0)), pl.BlockSpec((N_MEM, 2 * D_MODEL), lambda i: (0, 0)),
                  pl.BlockSpec((rows, D_MODEL), lambda i: (i, 0))],
        out_specs=[pl.BlockSpec((rows, D_MODEL), lambda i: (i, 0)), pl.BlockSpec((N_MEM, 2 * D_MODEL), lambda i: (0, 0))],
        out_shape=[jax.ShapeDtypeStruct((t, D_MODEL), BF), jax.ShapeDtypeStruct((N_MEM, 2 * D_MODEL), F32)],
        compiler_params=_params(("arbitrary",)),
    )(cq, ckv, dco)


CONV_TILE = 512
CONV_CHUNK = 128
HALO = 32


def _conv_fwd(z, w32, bias, t):
    tile = min(CONV_TILE, t)
    a_cb, b_cb = GLU_A_COL // LANES, GLU_B_COL // LANES
    hb = tile // HALO

    def kern(a_ref, b_ref, ah_ref, bh_ref, w_ref, bias_ref, o_ref, g_scr):
        i = pl.program_id(1)
        g_scr[HALO:, :] = a_ref[...] * _sig(b_ref[...])
        g_scr[:HALO, :] = ah_ref[...] * _sig(bh_ref[...]) * jnp.where(i > 0, 1.0, 0.0)
        for c in range(tile // CONV_CHUNK):
            acc = jnp.broadcast_to(bias_ref[...], (CONV_CHUNK, LANES))
            for j in range(CONV_K):
                lo = c * CONV_CHUNK + HALO - (CONV_K - 1) + j
                acc = acc + w_ref[j:j + 1, :] * g_scr[lo:lo + CONV_CHUNK, :]
            o_ref[c * CONV_CHUNK:(c + 1) * CONV_CHUNK, :] = acc

    def cur(cb):
        return pl.BlockSpec((tile, LANES), lambda j, i: (i, cb + j))

    def prev(cb):
        return pl.BlockSpec((HALO, LANES), lambda j, i: (jnp.maximum(i * hb - 1, 0), cb + j))

    return pl.pallas_call(
        kern, name="conv_fwd", grid=(CONV_CH // LANES, t // tile),
        in_specs=[cur(a_cb), cur(b_cb), prev(a_cb), prev(b_cb),
                  pl.BlockSpec((CONV_KP, LANES), lambda j, i: (0, j)), pl.BlockSpec((1, LANES), lambda j, i: (0, j))],
        out_specs=pl.BlockSpec((tile, LANES), lambda j, i: (i, j)),
        out_shape=jax.ShapeDtypeStruct((t, CONV_CH), F32),
        scratch_shapes=[pltpu.VMEM((tile + HALO, LANES), F32)],
        compiler_params=_params(("parallel", "parallel")),
    )(z, z, z, z, w32, bias)


def _conv_bwd(z, dc1, w32, t):
    tile = min(CONV_TILE, t)
    a_cb, b_cb = GLU_A_COL // LANES, GLU_B_COL // LANES
    hb = tile // HALO
    n_tiles = t // tile
    n_chunks = tile // CONV_CHUNK

    def kern(a_ref, b_ref, ah_ref, bh_ref, d_ref, dn_ref, w_ref, da_ref, db_ref, dw_ref, g_scr, d_scr):
        i = pl.program_id(1)
        sg = _sig(b_ref[...])
        g_scr[HALO:, :] = a_ref[...] * sg
        g_scr[:HALO, :] = ah_ref[...] * _sig(bh_ref[...]) * jnp.where(i > 0, 1.0, 0.0)
        d_scr[:tile, :] = d_ref[...]
        d_scr[tile:, :] = dn_ref[...] * jnp.where(i < n_tiles - 1, 1.0, 0.0)

        @pl.when(i == 0)
        def _():
            dw_ref[...] = jnp.zeros_like(dw_ref)

        for c in range(n_chunks):
            cs = slice(c * CONV_CHUNK, (c + 1) * CONV_CHUNK)
            acc = jnp.zeros((CONV_CHUNK, LANES), F32)
            for j in range(CONV_K):
                lo = c * CONV_CHUNK + (CONV_K - 1) - j
                acc = acc + w_ref[j:j + 1, :] * d_scr[lo:lo + CONV_CHUNK, :]
            sgc = _sig(b_ref[cs, :])
            da_ref[cs, :] = (acc * sgc).astype(BF)
            db_ref[cs, :] = (acc * a_ref[cs, :] * sgc * (1.0 - sgc)).astype(BF)
        for j in range(CONV_K):
            tot = jnp.zeros((1, LANES), F32)
            for c in range(n_chunks):
                lo = c * CONV_CHUNK + HALO - (CONV_K - 1) + j
                tot = tot + jnp.sum(d_ref[c * CONV_CHUNK:(c + 1) * CONV_CHUNK, :] * g_scr[lo:lo + CONV_CHUNK, :],
                                    axis=0, keepdims=True)
            dw_ref[j:j + 1, :] += tot
        dw_ref[CONV_K:CONV_KP, :] += jnp.sum(d_ref[...], axis=0, keepdims=True)

    def cur(cb):
        return pl.BlockSpec((tile, LANES), lambda j, i: (i, cb + j))

    def prev(cb):
        return pl.BlockSpec((HALO, LANES), lambda j, i: (jnp.maximum(i * hb - 1, 0), cb + j))

    return pl.pallas_call(
        kern, name="conv_bwd", grid=(CONV_CH // LANES, n_tiles),
        in_specs=[cur(a_cb), cur(b_cb), prev(a_cb), prev(b_cb), cur(0),
                  pl.BlockSpec((HALO, LANES), lambda j, i: (jnp.minimum((i + 1) * hb, t // HALO - 1), j)),
                  pl.BlockSpec((CONV_KP, LANES), lambda j, i: (0, j))],
        out_specs=[pl.BlockSpec((tile, LANES), lambda j, i: (i, j)), pl.BlockSpec((tile, LANES), lambda j, i: (i, j)),
                   pl.BlockSpec((CONV_KP, LANES), lambda j, i: (0, j))],
        out_shape=[jax.ShapeDtypeStruct((t, CONV_CH), BF), jax.ShapeDtypeStruct((t, CONV_CH), BF),
                   jax.ShapeDtypeStruct((CONV_KP, CONV_CH), F32)],
        scratch_shapes=[pltpu.VMEM((tile + HALO, LANES), F32), pltpu.VMEM((tile + HALO, LANES), F32)],
        compiler_params=_params(("parallel", "arbitrary")),
    )(z, z, z, z, dc1, dc1, w32)


def _adam_fn(w, g, m, v):
    m = ADAM_B1 * m + (1.0 - ADAM_B1) * g
    v = ADAM_B2 * v + (1.0 - ADAM_B2) * (g * g)
    m_hat = m / (1.0 - ADAM_B1 ** ADAM_STEP)
    v_hat = v / (1.0 - ADAM_B2 ** ADAM_STEP)
    delta = -ADAM_LR * (m_hat / (jnp.sqrt(v_hat) + ADAM_EPS) + ADAM_WD * w)
    return delta, m, v


def _adam(name, w, g, m, v):
    rows, cols = w.shape
    tile = _ew_tile(rows, cols)
    return _rowcall(name, _adam_fn, rows, tile, [(a, cols, 0) for a in (w, g, m, v)], [], [(cols, F32)] * 3)


def _place():
    x, y, c = lax.axis_index("x"), lax.axis_index("y"), lax.axis_index("c")
    chips = [(1 - x, y), (x, 1 - y), (1 - x, 1 - y)]
    return x, y, c, chips


def _gather_weights(shards):
    nw = len(shards)

    def body(*refs):
        ins, outs = refs[:nw], refs[nw:2 * nw]
        send_sems, recv_sems, local_sems = refs[2 * nw:]
        x, y, c, chips = _place()
        me = 2 * x + y
        sibling = (x, y, 1 - c)

        def half(ref, lead, w):
            n = shards[w].shape[0] // 2
            return ref.at[lead, pl.ds(c * n, n)] if lead is not None else ref.at[pl.ds(c * n, n)]

        def copy(w, k, src, dst, to):
            return pltpu.make_async_remote_copy(src_ref=src, dst_ref=dst, send_sem=send_sems.at[w, k],
                                                recv_sem=recv_sems.at[w, k], device_id=to, device_id_type=MESH)

        local, first, passed = [], [], []
        for w in range(nw):
            local.append(pltpu.make_async_copy(ins[w], outs[w].at[me], local_sems.at[w]))
            local[-1].start()
            for k, (px, py) in enumerate(chips):
                first.append(copy(w, k, half(ins[w], None, w), half(outs[w], me, w), (px, py, c)))
                first[-1].start()
        for w in range(nw):
            for k, (px, py) in enumerate(chips):
                landed = half(outs[w], 2 * px + py, w)
                copy(w, k, landed, landed, (px, py, c)).wait_recv()
                passed.append(copy(w, 3 + k, landed, landed, sibling))
                passed[-1].start()
        for w in range(nw):
            n = shards[w].shape[0] // 2
            for k, (px, py) in enumerate(chips):
                theirs = outs[w].at[2 * px + py, pl.ds((1 - c) * n, n)]
                copy(w, 3 + k, theirs, theirs, sibling).wait_recv()
        for cp in first + passed:
            cp.wait_send()
        for cp in local:
            cp.wait()

    return pl.pallas_call(
        body, name="gather_weights",
        in_specs=[ANY] * nw, out_specs=[ANY] * nw,
        out_shape=[jax.ShapeDtypeStruct((4,) + s.shape, s.dtype) for s in shards],
        scratch_shapes=[pltpu.SemaphoreType.DMA((nw, 6)), pltpu.SemaphoreType.DMA((nw, 6)), pltpu.SemaphoreType.DMA((nw,))],
    )(*shards)


def _gather8(name, v):
    rows = v.shape[0]

    def body(v_ref, all_ref, sum_ref, send_sems, recv_sems):
        x, y, c, _ = _place()
        me = 4 * x + 2 * y + c
        all_ref[me] = v_ref[...]
        copies = []
        for k in range(1, 8):
            px, py, pc = x ^ (k >> 2), y ^ ((k >> 1) & 1), c ^ (k & 1)
            copies.append(pltpu.make_async_remote_copy(
                src_ref=v_ref, dst_ref=all_ref.at[me], send_sem=send_sems.at[k - 1], recv_sem=recv_sems.at[k - 1],
                device_id=(px, py, pc), device_id_type=MESH))
            copies[-1].start()
        for k in range(1, 8):
            px, py, pc = x ^ (k >> 2), y ^ ((k >> 1) & 1), c ^ (k & 1)
            theirs = all_ref.at[4 * px + 2 * py + pc]
            pltpu.make_async_remote_copy(
                src_ref=theirs, dst_ref=theirs, send_sem=send_sems.at[k - 1], recv_sem=recv_sems.at[k - 1],
                device_id=(px, py, pc), device_id_type=MESH).wait_recv()
        for cp in copies:
            cp.wait_send()
        tot = all_ref[0]
        for d in range(1, 8):
            tot = tot + all_ref[d]
        sum_ref[...] = tot

    vm = pl.BlockSpec(memory_space=pltpu.VMEM)
    return pl.pallas_call(
        body, name=name, in_specs=[vm], out_specs=[vm, vm],
        out_shape=[jax.ShapeDtypeStruct((8, rows, LANES), F32), jax.ShapeDtypeStruct((rows, LANES), F32)],
        scratch_shapes=[pltpu.SemaphoreType.DMA((7,)), pltpu.SemaphoreType.DMA((7,))],
    )(v)


def _region(ref, col_sharded, shape, j, h):
    r, ccols = shape
    if col_sharded:
        return ref.at[pl.ds(h * (r // 2), r // 2), pl.ds(j * (ccols // 4), ccols // 4)]
    n = r // 8
    return ref.at[pl.ds((2 * j + h) * n, n), :]


def _region_shape(col_sharded, shape):
    r, ccols = shape
    return (r // 2, ccols // 4) if col_sharded else (r // 8, ccols)


def _swap_halves(grads, kinds):
    nw = len(grads)

    def body(*refs):
        ins, mine, theirs = refs[:nw], refs[nw:2 * nw], refs[2 * nw:3 * nw]
        send_sems, recv_sems, local_sems = refs[3 * nw:]
        x, y, c, _ = _place()
        local, sent = [], []
        for w in range(nw):
            for j in range(4):
                local.append(pltpu.make_async_copy(_region(ins[w], kinds[w], grads[w].shape, j, c), mine[w].at[j],
                                                   local_sems.at[w, j]))
                local[-1].start()
                sent.append(pltpu.make_async_remote_copy(
                    src_ref=_region(ins[w], kinds[w], grads[w].shape, j, 1 - c), dst_ref=theirs[w].at[j],
                    send_sem=send_sems.at[w, j], recv_sem=recv_sems.at[w, j], device_id=(x, y, 1 - c), device_id_type=MESH))
                sent[-1].start()
        for cp in sent:
            cp.wait()
        for cp in local:
            cp.wait()

    shapes = [jax.ShapeDtypeStruct((4,) + _region_shape(kinds[w], grads[w].shape), F32) for w in range(nw)]
    res = pl.pallas_call(
        body, name="grad_swap_halves", in_specs=[ANY] * nw, out_specs=[ANY] * (2 * nw), out_shape=shapes + shapes,
        scratch_shapes=[pltpu.SemaphoreType.DMA((nw, 4)), pltpu.SemaphoreType.DMA((nw, 4)), pltpu.SemaphoreType.DMA((nw, 4))],
    )(*grads)
    return res[:nw], res[nw:]


def _scatter_to_owners(parts):
    nw = len(parts)

    def body(*refs):
        ins, outs = refs[:nw], refs[nw:2 * nw]
        send_sems, recv_sems, local_sems = refs[2 * nw:]
        x, y, c, chips = _place()
        me = 2 * x + y
        local, sent = [], []
        for w in range(nw):
            local.append(pltpu.make_async_copy(ins[w].at[me], outs[w].at[3], local_sems.at[w]))
            local[-1].start()
            for k, (px, py) in enumerate(chips):
                sent.append(pltpu.make_async_remote_copy(
                    src_ref=ins[w].at[2 * px + py], dst_ref=outs[w].at[k], send_sem=send_sems.at[w, k],
                    recv_sem=recv_sems.at[w, k], device_id=(px, py, c), device_id_type=MESH))
                sent[-1].start()
        for cp in sent:
            cp.wait()
        for cp in local:
            cp.wait()

    return pl.pallas_call(
        body, name="grad_scatter", in_specs=[ANY] * nw, out_specs=[ANY] * nw,
        out_shape=[jax.ShapeDtypeStruct(p.shape, F32) for p in parts],
        scratch_shapes=[pltpu.SemaphoreType.DMA((nw, 3)), pltpu.SemaphoreType.DMA((nw, 3)), pltpu.SemaphoreType.DMA((nw,))],
    )(*parts)


def _share_halves(halves):
    nw = len(halves)

    def body(*refs):
        ins, outs = refs[:nw], refs[nw:2 * nw]
        send_sems, recv_sems, local_sems = refs[2 * nw:]
        x, y, c, _ = _place()
        local, sent = [], []
        for w in range(nw):
            local.append(pltpu.make_async_copy(ins[w], outs[w].at[c], local_sems.at[w]))
            local[-1].start()
            sent.append(pltpu.make_async_remote_copy(
                src_ref=ins[w], dst_ref=outs[w].at[c], send_sem=send_sems.at[w], recv_sem=recv_sems.at[w],
                device_id=(x, y, 1 - c), device_id_type=MESH))
            sent[-1].start()
        for cp in sent:
            cp.wait()
        for cp in local:
            cp.wait()

    return pl.pallas_call(
        body, name="grad_share_halves", in_specs=[ANY] * nw, out_specs=[ANY] * nw,
        out_shape=[jax.ShapeDtypeStruct((2,) + h.shape, F32) for h in halves],
        scratch_shapes=[pltpu.SemaphoreType.DMA((nw,)), pltpu.SemaphoreType.DMA((nw,)), pltpu.SemaphoreType.DMA((nw,))],
    )(*halves)


def _ew_tile(rows, cols):
    limit = max(8, (256 * 1024) // cols)
    return max(d for d in range(8, min(rows, limit) + 1, 8) if rows % d == 0)


def _reduce_grads(grads, kinds):
    nw = len(grads)
    mine, theirs = _swap_halves(grads, kinds)
    parts = []
    for w in range(nw):
        _, rr, cc = mine[w].shape
        a, b = mine[w].reshape(4 * rr, cc), theirs[w].reshape(4 * rr, cc)
        (p,) = _rowcall(f"grad_pair_sum_{w}", lambda u, v: (u + v,), 4 * rr, _ew_tile(4 * rr, cc),
                        [(a, cc, 0), (b, cc, 0)], [], [(cc, F32)])
        parts.append(p.reshape(4, rr, cc))
    landed = _scatter_to_owners(parts)
    halves = []
    for w in range(nw):
        _, rr, cc = landed[w].shape
        tile = _ew_tile(rr, cc)
        flat = landed[w].reshape(4 * rr, cc)
        nb = rr // tile
        (f,) = _rowcall(f"grad_chip_sum_{w}", lambda own, k0, k1, k2: (((own + k0) + k1) + k2,), rr, tile,
                        [(flat, cc, 0, 3 * nb), (flat, cc, 0, 0), (flat, cc, 0, nb), (flat, cc, 0, 2 * nb)], [], [(cc, F32)])
        halves.append(f)
    both = _share_halves(halves)
    return [b.reshape(2 * b.shape[1], b.shape[2]) for b in both]


BIG = ("w_in", "w_attn_proj", "w_conv_proj", "w_out", "w_cq", "w_ckv", "w_co", "w_up", "w_down")
COL_SHARDED = {"w_in": True, "w_attn_proj": True, "w_conv_proj": True, "w_out": False, "w_cq": False,
               "w_ckv": True, "w_co": False, "w_up": True, "w_down": False}
SMALL = ("g_mix", "b_gate", "conv_b", "conv_ln_g", "conv_ln_b", "g_cross", "g_mem", "g_mlp", "g_final")
ORDER = ("g_mix", "w_in", "b_gate", "conv_w", "conv_b", "conv_ln_g", "conv_ln_b", "w_attn_proj", "w_conv_proj", "w_out",
         "g_cross", "g_mem", "w_cq", "w_ckv", "w_co", "g_mlp", "w_up", "w_down", "g_final")


def _pad_rows(flat, rows):
    return jnp.pad(flat, (0, rows * LANES - flat.shape[0])).reshape(rows, LANES)


def _local_step(x, mem, tgt, wfull, small, conv_w_full):
    t = x.shape[0]
    tr = 256
    c_tab, s_tab = _rope_tables(t)
    row = lambda v: v.reshape(1, -1)
    g_mix, g_cross, g_mem, g_mlp, g_final = (row(small[n]) for n in ("g_mix", "g_cross", "g_mem", "g_mlp", "g_final"))
    b_gate, conv_b, ln_g, ln_b = (row(small[n]) for n in ("b_gate", "conv_b", "conv_ln_g", "conv_ln_b"))
    w32 = jnp.pad(conv_w_full, ((0, CONV_KP - CONV_K), (0, 0)))

    (u,) = _rowcall("mix_norm", lambda a, g: (_rms(a, g),), t, tr, [(x, D_MODEL, 0)], [g_mix], [(D_MODEL, BF)])
    z = _mm("in_proj", u, wfull["w_in"])
    qr, kr, vb = _rowcall("rope", _rope_fwd_fn, t, tr,
                          [(z, ATTN_WIDTH, 0), (z, ATTN_WIDTH, 1), (z, ATTN_WIDTH, 2), (c_tab, HEAD_DIM, 0), (s_tab, HEAD_DIM, 0)],
                          [], [(ATTN_WIDTH, BF)] * 3)
    qkv_cls, outs, lses = [], [], []
    for g, dil in enumerate(DILATIONS):
        if dil == 1:
            ops = ((qr, g), (kr, g), (vb, g))
        else:
            gs = slice(g * GROUP_W, (g + 1) * GROUP_W)
            ops = tuple((_to_classes(a[:, gs], dil), 0) for a in (qr, kr, vb))
        qkv_cls.append(ops)
        o_g, l_g = _attn_fwd(f"attn_fwd_{g}", *ops, t, dil)
        outs.append(_from_classes(o_g, dil))
        lses.append(_from_classes(l_g, dil))
    attn, lse = _rowcall("attn_merge", _merge_fn, t, tr, [(a, GROUP_W, 0) for a in outs + lses], [],
                         [(GROUP_W, F32), (GROUP_W, F32)])
    y_attn = _mm("attn_proj", attn, wfull["w_attn_proj"])
    c1 = _conv_fwd(z, w32, conv_b, t)
    (c2,) = _rowcall("conv_ln_silu", _ln_silu_fn, t, tr, [(c1, CONV_CH, 0)], [ln_g, ln_b], [(CONV_CH, BF)])
    y_conv = _mm("conv_proj", c2, wfull["w_conv_proj"])
    gate_ins = [(z, D_MODEL, GATE_A_COL // D_MODEL), (z, D_MODEL, GATE_B_COL // D_MODEL)]
    (merged,) = _rowcall("gate", _gate_fn, t, tr, [(y_attn, D_MODEL, 0), (y_conv, D_MODEL, 0)] + gate_ins, [b_gate],
                         [(D_MODEL, BF)])
    add = lambda acc, res: (res + acc,)
    x1 = _mm("out_proj", merged, wfull["w_out"], extras=(x,), epi=add)

    (uq,) = _rowcall("cross_norm", lambda a, g: (_rms(a, g),), t, tr, [(x1, D_MODEL, 0)], [g_cross], [(D_MODEL, BF)])
    (mn,) = _rowcall("mem_norm", lambda a, g: (_rms(a, g),), N_MEM, N_MEM, [(mem, D_MODEL, 0)], [g_mem], [(D_MODEL, BF)])
    cq = _mm("cross_q", uq, wfull["w_cq"], out_dtypes=(BF,))
    ckv = _mm("cross_kv", mn, wfull["w_ckv"], out_dtypes=(BF,))
    co = _cross_fwd(cq, ckv, t)
    x2 = _mm("cross_out", co, wfull["w_co"], extras=(x1,), epi=add)

    (um,) = _rowcall("mlp_norm", lambda a, g: (_rms(a, g),), t, tr, [(x2, D_MODEL, 0)], [g_mlp], [(D_MODEL, BF)])
    hpre, hact = _mm("mlp_up", um, wfull["w_up"], out_dtypes=(F32, BF),
                     epi=lambda acc: (acc, jnp.square(jnp.maximum(acc, 0.0))))
    x3 = _mm("mlp_down", hact, wfull["w_down"], extras=(x2,), epi=add)
    d3, loss_row, dg_final = _rowcall("final_norm_loss", _final_fn, t, tr, [(x3, D_MODEL, 0), (tgt, D_MODEL, 0)], [g_final],
                                      [(D_MODEL, F32)], accs=[(1, LANES), (1, D_MODEL)])

    gw = {}
    dhp = _mm("mlp_down_bwd", d3, wfull["w_down"], nt=True, extras=(hpre,), out_dtypes=(BF,),
              epi=lambda acc, hp: (acc * 2.0 * jnp.maximum(hp, 0.0),))
    gw["w_down"] = _mm_tn("mlp_down_wgrad", hact, d3)
    dum = _mm("mlp_up_bwd", dhp, wfull["w_up"], nt=True)
    gw["w_up"] = _mm_tn("mlp_up_wgrad", um, dhp)

    def norm_bwd(a, dn, dres, g):
        dx, dg = _rms_bwd(a, g, dn)
        return dres + dx, dg

    d2, dg_mlp = _rowcall("mlp_norm_bwd", norm_bwd, t, tr, [(x2, D_MODEL, 0), (dum, D_MODEL, 0), (d3, D_MODEL, 0)], [g_mlp],
                          [(D_MODEL, F32)], accs=[(1, D_MODEL)])

    dco = _mm("cross_out_bwd", d2, wfull["w_co"], nt=True, out_dtypes=(BF,))
    gw["w_co"] = _mm_tn("cross_out_wgrad", co, d2)
    dcq, dckv = _cross_bwd(cq, ckv, dco, t)
    gw["w_cq"] = _mm_tn("cross_q_wgrad", uq, dcq)
    duq = _mm("cross_q_bwd", dcq, wfull["w_cq"], nt=True)
    d1, dg_cross = _rowcall("cross_norm_bwd", norm_bwd, t, tr, [(x1, D_MODEL, 0), (duq, D_MODEL, 0), (d2, D_MODEL, 0)],
                            [g_cross], [(D_MODEL, F32)], accs=[(1, D_MODEL)])
    gw["w_ckv"] = _mm_tn("cross_kv_wgrad", mn, dckv, tk=N_MEM)
    dmn = _mm("cross_kv_bwd", dckv, wfull["w_ckv"], nt=True)
    (dg_mem,) = _rowcall("mem_norm_bwd", lambda a, dn, g: (_rms_bwd(a, g, dn)[1],), N_MEM, N_MEM,
                         [(mem, D_MODEL, 0), (dmn, D_MODEL, 0)], [g_mem], [], accs=[(1, D_MODEL)])

    dmerged = _mm("out_proj_bwd", d1, wfull["w_out"], nt=True)
    gw["w_out"] = _mm_tn("out_proj_wgrad", merged, d1)
    dya, dyc, dgate, dbg = _rowcall("gate_bwd", _gate_bwd_fn, t, tr,
                                    [(dmerged, D_MODEL, 0), (y_attn, D_MODEL, 0), (y_conv, D_MODEL, 0)] + gate_ins, [b_gate],
                                    [(D_MODEL, BF), (D_MODEL, BF), (2 * D_MODEL, BF)], accs=[(1, 2 * D_MODEL)])
    gw["w_attn_proj"] = _mm_tn("attn_proj_wgrad", attn, dya)
    dattn = _mm("attn_proj_bwd", dya, wfull["w_attn_proj"], nt=True)
    gw["w_conv_proj"] = _mm_tn("conv_proj_wgrad", c2, dyc)
    dc2 = _mm("conv_proj_bwd", dyc, wfull["w_conv_proj"], nt=True)
    dc1, dlng, dlnb = _rowcall("conv_ln_silu_bwd", _ln_silu_bwd_fn, t, tr, [(c1, CONV_CH, 0), (dc2, CONV_CH, 0)], [ln_g, ln_b],
                               [(CONV_CH, F32)], accs=[(1, CONV_CH), (1, CONV_CH)])
    dglu_a, dglu_b, dconv = _conv_bwd(z, dc1, w32, t)

    dqs, dks, dvs = [], [], []
    for g, dil in enumerate(DILATIONS):
        da_c, at_c, ls_c = (_to_classes(a, dil) for a in (dattn, attn, lse))
        dq_g, dk_g, dv_g = _attn_bwd(f"attn_bwd_{g}", *qkv_cls[g], da_c, at_c, ls_c, t, dil)
        dqs.append(_from_classes(dq_g, dil))
        dks.append(_from_classes(dk_g, dil))
        dvs.append(_from_classes(dv_g, dil))
    (dqkv,) = _rowcall("rope_bwd", _rope_bwd_fn, t, tr,
                       [(a, GROUP_W, 0) for a in dqs + dks + dvs] + [(c_tab, HEAD_DIM, 0), (s_tab, HEAD_DIM, 0)], [],
                       [(3 * ATTN_WIDTH, BF)])
    dz = jnp.concatenate([dqkv, dglu_a, dglu_b, dgate], axis=1)
    du = _mm("in_proj_bwd", dz, wfull["w_in"], nt=True)
    gw["w_in"] = _mm_tn("in_proj_wgrad", u, dz)
    gx, dg_mix = _rowcall("mix_norm_bwd", norm_bwd, t, tr, [(x, D_MODEL, 0), (du, D_MODEL, 0), (d1, D_MODEL, 0)], [g_mix],
                          [(D_MODEL, F32)], accs=[(1, D_MODEL)])

    gsmall = {"g_mix": dg_mix, "b_gate": dbg, "conv_b": dconv[CONV_K:CONV_K + 1], "conv_ln_g": dlng, "conv_ln_b": dlnb,
              "g_cross": dg_cross, "g_mem": dg_mem, "g_mlp": dg_mlp, "g_final": dg_final, "conv_w": dconv[:CONV_K]}
    return loss_row, gx, gw, gsmall


def kernel(x, mem, g_mix, w_in, b_gate, conv_w, conv_b, conv_ln_g, conv_ln_b, w_attn_proj, w_conv_proj, w_out, g_cross, g_mem, w_cq, w_ckv, w_co, g_mlp, w_up, w_down, g_final, loss_target, m_g_mix, m_w_in, m_b_gate, m_conv_w, m_conv_b, m_conv_ln_g, m_conv_ln_b, m_w_attn_proj, m_w_conv_proj, m_w_out, m_g_cross, m_g_mem, m_w_cq, m_w_ckv, m_w_co, m_g_mlp, m_w_up, m_w_down, m_g_final, v_g_mix, v_w_in, v_b_gate, v_conv_w, v_conv_b, v_conv_ln_g, v_conv_ln_b, v_w_attn_proj, v_w_conv_proj, v_w_out, v_g_cross, v_g_mem, v_w_cq, v_w_ckv, v_w_co, v_g_mlp, v_w_up, v_w_down, v_g_final):
    w = dict(g_mix=g_mix, w_in=w_in, b_gate=b_gate, conv_w=conv_w, conv_b=conv_b, conv_ln_g=conv_ln_g, conv_ln_b=conv_ln_b,
             w_attn_proj=w_attn_proj, w_conv_proj=w_conv_proj, w_out=w_out, g_cross=g_cross, g_mem=g_mem, w_cq=w_cq,
             w_ckv=w_ckv, w_co=w_co, g_mlp=g_mlp, w_up=w_up, w_down=w_down, g_final=g_final)
    mo = dict(g_mix=m_g_mix, w_in=m_w_in, b_gate=m_b_gate, conv_w=m_conv_w, conv_b=m_conv_b, conv_ln_g=m_conv_ln_g,
              conv_ln_b=m_conv_ln_b, w_attn_proj=m_w_attn_proj, w_conv_proj=m_w_conv_proj, w_out=m_w_out, g_cross=m_g_cross,
              g_mem=m_g_mem, w_cq=m_w_cq, w_ckv=m_w_ckv, w_co=m_w_co, g_mlp=m_g_mlp, w_up=m_w_up, w_down=m_w_down,
              g_final=m_g_final)
    vo = dict(g_mix=v_g_mix, w_in=v_w_in, b_gate=v_b_gate, conv_w=v_conv_w, conv_b=v_conv_b, conv_ln_g=v_conv_ln_g,
              conv_ln_b=v_conv_ln_b, w_attn_proj=v_w_attn_proj, w_conv_proj=v_w_conv_proj, w_out=v_w_out, g_cross=v_g_cross,
              g_mem=v_g_mem, w_cq=v_w_cq, w_ckv=v_w_ckv, w_co=v_w_co, g_mlp=v_g_mlp, w_up=v_w_up, w_down=v_w_down,
              g_final=v_g_final)
    shapes = {n: w[n].shape for n in ORDER}
    two_d = lambda a: a.reshape(a.shape[-2], a.shape[-1])
    chip = 2 * lax.axis_index("x") + lax.axis_index("y")

    shards = [two_d(w[n]).astype(BF) for n in BIG]
    gathered = _gather_weights(shards)
    wfull = {}
    for n, g in zip(BIG, gathered):
        wfull[n] = g if COL_SHARDED[n] else g.reshape(1, 4 * g.shape[1], g.shape[2])
    cw_rows = 48
    cw_all, _ = _gather8("gather_conv_w", _pad_rows(conv_w.reshape(-1), cw_rows))
    cw_shard = CONV_K * (CONV_CH // 4)
    conv_w_full = jnp.concatenate(
        [cw_all[2 * j].reshape(-1)[:cw_shard].reshape(CONV_K, CONV_CH // 4) for j in range(4)], axis=1)

    small = {n: w[n] for n in SMALL}
    loss_row, gx, gw, gsmall = _local_step(two_d(x), two_d(mem), two_d(loss_target), wfull, small, conv_w_full)
    loss = lax.psum(loss_row[0, 0], ("x", "y", "c"))

    gshard = dict(zip(BIG, _reduce_grads([gw[n] for n in BIG], [COL_SHARDED[n] for n in BIG])))
    small_names = SMALL + ("conv_w",)
    flat = jnp.concatenate([gsmall[n].reshape(-1) for n in small_names])
    sm_rows = -(-flat.shape[0] // (8 * LANES)) * 8
    _, sm_sum = _gather8("reduce_small_grads", _pad_rows(flat, sm_rows))
    sm_sum = sm_sum.reshape(-1)
    off = 0
    for n in small_names:
        size = gsmall[n].size
        gshard[n] = sm_sum[off:off + size].reshape(gsmall[n].shape)
        off += size
    gshard["conv_w"] = lax.dynamic_slice_in_dim(gshard["conv_w"], chip * (CONV_CH // 4), CONV_CH // 4, axis=1)

    grads, deltas, new_m, new_v = {}, {}, {}, {}
    for n in BIG:
        d, m2, v2 = _adam(f"adamw_{n}", two_d(w[n]), gshard[n], two_d(mo[n]), two_d(vo[n]))
        grads[n], deltas[n], new_m[n], new_v[n] = (a.reshape(shapes[n]) for a in (gshard[n], d, m2, v2))
    pack = lambda src: jnp.concatenate([src[n].reshape(-1) for n in small_names])
    n_small = sum(w[n].size for n in small_names)
    ad_rows = -(-n_small // (8 * LANES)) * 8
    d, m2, v2 = _adam("adamw_small", *[_pad_rows(pack(src), ad_rows) for src in (w, gshard, mo, vo)])
    off = 0
    for n in small_names:
        size = w[n].size
        grads[n] = gshard[n].reshape(shapes[n])
        deltas[n], new_m[n], new_v[n] = (a.reshape(-1)[off:off + size].reshape(shapes[n]) for a in (d, m2, v2))
        off += size

    return (loss, gx.reshape(x.shape), *[grads[n] for n in ORDER], *[deltas[n] for n in ORDER],
            *[new_m[n] for n in ORDER], *[new_v[n] for n in ORDER])
```

```python
import functools

import jax
import jax.numpy as jnp
from jax import lax
from jax.experimental import pallas as pl
from jax.experimental.pallas import tpu as pltpu

F32 = jnp.float32
BF = jnp.bfloat16

D_MODEL = 1024
N_MEM = 256
HEAD_DIM = 128
HEADS_PER_GROUP = 4
DILATIONS = (1, 4, 16)
BLK = 128
GROUP_W = HEADS_PER_GROUP * HEAD_DIM
ATTN_WIDTH = 3 * GROUP_W
ROT_DIM = 32
ROPE_THETA = 500000.0
CONV_CH = 768
CONV_K = 31
CONV_KP = 32
IN_WIDTH = 8192
CROSS_HEADS = 4
CROSS_HEAD_DIM = 256
D_FF = 4096
EPS = 1e-6
ATTN_SCALE = HEAD_DIM ** -0.5
CROSS_SCALE = CROSS_HEAD_DIM ** -0.5
NEG = -1e30

ADAM_LR = 0.001
ADAM_B1 = 0.9
ADAM_B2 = 0.999
ADAM_EPS = 1e-08
ADAM_WD = 0.01
ADAM_STEP = 10

LANES = 128
VMEM_LIMIT = 56 * 1024 * 1024
MESH = pl.DeviceIdType.MESH
ANY = pl.BlockSpec(memory_space=pl.ANY)

GLU_A_COL = 3 * ATTN_WIDTH
GLU_B_COL = GLU_A_COL + CONV_CH
GATE_A_COL = GLU_B_COL + CONV_CH
GATE_B_COL = GATE_A_COL + D_MODEL


def _params(sem=None):
    return pltpu.CompilerParams(dimension_semantics=sem, vmem_limit_bytes=VMEM_LIMIT)


def _dot(a, b):
    return lax.dot_general(a, b, (((1,), (0,)), ((), ())), preferred_element_type=F32)


def _dot_nt(a, b):
    return lax.dot_general(a, b, (((1,), (1,)), ((), ())), preferred_element_type=F32)


def _dot_tn(a, b):
    return lax.dot_general(a, b, (((0,), (0,)), ((), ())), preferred_element_type=F32)


def _sig(x):
    return 1.0 / (1.0 + jnp.exp(-x))


def _rowcall(name, fn, n_rows, tile, ins, params, outs, accs=()):
    tile = min(tile, n_rows)
    ni, npar, no, na = len(ins), len(params), len(outs), len(accs)

    def kern(*refs):
        in_refs = refs[:ni + npar]
        o_refs = refs[ni + npar:ni + npar + no]
        a_refs = refs[ni + npar + no:]
        vals = fn(*[r[...] for r in in_refs])
        for r, v in zip(o_refs, vals[:no]):
            r[...] = v.astype(r.dtype)
        if na:
            @pl.when(pl.program_id(0) == 0)
            def _():
                for r in a_refs:
                    r[...] = jnp.zeros_like(r)
            for r, v in zip(a_refs, vals[no:]):
                r[...] += v

    in_specs = []
    arrays = []
    for spec in ins:
        arr, width, cb = spec[0], spec[1], spec[2]
        rb = spec[3] if len(spec) > 3 else 0
        in_specs.append(pl.BlockSpec((tile, width), functools.partial(lambda i, cb, rb: (i + rb, cb), cb=cb, rb=rb)))
        arrays.append(arr)
    for p in params:
        in_specs.append(pl.BlockSpec(p.shape, lambda i: (0, 0)))
        arrays.append(p)
    out_specs = [pl.BlockSpec((tile, w), lambda i: (i, 0)) for w, _ in outs]
    out_specs += [pl.BlockSpec(s, lambda i: (0, 0)) for s in accs]
    out_shape = [jax.ShapeDtypeStruct((n_rows, w), dt) for w, dt in outs]
    out_shape += [jax.ShapeDtypeStruct(s, F32) for s in accs]
    res = pl.pallas_call(
        kern, name=name, grid=(n_rows // tile,), in_specs=in_specs, out_specs=out_specs, out_shape=out_shape,
        compiler_params=_params(("arbitrary",) if na else ("parallel",)),
    )(*arrays)
    return res


def _mm(name, a, w3, *, nt=False, extras=(), epi=None, out_dtypes=(F32,), tm=None, tn=None, tk=None):
    m, ka = a.shape
    ns, r, cs = w3.shape
    if not nt:
        k_dim, n = r, ns * cs
        tn = tn or min(cs, 1024)
        tk = tk or min(k_dim, 1024)
        nbs = cs // tn
        w_spec = pl.BlockSpec((None, tk, tn), lambda i, j, k: (j // nbs, k, j % nbs))
    else:
        k_dim, n = ns * cs, r
        tn = tn or min(r, 1024)
        tk = tk or min(cs, 1024)
        kbs = cs // tk
        w_spec = pl.BlockSpec((None, tn, tk), lambda i, j, k: (k // kbs, j, k % kbs))
    assert ka == k_dim, (name, a.shape, w3.shape)
    tm = tm or min(m, 1024)
    nk = k_dim // tk
    ne, no = len(extras), len(out_dtypes)

    def kern(a_ref, w_ref, *rest):
        e_refs = rest[:ne]
        o_refs = rest[ne:ne + no]
        acc = rest[ne + no]
        k = pl.program_id(2)

        @pl.when(k == 0)
        def _():
            acc[...] = jnp.zeros_like(acc)

        av = a_ref[...].astype(BF)
        acc[...] += _dot_nt(av, w_ref[...]) if nt else _dot(av, w_ref[...])

        @pl.when(k == nk - 1)
        def _():
            res = acc[...]
            vals = epi(res, *[e[...] for e in e_refs]) if epi else (res,)
            for o, v in zip(o_refs, vals):
                o[...] = v.astype(o.dtype)

    in_specs = [pl.BlockSpec((tm, tk), lambda i, j, k: (i, k)), w_spec]
    in_specs += [pl.BlockSpec((tm, tn), lambda i, j, k: (i, j)) for _ in extras]
    res = pl.pallas_call(
        kern, name=name, grid=(m // tm, n // tn, nk), in_specs=in_specs,
        out_specs=[pl.BlockSpec((tm, tn), lambda i, j, k: (i, j)) for _ in out_dtypes],
        out_shape=[jax.ShapeDtypeStruct((m, n), dt) for dt in out_dtypes],
        scratch_shapes=[pltpu.VMEM((tm, tn), F32)],
        compiler_params=_params(("parallel", "parallel", "arbitrary")),
    )(a, w3, *extras)
    return res[0] if no == 1 else res


def _mm_tn(name, a, b, tm=None, tn=None, tk=None):
    t, ka = a.shape
    _, n = b.shape
    tm = tm or min(ka, 1024)
    tn = tn or min(n, 1024)
    tk = tk or min(t, 512)

    def kern(a_ref, b_ref, o_ref):
        @pl.when(pl.program_id(2) == 0)
        def _():
            o_ref[...] = jnp.zeros_like(o_ref)
        o_ref[...] += _dot_tn(a_ref[...].astype(BF), b_ref[...].astype(BF))

    return pl.pallas_call(
        kern, name=name, grid=(ka // tm, n // tn, t // tk),
        in_specs=[pl.BlockSpec((tk, tm), lambda i, j, k: (k, i)), pl.BlockSpec((tk, tn), lambda i, j, k: (k, j))],
        out_specs=pl.BlockSpec((tm, tn), lambda i, j, k: (i, j)),
        out_shape=jax.ShapeDtypeStruct((ka, n), F32),
        compiler_params=_params(("parallel", "parallel", "arbitrary")),
    )(a, b)


def _rms(x, g):
    return x * lax.rsqrt(jnp.mean(x * x, axis=-1, keepdims=True) + EPS) * g


def _rms_bwd(x, g, dy):
    r = lax.rsqrt(jnp.mean(x * x, axis=-1, keepdims=True) + EPS)
    xh = x * r
    dxh = dy * g
    dx = r * (dxh - xh * jnp.mean(dxh * xh, axis=-1, keepdims=True))
    return dx, jnp.sum(dy * xh, axis=0, keepdims=True)


def _rot(t, c, s):
    lane = lax.broadcasted_iota(jnp.int32, t.shape, 1)
    swapped = jnp.where(lane < ROT_DIM // 2, pltpu.roll(t, HEAD_DIM - ROT_DIM // 2, 1), pltpu.roll(t, ROT_DIM // 2, 1))
    return t * c + swapped * s


def _rope_tables(t):
    half = ROT_DIM // 2
    pos = jnp.arange(t, dtype=F32)
    inv_freq = ROPE_THETA ** (-jnp.arange(0, ROT_DIM, 2, dtype=F32) / ROT_DIM)
    ang = pos[:, None] * inv_freq[None, :]
    cos, sin = jnp.cos(ang), jnp.sin(ang)
    ones = jnp.ones((t, HEAD_DIM - ROT_DIM), F32)
    c_tab = jnp.concatenate([cos, cos, ones], axis=1)
    s_tab = jnp.concatenate([-sin, sin, 0.0 * ones], axis=1)
    return c_tab, s_tab


def _heads(t):
    return [t[:, h * HEAD_DIM:(h + 1) * HEAD_DIM] for h in range(t.shape[1] // HEAD_DIM)]


def _rope_fwd_fn(zq, zk, zv, c, s):
    q = jnp.concatenate([_rot(th, c, s) for th in _heads(zq)], axis=1)
    k = jnp.concatenate([_rot(th, c, s) for th in _heads(zk)], axis=1)
    return q, k, zv


def _rope_bwd_fn(dq0, dq1, dq2, dk0, dk1, dk2, dv0, dv1, dv2, c, s):
    parts = []
    for grp in (dq0, dq1, dq2, dk0, dk1, dk2):
        parts += [_rot(th, c, -s) for th in _heads(grp)]
    parts += [dv0, dv1, dv2]
    return (jnp.concatenate(parts, axis=1),)


def _merge_fn(o0, o1, o2, l0, l1, l2):
    m = jnp.maximum(jnp.maximum(l0, l1), l2)
    e0, e1, e2 = jnp.exp(l0 - m), jnp.exp(l1 - m), jnp.exp(l2 - m)
    tot = e0 + e1 + e2
    return (e0 * o0 + e1 * o1 + e2 * o2) / tot, m + jnp.log(tot)


def _ln_parts(c1):
    mu = jnp.mean(c1, axis=-1, keepdims=True)
    xc = c1 - mu
    r = lax.rsqrt(jnp.mean(xc * xc, axis=-1, keepdims=True) + EPS)
    return xc * r, r


def _ln_silu_fn(c1, g, b):
    xh, _ = _ln_parts(c1)
    yl = xh * g + b
    return (yl * _sig(yl),)


def _ln_silu_bwd_fn(c1, dout, g, b):
    xh, r = _ln_parts(c1)
    yl = xh * g + b
    s = _sig(yl)
    dyl = dout * (s + yl * s * (1.0 - s))
    dxh = dyl * g
    dx = r * (dxh - jnp.mean(dxh, axis=-1, keepdims=True) - xh * jnp.mean(dxh * xh, axis=-1, keepdims=True))
    return dx, jnp.sum(dyl * xh, axis=0, keepdims=True), jnp.sum(dyl, axis=0, keepdims=True)


def _gate_fn(ya, yc, ga, gb, bg):
    sa = _sig(ga + bg[:, :D_MODEL])
    sb = _sig(gb + bg[:, D_MODEL:])
    return (sa * ya + sb * yc,)


def _gate_bwd_fn(dm, ya, yc, ga, gb, bg):
    sa = _sig(ga + bg[:, :D_MODEL])
    sb = _sig(gb + bg[:, D_MODEL:])
    dga = dm * ya * sa * (1.0 - sa)
    dgb = dm * yc * sb * (1.0 - sb)
    dgate = jnp.concatenate([dga, dgb], axis=1)
    return dm * sa, dm * sb, dgate, jnp.sum(dgate, axis=0, keepdims=True)


def _final_fn(x3, tgt, g):
    err = _rms(x3, g) - tgt
    lrow = jnp.sum(err * err, axis=-1, keepdims=True) * (0.5 / D_MODEL)
    lsum = jnp.sum(lrow, axis=0, keepdims=True)
    dx, dg = _rms_bwd(x3, g, err * (1.0 / D_MODEL))
    return dx, jnp.broadcast_to(lsum, (1, LANES)), dg


def _attn_geometry(t, dil):
    cls = t // dil
    rows = min(4 * BLK, cls)
    return rows, rows // BLK, cls // rows


def _band_masks():
    row = lax.broadcasted_iota(jnp.int32, (BLK, BLK), 0)
    col = lax.broadcasted_iota(jnp.int32, (BLK, BLK), 1)
    return row, col


def _attn_fwd(name, q, k, v, t, dil):
    rows, nbk, spc = _attn_geometry(t, dil)
    nblk = t // BLK

    def kern(q_ref, k_ref, kh_ref, v_ref, vh_ref, o_ref, l_ref):
        i = pl.program_id(0)
        first_shift = jnp.where(i % spc == 0, BLK, 0)
        row, col = _band_masks()
        for h in range(HEADS_PER_GROUP):
            hs = slice(h * HEAD_DIM, (h + 1) * HEAD_DIM)
            for b in range(nbk):
                rs = slice(b * BLK, (b + 1) * BLK)
                qb, kc, vc = q_ref[rs, hs], k_ref[rs, hs], v_ref[rs, hs]
                if b == 0:
                    kp, vp, shift = kh_ref[:, hs], vh_ref[:, hs], first_shift
                else:
                    ps = slice((b - 1) * BLK, b * BLK)
                    kp, vp, shift = k_ref[ps, hs], v_ref[ps, hs], 0
                sc = jnp.where(col <= row, _dot_nt(qb, kc) * ATTN_SCALE, NEG)
                sp = jnp.where(col >= row + shift, _dot_nt(qb, kp) * ATTN_SCALE, NEG)
                m = jnp.maximum(jnp.max(sc, axis=1, keepdims=True), jnp.max(sp, axis=1, keepdims=True))
                pc, pp = jnp.exp(sc - m), jnp.exp(sp - m)
                tot = jnp.sum(pc, axis=1, keepdims=True) + jnp.sum(pp, axis=1, keepdims=True)
                o_ref[rs, hs] = (_dot(pc.astype(BF), vc) + _dot(pp.astype(BF), vp)) / tot
                l_ref[rs, hs] = jnp.broadcast_to(m + jnp.log(tot), (BLK, HEAD_DIM))

    def cur(cb):
        return pl.BlockSpec((rows, GROUP_W), lambda i: (i, cb))

    def halo(cb):
        return pl.BlockSpec((BLK, GROUP_W), lambda i: (jnp.maximum(i * nbk - 1, 0), cb))

    (qa, qc), (ka, kc_), (va, vc_) = q, k, v
    return pl.pallas_call(
        kern, name=name, grid=(t // rows,),
        in_specs=[cur(qc), cur(kc_), halo(kc_), cur(vc_), halo(vc_)],
        out_specs=[pl.BlockSpec((rows, GROUP_W), lambda i: (i, 0))] * 2,
        out_shape=[jax.ShapeDtypeStruct((t, GROUP_W), F32)] * 2,
        compiler_params=_params(("parallel",)),
    )(qa, ka, ka, va, va)


def _attn_bwd(name, q, k, v, da, at, lse, t, dil):
    rows, nbk, spc = _attn_geometry(t, dil)
    nblk = t // BLK

    def kern(q_ref, qn_ref, k_ref, kh_ref, v_ref, vh_ref, da_ref, dan_ref, at_ref, atn_ref, ls_ref, lsn_ref,
             dq_ref, dk_ref, dv_ref):
        i = pl.program_id(0)
        first_shift = jnp.where(i % spc == 0, BLK, 0)
        next_shift = jnp.where((i + 1) % spc == 0, BLK, 0)
        row, col = _band_masks()
        for h in range(HEADS_PER_GROUP):
            hs = slice(h * HEAD_DIM, (h + 1) * HEAD_DIM)
            dq = [None] * nbk
            dk = [None] * nbk
            dv = [None] * nbk
            for b in range(nbk + 1):
                rs = slice(b * BLK, (b + 1) * BLK)
                if b < nbk:
                    qb, dab, atb, ls = q_ref[rs, hs], da_ref[rs, hs], at_ref[rs, hs], ls_ref[rs, hs]
                else:
                    qb, dab, atb, ls = qn_ref[:, hs], dan_ref[:, hs], atn_ref[:, hs], lsn_ref[:, hs]
                delta = jnp.sum(dab * atb, axis=1, keepdims=True)
                dab = dab.astype(BF)
                if b < nbk:
                    kc, vc = k_ref[rs, hs], v_ref[rs, hs]
                    p = jnp.where(col <= row, jnp.exp(_dot_nt(qb, kc) * ATTN_SCALE - ls), 0.0)
                    ds = (p * (_dot_nt(dab, vc) - delta) * ATTN_SCALE).astype(BF)
                    dv[b] = _dot_tn(p.astype(BF), dab)
                    dk[b] = _dot_tn(ds, qb)
                    dq[b] = _dot(ds, kc)
                if b == 0:
                    kp, vp, shift = kh_ref[:, hs], vh_ref[:, hs], first_shift
                else:
                    ps = slice((b - 1) * BLK, b * BLK)
                    kp, vp, shift = k_ref[ps, hs], v_ref[ps, hs], (next_shift if b == nbk else 0)
                p = jnp.where(col >= row + shift, jnp.exp(_dot_nt(qb, kp) * ATTN_SCALE - ls), 0.0)
                ds = (p * (_dot_nt(dab, vp) - delta) * ATTN_SCALE).astype(BF)
                if b < nbk:
                    dq[b] = dq[b] + _dot(ds, kp)
                if b >= 1:
                    dv[b - 1] = dv[b - 1] + _dot_tn(p.astype(BF), dab)
                    dk[b - 1] = dk[b - 1] + _dot_tn(ds, qb)
            for b in range(nbk):
                rs = slice(b * BLK, (b + 1) * BLK)
                dq_ref[rs, hs] = dq[b]
                dk_ref[rs, hs] = dk[b]
                dv_ref[rs, hs] = dv[b]

    def cur(cb):
        return pl.BlockSpec((rows, GROUP_W), lambda i: (i, cb))

    def prev(cb):
        return pl.BlockSpec((BLK, GROUP_W), lambda i: (jnp.maximum(i * nbk - 1, 0), cb))

    def nxt(cb):
        return pl.BlockSpec((BLK, GROUP_W), lambda i: (jnp.minimum((i + 1) * nbk, nblk - 1), cb))

    (qa, qc), (ka, kc_), (va, vc_) = q, k, v
    return pl.pallas_call(
        kern, name=name, grid=(t // rows,),
        in_specs=[cur(qc), nxt(qc), cur(kc_), prev(kc_), cur(vc_), prev(vc_),
                  cur(0), nxt(0), cur(0), nxt(0), cur(0), nxt(0)],
        out_specs=[pl.BlockSpec((rows, GROUP_W), lambda i: (i, 0))] * 3,
        out_shape=[jax.ShapeDtypeStruct((t, GROUP_W), F32)] * 3,
        compiler_params=_params(("parallel",)),
    )(qa, qa, ka, ka, va, va, da, da, at, at, lse, lse)


def _to_classes(a, dil):
    if dil == 1:
        return a
    t, w = a.shape
    return a.reshape(t // dil, dil, w).transpose(1, 0, 2).reshape(t, w)


def _from_classes(a, dil):
    if dil == 1:
        return a
    t, w = a.shape
    return a.reshape(dil, t // dil, w).transpose(1, 0, 2).reshape(t, w)


def _cross_probs(qh, kh):
    s = _dot_nt(qh, kh) * CROSS_SCALE
    e = jnp.exp(s - jnp.max(s, axis=1, keepdims=True))
    return e, jnp.sum(e, axis=1, keepdims=True)


def _cross_fwd(cq, ckv, t):
    rows = min(512, t)

    def kern(q_ref, kv_ref, o_ref):
        for h in range(CROSS_HEADS):
            hs = slice(h * CROSS_HEAD_DIM, (h + 1) * CROSS_HEAD_DIM)
            vs = slice(D_MODEL + h * CROSS_HEAD_DIM, D_MODEL + (h + 1) * CROSS_HEAD_DIM)
            e, tot = _cross_probs(q_ref[:, hs], kv_ref[:, hs])
            o_ref[:, hs] = (_dot(e.astype(BF), kv_ref[:, vs]) / tot).astype(BF)

    return pl.pallas_call(
        kern, name="cross_fwd", grid=(t // rows,),
        in_specs=[pl.BlockSpec((rows, D_MODEL), lambda i: (i, 0)), pl.BlockSpec((N_MEM, 2 * D_MODEL), lambda i: (0, 0))],
        out_specs=pl.BlockSpec((rows, D_MODEL), lambda i: (i, 0)),
        out_shape=jax.ShapeDtypeStruct((t, D_MODEL), BF),
        compiler_params=_params(("parallel",)),
    )(cq, ckv)


def _cross_bwd(cq, ckv, dco, t):
    rows = min(512, t)

    def kern(q_ref, kv_ref, do_ref, dq_ref, dkv_ref):
        @pl.when(pl.program_id(0) == 0)
        def _():
            dkv_ref[...] = jnp.zeros_like(dkv_ref)
        for h in range(CROSS_HEADS):
            hs = slice(h * CROSS_HEAD_DIM, (h + 1) * CROSS_HEAD_DIM)
            vs = slice(D_MODEL + h * CROSS_HEAD_DIM, D_MODEL + (h + 1) * CROSS_HEAD_DIM)
            qh, kh, vh, doh = q_ref[:, hs], kv_ref[:, hs], kv_ref[:, vs], do_ref[:, hs]
            e, tot = _cross_probs(qh, kh)
            p = e / tot
            dp = _dot_nt(doh, vh)
            ds = (p * (dp - jnp.sum(p * dp, axis=1, keepdims=True)) * CROSS_SCALE).astype(BF)
            dq_ref[:, hs] = _dot(ds, kh).astype(BF)
            dkv_ref[:, hs] += _dot_tn(ds, qh)
            dkv_ref[:, vs] += _dot_tn(p.astype(BF), doh)

    return pl.pallas_call(
        kern, name="cross_bwd", grid=(t // rows,),
        in_specs=[pl.BlockSpec((rows, D_MODEL), lambda i: (i, 0)), pl.BlockSpec((N_MEM, 2 * D_MODEL), lambda i: (0, 0)),
                  pl.BlockSpec((rows, D_MODEL), lambda i: (i, 0))],
        out_specs=[pl.BlockSpec((rows, D_MODEL), lambda i: (i, 0)), pl.BlockSpec((N_MEM, 2 * D_MODEL), lambda i: (0, 0))],
        out_shape=[jax.ShapeDtypeStruct((t, D_MODEL), BF), jax.ShapeDtypeStruct((N_MEM, 2 * D_MODEL), F32)],
        compiler_params=_params(("arbitrary",)),
    )(cq, ckv, dco)


CONV_TILE = 512
CONV_CHUNK = 128
HALO = 32


def _conv_fwd(z, w32, bias, t):
    tile = min(CONV_TILE, t)
    a_cb, b_cb = GLU_A_COL // LANES, GLU_B_COL // LANES
    hb = tile // HALO

    def kern(a_ref, b_ref, ah_ref, bh_ref, w_ref, bias_ref, o_ref, g_scr):
        i = pl.program_id(1)
        g_scr[HALO:, :] = a_ref[...] * _sig(b_ref[...])
        g_scr[:HALO, :] = ah_ref[...] * _sig(bh_ref[...]) * jnp.where(i > 0, 1.0, 0.0)
        for c in range(tile // CONV_CHUNK):
            acc = jnp.broadcast_to(bias_ref[...], (CONV_CHUNK, LANES))
            for j in range(CONV_K):
                lo = c * CONV_CHUNK + HALO - (CONV_K - 1) + j
                acc = acc + w_ref[j:j + 1, :] * g_scr[lo:lo + CONV_CHUNK, :]
            o_ref[c * CONV_CHUNK:(c + 1) * CONV_CHUNK, :] = acc

    def cur(cb):
        return pl.BlockSpec((tile, LANES), lambda j, i: (i, cb + j))

    def prev(cb):
        return pl.BlockSpec((HALO, LANES), lambda j, i: (jnp.maximum(i * hb - 1, 0), cb + j))

    return pl.pallas_call(
        kern, name="conv_fwd", grid=(CONV_CH // LANES, t // tile),
        in_specs=[cur(a_cb), cur(b_cb), prev(a_cb), prev(b_cb),
                  pl.BlockSpec((CONV_KP, LANES), lambda j, i: (0, j)), pl.BlockSpec((1, LANES), lambda j, i: (0, j))],
        out_specs=pl.BlockSpec((tile, LANES), lambda j, i: (i, j)),
        out_shape=jax.ShapeDtypeStruct((t, CONV_CH), F32),
        scratch_shapes=[pltpu.VMEM((tile + HALO, LANES), F32)],
        compiler_params=_params(("parallel", "parallel")),
    )(z, z, z, z, w32, bias)


def _conv_bwd(z, dc1, w32, t):
    tile = min(CONV_TILE, t)
    a_cb, b_cb = GLU_A_COL // LANES, GLU_B_COL // LANES
    hb = tile // HALO
    n_tiles = t // tile
    n_chunks = tile // CONV_CHUNK

    def kern(a_ref, b_ref, ah_ref, bh_ref, d_ref, dn_ref, w_ref, da_ref, db_ref, dw_ref, g_scr, d_scr):
        i = pl.program_id(1)
        sg = _sig(b_ref[...])
        g_scr[HALO:, :] = a_ref[...] * sg
        g_scr[:HALO, :] = ah_ref[...] * _sig(bh_ref[...]) * jnp.where(i > 0, 1.0, 0.0)
        d_scr[:tile, :] = d_ref[...]
        d_scr[tile:, :] = dn_ref[...] * jnp.where(i < n_tiles - 1, 1.0, 0.0)

        @pl.when(i == 0)
        def _():
            dw_ref[...] = jnp.zeros_like(dw_ref)

        for c in range(n_chunks):
            cs = slice(c * CONV_CHUNK, (c + 1) * CONV_CHUNK)
            acc = jnp.zeros((CONV_CHUNK, LANES), F32)
            for j in range(CONV_K):
                lo = c * CONV_CHUNK + (CONV_K - 1) - j
                acc = acc + w_ref[j:j + 1, :] * d_scr[lo:lo + CONV_CHUNK, :]
            sgc = _sig(b_ref[cs, :])
            da_ref[cs, :] = (acc * sgc).astype(BF)
            db_ref[cs, :] = (acc * a_ref[cs, :] * sgc * (1.0 - sgc)).astype(BF)
        for j in range(CONV_K):
            tot = jnp.zeros((1, LANES), F32)
            for c in range(n_chunks):
                lo = c * CONV_CHUNK + HALO - (CONV_K - 1) + j
                tot = tot + jnp.sum(d_ref[c * CONV_CHUNK:(c + 1) * CONV_CHUNK, :] * g_scr[lo:lo + CONV_CHUNK, :],
                                    axis=0, keepdims=True)
            dw_ref[j:j + 1, :] += tot
        dw_ref[CONV_K:CONV_KP, :] += jnp.sum(d_ref[...], axis=0, keepdims=True)

    def cur(cb):
        return pl.BlockSpec((tile, LANES), lambda j, i: (i, cb + j))

    def prev(cb):
        return pl.BlockSpec((HALO, LANES), lambda j, i: (jnp.maximum(i * hb - 1, 0), cb + j))

    return pl.pallas_call(
        kern, name="conv_bwd", grid=(CONV_CH // LANES, n_tiles),
        in_specs=[cur(a_cb), cur(b_cb), prev(a_cb), prev(b_cb), cur(0),
                  pl.BlockSpec((HALO, LANES), lambda j, i: (jnp.minimum((i + 1) * hb, t // HALO - 1), j)),
                  pl.BlockSpec((CONV_KP, LANES), lambda j, i: (0, j))],
        out_specs=[pl.BlockSpec((tile, LANES), lambda j, i: (i, j)), pl.BlockSpec((tile, LANES), lambda j, i: (i, j)),
                   pl.BlockSpec((CONV_KP, LANES), lambda j, i: (0, j))],
        out_shape=[jax.ShapeDtypeStruct((t, CONV_CH), BF), jax.ShapeDtypeStruct((t, CONV_CH), BF),
                   jax.ShapeDtypeStruct((CONV_KP, CONV_CH), F32)],
        scratch_shapes=[pltpu.VMEM((tile + HALO, LANES), F32), pltpu.VMEM((tile + HALO, LANES), F32)],
        compiler_params=_params(("parallel", "arbitrary")),
    )(z, z, z, z, dc1, dc1, w32)


def _adam_fn(w, g, m, v):
    m = ADAM_B1 * m + (1.0 - ADAM_B1) * g
    v = ADAM_B2 * v + (1.0 - ADAM_B2) * (g * g)
    m_hat = m / (1.0 - ADAM_B1 ** ADAM_STEP)
    v_hat = v / (1.0 - ADAM_B2 ** ADAM_STEP)
    delta = -ADAM_LR * (m_hat / (jnp.sqrt(v_hat) + ADAM_EPS) + ADAM_WD * w)
    return delta, m, v


def _adam(name, w, g, m, v):
    rows, cols = w.shape
    tile = _ew_tile(rows, cols)
    return _rowcall(name, _adam_fn, rows, tile, [(a, cols, 0) for a in (w, g, m, v)], [], [(cols, F32)] * 3)


def _place():
    x, y, c = lax.axis_index("x"), lax.axis_index("y"), lax.axis_index("c")
    chips = [(1 - x, y), (x, 1 - y), (1 - x, 1 - y)]
    return x, y, c, chips


def _gather_weights(shards):
    nw = len(shards)
    chip = 2 * lax.axis_index("x") + lax.axis_index("y")
    staged = [lax.dynamic_update_index_in_dim(jnp.zeros((4,) + s.shape, s.dtype), s, chip, 0) for s in shards]

    def body(*refs):
        outs = refs[nw:2 * nw]
        send_sems, recv_sems = refs[2 * nw:]
        x, y, c, chips = _place()
        me = 2 * x + y
        sibling = (x, y, 1 - c)

        def half(w, lead, h):
            n = shards[w].shape[0] // 2
            return outs[w].at[lead, pl.ds(h * n, n)]

        def copy(w, k, part, to):
            return pltpu.make_async_remote_copy(src_ref=part, dst_ref=part, send_sem=send_sems.at[w, k],
                                                recv_sem=recv_sems.at[w, k], device_id=to, device_id_type=MESH)

        first, passed = [], []
        for w in range(nw):
            for k, (px, py) in enumerate(chips):
                first.append(copy(w, k, half(w, me, c), (px, py, c)))
                first[-1].start()
        for w in range(nw):
            for k, (px, py) in enumerate(chips):
                landed = half(w, 2 * px + py, c)
                copy(w, k, landed, (px, py, c)).wait_recv()
                passed.append(copy(w, 3 + k, landed, sibling))
                passed[-1].start()
        for w in range(nw):
            for k, (px, py) in enumerate(chips):
                copy(w, 3 + k, half(w, 2 * px + py, 1 - c), sibling).wait_recv()
        for cp in first + passed:
            cp.wait_send()

    return pl.pallas_call(
        body, name="gather_weights",
        in_specs=[ANY] * nw, out_specs=[ANY] * nw,
        out_shape=[jax.ShapeDtypeStruct((4,) + s.shape, s.dtype) for s in shards],
        input_output_aliases={w: w for w in range(nw)},
        scratch_shapes=[pltpu.SemaphoreType.DMA((nw, 6)), pltpu.SemaphoreType.DMA((nw, 6))],
    )(*staged)


def _gather8(name, v):
    rows = v.shape[0]

    def body(v_ref, all_ref, sum_ref, send_sems, recv_sems):
        x, y, c, _ = _place()
        me = 4 * x + 2 * y + c
        all_ref[me] = v_ref[...]
        copies = []
        for k in range(1, 8):
            px, py, pc = x ^ (k >> 2), y ^ ((k >> 1) & 1), c ^ (k & 1)
            copies.append(pltpu.make_async_remote_copy(
                src_ref=v_ref, dst_ref=all_ref.at[me], send_sem=send_sems.at[k - 1], recv_sem=recv_sems.at[k - 1],
                device_id=(px, py, pc), device_id_type=MESH))
            copies[-1].start()
        for k in range(1, 8):
            px, py, pc = x ^ (k >> 2), y ^ ((k >> 1) & 1), c ^ (k & 1)
            theirs = all_ref.at[4 * px + 2 * py + pc]
            pltpu.make_async_remote_copy(
                src_ref=theirs, dst_ref=theirs, send_sem=send_sems.at[k - 1], recv_sem=recv_sems.at[k - 1],
                device_id=(px, py, pc), device_id_type=MESH).wait_recv()
        for cp in copies:
            cp.wait_send()
        tot = all_ref[0]
        for d in range(1, 8):
            tot = tot + all_ref[d]
        sum_ref[...] = tot

    vm = pl.BlockSpec(memory_space=pltpu.VMEM)
    return pl.pallas_call(
        body, name=name, in_specs=[vm], out_specs=[vm, vm],
        out_shape=[jax.ShapeDtypeStruct((8, rows, LANES), F32), jax.ShapeDtypeStruct((rows, LANES), F32)],
        scratch_shapes=[pltpu.SemaphoreType.DMA((7,)), pltpu.SemaphoreType.DMA((7,))],
    )(v)


def _region(ref, col_sharded, shape, j, h):
    r, ccols = shape
    if col_sharded:
        return ref.at[pl.ds(h * (r // 2), r // 2), pl.ds(j * (ccols // 4), ccols // 4)]
    n = r // 8
    return ref.at[pl.ds((2 * j + h) * n, n), :]


def _region_shape(col_sharded, shape):
    r, ccols = shape
    return (r // 2, ccols // 4) if col_sharded else (r // 8, ccols)


def _swap_halves(grads, kinds):
    nw = len(grads)

    def body(*refs):
        ins, theirs = refs[:nw], refs[nw:2 * nw]
        send_sems, recv_sems = refs[2 * nw:]
        x, y, c, _ = _place()
        sent = []
        for w in range(nw):
            for j in range(4):
                sent.append(pltpu.make_async_remote_copy(
                    src_ref=_region(ins[w], kinds[w], grads[w].shape, j, 1 - c), dst_ref=theirs[w].at[j],
                    send_sem=send_sems.at[w, j], recv_sem=recv_sems.at[w, j], device_id=(x, y, 1 - c), device_id_type=MESH))
                sent[-1].start()
        for cp in sent:
            cp.wait()

    return pl.pallas_call(
        body, name="grad_swap_halves", in_specs=[ANY] * nw, out_specs=[ANY] * nw,
        out_shape=[jax.ShapeDtypeStruct((4,) + _region_shape(kinds[w], grads[w].shape), F32) for w in range(nw)],
        scratch_shapes=[pltpu.SemaphoreType.DMA((nw, 4)), pltpu.SemaphoreType.DMA((nw, 4))],
    )(*grads)


def _kept_halves(grad, col_sharded, c):
    r, ccols = grad.shape
    if col_sharded:
        slab = lax.dynamic_slice_in_dim(grad, c * (r // 2), r // 2, axis=0)
        return slab.reshape(r // 2, 4, ccols // 4).transpose(1, 0, 2)
    return lax.dynamic_index_in_dim(grad.reshape(4, 2, r // 8, ccols), c, axis=1, keepdims=False)


def _scatter_to_owners(parts):
    nw = len(parts)

    def body(*refs):
        ins, outs = refs[:nw], refs[nw:2 * nw]
        send_sems, recv_sems = refs[2 * nw:]
        x, y, c, chips = _place()
        sent = []
        for w in range(nw):
            for k, (px, py) in enumerate(chips):
                sent.append(pltpu.make_async_remote_copy(
                    src_ref=ins[w].at[2 * px + py], dst_ref=outs[w].at[k], send_sem=send_sems.at[w, k],
                    recv_sem=recv_sems.at[w, k], device_id=(px, py, c), device_id_type=MESH))
                sent[-1].start()
        for cp in sent:
            cp.wait()

    return pl.pallas_call(
        body, name="grad_scatter", in_specs=[ANY] * nw, out_specs=[ANY] * nw,
        out_shape=[jax.ShapeDtypeStruct((3,) + p.shape[1:], F32) for p in parts],
        scratch_shapes=[pltpu.SemaphoreType.DMA((nw, 3)), pltpu.SemaphoreType.DMA((nw, 3))],
    )(*parts)


def _share_halves(halves):
    nw = len(halves)

    def body(*refs):
        ins, outs = refs[:nw], refs[nw:2 * nw]
        send_sems, recv_sems = refs[2 * nw:]
        x, y, c, _ = _place()
        sent = []
        for w in range(nw):
            sent.append(pltpu.make_async_remote_copy(
                src_ref=ins[w], dst_ref=outs[w], send_sem=send_sems.at[w], recv_sem=recv_sems.at[w],
                device_id=(x, y, 1 - c), device_id_type=MESH))
            sent[-1].start()
        for cp in sent:
            cp.wait()

    return pl.pallas_call(
        body, name="grad_share_halves", in_specs=[ANY] * nw, out_specs=[ANY] * nw,
        out_shape=[jax.ShapeDtypeStruct(h.shape, F32) for h in halves],
        scratch_shapes=[pltpu.SemaphoreType.DMA((nw,)), pltpu.SemaphoreType.DMA((nw,))],
    )(*halves)


def _ew_tile(rows, cols):
    limit = max(8, (256 * 1024) // cols)
    return max(d for d in range(8, min(rows, limit) + 1, 8) if rows % d == 0)


def _reduce_grads(grads, kinds):
    nw = len(grads)
    c = lax.axis_index("c")
    chip = 2 * lax.axis_index("x") + lax.axis_index("y")
    theirs = _swap_halves(grads, kinds)
    parts = []
    for w in range(nw):
        _, rr, cc = theirs[w].shape
        a = _kept_halves(grads[w], kinds[w], c).reshape(4 * rr, cc)
        (p,) = _rowcall(f"grad_pair_sum_{w}", lambda u, v: (u + v,), 4 * rr, _ew_tile(4 * rr, cc),
                        [(a, cc, 0), (theirs[w].reshape(4 * rr, cc), cc, 0)], [], [(cc, F32)])
        parts.append(p.reshape(4, rr, cc))
    landed = _scatter_to_owners(parts)
    halves = []
    for w in range(nw):
        _, rr, cc = landed[w].shape
        tile = _ew_tile(rr, cc)
        own = lax.dynamic_index_in_dim(parts[w], chip, axis=0, keepdims=False)
        flat = landed[w].reshape(3 * rr, cc)
        nb = rr // tile
        (f,) = _rowcall(f"grad_chip_sum_{w}", lambda own, k0, k1, k2: (((own + k0) + k1) + k2,), rr, tile,
                        [(own, cc, 0), (flat, cc, 0, 0), (flat, cc, 0, nb), (flat, cc, 0, 2 * nb)], [], [(cc, F32)])
        halves.append(f)
    other = _share_halves(halves)
    return [jnp.where(c == 0, jnp.concatenate([f, o], axis=0), jnp.concatenate([o, f], axis=0))
            for f, o in zip(halves, other)]


BIG = ("w_in", "w_attn_proj", "w_conv_proj", "w_out", "w_cq", "w_ckv", "w_co", "w_up", "w_down")
COL_SHARDED = {"w_in": True, "w_attn_proj": True, "w_conv_proj": True, "w_out": False, "w_cq": False,
               "w_ckv": True, "w_co": False, "w_up": True, "w_down": False}
SMALL = ("g_mix", "b_gate", "conv_b", "conv_ln_g", "conv_ln_b", "g_cross", "g_mem", "g_mlp", "g_final")
ORDER = ("g_mix", "w_in", "b_gate", "conv_w", "conv_b", "conv_ln_g", "conv_ln_b", "w_attn_proj", "w_conv_proj", "w_out",
         "g_cross", "g_mem", "w_cq", "w_ckv", "w_co", "g_mlp", "w_up", "w_down", "g_final")


def _pad_rows(flat, rows):
    return jnp.pad(flat, (0, rows * LANES - flat.shape[0])).reshape(rows, LANES)


def _local_step(x, mem, tgt, wfull, small, conv_w_full):
    t = x.shape[0]
    tr = 256
    c_tab, s_tab = _rope_tables(t)
    row = lambda v: v.reshape(1, -1)
    g_mix, g_cross, g_mem, g_mlp, g_final = (row(small[n]) for n in ("g_mix", "g_cross", "g_mem", "g_mlp", "g_final"))
    b_gate, conv_b, ln_g, ln_b = (row(small[n]) for n in ("b_gate", "conv_b", "conv_ln_g", "conv_ln_b"))
    w32 = jnp.pad(conv_w_full, ((0, CONV_KP - CONV_K), (0, 0)))

    (u,) = _rowcall("mix_norm", lambda a, g: (_rms(a, g),), t, tr, [(x, D_MODEL, 0)], [g_mix], [(D_MODEL, BF)])
    z = _mm("in_proj", u, wfull["w_in"])
    qr, kr, vb = _rowcall("rope", _rope_fwd_fn, t, tr,
                          [(z, ATTN_WIDTH, 0), (z, ATTN_WIDTH, 1), (z, ATTN_WIDTH, 2), (c_tab, HEAD_DIM, 0), (s_tab, HEAD_DIM, 0)],
                          [], [(ATTN_WIDTH, BF)] * 3)
    qkv_cls, outs, lses = [], [], []
    for g, dil in enumerate(DILATIONS):
        if dil == 1:
            ops = ((qr, g), (kr, g), (vb, g))
        else:
            gs = slice(g * GROUP_W, (g + 1) * GROUP_W)
            ops = tuple((_to_classes(a[:, gs], dil), 0) for a in (qr, kr, vb))
        qkv_cls.append(ops)
        o_g, l_g = _attn_fwd(f"attn_fwd_{g}", *ops, t, dil)
        outs.append(_from_classes(o_g, dil))
        lses.append(_from_classes(l_g, dil))
    attn, lse = _rowcall("attn_merge", _merge_fn, t, tr, [(a, GROUP_W, 0) for a in outs + lses], [],
                         [(GROUP_W, F32), (GROUP_W, F32)])
    y_attn = _mm("attn_proj", attn, wfull["w_attn_proj"])
    c1 = _conv_fwd(z, w32, conv_b, t)
    (c2,) = _rowcall("conv_ln_silu", _ln_silu_fn, t, tr, [(c1, CONV_CH, 0)], [ln_g, ln_b], [(CONV_CH, BF)])
    y_conv = _mm("conv_proj", c2, wfull["w_conv_proj"])
    gate_ins = [(z, D_MODEL, GATE_A_COL // D_MODEL), (z, D_MODEL, GATE_B_COL // D_MODEL)]
    (merged,) = _rowcall("gate", _gate_fn, t, tr, [(y_attn, D_MODEL, 0), (y_conv, D_MODEL, 0)] + gate_ins, [b_gate],
                         [(D_MODEL, BF)])
    add = lambda acc, res: (res + acc,)
    x1 = _mm("out_proj", merged, wfull["w_out"], extras=(x,), epi=add)

    (uq,) = _rowcall("cross_norm", lambda a, g: (_rms(a, g),), t, tr, [(x1, D_MODEL, 0)], [g_cross], [(D_MODEL, BF)])
    (mn,) = _rowcall("mem_norm", lambda a, g: (_rms(a, g),), N_MEM, N_MEM, [(mem, D_MODEL, 0)], [g_mem], [(D_MODEL, BF)])
    cq = _mm("cross_q", uq, wfull["w_cq"], out_dtypes=(BF,))
    ckv = _mm("cross_kv", mn, wfull["w_ckv"], out_dtypes=(BF,))
    co = _cross_fwd(cq, ckv, t)
    x2 = _mm("cross_out", co, wfull["w_co"], extras=(x1,), epi=add)

    (um,) = _rowcall("mlp_norm", lambda a, g: (_rms(a, g),), t, tr, [(x2, D_MODEL, 0)], [g_mlp], [(D_MODEL, BF)])
    hpre, hact = _mm("mlp_up", um, wfull["w_up"], out_dtypes=(F32, BF),
                     epi=lambda acc: (acc, jnp.square(jnp.maximum(acc, 0.0))))
    x3 = _mm("mlp_down", hact, wfull["w_down"], extras=(x2,), epi=add)
    d3, loss_row, dg_final = _rowcall("final_norm_loss", _final_fn, t, tr, [(x3, D_MODEL, 0), (tgt, D_MODEL, 0)], [g_final],
                                      [(D_MODEL, F32)], accs=[(1, LANES), (1, D_MODEL)])

    gw = {}
    dhp = _mm("mlp_down_bwd", d3, wfull["w_down"], nt=True, extras=(hpre,), out_dtypes=(BF,),
              epi=lambda acc, hp: (acc * 2.0 * jnp.maximum(hp, 0.0),))
    gw["w_down"] = _mm_tn("mlp_down_wgrad", hact, d3)
    dum = _mm("mlp_up_bwd", dhp, wfull["w_up"], nt=True)
    gw["w_up"] = _mm_tn("mlp_up_wgrad", um, dhp)

    def norm_bwd(a, dn, dres, g):
        dx, dg = _rms_bwd(a, g, dn)
        return dres + dx, dg

    d2, dg_mlp = _rowcall("mlp_norm_bwd", norm_bwd, t, tr, [(x2, D_MODEL, 0), (dum, D_MODEL, 0), (d3, D_MODEL, 0)], [g_mlp],
                          [(D_MODEL, F32)], accs=[(1, D_MODEL)])

    dco = _mm("cross_out_bwd", d2, wfull["w_co"], nt=True, out_dtypes=(BF,))
    gw["w_co"] = _mm_tn("cross_out_wgrad", co, d2)
    dcq, dckv = _cross_bwd(cq, ckv, dco, t)
    gw["w_cq"] = _mm_tn("cross_q_wgrad", uq, dcq)
    duq = _mm("cross_q_bwd", dcq, wfull["w_cq"], nt=True)
    d1, dg_cross = _rowcall("cross_norm_bwd", norm_bwd, t, tr, [(x1, D_MODEL, 0), (duq, D_MODEL, 0), (d2, D_MODEL, 0)],
                            [g_cross], [(D_MODEL, F32)], accs=[(1, D_MODEL)])
    gw["w_ckv"] = _mm_tn("cross_kv_wgrad", mn, dckv, tk=N_MEM)
    dmn = _mm("cross_kv_bwd", dckv, wfull["w_ckv"], nt=True)
    (dg_mem,) = _rowcall("mem_norm_bwd", lambda a, dn, g: (_rms_bwd(a, g, dn)[1],), N_MEM, N_MEM,
                         [(mem, D_MODEL, 0), (dmn, D_MODEL, 0)], [g_mem], [], accs=[(1, D_MODEL)])

    dmerged = _mm("out_proj_bwd", d1, wfull["w_out"], nt=True)
    gw["w_out"] = _mm_tn("out_proj_wgrad", merged, d1)
    dya, dyc, dgate, dbg = _rowcall("gate_bwd", _gate_bwd_fn, t, tr,
                                    [(dmerged, D_MODEL, 0), (y_attn, D_MODEL, 0), (y_conv, D_MODEL, 0)] + gate_ins, [b_gate],
                                    [(D_MODEL, BF), (D_MODEL, BF), (2 * D_MODEL, BF)], accs=[(1, 2 * D_MODEL)])
    gw["w_attn_proj"] = _mm_tn("attn_proj_wgrad", attn, dya)
    dattn = _mm("attn_proj_bwd", dya, wfull["w_attn_proj"], nt=True)
    gw["w_conv_proj"] = _mm_tn("conv_proj_wgrad", c2, dyc)
    dc2 = _mm("conv_proj_bwd", dyc, wfull["w_conv_proj"], nt=True)
    dc1, dlng, dlnb = _rowcall("conv_ln_silu_bwd", _ln_silu_bwd_fn, t, tr, [(c1, CONV_CH, 0), (dc2, CONV_CH, 0)], [ln_g, ln_b],
                               [(CONV_CH, F32)], accs=[(1, CONV_CH), (1, CONV_CH)])
    dglu_a, dglu_b, dconv = _conv_bwd(z, dc1, w32, t)

    dqs, dks, dvs = [], [], []
    for g, dil in enumerate(DILATIONS):
        da_c, at_c, ls_c = (_to_classes(a, dil) for a in (dattn, attn, lse))
        dq_g, dk_g, dv_g = _attn_bwd(f"attn_bwd_{g}", *qkv_cls[g], da_c, at_c, ls_c, t, dil)
        dqs.append(_from_classes(dq_g, dil))
        dks.append(_from_classes(dk_g, dil))
        dvs.append(_from_classes(dv_g, dil))
    (dqkv,) = _rowcall("rope_bwd", _rope_bwd_fn, t, tr,
                       [(a, GROUP_W, 0) for a in dqs + dks + dvs] + [(c_tab, HEAD_DIM, 0), (s_tab, HEAD_DIM, 0)], [],
                       [(3 * ATTN_WIDTH, BF)])
    dz = jnp.concatenate([dqkv, dglu_a, dglu_b, dgate], axis=1)
    du = _mm("in_proj_bwd", dz, wfull["w_in"], nt=True)
    gw["w_in"] = _mm_tn("in_proj_wgrad", u, dz)
    gx, dg_mix = _rowcall("mix_norm_bwd", norm_bwd, t, tr, [(x, D_MODEL, 0), (du, D_MODEL, 0), (d1, D_MODEL, 0)], [g_mix],
                          [(D_MODEL, F32)], accs=[(1, D_MODEL)])

    gsmall = {"g_mix": dg_mix, "b_gate": dbg, "conv_b": dconv[CONV_K:CONV_K + 1], "conv_ln_g": dlng, "conv_ln_b": dlnb,
              "g_cross": dg_cross, "g_mem": dg_mem, "g_mlp": dg_mlp, "g_final": dg_final, "conv_w": dconv[:CONV_K]}
    return loss_row, gx, gw, gsmall


def kernel(x, mem, g_mix, w_in, b_gate, conv_w, conv_b, conv_ln_g, conv_ln_b, w_attn_proj, w_conv_proj, w_out, g_cross, g_mem, w_cq, w_ckv, w_co, g_mlp, w_up, w_down, g_final, loss_target, m_g_mix, m_w_in, m_b_gate, m_conv_w, m_conv_b, m_conv_ln_g, m_conv_ln_b, m_w_attn_proj, m_w_conv_proj, m_w_out, m_g_cross, m_g_mem, m_w_cq, m_w_ckv, m_w_co, m_g_mlp, m_w_up, m_w_down, m_g_final, v_g_mix, v_w_in, v_b_gate, v_conv_w, v_conv_b, v_conv_ln_g, v_conv_ln_b, v_w_attn_proj, v_w_conv_proj, v_w_out, v_g_cross, v_g_mem, v_w_cq, v_w_ckv, v_w_co, v_g_mlp, v_w_up, v_w_down, v_g_final):
    w = dict(g_mix=g_mix, w_in=w_in, b_gate=b_gate, conv_w=conv_w, conv_b=conv_b, conv_ln_g=conv_ln_g, conv_ln_b=conv_ln_b,
             w_attn_proj=w_attn_proj, w_conv_proj=w_conv_proj, w_out=w_out, g_cross=g_cross, g_mem=g_mem, w_cq=w_cq,
             w_ckv=w_ckv, w_co=w_co, g_mlp=g_mlp, w_up=w_up, w_down=w_down, g_final=g_final)
    mo = dict(g_mix=m_g_mix, w_in=m_w_in, b_gate=m_b_gate, conv_w=m_conv_w, conv_b=m_conv_b, conv_ln_g=m_conv_ln_g,
              conv_ln_b=m_conv_ln_b, w_attn_proj=m_w_attn_proj, w_conv_proj=m_w_conv_proj, w_out=m_w_out, g_cross=m_g_cross,
              g_mem=m_g_mem, w_cq=m_w_cq, w_ckv=m_w_ckv, w_co=m_w_co, g_mlp=m_g_mlp, w_up=m_w_up, w_down=m_w_down,
              g_final=m_g_final)
    vo = dict(g_mix=v_g_mix, w_in=v_w_in, b_gate=v_b_gate, conv_w=v_conv_w, conv_b=v_conv_b, conv_ln_g=v_conv_ln_g,
              conv_ln_b=v_conv_ln_b, w_attn_proj=v_w_attn_proj, w_conv_proj=v_w_conv_proj, w_out=v_w_out, g_cross=v_g_cross,
              g_mem=v_g_mem, w_cq=v_w_cq, w_ckv=v_w_ckv, w_co=v_w_co, g_mlp=v_g_mlp, w_up=v_w_up, w_down=v_w_down,
              g_final=v_g_final)
    shapes = {n: w[n].shape for n in ORDER}
    two_d = lambda a: a.reshape(a.shape[-2], a.shape[-1])
    chip = 2 * lax.axis_index("x") + lax.axis_index("y")

    shards = [two_d(w[n]).astype(BF) for n in BIG]
    gathered = _gather_weights(shards)
    wfull = {}
    for n, g in zip(BIG, gathered):
        wfull[n] = g if COL_SHARDED[n] else g.reshape(1, 4 * g.shape[1], g.shape[2])
    cw_rows = 48
    cw_all, _ = _gather8("gather_conv_w", _pad_rows(conv_w.reshape(-1), cw_rows))
    cw_shard = CONV_K * (CONV_CH // 4)
    conv_w_full = jnp.concatenate(
        [cw_all[2 * j].reshape(-1)[:cw_shard].reshape(CONV_K, CONV_CH // 4) for j in range(4)], axis=1)

    small = {n: w[n] for n in SMALL}
    loss_row, gx, gw, gsmall = _local_step(two_d(x), two_d(mem), two_d(loss_target), wfull, small, conv_w_full)
    loss = lax.psum(loss_row[0, 0], ("x", "y", "c"))

    gshard = dict(zip(BIG, _reduce_grads([gw[n] for n in BIG], [COL_SHARDED[n] for n in BIG])))
    small_names = SMALL + ("conv_w",)
    flat = jnp.concatenate([gsmall[n].reshape(-1) for n in small_names])
    sm_rows = -(-flat.shape[0] // (8 * LANES)) * 8
    _, sm_sum = _gather8("reduce_small_grads", _pad_rows(flat, sm_rows))
    sm_sum = sm_sum.reshape(-1)
    off = 0
    for n in small_names:
        size = gsmall[n].size
        gshard[n] = sm_sum[off:off + size].reshape(gsmall[n].shape)
        off += size
    gshard["conv_w"] = lax.dynamic_slice_in_dim(gshard["conv_w"], chip * (CONV_CH // 4), CONV_CH // 4, axis=1)

    grads, deltas, new_m, new_v = {}, {}, {}, {}
    for n in BIG:
        d, m2, v2 = _adam(f"adamw_{n}", two_d(w[n]), gshard[n], two_d(mo[n]), two_d(vo[n]))
        grads[n], deltas[n], new_m[n], new_v[n] = (a.reshape(shapes[n]) for a in (gshard[n], d, m2, v2))
    pack = lambda src: jnp.concatenate([src[n].reshape(-1) for n in small_names])
    n_small = sum(w[n].size for n in small_names)
    ad_rows = -(-n_small // (8 * LANES)) * 8
    d, m2, v2 = _adam("adamw_small", *[_pad_rows(pack(src), ad_rows) for src in (w, gshard, mo, vo)])
    off = 0
    for n in small_names:
        size = w[n].size
        grads[n] = gshard[n].reshape(shapes[n])
        deltas[n], new_m[n], new_v[n] = (a.reshape(-1)[off:off + size].reshape(shapes[n]) for a in (d, m2, v2))
        off += size

    return (loss, gx.reshape(x.shape), *[grads[n] for n in ORDER], *[deltas[n] for n in ORDER],
            *[new_m[n] for n in ORDER], *[new_v[n] for n in ORDER])
```

```python
import functools

import jax
import jax.numpy as jnp
from jax import lax
from jax.experimental import pallas as pl
from jax.experimental.pallas import tpu as pltpu

F32 = jnp.float32
BF = jnp.bfloat16

D_MODEL = 1024
N_MEM = 256
HEAD_DIM = 128
HEADS_PER_GROUP = 4
DILATIONS = (1, 4, 16)
BLK = 128
GROUP_W = HEADS_PER_GROUP * HEAD_DIM
ATTN_WIDTH = 3 * GROUP_W
ROT_DIM = 32
ROPE_THETA = 500000.0
CONV_CH = 768
CONV_K = 31
CONV_KP = 32
IN_WIDTH = 8192
CROSS_HEADS = 4
CROSS_HEAD_DIM = 256
D_FF = 4096
EPS = 1e-6
ATTN_SCALE = HEAD_DIM ** -0.5
CROSS_SCALE = CROSS_HEAD_DIM ** -0.5
NEG = -1e30

ADAM_LR = 0.001
ADAM_B1 = 0.9
ADAM_B2 = 0.999
ADAM_EPS = 1e-08
ADAM_WD = 0.01
ADAM_STEP = 10

LANES = 128
VMEM_LIMIT = 56 * 1024 * 1024
MESH = pl.DeviceIdType.MESH
ANY = pl.BlockSpec(memory_space=pl.ANY)

GLU_A_COL = 3 * ATTN_WIDTH
GLU_B_COL = GLU_A_COL + CONV_CH
GATE_A_COL = GLU_B_COL + CONV_CH
GATE_B_COL = GATE_A_COL + D_MODEL


def _params(sem=None):
    return pltpu.CompilerParams(dimension_semantics=sem, vmem_limit_bytes=VMEM_LIMIT)


def _dot(a, b):
    return lax.dot_general(a, b, (((1,), (0,)), ((), ())), preferred_element_type=F32)


def _dot_nt(a, b):
    return lax.dot_general(a, b, (((1,), (1,)), ((), ())), preferred_element_type=F32)


def _dot_tn(a, b):
    return lax.dot_general(a, b, (((0,), (0,)), ((), ())), preferred_element_type=F32)


def _sig(x):
    return 1.0 / (1.0 + jnp.exp(-x))


def _rowcall(name, fn, n_rows, tile, ins, params, outs, accs=()):
    tile = min(tile, n_rows)
    ni, npar, no, na = len(ins), len(params), len(outs), len(accs)

    def kern(*refs):
        in_refs = refs[:ni + npar]
        o_refs = refs[ni + npar:ni + npar + no]
        a_refs = refs[ni + npar + no:]
        vals = fn(*[r[...] for r in in_refs])
        for r, v in zip(o_refs, vals[:no]):
            r[...] = v.astype(r.dtype)
        if na:
            @pl.when(pl.program_id(0) == 0)
            def _():
                for r in a_refs:
                    r[...] = jnp.zeros_like(r)
            for r, v in zip(a_refs, vals[no:]):
                r[...] += v

    in_specs = []
    arrays = []
    for spec in ins:
        arr, width, cb = spec[0], spec[1], spec[2]
        rb = spec[3] if len(spec) > 3 else 0
        in_specs.append(pl.BlockSpec((tile, width), functools.partial(lambda i, cb, rb: (i + rb, cb), cb=cb, rb=rb)))
        arrays.append(arr)
    for p in params:
        in_specs.append(pl.BlockSpec(p.shape, lambda i: (0, 0)))
        arrays.append(p)
    out_specs = [pl.BlockSpec((tile, w), lambda i: (i, 0)) for w, _ in outs]
    out_specs += [pl.BlockSpec(s, lambda i: (0, 0)) for s in accs]
    out_shape = [jax.ShapeDtypeStruct((n_rows, w), dt) for w, dt in outs]
    out_shape += [jax.ShapeDtypeStruct(s, F32) for s in accs]
    res = pl.pallas_call(
        kern, name=name, grid=(n_rows // tile,), in_specs=in_specs, out_specs=out_specs, out_shape=out_shape,
        compiler_params=_params(("arbitrary",) if na else ("parallel",)),
    )(*arrays)
    return res


def _mm(name, a, w3, *, nt=False, extras=(), epi=None, out_dtypes=(F32,), tm=None, tn=None, tk=None):
    m, ka = a.shape
    ns, r, cs = w3.shape
    if not nt:
        k_dim, n = r, ns * cs
        tn = tn or min(cs, 1024)
        tk = tk or min(k_dim, 1024)
        nbs = cs // tn
        w_spec = pl.BlockSpec((None, tk, tn), lambda i, j, k: (j // nbs, k, j % nbs))
    else:
        k_dim, n = ns * cs, r
        tn = tn or min(r, 1024)
        tk = tk or min(cs, 1024)
        kbs = cs // tk
        w_spec = pl.BlockSpec((None, tn, tk), lambda i, j, k: (k // kbs, j, k % kbs))
    assert ka == k_dim, (name, a.shape, w3.shape)
    tm = tm or min(m, 1024)
    nk = k_dim // tk
    ne, no = len(extras), len(out_dtypes)

    def kern(a_ref, w_ref, *rest):
        e_refs = rest[:ne]
        o_refs = rest[ne:ne + no]
        acc = rest[ne + no]
        k = pl.program_id(2)

        @pl.when(k == 0)
        def _():
            acc[...] = jnp.zeros_like(acc)

        av = a_ref[...].astype(BF)
        acc[...] += _dot_nt(av, w_ref[...]) if nt else _dot(av, w_ref[...])

        @pl.when(k == nk - 1)
        def _():
            res = acc[...]
            vals = epi(res, *[e[...] for e in e_refs]) if epi else (res,)
            for o, v in zip(o_refs, vals):
                o[...] = v.astype(o.dtype)

    in_specs = [pl.BlockSpec((tm, tk), lambda i, j, k: (i, k)), w_spec]
    in_specs += [pl.BlockSpec((tm, tn), lambda i, j, k: (i, j)) for _ in extras]
    res = pl.pallas_call(
        kern, name=name, grid=(m // tm, n // tn, nk), in_specs=in_specs,
        out_specs=[pl.BlockSpec((tm, tn), lambda i, j, k: (i, j)) for _ in out_dtypes],
        out_shape=[jax.ShapeDtypeStruct((m, n), dt) for dt in out_dtypes],
        scratch_shapes=[pltpu.VMEM((tm, tn), F32)],
        compiler_params=_params(("parallel", "parallel", "arbitrary")),
    )(a, w3, *extras)
    return res[0] if no == 1 else res


def _mm_tn(name, a, b, tm=None, tn=None, tk=None):
    t, ka = a.shape
    _, n = b.shape
    tm = tm or min(ka, 1024)
    tn = tn or min(n, 1024)
    tk = tk or min(t, 512)

    def kern(a_ref, b_ref, o_ref):
        @pl.when(pl.program_id(2) == 0)
        def _():
            o_ref[...] = jnp.zeros_like(o_ref)
        o_ref[...] += _dot_tn(a_ref[...].astype(BF), b_ref[...].astype(BF))

    return pl.pallas_call(
        kern, name=name, grid=(ka // tm, n // tn, t // tk),
        in_specs=[pl.BlockSpec((tk, tm), lambda i, j, k: (k, i)), pl.BlockSpec((tk, tn), lambda i, j, k: (k, j))],
        out_specs=pl.BlockSpec((tm, tn), lambda i, j, k: (i, j)),
        out_shape=jax.ShapeDtypeStruct((ka, n), F32),
        compiler_params=_params(("parallel", "parallel", "arbitrary")),
    )(a, b)


def _rms(x, g):
    return x * lax.rsqrt(jnp.mean(x * x, axis=-1, keepdims=True) + EPS) * g


def _rms_bwd(x, g, dy):
    r = lax.rsqrt(jnp.mean(x * x, axis=-1, keepdims=True) + EPS)
    xh = x * r
    dxh = dy * g
    dx = r * (dxh - xh * jnp.mean(dxh * xh, axis=-1, keepdims=True))
    return dx, jnp.sum(dy * xh, axis=0, keepdims=True)


def _rot(t, c, s):
    lane = lax.broadcasted_iota(jnp.int32, t.shape, 1)
    swapped = jnp.where(lane < ROT_DIM // 2, pltpu.roll(t, HEAD_DIM - ROT_DIM // 2, 1), pltpu.roll(t, ROT_DIM // 2, 1))
    return t * c + swapped * s


def _rope_tables(t):
    half = ROT_DIM // 2
    pos = jnp.arange(t, dtype=F32)
    inv_freq = ROPE_THETA ** (-jnp.arange(0, ROT_DIM, 2, dtype=F32) / ROT_DIM)
    ang = pos[:, None] * inv_freq[None, :]
    cos, sin = jnp.cos(ang), jnp.sin(ang)
    ones = jnp.ones((t, HEAD_DIM - ROT_DIM), F32)
    c_tab = jnp.concatenate([cos, cos, ones], axis=1)
    s_tab = jnp.concatenate([-sin, sin, 0.0 * ones], axis=1)
    return c_tab, s_tab


def _heads(t):
    return [t[:, h * HEAD_DIM:(h + 1) * HEAD_DIM] for h in range(t.shape[1] // HEAD_DIM)]


def _rope_fwd_fn(zq, zk, zv, c, s):
    q = jnp.concatenate([_rot(th, c, s) for th in _heads(zq)], axis=1)
    k = jnp.concatenate([_rot(th, c, s) for th in _heads(zk)], axis=1)
    return q, k, zv


def _rope_bwd_fn(dq0, dq1, dq2, dk0, dk1, dk2, dv0, dv1, dv2, c, s, dglu_a, dglu_b, dgate):
    parts = []
    for grp in (dq0, dq1, dq2, dk0, dk1, dk2):
        parts += [_rot(th, c, -s).astype(BF) for th in _heads(grp)]
    parts += [dv0.astype(BF), dv1.astype(BF), dv2.astype(BF), dglu_a, dglu_b, dgate]
    return (jnp.concatenate(parts, axis=1),)


def _merge_fn(o0, o1, o2, l0, l1, l2):
    m = jnp.maximum(jnp.maximum(l0, l1), l2)
    e0, e1, e2 = jnp.exp(l0 - m), jnp.exp(l1 - m), jnp.exp(l2 - m)
    tot = e0 + e1 + e2
    return (e0 * o0 + e1 * o1 + e2 * o2) / tot, m + jnp.log(tot)


def _ln_parts(c1):
    mu = jnp.mean(c1, axis=-1, keepdims=True)
    xc = c1 - mu
    r = lax.rsqrt(jnp.mean(xc * xc, axis=-1, keepdims=True) + EPS)
    return xc * r, r


def _ln_silu_fn(c1, g, b):
    xh, _ = _ln_parts(c1)
    yl = xh * g + b
    return (yl * _sig(yl),)


def _ln_silu_bwd_fn(c1, dout, g, b):
    xh, r = _ln_parts(c1)
    yl = xh * g + b
    s = _sig(yl)
    dyl = dout * (s + yl * s * (1.0 - s))
    dxh = dyl * g
    dx = r * (dxh - jnp.mean(dxh, axis=-1, keepdims=True) - xh * jnp.mean(dxh * xh, axis=-1, keepdims=True))
    return dx, jnp.sum(dyl * xh, axis=0, keepdims=True), jnp.sum(dyl, axis=0, keepdims=True)


def _gate_fn(ya, yc, ga, gb, bg):
    sa = _sig(ga + bg[:, :D_MODEL])
    sb = _sig(gb + bg[:, D_MODEL:])
    return (sa * ya + sb * yc,)


def _gate_bwd_fn(dm, ya, yc, ga, gb, bg):
    sa = _sig(ga + bg[:, :D_MODEL])
    sb = _sig(gb + bg[:, D_MODEL:])
    dga = dm * ya * sa * (1.0 - sa)
    dgb = dm * yc * sb * (1.0 - sb)
    dgate = jnp.concatenate([dga, dgb], axis=1)
    return dm * sa, dm * sb, dgate, jnp.sum(dgate, axis=0, keepdims=True)


def _final_fn(x3, tgt, g):
    err = _rms(x3, g) - tgt
    lrow = jnp.sum(err * err, axis=-1, keepdims=True) * (0.5 / D_MODEL)
    lsum = jnp.sum(lrow, axis=0, keepdims=True)
    dx, dg = _rms_bwd(x3, g, err * (1.0 / D_MODEL))
    return dx, jnp.broadcast_to(lsum, (1, LANES)), dg


def _attn_geometry(t, dil):
    cls = t // dil
    rows = min(4 * BLK, cls)
    return rows, rows // BLK, cls // rows


def _band_masks():
    row = lax.broadcasted_iota(jnp.int32, (BLK, BLK), 0)
    col = lax.broadcasted_iota(jnp.int32, (BLK, BLK), 1)
    return row, col


def _attn_fwd(name, q, k, v, t, dil):
    rows, nbk, spc = _attn_geometry(t, dil)
    nblk = t // BLK

    def kern(q_ref, k_ref, kh_ref, v_ref, vh_ref, o_ref, l_ref):
        i = pl.program_id(0)
        first_shift = jnp.where(i % spc == 0, BLK, 0)
        row, col = _band_masks()
        for h in range(HEADS_PER_GROUP):
            hs = slice(h * HEAD_DIM, (h + 1) * HEAD_DIM)
            for b in range(nbk):
                rs = slice(b * BLK, (b + 1) * BLK)
                qb, kc, vc = q_ref[rs, hs], k_ref[rs, hs], v_ref[rs, hs]
                if b == 0:
                    kp, vp, shift = kh_ref[:, hs], vh_ref[:, hs], first_shift
                else:
                    ps = slice((b - 1) * BLK, b * BLK)
                    kp, vp, shift = k_ref[ps, hs], v_ref[ps, hs], 0
                sc = jnp.where(col <= row, _dot_nt(qb, kc) * ATTN_SCALE, NEG)
                sp = jnp.where(col >= row + shift, _dot_nt(qb, kp) * ATTN_SCALE, NEG)
                m = jnp.maximum(jnp.max(sc, axis=1, keepdims=True), jnp.max(sp, axis=1, keepdims=True))
                pc, pp = jnp.exp(sc - m), jnp.exp(sp - m)
                tot = jnp.sum(pc, axis=1, keepdims=True) + jnp.sum(pp, axis=1, keepdims=True)
                o_ref[rs, hs] = (_dot(pc.astype(BF), vc) + _dot(pp.astype(BF), vp)) / tot
                l_ref[rs, hs] = jnp.broadcast_to(m + jnp.log(tot), (BLK, HEAD_DIM))

    def cur(cb):
        return pl.BlockSpec((rows, GROUP_W), lambda i: (i, cb))

    def halo(cb):
        return pl.BlockSpec((BLK, GROUP_W), lambda i: (jnp.maximum(i * nbk - 1, 0), cb))

    (qa, qc), (ka, kc_), (va, vc_) = q, k, v
    return pl.pallas_call(
        kern, name=name, grid=(t // rows,),
        in_specs=[cur(qc), cur(kc_), halo(kc_), cur(vc_), halo(vc_)],
        out_specs=[pl.BlockSpec((rows, GROUP_W), lambda i: (i, 0))] * 2,
        out_shape=[jax.ShapeDtypeStruct((t, GROUP_W), F32)] * 2,
        compiler_params=_params(("parallel",)),
    )(qa, ka, ka, va, va)


def _attn_bwd(name, q, k, v, da, at, lse, t, dil):
    rows, nbk, spc = _attn_geometry(t, dil)
    nblk = t // BLK

    def kern(q_ref, qn_ref, k_ref, kh_ref, v_ref, vh_ref, da_ref, dan_ref, at_ref, atn_ref, ls_ref, lsn_ref,
             dq_ref, dk_ref, dv_ref):
        i = pl.program_id(0)
        first_shift = jnp.where(i % spc == 0, BLK, 0)
        next_shift = jnp.where((i + 1) % spc == 0, BLK, 0)
        row, col = _band_masks()
        for h in range(HEADS_PER_GROUP):
            hs = slice(h * HEAD_DIM, (h + 1) * HEAD_DIM)
            dq = [None] * nbk
            dk = [None] * nbk
            dv = [None] * nbk
            for b in range(nbk + 1):
                rs = slice(b * BLK, (b + 1) * BLK)
                if b < nbk:
                    qb, dab, atb, ls = q_ref[rs, hs], da_ref[rs, hs], at_ref[rs, hs], ls_ref[rs, hs]
                else:
                    qb, dab, atb, ls = qn_ref[:, hs], dan_ref[:, hs], atn_ref[:, hs], lsn_ref[:, hs]
                delta = jnp.sum(dab * atb, axis=1, keepdims=True)
                dab = dab.astype(BF)
                if b < nbk:
                    kc, vc = k_ref[rs, hs], v_ref[rs, hs]
                    p = jnp.where(col <= row, jnp.exp(_dot_nt(qb, kc) * ATTN_SCALE - ls), 0.0)
                    ds = (p * (_dot_nt(dab, vc) - delta) * ATTN_SCALE).astype(BF)
                    dv[b] = _dot_tn(p.astype(BF), dab)
                    dk[b] = _dot_tn(ds, qb)
                    dq[b] = _dot(ds, kc)
                if b == 0:
                    kp, vp, shift = kh_ref[:, hs], vh_ref[:, hs], first_shift
                else:
                    ps = slice((b - 1) * BLK, b * BLK)
                    kp, vp, shift = k_ref[ps, hs], v_ref[ps, hs], (next_shift if b == nbk else 0)
                p = jnp.where(col >= row + shift, jnp.exp(_dot_nt(qb, kp) * ATTN_SCALE - ls), 0.0)
                ds = (p * (_dot_nt(dab, vp) - delta) * ATTN_SCALE).astype(BF)
                if b < nbk:
                    dq[b] = dq[b] + _dot(ds, kp)
                if b >= 1:
                    dv[b - 1] = dv[b - 1] + _dot_tn(p.astype(BF), dab)
                    dk[b - 1] = dk[b - 1] + _dot_tn(ds, qb)
            for b in range(nbk):
                rs = slice(b * BLK, (b + 1) * BLK)
                dq_ref[rs, hs] = dq[b]
                dk_ref[rs, hs] = dk[b]
                dv_ref[rs, hs] = dv[b]

    def cur(cb):
        return pl.BlockSpec((rows, GROUP_W), lambda i: (i, cb))

    def prev(cb):
        return pl.BlockSpec((BLK, GROUP_W), lambda i: (jnp.maximum(i * nbk - 1, 0), cb))

    def nxt(cb):
        return pl.BlockSpec((BLK, GROUP_W), lambda i: (jnp.minimum((i + 1) * nbk, nblk - 1), cb))

    (qa, qc), (ka, kc_), (va, vc_) = q, k, v
    return pl.pallas_call(
        kern, name=name, grid=(t // rows,),
        in_specs=[cur(qc), nxt(qc), cur(kc_), prev(kc_), cur(vc_), prev(vc_),
                  cur(0), nxt(0), cur(0), nxt(0), cur(0), nxt(0)],
        out_specs=[pl.BlockSpec((rows, GROUP_W), lambda i: (i, 0))] * 3,
        out_shape=[jax.ShapeDtypeStruct((t, GROUP_W), F32)] * 3,
        compiler_params=_params(("parallel",)),
    )(qa, qa, ka, ka, va, va, da, da, at, at, lse, lse)


def _to_classes(a, dil):
    if dil == 1:
        return a
    t, w = a.shape
    return a.reshape(t // dil, dil, w).transpose(1, 0, 2).reshape(t, w)


def _from_classes(a, dil):
    if dil == 1:
        return a
    t, w = a.shape
    return a.reshape(dil, t // dil, w).transpose(1, 0, 2).reshape(t, w)


def _cross_probs(qh, kh):
    s = _dot_nt(qh, kh) * CROSS_SCALE
    e = jnp.exp(s - jnp.max(s, axis=1, keepdims=True))
    return e, jnp.sum(e, axis=1, keepdims=True)


def _cross_fwd(cq, ckv, t):
    rows = min(512, t)

    def kern(q_ref, kv_ref, o_ref):
        for h in range(CROSS_HEADS):
            hs = slice(h * CROSS_HEAD_DIM, (h + 1) * CROSS_HEAD_DIM)
            vs = slice(D_MODEL + h * CROSS_HEAD_DIM, D_MODEL + (h + 1) * CROSS_HEAD_DIM)
            e, tot = _cross_probs(q_ref[:, hs], kv_ref[:, hs])
            o_ref[:, hs] = (_dot(e.astype(BF), kv_ref[:, vs]) / tot).astype(BF)

    return pl.pallas_call(
        kern, name="cross_fwd", grid=(t // rows,),
        in_specs=[pl.BlockSpec((rows, D_MODEL), lambda i: (i, 0)), pl.BlockSpec((N_MEM, 2 * D_MODEL), lambda i: (0, 0))],
        out_specs=pl.BlockSpec((rows, D_MODEL), lambda i: (i, 0)),
        out_shape=jax.ShapeDtypeStruct((t, D_MODEL), BF),
        compiler_params=_params(("parallel",)),
    )(cq, ckv)


def _cross_bwd(cq, ckv, dco, t):
    rows = min(512, t)

    def kern(q_ref, kv_ref, do_ref, dq_ref, dkv_ref):
        @pl.when(pl.program_id(0) == 0)
        def _():
            dkv_ref[...] = jnp.zeros_like(dkv_ref)
        for h in range(CROSS_HEADS):
            hs = slice(h * CROSS_HEAD_DIM, (h + 1) * CROSS_HEAD_DIM)
            vs = slice(D_MODEL + h * CROSS_HEAD_DIM, D_MODEL + (h + 1) * CROSS_HEAD_DIM)
            qh, kh, vh, doh = q_ref[:, hs], kv_ref[:, hs], kv_ref[:, vs], do_ref[:, hs]
            e, tot = _cross_probs(qh, kh)
            p = e / tot
            dp = _dot_nt(doh, vh)
            ds = (p * (dp - jnp.sum(p * dp, axis=1, keepdims=True)) * CROSS_SCALE).astype(BF)
            dq_ref[:, hs] = _dot(ds, kh).astype(BF)
            dkv_ref[:, hs] += _dot_tn(ds, qh)
            dkv_ref[:, vs] += _dot_tn(p.astype(BF), doh)

    return pl.pallas_call(
        kern, name="cross_bwd", grid=(t // rows,),
        in_specs=[pl.BlockSpec((rows, D_MODEL), lambda i: (i, 0)), pl.BlockSpec((N_MEM, 2 * D_MODEL), lambda i: (0, 0)),
                  pl.BlockSpec((rows, D_MODEL), lambda i: (i, 0))],
        out_specs=[pl.BlockSpec((rows, D_MODEL), lambda i: (i, 0)), pl.BlockSpec((N_MEM, 2 * D_MODEL), lambda i: (0, 0))],
        out_shape=[jax.ShapeDtypeStruct((t, D_MODEL), BF), jax.ShapeDtypeStruct((N_MEM, 2 * D_MODEL), F32)],
        compiler_params=_params(("arbitrary",)),
    )(cq, ckv, dco)


CONV_TILE = 512
CONV_CHUNK = 128
HALO = 32


def _conv_fwd(z, w32, bias, t):
    tile = min(CONV_TILE, t)
    a_cb, b_cb = GLU_A_COL // LANES, GLU_B_COL // LANES
    hb = tile // HALO

    def kern(a_ref, b_ref, ah_ref, bh_ref, w_ref, bias_ref, o_ref, g_scr):
        i = pl.program_id(1)
        g_scr[HALO:, :] = a_ref[...] * _sig(b_ref[...])
        g_scr[:HALO, :] = ah_ref[...] * _sig(bh_ref[...]) * jnp.where(i > 0, 1.0, 0.0)
        for c in range(tile // CONV_CHUNK):
            acc = jnp.broadcast_to(bias_ref[...], (CONV_CHUNK, LANES))
            for j in range(CONV_K):
                lo = c * CONV_CHUNK + HALO - (CONV_K - 1) + j
                acc = acc + w_ref[j:j + 1, :] * g_scr[lo:lo + CONV_CHUNK, :]
            o_ref[c * CONV_CHUNK:(c + 1) * CONV_CHUNK, :] = acc

    def cur(cb):
        return pl.BlockSpec((tile, LANES), lambda j, i: (i, cb + j))

    def prev(cb):
        return pl.BlockSpec((HALO, LANES), lambda j, i: (jnp.maximum(i * hb - 1, 0), cb + j))

    return pl.pallas_call(
        kern, name="conv_fwd", grid=(CONV_CH // LANES, t // tile),
        in_specs=[cur(a_cb), cur(b_cb), prev(a_cb), prev(b_cb),
                  pl.BlockSpec((CONV_KP, LANES), lambda j, i: (0, j)), pl.BlockSpec((1, LANES), lambda j, i: (0, j))],
        out_specs=pl.BlockSpec((tile, LANES), lambda j, i: (i, j)),
        out_shape=jax.ShapeDtypeStruct((t, CONV_CH), F32),
        scratch_shapes=[pltpu.VMEM((tile + HALO, LANES), F32)],
        compiler_params=_params(("parallel", "parallel")),
    )(z, z, z, z, w32, bias)


def _conv_bwd(z, dc1, w32, t):
    tile = min(CONV_TILE, t)
    a_cb, b_cb = GLU_A_COL // LANES, GLU_B_COL // LANES
    hb = tile // HALO
    n_tiles = t // tile
    n_chunks = tile // CONV_CHUNK

    def kern(a_ref, b_ref, ah_ref, bh_ref, d_ref, dn_ref, w_ref, da_ref, db_ref, dw_ref, g_scr, d_scr):
        i = pl.program_id(1)
        sg = _sig(b_ref[...])
        g_scr[HALO:, :] = a_ref[...] * sg
        g_scr[:HALO, :] = ah_ref[...] * _sig(bh_ref[...]) * jnp.where(i > 0, 1.0, 0.0)
        d_scr[:tile, :] = d_ref[...]
        d_scr[tile:, :] = dn_ref[...] * jnp.where(i < n_tiles - 1, 1.0, 0.0)

        @pl.when(i == 0)
        def _():
            dw_ref[...] = jnp.zeros_like(dw_ref)

        for c in range(n_chunks):
            cs = slice(c * CONV_CHUNK, (c + 1) * CONV_CHUNK)
            acc = jnp.zeros((CONV_CHUNK, LANES), F32)
            for j in range(CONV_K):
                lo = c * CONV_CHUNK + (CONV_K - 1) - j
                acc = acc + w_ref[j:j + 1, :] * d_scr[lo:lo + CONV_CHUNK, :]
            sgc = _sig(b_ref[cs, :])
            da_ref[cs, :] = (acc * sgc).astype(BF)
            db_ref[cs, :] = (acc * a_ref[cs, :] * sgc * (1.0 - sgc)).astype(BF)
        for j in range(CONV_K):
            tot = jnp.zeros((1, LANES), F32)
            for c in range(n_chunks):
                lo = c * CONV_CHUNK + HALO - (CONV_K - 1) + j
                tot = tot + jnp.sum(d_ref[c * CONV_CHUNK:(c + 1) * CONV_CHUNK, :] * g_scr[lo:lo + CONV_CHUNK, :],
                                    axis=0, keepdims=True)
            dw_ref[j:j + 1, :] += tot
        dw_ref[CONV_K:CONV_KP, :] += jnp.sum(d_ref[...], axis=0, keepdims=True)

    def cur(cb):
        return pl.BlockSpec((tile, LANES), lambda j, i: (i, cb + j))

    def prev(cb):
        return pl.BlockSpec((HALO, LANES), lambda j, i: (jnp.maximum(i * hb - 1, 0), cb + j))

    return pl.pallas_call(
        kern, name="conv_bwd", grid=(CONV_CH // LANES, n_tiles),
        in_specs=[cur(a_cb), cur(b_cb), prev(a_cb), prev(b_cb), cur(0),
                  pl.BlockSpec((HALO, LANES), lambda j, i: (jnp.minimum((i + 1) * hb, t // HALO - 1), j)),
                  pl.BlockSpec((CONV_KP, LANES), lambda j, i: (0, j))],
        out_specs=[pl.BlockSpec((tile, LANES), lambda j, i: (i, j)), pl.BlockSpec((tile, LANES), lambda j, i: (i, j)),
                   pl.BlockSpec((CONV_KP, LANES), lambda j, i: (0, j))],
        out_shape=[jax.ShapeDtypeStruct((t, CONV_CH), BF), jax.ShapeDtypeStruct((t, CONV_CH), BF),
                   jax.ShapeDtypeStruct((CONV_KP, CONV_CH), F32)],
        scratch_shapes=[pltpu.VMEM((tile + HALO, LANES), F32), pltpu.VMEM((tile + HALO, LANES), F32)],
        compiler_params=_params(("parallel", "arbitrary")),
    )(z, z, z, z, dc1, dc1, w32)


def _adam_fn(w, g, m, v):
    m = ADAM_B1 * m + (1.0 - ADAM_B1) * g
    v = ADAM_B2 * v + (1.0 - ADAM_B2) * (g * g)
    m_hat = m / (1.0 - ADAM_B1 ** ADAM_STEP)
    v_hat = v / (1.0 - ADAM_B2 ** ADAM_STEP)
    delta = -ADAM_LR * (m_hat / (jnp.sqrt(v_hat) + ADAM_EPS) + ADAM_WD * w)
    return delta, m, v


def _adam(name, w, g, m, v):
    rows, cols = w.shape
    tile = _ew_tile(rows, cols)
    return _rowcall(name, _adam_fn, rows, tile, [(a, cols, 0) for a in (w, g, m, v)], [], [(cols, F32)] * 3)


def _place():
    x, y, c = lax.axis_index("x"), lax.axis_index("y"), lax.axis_index("c")
    chips = [(1 - x, y), (x, 1 - y), (1 - x, 1 - y)]
    return x, y, c, chips


def _gather_weights(shards):
    nw = len(shards)
    chip = 2 * lax.axis_index("x") + lax.axis_index("y")
    staged = [lax.dynamic_update_index_in_dim(jnp.zeros((4,) + s.shape, s.dtype), s, chip, 0) for s in shards]

    def body(*refs):
        outs = refs[nw:2 * nw]
        send_sems, recv_sems = refs[2 * nw:]
        x, y, c, chips = _place()
        me = 2 * x + y
        sibling = (x, y, 1 - c)

        def half(w, lead, h):
            n = shards[w].shape[0] // 2
            return outs[w].at[lead, pl.ds(h * n, n)]

        def copy(w, k, part, to):
            return pltpu.make_async_remote_copy(src_ref=part, dst_ref=part, send_sem=send_sems.at[w, k],
                                                recv_sem=recv_sems.at[w, k], device_id=to, device_id_type=MESH)

        first, passed = [], []
        for w in range(nw):
            for k, (px, py) in enumerate(chips):
                first.append(copy(w, k, half(w, me, c), (px, py, c)))
                first[-1].start()
        for w in range(nw):
            for k, (px, py) in enumerate(chips):
                landed = half(w, 2 * px + py, c)
                copy(w, k, landed, (px, py, c)).wait_recv()
                passed.append(copy(w, 3 + k, landed, sibling))
                passed[-1].start()
        for w in range(nw):
            for k, (px, py) in enumerate(chips):
                copy(w, 3 + k, half(w, 2 * px + py, 1 - c), sibling).wait_recv()
        for cp in first + passed:
            cp.wait_send()

    return pl.pallas_call(
        body, name="gather_weights",
        in_specs=[ANY] * nw, out_specs=[ANY] * nw,
        out_shape=[jax.ShapeDtypeStruct((4,) + s.shape, s.dtype) for s in shards],
        input_output_aliases={w: w for w in range(nw)},
        scratch_shapes=[pltpu.SemaphoreType.DMA((nw, 6)), pltpu.SemaphoreType.DMA((nw, 6))],
    )(*staged)


def _gather8(name, v):
    rows = v.shape[0]

    def body(v_ref, all_ref, sum_ref, send_sems, recv_sems):
        x, y, c, _ = _place()
        me = 4 * x + 2 * y + c
        all_ref[me] = v_ref[...]
        copies = []
        for k in range(1, 8):
            px, py, pc = x ^ (k >> 2), y ^ ((k >> 1) & 1), c ^ (k & 1)
            copies.append(pltpu.make_async_remote_copy(
                src_ref=v_ref, dst_ref=all_ref.at[me], send_sem=send_sems.at[k - 1], recv_sem=recv_sems.at[k - 1],
                device_id=(px, py, pc), device_id_type=MESH))
            copies[-1].start()
        for k in range(1, 8):
            px, py, pc = x ^ (k >> 2), y ^ ((k >> 1) & 1), c ^ (k & 1)
            theirs = all_ref.at[4 * px + 2 * py + pc]
            pltpu.make_async_remote_copy(
                src_ref=theirs, dst_ref=theirs, send_sem=send_sems.at[k - 1], recv_sem=recv_sems.at[k - 1],
                device_id=(px, py, pc), device_id_type=MESH).wait_recv()
        for cp in copies:
            cp.wait_send()
        tot = all_ref[0]
        for d in range(1, 8):
            tot = tot + all_ref[d]
        sum_ref[...] = tot

    vm = pl.BlockSpec(memory_space=pltpu.VMEM)
    return pl.pallas_call(
        body, name=name, in_specs=[vm], out_specs=[vm, vm],
        out_shape=[jax.ShapeDtypeStruct((8, rows, LANES), F32), jax.ShapeDtypeStruct((rows, LANES), F32)],
        scratch_shapes=[pltpu.SemaphoreType.DMA((7,)), pltpu.SemaphoreType.DMA((7,))],
    )(v)


def _region(ref, col_sharded, shape, j, h):
    r, ccols = shape
    if col_sharded:
        return ref.at[pl.ds(h * (r // 2), r // 2), pl.ds(j * (ccols // 4), ccols // 4)]
    n = r // 8
    return ref.at[pl.ds((2 * j + h) * n, n), :]


def _region_shape(col_sharded, shape):
    r, ccols = shape
    return (r // 2, ccols // 4) if col_sharded else (r // 8, ccols)


def _swap_halves(grads, kinds):
    nw = len(grads)

    def body(*refs):
        ins, theirs = refs[:nw], refs[nw:2 * nw]
        send_sems, recv_sems = refs[2 * nw:]
        x, y, c, _ = _place()
        sent = []
        for w in range(nw):
            for j in range(4):
                sent.append(pltpu.make_async_remote_copy(
                    src_ref=_region(ins[w], kinds[w], grads[w].shape, j, 1 - c), dst_ref=theirs[w].at[j],
                    send_sem=send_sems.at[w, j], recv_sem=recv_sems.at[w, j], device_id=(x, y, 1 - c), device_id_type=MESH))
                sent[-1].start()
        for cp in sent:
            cp.wait()

    return pl.pallas_call(
        body, name="grad_swap_halves", in_specs=[ANY] * nw, out_specs=[ANY] * nw,
        out_shape=[jax.ShapeDtypeStruct((4,) + _region_shape(kinds[w], grads[w].shape), F32) for w in range(nw)],
        scratch_shapes=[pltpu.SemaphoreType.DMA((nw, 4)), pltpu.SemaphoreType.DMA((nw, 4))],
    )(*grads)


def _kept_halves(grad, col_sharded, c):
    r, ccols = grad.shape
    if col_sharded:
        slab = lax.dynamic_slice_in_dim(grad, c * (r // 2), r // 2, axis=0)
        return slab.reshape(r // 2, 4, ccols // 4).transpose(1, 0, 2)
    return lax.dynamic_index_in_dim(grad.reshape(4, 2, r // 8, ccols), c, axis=1, keepdims=False)


def _scatter_to_owners(parts):
    nw = len(parts)

    def body(*refs):
        ins, outs = refs[:nw], refs[nw:2 * nw]
        send_sems, recv_sems = refs[2 * nw:]
        x, y, c, chips = _place()
        sent = []
        for w in range(nw):
            for k, (px, py) in enumerate(chips):
                sent.append(pltpu.make_async_remote_copy(
                    src_ref=ins[w].at[2 * px + py], dst_ref=outs[w].at[k], send_sem=send_sems.at[w, k],
                    recv_sem=recv_sems.at[w, k], device_id=(px, py, c), device_id_type=MESH))
                sent[-1].start()
        for cp in sent:
            cp.wait()

    return pl.pallas_call(
        body, name="grad_scatter", in_specs=[ANY] * nw, out_specs=[ANY] * nw,
        out_shape=[jax.ShapeDtypeStruct((3,) + p.shape[1:], p.dtype) for p in parts],
        scratch_shapes=[pltpu.SemaphoreType.DMA((nw, 3)), pltpu.SemaphoreType.DMA((nw, 3))],
    )(*parts)


def _share_halves(halves):
    nw = len(halves)

    def body(*refs):
        ins, outs = refs[:nw], refs[nw:2 * nw]
        send_sems, recv_sems = refs[2 * nw:]
        x, y, c, _ = _place()
        sent = []
        for w in range(nw):
            sent.append(pltpu.make_async_remote_copy(
                src_ref=ins[w], dst_ref=outs[w], send_sem=send_sems.at[w], recv_sem=recv_sems.at[w],
                device_id=(x, y, 1 - c), device_id_type=MESH))
            sent[-1].start()
        for cp in sent:
            cp.wait()

    return pl.pallas_call(
        body, name="grad_share_halves", in_specs=[ANY] * nw, out_specs=[ANY] * nw,
        out_shape=[jax.ShapeDtypeStruct(h.shape, F32) for h in halves],
        scratch_shapes=[pltpu.SemaphoreType.DMA((nw,)), pltpu.SemaphoreType.DMA((nw,))],
    )(*halves)


def _ew_tile(rows, cols):
    limit = max(8, (256 * 1024) // cols)
    return max(d for d in range(8, min(rows, limit) + 1, 8) if rows % d == 0)


def _reduce_grads(grads, kinds):
    nw = len(grads)
    c = lax.axis_index("c")
    chip = 2 * lax.axis_index("x") + lax.axis_index("y")
    theirs = _swap_halves(grads, kinds)
    parts, parts_bf = [], []
    for w in range(nw):
        _, rr, cc = theirs[w].shape
        a = _kept_halves(grads[w], kinds[w], c).reshape(4 * rr, cc)
        p, pb = _rowcall(f"grad_pair_sum_{w}", lambda u, v: (u + v, u + v), 4 * rr, _ew_tile(4 * rr, cc),
                         [(a, cc, 0), (theirs[w].reshape(4 * rr, cc), cc, 0)], [], [(cc, F32), (cc, BF)])
        parts.append(p.reshape(4, rr, cc))
        parts_bf.append(pb.reshape(4, rr, cc))
    landed = _scatter_to_owners(parts_bf)
    halves = []
    for w in range(nw):
        _, rr, cc = landed[w].shape
        tile = _ew_tile(rr, cc)
        own = lax.dynamic_index_in_dim(parts[w], chip, axis=0, keepdims=False)
        flat = landed[w].reshape(3 * rr, cc)
        nb = rr // tile
        (f,) = _rowcall(f"grad_chip_sum_{w}", lambda own, k0, k1, k2: (((own + k0) + k1) + k2,), rr, tile,
                        [(own, cc, 0), (flat, cc, 0, 0), (flat, cc, 0, nb), (flat, cc, 0, 2 * nb)], [], [(cc, F32)])
        halves.append(f)
    other = _share_halves(halves)
    return [jnp.where(c == 0, jnp.concatenate([f, o], axis=0), jnp.concatenate([o, f], axis=0))
            for f, o in zip(halves, other)]


BIG = ("w_in", "w_attn_proj", "w_conv_proj", "w_out", "w_cq", "w_ckv", "w_co", "w_up", "w_down")
COL_SHARDED = {"w_in": True, "w_attn_proj": True, "w_conv_proj": True, "w_out": False, "w_cq": False,
               "w_ckv": True, "w_co": False, "w_up": True, "w_down": False}
SMALL = ("g_mix", "b_gate", "conv_b", "conv_ln_g", "conv_ln_b", "g_cross", "g_mem", "g_mlp", "g_final")
ORDER = ("g_mix", "w_in", "b_gate", "conv_w", "conv_b", "conv_ln_g", "conv_ln_b", "w_attn_proj", "w_conv_proj", "w_out",
         "g_cross", "g_mem", "w_cq", "w_ckv", "w_co", "g_mlp", "w_up", "w_down", "g_final")


def _pad_rows(flat, rows):
    return jnp.pad(flat, (0, rows * LANES - flat.shape[0])).reshape(rows, LANES)


def _local_step(x, mem, tgt, wfull, small, conv_w_full):
    t = x.shape[0]
    tr = 256
    c_tab, s_tab = _rope_tables(t)
    row = lambda v: v.reshape(1, -1)
    g_mix, g_cross, g_mem, g_mlp, g_final = (row(small[n]) for n in ("g_mix", "g_cross", "g_mem", "g_mlp", "g_final"))
    b_gate, conv_b, ln_g, ln_b = (row(small[n]) for n in ("b_gate", "conv_b", "conv_ln_g", "conv_ln_b"))
    w32 = jnp.pad(conv_w_full, ((0, CONV_KP - CONV_K), (0, 0)))

    (u,) = _rowcall("mix_norm", lambda a, g: (_rms(a, g),), t, tr, [(x, D_MODEL, 0)], [g_mix], [(D_MODEL, BF)])
    z = _mm("in_proj", u, wfull["w_in"])
    qr, kr, vb = _rowcall("rope", _rope_fwd_fn, t, tr,
                          [(z, ATTN_WIDTH, 0), (z, ATTN_WIDTH, 1), (z, ATTN_WIDTH, 2), (c_tab, HEAD_DIM, 0), (s_tab, HEAD_DIM, 0)],
                          [], [(ATTN_WIDTH, BF)] * 3)
    qkv_cls, outs, lses = [], [], []
    for g, dil in enumerate(DILATIONS):
        if dil == 1:
            ops = ((qr, g), (kr, g), (vb, g))
        else:
            gs = slice(g * GROUP_W, (g + 1) * GROUP_W)
            ops = tuple((_to_classes(a[:, gs], dil), 0) for a in (qr, kr, vb))
        qkv_cls.append(ops)
        o_g, l_g = _attn_fwd(f"attn_fwd_{g}", *ops, t, dil)
        outs.append(_from_classes(o_g, dil))
        lses.append(_from_classes(l_g, dil))
    attn, lse = _rowcall("attn_merge", _merge_fn, t, tr, [(a, GROUP_W, 0) for a in outs + lses], [],
                         [(GROUP_W, F32), (GROUP_W, F32)])
    y_attn = _mm("attn_proj", attn, wfull["w_attn_proj"])
    c1 = _conv_fwd(z, w32, conv_b, t)
    (c2,) = _rowcall("conv_ln_silu", _ln_silu_fn, t, tr, [(c1, CONV_CH, 0)], [ln_g, ln_b], [(CONV_CH, BF)])
    y_conv = _mm("conv_proj", c2, wfull["w_conv_proj"])
    gate_ins = [(z, D_MODEL, GATE_A_COL // D_MODEL), (z, D_MODEL, GATE_B_COL // D_MODEL)]
    (merged,) = _rowcall("gate", _gate_fn, t, tr, [(y_attn, D_MODEL, 0), (y_conv, D_MODEL, 0)] + gate_ins, [b_gate],
                         [(D_MODEL, BF)])
    add = lambda acc, res: (res + acc,)
    x1 = _mm("out_proj", merged, wfull["w_out"], extras=(x,), epi=add)

    (uq,) = _rowcall("cross_norm", lambda a, g: (_rms(a, g),), t, tr, [(x1, D_MODEL, 0)], [g_cross], [(D_MODEL, BF)])
    (mn,) = _rowcall("mem_norm", lambda a, g: (_rms(a, g),), N_MEM, N_MEM, [(mem, D_MODEL, 0)], [g_mem], [(D_MODEL, BF)])
    cq = _mm("cross_q", uq, wfull["w_cq"], out_dtypes=(BF,))
    ckv = _mm("cross_kv", mn, wfull["w_ckv"], out_dtypes=(BF,))
    co = _cross_fwd(cq, ckv, t)
    x2 = _mm("cross_out", co, wfull["w_co"], extras=(x1,), epi=add)

    (um,) = _rowcall("mlp_norm", lambda a, g: (_rms(a, g),), t, tr, [(x2, D_MODEL, 0)], [g_mlp], [(D_MODEL, BF)])
    hpre, hact = _mm("mlp_up", um, wfull["w_up"], out_dtypes=(F32, BF),
                     epi=lambda acc: (acc, jnp.square(jnp.maximum(acc, 0.0))))
    x3 = _mm("mlp_down", hact, wfull["w_down"], extras=(x2,), epi=add)
    d3, loss_row, dg_final = _rowcall("final_norm_loss", _final_fn, t, tr, [(x3, D_MODEL, 0), (tgt, D_MODEL, 0)], [g_final],
                                      [(D_MODEL, F32)], accs=[(1, LANES), (1, D_MODEL)])

    gw = {}
    dhp = _mm("mlp_down_bwd", d3, wfull["w_down"], nt=True, extras=(hpre,), out_dtypes=(BF,),
              epi=lambda acc, hp: (acc * 2.0 * jnp.maximum(hp, 0.0),))
    gw["w_down"] = _mm_tn("mlp_down_wgrad", hact, d3)
    dum = _mm("mlp_up_bwd", dhp, wfull["w_up"], nt=True)
    gw["w_up"] = _mm_tn("mlp_up_wgrad", um, dhp)

    def norm_bwd(a, dn, dres, g):
        dx, dg = _rms_bwd(a, g, dn)
        return dres + dx, dg

    d2, dg_mlp = _rowcall("mlp_norm_bwd", norm_bwd, t, tr, [(x2, D_MODEL, 0), (dum, D_MODEL, 0), (d3, D_MODEL, 0)], [g_mlp],
                          [(D_MODEL, F32)], accs=[(1, D_MODEL)])

    dco = _mm("cross_out_bwd", d2, wfull["w_co"], nt=True, out_dtypes=(BF,))
    gw["w_co"] = _mm_tn("cross_out_wgrad", co, d2)
    dcq, dckv = _cross_bwd(cq, ckv, dco, t)
    gw["w_cq"] = _mm_tn("cross_q_wgrad", uq, dcq)
    duq = _mm("cross_q_bwd", dcq, wfull["w_cq"], nt=True)
    d1, dg_cross = _rowcall("cross_norm_bwd", norm_bwd, t, tr, [(x1, D_MODEL, 0), (duq, D_MODEL, 0), (d2, D_MODEL, 0)],
                            [g_cross], [(D_MODEL, F32)], accs=[(1, D_MODEL)])
    gw["w_ckv"] = _mm_tn("cross_kv_wgrad", mn, dckv, tk=N_MEM)
    dmn = _mm("cross_kv_bwd", dckv, wfull["w_ckv"], nt=True)
    (dg_mem,) = _rowcall("mem_norm_bwd", lambda a, dn, g: (_rms_bwd(a, g, dn)[1],), N_MEM, N_MEM,
                         [(mem, D_MODEL, 0), (dmn, D_MODEL, 0)], [g_mem], [], accs=[(1, D_MODEL)])

    dmerged = _mm("out_proj_bwd", d1, wfull["w_out"], nt=True)
    gw["w_out"] = _mm_tn("out_proj_wgrad", merged, d1)
    dya, dyc, dgate, dbg = _rowcall("gate_bwd", _gate_bwd_fn, t, tr,
                                    [(dmerged, D_MODEL, 0), (y_attn, D_MODEL, 0), (y_conv, D_MODEL, 0)] + gate_ins, [b_gate],
                                    [(D_MODEL, BF), (D_MODEL, BF), (2 * D_MODEL, BF)], accs=[(1, 2 * D_MODEL)])
    gw["w_attn_proj"] = _mm_tn("attn_proj_wgrad", attn, dya)
    dattn = _mm("attn_proj_bwd", dya, wfull["w_attn_proj"], nt=True)
    gw["w_conv_proj"] = _mm_tn("conv_proj_wgrad", c2, dyc)
    dc2 = _mm("conv_proj_bwd", dyc, wfull["w_conv_proj"], nt=True)
    dc1, dlng, dlnb = _rowcall("conv_ln_silu_bwd", _ln_silu_bwd_fn, t, tr, [(c1, CONV_CH, 0), (dc2, CONV_CH, 0)], [ln_g, ln_b],
                               [(CONV_CH, F32)], accs=[(1, CONV_CH), (1, CONV_CH)])
    dglu_a, dglu_b, dconv = _conv_bwd(z, dc1, w32, t)

    dqs, dks, dvs = [], [], []
    for g, dil in enumerate(DILATIONS):
        da_c, at_c, ls_c = (_to_classes(a, dil) for a in (dattn, attn, lse))
        dq_g, dk_g, dv_g = _attn_bwd(f"attn_bwd_{g}", *qkv_cls[g], da_c, at_c, ls_c, t, dil)
        dqs.append(_from_classes(dq_g, dil))
        dks.append(_from_classes(dk_g, dil))
        dvs.append(_from_classes(dv_g, dil))
    (dz,) = _rowcall("rope_bwd", _rope_bwd_fn, t, tr,
                     [(a, GROUP_W, 0) for a in dqs + dks + dvs] + [(c_tab, HEAD_DIM, 0), (s_tab, HEAD_DIM, 0)]
                     + [(dglu_a, CONV_CH, 0), (dglu_b, CONV_CH, 0), (dgate, 2 * D_MODEL, 0)], [], [(IN_WIDTH, BF)])
    du = _mm("in_proj_bwd", dz, wfull["w_in"], nt=True)
    gw["w_in"] = _mm_tn("in_proj_wgrad", u, dz)
    gx, dg_mix = _rowcall("mix_norm_bwd", norm_bwd, t, tr, [(x, D_MODEL, 0), (du, D_MODEL, 0), (d1, D_MODEL, 0)], [g_mix],
                          [(D_MODEL, F32)], accs=[(1, D_MODEL)])

    gsmall = {"g_mix": dg_mix, "b_gate": dbg, "conv_b": dconv[CONV_K:CONV_K + 1], "conv_ln_g": dlng, "conv_ln_b": dlnb,
              "g_cross": dg_cross, "g_mem": dg_mem, "g_mlp": dg_mlp, "g_final": dg_final, "conv_w": dconv[:CONV_K]}
    return loss_row, gx, gw, gsmall


def kernel(x, mem, g_mix, w_in, b_gate, conv_w, conv_b, conv_ln_g, conv_ln_b, w_attn_proj, w_conv_proj, w_out, g_cross, g_mem, w_cq, w_ckv, w_co, g_mlp, w_up, w_down, g_final, loss_target, m_g_mix, m_w_in, m_b_gate, m_conv_w, m_conv_b, m_conv_ln_g, m_conv_ln_b, m_w_attn_proj, m_w_conv_proj, m_w_out, m_g_cross, m_g_mem, m_w_cq, m_w_ckv, m_w_co, m_g_mlp, m_w_up, m_w_down, m_g_final, v_g_mix, v_w_in, v_b_gate, v_conv_w, v_conv_b, v_conv_ln_g, v_conv_ln_b, v_w_attn_proj, v_w_conv_proj, v_w_out, v_g_cross, v_g_mem, v_w_cq, v_w_ckv, v_w_co, v_g_mlp, v_w_up, v_w_down, v_g_final):
    w = dict(g_mix=g_mix, w_in=w_in, b_gate=b_gate, conv_w=conv_w, conv_b=conv_b, conv_ln_g=conv_ln_g, conv_ln_b=conv_ln_b,
             w_attn_proj=w_attn_proj, w_conv_proj=w_conv_proj, w_out=w_out, g_cross=g_cross, g_mem=g_mem, w_cq=w_cq,
             w_ckv=w_ckv, w_co=w_co, g_mlp=g_mlp, w_up=w_up, w_down=w_down, g_final=g_final)
    mo = dict(g_mix=m_g_mix, w_in=m_w_in, b_gate=m_b_gate, conv_w=m_conv_w, conv_b=m_conv_b, conv_ln_g=m_conv_ln_g,
              conv_ln_b=m_conv_ln_b, w_attn_proj=m_w_attn_proj, w_conv_proj=m_w_conv_proj, w_out=m_w_out, g_cross=m_g_cross,
              g_mem=m_g_mem, w_cq=m_w_cq, w_ckv=m_w_ckv, w_co=m_w_co, g_mlp=m_g_mlp, w_up=m_w_up, w_down=m_w_down,
              g_final=m_g_final)
    vo = dict(g_mix=v_g_mix, w_in=v_w_in, b_gate=v_b_gate, conv_w=v_conv_w, conv_b=v_conv_b, conv_ln_g=v_conv_ln_g,
              conv_ln_b=v_conv_ln_b, w_attn_proj=v_w_attn_proj, w_conv_proj=v_w_conv_proj, w_out=v_w_out, g_cross=v_g_cross,
              g_mem=v_g_mem, w_cq=v_w_cq, w_ckv=v_w_ckv, w_co=v_w_co, g_mlp=v_g_mlp, w_up=v_w_up, w_down=v_w_down,
              g_final=v_g_final)
    shapes = {n: w[n].shape for n in ORDER}
    two_d = lambda a: a.reshape(a.shape[-2], a.shape[-1])
    chip = 2 * lax.axis_index("x") + lax.axis_index("y")

    shards = [two_d(w[n]).astype(BF) for n in BIG]
    gathered = _gather_weights(shards)
    wfull = {}
    for n, g in zip(BIG, gathered):
        wfull[n] = g if COL_SHARDED[n] else g.reshape(1, 4 * g.shape[1], g.shape[2])
    cw_rows = 48
    cw_all, _ = _gather8("gather_conv_w", _pad_rows(conv_w.reshape(-1), cw_rows))
    cw_shard = CONV_K * (CONV_CH // 4)
    conv_w_full = jnp.concatenate(
        [cw_all[2 * j].reshape(-1)[:cw_shard].reshape(CONV_K, CONV_CH // 4) for j in range(4)], axis=1)

    small = {n: w[n] for n in SMALL}
    loss_row, gx, gw, gsmall = _local_step(two_d(x), two_d(mem), two_d(loss_target), wfull, small, conv_w_full)
    loss = lax.psum(loss_row[0, 0], ("x", "y", "c"))

    gshard = dict(zip(BIG, _reduce_grads([gw[n] for n in BIG], [COL_SHARDED[n] for n in BIG])))
    small_names = SMALL + ("conv_w",)
    flat = jnp.concatenate([gsmall[n].reshape(-1) for n in small_names])
    sm_rows = -(-flat.shape[0] // (8 * LANES)) * 8
    _, sm_sum = _gather8("reduce_small_grads", _pad_rows(flat, sm_rows))
    sm_sum = sm_sum.reshape(-1)
    off = 0
    for n in small_names:
        size = gsmall[n].size
        gshard[n] = sm_sum[off:off + size].reshape(gsmall[n].shape)
        off += size
    gshard["conv_w"] = lax.dynamic_slice_in_dim(gshard["conv_w"], chip * (CONV_CH // 4), CONV_CH // 4, axis=1)

    grads, deltas, new_m, new_v = {}, {}, {}, {}
    for n in BIG:
        d, m2, v2 = _adam(f"adamw_{n}", two_d(w[n]), gshard[n], two_d(mo[n]), two_d(vo[n]))
        grads[n], deltas[n], new_m[n], new_v[n] = (a.reshape(shapes[n]) for a in (gshard[n], d, m2, v2))
    pack = lambda src: jnp.concatenate([src[n].reshape(-1) for n in small_names])
    n_small = sum(w[n].size for n in small_names)
    ad_rows = -(-n_small // (8 * LANES)) * 8
    d, m2, v2 = _adam("adamw_small", *[_pad_rows(pack(src), ad_rows) for src in (w, gshard, mo, vo)])
    off = 0
    for n in small_names:
        size = w[n].size
        grads[n] = gshard[n].reshape(shapes[n])
        deltas[n], new_m[n], new_v[n] = (a.reshape(-1)[off:off + size].reshape(shapes[n]) for a in (d, m2, v2))
        off += size

    return (loss, gx.reshape(x.shape), *[grads[n] for n in ORDER], *[deltas[n] for n in ORDER],
            *[new_m[n] for n in ORDER], *[new_v[n] for n in ORDER])
```

```python
import functools

import jax
import jax.numpy as jnp
from jax import lax
from jax.experimental import pallas as pl
from jax.experimental.pallas import tpu as pltpu

F32 = jnp.float32
BF = jnp.bfloat16

D_MODEL = 1024
N_MEM = 256
HEAD_DIM = 128
HEADS_PER_GROUP = 4
DILATIONS = (1, 4, 16)
BLK = 128
GROUP_W = HEADS_PER_GROUP * HEAD_DIM
ATTN_WIDTH = 3 * GROUP_W
ROT_DIM = 32
ROPE_THETA = 500000.0
CONV_CH = 768
CONV_K = 31
CONV_KP = 32
IN_WIDTH = 8192
CROSS_HEADS = 4
CROSS_HEAD_DIM = 256
D_FF = 4096
EPS = 1e-6
ATTN_SCALE = HEAD_DIM ** -0.5
CROSS_SCALE = CROSS_HEAD_DIM ** -0.5
NEG = -1e30

ADAM_LR = 0.001
ADAM_B1 = 0.9
ADAM_B2 = 0.999
ADAM_EPS = 1e-08
ADAM_WD = 0.01
ADAM_STEP = 10

LANES = 128
VMEM_LIMIT = 56 * 1024 * 1024
MESH = pl.DeviceIdType.MESH
ANY = pl.BlockSpec(memory_space=pl.ANY)

GLU_A_COL = 3 * ATTN_WIDTH
GLU_B_COL = GLU_A_COL + CONV_CH
GATE_A_COL = GLU_B_COL + CONV_CH
GATE_B_COL = GATE_A_COL + D_MODEL


def _params(sem=None):
    return pltpu.CompilerParams(dimension_semantics=sem, vmem_limit_bytes=VMEM_LIMIT)


def _dot(a, b):
    return lax.dot_general(a, b, (((1,), (0,)), ((), ())), preferred_element_type=F32)


def _dot_nt(a, b):
    return lax.dot_general(a, b, (((1,), (1,)), ((), ())), preferred_element_type=F32)


def _dot_tn(a, b):
    return lax.dot_general(a, b, (((0,), (0,)), ((), ())), preferred_element_type=F32)


def _sig(x):
    return 1.0 / (1.0 + jnp.exp(-x))


class _Side:
    def __init__(self, arrays, out_shapes, n_sems, build, aliases=None):
        self.arrays, self.out_shapes, self.n_sems, self.build = list(arrays), list(out_shapes), n_sems, build
        self.aliases = aliases or {}


def _pcall(name, kern, grid, in_specs, out_specs, out_shape, scratch_shapes, sem, args, side=None):
    in_specs, out_specs, out_shape, scratch_shapes = list(in_specs), list(out_specs), list(out_shape), list(scratch_shapes)
    if side is None:
        return pl.pallas_call(kern, name=name, grid=grid, in_specs=in_specs, out_specs=out_specs, out_shape=out_shape,
                              scratch_shapes=scratch_shapes, compiler_params=_params(sem))(*args)
    ni, no, nsc = len(in_specs), len(out_specs), len(scratch_shapes)
    nsi, nso = len(side.arrays), len(side.out_shapes)

    def wrapped(*refs):
        ins, side_ins = refs[:ni], refs[ni:ni + nsi]
        outs, side_outs = refs[ni + nsi:ni + nsi + no], refs[ni + nsi + no:ni + nsi + no + nso]
        scratch = refs[ni + nsi + no + nso:ni + nsi + no + nso + nsc]
        send_sems, recv_sems = refs[-2:]
        start, finish = side.build(side_ins, side_outs, send_sems, recv_sems)
        if grid:
            first = functools.reduce(jnp.logical_and, [pl.program_id(a) == 0 for a in range(len(grid))])
            last = functools.reduce(jnp.logical_and, [pl.program_id(a) == g - 1 for a, g in enumerate(grid)])
            pl.when(first)(start)
            kern(*ins, *outs, *scratch)
            pl.when(last)(finish)
        else:
            start()
            kern(*ins, *outs, *scratch)
            finish()

    res = pl.pallas_call(
        wrapped, name=name, grid=grid, in_specs=in_specs + [ANY] * nsi, out_specs=out_specs + [ANY] * nso,
        out_shape=out_shape + side.out_shapes,
        scratch_shapes=scratch_shapes + [pltpu.SemaphoreType.DMA((side.n_sems,)), pltpu.SemaphoreType.DMA((side.n_sems,))],
        input_output_aliases={ni + k: no + v for k, v in side.aliases.items()},
        compiler_params=_params(("arbitrary",) * len(grid) if grid else None),
    )(*args, *side.arrays)
    return res[:no], res[no:]


def _comm_only(name, side):
    return _pcall(name, lambda: None, (), [], [], [], [], None, [], side=side)[1]


def _rowcall(name, fn, n_rows, tile, ins, params, outs, accs=(), side=None):
    tile = min(tile, n_rows)
    ni, npar, no, na = len(ins), len(params), len(outs), len(accs)

    def kern(*refs):
        in_refs = refs[:ni + npar]
        o_refs = refs[ni + npar:ni + npar + no]
        a_refs = refs[ni + npar + no:]
        vals = fn(*[r[...] for r in in_refs])
        for r, v in zip(o_refs, vals[:no]):
            r[...] = v.astype(r.dtype)
        if na:
            @pl.when(pl.program_id(0) == 0)
            def _():
                for r in a_refs:
                    r[...] = jnp.zeros_like(r)
            for r, v in zip(a_refs, vals[no:]):
                r[...] += v

    in_specs = []
    arrays = []
    for spec in ins:
        arr, width, cb = spec[0], spec[1], spec[2]
        rb = spec[3] if len(spec) > 3 else 0
        in_specs.append(pl.BlockSpec((tile, width), functools.partial(lambda i, cb, rb: (i + rb, cb), cb=cb, rb=rb)))
        arrays.append(arr)
    for p in params:
        in_specs.append(pl.BlockSpec(p.shape, lambda i: (0, 0)))
        arrays.append(p)
    out_specs = [pl.BlockSpec((tile, w), lambda i: (i, 0)) for w, _ in outs]
    out_specs += [pl.BlockSpec(s, lambda i: (0, 0)) for s in accs]
    out_shape = [jax.ShapeDtypeStruct((n_rows, w), dt) for w, dt in outs]
    out_shape += [jax.ShapeDtypeStruct(s, F32) for s in accs]
    return _pcall(name, kern, (n_rows // tile,), in_specs, out_specs, out_shape, [],
                  ("arbitrary",) if na else ("parallel",), arrays, side)


def _mm(name, a, w3, *, nt=False, extras=(), epi=None, out_dtypes=(F32,), tm=None, tn=None, tk=None, k_part=(0, 1),
        side=None):
    m, ka = a.shape
    ns, r, cs = w3.shape
    if not nt:
        k_dim, n = r, ns * cs
        tn = tn or min(cs, 1024)
        tk = tk or min(k_dim, 1024)
    else:
        k_dim, n = ns * cs, r
        tn = tn or min(r, 1024)
        tk = tk or min(cs, 1024)
    assert ka == k_dim, (name, a.shape, w3.shape)
    nk = k_dim // tk // k_part[1]
    k0 = k_part[0] * nk
    if not nt:
        nbs = cs // tn
        w_spec = pl.BlockSpec((None, tk, tn), lambda i, j, k: (j // nbs, k + k0, j % nbs))
    else:
        kbs = cs // tk
        w_spec = pl.BlockSpec((None, tn, tk), lambda i, j, k: ((k + k0) // kbs, j, (k + k0) % kbs))
    tm = tm or min(m, 1024)
    ne, no = len(extras), len(out_dtypes)

    def kern(a_ref, w_ref, *rest):
        e_refs = rest[:ne]
        o_refs = rest[ne:ne + no]
        acc = rest[ne + no]
        k = pl.program_id(2)

        @pl.when(k == 0)
        def _():
            acc[...] = jnp.zeros_like(acc)

        av = a_ref[...].astype(BF)
        acc[...] += _dot_nt(av, w_ref[...]) if nt else _dot(av, w_ref[...])

        @pl.when(k == nk - 1)
        def _():
            res = acc[...]
            vals = epi(res, *[e[...] for e in e_refs]) if epi else (res,)
            for o, v in zip(o_refs, vals):
                o[...] = v.astype(o.dtype)

    in_specs = [pl.BlockSpec((tm, tk), lambda i, j, k: (i, k + k0)), w_spec]
    in_specs += [pl.BlockSpec((tm, tn), lambda i, j, k: (i, j)) for _ in extras]
    res = _pcall(name, kern, (m // tm, n // tn, nk), in_specs,
                 [pl.BlockSpec((tm, tn), lambda i, j, k: (i, j)) for _ in out_dtypes],
                 [jax.ShapeDtypeStruct((m, n), dt) for dt in out_dtypes], [pltpu.VMEM((tm, tn), F32)],
                 ("parallel", "parallel", "arbitrary"), (a, w3, *extras), side)
    if side is not None:
        return (res[0][0] if no == 1 else res[0]), res[1]
    return res[0] if no == 1 else res


def _mm_tn(name, a, b, tm=None, tn=None, tk=None):
    t, ka = a.shape
    _, n = b.shape
    tm = tm or min(ka, 1024)
    tn = tn or min(n, 1024)
    tk = tk or min(t, 512)

    def kern(a_ref, b_ref, o_ref):
        @pl.when(pl.program_id(2) == 0)
        def _():
            o_ref[...] = jnp.zeros_like(o_ref)
        o_ref[...] += _dot_tn(a_ref[...].astype(BF), b_ref[...].astype(BF))

    return pl.pallas_call(
        kern, name=name, grid=(ka // tm, n // tn, t // tk),
        in_specs=[pl.BlockSpec((tk, tm), lambda i, j, k: (k, i)), pl.BlockSpec((tk, tn), lambda i, j, k: (k, j))],
        out_specs=pl.BlockSpec((tm, tn), lambda i, j, k: (i, j)),
        out_shape=jax.ShapeDtypeStruct((ka, n), F32),
        compiler_params=_params(("parallel", "parallel", "arbitrary")),
    )(a, b)


def _rms(x, g):
    return x * lax.rsqrt(jnp.mean(x * x, axis=-1, keepdims=True) + EPS) * g


def _rms_bwd(x, g, dy):
    r = lax.rsqrt(jnp.mean(x * x, axis=-1, keepdims=True) + EPS)
    xh = x * r
    dxh = dy * g
    dx = r * (dxh - xh * jnp.mean(dxh * xh, axis=-1, keepdims=True))
    return dx, jnp.sum(dy * xh, axis=0, keepdims=True)


def _rot(t, c, s):
    lane = lax.broadcasted_iota(jnp.int32, t.shape, 1)
    swapped = jnp.where(lane < ROT_DIM // 2, pltpu.roll(t, HEAD_DIM - ROT_DIM // 2, 1), pltpu.roll(t, ROT_DIM // 2, 1))
    return t * c + swapped * s


def _rope_tables(t):
    half = ROT_DIM // 2
    pos = jnp.arange(t, dtype=F32)
    inv_freq = ROPE_THETA ** (-jnp.arange(0, ROT_DIM, 2, dtype=F32) / ROT_DIM)
    ang = pos[:, None] * inv_freq[None, :]
    cos, sin = jnp.cos(ang), jnp.sin(ang)
    ones = jnp.ones((t, HEAD_DIM - ROT_DIM), F32)
    c_tab = jnp.concatenate([cos, cos, ones], axis=1)
    s_tab = jnp.concatenate([-sin, sin, 0.0 * ones], axis=1)
    return c_tab, s_tab


def _heads(t):
    return [t[:, h * HEAD_DIM:(h + 1) * HEAD_DIM] for h in range(t.shape[1] // HEAD_DIM)]


def _rope_fwd_fn(zq, zk, zv, c, s):
    q = jnp.concatenate([_rot(th, c, s) for th in _heads(zq)], axis=1)
    k = jnp.concatenate([_rot(th, c, s) for th in _heads(zk)], axis=1)
    return q, k, zv


def _rope_bwd_fn(dq0, dq1, dq2, dk0, dk1, dk2, dv0, dv1, dv2, c, s, dglu_a, dglu_b, dgate):
    parts = []
    for grp in (dq0, dq1, dq2, dk0, dk1, dk2):
        parts += [_rot(th, c, -s).astype(BF) for th in _heads(grp)]
    parts += [dv0.astype(BF), dv1.astype(BF), dv2.astype(BF), dglu_a, dglu_b, dgate]
    return (jnp.concatenate(parts, axis=1),)


def _merge_fn(o0, o1, o2, l0, l1, l2):
    m = jnp.maximum(jnp.maximum(l0, l1), l2)
    e0, e1, e2 = jnp.exp(l0 - m), jnp.exp(l1 - m), jnp.exp(l2 - m)
    tot = e0 + e1 + e2
    return (e0 * o0 + e1 * o1 + e2 * o2) / tot, m + jnp.log(tot)


def _ln_parts(c1):
    mu = jnp.mean(c1, axis=-1, keepdims=True)
    xc = c1 - mu
    r = lax.rsqrt(jnp.mean(xc * xc, axis=-1, keepdims=True) + EPS)
    return xc * r, r


def _ln_silu_fn(c1, g, b):
    xh, _ = _ln_parts(c1)
    yl = xh * g + b
    return (yl * _sig(yl),)


def _ln_silu_bwd_fn(c1, dout, g, b):
    xh, r = _ln_parts(c1)
    yl = xh * g + b
    s = _sig(yl)
    dyl = dout * (s + yl * s * (1.0 - s))
    dxh = dyl * g
    dx = r * (dxh - jnp.mean(dxh, axis=-1, keepdims=True) - xh * jnp.mean(dxh * xh, axis=-1, keepdims=True))
    return dx, jnp.sum(dyl * xh, axis=0, keepdims=True), jnp.sum(dyl, axis=0, keepdims=True)


def _gate_fn(ya, yc, ga, gb, bg):
    sa = _sig(ga + bg[:, :D_MODEL])
    sb = _sig(gb + bg[:, D_MODEL:])
    return (sa * ya + sb * yc,)


def _gate_bwd_fn(dm, ya, yc, ga, gb, bg):
    sa = _sig(ga + bg[:, :D_MODEL])
    sb = _sig(gb + bg[:, D_MODEL:])
    dga = dm * ya * sa * (1.0 - sa)
    dgb = dm * yc * sb * (1.0 - sb)
    dgate = jnp.concatenate([dga, dgb], axis=1)
    return dm * sa, dm * sb, dgate, jnp.sum(dgate, axis=0, keepdims=True)


def _final_fn(x3, tgt, g):
    err = _rms(x3, g) - tgt
    lrow = jnp.sum(err * err, axis=-1, keepdims=True) * (0.5 / D_MODEL)
    lsum = jnp.sum(lrow, axis=0, keepdims=True)
    dx, dg = _rms_bwd(x3, g, err * (1.0 / D_MODEL))
    return dx, jnp.broadcast_to(lsum, (1, LANES)), dg


def _attn_geometry(t, dil):
    cls = t // dil
    rows = min(4 * BLK, cls)
    return rows, rows // BLK, cls // rows


def _band_masks():
    row = lax.broadcasted_iota(jnp.int32, (BLK, BLK), 0)
    col = lax.broadcasted_iota(jnp.int32, (BLK, BLK), 1)
    return row, col


def _attn_fwd(name, q, k, v, t, dil):
    rows, nbk, spc = _attn_geometry(t, dil)
    nblk = t // BLK

    def kern(q_ref, k_ref, kh_ref, v_ref, vh_ref, o_ref, l_ref):
        i = pl.program_id(0)
        first_shift = jnp.where(i % spc == 0, BLK, 0)
        row, col = _band_masks()
        for h in range(HEADS_PER_GROUP):
            hs = slice(h * HEAD_DIM, (h + 1) * HEAD_DIM)
            for b in range(nbk):
                rs = slice(b * BLK, (b + 1) * BLK)
                qb, kc, vc = q_ref[rs, hs], k_ref[rs, hs], v_ref[rs, hs]
                if b == 0:
                    kp, vp, shift = kh_ref[:, hs], vh_ref[:, hs], first_shift
                else:
                    ps = slice((b - 1) * BLK, b * BLK)
                    kp, vp, shift = k_ref[ps, hs], v_ref[ps, hs], 0
                sc = jnp.where(col <= row, _dot_nt(qb, kc) * ATTN_SCALE, NEG)
                sp = jnp.where(col >= row + shift, _dot_nt(qb, kp) * ATTN_SCALE, NEG)
                m = jnp.maximum(jnp.max(sc, axis=1, keepdims=True), jnp.max(sp, axis=1, keepdims=True))
                pc, pp = jnp.exp(sc - m), jnp.exp(sp - m)
                tot = jnp.sum(pc, axis=1, keepdims=True) + jnp.sum(pp, axis=1, keepdims=True)
                o_ref[rs, hs] = (_dot(pc.astype(BF), vc) + _dot(pp.astype(BF), vp)) / tot
                l_ref[rs, hs] = jnp.broadcast_to(m + jnp.log(tot), (BLK, HEAD_DIM))

    def cur(cb):
        return pl.BlockSpec((rows, GROUP_W), lambda i: (i, cb))

    def halo(cb):
        return pl.BlockSpec((BLK, GROUP_W), lambda i: (jnp.maximum(i * nbk - 1, 0), cb))

    (qa, qc), (ka, kc_), (va, vc_) = q, k, v
    return pl.pallas_call(
        kern, name=name, grid=(t // rows,),
        in_specs=[cur(qc), cur(kc_), halo(kc_), cur(vc_), halo(vc_)],
        out_specs=[pl.BlockSpec((rows, GROUP_W), lambda i: (i, 0))] * 2,
        out_shape=[jax.ShapeDtypeStruct((t, GROUP_W), F32)] * 2,
        compiler_params=_params(("parallel",)),
    )(qa, ka, ka, va, va)


def _attn_bwd(name, q, k, v, da, at, lse, t, dil, side=None):
    rows, nbk, spc = _attn_geometry(t, dil)
    nblk = t // BLK

    def kern(q_ref, qn_ref, k_ref, kh_ref, v_ref, vh_ref, da_ref, dan_ref, at_ref, atn_ref, ls_ref, lsn_ref,
             dq_ref, dk_ref, dv_ref):
        i = pl.program_id(0)
        first_shift = jnp.where(i % spc == 0, BLK, 0)
        next_shift = jnp.where((i + 1) % spc == 0, BLK, 0)
        row, col = _band_masks()
        for h in range(HEADS_PER_GROUP):
            hs = slice(h * HEAD_DIM, (h + 1) * HEAD_DIM)
            dq = [None] * nbk
            dk = [None] * nbk
            dv = [None] * nbk
            for b in range(nbk + 1):
                rs = slice(b * BLK, (b + 1) * BLK)
                if b < nbk:
                    qb, dab, atb, ls = q_ref[rs, hs], da_ref[rs, hs], at_ref[rs, hs], ls_ref[rs, hs]
                else:
                    qb, dab, atb, ls = qn_ref[:, hs], dan_ref[:, hs], atn_ref[:, hs], lsn_ref[:, hs]
                delta = jnp.sum(dab * atb, axis=1, keepdims=True)
                dab = dab.astype(BF)
                if b < nbk:
                    kc, vc = k_ref[rs, hs], v_ref[rs, hs]
                    p = jnp.where(col <= row, jnp.exp(_dot_nt(qb, kc) * ATTN_SCALE - ls), 0.0)
                    ds = (p * (_dot_nt(dab, vc) - delta) * ATTN_SCALE).astype(BF)
                    dv[b] = _dot_tn(p.astype(BF), dab)
                    dk[b] = _dot_tn(ds, qb)
                    dq[b] = _dot(ds, kc)
                if b == 0:
                    kp, vp, shift = kh_ref[:, hs], vh_ref[:, hs], first_shift
                else:
                    ps = slice((b - 1) * BLK, b * BLK)
                    kp, vp, shift = k_ref[ps, hs], v_ref[ps, hs], (next_shift if b == nbk else 0)
                p = jnp.where(col >= row + shift, jnp.exp(_dot_nt(qb, kp) * ATTN_SCALE - ls), 0.0)
                ds = (p * (_dot_nt(dab, vp) - delta) * ATTN_SCALE).astype(BF)
                if b < nbk:
                    dq[b] = dq[b] + _dot(ds, kp)
                if b >= 1:
                    dv[b - 1] = dv[b - 1] + _dot_tn(p.astype(BF), dab)
                    dk[b - 1] = dk[b - 1] + _dot_tn(ds, qb)
            for b in range(nbk):
                rs = slice(b * BLK, (b + 1) * BLK)
                dq_ref[rs, hs] = dq[b]
                dk_ref[rs, hs] = dk[b]
                dv_ref[rs, hs] = dv[b]

    def cur(cb):
        return pl.BlockSpec((rows, GROUP_W), lambda i: (i, cb))

    def prev(cb):
        return pl.BlockSpec((BLK, GROUP_W), lambda i: (jnp.maximum(i * nbk - 1, 0), cb))

    def nxt(cb):
        return pl.BlockSpec((BLK, GROUP_W), lambda i: (jnp.minimum((i + 1) * nbk, nblk - 1), cb))

    (qa, qc), (ka, kc_), (va, vc_) = q, k, v
    return _pcall(name, kern, (t // rows,),
                  [cur(qc), nxt(qc), cur(kc_), prev(kc_), cur(vc_), prev(vc_), cur(0), nxt(0), cur(0), nxt(0), cur(0), nxt(0)],
                  [pl.BlockSpec((rows, GROUP_W), lambda i: (i, 0))] * 3, [jax.ShapeDtypeStruct((t, GROUP_W), F32)] * 3, [],
                  ("parallel",), (qa, qa, ka, ka, va, va, da, da, at, at, lse, lse), side)


def _to_classes(a, dil):
    if dil == 1:
        return a
    t, w = a.shape
    return a.reshape(t // dil, dil, w).transpose(1, 0, 2).reshape(t, w)


def _from_classes(a, dil):
    if dil == 1:
        return a
    t, w = a.shape
    return a.reshape(dil, t // dil, w).transpose(1, 0, 2).reshape(t, w)


def _cross_probs(qh, kh):
    s = _dot_nt(qh, kh) * CROSS_SCALE
    e = jnp.exp(s - jnp.max(s, axis=1, keepdims=True))
    return e, jnp.sum(e, axis=1, keepdims=True)


def _cross_fwd(cq, ckv, t):
    rows = min(512, t)

    def kern(q_ref, kv_ref, o_ref):
        for h in range(CROSS_HEADS):
            hs = slice(h * CROSS_HEAD_DIM, (h + 1) * CROSS_HEAD_DIM)
            vs = slice(D_MODEL + h * CROSS_HEAD_DIM, D_MODEL + (h + 1) * CROSS_HEAD_DIM)
            e, tot = _cross_probs(q_ref[:, hs], kv_ref[:, hs])
            o_ref[:, hs] = (_dot(e.astype(BF), kv_ref[:, vs]) / tot).astype(BF)

    return pl.pallas_call(
        kern, name="cross_fwd", grid=(t // rows,),
        in_specs=[pl.BlockSpec((rows, D_MODEL), lambda i: (i, 0)), pl.BlockSpec((N_MEM, 2 * D_MODEL), lambda i: (0, 0))],
        out_specs=pl.BlockSpec((rows, D_MODEL), lambda i: (i, 0)),
        out_shape=jax.ShapeDtypeStruct((t, D_MODEL), BF),
        compiler_params=_params(("parallel",)),
    )(cq, ckv)


def _cross_bwd(cq, ckv, dco, t):
    rows = min(512, t)

    def kern(q_ref, kv_ref, do_ref, dq_ref, dkv_ref):
        @pl.when(pl.program_id(0) == 0)
        def _():
            dkv_ref[...] = jnp.zeros_like(dkv_ref)
        for h in range(CROSS_HEADS):
            hs = slice(h * CROSS_HEAD_DIM, (h + 1) * CROSS_HEAD_DIM)
            vs = slice(D_MODEL + h * CROSS_HEAD_DIM, D_MODEL + (h + 1) * CROSS_HEAD_DIM)
            qh, kh, vh, doh = q_ref[:, hs], kv_ref[:, hs], kv_ref[:, vs], do_ref[:, hs]
            e, tot = _cross_probs(qh, kh)
            p = e / tot
            dp = _dot_nt(doh, vh)
            ds = (p * (dp - jnp.sum(p * dp, axis=1, keepdims=True)) * CROSS_SCALE).astype(BF)
            dq_ref[:, hs] = _dot(ds, kh).astype(BF)
            dkv_ref[:, hs] += _dot_tn(ds, qh)
            dkv_ref[:, vs] += _dot_tn(p.astype(BF), doh)

    return pl.pallas_call(
        kern, name="cross_bwd", grid=(t // rows,),
        in_specs=[pl.BlockSpec((rows, D_MODEL), lambda i: (i, 0)), pl.BlockSpec((N_MEM, 2 * D_MODEL), lambda i: (0, 0)),
                  pl.BlockSpec((rows, D_MODEL), lambda i: (i, 0))],
        out_specs=[pl.BlockSpec((rows, D_MODEL), lambda i: (i, 0)), pl.BlockSpec((N_MEM, 2 * D_MODEL), lambda i: (0, 0))],
        out_shape=[jax.ShapeDtypeStruct((t, D_MODEL), BF), jax.ShapeDtypeStruct((N_MEM, 2 * D_MODEL), F32)],
        compiler_params=_params(("arbitrary",)),
    )(cq, ckv, dco)


CONV_TILE = 512
CONV_CHUNK = 128
HALO = 32


def _conv_fwd(z, w32, bias, t):
    tile = min(CONV_TILE, t)
    a_cb, b_cb = GLU_A_COL // LANES, GLU_B_COL // LANES
    hb = tile // HALO

    def kern(a_ref, b_ref, ah_ref, bh_ref, w_ref, bias_ref, o_ref, g_scr):
        i = pl.program_id(1)
        g_scr[HALO:, :] = a_ref[...] * _sig(b_ref[...])
        g_scr[:HALO, :] = ah_ref[...] * _sig(bh_ref[...]) * jnp.where(i > 0, 1.0, 0.0)
        for c in range(tile // CONV_CHUNK):
            acc = jnp.broadcast_to(bias_ref[...], (CONV_CHUNK, LANES))
            for j in range(CONV_K):
                lo = c * CONV_CHUNK + HALO - (CONV_K - 1) + j
                acc = acc + w_ref[j:j + 1, :] * g_scr[lo:lo + CONV_CHUNK, :]
            o_ref[c * CONV_CHUNK:(c + 1) * CONV_CHUNK, :] = acc

    def cur(cb):
        return pl.BlockSpec((tile, LANES), lambda j, i: (i, cb + j))

    def prev(cb):
        return pl.BlockSpec((HALO, LANES), lambda j, i: (jnp.maximum(i * hb - 1, 0), cb + j))

    return pl.pallas_call(
        kern, name="conv_fwd", grid=(CONV_CH // LANES, t // tile),
        in_specs=[cur(a_cb), cur(b_cb), prev(a_cb), prev(b_cb),
                  pl.BlockSpec((CONV_KP, LANES), lambda j, i: (0, j)), pl.BlockSpec((1, LANES), lambda j, i: (0, j))],
        out_specs=pl.BlockSpec((tile, LANES), lambda j, i: (i, j)),
        out_shape=jax.ShapeDtypeStruct((t, CONV_CH), F32),
        scratch_shapes=[pltpu.VMEM((tile + HALO, LANES), F32)],
        compiler_params=_params(("parallel", "parallel")),
    )(z, z, z, z, w32, bias)


def _conv_bwd(z, dc1, w32, t, side=None):
    tile = min(CONV_TILE, t)
    a_cb, b_cb = GLU_A_COL // LANES, GLU_B_COL // LANES
    hb = tile // HALO
    n_tiles = t // tile
    n_chunks = tile // CONV_CHUNK

    def kern(a_ref, b_ref, ah_ref, bh_ref, d_ref, dn_ref, w_ref, da_ref, db_ref, dw_ref, g_scr, d_scr):
        i = pl.program_id(1)
        sg = _sig(b_ref[...])
        g_scr[HALO:, :] = a_ref[...] * sg
        g_scr[:HALO, :] = ah_ref[...] * _sig(bh_ref[...]) * jnp.where(i > 0, 1.0, 0.0)
        d_scr[:tile, :] = d_ref[...]
        d_scr[tile:, :] = dn_ref[...] * jnp.where(i < n_tiles - 1, 1.0, 0.0)

        @pl.when(i == 0)
        def _():
            dw_ref[...] = jnp.zeros_like(dw_ref)

        for c in range(n_chunks):
            cs = slice(c * CONV_CHUNK, (c + 1) * CONV_CHUNK)
            acc = jnp.zeros((CONV_CHUNK, LANES), F32)
            for j in range(CONV_K):
                lo = c * CONV_CHUNK + (CONV_K - 1) - j
                acc = acc + w_ref[j:j + 1, :] * d_scr[lo:lo + CONV_CHUNK, :]
            sgc = _sig(b_ref[cs, :])
            da_ref[cs, :] = (acc * sgc).astype(BF)
            db_ref[cs, :] = (acc * a_ref[cs, :] * sgc * (1.0 - sgc)).astype(BF)
        for j in range(CONV_K):
            tot = jnp.zeros((1, LANES), F32)
            for c in range(n_chunks):
                lo = c * CONV_CHUNK + HALO - (CONV_K - 1) + j
                tot = tot + jnp.sum(d_ref[c * CONV_CHUNK:(c + 1) * CONV_CHUNK, :] * g_scr[lo:lo + CONV_CHUNK, :],
                                    axis=0, keepdims=True)
            dw_ref[j:j + 1, :] += tot
        dw_ref[CONV_K:CONV_KP, :] += jnp.sum(d_ref[...], axis=0, keepdims=True)

    def cur(cb):
        return pl.BlockSpec((tile, LANES), lambda j, i: (i, cb + j))

    def prev(cb):
        return pl.BlockSpec((HALO, LANES), lambda j, i: (jnp.maximum(i * hb - 1, 0), cb + j))

    return _pcall(
        "conv_bwd", kern, (CONV_CH // LANES, n_tiles),
        [cur(a_cb), cur(b_cb), prev(a_cb), prev(b_cb), cur(0),
         pl.BlockSpec((HALO, LANES), lambda j, i: (jnp.minimum((i + 1) * hb, t // HALO - 1), j)),
         pl.BlockSpec((CONV_KP, LANES), lambda j, i: (0, j))],
        [pl.BlockSpec((tile, LANES), lambda j, i: (i, j)), pl.BlockSpec((tile, LANES), lambda j, i: (i, j)),
         pl.BlockSpec((CONV_KP, LANES), lambda j, i: (0, j))],
        [jax.ShapeDtypeStruct((t, CONV_CH), BF), jax.ShapeDtypeStruct((t, CONV_CH), BF),
         jax.ShapeDtypeStruct((CONV_KP, CONV_CH), F32)],
        [pltpu.VMEM((tile + HALO, LANES), F32), pltpu.VMEM((tile + HALO, LANES), F32)],
        ("parallel", "arbitrary"), (z, z, z, z, dc1, dc1, w32), side)


def _adam_fn(w, g, m, v):
    m = ADAM_B1 * m + (1.0 - ADAM_B1) * g
    v = ADAM_B2 * v + (1.0 - ADAM_B2) * (g * g)
    m_hat = m / (1.0 - ADAM_B1 ** ADAM_STEP)
    v_hat = v / (1.0 - ADAM_B2 ** ADAM_STEP)
    delta = -ADAM_LR * (m_hat / (jnp.sqrt(v_hat) + ADAM_EPS) + ADAM_WD * w)
    return delta, m, v


def _adam(name, w, g, m, v):
    rows, cols = w.shape
    tile = _ew_tile(rows, cols)
    return _rowcall(name, _adam_fn, rows, tile, [(a, cols, 0) for a in (w, g, m, v)], [], [(cols, F32)] * 3)


def _place():
    x, y, c = lax.axis_index("x"), lax.axis_index("y"), lax.axis_index("c")
    chips = [(1 - x, y), (x, 1 - y), (1 - x, 1 - y)]
    return x, y, c, chips


def _gather_side(shards):
    nw = len(shards)
    chip = 2 * lax.axis_index("x") + lax.axis_index("y")
    staged = [lax.dynamic_update_index_in_dim(jnp.zeros((4,) + s.shape, s.dtype), s, chip, 0) for s in shards]

    def build(_, outs, send_sems, recv_sems):
        x, y, c, chips = _place()
        me = 2 * x + y
        sibling = (x, y, 1 - c)

        def half(w, lead, h):
            n = shards[w].shape[0] // 2
            return outs[w].at[lead, pl.ds(h * n, n)]

        def copy(w, k, part, to):
            return pltpu.make_async_remote_copy(src_ref=part, dst_ref=part, send_sem=send_sems.at[6 * w + k],
                                                recv_sem=recv_sems.at[6 * w + k], device_id=to, device_id_type=MESH)

        def start():
            for w in range(nw):
                for k, (px, py) in enumerate(chips):
                    copy(w, k, half(w, me, c), (px, py, c)).start()

        def finish():
            for w in range(nw):
                for k, (px, py) in enumerate(chips):
                    landed = half(w, 2 * px + py, c)
                    copy(w, k, landed, (px, py, c)).wait_recv()
                    copy(w, 3 + k, landed, sibling).start()
            for w in range(nw):
                for k, (px, py) in enumerate(chips):
                    copy(w, 3 + k, half(w, 2 * px + py, 1 - c), sibling).wait_recv()
            for w in range(nw):
                for k, (px, py) in enumerate(chips):
                    copy(w, k, half(w, me, c), (px, py, c)).wait_send()
                    copy(w, 3 + k, half(w, 2 * px + py, c), sibling).wait_send()

        return start, finish

    return _Side(staged, [jax.ShapeDtypeStruct((4,) + s.shape, s.dtype) for s in shards], 6 * nw, build,
                 aliases={w: w for w in range(nw)})


def _gather8(name, v):
    rows = v.shape[0]

    def body(v_ref, all_ref, sum_ref, send_sems, recv_sems):
        x, y, c, _ = _place()
        me = 4 * x + 2 * y + c
        all_ref[me] = v_ref[...]
        copies = []
        for k in range(1, 8):
            px, py, pc = x ^ (k >> 2), y ^ ((k >> 1) & 1), c ^ (k & 1)
            copies.append(pltpu.make_async_remote_copy(
                src_ref=v_ref, dst_ref=all_ref.at[me], send_sem=send_sems.at[k - 1], recv_sem=recv_sems.at[k - 1],
                device_id=(px, py, pc), device_id_type=MESH))
            copies[-1].start()
        for k in range(1, 8):
            px, py, pc = x ^ (k >> 2), y ^ ((k >> 1) & 1), c ^ (k & 1)
            theirs = all_ref.at[4 * px + 2 * py + pc]
            pltpu.make_async_remote_copy(
                src_ref=theirs, dst_ref=theirs, send_sem=send_sems.at[k - 1], recv_sem=recv_sems.at[k - 1],
                device_id=(px, py, pc), device_id_type=MESH).wait_recv()
        for cp in copies:
            cp.wait_send()
        tot = all_ref[0]
        for d in range(1, 8):
            tot = tot + all_ref[d]
        sum_ref[...] = tot

    vm = pl.BlockSpec(memory_space=pltpu.VMEM)
    return pl.pallas_call(
        body, name=name, in_specs=[vm], out_specs=[vm, vm],
        out_shape=[jax.ShapeDtypeStruct((8, rows, LANES), F32), jax.ShapeDtypeStruct((rows, LANES), F32)],
        scratch_shapes=[pltpu.SemaphoreType.DMA((7,)), pltpu.SemaphoreType.DMA((7,))],
    )(v)


def _region(ref, col_sharded, shape, j, h):
    r, ccols = shape
    if col_sharded:
        return ref.at[pl.ds(h * (r // 2), r // 2), pl.ds(j * (ccols // 4), ccols // 4)]
    n = r // 8
    return ref.at[pl.ds((2 * j + h) * n, n), :]


def _region_shape(col_sharded, shape):
    r, ccols = shape
    return (r // 2, ccols // 4) if col_sharded else (r // 8, ccols)


def _exchange(copies):
    def build(ins, outs, send_sems, recv_sems):
        def start():
            for cp in copies(ins, outs, send_sems, recv_sems):
                cp.start()

        def finish():
            for cp in copies(ins, outs, send_sems, recv_sems):
                cp.wait()

        return start, finish
    return build


def _swap_side(grads, kinds):
    nw = len(grads)

    def copies(ins, theirs, send_sems, recv_sems):
        x, y, c, _ = _place()
        return [pltpu.make_async_remote_copy(
            src_ref=_region(ins[w], kinds[w], grads[w].shape, j, 1 - c), dst_ref=theirs[w].at[j],
            send_sem=send_sems.at[4 * w + j], recv_sem=recv_sems.at[4 * w + j], device_id=(x, y, 1 - c), device_id_type=MESH)
            for w in range(nw) for j in range(4)]

    shapes = [jax.ShapeDtypeStruct((4,) + _region_shape(kinds[w], grads[w].shape), F32) for w in range(nw)]
    return _Side(grads, shapes, 4 * nw, _exchange(copies))


def _kept_halves(grad, col_sharded, c):
    r, ccols = grad.shape
    if col_sharded:
        slab = lax.dynamic_slice_in_dim(grad, c * (r // 2), r // 2, axis=0)
        return slab.reshape(r // 2, 4, ccols // 4).transpose(1, 0, 2)
    return lax.dynamic_index_in_dim(grad.reshape(4, 2, r // 8, ccols), c, axis=1, keepdims=False)


def _scatter_side(parts):
    nw = len(parts)

    def copies(ins, outs, send_sems, recv_sems):
        x, y, c, chips = _place()
        return [pltpu.make_async_remote_copy(
            src_ref=ins[w].at[2 * px + py], dst_ref=outs[w].at[k], send_sem=send_sems.at[3 * w + k],
            recv_sem=recv_sems.at[3 * w + k], device_id=(px, py, c), device_id_type=MESH)
            for w in range(nw) for k, (px, py) in enumerate(chips)]

    shapes = [jax.ShapeDtypeStruct((3,) + p.shape[1:], p.dtype) for p in parts]
    return _Side(parts, shapes, 3 * nw, _exchange(copies))


def _share_side(halves):
    nw = len(halves)

    def copies(ins, outs, send_sems, recv_sems):
        x, y, c, _ = _place()
        return [pltpu.make_async_remote_copy(
            src_ref=ins[w], dst_ref=outs[w], send_sem=send_sems.at[w], recv_sem=recv_sems.at[w],
            device_id=(x, y, 1 - c), device_id_type=MESH) for w in range(nw)]

    return _Side(halves, [jax.ShapeDtypeStruct(h.shape, F32) for h in halves], nw, _exchange(copies))


def _ew_tile(rows, cols):
    limit = max(8, (256 * 1024) // cols)
    return max(d for d in range(8, min(rows, limit) + 1, 8) if rows % d == 0)


def _pair_sums(names, grads, theirs):
    c = lax.axis_index("c")
    parts, parts_bf = {}, {}
    for n, other in zip(names, theirs):
        _, rr, cc = other.shape
        a = _kept_halves(grads[n], COL_SHARDED[n], c).reshape(4 * rr, cc)
        p, pb = _rowcall(f"grad_pair_sum_{n}", lambda u, v: (u + v, u + v), 4 * rr, _ew_tile(4 * rr, cc),
                         [(a, cc, 0), (other.reshape(4 * rr, cc), cc, 0)], [], [(cc, F32), (cc, BF)])
        parts[n], parts_bf[n] = p.reshape(4, rr, cc), pb.reshape(4, rr, cc)
    return parts, parts_bf


def _chip_sums(names, parts, landed):
    chip = 2 * lax.axis_index("x") + lax.axis_index("y")
    halves = {}
    for n, got in zip(names, landed):
        _, rr, cc = got.shape
        tile = _ew_tile(rr, cc)
        own = lax.dynamic_index_in_dim(parts[n], chip, axis=0, keepdims=False)
        flat = got.reshape(3 * rr, cc)
        nb = rr // tile
        (halves[n],) = _rowcall(f"grad_chip_sum_{n}", lambda own, k0, k1, k2: (((own + k0) + k1) + k2,), rr, tile,
                                [(own, cc, 0), (flat, cc, 0, 0), (flat, cc, 0, nb), (flat, cc, 0, 2 * nb)], [], [(cc, F32)])
    return halves


def _both_halves(mine, other):
    c = lax.axis_index("c")
    return jnp.where(c == 0, jnp.concatenate([mine, other], axis=0), jnp.concatenate([other, mine], axis=0))


BIG = ("w_in", "w_attn_proj", "w_conv_proj", "w_out", "w_cq", "w_ckv", "w_co", "w_up", "w_down")
COL_SHARDED = {"w_in": True, "w_attn_proj": True, "w_conv_proj": True, "w_out": False, "w_cq": False,
               "w_ckv": True, "w_co": False, "w_up": True, "w_down": False}
SMALL = ("g_mix", "b_gate", "conv_b", "conv_ln_g", "conv_ln_b", "g_cross", "g_mem", "g_mlp", "g_final")
ORDER = ("g_mix", "w_in", "b_gate", "conv_w", "conv_b", "conv_ln_g", "conv_ln_b", "w_attn_proj", "w_conv_proj", "w_out",
         "g_cross", "g_mem", "w_cq", "w_ckv", "w_co", "g_mlp", "w_up", "w_down", "g_final")


def _pad_rows(flat, rows):
    return jnp.pad(flat, (0, rows * LANES - flat.shape[0])).reshape(rows, LANES)


REST = tuple(n for n in BIG if n != "w_in")
SCATTER_GROUPS = (("w_down", "w_attn_proj"), ("w_up", "w_conv_proj"), ("w_ckv", "w_out", "w_cq", "w_co"))


def _local_step(x, mem, tgt, shards, small, conv_w_full):
    t = x.shape[0]
    tr = 256
    c_tab, s_tab = _rope_tables(t)
    row = lambda v: v.reshape(1, -1)
    g_mix, g_cross, g_mem, g_mlp, g_final = (row(small[n]) for n in ("g_mix", "g_cross", "g_mem", "g_mlp", "g_final"))
    b_gate, conv_b, ln_g, ln_b = (row(small[n]) for n in ("b_gate", "conv_b", "conv_ln_g", "conv_ln_b"))
    w32 = jnp.pad(conv_w_full, ((0, CONV_KP - CONV_K), (0, 0)))

    (w_in_all,) = _comm_only("gather_w_in", _gather_side([shards["w_in"]]))
    (u,) = _rowcall("mix_norm", lambda a, g: (_rms(a, g),), t, tr, [(x, D_MODEL, 0)], [g_mix], [(D_MODEL, BF)])
    z, gathered = _mm("in_proj", u, w_in_all, side=_gather_side([shards[n] for n in REST]))
    wfull = {"w_in": w_in_all}
    for n, g in zip(REST, gathered):
        wfull[n] = g if COL_SHARDED[n] else g.reshape(1, 4 * g.shape[1], g.shape[2])
    qr, kr, vb = _rowcall("rope", _rope_fwd_fn, t, tr,
                          [(z, ATTN_WIDTH, 0), (z, ATTN_WIDTH, 1), (z, ATTN_WIDTH, 2), (c_tab, HEAD_DIM, 0), (s_tab, HEAD_DIM, 0)],
                          [], [(ATTN_WIDTH, BF)] * 3)
    qkv_cls, outs, lses = [], [], []
    for g, dil in enumerate(DILATIONS):
        if dil == 1:
            ops = ((qr, g), (kr, g), (vb, g))
        else:
            gs = slice(g * GROUP_W, (g + 1) * GROUP_W)
            ops = tuple((_to_classes(a[:, gs], dil), 0) for a in (qr, kr, vb))
        qkv_cls.append(ops)
        o_g, l_g = _attn_fwd(f"attn_fwd_{g}", *ops, t, dil)
        outs.append(_from_classes(o_g, dil))
        lses.append(_from_classes(l_g, dil))
    attn, lse = _rowcall("attn_merge", _merge_fn, t, tr, [(a, GROUP_W, 0) for a in outs + lses], [],
                         [(GROUP_W, F32), (GROUP_W, F32)])
    y_attn = _mm("attn_proj", attn, wfull["w_attn_proj"])
    c1 = _conv_fwd(z, w32, conv_b, t)
    (c2,) = _rowcall("conv_ln_silu", _ln_silu_fn, t, tr, [(c1, CONV_CH, 0)], [ln_g, ln_b], [(CONV_CH, BF)])
    y_conv = _mm("conv_proj", c2, wfull["w_conv_proj"])
    gate_ins = [(z, D_MODEL, GATE_A_COL // D_MODEL), (z, D_MODEL, GATE_B_COL // D_MODEL)]
    (merged,) = _rowcall("gate", _gate_fn, t, tr, [(y_attn, D_MODEL, 0), (y_conv, D_MODEL, 0)] + gate_ins, [b_gate],
                         [(D_MODEL, BF)])
    add = lambda acc, res: (res + acc,)
    x1 = _mm("out_proj", merged, wfull["w_out"], extras=(x,), epi=add)

    (uq,) = _rowcall("cross_norm", lambda a, g: (_rms(a, g),), t, tr, [(x1, D_MODEL, 0)], [g_cross], [(D_MODEL, BF)])
    (mn,) = _rowcall("mem_norm", lambda a, g: (_rms(a, g),), N_MEM, N_MEM, [(mem, D_MODEL, 0)], [g_mem], [(D_MODEL, BF)])
    cq = _mm("cross_q", uq, wfull["w_cq"], out_dtypes=(BF,))
    ckv = _mm("cross_kv", mn, wfull["w_ckv"], out_dtypes=(BF,))
    co = _cross_fwd(cq, ckv, t)
    x2 = _mm("cross_out", co, wfull["w_co"], extras=(x1,), epi=add)

    (um,) = _rowcall("mlp_norm", lambda a, g: (_rms(a, g),), t, tr, [(x2, D_MODEL, 0)], [g_mlp], [(D_MODEL, BF)])
    hpre, hact = _mm("mlp_up", um, wfull["w_up"], out_dtypes=(F32, BF),
                     epi=lambda acc: (acc, jnp.square(jnp.maximum(acc, 0.0))))
    x3 = _mm("mlp_down", hact, wfull["w_down"], extras=(x2,), epi=add)
    d3, loss_row, dg_final = _rowcall("final_norm_loss", _final_fn, t, tr, [(x3, D_MODEL, 0), (tgt, D_MODEL, 0)], [g_final],
                                      [(D_MODEL, F32)], accs=[(1, LANES), (1, D_MODEL)])

    gw = {}
    dhp = _mm("mlp_down_bwd", d3, wfull["w_down"], nt=True, extras=(hpre,), out_dtypes=(BF,),
              epi=lambda acc, hp: (acc * 2.0 * jnp.maximum(hp, 0.0),))
    gw["w_down"] = _mm_tn("mlp_down_wgrad", hact, d3)
    dum = _mm("mlp_up_bwd", dhp, wfull["w_up"], nt=True)
    gw["w_up"] = _mm_tn("mlp_up_wgrad", um, dhp)

    def norm_bwd(a, dn, dres, g):
        dx, dg = _rms_bwd(a, g, dn)
        return dres + dx, dg

    d2, dg_mlp = _rowcall("mlp_norm_bwd", norm_bwd, t, tr, [(x2, D_MODEL, 0), (dum, D_MODEL, 0), (d3, D_MODEL, 0)], [g_mlp],
                          [(D_MODEL, F32)], accs=[(1, D_MODEL)])

    dco = _mm("cross_out_bwd", d2, wfull["w_co"], nt=True, out_dtypes=(BF,))
    gw["w_co"] = _mm_tn("cross_out_wgrad", co, d2)
    dcq, dckv = _cross_bwd(cq, ckv, dco, t)
    gw["w_cq"] = _mm_tn("cross_q_wgrad", uq, dcq)
    duq = _mm("cross_q_bwd", dcq, wfull["w_cq"], nt=True)
    d1, dg_cross = _rowcall("cross_norm_bwd", norm_bwd, t, tr, [(x1, D_MODEL, 0), (duq, D_MODEL, 0), (d2, D_MODEL, 0)],
                            [g_cross], [(D_MODEL, F32)], accs=[(1, D_MODEL)])
    gw["w_ckv"] = _mm_tn("cross_kv_wgrad", mn, dckv, tk=N_MEM)
    dmn = _mm("cross_kv_bwd", dckv, wfull["w_ckv"], nt=True)
    (dg_mem,) = _rowcall("mem_norm_bwd", lambda a, dn, g: (_rms_bwd(a, g, dn)[1],), N_MEM, N_MEM,
                         [(mem, D_MODEL, 0), (dmn, D_MODEL, 0)], [g_mem], [], accs=[(1, D_MODEL)])

    dmerged = _mm("out_proj_bwd", d1, wfull["w_out"], nt=True)
    gw["w_out"] = _mm_tn("out_proj_wgrad", merged, d1)
    dya, dyc, dgate, dbg = _rowcall("gate_bwd", _gate_bwd_fn, t, tr,
                                    [(dmerged, D_MODEL, 0), (y_attn, D_MODEL, 0), (y_conv, D_MODEL, 0)] + gate_ins, [b_gate],
                                    [(D_MODEL, BF), (D_MODEL, BF), (2 * D_MODEL, BF)], accs=[(1, 2 * D_MODEL)])
    gw["w_attn_proj"] = _mm_tn("attn_proj_wgrad", attn, dya)
    dattn = _mm("attn_proj_bwd", dya, wfull["w_attn_proj"], nt=True)
    gw["w_conv_proj"] = _mm_tn("conv_proj_wgrad", c2, dyc)
    dc2 = _mm("conv_proj_bwd", dyc, wfull["w_conv_proj"], nt=True)
    dc1, dlng, dlnb = _rowcall("conv_ln_silu_bwd", _ln_silu_bwd_fn, t, tr, [(c1, CONV_CH, 0), (dc2, CONV_CH, 0)], [ln_g, ln_b],
                               [(CONV_CH, F32)], accs=[(1, CONV_CH), (1, CONV_CH)])
    (dglu_a, dglu_b, dconv), theirs = _conv_bwd(z, dc1, w32, t, side=_swap_side([gw[n] for n in REST],
                                                                                  [COL_SHARDED[n] for n in REST]))
    parts, parts_bf = _pair_sums(REST, gw, theirs)

    dqs, dks, dvs, landed = [], [], [], {}
    for g, dil in enumerate(DILATIONS):
        da_c, at_c, ls_c = (_to_classes(a, dil) for a in (dattn, attn, lse))
        (dq_g, dk_g, dv_g), got = _attn_bwd(f"attn_bwd_{g}", *qkv_cls[g], da_c, at_c, ls_c, t, dil,
                                            side=_scatter_side([parts_bf[n] for n in SCATTER_GROUPS[g]]))
        landed.update(zip(SCATTER_GROUPS[g], got))
        dqs.append(_from_classes(dq_g, dil))
        dks.append(_from_classes(dk_g, dil))
        dvs.append(_from_classes(dv_g, dil))
    halves = _chip_sums(REST, parts, [landed[n] for n in REST])
    (dz,), others = _rowcall("rope_bwd", _rope_bwd_fn, t, tr,
                             [(a, GROUP_W, 0) for a in dqs + dks + dvs] + [(c_tab, HEAD_DIM, 0), (s_tab, HEAD_DIM, 0)]
                             + [(dglu_a, CONV_CH, 0), (dglu_b, CONV_CH, 0), (dgate, 2 * D_MODEL, 0)], [], [(IN_WIDTH, BF)],
                             side=_share_side([halves[n] for n in REST]))
    gshard = {n: _both_halves(halves[n], o) for n, o in zip(REST, others)}

    gw_in = {"w_in": _mm_tn("in_proj_wgrad", u, dz)}
    du_a, theirs = _mm("in_proj_bwd_a", dz, wfull["w_in"], nt=True, k_part=(0, 2), side=_swap_side([gw_in["w_in"]], [True]))
    parts, parts_bf = _pair_sums(("w_in",), gw_in, theirs)
    du, landed_in = _mm("in_proj_bwd_b", dz, wfull["w_in"], nt=True, k_part=(1, 2), extras=(du_a,), epi=add,
                        side=_scatter_side([parts_bf["w_in"]]))
    halves = _chip_sums(("w_in",), parts, landed_in)
    (gx, dg_mix), others = _rowcall("mix_norm_bwd", norm_bwd, t, tr, [(x, D_MODEL, 0), (du, D_MODEL, 0), (d1, D_MODEL, 0)],
                                    [g_mix], [(D_MODEL, F32)], accs=[(1, D_MODEL)], side=_share_side([halves["w_in"]]))
    gshard["w_in"] = _both_halves(halves["w_in"], others[0])

    gsmall = {"g_mix": dg_mix, "b_gate": dbg, "conv_b": dconv[CONV_K:CONV_K + 1], "conv_ln_g": dlng, "conv_ln_b": dlnb,
              "g_cross": dg_cross, "g_mem": dg_mem, "g_mlp": dg_mlp, "g_final": dg_final, "conv_w": dconv[:CONV_K]}
    return loss_row, gx, gshard, gsmall


def kernel(x, mem, g_mix, w_in, b_gate, conv_w, conv_b, conv_ln_g, conv_ln_b, w_attn_proj, w_conv_proj, w_out, g_cross, g_mem, w_cq, w_ckv, w_co, g_mlp, w_up, w_down, g_final, loss_target, m_g_mix, m_w_in, m_b_gate, m_conv_w, m_conv_b, m_conv_ln_g, m_conv_ln_b, m_w_attn_proj, m_w_conv_proj, m_w_out, m_g_cross, m_g_mem, m_w_cq, m_w_ckv, m_w_co, m_g_mlp, m_w_up, m_w_down, m_g_final, v_g_mix, v_w_in, v_b_gate, v_conv_w, v_conv_b, v_conv_ln_g, v_conv_ln_b, v_w_attn_proj, v_w_conv_proj, v_w_out, v_g_cross, v_g_mem, v_w_cq, v_w_ckv, v_w_co, v_g_mlp, v_w_up, v_w_down, v_g_final):
    w = dict(g_mix=g_mix, w_in=w_in, b_gate=b_gate, conv_w=conv_w, conv_b=conv_b, conv_ln_g=conv_ln_g, conv_ln_b=conv_ln_b,
             w_attn_proj=w_attn_proj, w_conv_proj=w_conv_proj, w_out=w_out, g_cross=g_cross, g_mem=g_mem, w_cq=w_cq,
             w_ckv=w_ckv, w_co=w_co, g_mlp=g_mlp, w_up=w_up, w_down=w_down, g_final=g_final)
    mo = dict(g_mix=m_g_mix, w_in=m_w_in, b_gate=m_b_gate, conv_w=m_conv_w, conv_b=m_conv_b, conv_ln_g=m_conv_ln_g,
              conv_ln_b=m_conv_ln_b, w_attn_proj=m_w_attn_proj, w_conv_proj=m_w_conv_proj, w_out=m_w_out, g_cross=m_g_cross,
              g_mem=m_g_mem, w_cq=m_w_cq, w_ckv=m_w_ckv, w_co=m_w_co, g_mlp=m_g_mlp, w_up=m_w_up, w_down=m_w_down,
              g_final=m_g_final)
    vo = dict(g_mix=v_g_mix, w_in=v_w_in, b_gate=v_b_gate, conv_w=v_conv_w, conv_b=v_conv_b, conv_ln_g=v_conv_ln_g,
              conv_ln_b=v_conv_ln_b, w_attn_proj=v_w_attn_proj, w_conv_proj=v_w_conv_proj, w_out=v_w_out, g_cross=v_g_cross,
              g_mem=v_g_mem, w_cq=v_w_cq, w_ckv=v_w_ckv, w_co=v_w_co, g_mlp=v_g_mlp, w_up=v_w_up, w_down=v_w_down,
              g_final=v_g_final)
    shapes = {n: w[n].shape for n in ORDER}
    two_d = lambda a: a.reshape(a.shape[-2], a.shape[-1])
    chip = 2 * lax.axis_index("x") + lax.axis_index("y")

    shards = {n: two_d(w[n]).astype(BF) for n in BIG}
    cw_rows = 48
    cw_all, _ = _gather8("gather_conv_w", _pad_rows(conv_w.reshape(-1), cw_rows))
    cw_shard = CONV_K * (CONV_CH // 4)
    conv_w_full = jnp.concatenate(
        [cw_all[2 * j].reshape(-1)[:cw_shard].reshape(CONV_K, CONV_CH // 4) for j in range(4)], axis=1)

    small = {n: w[n] for n in SMALL}
    loss_row, gx, gshard, gsmall = _local_step(two_d(x), two_d(mem), two_d(loss_target), shards, small, conv_w_full)
    loss = lax.psum(loss_row[0, 0], ("x", "y", "c"))

    small_names = SMALL + ("conv_w",)
    flat = jnp.concatenate([gsmall[n].reshape(-1) for n in small_names])
    sm_rows = -(-flat.shape[0] // (8 * LANES)) * 8
    _, sm_sum = _gather8("reduce_small_grads", _pad_rows(flat, sm_rows))
    sm_sum = sm_sum.reshape(-1)
    off = 0
    for n in small_names:
        size = gsmall[n].size
        gshard[n] = sm_sum[off:off + size].reshape(gsmall[n].shape)
        off += size
    gshard["conv_w"] = lax.dynamic_slice_in_dim(gshard["conv_w"], chip * (CONV_CH // 4), CONV_CH // 4, axis=1)

    grads, deltas, new_m, new_v = {}, {}, {}, {}
    for n in BIG:
        d, m2, v2 = _adam(f"adamw_{n}", two_d(w[n]), gshard[n], two_d(mo[n]), two_d(vo[n]))
        grads[n], deltas[n], new_m[n], new_v[n] = (a.reshape(shapes[n]) for a in (gshard[n], d, m2, v2))
    pack = lambda src: jnp.concatenate([src[n].reshape(-1) for n in small_names])
    n_small = sum(w[n].size for n in small_names)
    ad_rows = -(-n_small // (8 * LANES)) * 8
    d, m2, v2 = _adam("adamw_small", *[_pad_rows(pack(src), ad_rows) for src in (w, gshard, mo, vo)])
    off = 0
    for n in small_names:
        size = w[n].size
        grads[n] = gshard[n].reshape(shapes[n])
        deltas[n], new_m[n], new_v[n] = (a.reshape(-1)[off:off + size].reshape(shapes[n]) for a in (d, m2, v2))
        off += size

    return (loss, gx.reshape(x.shape), *[grads[n] for n in ORDER], *[deltas[n] for n in ORDER],
            *[new_m[n] for n in ORDER], *[new_v[n] for n in ORDER])
```

```python
import functools

import jax
import jax.numpy as jnp
from jax import lax
from jax.experimental import pallas as pl
from jax.experimental.pallas import tpu as pltpu

F32 = jnp.float32
BF = jnp.bfloat16

D_MODEL = 1024
N_MEM = 256
HEAD_DIM = 128
HEADS_PER_GROUP = 4
DILATIONS = (1, 4, 16)
BLK = 128
GROUP_W = HEADS_PER_GROUP * HEAD_DIM
ATTN_WIDTH = 3 * GROUP_W
ROT_DIM = 32
ROPE_THETA = 500000.0
CONV_CH = 768
CONV_K = 31
CONV_KP = 32
IN_WIDTH = 8192
CROSS_HEADS = 4
CROSS_HEAD_DIM = 256
D_FF = 4096
EPS = 1e-6
ATTN_SCALE = HEAD_DIM ** -0.5
CROSS_SCALE = CROSS_HEAD_DIM ** -0.5
NEG = -1e30

ADAM_LR = 0.001
ADAM_B1 = 0.9
ADAM_B2 = 0.999
ADAM_EPS = 1e-08
ADAM_WD = 0.01
ADAM_STEP = 10

LANES = 128
VMEM_LIMIT = 56 * 1024 * 1024
MESH = pl.DeviceIdType.MESH
ANY = pl.BlockSpec(memory_space=pl.ANY)

GLU_A_COL = 3 * ATTN_WIDTH
GLU_B_COL = GLU_A_COL + CONV_CH
GATE_A_COL = GLU_B_COL + CONV_CH
GATE_B_COL = GATE_A_COL + D_MODEL


def _params(sem=None):
    return pltpu.CompilerParams(dimension_semantics=sem, vmem_limit_bytes=VMEM_LIMIT)


def _dot(a, b):
    return lax.dot_general(a, b, (((1,), (0,)), ((), ())), preferred_element_type=F32)


def _dot_nt(a, b):
    return lax.dot_general(a, b, (((1,), (1,)), ((), ())), preferred_element_type=F32)


def _dot_tn(a, b):
    return lax.dot_general(a, b, (((0,), (0,)), ((), ())), preferred_element_type=F32)


def _sig(x):
    return 1.0 / (1.0 + jnp.exp(-x))


class _Side:
    def __init__(self, arrays, out_shapes, n_sems, build, aliases=None):
        self.arrays, self.out_shapes, self.n_sems, self.build = list(arrays), list(out_shapes), n_sems, build
        self.aliases = aliases or {}


def _pcall(name, kern, grid, in_specs, out_specs, out_shape, scratch_shapes, sem, args, side=None):
    in_specs, out_specs, out_shape, scratch_shapes = list(in_specs), list(out_specs), list(out_shape), list(scratch_shapes)
    if side is None:
        return pl.pallas_call(kern, name=name, grid=grid, in_specs=in_specs, out_specs=out_specs, out_shape=out_shape,
                              scratch_shapes=scratch_shapes, compiler_params=_params(sem))(*args)
    ni, no, nsc = len(in_specs), len(out_specs), len(scratch_shapes)
    nsi, nso = len(side.arrays), len(side.out_shapes)

    def wrapped(*refs):
        ins, side_ins = refs[:ni], refs[ni:ni + nsi]
        outs, side_outs = refs[ni + nsi:ni + nsi + no], refs[ni + nsi + no:ni + nsi + no + nso]
        scratch = refs[ni + nsi + no + nso:ni + nsi + no + nso + nsc]
        send_sems, recv_sems = refs[-2:]
        start, finish = side.build(side_ins, side_outs, send_sems, recv_sems)
        if grid:
            first = functools.reduce(jnp.logical_and, [pl.program_id(a) == 0 for a in range(len(grid))])
            last = functools.reduce(jnp.logical_and, [pl.program_id(a) == g - 1 for a, g in enumerate(grid)])
            pl.when(first)(start)
            kern(*ins, *outs, *scratch)
            pl.when(last)(finish)
        else:
            start()
            kern(*ins, *outs, *scratch)
            finish()

    res = pl.pallas_call(
        wrapped, name=name, grid=grid, in_specs=in_specs + [ANY] * nsi, out_specs=out_specs + [ANY] * nso,
        out_shape=out_shape + side.out_shapes,
        scratch_shapes=scratch_shapes + [pltpu.SemaphoreType.DMA((side.n_sems,)), pltpu.SemaphoreType.DMA((side.n_sems,))],
        input_output_aliases={ni + k: no + v for k, v in side.aliases.items()},
        compiler_params=_params(("arbitrary",) * len(grid) if grid else None),
    )(*args, *side.arrays)
    return res[:no], res[no:]


def _rowcall(name, fn, n_rows, tile, ins, params, outs, accs=(), side=None):
    tile = min(tile, n_rows)
    ni, npar, no, na = len(ins), len(params), len(outs), len(accs)

    def kern(*refs):
        in_refs = refs[:ni + npar]
        o_refs = refs[ni + npar:ni + npar + no]
        a_refs = refs[ni + npar + no:]
        vals = fn(*[r[...] for r in in_refs])
        for r, v in zip(o_refs, vals[:no]):
            r[...] = v.astype(r.dtype)
        if na:
            @pl.when(pl.program_id(0) == 0)
            def _():
                for r in a_refs:
                    r[...] = jnp.zeros_like(r)
            for r, v in zip(a_refs, vals[no:]):
                r[...] += v

    in_specs = []
    arrays = []
    for spec in ins:
        arr, width, cb = spec[0], spec[1], spec[2]
        rb = spec[3] if len(spec) > 3 else 0
        in_specs.append(pl.BlockSpec((tile, width), functools.partial(lambda i, cb, rb: (i + rb, cb), cb=cb, rb=rb)))
        arrays.append(arr)
    for p in params:
        in_specs.append(pl.BlockSpec(p.shape, lambda i: (0, 0)))
        arrays.append(p)
    out_specs = [pl.BlockSpec((tile, w), lambda i: (i, 0)) for w, _ in outs]
    out_specs += [pl.BlockSpec(s, lambda i: (0, 0)) for s in accs]
    out_shape = [jax.ShapeDtypeStruct((n_rows, w), dt) for w, dt in outs]
    out_shape += [jax.ShapeDtypeStruct(s, F32) for s in accs]
    return _pcall(name, kern, (n_rows // tile,), in_specs, out_specs, out_shape, [],
                  ("arbitrary",) if na else ("parallel",), arrays, side)


def _mm(name, a, w3, *, nt=False, extras=(), epi=None, out_dtypes=(F32,), tm=None, tn=None, tk=None, k_part=(0, 1),
        side=None):
    m, ka = a.shape
    ns, r, cs = w3.shape
    if not nt:
        k_dim, n = r, ns * cs
        tn = tn or min(cs, 1024)
        tk = tk or min(k_dim, 1024)
    else:
        k_dim, n = ns * cs, r
        tn = tn or min(r, 1024)
        tk = tk or min(cs, 1024)
    assert ka == k_dim, (name, a.shape, w3.shape)
    nk = k_dim // tk // k_part[1]
    k0 = k_part[0] * nk
    if not nt:
        nbs = cs // tn
        w_spec = pl.BlockSpec((None, tk, tn), lambda i, j, k: (j // nbs, k + k0, j % nbs))
    else:
        kbs = cs // tk
        w_spec = pl.BlockSpec((None, tn, tk), lambda i, j, k: ((k + k0) // kbs, j, (k + k0) % kbs))
    tm = tm or min(m, 1024)
    ne, no = len(extras), len(out_dtypes)

    def kern(a_ref, w_ref, *rest):
        e_refs = rest[:ne]
        o_refs = rest[ne:ne + no]
        acc = rest[ne + no]
        k = pl.program_id(2)

        @pl.when(k == 0)
        def _():
            acc[...] = jnp.zeros_like(acc)

        av = a_ref[...].astype(BF)
        acc[...] += _dot_nt(av, w_ref[...]) if nt else _dot(av, w_ref[...])

        @pl.when(k == nk - 1)
        def _():
            res = acc[...]
            vals = epi(res, *[e[...] for e in e_refs]) if epi else (res,)
            for o, v in zip(o_refs, vals):
                o[...] = v.astype(o.dtype)

    in_specs = [pl.BlockSpec((tm, tk), lambda i, j, k: (i, k + k0)), w_spec]
    in_specs += [pl.BlockSpec((tm, tn), lambda i, j, k: (i, j)) for _ in extras]
    res = _pcall(name, kern, (m // tm, n // tn, nk), in_specs,
                 [pl.BlockSpec((tm, tn), lambda i, j, k: (i, j)) for _ in out_dtypes],
                 [jax.ShapeDtypeStruct((m, n), dt) for dt in out_dtypes], [pltpu.VMEM((tm, tn), F32)],
                 ("parallel", "parallel", "arbitrary"), (a, w3, *extras), side)
    if side is not None:
        return (res[0][0] if no == 1 else res[0]), res[1]
    return res[0] if no == 1 else res


def _mm_tn(name, a, b, tm=None, tn=None, tk=None):
    t, ka = a.shape
    _, n = b.shape
    tm = tm or min(ka, 1024)
    tn = tn or min(n, 1024)
    tk = tk or min(t, 512)

    def kern(a_ref, b_ref, o_ref):
        @pl.when(pl.program_id(2) == 0)
        def _():
            o_ref[...] = jnp.zeros_like(o_ref)
        o_ref[...] += _dot_tn(a_ref[...].astype(BF), b_ref[...].astype(BF))

    return pl.pallas_call(
        kern, name=name, grid=(ka // tm, n // tn, t // tk),
        in_specs=[pl.BlockSpec((tk, tm), lambda i, j, k: (k, i)), pl.BlockSpec((tk, tn), lambda i, j, k: (k, j))],
        out_specs=pl.BlockSpec((tm, tn), lambda i, j, k: (i, j)),
        out_shape=jax.ShapeDtypeStruct((ka, n), F32),
        compiler_params=_params(("parallel", "parallel", "arbitrary")),
    )(a, b)


def _rms(x, g):
    return x * lax.rsqrt(jnp.mean(x * x, axis=-1, keepdims=True) + EPS) * g


def _rms_bwd(x, g, dy):
    r = lax.rsqrt(jnp.mean(x * x, axis=-1, keepdims=True) + EPS)
    xh = x * r
    dxh = dy * g
    dx = r * (dxh - xh * jnp.mean(dxh * xh, axis=-1, keepdims=True))
    return dx, jnp.sum(dy * xh, axis=0, keepdims=True)


def _rot(t, c, s):
    lane = lax.broadcasted_iota(jnp.int32, t.shape, 1)
    swapped = jnp.where(lane < ROT_DIM // 2, pltpu.roll(t, HEAD_DIM - ROT_DIM // 2, 1), pltpu.roll(t, ROT_DIM // 2, 1))
    return t * c + swapped * s


def _rope_tables(t):
    half = ROT_DIM // 2
    pos = jnp.arange(t, dtype=F32)
    inv_freq = ROPE_THETA ** (-jnp.arange(0, ROT_DIM, 2, dtype=F32) / ROT_DIM)
    ang = pos[:, None] * inv_freq[None, :]
    cos, sin = jnp.cos(ang), jnp.sin(ang)
    ones = jnp.ones((t, HEAD_DIM - ROT_DIM), F32)
    c_tab = jnp.concatenate([cos, cos, ones], axis=1)
    s_tab = jnp.concatenate([-sin, sin, 0.0 * ones], axis=1)
    return c_tab, s_tab


def _merge_fn(o0, o1, o2, l0, l1, l2):
    m = jnp.maximum(jnp.maximum(l0, l1), l2)
    e0, e1, e2 = jnp.exp(l0 - m), jnp.exp(l1 - m), jnp.exp(l2 - m)
    tot = e0 + e1 + e2
    return (e0 * o0 + e1 * o1 + e2 * o2) / tot, m + jnp.log(tot)


def _ln_parts(c1):
    mu = jnp.mean(c1, axis=-1, keepdims=True)
    xc = c1 - mu
    r = lax.rsqrt(jnp.mean(xc * xc, axis=-1, keepdims=True) + EPS)
    return xc * r, r


def _ln_silu_fn(c1, g, b):
    xh, _ = _ln_parts(c1)
    yl = xh * g + b
    return (yl * _sig(yl),)


def _ln_silu_bwd_fn(c1, dout, g, b):
    xh, r = _ln_parts(c1)
    yl = xh * g + b
    s = _sig(yl)
    dyl = dout * (s + yl * s * (1.0 - s))
    dxh = dyl * g
    dx = r * (dxh - jnp.mean(dxh, axis=-1, keepdims=True) - xh * jnp.mean(dxh * xh, axis=-1, keepdims=True))
    return dx, jnp.sum(dyl * xh, axis=0, keepdims=True), jnp.sum(dyl, axis=0, keepdims=True)


def _gate_fn(ya, yc, ga, gb, bg):
    sa = _sig(ga + bg[:, :D_MODEL])
    sb = _sig(gb + bg[:, D_MODEL:])
    return (sa * ya + sb * yc,)


def _gate_bwd_fn(dm, ya, yc, ga, gb, bg):
    sa = _sig(ga + bg[:, :D_MODEL])
    sb = _sig(gb + bg[:, D_MODEL:])
    dga = dm * ya * sa * (1.0 - sa)
    dgb = dm * yc * sb * (1.0 - sb)
    dgate = jnp.concatenate([dga, dgb], axis=1)
    return dm * sa, dm * sb, dgate, jnp.sum(dgate, axis=0, keepdims=True)


def _final_fn(x3, tgt, g):
    err = _rms(x3, g) - tgt
    lrow = jnp.sum(err * err, axis=-1, keepdims=True) * (0.5 / D_MODEL)
    lsum = jnp.sum(lrow, axis=0, keepdims=True)
    dx, dg = _rms_bwd(x3, g, err * (1.0 / D_MODEL))
    return dx, jnp.broadcast_to(lsum, (1, LANES)), dg


def _attn_geometry(t, dil):
    cls = t // dil
    rows = min(4 * BLK, cls)
    return rows, rows // BLK, cls // rows


def _band_masks():
    row = lax.broadcasted_iota(jnp.int32, (BLK, BLK), 0)
    col = lax.broadcasted_iota(jnp.int32, (BLK, BLK), 1)
    return row, col


def _attn_fwd(name, q, k, v, t, dil):
    rows, nbk, spc = _attn_geometry(t, dil)
    nblk = t // BLK

    def kern(q_ref, k_ref, kh_ref, v_ref, vh_ref, o_ref, l_ref):
        i = pl.program_id(0)
        first_shift = jnp.where(i % spc == 0, BLK, 0)
        row, col = _band_masks()
        for h in range(HEADS_PER_GROUP):
            hs = slice(h * HEAD_DIM, (h + 1) * HEAD_DIM)
            for b in range(nbk):
                rs = slice(b * BLK, (b + 1) * BLK)
                qb, kc, vc = q_ref[rs, hs], k_ref[rs, hs], v_ref[rs, hs]
                if b == 0:
                    kp, vp, shift = kh_ref[:, hs], vh_ref[:, hs], first_shift
                else:
                    ps = slice((b - 1) * BLK, b * BLK)
                    kp, vp, shift = k_ref[ps, hs], v_ref[ps, hs], 0
                sc = jnp.where(col <= row, _dot_nt(qb, kc) * ATTN_SCALE, NEG)
                sp = jnp.where(col >= row + shift, _dot_nt(qb, kp) * ATTN_SCALE, NEG)
                m = jnp.maximum(jnp.max(sc, axis=1, keepdims=True), jnp.max(sp, axis=1, keepdims=True))
                pc, pp = jnp.exp(sc - m), jnp.exp(sp - m)
                tot = jnp.sum(pc, axis=1, keepdims=True) + jnp.sum(pp, axis=1, keepdims=True)
                o_ref[rs, hs] = (_dot(pc.astype(BF), vc) + _dot(pp.astype(BF), vp)) / tot
                l_ref[rs, hs] = jnp.broadcast_to(m + jnp.log(tot), (BLK, HEAD_DIM))

    def cur(cb):
        return pl.BlockSpec((rows, GROUP_W), lambda i: (i, cb))

    def halo(cb):
        return pl.BlockSpec((BLK, GROUP_W), lambda i: (jnp.maximum(i * nbk - 1, 0), cb))

    (qa, qc), (ka, kc_), (va, vc_) = q, k, v
    return pl.pallas_call(
        kern, name=name, grid=(t // rows,),
        in_specs=[cur(qc), cur(kc_), halo(kc_), cur(vc_), halo(vc_)],
        out_specs=[pl.BlockSpec((rows, GROUP_W), lambda i: (i, 0))] * 2,
        out_shape=[jax.ShapeDtypeStruct((t, GROUP_W), F32)] * 2,
        compiler_params=_params(("parallel",)),
    )(qa, ka, ka, va, va)


def _attn_bwd(name, q, k, v, da, dl, lse, t, dil, side=None):
    rows, nbk, spc = _attn_geometry(t, dil)
    nblk = t // BLK

    def kern(q_ref, qn_ref, k_ref, kh_ref, v_ref, vh_ref, da_ref, dan_ref, dl_ref, dln_ref, ls_ref, lsn_ref,
             dq_ref, dk_ref, dv_ref):
        i = pl.program_id(0)
        first_shift = jnp.where(i % spc == 0, BLK, 0)
        next_shift = jnp.where((i + 1) % spc == 0, BLK, 0)
        row, col = _band_masks()
        for h in range(HEADS_PER_GROUP):
            hs = slice(h * HEAD_DIM, (h + 1) * HEAD_DIM)
            dq = [None] * nbk
            dk = [None] * nbk
            dv = [None] * nbk
            for b in range(nbk + 1):
                rs = slice(b * BLK, (b + 1) * BLK)
                if b < nbk:
                    qb, dab, delta, ls = q_ref[rs, hs], da_ref[rs, hs], dl_ref[rs, hs], ls_ref[rs, hs]
                else:
                    qb, dab, delta, ls = qn_ref[:, hs], dan_ref[:, hs], dln_ref[:, hs], lsn_ref[:, hs]
                dab = dab.astype(BF)
                if b < nbk:
                    kc, vc = k_ref[rs, hs], v_ref[rs, hs]
                    p = jnp.where(col <= row, jnp.exp(_dot_nt(qb, kc) * ATTN_SCALE - ls), 0.0)
                    ds = (p * (_dot_nt(dab, vc) - delta) * ATTN_SCALE).astype(BF)
                    dv[b] = _dot_tn(p.astype(BF), dab)
                    dk[b] = _dot_tn(ds, qb)
                    dq[b] = _dot(ds, kc)
                if b == 0:
                    kp, vp, shift = kh_ref[:, hs], vh_ref[:, hs], first_shift
                else:
                    ps = slice((b - 1) * BLK, b * BLK)
                    kp, vp, shift = k_ref[ps, hs], v_ref[ps, hs], (next_shift if b == nbk else 0)
                p = jnp.where(col >= row + shift, jnp.exp(_dot_nt(qb, kp) * ATTN_SCALE - ls), 0.0)
                ds = (p * (_dot_nt(dab, vp) - delta) * ATTN_SCALE).astype(BF)
                if b < nbk:
                    dq[b] = dq[b] + _dot(ds, kp)
                if b >= 1:
                    dv[b - 1] = dv[b - 1] + _dot_tn(p.astype(BF), dab)
                    dk[b - 1] = dk[b - 1] + _dot_tn(ds, qb)
            for b in range(nbk):
                rs = slice(b * BLK, (b + 1) * BLK)
                dq_ref[rs, hs] = dq[b]
                dk_ref[rs, hs] = dk[b]
                dv_ref[rs, hs] = dv[b]

    def cur(cb):
        return pl.BlockSpec((rows, GROUP_W), lambda i: (i, cb))

    def prev(cb):
        return pl.BlockSpec((BLK, GROUP_W), lambda i: (jnp.maximum(i * nbk - 1, 0), cb))

    def nxt(cb):
        return pl.BlockSpec((BLK, GROUP_W), lambda i: (jnp.minimum((i + 1) * nbk, nblk - 1), cb))

    (qa, qc), (ka, kc_), (va, vc_) = q, k, v
    return _pcall(name, kern, (t // rows,),
                  [cur(qc), nxt(qc), cur(kc_), prev(kc_), cur(vc_), prev(vc_), cur(0), nxt(0), cur(0), nxt(0), cur(0), nxt(0)],
                  [pl.BlockSpec((rows, GROUP_W), lambda i: (i, 0))] * 3, [jax.ShapeDtypeStruct((t, GROUP_W), F32)] * 3, [],
                  ("parallel",), (qa, qa, ka, ka, va, va, da, da, dl, dl, lse, lse), side)


CLS_TILE = 512


def _cls_block(t, tile, dil, dtype):
    if dil == 1:
        return pl.BlockSpec((tile, GROUP_W), lambda i: (i, 0)), jax.ShapeDtypeStruct((t, GROUP_W), dtype)
    return (pl.BlockSpec((dil, tile // dil, GROUP_W), lambda i: (0, i, 0)),
            jax.ShapeDtypeStruct((dil, t // dil, GROUP_W), dtype))


def _head_lanes(h):
    return slice(h * HEAD_DIM, (h + 1) * HEAD_DIM)


def _head_scratch(tile):
    return pltpu.VMEM((tile, HEAD_DIM), F32)


def _rope_split(z, c_tab, s_tab, t):
    tile = min(CLS_TILE, t)
    n_heads = ATTN_WIDTH // HEAD_DIM

    def kern(zq_ref, zk_ref, zv_ref, c_ref, s_ref, *rest):
        outs, scr = rest[:9], rest[9]
        c, s = c_ref[...], s_ref[...]
        for which, z_ref in enumerate((zq_ref, zk_ref, zv_ref)):
            for h in range(n_heads):
                g, hs = h // HEADS_PER_GROUP, _head_lanes(h % HEADS_PER_GROUP)
                val = z_ref[:, h * HEAD_DIM:(h + 1) * HEAD_DIM]
                if which < 2:
                    val = _rot(val, c, s)
                if g == 0:
                    outs[which][:, hs] = val.astype(BF)
                    continue
                scr[...] = val
                dil = DILATIONS[g]
                for r in range(dil):
                    outs[3 * g + which][r, :, hs] = scr[pl.ds(r, tile // dil, stride=dil), :].astype(BF)

    blocks = [_cls_block(t, tile, DILATIONS[g], BF) for g in range(3) for _ in range(3)]
    zspec = lambda cb: pl.BlockSpec((tile, ATTN_WIDTH), lambda i: (i, cb))
    tab = pl.BlockSpec((tile, HEAD_DIM), lambda i: (i, 0))
    return _pcall("rope", kern, (t // tile,), [zspec(0), zspec(1), zspec(2), tab, tab], [b[0] for b in blocks],
                  [b[1] for b in blocks], [_head_scratch(tile)], ("parallel",), (z, z, z, c_tab, s_tab))


def _merge_classes(outs, lses, t):
    tile = min(CLS_TILE, t)

    def kern(o0, l0, o1, l1, o2, l2, attn_ref, lse_ref, s_o1, s_l1, s_o2, s_l2):
        for h in range(HEADS_PER_GROUP):
            hs = _head_lanes(h)
            for src, dst, dil in ((o1, s_o1, DILATIONS[1]), (l1, s_l1, DILATIONS[1]), (o2, s_o2, DILATIONS[2]),
                                  (l2, s_l2, DILATIONS[2])):
                for r in range(dil):
                    dst[pl.ds(r, tile // dil, stride=dil), :] = src[r, :, hs]
            attn_ref[:, hs], lse_ref[:, hs] = _merge_fn(o0[:, hs], s_o1[...], s_o2[...], l0[:, hs], s_l1[...], s_l2[...])

    blocks = [_cls_block(t, tile, DILATIONS[g], F32) for g in range(3)]
    args = []
    for g in range(3):
        args += [outs[g].reshape(blocks[g][1].shape), lses[g].reshape(blocks[g][1].shape)]
    tok = pl.BlockSpec((tile, GROUP_W), lambda i: (i, 0))
    return _pcall("attn_merge", kern, (t // tile,), [blocks[g][0] for g in range(3) for _ in range(2)], [tok, tok],
                  [jax.ShapeDtypeStruct((t, GROUP_W), F32)] * 2, [_head_scratch(tile)] * 4, ("parallel",), args)


def _attn_bwd_prep(dattn, attn, lse, t):
    tile = min(CLS_TILE, t)

    def kern(da_ref, at_ref, ls_ref, dl0, da1, dl1, ls1, da2, dl2, ls2, s_da, s_dl, s_ls):
        for h in range(HEADS_PER_GROUP):
            hs = _head_lanes(h)
            da = da_ref[:, hs]
            s_da[...] = da
            s_dl[...] = jnp.broadcast_to(jnp.sum(da * at_ref[:, hs], axis=1, keepdims=True), (tile, HEAD_DIM))
            s_ls[...] = ls_ref[:, hs]
            dl0[:, hs] = s_dl[...]
            for oda, odl, ols, dil in ((da1, dl1, ls1, DILATIONS[1]), (da2, dl2, ls2, DILATIONS[2])):
                for r in range(dil):
                    rows = pl.ds(r, tile // dil, stride=dil)
                    oda[r, :, hs] = s_da[rows, :].astype(BF)
                    odl[r, :, hs] = s_dl[rows, :]
                    ols[r, :, hs] = s_ls[rows, :]

    tok = pl.BlockSpec((tile, GROUP_W), lambda i: (i, 0))
    blocks = [(tok, jax.ShapeDtypeStruct((t, GROUP_W), F32))]
    for g in (1, 2):
        blocks += [_cls_block(t, tile, DILATIONS[g], BF), _cls_block(t, tile, DILATIONS[g], F32),
                   _cls_block(t, tile, DILATIONS[g], F32)]
    res = _pcall("attn_bwd_prep", kern, (t // tile,), [tok, tok, tok], [b[0] for b in blocks], [b[1] for b in blocks],
                 [_head_scratch(tile)] * 3, ("parallel",), (dattn, attn, lse))
    flat = [a.reshape(t, GROUP_W) for a in res]
    return flat[0], flat[1:4], flat[4:7]


def _rope_bwd_join(dqs, dks, dvs, c_tab, s_tab, dglu_a, dglu_b, dgate, t, side=None):
    tile = min(256, t)
    n_heads = ATTN_WIDTH // HEAD_DIM

    def kern(q0, q1, q2, k0, k1, k2, v0, v1, v2, c_ref, s_ref, ga_ref, gb_ref, gt_ref, dz_ref, scr):
        c, s = c_ref[...], -s_ref[...]
        for which, srcs in enumerate(((q0, q1, q2), (k0, k1, k2), (v0, v1, v2))):
            for h in range(n_heads):
                g, hs = h // HEADS_PER_GROUP, _head_lanes(h % HEADS_PER_GROUP)
                if g == 0:
                    val = srcs[0][:, hs]
                else:
                    dil = DILATIONS[g]
                    for r in range(dil):
                        scr[pl.ds(r, tile // dil, stride=dil), :] = srcs[g][r, :, hs]
                    val = scr[...]
                if which < 2:
                    val = _rot(val, c, s)
                col = which * ATTN_WIDTH + h * HEAD_DIM
                dz_ref[:, col:col + HEAD_DIM] = val.astype(BF)
        dz_ref[:, GLU_A_COL:GLU_B_COL] = ga_ref[...]
        dz_ref[:, GLU_B_COL:GATE_A_COL] = gb_ref[...]
        dz_ref[:, GATE_A_COL:] = gt_ref[...]

    blocks = [_cls_block(t, tile, DILATIONS[g], F32) for g in range(3)]
    args = [a.reshape(blocks[g][1].shape) for grp in (dqs, dks, dvs) for g, a in enumerate(grp)]
    tab = pl.BlockSpec((tile, HEAD_DIM), lambda i: (i, 0))
    row = lambda w: pl.BlockSpec((tile, w), lambda i: (i, 0))
    return _pcall("rope_bwd", kern, (t // tile,), [blocks[g][0] for _ in range(3) for g in range(3)]
                  + [tab, tab, row(CONV_CH), row(CONV_CH), row(2 * D_MODEL)], [row(IN_WIDTH)],
                  [jax.ShapeDtypeStruct((t, IN_WIDTH), BF)], [_head_scratch(tile)], ("parallel",),
                  (*args, c_tab, s_tab, dglu_a, dglu_b, dgate), side)


def _cross_probs(qh, kh):
    s = _dot_nt(qh, kh) * CROSS_SCALE
    e = jnp.exp(s - jnp.max(s, axis=1, keepdims=True))
    return e, jnp.sum(e, axis=1, keepdims=True)


def _cross_fwd(cq, ckv, t):
    rows = min(512, t)

    def kern(q_ref, kv_ref, o_ref):
        for h in range(CROSS_HEADS):
            hs = slice(h * CROSS_HEAD_DIM, (h + 1) * CROSS_HEAD_DIM)
            vs = slice(D_MODEL + h * CROSS_HEAD_DIM, D_MODEL + (h + 1) * CROSS_HEAD_DIM)
            e, tot = _cross_probs(q_ref[:, hs], kv_ref[:, hs])
            o_ref[:, hs] = (_dot(e.astype(BF), kv_ref[:, vs]) / tot).astype(BF)

    return pl.pallas_call(
        kern, name="cross_fwd", grid=(t // rows,),
        in_specs=[pl.BlockSpec((rows, D_MODEL), lambda i: (i, 0)), pl.BlockSpec((N_MEM, 2 * D_MODEL), lambda i: (0, 0))],
        out_specs=pl.BlockSpec((rows, D_MODEL), lambda i: (i, 0)),
        out_shape=jax.ShapeDtypeStruct((t, D_MODEL), BF),
        compiler_params=_params(("parallel",)),
    )(cq, ckv)


def _cross_bwd(cq, ckv, dco, t):
    rows = min(512, t)

    def kern(q_ref, kv_ref, do_ref, dq_ref, dkv_ref):
        @pl.when(pl.program_id(0) == 0)
        def _():
            dkv_ref[...] = jnp.zeros_like(dkv_ref)
        for h in range(CROSS_HEADS):
            hs = slice(h * CROSS_HEAD_DIM, (h + 1) * CROSS_HEAD_DIM)
            vs = slice(D_MODEL + h * CROSS_HEAD_DIM, D_MODEL + (h + 1) * CROSS_HEAD_DIM)
            qh, kh, vh, doh = q_ref[:, hs], kv_ref[:, hs], kv_ref[:, vs], do_ref[:, hs]
            e, tot = _cross_probs(qh, kh)
            p = e / tot
            dp = _dot_nt(doh, vh)
            ds = (p * (dp - jnp.sum(p * dp, axis=1, keepdims=True)) * CROSS_SCALE).astype(BF)
            dq_ref[:, hs] = _dot(ds, kh).astype(BF)
            dkv_ref[:, hs] += _dot_tn(ds, qh)
            dkv_ref[:, vs] += _dot_tn(p.astype(BF), doh)

    return pl.pallas_call(
        kern, name="cross_bwd", grid=(t // rows,),
        in_specs=[pl.BlockSpec((rows, D_MODEL), lambda i: (i, 0)), pl.BlockSpec((N_MEM, 2 * D_MODEL), lambda i: (0, 0)),
                  pl.BlockSpec((rows, D_MODEL), lambda i: (i, 0))],
        out_specs=[pl.BlockSpec((rows, D_MODEL), lambda i: (i, 0)), pl.BlockSpec((N_MEM, 2 * D_MODEL), lambda i: (0, 0))],
        out_shape=[jax.ShapeDtypeStruct((t, D_MODEL), BF), jax.ShapeDtypeStruct((N_MEM, 2 * D_MODEL), F32)],
        compiler_params=_params(("arbitrary",)),
    )(cq, ckv, dco)


CONV_TILE = 512
CONV_CHUNK = 128
HALO = 32


def _conv_fwd(z, w32, bias, t):
    tile = min(CONV_TILE, t)
    a_cb, b_cb = GLU_A_COL // LANES, GLU_B_COL // LANES
    hb = tile // HALO

    def kern(a_ref, b_ref, ah_ref, bh_ref, w_ref, bias_ref, o_ref, g_scr):
        i = pl.program_id(1)
        g_scr[HALO:, :] = a_ref[...] * _sig(b_ref[...])
        g_scr[:HALO, :] = ah_ref[...] * _sig(bh_ref[...]) * jnp.where(i > 0, 1.0, 0.0)
        for c in range(tile // CONV_CHUNK):
            acc = jnp.broadcast_to(bias_ref[...], (CONV_CHUNK, LANES))
            for j in range(CONV_K):
                lo = c * CONV_CHUNK + HALO - (CONV_K - 1) + j
                acc = acc + w_ref[j:j + 1, :] * g_scr[lo:lo + CONV_CHUNK, :]
            o_ref[c * CONV_CHUNK:(c + 1) * CONV_CHUNK, :] = acc

    def cur(cb):
        return pl.BlockSpec((tile, LANES), lambda j, i: (i, cb + j))

    def prev(cb):
        return pl.BlockSpec((HALO, LANES), lambda j, i: (jnp.maximum(i * hb - 1, 0), cb + j))

    return pl.pallas_call(
        kern, name="conv_fwd", grid=(CONV_CH // LANES, t // tile),
        in_specs=[cur(a_cb), cur(b_cb), prev(a_cb), prev(b_cb),
                  pl.BlockSpec((CONV_KP, LANES), lambda j, i: (0, j)), pl.BlockSpec((1, LANES), lambda j, i: (0, j))],
        out_specs=pl.BlockSpec((tile, LANES), lambda j, i: (i, j)),
        out_shape=jax.ShapeDtypeStruct((t, CONV_CH), F32),
        scratch_shapes=[pltpu.VMEM((tile + HALO, LANES), F32)],
        compiler_params=_params(("parallel", "parallel")),
    )(z, z, z, z, w32, bias)


def _conv_bwd(z, dc1, w32, t, side=None):
    tile = min(CONV_TILE, t)
    a_cb, b_cb = GLU_A_COL // LANES, GLU_B_COL // LANES
    hb = tile // HALO
    n_tiles = t // tile
    n_chunks = tile // CONV_CHUNK

    def kern(a_ref, b_ref, ah_ref, bh_ref, d_ref, dn_ref, w_ref, da_ref, db_ref, dw_ref, g_scr, d_scr):
        i = pl.program_id(1)
        sg = _sig(b_ref[...])
        g_scr[HALO:, :] = a_ref[...] * sg
        g_scr[:HALO, :] = ah_ref[...] * _sig(bh_ref[...]) * jnp.where(i > 0, 1.0, 0.0)
        d_scr[:tile, :] = d_ref[...]
        d_scr[tile:, :] = dn_ref[...] * jnp.where(i < n_tiles - 1, 1.0, 0.0)

        @pl.when(i == 0)
        def _():
            dw_ref[...] = jnp.zeros_like(dw_ref)

        for c in range(n_chunks):
            cs = slice(c * CONV_CHUNK, (c + 1) * CONV_CHUNK)
            acc = jnp.zeros((CONV_CHUNK, LANES), F32)
            for j in range(CONV_K):
                lo = c * CONV_CHUNK + (CONV_K - 1) - j
                acc = acc + w_ref[j:j + 1, :] * d_scr[lo:lo + CONV_CHUNK, :]
            sgc = _sig(b_ref[cs, :])
            da_ref[cs, :] = (acc * sgc).astype(BF)
            db_ref[cs, :] = (acc * a_ref[cs, :] * sgc * (1.0 - sgc)).astype(BF)
        for j in range(CONV_K):
            tot = jnp.zeros((1, LANES), F32)
            for c in range(n_chunks):
                lo = c * CONV_CHUNK + HALO - (CONV_K - 1) + j
                tot = tot + jnp.sum(d_ref[c * CONV_CHUNK:(c + 1) * CONV_CHUNK, :] * g_scr[lo:lo + CONV_CHUNK, :],
                                    axis=0, keepdims=True)
            dw_ref[j:j + 1, :] += tot
        dw_ref[CONV_K:CONV_KP, :] += jnp.sum(d_ref[...], axis=0, keepdims=True)

    def cur(cb):
        return pl.BlockSpec((tile, LANES), lambda j, i: (i, cb + j))

    def prev(cb):
        return pl.BlockSpec((HALO, LANES), lambda j, i: (jnp.maximum(i * hb - 1, 0), cb + j))

    return _pcall(
        "conv_bwd", kern, (CONV_CH // LANES, n_tiles),
        [cur(a_cb), cur(b_cb), prev(a_cb), prev(b_cb), cur(0),
         pl.BlockSpec((HALO, LANES), lambda j, i: (jnp.minimum((i + 1) * hb, t // HALO - 1), j)),
         pl.BlockSpec((CONV_KP, LANES), lambda j, i: (0, j))],
        [pl.BlockSpec((tile, LANES), lambda j, i: (i, j)), pl.BlockSpec((tile, LANES), lambda j, i: (i, j)),
         pl.BlockSpec((CONV_KP, LANES), lambda j, i: (0, j))],
        [jax.ShapeDtypeStruct((t, CONV_CH), BF), jax.ShapeDtypeStruct((t, CONV_CH), BF),
         jax.ShapeDtypeStruct((CONV_KP, CONV_CH), F32)],
        [pltpu.VMEM((tile + HALO, LANES), F32), pltpu.VMEM((tile + HALO, LANES), F32)],
        ("parallel", "arbitrary"), (z, z, z, z, dc1, dc1, w32), side)


def _adam_fn(w, g, m, v):
    m = ADAM_B1 * m + (1.0 - ADAM_B1) * g
    v = ADAM_B2 * v + (1.0 - ADAM_B2) * (g * g)
    m_hat = m / (1.0 - ADAM_B1 ** ADAM_STEP)
    v_hat = v / (1.0 - ADAM_B2 ** ADAM_STEP)
    delta = -ADAM_LR * (m_hat / (jnp.sqrt(v_hat) + ADAM_EPS) + ADAM_WD * w)
    return delta, m, v


def _adam(name, w, g, m, v):
    rows, cols = w.shape
    tile = _ew_tile(rows, cols)
    return _rowcall(name, _adam_fn, rows, tile, [(a, cols, 0) for a in (w, g, m, v)], [], [(cols, F32)] * 3)


def _place():
    x, y, c = lax.axis_index("x"), lax.axis_index("y"), lax.axis_index("c")
    chips = [(1 - x, y), (x, 1 - y), (1 - x, 1 - y)]
    return x, y, c, chips


def _gather_side(shards):
    nw = len(shards)
    chip = 2 * lax.axis_index("x") + lax.axis_index("y")
    staged = [lax.dynamic_update_index_in_dim(jnp.zeros((4,) + s.shape, s.dtype), s, chip, 0) for s in shards]

    def build(_, outs, send_sems, recv_sems):
        x, y, c, chips = _place()
        me = 2 * x + y
        sibling = (x, y, 1 - c)

        def half(w, lead, h):
            n = shards[w].shape[0] // 2
            return outs[w].at[lead, pl.ds(h * n, n)]

        def copy(w, k, part, to):
            return pltpu.make_async_remote_copy(src_ref=part, dst_ref=part, send_sem=send_sems.at[6 * w + k],
                                                recv_sem=recv_sems.at[6 * w + k], device_id=to, device_id_type=MESH)

        def start():
            for w in range(nw):
                for k, (px, py) in enumerate(chips):
                    copy(w, k, half(w, me, c), (px, py, c)).start()

        def finish():
            for w in range(nw):
                for k, (px, py) in enumerate(chips):
                    landed = half(w, 2 * px + py, c)
                    copy(w, k, landed, (px, py, c)).wait_recv()
                    copy(w, 3 + k, landed, sibling).start()
            for w in range(nw):
                for k, (px, py) in enumerate(chips):
                    copy(w, 3 + k, half(w, 2 * px + py, 1 - c), sibling).wait_recv()
            for w in range(nw):
                for k, (px, py) in enumerate(chips):
                    copy(w, k, half(w, me, c), (px, py, c)).wait_send()
                    copy(w, 3 + k, half(w, 2 * px + py, c), sibling).wait_send()

        return start, finish

    return _Side(staged, [jax.ShapeDtypeStruct((4,) + s.shape, s.dtype) for s in shards], 6 * nw, build,
                 aliases={w: w for w in range(nw)})


def _gather8(name, v):
    rows = v.shape[0]

    def body(v_ref, all_ref, sum_ref, send_sems, recv_sems):
        x, y, c, _ = _place()
        me = 4 * x + 2 * y + c
        all_ref[me] = v_ref[...]
        copies = []
        for k in range(1, 8):
            px, py, pc = x ^ (k >> 2), y ^ ((k >> 1) & 1), c ^ (k & 1)
            copies.append(pltpu.make_async_remote_copy(
                src_ref=v_ref, dst_ref=all_ref.at[me], send_sem=send_sems.at[k - 1], recv_sem=recv_sems.at[k - 1],
                device_id=(px, py, pc), device_id_type=MESH))
            copies[-1].start()
        for k in range(1, 8):
            px, py, pc = x ^ (k >> 2), y ^ ((k >> 1) & 1), c ^ (k & 1)
            theirs = all_ref.at[4 * px + 2 * py + pc]
            pltpu.make_async_remote_copy(
                src_ref=theirs, dst_ref=theirs, send_sem=send_sems.at[k - 1], recv_sem=recv_sems.at[k - 1],
                device_id=(px, py, pc), device_id_type=MESH).wait_recv()
        for cp in copies:
            cp.wait_send()
        tot = all_ref[0]
        for d in range(1, 8):
            tot = tot + all_ref[d]
        sum_ref[...] = tot

    vm = pl.BlockSpec(memory_space=pltpu.VMEM)
    return pl.pallas_call(
        body, name=name, in_specs=[vm], out_specs=[vm, vm],
        out_shape=[jax.ShapeDtypeStruct((8, rows, LANES), F32), jax.ShapeDtypeStruct((rows, LANES), F32)],
        scratch_shapes=[pltpu.SemaphoreType.DMA((7,)), pltpu.SemaphoreType.DMA((7,))],
    )(v)


def _region(ref, col_sharded, shape, j, h):
    r, ccols = shape
    if col_sharded:
        return ref.at[pl.ds(h * (r // 2), r // 2), pl.ds(j * (ccols // 4), ccols // 4)]
    n = r // 8
    return ref.at[pl.ds((2 * j + h) * n, n), :]


def _region_shape(col_sharded, shape):
    r, ccols = shape
    return (r // 2, ccols // 4) if col_sharded else (r // 8, ccols)


def _exchange(copies):
    def build(ins, outs, send_sems, recv_sems):
        def start():
            for cp in copies(ins, outs, send_sems, recv_sems):
                cp.start()

        def finish():
            for cp in copies(ins, outs, send_sems, recv_sems):
                cp.wait()

        return start, finish
    return build


def _swap_side(grads, kinds):
    nw = len(grads)

    def copies(ins, theirs, send_sems, recv_sems):
        x, y, c, _ = _place()
        return [pltpu.make_async_remote_copy(
            src_ref=_region(ins[w], kinds[w], grads[w].shape, j, 1 - c), dst_ref=theirs[w].at[j],
            send_sem=send_sems.at[4 * w + j], recv_sem=recv_sems.at[4 * w + j], device_id=(x, y, 1 - c), device_id_type=MESH)
            for w in range(nw) for j in range(4)]

    shapes = [jax.ShapeDtypeStruct((4,) + _region_shape(kinds[w], grads[w].shape), F32) for w in range(nw)]
    return _Side(grads, shapes, 4 * nw, _exchange(copies))


def _kept_halves(grad, col_sharded, c):
    r, ccols = grad.shape
    if col_sharded:
        slab = lax.dynamic_slice_in_dim(grad, c * (r // 2), r // 2, axis=0)
        return slab.reshape(r // 2, 4, ccols // 4).transpose(1, 0, 2)
    return lax.dynamic_index_in_dim(grad.reshape(4, 2, r // 8, ccols), c, axis=1, keepdims=False)


def _scatter_side(parts):
    nw = len(parts)

    def copies(ins, outs, send_sems, recv_sems):
        x, y, c, chips = _place()
        return [pltpu.make_async_remote_copy(
            src_ref=ins[w].at[2 * px + py], dst_ref=outs[w].at[k], send_sem=send_sems.at[3 * w + k],
            recv_sem=recv_sems.at[3 * w + k], device_id=(px, py, c), device_id_type=MESH)
            for w in range(nw) for k, (px, py) in enumerate(chips)]

    shapes = [jax.ShapeDtypeStruct((3,) + p.shape[1:], p.dtype) for p in parts]
    return _Side(parts, shapes, 3 * nw, _exchange(copies))


def _share_side(halves):
    nw = len(halves)

    def copies(ins, outs, send_sems, recv_sems):
        x, y, c, _ = _place()
        return [pltpu.make_async_remote_copy(
            src_ref=ins[w], dst_ref=outs[w], send_sem=send_sems.at[w], recv_sem=recv_sems.at[w],
            device_id=(x, y, 1 - c), device_id_type=MESH) for w in range(nw)]

    return _Side(halves, [jax.ShapeDtypeStruct(h.shape, F32) for h in halves], nw, _exchange(copies))


def _ew_tile(rows, cols):
    limit = max(8, (256 * 1024) // cols)
    return max(d for d in range(8, min(rows, limit) + 1, 8) if rows % d == 0)


def _pair_sums(names, grads, theirs):
    c = lax.axis_index("c")
    parts, parts_bf = {}, {}
    for n, other in zip(names, theirs):
        _, rr, cc = other.shape
        a = _kept_halves(grads[n], COL_SHARDED[n], c).reshape(4 * rr, cc)
        p, pb = _rowcall(f"grad_pair_sum_{n}", lambda u, v: (u + v, u + v), 4 * rr, _ew_tile(4 * rr, cc),
                         [(a, cc, 0), (other.reshape(4 * rr, cc), cc, 0)], [], [(cc, F32), (cc, BF)])
        parts[n], parts_bf[n] = p.reshape(4, rr, cc), pb.reshape(4, rr, cc)
    return parts, parts_bf


def _chip_sums(names, parts, landed):
    chip = 2 * lax.axis_index("x") + lax.axis_index("y")
    halves = {}
    for n, got in zip(names, landed):
        _, rr, cc = got.shape
        tile = _ew_tile(rr, cc)
        own = lax.dynamic_index_in_dim(parts[n], chip, axis=0, keepdims=False)
        flat = got.reshape(3 * rr, cc)
        nb = rr // tile
        (halves[n],) = _rowcall(f"grad_chip_sum_{n}", lambda own, k0, k1, k2: (((own + k0) + k1) + k2,), rr, tile,
                                [(own, cc, 0), (flat, cc, 0, 0), (flat, cc, 0, nb), (flat, cc, 0, 2 * nb)], [], [(cc, F32)])
    return halves


def _both_halves(mine, other):
    c = lax.axis_index("c")
    return jnp.where(c == 0, jnp.concatenate([mine, other], axis=0), jnp.concatenate([other, mine], axis=0))


BIG = ("w_in", "w_attn_proj", "w_conv_proj", "w_out", "w_cq", "w_ckv", "w_co", "w_up", "w_down")
COL_SHARDED = {"w_in": True, "w_attn_proj": True, "w_conv_proj": True, "w_out": False, "w_cq": False,
               "w_ckv": True, "w_co": False, "w_up": True, "w_down": False}
SMALL = ("g_mix", "b_gate", "conv_b", "conv_ln_g", "conv_ln_b", "g_cross", "g_mem", "g_mlp", "g_final")
ORDER = ("g_mix", "w_in", "b_gate", "conv_w", "conv_b", "conv_ln_g", "conv_ln_b", "w_attn_proj", "w_conv_proj", "w_out",
         "g_cross", "g_mem", "w_cq", "w_ckv", "w_co", "g_mlp", "w_up", "w_down", "g_final")


def _pad_rows(flat, rows):
    return jnp.pad(flat, (0, rows * LANES - flat.shape[0])).reshape(rows, LANES)


REST = tuple(n for n in BIG if n != "w_in")
SCATTER_GROUPS = (("w_down", "w_attn_proj"), ("w_up", "w_conv_proj"), ("w_ckv", "w_out", "w_cq", "w_co"))


def _local_step(x, mem, tgt, shards, small, conv_w_full):
    t = x.shape[0]
    tr = 256
    c_tab, s_tab = _rope_tables(t)
    row = lambda v: v.reshape(1, -1)
    g_mix, g_cross, g_mem, g_mlp, g_final = (row(small[n]) for n in ("g_mix", "g_cross", "g_mem", "g_mlp", "g_final"))
    b_gate, conv_b, ln_g, ln_b = (row(small[n]) for n in ("b_gate", "conv_b", "conv_ln_g", "conv_ln_b"))
    w32 = jnp.pad(conv_w_full, ((0, CONV_KP - CONV_K), (0, 0)))

    (u,), (w_in_all,) = _rowcall("mix_norm", lambda a, g: (_rms(a, g),), t, tr, [(x, D_MODEL, 0)], [g_mix], [(D_MODEL, BF)],
                                 side=_gather_side([shards["w_in"]]))
    z, gathered = _mm("in_proj", u, w_in_all, side=_gather_side([shards[n] for n in REST]))
    wfull = {"w_in": w_in_all}
    for n, g in zip(REST, gathered):
        wfull[n] = g if COL_SHARDED[n] else g.reshape(1, 4 * g.shape[1], g.shape[2])
    qkv = _rope_split(z, c_tab, s_tab, t)
    qkv_cls, outs, lses = [], [], []
    for g, dil in enumerate(DILATIONS):
        ops = tuple((a.reshape(t, GROUP_W), 0) for a in qkv[3 * g:3 * g + 3])
        qkv_cls.append(ops)
        o_g, l_g = _attn_fwd(f"attn_fwd_{g}", *ops, t, dil)
        outs.append(o_g)
        lses.append(l_g)
    attn, lse = _merge_classes(outs, lses, t)
    y_attn = _mm("attn_proj", attn, wfull["w_attn_proj"])
    c1 = _conv_fwd(z, w32, conv_b, t)
    (c2,) = _rowcall("conv_ln_silu", _ln_silu_fn, t, tr, [(c1, CONV_CH, 0)], [ln_g, ln_b], [(CONV_CH, BF)])
    y_conv = _mm("conv_proj", c2, wfull["w_conv_proj"])
    gate_ins = [(z, D_MODEL, GATE_A_COL // D_MODEL), (z, D_MODEL, GATE_B_COL // D_MODEL)]
    (merged,) = _rowcall("gate", _gate_fn, t, tr, [(y_attn, D_MODEL, 0), (y_conv, D_MODEL, 0)] + gate_ins, [b_gate],
                         [(D_MODEL, BF)])
    add = lambda acc, res: (res + acc,)
    x1 = _mm("out_proj", merged, wfull["w_out"], extras=(x,), epi=add)

    (uq,) = _rowcall("cross_norm", lambda a, g: (_rms(a, g),), t, tr, [(x1, D_MODEL, 0)], [g_cross], [(D_MODEL, BF)])
    (mn,) = _rowcall("mem_norm", lambda a, g: (_rms(a, g),), N_MEM, N_MEM, [(mem, D_MODEL, 0)], [g_mem], [(D_MODEL, BF)])
    cq = _mm("cross_q", uq, wfull["w_cq"], out_dtypes=(BF,))
    ckv = _mm("cross_kv", mn, wfull["w_ckv"], out_dtypes=(BF,))
    co = _cross_fwd(cq, ckv, t)
    x2 = _mm("cross_out", co, wfull["w_co"], extras=(x1,), epi=add)

    (um,) = _rowcall("mlp_norm", lambda a, g: (_rms(a, g),), t, tr, [(x2, D_MODEL, 0)], [g_mlp], [(D_MODEL, BF)])
    hpre, hact = _mm("mlp_up", um, wfull["w_up"], out_dtypes=(F32, BF),
                     epi=lambda acc: (acc, jnp.square(jnp.maximum(acc, 0.0))))
    x3 = _mm("mlp_down", hact, wfull["w_down"], extras=(x2,), epi=add)
    d3, loss_row, dg_final = _rowcall("final_norm_loss", _final_fn, t, tr, [(x3, D_MODEL, 0), (tgt, D_MODEL, 0)], [g_final],
                                      [(D_MODEL, F32)], accs=[(1, LANES), (1, D_MODEL)])

    gw = {}
    dhp = _mm("mlp_down_bwd", d3, wfull["w_down"], nt=True, extras=(hpre,), out_dtypes=(BF,),
              epi=lambda acc, hp: (acc * 2.0 * jnp.maximum(hp, 0.0),))
    gw["w_down"] = _mm_tn("mlp_down_wgrad", hact, d3)
    dum = _mm("mlp_up_bwd", dhp, wfull["w_up"], nt=True)
    gw["w_up"] = _mm_tn("mlp_up_wgrad", um, dhp)

    def norm_bwd(a, dn, dres, g):
        dx, dg = _rms_bwd(a, g, dn)
        return dres + dx, dg

    d2, dg_mlp = _rowcall("mlp_norm_bwd", norm_bwd, t, tr, [(x2, D_MODEL, 0), (dum, D_MODEL, 0), (d3, D_MODEL, 0)], [g_mlp],
                          [(D_MODEL, F32)], accs=[(1, D_MODEL)])

    dco = _mm("cross_out_bwd", d2, wfull["w_co"], nt=True, out_dtypes=(BF,))
    gw["w_co"] = _mm_tn("cross_out_wgrad", co, d2)
    dcq, dckv = _cross_bwd(cq, ckv, dco, t)
    gw["w_cq"] = _mm_tn("cross_q_wgrad", uq, dcq)
    duq = _mm("cross_q_bwd", dcq, wfull["w_cq"], nt=True)
    d1, dg_cross = _rowcall("cross_norm_bwd", norm_bwd, t, tr, [(x1, D_MODEL, 0), (duq, D_MODEL, 0), (d2, D_MODEL, 0)],
                            [g_cross], [(D_MODEL, F32)], accs=[(1, D_MODEL)])
    gw["w_ckv"] = _mm_tn("cross_kv_wgrad", mn, dckv, tk=N_MEM)
    dmn = _mm("cross_kv_bwd", dckv, wfull["w_ckv"], nt=True)
    (dg_mem,) = _rowcall("mem_norm_bwd", lambda a, dn, g: (_rms_bwd(a, g, dn)[1],), N_MEM, N_MEM,
                         [(mem, D_MODEL, 0), (dmn, D_MODEL, 0)], [g_mem], [], accs=[(1, D_MODEL)])

    dmerged = _mm("out_proj_bwd", d1, wfull["w_out"], nt=True)
    gw["w_out"] = _mm_tn("out_proj_wgrad", merged, d1)
    dya, dyc, dgate, dbg = _rowcall("gate_bwd", _gate_bwd_fn, t, tr,
                                    [(dmerged, D_MODEL, 0), (y_attn, D_MODEL, 0), (y_conv, D_MODEL, 0)] + gate_ins, [b_gate],
                                    [(D_MODEL, BF), (D_MODEL, BF), (2 * D_MODEL, BF)], accs=[(1, 2 * D_MODEL)])
    gw["w_attn_proj"] = _mm_tn("attn_proj_wgrad", attn, dya)
    dattn = _mm("attn_proj_bwd", dya, wfull["w_attn_proj"], nt=True)
    gw["w_conv_proj"] = _mm_tn("conv_proj_wgrad", c2, dyc)
    dc2 = _mm("conv_proj_bwd", dyc, wfull["w_conv_proj"], nt=True)
    dc1, dlng, dlnb = _rowcall("conv_ln_silu_bwd", _ln_silu_bwd_fn, t, tr, [(c1, CONV_CH, 0), (dc2, CONV_CH, 0)], [ln_g, ln_b],
                               [(CONV_CH, F32)], accs=[(1, CONV_CH), (1, CONV_CH)])
    (dglu_a, dglu_b, dconv), theirs = _conv_bwd(z, dc1, w32, t, side=_swap_side([gw[n] for n in REST],
                                                                                  [COL_SHARDED[n] for n in REST]))
    parts, parts_bf = _pair_sums(REST, gw, theirs)

    dl0, cls1, cls2 = _attn_bwd_prep(dattn, attn, lse, t)
    dqs, dks, dvs, landed = [], [], [], {}
    for g, (dil, (da_c, dl_c, ls_c)) in enumerate(zip(DILATIONS, ((dattn, dl0, lse), cls1, cls2))):
        (dq_g, dk_g, dv_g), got = _attn_bwd(f"attn_bwd_{g}", *qkv_cls[g], da_c, dl_c, ls_c, t, dil,
                                            side=_scatter_side([parts_bf[n] for n in SCATTER_GROUPS[g]]))
        landed.update(zip(SCATTER_GROUPS[g], got))
        dqs.append(dq_g)
        dks.append(dk_g)
        dvs.append(dv_g)
    halves = _chip_sums(REST, parts, [landed[n] for n in REST])
    (dz,), others = _rope_bwd_join(dqs, dks, dvs, c_tab, s_tab, dglu_a, dglu_b, dgate, t,
                                   side=_share_side([halves[n] for n in REST]))
    gshard = {n: _both_halves(halves[n], o) for n, o in zip(REST, others)}

    gw_in = {"w_in": _mm_tn("in_proj_wgrad", u, dz)}
    du_a, theirs = _mm("in_proj_bwd_a", dz, wfull["w_in"], nt=True, k_part=(0, 2), side=_swap_side([gw_in["w_in"]], [True]))
    parts, parts_bf = _pair_sums(("w_in",), gw_in, theirs)
    du, landed_in = _mm("in_proj_bwd_b", dz, wfull["w_in"], nt=True, k_part=(1, 2), extras=(du_a,), epi=add,
                        side=_scatter_side([parts_bf["w_in"]]))
    halves = _chip_sums(("w_in",), parts, landed_in)
    (gx, dg_mix), others = _rowcall("mix_norm_bwd", norm_bwd, t, tr, [(x, D_MODEL, 0), (du, D_MODEL, 0), (d1, D_MODEL, 0)],
                                    [g_mix], [(D_MODEL, F32)], accs=[(1, D_MODEL)], side=_share_side([halves["w_in"]]))
    gshard["w_in"] = _both_halves(halves["w_in"], others[0])

    gsmall = {"g_mix": dg_mix, "b_gate": dbg, "conv_b": dconv[CONV_K:CONV_K + 1], "conv_ln_g": dlng, "conv_ln_b": dlnb,
              "g_cross": dg_cross, "g_mem": dg_mem, "g_mlp": dg_mlp, "g_final": dg_final, "conv_w": dconv[:CONV_K]}
    return loss_row, gx, gshard, gsmall


def kernel(x, mem, g_mix, w_in, b_gate, conv_w, conv_b, conv_ln_g, conv_ln_b, w_attn_proj, w_conv_proj, w_out, g_cross, g_mem, w_cq, w_ckv, w_co, g_mlp, w_up, w_down, g_final, loss_target, m_g_mix, m_w_in, m_b_gate, m_conv_w, m_conv_b, m_conv_ln_g, m_conv_ln_b, m_w_attn_proj, m_w_conv_proj, m_w_out, m_g_cross, m_g_mem, m_w_cq, m_w_ckv, m_w_co, m_g_mlp, m_w_up, m_w_down, m_g_final, v_g_mix, v_w_in, v_b_gate, v_conv_w, v_conv_b, v_conv_ln_g, v_conv_ln_b, v_w_attn_proj, v_w_conv_proj, v_w_out, v_g_cross, v_g_mem, v_w_cq, v_w_ckv, v_w_co, v_g_mlp, v_w_up, v_w_down, v_g_final):
    w = dict(g_mix=g_mix, w_in=w_in, b_gate=b_gate, conv_w=conv_w, conv_b=conv_b, conv_ln_g=conv_ln_g, conv_ln_b=conv_ln_b,
             w_attn_proj=w_attn_proj, w_conv_proj=w_conv_proj, w_out=w_out, g_cross=g_cross, g_mem=g_mem, w_cq=w_cq,
             w_ckv=w_ckv, w_co=w_co, g_mlp=g_mlp, w_up=w_up, w_down=w_down, g_final=g_final)
    mo = dict(g_mix=m_g_mix, w_in=m_w_in, b_gate=m_b_gate, conv_w=m_conv_w, conv_b=m_conv_b, conv_ln_g=m_conv_ln_g,
              conv_ln_b=m_conv_ln_b, w_attn_proj=m_w_attn_proj, w_conv_proj=m_w_conv_proj, w_out=m_w_out, g_cross=m_g_cross,
              g_mem=m_g_mem, w_cq=m_w_cq, w_ckv=m_w_ckv, w_co=m_w_co, g_mlp=m_g_mlp, w_up=m_w_up, w_down=m_w_down,
              g_final=m_g_final)
    vo = dict(g_mix=v_g_mix, w_in=v_w_in, b_gate=v_b_gate, conv_w=v_conv_w, conv_b=v_conv_b, conv_ln_g=v_conv_ln_g,
              conv_ln_b=v_conv_ln_b, w_attn_proj=v_w_attn_proj, w_conv_proj=v_w_conv_proj, w_out=v_w_out, g_cross=v_g_cross,
              g_mem=v_g_mem, w_cq=v_w_cq, w_ckv=v_w_ckv, w_co=v_w_co, g_mlp=v_g_mlp, w_up=v_w_up, w_down=v_w_down,
              g_final=v_g_final)
    shapes = {n: w[n].shape for n in ORDER}
    two_d = lambda a: a.reshape(a.shape[-2], a.shape[-1])
    chip = 2 * lax.axis_index("x") + lax.axis_index("y")

    shards = {n: two_d(w[n]).astype(BF) for n in BIG}
    cw_rows = 48
    cw_all, _ = _gather8("gather_conv_w", _pad_rows(conv_w.reshape(-1), cw_rows))
    cw_shard = CONV_K * (CONV_CH // 4)
    conv_w_full = jnp.concatenate(
        [cw_all[2 * j].reshape(-1)[:cw_shard].reshape(CONV_K, CONV_CH // 4) for j in range(4)], axis=1)

    small = {n: w[n] for n in SMALL}
    loss_row, gx, gshard, gsmall = _local_step(two_d(x), two_d(mem), two_d(loss_target), shards, small, conv_w_full)
    loss = lax.psum(loss_row[0, 0], ("x", "y", "c"))

    small_names = SMALL + ("conv_w",)
    flat = jnp.concatenate([gsmall[n].reshape(-1) for n in small_names])
    sm_rows = -(-flat.shape[0] // (8 * LANES)) * 8
    _, sm_sum = _gather8("reduce_small_grads", _pad_rows(flat, sm_rows))
    sm_sum = sm_sum.reshape(-1)
    off = 0
    for n in small_names:
        size = gsmall[n].size
        gshard[n] = sm_sum[off:off + size].reshape(gsmall[n].shape)
        off += size
    gshard["conv_w"] = lax.dynamic_slice_in_dim(gshard["conv_w"], chip * (CONV_CH // 4), CONV_CH // 4, axis=1)

    grads, deltas, new_m, new_v = {}, {}, {}, {}
    for n in BIG:
        d, m2, v2 = _adam(f"adamw_{n}", two_d(w[n]), gshard[n], two_d(mo[n]), two_d(vo[n]))
        grads[n], deltas[n], new_m[n], new_v[n] = (a.reshape(shapes[n]) for a in (gshard[n], d, m2, v2))
    pack = lambda src: jnp.concatenate([src[n].reshape(-1) for n in small_names])
    n_small = sum(w[n].size for n in small_names)
    ad_rows = -(-n_small // (8 * LANES)) * 8
    d, m2, v2 = _adam("adamw_small", *[_pad_rows(pack(src), ad_rows) for src in (w, gshard, mo, vo)])
    off = 0
    for n in small_names:
        size = w[n].size
        grads[n] = gshard[n].reshape(shapes[n])
        deltas[n], new_m[n], new_v[n] = (a.reshape(-1)[off:off + size].reshape(shapes[n]) for a in (d, m2, v2))
        off += size

    return (loss, gx.reshape(x.shape), *[grads[n] for n in ORDER], *[deltas[n] for n in ORDER],
            *[new_m[n] for n in ORDER], *[new_v[n] for n in ORDER])
```

```python
import functools

import jax
import jax.numpy as jnp
from jax import lax
from jax.experimental import pallas as pl
from jax.experimental.pallas import tpu as pltpu

F32 = jnp.float32
BF = jnp.bfloat16

D_MODEL = 1024
N_MEM = 256
HEAD_DIM = 128
HEADS_PER_GROUP = 4
DILATIONS = (1, 4, 16)
BLK = 128
GROUP_W = HEADS_PER_GROUP * HEAD_DIM
ATTN_WIDTH = 3 * GROUP_W
ROT_DIM = 32
ROPE_THETA = 500000.0
CONV_CH = 768
CONV_K = 31
CONV_KP = 32
IN_WIDTH = 8192
CROSS_HEADS = 4
CROSS_HEAD_DIM = 256
D_FF = 4096
EPS = 1e-6
ATTN_SCALE = HEAD_DIM ** -0.5
CROSS_SCALE = CROSS_HEAD_DIM ** -0.5
NEG = -1e30

ADAM_LR = 0.001
ADAM_B1 = 0.9
ADAM_B2 = 0.999
ADAM_EPS = 1e-08
ADAM_WD = 0.01
ADAM_STEP = 10

LANES = 128
VMEM_LIMIT = 56 * 1024 * 1024
MESH = pl.DeviceIdType.MESH
ANY = pl.BlockSpec(memory_space=pl.ANY)

GLU_A_COL = 3 * ATTN_WIDTH
GLU_B_COL = GLU_A_COL + CONV_CH
GATE_A_COL = GLU_B_COL + CONV_CH
GATE_B_COL = GATE_A_COL + D_MODEL


def _params(sem=None):
    return pltpu.CompilerParams(dimension_semantics=sem, vmem_limit_bytes=VMEM_LIMIT)


def _dot(a, b):
    return lax.dot_general(a, b, (((1,), (0,)), ((), ())), preferred_element_type=F32)


def _dot_nt(a, b):
    return lax.dot_general(a, b, (((1,), (1,)), ((), ())), preferred_element_type=F32)


def _dot_tn(a, b):
    return lax.dot_general(a, b, (((0,), (0,)), ((), ())), preferred_element_type=F32)


def _sig(x):
    return 1.0 / (1.0 + jnp.exp(-x))


class _Side:
    def __init__(self, arrays, out_shapes, n_sems, build, aliases=None):
        self.arrays, self.out_shapes, self.n_sems, self.build = list(arrays), list(out_shapes), n_sems, build
        self.aliases = aliases or {}


def _pcall(name, kern, grid, in_specs, out_specs, out_shape, scratch_shapes, sem, args, side=None):
    in_specs, out_specs, out_shape, scratch_shapes = list(in_specs), list(out_specs), list(out_shape), list(scratch_shapes)
    if side is None:
        return pl.pallas_call(kern, name=name, grid=grid, in_specs=in_specs, out_specs=out_specs, out_shape=out_shape,
                              scratch_shapes=scratch_shapes, compiler_params=_params(sem))(*args)
    ni, no, nsc = len(in_specs), len(out_specs), len(scratch_shapes)
    nsi, nso = len(side.arrays), len(side.out_shapes)

    def wrapped(*refs):
        ins, side_ins = refs[:ni], refs[ni:ni + nsi]
        outs, side_outs = refs[ni + nsi:ni + nsi + no], refs[ni + nsi + no:ni + nsi + no + nso]
        scratch = refs[ni + nsi + no + nso:ni + nsi + no + nso + nsc]
        send_sems, recv_sems = refs[-2:]
        start, finish = side.build(side_ins, side_outs, send_sems, recv_sems)
        if grid:
            first = functools.reduce(jnp.logical_and, [pl.program_id(a) == 0 for a in range(len(grid))])
            last = functools.reduce(jnp.logical_and, [pl.program_id(a) == g - 1 for a, g in enumerate(grid)])
            pl.when(first)(start)
            kern(*ins, *outs, *scratch)
            pl.when(last)(finish)
        else:
            start()
            kern(*ins, *outs, *scratch)
            finish()

    res = pl.pallas_call(
        wrapped, name=name, grid=grid, in_specs=in_specs + [ANY] * nsi, out_specs=out_specs + [ANY] * nso,
        out_shape=out_shape + side.out_shapes,
        scratch_shapes=scratch_shapes + [pltpu.SemaphoreType.DMA((side.n_sems,)), pltpu.SemaphoreType.DMA((side.n_sems,))],
        input_output_aliases={ni + k: no + v for k, v in side.aliases.items()},
        compiler_params=_params(("arbitrary",) * len(grid) if grid else None),
    )(*args, *side.arrays)
    return res[:no], res[no:]


def _rowcall(name, fn, n_rows, tile, ins, params, outs, accs=(), side=None):
    tile = min(tile, n_rows)
    ni, npar, no, na = len(ins), len(params), len(outs), len(accs)

    def kern(*refs):
        in_refs = refs[:ni + npar]
        o_refs = refs[ni + npar:ni + npar + no]
        a_refs = refs[ni + npar + no:]
        vals = fn(*[r[...] for r in in_refs])
        for r, v in zip(o_refs, vals[:no]):
            r[...] = v.astype(r.dtype)
        if na:
            @pl.when(pl.program_id(0) == 0)
            def _():
                for r in a_refs:
                    r[...] = jnp.zeros_like(r)
            for r, v in zip(a_refs, vals[no:]):
                r[...] += v

    in_specs = []
    arrays = []
    for spec in ins:
        arr, width, cb = spec[0], spec[1], spec[2]
        rb = spec[3] if len(spec) > 3 else 0
        in_specs.append(pl.BlockSpec((tile, width), functools.partial(lambda i, cb, rb: (i + rb, cb), cb=cb, rb=rb)))
        arrays.append(arr)
    for p in params:
        in_specs.append(pl.BlockSpec(p.shape, lambda i: (0, 0)))
        arrays.append(p)
    out_specs = [pl.BlockSpec((tile, w), lambda i: (i, 0)) for w, _ in outs]
    out_specs += [pl.BlockSpec(s, lambda i: (0, 0)) for s in accs]
    out_shape = [jax.ShapeDtypeStruct((n_rows, w), dt) for w, dt in outs]
    out_shape += [jax.ShapeDtypeStruct(s, F32) for s in accs]
    return _pcall(name, kern, (n_rows // tile,), in_specs, out_specs, out_shape, [],
                  ("arbitrary",) if na else ("parallel",), arrays, side)


def _mm(name, a, w3, *, nt=False, extras=(), epi=None, out_dtypes=(F32,), tm=None, tn=None, tk=None, k_part=(0, 1),
        side=None):
    m, ka = a.shape
    ns, r, cs = w3.shape
    if not nt:
        k_dim, n = r, ns * cs
        tn = tn or min(cs, 1024)
        tk = tk or min(k_dim, 1024)
    else:
        k_dim, n = ns * cs, r
        tn = tn or min(r, 1024)
        tk = tk or min(cs, 1024)
    assert ka == k_dim, (name, a.shape, w3.shape)
    nk = k_dim // tk // k_part[1]
    k0 = k_part[0] * nk
    if not nt:
        nbs = cs // tn
        w_spec = pl.BlockSpec((None, tk, tn), lambda i, j, k: (j // nbs, k + k0, j % nbs))
    else:
        kbs = cs // tk
        w_spec = pl.BlockSpec((None, tn, tk), lambda i, j, k: ((k + k0) // kbs, j, (k + k0) % kbs))
    tm = tm or min(m, 1024)
    ne, no = len(extras), len(out_dtypes)

    def kern(a_ref, w_ref, *rest):
        e_refs = rest[:ne]
        o_refs = rest[ne:ne + no]

        def part():
            av = a_ref[...].astype(BF)
            return _dot_nt(av, w_ref[...]) if nt else _dot(av, w_ref[...])

        def finish(res):
            vals = epi(res, *[e[...] for e in e_refs]) if epi else (res,)
            for o, v in zip(o_refs, vals):
                o[...] = v.astype(o.dtype)

        if nk == 1:
            finish(part())
            return
        acc = rest[ne + no]
        k = pl.program_id(2)

        @pl.when(k == 0)
        def _():
            acc[...] = part()

        @pl.when(jnp.logical_and(k > 0, k < nk - 1))
        def _():
            acc[...] += part()

        @pl.when(k == nk - 1)
        def _():
            finish(acc[...] + part())

    in_specs = [pl.BlockSpec((tm, tk), lambda i, j, k: (i, k + k0)), w_spec]
    in_specs += [pl.BlockSpec((tm, tn), lambda i, j, k: (i, j)) for _ in extras]
    res = _pcall(name, kern, (m // tm, n // tn, nk), in_specs,
                 [pl.BlockSpec((tm, tn), lambda i, j, k: (i, j)) for _ in out_dtypes],
                 [jax.ShapeDtypeStruct((m, n), dt) for dt in out_dtypes], [pltpu.VMEM((tm, tn), F32)] if nk > 1 else [],
                 ("parallel", "parallel", "arbitrary"), (a, w3, *extras), side)
    if side is not None:
        return (res[0][0] if no == 1 else res[0]), res[1]
    return res[0] if no == 1 else res


def _mm_tn(name, a, b, tm=None, tn=None, tk=None):
    t, ka = a.shape
    _, n = b.shape
    tm = tm or min(ka, 1024)
    tn = tn or min(n, 1024)
    tk = tk or min(t, 512)

    def kern(a_ref, b_ref, o_ref):
        def part():
            return _dot_tn(a_ref[...].astype(BF), b_ref[...].astype(BF))

        @pl.when(pl.program_id(2) == 0)
        def _():
            o_ref[...] = part()

        @pl.when(pl.program_id(2) > 0)
        def _():
            o_ref[...] += part()

    return pl.pallas_call(
        kern, name=name, grid=(ka // tm, n // tn, t // tk),
        in_specs=[pl.BlockSpec((tk, tm), lambda i, j, k: (k, i)), pl.BlockSpec((tk, tn), lambda i, j, k: (k, j))],
        out_specs=pl.BlockSpec((tm, tn), lambda i, j, k: (i, j)),
        out_shape=jax.ShapeDtypeStruct((ka, n), F32),
        compiler_params=_params(("parallel", "parallel", "arbitrary")),
    )(a, b)


def _rms(x, g):
    return x * lax.rsqrt(jnp.mean(x * x, axis=-1, keepdims=True) + EPS) * g


def _rms_bwd(x, g, dy):
    r = lax.rsqrt(jnp.mean(x * x, axis=-1, keepdims=True) + EPS)
    xh = x * r
    dxh = dy * g
    dx = r * (dxh - xh * jnp.mean(dxh * xh, axis=-1, keepdims=True))
    return dx, jnp.sum(dy * xh, axis=0, keepdims=True)


def _rot(t, c, s):
    lane = lax.broadcasted_iota(jnp.int32, t.shape, 1)
    swapped = jnp.where(lane < ROT_DIM // 2, pltpu.roll(t, HEAD_DIM - ROT_DIM // 2, 1), pltpu.roll(t, ROT_DIM // 2, 1))
    return t * c + swapped * s


def _rope_tables(t):
    half = ROT_DIM // 2
    pos = jnp.arange(t, dtype=F32)
    inv_freq = ROPE_THETA ** (-jnp.arange(0, ROT_DIM, 2, dtype=F32) / ROT_DIM)
    ang = pos[:, None] * inv_freq[None, :]
    cos, sin = jnp.cos(ang), jnp.sin(ang)
    ones = jnp.ones((t, HEAD_DIM - ROT_DIM), F32)
    c_tab = jnp.concatenate([cos, cos, ones], axis=1)
    s_tab = jnp.concatenate([-sin, sin, 0.0 * ones], axis=1)
    return c_tab, s_tab


def _merge_fn(o0, o1, o2, l0, l1, l2):
    m = jnp.maximum(jnp.maximum(l0, l1), l2)
    e0, e1, e2 = jnp.exp(l0 - m), jnp.exp(l1 - m), jnp.exp(l2 - m)
    tot = e0 + e1 + e2
    return (e0 * o0 + e1 * o1 + e2 * o2) / tot, m + jnp.log(tot)


def _ln_parts(c1):
    mu = jnp.mean(c1, axis=-1, keepdims=True)
    xc = c1 - mu
    r = lax.rsqrt(jnp.mean(xc * xc, axis=-1, keepdims=True) + EPS)
    return xc * r, r


def _ln_silu_fn(c1, g, b):
    xh, _ = _ln_parts(c1)
    yl = xh * g + b
    return (yl * _sig(yl),)


def _ln_silu_bwd_fn(c1, dout, g, b):
    xh, r = _ln_parts(c1)
    yl = xh * g + b
    s = _sig(yl)
    dyl = dout * (s + yl * s * (1.0 - s))
    dxh = dyl * g
    dx = r * (dxh - jnp.mean(dxh, axis=-1, keepdims=True) - xh * jnp.mean(dxh * xh, axis=-1, keepdims=True))
    return dx, jnp.sum(dyl * xh, axis=0, keepdims=True), jnp.sum(dyl, axis=0, keepdims=True)


def _gate_fn(ya, yc, ga, gb, bg):
    sa = _sig(ga + bg[:, :D_MODEL])
    sb = _sig(gb + bg[:, D_MODEL:])
    return (sa * ya + sb * yc,)


def _gate_bwd_fn(dm, ya, yc, ga, gb, bg):
    sa = _sig(ga + bg[:, :D_MODEL])
    sb = _sig(gb + bg[:, D_MODEL:])
    dga = dm * ya * sa * (1.0 - sa)
    dgb = dm * yc * sb * (1.0 - sb)
    dgate = jnp.concatenate([dga, dgb], axis=1)
    return dm * sa, dm * sb, dgate, jnp.sum(dgate, axis=0, keepdims=True)


def _final_fn(x3, tgt, g):
    err = _rms(x3, g) - tgt
    lrow = jnp.sum(err * err, axis=-1, keepdims=True) * (0.5 / D_MODEL)
    lsum = jnp.sum(lrow, axis=0, keepdims=True)
    dx, dg = _rms_bwd(x3, g, err * (1.0 / D_MODEL))
    return dx, dx, jnp.broadcast_to(lsum, (1, LANES)), dg


def _attn_geometry(t, dil):
    cls = t // dil
    rows = min(4 * BLK, cls)
    return rows, rows // BLK, cls // rows


def _head_lanes(h):
    return slice(h * HEAD_DIM, (h + 1) * HEAD_DIM)


def _band_mask():
    row = lax.broadcasted_iota(jnp.int32, (BLK, 2 * BLK), 0)
    col = lax.broadcasted_iota(jnp.int32, (BLK, 2 * BLK), 1)
    return jnp.logical_and(col >= row, col <= row + BLK), col


def _stage_window(scr, halo_ref, cur_ref):
    scr[:BLK, :] = halo_ref[...]
    scr[BLK:, :] = cur_ref[...]


def _attn_fwd(name, q, k, v, t, dil):
    rows, nbk, spc = _attn_geometry(t, dil)

    def kern(q_ref, k_ref, kh_ref, v_ref, vh_ref, o_ref, l_ref, k_scr, v_scr):
        i = pl.program_id(0)
        first_shift = jnp.where(i % spc == 0, BLK, 0)
        _stage_window(k_scr, kh_ref, k_ref)
        _stage_window(v_scr, vh_ref, v_ref)
        band, col = _band_mask()
        band_first = jnp.logical_and(band, col >= first_shift)
        for h in range(HEADS_PER_GROUP):
            hs = _head_lanes(h)
            for b in range(nbk):
                rs, win = slice(b * BLK, (b + 1) * BLK), slice(b * BLK, (b + 2) * BLK)
                s = jnp.where(band_first if b == 0 else band, _dot_nt(q_ref[rs, hs], k_scr[win, hs]) * ATTN_SCALE, NEG)
                m = jnp.max(s, axis=1, keepdims=True)
                p = jnp.exp(s - m)
                tot = jnp.sum(p, axis=1, keepdims=True)
                o_ref[rs, hs] = _dot(p.astype(BF), v_scr[win, hs]) / tot
                l_ref[rs, hs] = jnp.broadcast_to(m + jnp.log(tot), (BLK, HEAD_DIM))

    def cur(cb):
        return pl.BlockSpec((rows, GROUP_W), lambda i: (i, cb))

    def halo(cb):
        return pl.BlockSpec((BLK, GROUP_W), lambda i: (jnp.maximum(i * nbk - 1, 0), cb))

    (qa, qc), (ka, kc_), (va, vc_) = q, k, v
    return _pcall(name, kern, (t // rows,), [cur(qc), cur(kc_), halo(kc_), cur(vc_), halo(vc_)],
                  [pl.BlockSpec((rows, GROUP_W), lambda i: (i, 0))] * 2, [jax.ShapeDtypeStruct((t, GROUP_W), F32)] * 2,
                  [pltpu.VMEM((rows + BLK, GROUP_W), BF)] * 2, ("parallel",), (qa, ka, ka, va, va))


def _attn_bwd(name, q, k, v, da, dl, lse, t, dil, side=None):
    rows, nbk, spc = _attn_geometry(t, dil)
    nblk = t // BLK

    def kern(q_ref, qn_ref, k_ref, kh_ref, v_ref, vh_ref, da_ref, dan_ref, dl_ref, dln_ref, ls_ref, lsn_ref,
             dq_ref, dk_ref, dv_ref, k_scr, v_scr):
        i = pl.program_id(0)
        first_shift = jnp.where(i % spc == 0, BLK, 0)
        next_shift = jnp.where((i + 1) % spc == 0, BLK, 0)
        _stage_window(k_scr, kh_ref, k_ref)
        _stage_window(v_scr, vh_ref, v_ref)
        band, col = _band_mask()
        band_first = jnp.logical_and(band, col >= first_shift)
        row1 = lax.broadcasted_iota(jnp.int32, (BLK, BLK), 0)
        col1 = lax.broadcasted_iota(jnp.int32, (BLK, BLK), 1)
        for h in range(HEADS_PER_GROUP):
            hs = _head_lanes(h)
            for b in range(nbk):
                rs, win = slice(b * BLK, (b + 1) * BLK), slice(b * BLK, (b + 2) * BLK)
                qb, dab = q_ref[rs, hs], da_ref[rs, hs].astype(BF)
                kw, vw = k_scr[win, hs], v_scr[win, hs]
                p = jnp.where(band_first if b == 0 else band,
                              jnp.exp(_dot_nt(qb, kw) * ATTN_SCALE - ls_ref[rs, hs][:, :1]), 0.0)
                ds = (p * (_dot_nt(dab, vw) - dl_ref[rs, hs][:, :1]) * ATTN_SCALE).astype(BF)
                dq_ref[rs, hs] = _dot(ds, kw)
                dkw, dvw = _dot_tn(ds, qb), _dot_tn(p.astype(BF), dab)
                if b >= 1:
                    ps = slice((b - 1) * BLK, b * BLK)
                    dk_ref[ps, hs] += dkw[:BLK]
                    dv_ref[ps, hs] += dvw[:BLK]
                dk_ref[rs, hs] = dkw[BLK:]
                dv_ref[rs, hs] = dvw[BLK:]
            ls_rows = slice((nbk - 1) * BLK, nbk * BLK)
            last = slice(nbk * BLK, (nbk + 1) * BLK)
            qb, dab = qn_ref[:, hs], dan_ref[:, hs].astype(BF)
            kp, vp = k_scr[last, hs], v_scr[last, hs]
            p = jnp.where(col1 >= row1 + next_shift, jnp.exp(_dot_nt(qb, kp) * ATTN_SCALE - lsn_ref[:, hs]), 0.0)
            ds = (p * (_dot_nt(dab, vp) - dln_ref[:, hs]) * ATTN_SCALE).astype(BF)
            dk_ref[ls_rows, hs] += _dot_tn(ds, qb)
            dv_ref[ls_rows, hs] += _dot_tn(p.astype(BF), dab)

    def cur(cb):
        return pl.BlockSpec((rows, GROUP_W), lambda i: (i, cb))

    def prev(cb):
        return pl.BlockSpec((BLK, GROUP_W), lambda i: (jnp.maximum(i * nbk - 1, 0), cb))

    def nxt(cb):
        return pl.BlockSpec((BLK, GROUP_W), lambda i: (jnp.minimum((i + 1) * nbk, nblk - 1), cb))

    (qa, qc), (ka, kc_), (va, vc_) = q, k, v
    return _pcall(name, kern, (t // rows,),
                  [cur(qc), nxt(qc), cur(kc_), prev(kc_), cur(vc_), prev(vc_), cur(0), nxt(0), cur(0), nxt(0), cur(0), nxt(0)],
                  [pl.BlockSpec((rows, GROUP_W), lambda i: (i, 0))] * 3, [jax.ShapeDtypeStruct((t, GROUP_W), F32)] * 3,
                  [pltpu.VMEM((rows + BLK, GROUP_W), BF)] * 2, ("parallel",),
                  (qa, qa, ka, ka, va, va, da, da, dl, dl, lse, lse), side)


CLS_TILE = 512


def _cls_block(t, tile, dil, dtype):
    if dil == 1:
        return pl.BlockSpec((tile, GROUP_W), lambda i: (i, 0)), jax.ShapeDtypeStruct((t, GROUP_W), dtype)
    return (pl.BlockSpec((dil, tile // dil, GROUP_W), lambda i: (0, i, 0)),
            jax.ShapeDtypeStruct((dil, t // dil, GROUP_W), dtype))


def _head_scratch(tile):
    return pltpu.VMEM((tile, HEAD_DIM), F32)


def _rope_split(z, c_tab, s_tab, t):
    tile = min(CLS_TILE, t)
    n_heads = ATTN_WIDTH // HEAD_DIM

    def kern(zq_ref, zk_ref, zv_ref, c_ref, s_ref, *rest):
        outs, scr = rest[:9], rest[9]
        c, s = c_ref[...], s_ref[...]
        for which, z_ref in enumerate((zq_ref, zk_ref, zv_ref)):
            for h in range(n_heads):
                g, hs = h // HEADS_PER_GROUP, _head_lanes(h % HEADS_PER_GROUP)
                val = z_ref[:, h * HEAD_DIM:(h + 1) * HEAD_DIM]
                if which < 2:
                    val = _rot(val, c, s)
                if g == 0:
                    outs[which][:, hs] = val.astype(BF)
                    continue
                scr[...] = val
                dil = DILATIONS[g]
                for r in range(dil):
                    outs[3 * g + which][r, :, hs] = scr[pl.ds(r, tile // dil, stride=dil), :].astype(BF)

    blocks = [_cls_block(t, tile, DILATIONS[g], BF) for g in range(3) for _ in range(3)]
    zspec = lambda cb: pl.BlockSpec((tile, ATTN_WIDTH), lambda i: (i, cb))
    tab = pl.BlockSpec((tile, HEAD_DIM), lambda i: (i, 0))
    return _pcall("rope", kern, (t // tile,), [zspec(0), zspec(1), zspec(2), tab, tab], [b[0] for b in blocks],
                  [b[1] for b in blocks], [_head_scratch(tile)], ("parallel",), (z, z, z, c_tab, s_tab))


def _merge_classes(outs, lses, t):
    tile = min(CLS_TILE, t)

    def kern(o0, l0, o1, l1, o2, l2, attn_ref, lse_ref, s_o1, s_l1, s_o2, s_l2):
        for h in range(HEADS_PER_GROUP):
            hs = _head_lanes(h)
            for src, dst, dil in ((o1, s_o1, DILATIONS[1]), (l1, s_l1, DILATIONS[1]), (o2, s_o2, DILATIONS[2]),
                                  (l2, s_l2, DILATIONS[2])):
                for r in range(dil):
                    dst[pl.ds(r, tile // dil, stride=dil), :] = src[r, :, hs]
            attn_ref[:, hs], lse_ref[:, hs] = _merge_fn(o0[:, hs], s_o1[...], s_o2[...], l0[:, hs], s_l1[...], s_l2[...])

    blocks = [_cls_block(t, tile, DILATIONS[g], F32) for g in range(3)]
    args = []
    for g in range(3):
        args += [outs[g].reshape(blocks[g][1].shape), lses[g].reshape(blocks[g][1].shape)]
    tok = pl.BlockSpec((tile, GROUP_W), lambda i: (i, 0))
    return _pcall("attn_merge", kern, (t // tile,), [blocks[g][0] for g in range(3) for _ in range(2)], [tok, tok],
                  [jax.ShapeDtypeStruct((t, GROUP_W), F32)] * 2, [_head_scratch(tile)] * 4, ("parallel",), args)


def _attn_bwd_prep(dattn, attn, lse, t):
    tile = min(CLS_TILE, t)

    def kern(da_ref, at_ref, ls_ref, dl0, da1, dl1, ls1, da2, dl2, ls2, s_da, s_dl, s_ls):
        for h in range(HEADS_PER_GROUP):
            hs = _head_lanes(h)
            da = da_ref[:, hs]
            s_da[...] = da
            s_dl[...] = jnp.broadcast_to(jnp.sum(da * at_ref[:, hs], axis=1, keepdims=True), (tile, HEAD_DIM))
            s_ls[...] = ls_ref[:, hs]
            dl0[:, hs] = s_dl[...]
            for oda, odl, ols, dil in ((da1, dl1, ls1, DILATIONS[1]), (da2, dl2, ls2, DILATIONS[2])):
                for r in range(dil):
                    rows = pl.ds(r, tile // dil, stride=dil)
                    oda[r, :, hs] = s_da[rows, :].astype(BF)
                    odl[r, :, hs] = s_dl[rows, :]
                    ols[r, :, hs] = s_ls[rows, :]

    tok = pl.BlockSpec((tile, GROUP_W), lambda i: (i, 0))
    blocks = [(tok, jax.ShapeDtypeStruct((t, GROUP_W), F32))]
    for g in (1, 2):
        blocks += [_cls_block(t, tile, DILATIONS[g], BF), _cls_block(t, tile, DILATIONS[g], F32),
                   _cls_block(t, tile, DILATIONS[g], F32)]
    res = _pcall("attn_bwd_prep", kern, (t // tile,), [tok, tok, tok], [b[0] for b in blocks], [b[1] for b in blocks],
                 [_head_scratch(tile)] * 3, ("parallel",), (dattn, attn, lse))
    flat = [a.reshape(t, GROUP_W) for a in res]
    return flat[0], flat[1:4], flat[4:7]


def _rope_bwd_join(dqs, dks, dvs, c_tab, s_tab, dglu_a, dglu_b, dgate, t, side=None):
    tile = min(256, t)
    n_heads = ATTN_WIDTH // HEAD_DIM

    def kern(q0, q1, q2, k0, k1, k2, v0, v1, v2, c_ref, s_ref, ga_ref, gb_ref, gt_ref, dz_ref, scr):
        c, s = c_ref[...], -s_ref[...]
        for which, srcs in enumerate(((q0, q1, q2), (k0, k1, k2), (v0, v1, v2))):
            for h in range(n_heads):
                g, hs = h // HEADS_PER_GROUP, _head_lanes(h % HEADS_PER_GROUP)
                if g == 0:
                    val = srcs[0][:, hs]
                else:
                    dil = DILATIONS[g]
                    for r in range(dil):
                        scr[pl.ds(r, tile // dil, stride=dil), :] = srcs[g][r, :, hs]
                    val = scr[...]
                if which < 2:
                    val = _rot(val, c, s)
                col = which * ATTN_WIDTH + h * HEAD_DIM
                dz_ref[:, col:col + HEAD_DIM] = val.astype(BF)
        dz_ref[:, GLU_A_COL:GLU_B_COL] = ga_ref[...]
        dz_ref[:, GLU_B_COL:GATE_A_COL] = gb_ref[...]
        dz_ref[:, GATE_A_COL:] = gt_ref[...]

    blocks = [_cls_block(t, tile, DILATIONS[g], F32) for g in range(3)]
    args = [a.reshape(blocks[g][1].shape) for grp in (dqs, dks, dvs) for g, a in enumerate(grp)]
    tab = pl.BlockSpec((tile, HEAD_DIM), lambda i: (i, 0))
    row = lambda w: pl.BlockSpec((tile, w), lambda i: (i, 0))
    return _pcall("rope_bwd", kern, (t // tile,), [blocks[g][0] for _ in range(3) for g in range(3)]
                  + [tab, tab, row(CONV_CH), row(CONV_CH), row(2 * D_MODEL)], [row(IN_WIDTH)],
                  [jax.ShapeDtypeStruct((t, IN_WIDTH), BF)], [_head_scratch(tile)], ("parallel",),
                  (*args, c_tab, s_tab, dglu_a, dglu_b, dgate), side)


def _cross_probs(qh, kh):
    s = _dot_nt(qh, kh) * CROSS_SCALE
    e = jnp.exp(s - jnp.max(s, axis=1, keepdims=True))
    return e, jnp.sum(e, axis=1, keepdims=True)


def _cross_fwd(cq, ckv, t):
    rows = min(512, t)

    def kern(q_ref, kv_ref, o_ref):
        for h in range(CROSS_HEADS):
            hs = slice(h * CROSS_HEAD_DIM, (h + 1) * CROSS_HEAD_DIM)
            vs = slice(D_MODEL + h * CROSS_HEAD_DIM, D_MODEL + (h + 1) * CROSS_HEAD_DIM)
            e, tot = _cross_probs(q_ref[:, hs], kv_ref[:, hs])
            o_ref[:, hs] = (_dot(e.astype(BF), kv_ref[:, vs]) / tot).astype(BF)

    return pl.pallas_call(
        kern, name="cross_fwd", grid=(t // rows,),
        in_specs=[pl.BlockSpec((rows, D_MODEL), lambda i: (i, 0)), pl.BlockSpec((N_MEM, 2 * D_MODEL), lambda i: (0, 0))],
        out_specs=pl.BlockSpec((rows, D_MODEL), lambda i: (i, 0)),
        out_shape=jax.ShapeDtypeStruct((t, D_MODEL), BF),
        compiler_params=_params(("parallel",)),
    )(cq, ckv)


def _cross_bwd(cq, ckv, dco, t):
    rows = min(512, t)

    def kern(q_ref, kv_ref, do_ref, dq_ref, dkv_ref):
        @pl.when(pl.program_id(0) == 0)
        def _():
            dkv_ref[...] = jnp.zeros_like(dkv_ref)
        for h in range(CROSS_HEADS):
            hs = slice(h * CROSS_HEAD_DIM, (h + 1) * CROSS_HEAD_DIM)
            vs = slice(D_MODEL + h * CROSS_HEAD_DIM, D_MODEL + (h + 1) * CROSS_HEAD_DIM)
            qh, kh, vh, doh = q_ref[:, hs], kv_ref[:, hs], kv_ref[:, vs], do_ref[:, hs]
            e, tot = _cross_probs(qh, kh)
            p = e / tot
            dp = _dot_nt(doh, vh)
            ds = (p * (dp - jnp.sum(p * dp, axis=1, keepdims=True)) * CROSS_SCALE).astype(BF)
            dq_ref[:, hs] = _dot(ds, kh).astype(BF)
            dkv_ref[:, hs] += _dot_tn(ds, qh)
            dkv_ref[:, vs] += _dot_tn(p.astype(BF), doh)

    return pl.pallas_call(
        kern, name="cross_bwd", grid=(t // rows,),
        in_specs=[pl.BlockSpec((rows, D_MODEL), lambda i: (i, 0)), pl.BlockSpec((N_MEM, 2 * D_MODEL), lambda i: (0, 0)),
                  pl.BlockSpec((rows, D_MODEL), lambda i: (i, 0))],
        out_specs=[pl.BlockSpec((rows, D_MODEL), lambda i: (i, 0)), pl.BlockSpec((N_MEM, 2 * D_MODEL), lambda i: (0, 0))],
        out_shape=[jax.ShapeDtypeStruct((t, D_MODEL), BF), jax.ShapeDtypeStruct((N_MEM, 2 * D_MODEL), F32)],
        compiler_params=_params(("arbitrary",)),
    )(cq, ckv, dco)


CONV_TILE = 512
CONV_CHUNK = 128
HALO = 32


def _conv_fwd(z, w32, bias, t):
    tile = min(CONV_TILE, t)
    a_cb, b_cb = GLU_A_COL // LANES, GLU_B_COL // LANES
    hb = tile // HALO

    def kern(a_ref, b_ref, ah_ref, bh_ref, w_ref, bias_ref, o_ref, g_scr):
        i = pl.program_id(1)
        g_scr[HALO:, :] = a_ref[...] * _sig(b_ref[...])
        g_scr[:HALO, :] = ah_ref[...] * _sig(bh_ref[...]) * jnp.where(i > 0, 1.0, 0.0)
        for c in range(tile // CONV_CHUNK):
            acc = jnp.broadcast_to(bias_ref[...], (CONV_CHUNK, LANES))
            for j in range(CONV_K):
                lo = c * CONV_CHUNK + HALO - (CONV_K - 1) + j
                acc = acc + w_ref[j:j + 1, :] * g_scr[lo:lo + CONV_CHUNK, :]
            o_ref[c * CONV_CHUNK:(c + 1) * CONV_CHUNK, :] = acc

    def cur(cb):
        return pl.BlockSpec((tile, LANES), lambda j, i: (i, cb + j))

    def prev(cb):
        return pl.BlockSpec((HALO, LANES), lambda j, i: (jnp.maximum(i * hb - 1, 0), cb + j))

    return pl.pallas_call(
        kern, name="conv_fwd", grid=(CONV_CH // LANES, t // tile),
        in_specs=[cur(a_cb), cur(b_cb), prev(a_cb), prev(b_cb),
                  pl.BlockSpec((CONV_KP, LANES), lambda j, i: (0, j)), pl.BlockSpec((1, LANES), lambda j, i: (0, j))],
        out_specs=pl.BlockSpec((tile, LANES), lambda j, i: (i, j)),
        out_shape=jax.ShapeDtypeStruct((t, CONV_CH), F32),
        scratch_shapes=[pltpu.VMEM((tile + HALO, LANES), F32)],
        compiler_params=_params(("parallel", "parallel")),
    )(z, z, z, z, w32, bias)


def _conv_bwd(z, dc1, w32, t, side=None):
    tile = min(CONV_TILE, t)
    a_cb, b_cb = GLU_A_COL // LANES, GLU_B_COL // LANES
    hb = tile // HALO
    n_tiles = t // tile
    n_chunks = tile // CONV_CHUNK

    def kern(a_ref, b_ref, ah_ref, bh_ref, d_ref, dn_ref, w_ref, da_ref, db_ref, dw_ref, g_scr, d_scr):
        i = pl.program_id(1)
        sg = _sig(b_ref[...])
        g_scr[HALO:, :] = a_ref[...] * sg
        g_scr[:HALO, :] = ah_ref[...] * _sig(bh_ref[...]) * jnp.where(i > 0, 1.0, 0.0)
        d_scr[:tile, :] = d_ref[...]
        d_scr[tile:, :] = dn_ref[...] * jnp.where(i < n_tiles - 1, 1.0, 0.0)

        @pl.when(i == 0)
        def _():
            dw_ref[...] = jnp.zeros_like(dw_ref)

        for c in range(n_chunks):
            cs = slice(c * CONV_CHUNK, (c + 1) * CONV_CHUNK)
            acc = jnp.zeros((CONV_CHUNK, LANES), F32)
            for j in range(CONV_K):
                lo = c * CONV_CHUNK + (CONV_K - 1) - j
                acc = acc + w_ref[j:j + 1, :] * d_scr[lo:lo + CONV_CHUNK, :]
            sgc = _sig(b_ref[cs, :])
            da_ref[cs, :] = (acc * sgc).astype(BF)
            db_ref[cs, :] = (acc * a_ref[cs, :] * sgc * (1.0 - sgc)).astype(BF)
        for j in range(CONV_K):
            tot = jnp.zeros((1, LANES), F32)
            for c in range(n_chunks):
                lo = c * CONV_CHUNK + HALO - (CONV_K - 1) + j
                tot = tot + jnp.sum(d_ref[c * CONV_CHUNK:(c + 1) * CONV_CHUNK, :] * g_scr[lo:lo + CONV_CHUNK, :],
                                    axis=0, keepdims=True)
            dw_ref[j:j + 1, :] += tot
        dw_ref[CONV_K:CONV_KP, :] += jnp.sum(d_ref[...], axis=0, keepdims=True)

    def cur(cb):
        return pl.BlockSpec((tile, LANES), lambda j, i: (i, cb + j))

    def prev(cb):
        return pl.BlockSpec((HALO, LANES), lambda j, i: (jnp.maximum(i * hb - 1, 0), cb + j))

    return _pcall(
        "conv_bwd", kern, (CONV_CH // LANES, n_tiles),
        [cur(a_cb), cur(b_cb), prev(a_cb), prev(b_cb), cur(0),
         pl.BlockSpec((HALO, LANES), lambda j, i: (jnp.minimum((i + 1) * hb, t // HALO - 1), j)),
         pl.BlockSpec((CONV_KP, LANES), lambda j, i: (0, j))],
        [pl.BlockSpec((tile, LANES), lambda j, i: (i, j)), pl.BlockSpec((tile, LANES), lambda j, i: (i, j)),
         pl.BlockSpec((CONV_KP, LANES), lambda j, i: (0, j))],
        [jax.ShapeDtypeStruct((t, CONV_CH), BF), jax.ShapeDtypeStruct((t, CONV_CH), BF),
         jax.ShapeDtypeStruct((CONV_KP, CONV_CH), F32)],
        [pltpu.VMEM((tile + HALO, LANES), F32), pltpu.VMEM((tile + HALO, LANES), F32)],
        ("parallel", "arbitrary"), (z, z, z, z, dc1, dc1, w32), side)


def _adam_fn(w, g, m, v):
    m = ADAM_B1 * m + (1.0 - ADAM_B1) * g
    v = ADAM_B2 * v + (1.0 - ADAM_B2) * (g * g)
    m_hat = m / (1.0 - ADAM_B1 ** ADAM_STEP)
    v_hat = v / (1.0 - ADAM_B2 ** ADAM_STEP)
    delta = -ADAM_LR * (m_hat / (jnp.sqrt(v_hat) + ADAM_EPS) + ADAM_WD * w)
    return delta, m, v


def _adam(name, w, g, m, v):
    rows, cols = w.shape
    tile = _ew_tile(rows, cols)
    return _rowcall(name, _adam_fn, rows, tile, [(a, cols, 0) for a in (w, g, m, v)], [], [(cols, F32)] * 3)


def _place():
    x, y, c = lax.axis_index("x"), lax.axis_index("y"), lax.axis_index("c")
    chips = [(1 - x, y), (x, 1 - y), (1 - x, 1 - y)]
    return x, y, c, chips


def _gather_side(shards):
    nw = len(shards)
    chip = 2 * lax.axis_index("x") + lax.axis_index("y")
    staged = [lax.dynamic_update_index_in_dim(jnp.zeros((4,) + s.shape, s.dtype), s, chip, 0) for s in shards]

    def build(_, outs, send_sems, recv_sems):
        x, y, c, chips = _place()
        me = 2 * x + y
        sibling = (x, y, 1 - c)

        def half(w, lead, h):
            n = shards[w].shape[0] // 2
            return outs[w].at[lead, pl.ds(h * n, n)]

        def copy(w, k, part, to):
            return pltpu.make_async_remote_copy(src_ref=part, dst_ref=part, send_sem=send_sems.at[6 * w + k],
                                                recv_sem=recv_sems.at[6 * w + k], device_id=to, device_id_type=MESH)

        def start():
            for w in range(nw):
                for k, (px, py) in enumerate(chips):
                    copy(w, k, half(w, me, c), (px, py, c)).start()

        def finish():
            for w in range(nw):
                for k, (px, py) in enumerate(chips):
                    landed = half(w, 2 * px + py, c)
                    copy(w, k, landed, (px, py, c)).wait_recv()
                    copy(w, 3 + k, landed, sibling).start()
            for w in range(nw):
                for k, (px, py) in enumerate(chips):
                    copy(w, 3 + k, half(w, 2 * px + py, 1 - c), sibling).wait_recv()
            for w in range(nw):
                for k, (px, py) in enumerate(chips):
                    copy(w, k, half(w, me, c), (px, py, c)).wait_send()
                    copy(w, 3 + k, half(w, 2 * px + py, c), sibling).wait_send()

        return start, finish

    return _Side(staged, [jax.ShapeDtypeStruct((4,) + s.shape, s.dtype) for s in shards], 6 * nw, build,
                 aliases={w: w for w in range(nw)})


def _gather8(name, v):
    rows = v.shape[0]

    def body(v_ref, all_ref, sum_ref, send_sems, recv_sems):
        x, y, c, _ = _place()
        me = 4 * x + 2 * y + c
        all_ref[me] = v_ref[...]
        copies = []
        for k in range(1, 8):
            px, py, pc = x ^ (k >> 2), y ^ ((k >> 1) & 1), c ^ (k & 1)
            copies.append(pltpu.make_async_remote_copy(
                src_ref=v_ref, dst_ref=all_ref.at[me], send_sem=send_sems.at[k - 1], recv_sem=recv_sems.at[k - 1],
                device_id=(px, py, pc), device_id_type=MESH))
            copies[-1].start()
        for k in range(1, 8):
            px, py, pc = x ^ (k >> 2), y ^ ((k >> 1) & 1), c ^ (k & 1)
            theirs = all_ref.at[4 * px + 2 * py + pc]
            pltpu.make_async_remote_copy(
                src_ref=theirs, dst_ref=theirs, send_sem=send_sems.at[k - 1], recv_sem=recv_sems.at[k - 1],
                device_id=(px, py, pc), device_id_type=MESH).wait_recv()
        for cp in copies:
            cp.wait_send()
        tot = all_ref[0]
        for d in range(1, 8):
            tot = tot + all_ref[d]
        sum_ref[...] = tot

    vm = pl.BlockSpec(memory_space=pltpu.VMEM)
    return pl.pallas_call(
        body, name=name, in_specs=[vm], out_specs=[vm, vm],
        out_shape=[jax.ShapeDtypeStruct((8, rows, LANES), F32), jax.ShapeDtypeStruct((rows, LANES), F32)],
        scratch_shapes=[pltpu.SemaphoreType.DMA((7,)), pltpu.SemaphoreType.DMA((7,))],
    )(v)


def _region(ref, col_sharded, shape, j, h):
    r, ccols = shape
    if col_sharded:
        return ref.at[pl.ds(h * (r // 2), r // 2), pl.ds(j * (ccols // 4), ccols // 4)]
    n = r // 8
    return ref.at[pl.ds((2 * j + h) * n, n), :]


def _region_shape(col_sharded, shape):
    r, ccols = shape
    return (r // 2, ccols // 4) if col_sharded else (r // 8, ccols)


def _exchange(copies):
    def build(ins, outs, send_sems, recv_sems):
        def start():
            for cp in copies(ins, outs, send_sems, recv_sems):
                cp.start()

        def finish():
            for cp in copies(ins, outs, send_sems, recv_sems):
                cp.wait()

        return start, finish
    return build


def _swap_side(grads, kinds):
    nw = len(grads)

    def copies(ins, theirs, send_sems, recv_sems):
        x, y, c, _ = _place()
        return [pltpu.make_async_remote_copy(
            src_ref=_region(ins[w], kinds[w], grads[w].shape, j, 1 - c), dst_ref=theirs[w].at[j],
            send_sem=send_sems.at[4 * w + j], recv_sem=recv_sems.at[4 * w + j], device_id=(x, y, 1 - c), device_id_type=MESH)
            for w in range(nw) for j in range(4)]

    shapes = [jax.ShapeDtypeStruct((4,) + _region_shape(kinds[w], grads[w].shape), F32) for w in range(nw)]
    return _Side(grads, shapes, 4 * nw, _exchange(copies))


def _kept_halves(grad, col_sharded, c):
    r, ccols = grad.shape
    if col_sharded:
        slab = lax.dynamic_slice_in_dim(grad, c * (r // 2), r // 2, axis=0)
        return slab.reshape(r // 2, 4, ccols // 4).transpose(1, 0, 2)
    return lax.dynamic_index_in_dim(grad.reshape(4, 2, r // 8, ccols), c, axis=1, keepdims=False)


def _scatter_side(parts):
    nw = len(parts)

    def copies(ins, outs, send_sems, recv_sems):
        x, y, c, chips = _place()
        return [pltpu.make_async_remote_copy(
            src_ref=ins[w].at[2 * px + py], dst_ref=outs[w].at[k], send_sem=send_sems.at[3 * w + k],
            recv_sem=recv_sems.at[3 * w + k], device_id=(px, py, c), device_id_type=MESH)
            for w in range(nw) for k, (px, py) in enumerate(chips)]

    shapes = [jax.ShapeDtypeStruct((3,) + p.shape[1:], p.dtype) for p in parts]
    return _Side(parts, shapes, 3 * nw, _exchange(copies))


def _share_side(halves):
    nw = len(halves)

    def copies(ins, outs, send_sems, recv_sems):
        x, y, c, _ = _place()
        return [pltpu.make_async_remote_copy(
            src_ref=ins[w], dst_ref=outs[w], send_sem=send_sems.at[w], recv_sem=recv_sems.at[w],
            device_id=(x, y, 1 - c), device_id_type=MESH) for w in range(nw)]

    return _Side(halves, [jax.ShapeDtypeStruct(h.shape, F32) for h in halves], nw, _exchange(copies))


def _ew_tile(rows, cols):
    limit = max(8, (256 * 1024) // cols)
    return max(d for d in range(8, min(rows, limit) + 1, 8) if rows % d == 0)


def _pair_sums(names, grads, theirs):
    c = lax.axis_index("c")
    parts, parts_bf = {}, {}
    for n, other in zip(names, theirs):
        _, rr, cc = other.shape
        a = _kept_halves(grads[n], COL_SHARDED[n], c).reshape(4 * rr, cc)
        p, pb = _rowcall(f"grad_pair_sum_{n}", lambda u, v: (u + v, u + v), 4 * rr, _ew_tile(4 * rr, cc),
                         [(a, cc, 0), (other.reshape(4 * rr, cc), cc, 0)], [], [(cc, F32), (cc, BF)])
        parts[n], parts_bf[n] = p.reshape(4, rr, cc), pb.reshape(4, rr, cc)
    return parts, parts_bf


def _chip_sums(names, parts, landed):
    chip = 2 * lax.axis_index("x") + lax.axis_index("y")
    halves = {}
    for n, got in zip(names, landed):
        _, rr, cc = got.shape
        tile = _ew_tile(rr, cc)
        own = lax.dynamic_index_in_dim(parts[n], chip, axis=0, keepdims=False)
        flat = got.reshape(3 * rr, cc)
        nb = rr // tile
        (halves[n],) = _rowcall(f"grad_chip_sum_{n}", lambda own, k0, k1, k2: (((own + k0) + k1) + k2,), rr, tile,
                                [(own, cc, 0), (flat, cc, 0, 0), (flat, cc, 0, nb), (flat, cc, 0, 2 * nb)], [], [(cc, F32)])
    return halves


def _both_halves(mine, other):
    c = lax.axis_index("c")
    return jnp.where(c == 0, jnp.concatenate([mine, other], axis=0), jnp.concatenate([other, mine], axis=0))


BIG = ("w_in", "w_attn_proj", "w_conv_proj", "w_out", "w_cq", "w_ckv", "w_co", "w_up", "w_down")
COL_SHARDED = {"w_in": True, "w_attn_proj": True, "w_conv_proj": True, "w_out": False, "w_cq": False,
               "w_ckv": True, "w_co": False, "w_up": True, "w_down": False}
SMALL = ("g_mix", "b_gate", "conv_b", "conv_ln_g", "conv_ln_b", "g_cross", "g_mem", "g_mlp", "g_final")
ORDER = ("g_mix", "w_in", "b_gate", "conv_w", "conv_b", "conv_ln_g", "conv_ln_b", "w_attn_proj", "w_conv_proj", "w_out",
         "g_cross", "g_mem", "w_cq", "w_ckv", "w_co", "g_mlp", "w_up", "w_down", "g_final")


def _pad_rows(flat, rows):
    return jnp.pad(flat, (0, rows * LANES - flat.shape[0])).reshape(rows, LANES)


REST = tuple(n for n in BIG if n != "w_in")
SCATTER_GROUPS = (("w_down", "w_attn_proj"), ("w_up", "w_conv_proj"), ("w_ckv", "w_out", "w_cq", "w_co"))


def _local_step(x, mem, tgt, shards, small, conv_w_full):
    t = x.shape[0]
    tr = 256
    c_tab, s_tab = _rope_tables(t)
    row = lambda v: v.reshape(1, -1)
    g_mix, g_cross, g_mem, g_mlp, g_final = (row(small[n]) for n in ("g_mix", "g_cross", "g_mem", "g_mlp", "g_final"))
    b_gate, conv_b, ln_g, ln_b = (row(small[n]) for n in ("b_gate", "conv_b", "conv_ln_g", "conv_ln_b"))
    w32 = jnp.pad(conv_w_full, ((0, CONV_KP - CONV_K), (0, 0)))

    (u,), (w_in_all,) = _rowcall("mix_norm", lambda a, g: (_rms(a, g),), t, tr, [(x, D_MODEL, 0)], [g_mix], [(D_MODEL, BF)],
                                 side=_gather_side([shards["w_in"]]))
    z, gathered = _mm("in_proj", u, w_in_all, side=_gather_side([shards[n] for n in REST]))
    wfull = {"w_in": w_in_all}
    for n, g in zip(REST, gathered):
        wfull[n] = g if COL_SHARDED[n] else g.reshape(1, 4 * g.shape[1], g.shape[2])
    qkv = _rope_split(z, c_tab, s_tab, t)
    qkv_cls, outs, lses = [], [], []
    for g, dil in enumerate(DILATIONS):
        ops = tuple((a.reshape(t, GROUP_W), 0) for a in qkv[3 * g:3 * g + 3])
        qkv_cls.append(ops)
        o_g, l_g = _attn_fwd(f"attn_fwd_{g}", *ops, t, dil)
        outs.append(o_g)
        lses.append(l_g)
    attn, lse = _merge_classes(outs, lses, t)
    y_attn = _mm("attn_proj", attn, wfull["w_attn_proj"])
    c1 = _conv_fwd(z, w32, conv_b, t)
    (c2,) = _rowcall("conv_ln_silu", _ln_silu_fn, t, tr, [(c1, CONV_CH, 0)], [ln_g, ln_b], [(CONV_CH, BF)])
    y_conv = _mm("conv_proj", c2, wfull["w_conv_proj"])
    gate_ins = [(z, D_MODEL, GATE_A_COL // D_MODEL), (z, D_MODEL, GATE_B_COL // D_MODEL)]
    (merged,) = _rowcall("gate", _gate_fn, t, tr, [(y_attn, D_MODEL, 0), (y_conv, D_MODEL, 0)] + gate_ins, [b_gate],
                         [(D_MODEL, BF)])
    add = lambda acc, res: (res + acc,)
    x1 = _mm("out_proj", merged, wfull["w_out"], extras=(x,), epi=add)

    (uq,) = _rowcall("cross_norm", lambda a, g: (_rms(a, g),), t, tr, [(x1, D_MODEL, 0)], [g_cross], [(D_MODEL, BF)])
    (mn,) = _rowcall("mem_norm", lambda a, g: (_rms(a, g),), N_MEM, N_MEM, [(mem, D_MODEL, 0)], [g_mem], [(D_MODEL, BF)])
    cq = _mm("cross_q", uq, wfull["w_cq"], out_dtypes=(BF,))
    ckv = _mm("cross_kv", mn, wfull["w_ckv"], out_dtypes=(BF,))
    co = _cross_fwd(cq, ckv, t)
    x2 = _mm("cross_out", co, wfull["w_co"], extras=(x1,), epi=add)

    (um,) = _rowcall("mlp_norm", lambda a, g: (_rms(a, g),), t, tr, [(x2, D_MODEL, 0)], [g_mlp], [(D_MODEL, BF)])
    hpre, hact = _mm("mlp_up", um, wfull["w_up"], out_dtypes=(F32, BF),
                     epi=lambda acc: (acc, jnp.square(jnp.maximum(acc, 0.0))))
    x3 = _mm("mlp_down", hact, wfull["w_down"], extras=(x2,), epi=add)
    d3, d3b, loss_row, dg_final = _rowcall("final_norm_loss", _final_fn, t, tr, [(x3, D_MODEL, 0), (tgt, D_MODEL, 0)],
                                           [g_final], [(D_MODEL, F32), (D_MODEL, BF)], accs=[(1, LANES), (1, D_MODEL)])

    gw = {}
    dhp = _mm("mlp_down_bwd", d3b, wfull["w_down"], nt=True, extras=(hpre,), out_dtypes=(BF,),
              epi=lambda acc, hp: (acc * 2.0 * jnp.maximum(hp, 0.0),))
    gw["w_down"] = _mm_tn("mlp_down_wgrad", hact, d3b)
    dum = _mm("mlp_up_bwd", dhp, wfull["w_up"], nt=True)
    gw["w_up"] = _mm_tn("mlp_up_wgrad", um, dhp)

    def norm_bwd(a, dn, dres, g):
        dx, dg = _rms_bwd(a, g, dn)
        return dres + dx, dres + dx, dg

    d2, d2b, dg_mlp = _rowcall("mlp_norm_bwd", norm_bwd, t, tr, [(x2, D_MODEL, 0), (dum, D_MODEL, 0), (d3, D_MODEL, 0)], [g_mlp],
                               [(D_MODEL, F32), (D_MODEL, BF)], accs=[(1, D_MODEL)])

    dco = _mm("cross_out_bwd", d2b, wfull["w_co"], nt=True, out_dtypes=(BF,))
    gw["w_co"] = _mm_tn("cross_out_wgrad", co, d2b)
    dcq, dckv = _cross_bwd(cq, ckv, dco, t)
    gw["w_cq"] = _mm_tn("cross_q_wgrad", uq, dcq)
    duq = _mm("cross_q_bwd", dcq, wfull["w_cq"], nt=True)
    d1, d1b, dg_cross = _rowcall("cross_norm_bwd", norm_bwd, t, tr, [(x1, D_MODEL, 0), (duq, D_MODEL, 0), (d2, D_MODEL, 0)],
                                 [g_cross], [(D_MODEL, F32), (D_MODEL, BF)], accs=[(1, D_MODEL)])
    gw["w_ckv"] = _mm_tn("cross_kv_wgrad", mn, dckv, tk=N_MEM)
    dmn = _mm("cross_kv_bwd", dckv, wfull["w_ckv"], nt=True)
    (dg_mem,) = _rowcall("mem_norm_bwd", lambda a, dn, g: (_rms_bwd(a, g, dn)[1],), N_MEM, N_MEM,
                         [(mem, D_MODEL, 0), (dmn, D_MODEL, 0)], [g_mem], [], accs=[(1, D_MODEL)])

    dmerged = _mm("out_proj_bwd", d1b, wfull["w_out"], nt=True)
    gw["w_out"] = _mm_tn("out_proj_wgrad", merged, d1b)
    dya, dyc, dgate, dbg = _rowcall("gate_bwd", _gate_bwd_fn, t, tr,
                                    [(dmerged, D_MODEL, 0), (y_attn, D_MODEL, 0), (y_conv, D_MODEL, 0)] + gate_ins, [b_gate],
                                    [(D_MODEL, BF), (D_MODEL, BF), (2 * D_MODEL, BF)], accs=[(1, 2 * D_MODEL)])
    gw["w_attn_proj"] = _mm_tn("attn_proj_wgrad", attn, dya)
    dattn = _mm("attn_proj_bwd", dya, wfull["w_attn_proj"], nt=True)
    gw["w_conv_proj"] = _mm_tn("conv_proj_wgrad", c2, dyc)
    dc2 = _mm("conv_proj_bwd", dyc, wfull["w_conv_proj"], nt=True)
    dc1, dlng, dlnb = _rowcall("conv_ln_silu_bwd", _ln_silu_bwd_fn, t, tr, [(c1, CONV_CH, 0), (dc2, CONV_CH, 0)], [ln_g, ln_b],
                               [(CONV_CH, F32)], accs=[(1, CONV_CH), (1, CONV_CH)])
    (dglu_a, dglu_b, dconv), theirs = _conv_bwd(z, dc1, w32, t, side=_swap_side([gw[n] for n in REST],
                                                                                  [COL_SHARDED[n] for n in REST]))
    parts, parts_bf = _pair_sums(REST, gw, theirs)

    dl0, cls1, cls2 = _attn_bwd_prep(dattn, attn, lse, t)
    dqs, dks, dvs, landed = [], [], [], {}
    for g, (dil, (da_c, dl_c, ls_c)) in enumerate(zip(DILATIONS, ((dattn, dl0, lse), cls1, cls2))):
        (dq_g, dk_g, dv_g), got = _attn_bwd(f"attn_bwd_{g}", *qkv_cls[g], da_c, dl_c, ls_c, t, dil,
                                            side=_scatter_side([parts_bf[n] for n in SCATTER_GROUPS[g]]))
        landed.update(zip(SCATTER_GROUPS[g], got))
        dqs.append(dq_g)
        dks.append(dk_g)
        dvs.append(dv_g)
    halves = _chip_sums(REST, parts, [landed[n] for n in REST])
    (dz,), others = _rope_bwd_join(dqs, dks, dvs, c_tab, s_tab, dglu_a, dglu_b, dgate, t,
                                   side=_share_side([halves[n] for n in REST]))
    gshard = {n: _both_halves(halves[n], o) for n, o in zip(REST, others)}

    gw_in = {"w_in": _mm_tn("in_proj_wgrad", u, dz)}
    du_a, theirs = _mm("in_proj_bwd_a", dz, wfull["w_in"], nt=True, k_part=(0, 2), side=_swap_side([gw_in["w_in"]], [True]))
    parts, parts_bf = _pair_sums(("w_in",), gw_in, theirs)
    du, landed_in = _mm("in_proj_bwd_b", dz, wfull["w_in"], nt=True, k_part=(1, 2), extras=(du_a,), epi=add,
                        side=_scatter_side([parts_bf["w_in"]]))
    halves = _chip_sums(("w_in",), parts, landed_in)
    (gx, dg_mix), others = _rowcall("mix_norm_bwd", lambda a, dn, dres, g: norm_bwd(a, dn, dres, g)[1:], t, tr,
                                    [(x, D_MODEL, 0), (du, D_MODEL, 0), (d1, D_MODEL, 0)], [g_mix], [(D_MODEL, F32)],
                                    accs=[(1, D_MODEL)], side=_share_side([halves["w_in"]]))
    gshard["w_in"] = _both_halves(halves["w_in"], others[0])

    gsmall = {"g_mix": dg_mix, "b_gate": dbg, "conv_b": dconv[CONV_K:CONV_K + 1], "conv_ln_g": dlng, "conv_ln_b": dlnb,
              "g_cross": dg_cross, "g_mem": dg_mem, "g_mlp": dg_mlp, "g_final": dg_final, "conv_w": dconv[:CONV_K]}
    return loss_row, gx, gshard, gsmall


def kernel(x, mem, g_mix, w_in, b_gate, conv_w, conv_b, conv_ln_g, conv_ln_b, w_attn_proj, w_conv_proj, w_out, g_cross, g_mem, w_cq, w_ckv, w_co, g_mlp, w_up, w_down, g_final, loss_target, m_g_mix, m_w_in, m_b_gate, m_conv_w, m_conv_b, m_conv_ln_g, m_conv_ln_b, m_w_attn_proj, m_w_conv_proj, m_w_out, m_g_cross, m_g_mem, m_w_cq, m_w_ckv, m_w_co, m_g_mlp, m_w_up, m_w_down, m_g_final, v_g_mix, v_w_in, v_b_gate, v_conv_w, v_conv_b, v_conv_ln_g, v_conv_ln_b, v_w_attn_proj, v_w_conv_proj, v_w_out, v_g_cross, v_g_mem, v_w_cq, v_w_ckv, v_w_co, v_g_mlp, v_w_up, v_w_down, v_g_final):
    w = dict(g_mix=g_mix, w_in=w_in, b_gate=b_gate, conv_w=conv_w, conv_b=conv_b, conv_ln_g=conv_ln_g, conv_ln_b=conv_ln_b,
             w_attn_proj=w_attn_proj, w_conv_proj=w_conv_proj, w_out=w_out, g_cross=g_cross, g_mem=g_mem, w_cq=w_cq,
             w_ckv=w_ckv, w_co=w_co, g_mlp=g_mlp, w_up=w_up, w_down=w_down, g_final=g_final)
    mo = dict(g_mix=m_g_mix, w_in=m_w_in, b_gate=m_b_gate, conv_w=m_conv_w, conv_b=m_conv_b, conv_ln_g=m_conv_ln_g,
              conv_ln_b=m_conv_ln_b, w_attn_proj=m_w_attn_proj, w_conv_proj=m_w_conv_proj, w_out=m_w_out, g_cross=m_g_cross,
              g_mem=m_g_mem, w_cq=m_w_cq, w_ckv=m_w_ckv, w_co=m_w_co, g_mlp=m_g_mlp, w_up=m_w_up, w_down=m_w_down,
              g_final=m_g_final)
    vo = dict(g_mix=v_g_mix, w_in=v_w_in, b_gate=v_b_gate, conv_w=v_conv_w, conv_b=v_conv_b, conv_ln_g=v_conv_ln_g,
              conv_ln_b=v_conv_ln_b, w_attn_proj=v_w_attn_proj, w_conv_proj=v_w_conv_proj, w_out=v_w_out, g_cross=v_g_cross,
              g_mem=v_g_mem, w_cq=v_w_cq, w_ckv=v_w_ckv, w_co=v_w_co, g_mlp=v_g_mlp, w_up=v_w_up, w_down=v_w_down,
              g_final=v_g_final)
    shapes = {n: w[n].shape for n in ORDER}
    two_d = lambda a: a.reshape(a.shape[-2], a.shape[-1])
    chip = 2 * lax.axis_index("x") + lax.axis_index("y")

    shards = {n: two_d(w[n]).astype(BF) for n in BIG}
    cw_rows = 48
    cw_all, _ = _gather8("gather_conv_w", _pad_rows(conv_w.reshape(-1), cw_rows))
    cw_shard = CONV_K * (CONV_CH // 4)
    conv_w_full = jnp.concatenate(
        [cw_all[2 * j].reshape(-1)[:cw_shard].reshape(CONV_K, CONV_CH // 4) for j in range(4)], axis=1)

    small = {n: w[n] for n in SMALL}
    loss_row, gx, gshard, gsmall = _local_step(two_d(x), two_d(mem), two_d(loss_target), shards, small, conv_w_full)
    loss = lax.psum(loss_row[0, 0], ("x", "y", "c"))

    small_names = SMALL + ("conv_w",)
    flat = jnp.concatenate([gsmall[n].reshape(-1) for n in small_names])
    sm_rows = -(-flat.shape[0] // (8 * LANES)) * 8
    _, sm_sum = _gather8("reduce_small_grads", _pad_rows(flat, sm_rows))
    sm_sum = sm_sum.reshape(-1)
    off = 0
    for n in small_names:
        size = gsmall[n].size
        gshard[n] = sm_sum[off:off + size].reshape(gsmall[n].shape)
        off += size
    gshard["conv_w"] = lax.dynamic_slice_in_dim(gshard["conv_w"], chip * (CONV_CH // 4), CONV_CH // 4, axis=1)

    grads, deltas, new_m, new_v = {}, {}, {}, {}
    for n in BIG:
        d, m2, v2 = _adam(f"adamw_{n}", two_d(w[n]), gshard[n], two_d(mo[n]), two_d(vo[n]))
        grads[n], deltas[n], new_m[n], new_v[n] = (a.reshape(shapes[n]) for a in (gshard[n], d, m2, v2))
    pack = lambda src: jnp.concatenate([src[n].reshape(-1) for n in small_names])
    n_small = sum(w[n].size for n in small_names)
    ad_rows = -(-n_small // (8 * LANES)) * 8
    d, m2, v2 = _adam("adamw_small", *[_pad_rows(pack(src), ad_rows) for src in (w, gshard, mo, vo)])
    off = 0
    for n in small_names:
        size = w[n].size
        grads[n] = gshard[n].reshape(shapes[n])
        deltas[n], new_m[n], new_v[n] = (a.reshape(-1)[off:off + size].reshape(shapes[n]) for a in (d, m2, v2))
        off += size

    return (loss, gx.reshape(x.shape), *[grads[n] for n in ORDER], *[deltas[n] for n in ORDER],
            *[new_m[n] for n in ORDER], *[new_v[n] for n in ORDER])
```

```python
import functools

import jax
import jax.numpy as jnp
from jax import lax
from jax.experimental import pallas as pl
from jax.experimental.pallas import tpu as pltpu

F32 = jnp.float32
BF = jnp.bfloat16

D_MODEL = 1024
N_MEM = 256
HEAD_DIM = 128
HEADS_PER_GROUP = 4
DILATIONS = (1, 4, 16)
BLK = 128
GROUP_W = HEADS_PER_GROUP * HEAD_DIM
ATTN_WIDTH = 3 * GROUP_W
ROT_DIM = 32
ROPE_THETA = 500000.0
CONV_CH = 768
CONV_K = 31
CONV_KP = 32
IN_WIDTH = 8192
CROSS_HEADS = 4
CROSS_HEAD_DIM = 256
D_FF = 4096
EPS = 1e-6
ATTN_SCALE = HEAD_DIM ** -0.5
CROSS_SCALE = CROSS_HEAD_DIM ** -0.5
NEG = -1e30

ADAM_LR = 0.001
ADAM_B1 = 0.9
ADAM_B2 = 0.999
ADAM_EPS = 1e-08
ADAM_WD = 0.01
ADAM_STEP = 10

LANES = 128
VMEM_LIMIT = 56 * 1024 * 1024
MESH = pl.DeviceIdType.MESH
ANY = pl.BlockSpec(memory_space=pl.ANY)

GLU_A_COL = 3 * ATTN_WIDTH
GLU_B_COL = GLU_A_COL + CONV_CH
GATE_A_COL = GLU_B_COL + CONV_CH
GATE_B_COL = GATE_A_COL + D_MODEL


def _params(sem=None):
    return pltpu.CompilerParams(dimension_semantics=sem, vmem_limit_bytes=VMEM_LIMIT)


def _dot(a, b):
    return lax.dot_general(a, b, (((1,), (0,)), ((), ())), preferred_element_type=F32)


def _dot_nt(a, b):
    return lax.dot_general(a, b, (((1,), (1,)), ((), ())), preferred_element_type=F32)


def _dot_tn(a, b):
    return lax.dot_general(a, b, (((0,), (0,)), ((), ())), preferred_element_type=F32)


def _sig(x):
    return 1.0 / (1.0 + jnp.exp(-x))


def _glu(a, b):
    return a.astype(F32) * _sig(b.astype(F32))


class _Side:
    def __init__(self, arrays, out_shapes, n_sems, build, aliases=None):
        self.arrays, self.out_shapes, self.n_sems, self.build = list(arrays), list(out_shapes), n_sems, build
        self.aliases = aliases or {}


def _pcall(name, kern, grid, in_specs, out_specs, out_shape, scratch_shapes, sem, args, side=None):
    in_specs, out_specs, out_shape, scratch_shapes = list(in_specs), list(out_specs), list(out_shape), list(scratch_shapes)
    if side is None:
        return pl.pallas_call(kern, name=name, grid=grid, in_specs=in_specs, out_specs=out_specs, out_shape=out_shape,
                              scratch_shapes=scratch_shapes, compiler_params=_params(sem))(*args)
    ni, no, nsc = len(in_specs), len(out_specs), len(scratch_shapes)
    nsi, nso = len(side.arrays), len(side.out_shapes)

    def wrapped(*refs):
        ins, side_ins = refs[:ni], refs[ni:ni + nsi]
        outs, side_outs = refs[ni + nsi:ni + nsi + no], refs[ni + nsi + no:ni + nsi + no + nso]
        scratch = refs[ni + nsi + no + nso:ni + nsi + no + nso + nsc]
        send_sems, recv_sems = refs[-2:]
        start, finish = side.build(side_ins, side_outs, send_sems, recv_sems)
        if grid:
            first = functools.reduce(jnp.logical_and, [pl.program_id(a) == 0 for a in range(len(grid))])
            last = functools.reduce(jnp.logical_and, [pl.program_id(a) == g - 1 for a, g in enumerate(grid)])
            pl.when(first)(start)
            kern(*ins, *outs, *scratch)
            pl.when(last)(finish)
        else:
            start()
            kern(*ins, *outs, *scratch)
            finish()

    res = pl.pallas_call(
        wrapped, name=name, grid=grid, in_specs=in_specs + [ANY] * nsi, out_specs=out_specs + [ANY] * nso,
        out_shape=out_shape + side.out_shapes,
        scratch_shapes=scratch_shapes + [pltpu.SemaphoreType.DMA((side.n_sems,)), pltpu.SemaphoreType.DMA((side.n_sems,))],
        input_output_aliases={ni + k: no + v for k, v in side.aliases.items()},
        compiler_params=_params(("arbitrary",) * len(grid) if grid else None),
    )(*args, *side.arrays)
    return res[:no], res[no:]


def _rowcall(name, fn, n_rows, tile, ins, params, outs, accs=(), side=None):
    tile = min(tile, n_rows)
    ni, npar, no, na = len(ins), len(params), len(outs), len(accs)

    def kern(*refs):
        in_refs = refs[:ni + npar]
        o_refs = refs[ni + npar:ni + npar + no]
        a_refs = refs[ni + npar + no:]
        vals = fn(*[r[...] for r in in_refs])
        for r, v in zip(o_refs, vals[:no]):
            r[...] = v.astype(r.dtype)
        if na:
            @pl.when(pl.program_id(0) == 0)
            def _():
                for r in a_refs:
                    r[...] = jnp.zeros_like(r)
            for r, v in zip(a_refs, vals[no:]):
                r[...] += v

    in_specs = []
    arrays = []
    for spec in ins:
        arr, width, cb = spec[0], spec[1], spec[2]
        rb = spec[3] if len(spec) > 3 else 0
        in_specs.append(pl.BlockSpec((tile, width), functools.partial(lambda i, cb, rb: (i + rb, cb), cb=cb, rb=rb)))
        arrays.append(arr)
    for p in params:
        in_specs.append(pl.BlockSpec(p.shape, lambda i: (0, 0)))
        arrays.append(p)
    out_specs = [pl.BlockSpec((tile, w), lambda i: (i, 0)) for w, _ in outs]
    out_specs += [pl.BlockSpec(s, lambda i: (0, 0)) for s in accs]
    out_shape = [jax.ShapeDtypeStruct((n_rows, w), dt) for w, dt in outs]
    out_shape += [jax.ShapeDtypeStruct(s, F32) for s in accs]
    return _pcall(name, kern, (n_rows // tile,), in_specs, out_specs, out_shape, [],
                  ("arbitrary",) if na else ("parallel",), arrays, side)


def _mm(name, a, w3, *, nt=False, extras=(), epi=None, out_dtypes=(F32,), tm=None, tn=None, tk=None, k_part=(0, 1),
        side=None):
    m, ka = a.shape
    ns, r, cs = w3.shape
    if not nt:
        k_dim, n = r, ns * cs
        tn = tn or min(cs, 1024)
        tk = tk or min(k_dim, 1024)
    else:
        k_dim, n = ns * cs, r
        tn = tn or min(r, 1024)
        tk = tk or min(cs, 1024)
    assert ka == k_dim, (name, a.shape, w3.shape)
    nk = k_dim // tk // k_part[1]
    k0 = k_part[0] * nk
    if not nt:
        nbs = cs // tn
        w_spec = pl.BlockSpec((None, tk, tn), lambda i, j, k: (j // nbs, k + k0, j % nbs))
    else:
        kbs = cs // tk
        w_spec = pl.BlockSpec((None, tn, tk), lambda i, j, k: ((k + k0) // kbs, j, (k + k0) % kbs))
    tm = tm or min(m, 1024)
    ne, no = len(extras), len(out_dtypes)

    def kern(a_ref, w_ref, *rest):
        e_refs = rest[:ne]
        o_refs = rest[ne:ne + no]

        def part():
            av = a_ref[...].astype(BF)
            return _dot_nt(av, w_ref[...]) if nt else _dot(av, w_ref[...])

        def finish(res):
            vals = epi(res, *[e[...] for e in e_refs]) if epi else (res,)
            for o, v in zip(o_refs, vals):
                o[...] = v.astype(o.dtype)

        if nk == 1:
            finish(part())
            return
        acc = rest[ne + no]
        k = pl.program_id(2)

        @pl.when(k == 0)
        def _():
            acc[...] = part()

        @pl.when(jnp.logical_and(k > 0, k < nk - 1))
        def _():
            acc[...] += part()

        @pl.when(k == nk - 1)
        def _():
            finish(acc[...] + part())

    in_specs = [pl.BlockSpec((tm, tk), lambda i, j, k: (i, k + k0)), w_spec]
    in_specs += [pl.BlockSpec((tm, tn), lambda i, j, k: (i, j)) for _ in extras]
    res = _pcall(name, kern, (m // tm, n // tn, nk), in_specs,
                 [pl.BlockSpec((tm, tn), lambda i, j, k: (i, j)) for _ in out_dtypes],
                 [jax.ShapeDtypeStruct((m, n), dt) for dt in out_dtypes], [pltpu.VMEM((tm, tn), F32)] if nk > 1 else [],
                 ("parallel", "parallel", "arbitrary"), (a, w3, *extras), side)
    if side is not None:
        return (res[0][0] if no == 1 else res[0]), res[1]
    return res[0] if no == 1 else res


def _mm_tn(name, a, b, tm=None, tn=None, tk=None, side=None):
    t, ka = a.shape
    _, n = b.shape
    tm = tm or min(ka, 1024)
    tn = tn or min(n, 1024)
    tk = tk or min(t, 512)

    def kern(a_ref, b_ref, o_ref):
        def part():
            return _dot_tn(a_ref[...].astype(BF), b_ref[...].astype(BF))

        @pl.when(pl.program_id(2) == 0)
        def _():
            o_ref[...] = part()

        @pl.when(pl.program_id(2) > 0)
        def _():
            o_ref[...] += part()

    res = _pcall(name, kern, (ka // tm, n // tn, t // tk),
                 [pl.BlockSpec((tk, tm), lambda i, j, k: (k, i)), pl.BlockSpec((tk, tn), lambda i, j, k: (k, j))],
                 [pl.BlockSpec((tm, tn), lambda i, j, k: (i, j))], [jax.ShapeDtypeStruct((ka, n), F32)], [],
                 ("parallel", "parallel", "arbitrary"), (a, b), side)
    return (res[0][0], res[1]) if side is not None else res[0]


def _rms(x, g):
    return x * lax.rsqrt(jnp.mean(x * x, axis=-1, keepdims=True) + EPS) * g


def _rms_bwd(x, g, dy):
    r = lax.rsqrt(jnp.mean(x * x, axis=-1, keepdims=True) + EPS)
    xh = x * r
    dxh = dy * g
    dx = r * (dxh - xh * jnp.mean(dxh * xh, axis=-1, keepdims=True))
    return dx, jnp.sum(dy * xh, axis=0, keepdims=True)


def _rot(t, c, s):
    lane = lax.broadcasted_iota(jnp.int32, t.shape, 1)
    swapped = jnp.where(lane < ROT_DIM // 2, pltpu.roll(t, HEAD_DIM - ROT_DIM // 2, 1), pltpu.roll(t, ROT_DIM // 2, 1))
    return t * c + swapped * s


def _rope_tables(t):
    half = ROT_DIM // 2
    pos = jnp.arange(t, dtype=F32)
    inv_freq = ROPE_THETA ** (-jnp.arange(0, ROT_DIM, 2, dtype=F32) / ROT_DIM)
    ang = pos[:, None] * inv_freq[None, :]
    cos, sin = jnp.cos(ang), jnp.sin(ang)
    ones = jnp.ones((t, HEAD_DIM - ROT_DIM), F32)
    c_tab = jnp.concatenate([cos, cos, ones], axis=1)
    s_tab = jnp.concatenate([-sin, sin, 0.0 * ones], axis=1)
    return c_tab, s_tab


def _merge_fn(o0, o1, o2, l0, l1, l2):
    m = jnp.maximum(jnp.maximum(l0, l1), l2)
    e0, e1, e2 = jnp.exp(l0 - m), jnp.exp(l1 - m), jnp.exp(l2 - m)
    tot = e0 + e1 + e2
    return (e0 * o0 + e1 * o1 + e2 * o2) / tot, m + jnp.log(tot)


def _ln_parts(c1):
    mu = jnp.mean(c1, axis=-1, keepdims=True)
    xc = c1 - mu
    r = lax.rsqrt(jnp.mean(xc * xc, axis=-1, keepdims=True) + EPS)
    return xc * r, r


def _ln_silu_fn(c1, g, b):
    xh, _ = _ln_parts(c1)
    yl = xh * g + b
    return (yl * _sig(yl),)


def _ln_silu_bwd_fn(c1, dout, g, b):
    xh, r = _ln_parts(c1)
    yl = xh * g + b
    s = _sig(yl)
    dyl = dout * (s + yl * s * (1.0 - s))
    dxh = dyl * g
    dx = r * (dxh - jnp.mean(dxh, axis=-1, keepdims=True) - xh * jnp.mean(dxh * xh, axis=-1, keepdims=True))
    return dx, jnp.sum(dyl * xh, axis=0, keepdims=True), jnp.sum(dyl, axis=0, keepdims=True)


def _gate_fn(ya, yc, ga, gb, bg):
    sa = _sig(ga + bg[:, :D_MODEL])
    sb = _sig(gb + bg[:, D_MODEL:])
    return (sa * ya + sb * yc,)


def _gate_bwd_fn(dm, ya, yc, ga, gb, bg):
    sa = _sig(ga + bg[:, :D_MODEL])
    sb = _sig(gb + bg[:, D_MODEL:])
    dga = dm * ya * sa * (1.0 - sa)
    dgb = dm * yc * sb * (1.0 - sb)
    dgate = jnp.concatenate([dga, dgb], axis=1)
    return dm * sa, dm * sb, dgate, jnp.sum(dgate, axis=0, keepdims=True)


def _final_fn(x3, tgt, g):
    err = _rms(x3, g) - tgt
    lrow = jnp.sum(err * err, axis=-1, keepdims=True) * (0.5 / D_MODEL)
    lsum = jnp.sum(lrow, axis=0, keepdims=True)
    dx, dg = _rms_bwd(x3, g, err * (1.0 / D_MODEL))
    return dx, dx, jnp.broadcast_to(lsum, (1, LANES)), dg


def _attn_geometry(t, dil):
    cls = t // dil
    rows = min(4 * BLK, cls)
    return rows, rows // BLK, cls // rows


def _head_lanes(h):
    return slice(h * HEAD_DIM, (h + 1) * HEAD_DIM)


def _band_mask():
    row = lax.broadcasted_iota(jnp.int32, (BLK, 2 * BLK), 0)
    col = lax.broadcasted_iota(jnp.int32, (BLK, 2 * BLK), 1)
    return jnp.logical_and(col >= row, col <= row + BLK), col


def _stage_window(scr, halo_ref, cur_ref):
    scr[:BLK, :] = halo_ref[...]
    scr[BLK:, :] = cur_ref[...]


def _attn_fwd(name, q, k, v, t, dil):
    rows, nbk, spc = _attn_geometry(t, dil)

    def kern(q_ref, k_ref, kh_ref, v_ref, vh_ref, o_ref, l_ref, k_scr, v_scr):
        i = pl.program_id(0)
        first_shift = jnp.where(i % spc == 0, BLK, 0)
        _stage_window(k_scr, kh_ref, k_ref)
        _stage_window(v_scr, vh_ref, v_ref)
        band, col = _band_mask()
        band_first = jnp.logical_and(band, col >= first_shift)
        for h in range(HEADS_PER_GROUP):
            hs = _head_lanes(h)
            for b in range(nbk):
                rs, win = slice(b * BLK, (b + 1) * BLK), slice(b * BLK, (b + 2) * BLK)
                s = jnp.where(band_first if b == 0 else band, _dot_nt(q_ref[rs, hs], k_scr[win, hs]) * ATTN_SCALE, NEG)
                m = jnp.max(s, axis=1, keepdims=True)
                p = jnp.exp(s - m)
                tot = jnp.sum(p, axis=1, keepdims=True)
                o_ref[rs, hs] = _dot(p.astype(BF), v_scr[win, hs]) / tot
                l_ref[rs, hs] = jnp.broadcast_to(m + jnp.log(tot), (BLK, HEAD_DIM))

    def cur(cb):
        return pl.BlockSpec((rows, GROUP_W), lambda i: (i, cb))

    def halo(cb):
        return pl.BlockSpec((BLK, GROUP_W), lambda i: (jnp.maximum(i * nbk - 1, 0), cb))

    (qa, qc), (ka, kc_), (va, vc_) = q, k, v
    return _pcall(name, kern, (t // rows,), [cur(qc), cur(kc_), halo(kc_), cur(vc_), halo(vc_)],
                  [pl.BlockSpec((rows, GROUP_W), lambda i: (i, 0))] * 2, [jax.ShapeDtypeStruct((t, GROUP_W), F32)] * 2,
                  [pltpu.VMEM((rows + BLK, GROUP_W), BF)] * 2, ("parallel",), (qa, ka, ka, va, va))


def _attn_bwd(name, q, k, v, da, dl, lse, t, dil, side=None):
    rows, nbk, spc = _attn_geometry(t, dil)
    nblk = t // BLK

    def kern(q_ref, qn_ref, k_ref, kh_ref, v_ref, vh_ref, da_ref, dan_ref, dl_ref, dln_ref, ls_ref, lsn_ref,
             dq_ref, dk_ref, dv_ref, k_scr, v_scr):
        i = pl.program_id(0)
        first_shift = jnp.where(i % spc == 0, BLK, 0)
        next_shift = jnp.where((i + 1) % spc == 0, BLK, 0)
        _stage_window(k_scr, kh_ref, k_ref)
        _stage_window(v_scr, vh_ref, v_ref)
        band, col = _band_mask()
        band_first = jnp.logical_and(band, col >= first_shift)
        row1 = lax.broadcasted_iota(jnp.int32, (BLK, BLK), 0)
        col1 = lax.broadcasted_iota(jnp.int32, (BLK, BLK), 1)
        for h in range(HEADS_PER_GROUP):
            hs = _head_lanes(h)
            for b in range(nbk):
                rs, win = slice(b * BLK, (b + 1) * BLK), slice(b * BLK, (b + 2) * BLK)
                qb, dab = q_ref[rs, hs], da_ref[rs, hs].astype(BF)
                kw, vw = k_scr[win, hs], v_scr[win, hs]
                p = jnp.where(band_first if b == 0 else band,
                              jnp.exp(_dot_nt(qb, kw) * ATTN_SCALE - ls_ref[rs, hs][:, :1]), 0.0)
                ds = (p * (_dot_nt(dab, vw) - dl_ref[rs, hs][:, :1]) * ATTN_SCALE).astype(BF)
                dq_ref[rs, hs] = _dot(ds, kw)
                dkw, dvw = _dot_tn(ds, qb), _dot_tn(p.astype(BF), dab)
                if b >= 1:
                    ps = slice((b - 1) * BLK, b * BLK)
                    dk_ref[ps, hs] += dkw[:BLK]
                    dv_ref[ps, hs] += dvw[:BLK]
                dk_ref[rs, hs] = dkw[BLK:]
                dv_ref[rs, hs] = dvw[BLK:]
            ls_rows = slice((nbk - 1) * BLK, nbk * BLK)
            last = slice(nbk * BLK, (nbk + 1) * BLK)
            qb, dab = qn_ref[:, hs], dan_ref[:, hs].astype(BF)
            kp, vp = k_scr[last, hs], v_scr[last, hs]
            p = jnp.where(col1 >= row1 + next_shift, jnp.exp(_dot_nt(qb, kp) * ATTN_SCALE - lsn_ref[:, hs]), 0.0)
            ds = (p * (_dot_nt(dab, vp) - dln_ref[:, hs]) * ATTN_SCALE).astype(BF)
            dk_ref[ls_rows, hs] += _dot_tn(ds, qb)
            dv_ref[ls_rows, hs] += _dot_tn(p.astype(BF), dab)

    def cur(cb):
        return pl.BlockSpec((rows, GROUP_W), lambda i: (i, cb))

    def prev(cb):
        return pl.BlockSpec((BLK, GROUP_W), lambda i: (jnp.maximum(i * nbk - 1, 0), cb))

    def nxt(cb):
        return pl.BlockSpec((BLK, GROUP_W), lambda i: (jnp.minimum((i + 1) * nbk, nblk - 1), cb))

    (qa, qc), (ka, kc_), (va, vc_) = q, k, v
    return _pcall(name, kern, (t // rows,),
                  [cur(qc), nxt(qc), cur(kc_), prev(kc_), cur(vc_), prev(vc_), cur(0), nxt(0), cur(0), nxt(0), cur(0), nxt(0)],
                  [pl.BlockSpec((rows, GROUP_W), lambda i: (i, 0))] * 3, [jax.ShapeDtypeStruct((t, GROUP_W), F32)] * 3,
                  [pltpu.VMEM((rows + BLK, GROUP_W), BF)] * 2, ("parallel",),
                  (qa, qa, ka, ka, va, va, da, da, dl, dl, lse, lse), side)


CLS_TILE = 512


def _cls_block(t, tile, dil, dtype):
    if dil == 1:
        return pl.BlockSpec((tile, GROUP_W), lambda i: (i, 0)), jax.ShapeDtypeStruct((t, GROUP_W), dtype)
    return (pl.BlockSpec((dil, tile // dil, GROUP_W), lambda i: (0, i, 0)),
            jax.ShapeDtypeStruct((dil, t // dil, GROUP_W), dtype))


def _head_scratch(tile):
    return pltpu.VMEM((tile, HEAD_DIM), F32)


def _rope_split(z, c_tab, s_tab, t):
    tile = min(CLS_TILE, t)
    n_heads = ATTN_WIDTH // HEAD_DIM

    def kern(zq_ref, zk_ref, zv_ref, c_ref, s_ref, *rest):
        outs, scr = rest[:9], rest[9]
        c, s = c_ref[...], s_ref[...]
        for which, z_ref in enumerate((zq_ref, zk_ref, zv_ref)):
            for h in range(n_heads):
                g, hs = h // HEADS_PER_GROUP, _head_lanes(h % HEADS_PER_GROUP)
                val = z_ref[:, h * HEAD_DIM:(h + 1) * HEAD_DIM].astype(F32)
                if which < 2:
                    val = _rot(val, c, s)
                if g == 0:
                    outs[which][:, hs] = val.astype(BF)
                    continue
                scr[...] = val
                dil = DILATIONS[g]
                for r in range(dil):
                    outs[3 * g + which][r, :, hs] = scr[pl.ds(r, tile // dil, stride=dil), :].astype(BF)

    blocks = [_cls_block(t, tile, DILATIONS[g], BF) for g in range(3) for _ in range(3)]
    zspec = lambda cb: pl.BlockSpec((tile, ATTN_WIDTH), lambda i: (i, cb))
    tab = pl.BlockSpec((tile, HEAD_DIM), lambda i: (i, 0))
    return _pcall("rope", kern, (t // tile,), [zspec(0), zspec(1), zspec(2), tab, tab], [b[0] for b in blocks],
                  [b[1] for b in blocks], [_head_scratch(tile)], ("parallel",), (z, z, z, c_tab, s_tab))


def _merge_classes(outs, lses, t):
    tile = min(CLS_TILE, t)

    def kern(o0, l0, o1, l1, o2, l2, attn_ref, lse_ref, s_o1, s_l1, s_o2, s_l2):
        for h in range(HEADS_PER_GROUP):
            hs = _head_lanes(h)
            for src, dst, dil in ((o1, s_o1, DILATIONS[1]), (l1, s_l1, DILATIONS[1]), (o2, s_o2, DILATIONS[2]),
                                  (l2, s_l2, DILATIONS[2])):
                for r in range(dil):
                    dst[pl.ds(r, tile // dil, stride=dil), :] = src[r, :, hs]
            attn_ref[:, hs], lse_ref[:, hs] = _merge_fn(o0[:, hs], s_o1[...], s_o2[...], l0[:, hs], s_l1[...], s_l2[...])

    blocks = [_cls_block(t, tile, DILATIONS[g], F32) for g in range(3)]
    args = []
    for g in range(3):
        args += [outs[g].reshape(blocks[g][1].shape), lses[g].reshape(blocks[g][1].shape)]
    tok = pl.BlockSpec((tile, GROUP_W), lambda i: (i, 0))
    return _pcall("attn_merge", kern, (t // tile,), [blocks[g][0] for g in range(3) for _ in range(2)], [tok, tok],
                  [jax.ShapeDtypeStruct((t, GROUP_W), F32)] * 2, [_head_scratch(tile)] * 4, ("parallel",), args)


def _attn_bwd_prep(dattn, attn, lse, t):
    tile = min(CLS_TILE, t)

    def kern(da_ref, at_ref, ls_ref, dl0, da1, dl1, ls1, da2, dl2, ls2, s_da, s_dl, s_ls):
        for h in range(HEADS_PER_GROUP):
            hs = _head_lanes(h)
            da = da_ref[:, hs]
            s_da[...] = da
            s_dl[...] = jnp.broadcast_to(jnp.sum(da * at_ref[:, hs], axis=1, keepdims=True), (tile, HEAD_DIM))
            s_ls[...] = ls_ref[:, hs]
            dl0[:, hs] = s_dl[...]
            for oda, odl, ols, dil in ((da1, dl1, ls1, DILATIONS[1]), (da2, dl2, ls2, DILATIONS[2])):
                for r in range(dil):
                    rows = pl.ds(r, tile // dil, stride=dil)
                    oda[r, :, hs] = s_da[rows, :].astype(BF)
                    odl[r, :, hs] = s_dl[rows, :]
                    ols[r, :, hs] = s_ls[rows, :]

    tok = pl.BlockSpec((tile, GROUP_W), lambda i: (i, 0))
    blocks = [(tok, jax.ShapeDtypeStruct((t, GROUP_W), F32))]
    for g in (1, 2):
        blocks += [_cls_block(t, tile, DILATIONS[g], BF), _cls_block(t, tile, DILATIONS[g], F32),
                   _cls_block(t, tile, DILATIONS[g], F32)]
    res = _pcall("attn_bwd_prep", kern, (t // tile,), [tok, tok, tok], [b[0] for b in blocks], [b[1] for b in blocks],
                 [_head_scratch(tile)] * 3, ("parallel",), (dattn, attn, lse))
    flat = [a.reshape(t, GROUP_W) for a in res]
    return flat[0], flat[1:4], flat[4:7]


def _rope_bwd_join(dqs, dks, dvs, c_tab, s_tab, dglu_a, dglu_b, dgate, t, side=None):
    tile = min(256, t)
    n_heads = ATTN_WIDTH // HEAD_DIM

    def kern(q0, q1, q2, k0, k1, k2, v0, v1, v2, c_ref, s_ref, ga_ref, gb_ref, gt_ref, dz_ref, scr):
        c, s = c_ref[...], -s_ref[...]
        for which, srcs in enumerate(((q0, q1, q2), (k0, k1, k2), (v0, v1, v2))):
            for h in range(n_heads):
                g, hs = h // HEADS_PER_GROUP, _head_lanes(h % HEADS_PER_GROUP)
                if g == 0:
                    val = srcs[0][:, hs]
                else:
                    dil = DILATIONS[g]
                    for r in range(dil):
                        scr[pl.ds(r, tile // dil, stride=dil), :] = srcs[g][r, :, hs]
                    val = scr[...]
                if which < 2:
                    val = _rot(val, c, s)
                col = which * ATTN_WIDTH + h * HEAD_DIM
                dz_ref[:, col:col + HEAD_DIM] = val.astype(BF)
        dz_ref[:, GLU_A_COL:GLU_B_COL] = ga_ref[...]
        dz_ref[:, GLU_B_COL:GATE_A_COL] = gb_ref[...]
        dz_ref[:, GATE_A_COL:] = gt_ref[...]

    blocks = [_cls_block(t, tile, DILATIONS[g], F32) for g in range(3)]
    args = [a.reshape(blocks[g][1].shape) for grp in (dqs, dks, dvs) for g, a in enumerate(grp)]
    tab = pl.BlockSpec((tile, HEAD_DIM), lambda i: (i, 0))
    row = lambda w: pl.BlockSpec((tile, w), lambda i: (i, 0))
    return _pcall("rope_bwd", kern, (t // tile,), [blocks[g][0] for _ in range(3) for g in range(3)]
                  + [tab, tab, row(CONV_CH), row(CONV_CH), row(2 * D_MODEL)], [row(IN_WIDTH)],
                  [jax.ShapeDtypeStruct((t, IN_WIDTH), BF)], [_head_scratch(tile)], ("parallel",),
                  (*args, c_tab, s_tab, dglu_a, dglu_b, dgate), side)


def _cross_probs(qh, kh):
    s = _dot_nt(qh, kh) * CROSS_SCALE
    e = jnp.exp(s - jnp.max(s, axis=1, keepdims=True))
    return e, jnp.sum(e, axis=1, keepdims=True)


def _cross_fwd(cq, ckv, t):
    rows = min(512, t)

    def kern(q_ref, kv_ref, o_ref):
        for h in range(CROSS_HEADS):
            hs = slice(h * CROSS_HEAD_DIM, (h + 1) * CROSS_HEAD_DIM)
            vs = slice(D_MODEL + h * CROSS_HEAD_DIM, D_MODEL + (h + 1) * CROSS_HEAD_DIM)
            e, tot = _cross_probs(q_ref[:, hs], kv_ref[:, hs])
            o_ref[:, hs] = (_dot(e.astype(BF), kv_ref[:, vs]) / tot).astype(BF)

    return pl.pallas_call(
        kern, name="cross_fwd", grid=(t // rows,),
        in_specs=[pl.BlockSpec((rows, D_MODEL), lambda i: (i, 0)), pl.BlockSpec((N_MEM, 2 * D_MODEL), lambda i: (0, 0))],
        out_specs=pl.BlockSpec((rows, D_MODEL), lambda i: (i, 0)),
        out_shape=jax.ShapeDtypeStruct((t, D_MODEL), BF),
        compiler_params=_params(("parallel",)),
    )(cq, ckv)


def _cross_bwd(cq, ckv, dco, t):
    rows = min(512, t)

    def kern(q_ref, kv_ref, do_ref, dq_ref, dkv_ref):
        @pl.when(pl.program_id(0) == 0)
        def _():
            dkv_ref[...] = jnp.zeros_like(dkv_ref)
        for h in range(CROSS_HEADS):
            hs = slice(h * CROSS_HEAD_DIM, (h + 1) * CROSS_HEAD_DIM)
            vs = slice(D_MODEL + h * CROSS_HEAD_DIM, D_MODEL + (h + 1) * CROSS_HEAD_DIM)
            qh, kh, vh, doh = q_ref[:, hs], kv_ref[:, hs], kv_ref[:, vs], do_ref[:, hs]
            e, tot = _cross_probs(qh, kh)
            p = e / tot
            dp = _dot_nt(doh, vh)
            ds = (p * (dp - jnp.sum(p * dp, axis=1, keepdims=True)) * CROSS_SCALE).astype(BF)
            dq_ref[:, hs] = _dot(ds, kh).astype(BF)
            dkv_ref[:, hs] += _dot_tn(ds, qh)
            dkv_ref[:, vs] += _dot_tn(p.astype(BF), doh)

    return pl.pallas_call(
        kern, name="cross_bwd", grid=(t // rows,),
        in_specs=[pl.BlockSpec((rows, D_MODEL), lambda i: (i, 0)), pl.BlockSpec((N_MEM, 2 * D_MODEL), lambda i: (0, 0)),
                  pl.BlockSpec((rows, D_MODEL), lambda i: (i, 0))],
        out_specs=[pl.BlockSpec((rows, D_MODEL), lambda i: (i, 0)), pl.BlockSpec((N_MEM, 2 * D_MODEL), lambda i: (0, 0))],
        out_shape=[jax.ShapeDtypeStruct((t, D_MODEL), BF), jax.ShapeDtypeStruct((N_MEM, 2 * D_MODEL), F32)],
        compiler_params=_params(("arbitrary",)),
    )(cq, ckv, dco)


CONV_TILE = 512
CONV_CHUNK = 128
HALO = 32


def _conv_fwd(z, w32, bias, t):
    tile = min(CONV_TILE, t)
    a_cb, b_cb = GLU_A_COL // LANES, GLU_B_COL // LANES
    hb = tile // HALO

    def kern(a_ref, b_ref, ah_ref, bh_ref, w_ref, bias_ref, o_ref, g_scr):
        i = pl.program_id(1)
        g_scr[HALO:, :] = _glu(a_ref[...], b_ref[...])
        g_scr[:HALO, :] = _glu(ah_ref[...], bh_ref[...]) * jnp.where(i > 0, 1.0, 0.0)
        for c in range(tile // CONV_CHUNK):
            acc = jnp.broadcast_to(bias_ref[...], (CONV_CHUNK, LANES))
            for j in range(CONV_K):
                lo = c * CONV_CHUNK + HALO - (CONV_K - 1) + j
                acc = acc + w_ref[j:j + 1, :] * g_scr[lo:lo + CONV_CHUNK, :]
            o_ref[c * CONV_CHUNK:(c + 1) * CONV_CHUNK, :] = acc

    def cur(cb):
        return pl.BlockSpec((tile, LANES), lambda j, i: (i, cb + j))

    def prev(cb):
        return pl.BlockSpec((HALO, LANES), lambda j, i: (jnp.maximum(i * hb - 1, 0), cb + j))

    return pl.pallas_call(
        kern, name="conv_fwd", grid=(CONV_CH // LANES, t // tile),
        in_specs=[cur(a_cb), cur(b_cb), prev(a_cb), prev(b_cb),
                  pl.BlockSpec((CONV_KP, LANES), lambda j, i: (0, j)), pl.BlockSpec((1, LANES), lambda j, i: (0, j))],
        out_specs=pl.BlockSpec((tile, LANES), lambda j, i: (i, j)),
        out_shape=jax.ShapeDtypeStruct((t, CONV_CH), F32),
        scratch_shapes=[pltpu.VMEM((tile + HALO, LANES), F32)],
        compiler_params=_params(("parallel", "parallel")),
    )(z, z, z, z, w32, bias)


def _conv_bwd(z, dc1, w32, t, side=None):
    tile = min(CONV_TILE, t)
    a_cb, b_cb = GLU_A_COL // LANES, GLU_B_COL // LANES
    hb = tile // HALO
    n_tiles = t // tile
    n_chunks = tile // CONV_CHUNK

    def kern(a_ref, b_ref, ah_ref, bh_ref, d_ref, dn_ref, w_ref, da_ref, db_ref, dw_ref, g_scr, d_scr):
        i = pl.program_id(1)
        g_scr[HALO:, :] = _glu(a_ref[...], b_ref[...])
        g_scr[:HALO, :] = _glu(ah_ref[...], bh_ref[...]) * jnp.where(i > 0, 1.0, 0.0)
        d_scr[:tile, :] = d_ref[...]
        d_scr[tile:, :] = dn_ref[...] * jnp.where(i < n_tiles - 1, 1.0, 0.0)

        @pl.when(i == 0)
        def _():
            dw_ref[...] = jnp.zeros_like(dw_ref)

        for c in range(n_chunks):
            cs = slice(c * CONV_CHUNK, (c + 1) * CONV_CHUNK)
            acc = jnp.zeros((CONV_CHUNK, LANES), F32)
            for j in range(CONV_K):
                lo = c * CONV_CHUNK + (CONV_K - 1) - j
                acc = acc + w_ref[j:j + 1, :] * d_scr[lo:lo + CONV_CHUNK, :]
            sgc = _sig(b_ref[cs, :].astype(F32))
            da_ref[cs, :] = (acc * sgc).astype(BF)
            db_ref[cs, :] = (acc * a_ref[cs, :].astype(F32) * sgc * (1.0 - sgc)).astype(BF)
        for j in range(CONV_K):
            tot = jnp.zeros((1, LANES), F32)
            for c in range(n_chunks):
                lo = c * CONV_CHUNK + HALO - (CONV_K - 1) + j
                tot = tot + jnp.sum(d_ref[c * CONV_CHUNK:(c + 1) * CONV_CHUNK, :] * g_scr[lo:lo + CONV_CHUNK, :],
                                    axis=0, keepdims=True)
            dw_ref[j:j + 1, :] += tot
        dw_ref[CONV_K:CONV_KP, :] += jnp.sum(d_ref[...], axis=0, keepdims=True)

    def cur(cb):
        return pl.BlockSpec((tile, LANES), lambda j, i: (i, cb + j))

    def prev(cb):
        return pl.BlockSpec((HALO, LANES), lambda j, i: (jnp.maximum(i * hb - 1, 0), cb + j))

    return _pcall(
        "conv_bwd", kern, (CONV_CH // LANES, n_tiles),
        [cur(a_cb), cur(b_cb), prev(a_cb), prev(b_cb), cur(0),
         pl.BlockSpec((HALO, LANES), lambda j, i: (jnp.minimum((i + 1) * hb, t // HALO - 1), j)),
         pl.BlockSpec((CONV_KP, LANES), lambda j, i: (0, j))],
        [pl.BlockSpec((tile, LANES), lambda j, i: (i, j)), pl.BlockSpec((tile, LANES), lambda j, i: (i, j)),
         pl.BlockSpec((CONV_KP, LANES), lambda j, i: (0, j))],
        [jax.ShapeDtypeStruct((t, CONV_CH), BF), jax.ShapeDtypeStruct((t, CONV_CH), BF),
         jax.ShapeDtypeStruct((CONV_KP, CONV_CH), F32)],
        [pltpu.VMEM((tile + HALO, LANES), F32), pltpu.VMEM((tile + HALO, LANES), F32)],
        ("parallel", "arbitrary"), (z, z, z, z, dc1, dc1, w32), side)


def _adam_fn(w, g, m, v):
    m = ADAM_B1 * m + (1.0 - ADAM_B1) * g
    v = ADAM_B2 * v + (1.0 - ADAM_B2) * (g * g)
    m_hat = m / (1.0 - ADAM_B1 ** ADAM_STEP)
    v_hat = v / (1.0 - ADAM_B2 ** ADAM_STEP)
    delta = -ADAM_LR * (m_hat / (jnp.sqrt(v_hat) + ADAM_EPS) + ADAM_WD * w)
    return delta, m, v


def _adam(name, w, g, m, v):
    rows, cols = w.shape
    tile = _ew_tile(rows, cols)
    return _rowcall(name, _adam_fn, rows, tile, [(a, cols, 0) for a in (w, g, m, v)], [], [(cols, F32)] * 3)


def _place():
    x, y, c = lax.axis_index("x"), lax.axis_index("y"), lax.axis_index("c")
    chips = [(1 - x, y), (x, 1 - y), (1 - x, 1 - y)]
    return x, y, c, chips


def _gather_side(shards):
    nw = len(shards)
    chip = 2 * lax.axis_index("x") + lax.axis_index("y")
    staged = [lax.dynamic_update_index_in_dim(jnp.zeros((4,) + s.shape, s.dtype), s, chip, 0) for s in shards]

    def build(_, outs, send_sems, recv_sems):
        x, y, c, chips = _place()
        me = 2 * x + y
        sibling = (x, y, 1 - c)

        def half(w, lead, h):
            n = shards[w].shape[0] // 2
            return outs[w].at[lead, pl.ds(h * n, n)]

        def copy(w, k, part, to):
            return pltpu.make_async_remote_copy(src_ref=part, dst_ref=part, send_sem=send_sems.at[6 * w + k],
                                                recv_sem=recv_sems.at[6 * w + k], device_id=to, device_id_type=MESH)

        def start():
            for w in range(nw):
                for k, (px, py) in enumerate(chips):
                    copy(w, k, half(w, me, c), (px, py, c)).start()

        def finish():
            for w in range(nw):
                for k, (px, py) in enumerate(chips):
                    landed = half(w, 2 * px + py, c)
                    copy(w, k, landed, (px, py, c)).wait_recv()
                    copy(w, 3 + k, landed, sibling).start()
            for w in range(nw):
                for k, (px, py) in enumerate(chips):
                    copy(w, 3 + k, half(w, 2 * px + py, 1 - c), sibling).wait_recv()
            for w in range(nw):
                for k, (px, py) in enumerate(chips):
                    copy(w, k, half(w, me, c), (px, py, c)).wait_send()
                    copy(w, 3 + k, half(w, 2 * px + py, c), sibling).wait_send()

        return start, finish

    return _Side(staged, [jax.ShapeDtypeStruct((4,) + s.shape, s.dtype) for s in shards], 6 * nw, build,
                 aliases={w: w for w in range(nw)})


def _gather8(name, v):
    rows = v.shape[0]

    def body(v_ref, all_ref, sum_ref, send_sems, recv_sems):
        x, y, c, _ = _place()
        me = 4 * x + 2 * y + c
        all_ref[me] = v_ref[...]
        copies = []
        for k in range(1, 8):
            px, py, pc = x ^ (k >> 2), y ^ ((k >> 1) & 1), c ^ (k & 1)
            copies.append(pltpu.make_async_remote_copy(
                src_ref=v_ref, dst_ref=all_ref.at[me], send_sem=send_sems.at[k - 1], recv_sem=recv_sems.at[k - 1],
                device_id=(px, py, pc), device_id_type=MESH))
            copies[-1].start()
        for k in range(1, 8):
            px, py, pc = x ^ (k >> 2), y ^ ((k >> 1) & 1), c ^ (k & 1)
            theirs = all_ref.at[4 * px + 2 * py + pc]
            pltpu.make_async_remote_copy(
                src_ref=theirs, dst_ref=theirs, send_sem=send_sems.at[k - 1], recv_sem=recv_sems.at[k - 1],
                device_id=(px, py, pc), device_id_type=MESH).wait_recv()
        for cp in copies:
            cp.wait_send()
        tot = all_ref[0]
        for d in range(1, 8):
            tot = tot + all_ref[d]
        sum_ref[...] = tot

    vm = pl.BlockSpec(memory_space=pltpu.VMEM)
    return pl.pallas_call(
        body, name=name, in_specs=[vm], out_specs=[vm, vm],
        out_shape=[jax.ShapeDtypeStruct((8, rows, LANES), F32), jax.ShapeDtypeStruct((rows, LANES), F32)],
        scratch_shapes=[pltpu.SemaphoreType.DMA((7,)), pltpu.SemaphoreType.DMA((7,))],
    )(v)


def _region(ref, col_sharded, shape, j, h):
    r, ccols = shape
    if col_sharded:
        return ref.at[pl.ds(h * (r // 2), r // 2), pl.ds(j * (ccols // 4), ccols // 4)]
    n = r // 8
    return ref.at[pl.ds((2 * j + h) * n, n), :]


def _region_shape(col_sharded, shape):
    r, ccols = shape
    return (r // 2, ccols // 4) if col_sharded else (r // 8, ccols)


def _exchange(copies):
    def build(ins, outs, send_sems, recv_sems):
        def start():
            for cp in copies(ins, outs, send_sems, recv_sems):
                cp.start()

        def finish():
            for cp in copies(ins, outs, send_sems, recv_sems):
                cp.wait()

        return start, finish
    return build


def _swap_side(grads, kinds):
    nw = len(grads)

    def copies(ins, theirs, send_sems, recv_sems):
        x, y, c, _ = _place()
        return [pltpu.make_async_remote_copy(
            src_ref=_region(ins[w], kinds[w], grads[w].shape, j, 1 - c), dst_ref=theirs[w].at[j],
            send_sem=send_sems.at[4 * w + j], recv_sem=recv_sems.at[4 * w + j], device_id=(x, y, 1 - c), device_id_type=MESH)
            for w in range(nw) for j in range(4)]

    shapes = [jax.ShapeDtypeStruct((4,) + _region_shape(kinds[w], grads[w].shape), F32) for w in range(nw)]
    return _Side(grads, shapes, 4 * nw, _exchange(copies))


def _kept_halves(grad, col_sharded, c):
    r, ccols = grad.shape
    if col_sharded:
        slab = lax.dynamic_slice_in_dim(grad, c * (r // 2), r // 2, axis=0)
        return slab.reshape(r // 2, 4, ccols // 4).transpose(1, 0, 2)
    return lax.dynamic_index_in_dim(grad.reshape(4, 2, r // 8, ccols), c, axis=1, keepdims=False)


def _scatter_side(parts):
    nw = len(parts)

    def copies(ins, outs, send_sems, recv_sems):
        x, y, c, chips = _place()
        return [pltpu.make_async_remote_copy(
            src_ref=ins[w].at[2 * px + py], dst_ref=outs[w].at[k], send_sem=send_sems.at[3 * w + k],
            recv_sem=recv_sems.at[3 * w + k], device_id=(px, py, c), device_id_type=MESH)
            for w in range(nw) for k, (px, py) in enumerate(chips)]

    shapes = [jax.ShapeDtypeStruct((3,) + p.shape[1:], p.dtype) for p in parts]
    return _Side(parts, shapes, 3 * nw, _exchange(copies))


def _share_side(halves):
    nw = len(halves)

    def copies(ins, outs, send_sems, recv_sems):
        x, y, c, _ = _place()
        return [pltpu.make_async_remote_copy(
            src_ref=ins[w], dst_ref=outs[w], send_sem=send_sems.at[w], recv_sem=recv_sems.at[w],
            device_id=(x, y, 1 - c), device_id_type=MESH) for w in range(nw)]

    return _Side(halves, [jax.ShapeDtypeStruct(h.shape, F32) for h in halves], nw, _exchange(copies))


def _ew_tile(rows, cols):
    limit = max(8, (256 * 1024) // cols)
    return max(d for d in range(8, min(rows, limit) + 1, 8) if rows % d == 0)


def _pair_sums(names, grads, theirs):
    c = lax.axis_index("c")
    parts, parts_bf = {}, {}
    for n, other in zip(names, theirs):
        _, rr, cc = other.shape
        a = _kept_halves(grads[n], COL_SHARDED[n], c).reshape(4 * rr, cc)
        p, pb = _rowcall(f"grad_pair_sum_{n}", lambda u, v: (u + v, u + v), 4 * rr, _ew_tile(4 * rr, cc),
                         [(a, cc, 0), (other.reshape(4 * rr, cc), cc, 0)], [], [(cc, F32), (cc, BF)])
        parts[n], parts_bf[n] = p.reshape(4, rr, cc), pb.reshape(4, rr, cc)
    return parts, parts_bf


def _chip_sums(names, parts, landed):
    chip = 2 * lax.axis_index("x") + lax.axis_index("y")
    halves = {}
    for n, got in zip(names, landed):
        _, rr, cc = got.shape
        tile = _ew_tile(rr, cc)
        own = lax.dynamic_index_in_dim(parts[n], chip, axis=0, keepdims=False)
        flat = got.reshape(3 * rr, cc)
        nb = rr // tile
        (halves[n],) = _rowcall(f"grad_chip_sum_{n}", lambda own, k0, k1, k2: (((own + k0) + k1) + k2,), rr, tile,
                                [(own, cc, 0), (flat, cc, 0, 0), (flat, cc, 0, nb), (flat, cc, 0, 2 * nb)], [], [(cc, F32)])
    return halves


def _both_halves(mine, other):
    c = lax.axis_index("c")
    return jnp.where(c == 0, jnp.concatenate([mine, other], axis=0), jnp.concatenate([other, mine], axis=0))


BIG = ("w_in", "w_attn_proj", "w_conv_proj", "w_out", "w_cq", "w_ckv", "w_co", "w_up", "w_down")
COL_SHARDED = {"w_in": True, "w_attn_proj": True, "w_conv_proj": True, "w_out": False, "w_cq": False,
               "w_ckv": True, "w_co": False, "w_up": True, "w_down": False}
SMALL = ("g_mix", "b_gate", "conv_b", "conv_ln_g", "conv_ln_b", "g_cross", "g_mem", "g_mlp", "g_final")
ORDER = ("g_mix", "w_in", "b_gate", "conv_w", "conv_b", "conv_ln_g", "conv_ln_b", "w_attn_proj", "w_conv_proj", "w_out",
         "g_cross", "g_mem", "w_cq", "w_ckv", "w_co", "g_mlp", "w_up", "w_down", "g_final")


def _pad_rows(flat, rows):
    return jnp.pad(flat, (0, rows * LANES - flat.shape[0])).reshape(rows, LANES)


REST = tuple(n for n in BIG if n != "w_in")
WAVE_MLP = ("w_down", "w_up")
WAVE_MID = ("w_co", "w_cq", "w_ckv", "w_out", "w_attn_proj", "w_conv_proj")


def _local_step(x, mem, tgt, shards, small, conv_w_full):
    t = x.shape[0]
    tr = 256
    c_tab, s_tab = _rope_tables(t)
    row = lambda v: v.reshape(1, -1)
    g_mix, g_cross, g_mem, g_mlp, g_final = (row(small[n]) for n in ("g_mix", "g_cross", "g_mem", "g_mlp", "g_final"))
    b_gate, conv_b, ln_g, ln_b = (row(small[n]) for n in ("b_gate", "conv_b", "conv_ln_g", "conv_ln_b"))
    w32 = jnp.pad(conv_w_full, ((0, CONV_KP - CONV_K), (0, 0)))

    (u,), (w_in_all,) = _rowcall("mix_norm", lambda a, g: (_rms(a, g),), t, tr, [(x, D_MODEL, 0)], [g_mix], [(D_MODEL, BF)],
                                 side=_gather_side([shards["w_in"]]))
    z, gathered = _mm("in_proj", u, w_in_all, out_dtypes=(BF,), side=_gather_side([shards[n] for n in REST]))
    wfull = {"w_in": w_in_all}
    for n, g in zip(REST, gathered):
        wfull[n] = g if COL_SHARDED[n] else g.reshape(1, 4 * g.shape[1], g.shape[2])
    qkv = _rope_split(z, c_tab, s_tab, t)
    qkv_cls, outs, lses = [], [], []
    for g, dil in enumerate(DILATIONS):
        ops = tuple((a.reshape(t, GROUP_W), 0) for a in qkv[3 * g:3 * g + 3])
        qkv_cls.append(ops)
        o_g, l_g = _attn_fwd(f"attn_fwd_{g}", *ops, t, dil)
        outs.append(o_g)
        lses.append(l_g)
    attn, lse = _merge_classes(outs, lses, t)
    y_attn = _mm("attn_proj", attn, wfull["w_attn_proj"])
    c1 = _conv_fwd(z, w32, conv_b, t)
    (c2,) = _rowcall("conv_ln_silu", _ln_silu_fn, t, tr, [(c1, CONV_CH, 0)], [ln_g, ln_b], [(CONV_CH, BF)])
    y_conv = _mm("conv_proj", c2, wfull["w_conv_proj"])
    gate_ins = [(z, D_MODEL, GATE_A_COL // D_MODEL), (z, D_MODEL, GATE_B_COL // D_MODEL)]
    (merged,) = _rowcall("gate", _gate_fn, t, tr, [(y_attn, D_MODEL, 0), (y_conv, D_MODEL, 0)] + gate_ins, [b_gate],
                         [(D_MODEL, BF)])
    add = lambda acc, res: (res + acc,)
    x1 = _mm("out_proj", merged, wfull["w_out"], extras=(x,), epi=add)

    (uq,) = _rowcall("cross_norm", lambda a, g: (_rms(a, g),), t, tr, [(x1, D_MODEL, 0)], [g_cross], [(D_MODEL, BF)])
    (mn,) = _rowcall("mem_norm", lambda a, g: (_rms(a, g),), N_MEM, N_MEM, [(mem, D_MODEL, 0)], [g_mem], [(D_MODEL, BF)])
    cq = _mm("cross_q", uq, wfull["w_cq"], out_dtypes=(BF,))
    ckv = _mm("cross_kv", mn, wfull["w_ckv"], out_dtypes=(BF,))
    co = _cross_fwd(cq, ckv, t)
    x2 = _mm("cross_out", co, wfull["w_co"], extras=(x1,), epi=add)

    (um,) = _rowcall("mlp_norm", lambda a, g: (_rms(a, g),), t, tr, [(x2, D_MODEL, 0)], [g_mlp], [(D_MODEL, BF)])
    hact = _mm("mlp_up", um, wfull["w_up"], out_dtypes=(BF,), epi=lambda acc: (jnp.square(jnp.maximum(acc, 0.0)),))
    x3 = _mm("mlp_down", hact, wfull["w_down"], extras=(x2,), epi=add)
    d3, d3b, loss_row, dg_final = _rowcall("final_norm_loss", _final_fn, t, tr, [(x3, D_MODEL, 0), (tgt, D_MODEL, 0)],
                                           [g_final], [(D_MODEL, F32), (D_MODEL, BF)], accs=[(1, LANES), (1, D_MODEL)])

    gw = {}
    dhp = _mm("mlp_down_bwd", d3b, wfull["w_down"], nt=True, extras=(hact,), out_dtypes=(BF,),
              epi=lambda acc, h: (acc * 2.0 * jnp.sqrt(h.astype(F32)),))
    gw["w_down"] = _mm_tn("mlp_down_wgrad", hact, d3b)
    dum = _mm("mlp_up_bwd", dhp, wfull["w_up"], nt=True)
    gw["w_up"] = _mm_tn("mlp_up_wgrad", um, dhp)

    def norm_bwd(a, dn, dres, g):
        dx, dg = _rms_bwd(a, g, dn)
        return dres + dx, dres + dx, dg

    def swap_of(names):
        return _swap_side([gw[n] for n in names], [COL_SHARDED[n] for n in names])

    (d2, d2b, dg_mlp), theirs = _rowcall("mlp_norm_bwd", norm_bwd, t, tr,
                                         [(x2, D_MODEL, 0), (dum, D_MODEL, 0), (d3, D_MODEL, 0)], [g_mlp],
                                         [(D_MODEL, F32), (D_MODEL, BF)], accs=[(1, D_MODEL)], side=swap_of(WAVE_MLP))
    parts_mlp, parts_bf_mlp = _pair_sums(WAVE_MLP, gw, theirs)

    dco = _mm("cross_out_bwd", d2b, wfull["w_co"], nt=True, out_dtypes=(BF,))
    gw["w_co"] = _mm_tn("cross_out_wgrad", co, d2b)
    dcq, dckv = _cross_bwd(cq, ckv, dco, t)
    gw["w_cq"] = _mm_tn("cross_q_wgrad", uq, dcq)
    duq = _mm("cross_q_bwd", dcq, wfull["w_cq"], nt=True)
    d1, d1b, dg_cross = _rowcall("cross_norm_bwd", norm_bwd, t, tr, [(x1, D_MODEL, 0), (duq, D_MODEL, 0), (d2, D_MODEL, 0)],
                                 [g_cross], [(D_MODEL, F32), (D_MODEL, BF)], accs=[(1, D_MODEL)])
    gw["w_ckv"] = _mm_tn("cross_kv_wgrad", mn, dckv, tk=N_MEM)
    dmn = _mm("cross_kv_bwd", dckv, wfull["w_ckv"], nt=True)
    (dg_mem,) = _rowcall("mem_norm_bwd", lambda a, dn, g: (_rms_bwd(a, g, dn)[1],), N_MEM, N_MEM,
                         [(mem, D_MODEL, 0), (dmn, D_MODEL, 0)], [g_mem], [], accs=[(1, D_MODEL)])

    dmerged = _mm("out_proj_bwd", d1b, wfull["w_out"], nt=True)
    gw["w_out"] = _mm_tn("out_proj_wgrad", merged, d1b)
    dya, dyc, dgate, dbg = _rowcall("gate_bwd", _gate_bwd_fn, t, tr,
                                    [(dmerged, D_MODEL, 0), (y_attn, D_MODEL, 0), (y_conv, D_MODEL, 0)] + gate_ins, [b_gate],
                                    [(D_MODEL, BF), (D_MODEL, BF), (2 * D_MODEL, BF)], accs=[(1, 2 * D_MODEL)])
    gw["w_attn_proj"] = _mm_tn("attn_proj_wgrad", attn, dya)
    dattn = _mm("attn_proj_bwd", dya, wfull["w_attn_proj"], nt=True)
    gw["w_conv_proj"] = _mm_tn("conv_proj_wgrad", c2, dyc)
    dc2 = _mm("conv_proj_bwd", dyc, wfull["w_conv_proj"], nt=True)
    (dc1, dlng, dlnb), theirs = _rowcall("conv_ln_silu_bwd", _ln_silu_bwd_fn, t, tr, [(c1, CONV_CH, 0), (dc2, CONV_CH, 0)],
                                         [ln_g, ln_b], [(CONV_CH, F32)], accs=[(1, CONV_CH), (1, CONV_CH)],
                                         side=swap_of(WAVE_MID))
    parts_mid, parts_bf_mid = _pair_sums(WAVE_MID, gw, theirs)
    (dglu_a, dglu_b, dconv), landed = _conv_bwd(z, dc1, w32, t, side=_scatter_side([parts_bf_mlp[n] for n in WAVE_MLP]))
    halves_mlp = _chip_sums(WAVE_MLP, parts_mlp, landed)

    dl0, cls1, cls2 = _attn_bwd_prep(dattn, attn, lse, t)
    dqs, dks, dvs = [], [], []
    for g, (dil, (da_c, dl_c, ls_c)) in enumerate(zip(DILATIONS, ((dattn, dl0, lse), cls1, cls2))):
        res = _attn_bwd(f"attn_bwd_{g}", *qkv_cls[g], da_c, dl_c, ls_c, t, dil,
                        side=_share_side([halves_mlp[n] for n in WAVE_MLP]) if g == 0 else None)
        if g == 0:
            res, others = res
            gshard = {n: _both_halves(halves_mlp[n], o) for n, o in zip(WAVE_MLP, others)}
        dqs.append(res[0])
        dks.append(res[1])
        dvs.append(res[2])
    (dz,), landed = _rope_bwd_join(dqs, dks, dvs, c_tab, s_tab, dglu_a, dglu_b, dgate, t,
                                   side=_scatter_side([parts_bf_mid[n] for n in WAVE_MID]))
    halves_mid = _chip_sums(WAVE_MID, parts_mid, landed)

    gw_in_full, others = _mm_tn("in_proj_wgrad", u, dz, side=_share_side([halves_mid[n] for n in WAVE_MID]))
    gshard.update({n: _both_halves(halves_mid[n], o) for n, o in zip(WAVE_MID, others)})
    gw_in = {"w_in": gw_in_full}
    du_a, theirs = _mm("in_proj_bwd_a", dz, wfull["w_in"], nt=True, k_part=(0, 2), side=_swap_side([gw_in["w_in"]], [True]))
    parts, parts_bf = _pair_sums(("w_in",), gw_in, theirs)
    du, landed_in = _mm("in_proj_bwd_b", dz, wfull["w_in"], nt=True, k_part=(1, 2), extras=(du_a,), epi=add,
                        side=_scatter_side([parts_bf["w_in"]]))
    halves = _chip_sums(("w_in",), parts, landed_in)
    (gx, dg_mix), others = _rowcall("mix_norm_bwd", lambda a, dn, dres, g: norm_bwd(a, dn, dres, g)[1:], t, tr,
                                    [(x, D_MODEL, 0), (du, D_MODEL, 0), (d1, D_MODEL, 0)], [g_mix], [(D_MODEL, F32)],
                                    accs=[(1, D_MODEL)], side=_share_side([halves["w_in"]]))
    gshard["w_in"] = _both_halves(halves["w_in"], others[0])

    gsmall = {"g_mix": dg_mix, "b_gate": dbg, "conv_b": dconv[CONV_K:CONV_K + 1], "conv_ln_g": dlng, "conv_ln_b": dlnb,
              "g_cross": dg_cross, "g_mem": dg_mem, "g_mlp": dg_mlp, "g_final": dg_final, "conv_w": dconv[:CONV_K]}
    return loss_row, gx, gshard, gsmall


def kernel(x, mem, g_mix, w_in, b_gate, conv_w, conv_b, conv_ln_g, conv_ln_b, w_attn_proj, w_conv_proj, w_out, g_cross, g_mem, w_cq, w_ckv, w_co, g_mlp, w_up, w_down, g_final, loss_target, m_g_mix, m_w_in, m_b_gate, m_conv_w, m_conv_b, m_conv_ln_g, m_conv_ln_b, m_w_attn_proj, m_w_conv_proj, m_w_out, m_g_cross, m_g_mem, m_w_cq, m_w_ckv, m_w_co, m_g_mlp, m_w_up, m_w_down, m_g_final, v_g_mix, v_w_in, v_b_gate, v_conv_w, v_conv_b, v_conv_ln_g, v_conv_ln_b, v_w_attn_proj, v_w_conv_proj, v_w_out, v_g_cross, v_g_mem, v_w_cq, v_w_ckv, v_w_co, v_g_mlp, v_w_up, v_w_down, v_g_final):
    w = dict(g_mix=g_mix, w_in=w_in, b_gate=b_gate, conv_w=conv_w, conv_b=conv_b, conv_ln_g=conv_ln_g, conv_ln_b=conv_ln_b,
             w_attn_proj=w_attn_proj, w_conv_proj=w_conv_proj, w_out=w_out, g_cross=g_cross, g_mem=g_mem, w_cq=w_cq,
             w_ckv=w_ckv, w_co=w_co, g_mlp=g_mlp, w_up=w_up, w_down=w_down, g_final=g_final)
    mo = dict(g_mix=m_g_mix, w_in=m_w_in, b_gate=m_b_gate, conv_w=m_conv_w, conv_b=m_conv_b, conv_ln_g=m_conv_ln_g,
              conv_ln_b=m_conv_ln_b, w_attn_proj=m_w_attn_proj, w_conv_proj=m_w_conv_proj, w_out=m_w_out, g_cross=m_g_cross,
              g_mem=m_g_mem, w_cq=m_w_cq, w_ckv=m_w_ckv, w_co=m_w_co, g_mlp=m_g_mlp, w_up=m_w_up, w_down=m_w_down,
              g_final=m_g_final)
    vo = dict(g_mix=v_g_mix, w_in=v_w_in, b_gate=v_b_gate, conv_w=v_conv_w, conv_b=v_conv_b, conv_ln_g=v_conv_ln_g,
              conv_ln_b=v_conv_ln_b, w_attn_proj=v_w_attn_proj, w_conv_proj=v_w_conv_proj, w_out=v_w_out, g_cross=v_g_cross,
              g_mem=v_g_mem, w_cq=v_w_cq, w_ckv=v_w_ckv, w_co=v_w_co, g_mlp=v_g_mlp, w_up=v_w_up, w_down=v_w_down,
              g_final=v_g_final)
    shapes = {n: w[n].shape for n in ORDER}
    two_d = lambda a: a.reshape(a.shape[-2], a.shape[-1])
    chip = 2 * lax.axis_index("x") + lax.axis_index("y")

    shards = {n: two_d(w[n]).astype(BF) for n in BIG}
    cw_rows = 48
    cw_all, _ = _gather8("gather_conv_w", _pad_rows(conv_w.reshape(-1), cw_rows))
    cw_shard = CONV_K * (CONV_CH // 4)
    conv_w_full = jnp.concatenate(
        [cw_all[2 * j].reshape(-1)[:cw_shard].reshape(CONV_K, CONV_CH // 4) for j in range(4)], axis=1)

    small = {n: w[n] for n in SMALL}
    loss_row, gx, gshard, gsmall = _local_step(two_d(x), two_d(mem), two_d(loss_target), shards, small, conv_w_full)
    loss = lax.psum(loss_row[0, 0], ("x", "y", "c"))

    small_names = SMALL + ("conv_w",)
    flat = jnp.concatenate([gsmall[n].reshape(-1) for n in small_names])
    sm_rows = -(-flat.shape[0] // (8 * LANES)) * 8
    _, sm_sum = _gather8("reduce_small_grads", _pad_rows(flat, sm_rows))
    sm_sum = sm_sum.reshape(-1)
    off = 0
    for n in small_names:
        size = gsmall[n].size
        gshard[n] = sm_sum[off:off + size].reshape(gsmall[n].shape)
        off += size
    gshard["conv_w"] = lax.dynamic_slice_in_dim(gshard["conv_w"], chip * (CONV_CH // 4), CONV_CH // 4, axis=1)

    grads, deltas, new_m, new_v = {}, {}, {}, {}
    for n in BIG:
        d, m2, v2 = _adam(f"adamw_{n}", two_d(w[n]), gshard[n], two_d(mo[n]), two_d(vo[n]))
        grads[n], deltas[n], new_m[n], new_v[n] = (a.reshape(shapes[n]) for a in (gshard[n], d, m2, v2))
    pack = lambda src: jnp.concatenate([src[n].reshape(-1) for n in small_names])
    n_small = sum(w[n].size for n in small_names)
    ad_rows = -(-n_small // (8 * LANES)) * 8
    d, m2, v2 = _adam("adamw_small", *[_pad_rows(pack(src), ad_rows) for src in (w, gshard, mo, vo)])
    off = 0
    for n in small_names:
        size = w[n].size
        grads[n] = gshard[n].reshape(shapes[n])
        deltas[n], new_m[n], new_v[n] = (a.reshape(-1)[off:off + size].reshape(shapes[n]) for a in (d, m2, v2))
        off += size

    return (loss, gx.reshape(x.shape), *[grads[n] for n in ORDER], *[deltas[n] for n in ORDER],
            *[new_m[n] for n in ORDER], *[new_v[n] for n in ORDER])
```

```python
import functools

import jax
import jax.numpy as jnp
from jax import lax
from jax.experimental import pallas as pl
from jax.experimental.pallas import tpu as pltpu

F32 = jnp.float32
BF = jnp.bfloat16

D_MODEL = 1024
N_MEM = 256
HEAD_DIM = 128
HEADS_PER_GROUP = 4
DILATIONS = (1, 4, 16)
BLK = 128
GROUP_W = HEADS_PER_GROUP * HEAD_DIM
ATTN_WIDTH = 3 * GROUP_W
ROT_DIM = 32
ROPE_THETA = 500000.0
CONV_CH = 768
CONV_K = 31
CONV_KP = 32
IN_WIDTH = 8192
CROSS_HEADS = 4
CROSS_HEAD_DIM = 256
D_FF = 4096
EPS = 1e-6
ATTN_SCALE = HEAD_DIM ** -0.5
CROSS_SCALE = CROSS_HEAD_DIM ** -0.5
NEG = -1e30

ADAM_LR = 0.001
ADAM_B1 = 0.9
ADAM_B2 = 0.999
ADAM_EPS = 1e-08
ADAM_WD = 0.01
ADAM_STEP = 10

LANES = 128
VMEM_LIMIT = 56 * 1024 * 1024
MESH = pl.DeviceIdType.MESH
ANY = pl.BlockSpec(memory_space=pl.ANY)

GLU_A_COL = 3 * ATTN_WIDTH
GLU_B_COL = GLU_A_COL + CONV_CH
GATE_A_COL = GLU_B_COL + CONV_CH
GATE_B_COL = GATE_A_COL + D_MODEL


def _params(sem=None):
    return pltpu.CompilerParams(dimension_semantics=sem, vmem_limit_bytes=VMEM_LIMIT)


def _dot(a, b):
    return lax.dot_general(a, b, (((1,), (0,)), ((), ())), preferred_element_type=F32)


def _dot_nt(a, b):
    return lax.dot_general(a, b, (((1,), (1,)), ((), ())), preferred_element_type=F32)


def _dot_tn(a, b):
    return lax.dot_general(a, b, (((0,), (0,)), ((), ())), preferred_element_type=F32)


def _sig(x):
    return 1.0 / (1.0 + jnp.exp(-x))


def _glu(a, b):
    return a.astype(F32) * _sig(b.astype(F32))


class _Side:
    def __init__(self, arrays, out_shapes, n_sems, build, aliases=None):
        self.arrays, self.out_shapes, self.n_sems, self.build = list(arrays), list(out_shapes), n_sems, build
        self.aliases = aliases or {}


def _pcall(name, kern, grid, in_specs, out_specs, out_shape, scratch_shapes, sem, args, side=None):
    in_specs, out_specs, out_shape, scratch_shapes = list(in_specs), list(out_specs), list(out_shape), list(scratch_shapes)
    if side is None:
        return pl.pallas_call(kern, name=name, grid=grid, in_specs=in_specs, out_specs=out_specs, out_shape=out_shape,
                              scratch_shapes=scratch_shapes, compiler_params=_params(sem))(*args)
    ni, no, nsc = len(in_specs), len(out_specs), len(scratch_shapes)
    nsi, nso = len(side.arrays), len(side.out_shapes)

    def wrapped(*refs):
        ins, side_ins = refs[:ni], refs[ni:ni + nsi]
        outs, side_outs = refs[ni + nsi:ni + nsi + no], refs[ni + nsi + no:ni + nsi + no + nso]
        scratch = refs[ni + nsi + no + nso:ni + nsi + no + nso + nsc]
        send_sems, recv_sems = refs[-2:]
        start, finish = side.build(side_ins, side_outs, send_sems, recv_sems)
        if grid:
            first = functools.reduce(jnp.logical_and, [pl.program_id(a) == 0 for a in range(len(grid))])
            last = functools.reduce(jnp.logical_and, [pl.program_id(a) == g - 1 for a, g in enumerate(grid)])
            pl.when(first)(start)
            kern(*ins, *outs, *scratch)
            pl.when(last)(finish)
        else:
            start()
            kern(*ins, *outs, *scratch)
            finish()

    res = pl.pallas_call(
        wrapped, name=name, grid=grid, in_specs=in_specs + [ANY] * nsi, out_specs=out_specs + [ANY] * nso,
        out_shape=out_shape + side.out_shapes,
        scratch_shapes=scratch_shapes + [pltpu.SemaphoreType.DMA((side.n_sems,)), pltpu.SemaphoreType.DMA((side.n_sems,))],
        input_output_aliases={ni + k: no + v for k, v in side.aliases.items()},
        compiler_params=_params(("arbitrary",) * len(grid) if grid else None),
    )(*args, *side.arrays)
    return res[:no], res[no:]


def _rowcall(name, fn, n_rows, tile, ins, params, outs, accs=(), side=None):
    tile = min(tile, n_rows)
    ni, npar, no, na = len(ins), len(params), len(outs), len(accs)

    def kern(*refs):
        in_refs = refs[:ni + npar]
        o_refs = refs[ni + npar:ni + npar + no]
        a_refs = refs[ni + npar + no:]
        vals = fn(*[r[...] for r in in_refs])
        for r, v in zip(o_refs, vals[:no]):
            r[...] = v.astype(r.dtype)
        if na:
            @pl.when(pl.program_id(0) == 0)
            def _():
                for r in a_refs:
                    r[...] = jnp.zeros_like(r)
            for r, v in zip(a_refs, vals[no:]):
                r[...] += v

    in_specs = []
    arrays = []
    for spec in ins:
        arr, width, cb = spec[0], spec[1], spec[2]
        rb = spec[3] if len(spec) > 3 else 0
        in_specs.append(pl.BlockSpec((tile, width), functools.partial(lambda i, cb, rb: (i + rb, cb), cb=cb, rb=rb)))
        arrays.append(arr)
    for p in params:
        in_specs.append(pl.BlockSpec(p.shape, lambda i: (0, 0)))
        arrays.append(p)
    out_specs = [pl.BlockSpec((tile, w), lambda i: (i, 0)) for w, _ in outs]
    out_specs += [pl.BlockSpec(s, lambda i: (0, 0)) for s in accs]
    out_shape = [jax.ShapeDtypeStruct((n_rows, w), dt) for w, dt in outs]
    out_shape += [jax.ShapeDtypeStruct(s, F32) for s in accs]
    return _pcall(name, kern, (n_rows // tile,), in_specs, out_specs, out_shape, [],
                  ("arbitrary",) if na else ("parallel",), arrays, side)


def _mm(name, a, w3, *, nt=False, extras=(), params=(), epi=None, out_dtypes=(F32,), sums=(), tm=None, tn=None, tk=None,
        k_part=(0, 1), side=None):
    m, ka = a.shape
    ns, r, cs = w3.shape
    if not nt:
        k_dim, n = r, ns * cs
        tn = tn or min(cs, 1024)
        tk = tk or min(k_dim, 1024)
    else:
        k_dim, n = ns * cs, r
        tn = tn or min(r, 1024)
        tk = tk or min(cs, 1024)
    assert ka == k_dim, (name, a.shape, w3.shape)
    assert not sums or tn == n, name
    nk = k_dim // tk // k_part[1]
    k0 = k_part[0] * nk
    if not nt:
        nbs = cs // tn
        w_spec = pl.BlockSpec((None, tk, tn), lambda i, j, k: (j // nbs, k + k0, j % nbs))
    else:
        kbs = cs // tk
        w_spec = pl.BlockSpec((None, tn, tk), lambda i, j, k: ((k + k0) // kbs, j, (k + k0) % kbs))
    tm = tm or min(m, 1024)
    ne, no, nsum = len(extras) + len(params), len(out_dtypes), len(sums)

    def kern(a_ref, w_ref, *rest):
        e_refs = rest[:ne]
        o_refs = rest[ne:ne + no]
        s_refs = rest[ne + no:ne + no + nsum]

        def part():
            av = a_ref[...].astype(BF)
            return _dot_nt(av, w_ref[...]) if nt else _dot(av, w_ref[...])

        def finish(res):
            vals = epi(res, *[e[...] for e in e_refs]) if epi else (res,)
            for o, v in zip(o_refs, vals[:no]):
                o[...] = v.astype(o.dtype)
            for sr, v in zip(s_refs, vals[no:]):
                sr[...] += v

        if nsum:
            @pl.when(jnp.logical_and(pl.program_id(0) == 0, pl.program_id(2) == 0))
            def _():
                for sr in s_refs:
                    sr[...] = jnp.zeros_like(sr)

        if nk == 1:
            finish(part())
            return
        acc = rest[ne + no + nsum]
        k = pl.program_id(2)

        @pl.when(k == 0)
        def _():
            acc[...] = part()

        @pl.when(jnp.logical_and(k > 0, k < nk - 1))
        def _():
            acc[...] += part()

        @pl.when(k == nk - 1)
        def _():
            finish(acc[...] + part())

    in_specs = [pl.BlockSpec((tm, tk), lambda i, j, k: (i, k + k0)), w_spec]
    in_specs += [pl.BlockSpec((tm, tn), lambda i, j, k: (i, j)) for _ in extras]
    in_specs += [pl.BlockSpec(p.shape, lambda i, j, k: (0, 0)) for p in params]
    out_specs = [pl.BlockSpec((tm, tn), lambda i, j, k: (i, j)) for _ in out_dtypes]
    out_specs += [pl.BlockSpec(sh, lambda i, j, k: (0, 0)) for sh in sums]
    out_shape = [jax.ShapeDtypeStruct((m, n), dt) for dt in out_dtypes] + [jax.ShapeDtypeStruct(sh, F32) for sh in sums]
    res = _pcall(name, kern, (m // tm, n // tn, nk), in_specs, out_specs, out_shape,
                 [pltpu.VMEM((tm, tn), F32)] if nk > 1 else [],
                 ("arbitrary",) * 3 if nsum else ("parallel", "parallel", "arbitrary"), (a, w3, *extras, *params), side)
    main = res[0] if side is not None else res
    main = main[0] if no + nsum == 1 else main
    return (main, res[1]) if side is not None else main


def _mm_tn(name, a, b, tm=None, tn=None, tk=None, side=None):
    t, ka = a.shape
    _, n = b.shape
    tm = tm or min(ka, 1024)
    tn = tn or min(n, 1024)
    tk = tk or min(t, 1024)

    def kern(a_ref, b_ref, o_ref):
        def part():
            return _dot_tn(a_ref[...].astype(BF), b_ref[...].astype(BF))

        @pl.when(pl.program_id(2) == 0)
        def _():
            o_ref[...] = part()

        @pl.when(pl.program_id(2) > 0)
        def _():
            o_ref[...] += part()

    res = _pcall(name, kern, (ka // tm, n // tn, t // tk),
                 [pl.BlockSpec((tk, tm), lambda i, j, k: (k, i)), pl.BlockSpec((tk, tn), lambda i, j, k: (k, j))],
                 [pl.BlockSpec((tm, tn), lambda i, j, k: (i, j))], [jax.ShapeDtypeStruct((ka, n), F32)], [],
                 ("parallel", "parallel", "arbitrary"), (a, b), side)
    return (res[0][0], res[1]) if side is not None else res[0]


def _rms(x, g):
    return x * lax.rsqrt(jnp.mean(x * x, axis=-1, keepdims=True) + EPS) * g


def _rms_bwd(x, g, dy):
    r = lax.rsqrt(jnp.mean(x * x, axis=-1, keepdims=True) + EPS)
    xh = x * r
    dxh = dy * g
    dx = r * (dxh - xh * jnp.mean(dxh * xh, axis=-1, keepdims=True))
    return dx, jnp.sum(dy * xh, axis=0, keepdims=True)


def _rot(t, c, s):
    lane = lax.broadcasted_iota(jnp.int32, t.shape, 1)
    swapped = jnp.where(lane < ROT_DIM // 2, pltpu.roll(t, HEAD_DIM - ROT_DIM // 2, 1), pltpu.roll(t, ROT_DIM // 2, 1))
    return t * c + swapped * s


def _rope_tables(t):
    half = ROT_DIM // 2
    pos = jnp.arange(t, dtype=F32)
    inv_freq = ROPE_THETA ** (-jnp.arange(0, ROT_DIM, 2, dtype=F32) / ROT_DIM)
    ang = pos[:, None] * inv_freq[None, :]
    cos, sin = jnp.cos(ang), jnp.sin(ang)
    ones = jnp.ones((t, HEAD_DIM - ROT_DIM), F32)
    c_tab = jnp.concatenate([cos, cos, ones], axis=1)
    s_tab = jnp.concatenate([-sin, sin, 0.0 * ones], axis=1)
    return c_tab, s_tab


def _merge_fn(o0, o1, o2, l0, l1, l2):
    m = jnp.maximum(jnp.maximum(l0, l1), l2)
    e0, e1, e2 = jnp.exp(l0 - m), jnp.exp(l1 - m), jnp.exp(l2 - m)
    tot = e0 + e1 + e2
    return (e0 * o0 + e1 * o1 + e2 * o2) / tot, m + jnp.log(tot)


def _ln_parts(c1):
    mu = jnp.mean(c1, axis=-1, keepdims=True)
    xc = c1 - mu
    r = lax.rsqrt(jnp.mean(xc * xc, axis=-1, keepdims=True) + EPS)
    return xc * r, r


def _ln_silu_fn(c1, g, b):
    xh, _ = _ln_parts(c1)
    yl = xh * g + b
    return (yl * _sig(yl),)


def _ln_silu_bwd_fn(c1, dout, g, b):
    xh, r = _ln_parts(c1)
    yl = xh * g + b
    s = _sig(yl)
    dyl = dout * (s + yl * s * (1.0 - s))
    dxh = dyl * g
    dx = r * (dxh - jnp.mean(dxh, axis=-1, keepdims=True) - xh * jnp.mean(dxh * xh, axis=-1, keepdims=True))
    return dx, jnp.sum(dyl * xh, axis=0, keepdims=True), jnp.sum(dyl, axis=0, keepdims=True)


def _gate_fn(ya, yc, ga, gb, bg):
    sa = _sig(ga + bg[:, :D_MODEL])
    sb = _sig(gb + bg[:, D_MODEL:])
    return (sa * ya + sb * yc,)


def _gate_bwd_fn(dm, ya, yc, ga, gb, bg):
    sa = _sig(ga + bg[:, :D_MODEL])
    sb = _sig(gb + bg[:, D_MODEL:])
    dga = dm * ya * sa * (1.0 - sa)
    dgb = dm * yc * sb * (1.0 - sb)
    dgate = jnp.concatenate([dga, dgb], axis=1)
    return dm * sa, dm * sb, dgate, jnp.sum(dgate, axis=0, keepdims=True)


def _res_norm_epi(acc, res, g):
    xn = res + acc
    return xn, _rms(xn, g)


def _norm_bwd_epi(acc, a, dres, g):
    dx, dg = _rms_bwd(a, g, acc)
    return dres + dx, dres + dx, dg


def _final_epi(acc, res, tgt, g):
    return _final_fn(res + acc, tgt, g)


def _final_fn(x3, tgt, g):
    err = _rms(x3, g) - tgt
    lrow = jnp.sum(err * err, axis=-1, keepdims=True) * (0.5 / D_MODEL)
    lsum = jnp.sum(lrow, axis=0, keepdims=True)
    dx, dg = _rms_bwd(x3, g, err * (1.0 / D_MODEL))
    return dx, dx, jnp.broadcast_to(lsum, (1, LANES)), dg


def _attn_geometry(t, dil):
    cls = t // dil
    rows = min(4 * BLK, cls)
    return rows, rows // BLK, cls // rows


def _head_lanes(h):
    return slice(h * HEAD_DIM, (h + 1) * HEAD_DIM)


def _band_mask():
    row = lax.broadcasted_iota(jnp.int32, (BLK, 2 * BLK), 0)
    col = lax.broadcasted_iota(jnp.int32, (BLK, 2 * BLK), 1)
    return jnp.logical_and(col >= row, col <= row + BLK), col


def _stage_window(scr, halo_ref, cur_ref):
    scr[:BLK, :] = halo_ref[...]
    scr[BLK:, :] = cur_ref[...]


def _attn_fwd(name, q, k, v, t, dil):
    rows, nbk, spc = _attn_geometry(t, dil)

    def kern(q_ref, k_ref, kh_ref, v_ref, vh_ref, o_ref, l_ref, k_scr, v_scr):
        i = pl.program_id(0)
        first_shift = jnp.where(i % spc == 0, BLK, 0)
        _stage_window(k_scr, kh_ref, k_ref)
        _stage_window(v_scr, vh_ref, v_ref)
        band, col = _band_mask()
        band_first = jnp.logical_and(band, col >= first_shift)
        for h in range(HEADS_PER_GROUP):
            hs = _head_lanes(h)
            for b in range(nbk):
                rs, win = slice(b * BLK, (b + 1) * BLK), slice(b * BLK, (b + 2) * BLK)
                s = jnp.where(band_first if b == 0 else band, _dot_nt(q_ref[rs, hs], k_scr[win, hs]) * ATTN_SCALE, NEG)
                m = jnp.max(s, axis=1, keepdims=True)
                p = jnp.exp(s - m)
                tot = jnp.sum(p, axis=1, keepdims=True)
                o_ref[rs, hs] = _dot(p.astype(BF), v_scr[win, hs]) / tot
                l_ref[rs, hs] = jnp.broadcast_to(m + jnp.log(tot), (BLK, HEAD_DIM))

    def cur(cb):
        return pl.BlockSpec((rows, GROUP_W), lambda i: (i, cb))

    def halo(cb):
        return pl.BlockSpec((BLK, GROUP_W), lambda i: (jnp.maximum(i * nbk - 1, 0), cb))

    (qa, qc), (ka, kc_), (va, vc_) = q, k, v
    return _pcall(name, kern, (t // rows,), [cur(qc), cur(kc_), halo(kc_), cur(vc_), halo(vc_)],
                  [pl.BlockSpec((rows, GROUP_W), lambda i: (i, 0))] * 2, [jax.ShapeDtypeStruct((t, GROUP_W), F32)] * 2,
                  [pltpu.VMEM((rows + BLK, GROUP_W), BF)] * 2, ("parallel",), (qa, ka, ka, va, va))


def _attn_bwd(name, q, k, v, da, dl, lse, t, dil, side=None):
    rows, nbk, spc = _attn_geometry(t, dil)
    nblk = t // BLK

    def kern(q_ref, qn_ref, k_ref, kh_ref, v_ref, vh_ref, da_ref, dan_ref, dl_ref, dln_ref, ls_ref, lsn_ref,
             dq_ref, dk_ref, dv_ref, k_scr, v_scr):
        i = pl.program_id(0)
        first_shift = jnp.where(i % spc == 0, BLK, 0)
        next_shift = jnp.where((i + 1) % spc == 0, BLK, 0)
        _stage_window(k_scr, kh_ref, k_ref)
        _stage_window(v_scr, vh_ref, v_ref)
        band, col = _band_mask()
        band_first = jnp.logical_and(band, col >= first_shift)
        row1 = lax.broadcasted_iota(jnp.int32, (BLK, BLK), 0)
        col1 = lax.broadcasted_iota(jnp.int32, (BLK, BLK), 1)
        for h in range(HEADS_PER_GROUP):
            hs = _head_lanes(h)
            for b in range(nbk):
                rs, win = slice(b * BLK, (b + 1) * BLK), slice(b * BLK, (b + 2) * BLK)
                qb, dab = q_ref[rs, hs], da_ref[rs, hs].astype(BF)
                kw, vw = k_scr[win, hs], v_scr[win, hs]
                p = jnp.where(band_first if b == 0 else band,
                              jnp.exp(_dot_nt(qb, kw) * ATTN_SCALE - ls_ref[rs, hs][:, :1]), 0.0)
                ds = (p * (_dot_nt(dab, vw) - dl_ref[rs, hs][:, :1]) * ATTN_SCALE).astype(BF)
                dq_ref[rs, hs] = _dot(ds, kw)
                dkw, dvw = _dot_tn(ds, qb), _dot_tn(p.astype(BF), dab)
                if b >= 1:
                    ps = slice((b - 1) * BLK, b * BLK)
                    dk_ref[ps, hs] = pend_k + dkw[:BLK]
                    dv_ref[ps, hs] = pend_v + dvw[:BLK]
                pend_k, pend_v = dkw[BLK:], dvw[BLK:]
            ls_rows = slice((nbk - 1) * BLK, nbk * BLK)
            last = slice(nbk * BLK, (nbk + 1) * BLK)
            qb, dab = qn_ref[:, hs], dan_ref[:, hs].astype(BF)
            kp, vp = k_scr[last, hs], v_scr[last, hs]
            p = jnp.where(col1 >= row1 + next_shift, jnp.exp(_dot_nt(qb, kp) * ATTN_SCALE - lsn_ref[:, hs]), 0.0)
            ds = (p * (_dot_nt(dab, vp) - dln_ref[:, hs]) * ATTN_SCALE).astype(BF)
            dk_ref[ls_rows, hs] = pend_k + _dot_tn(ds, qb)
            dv_ref[ls_rows, hs] = pend_v + _dot_tn(p.astype(BF), dab)

    def cur(cb):
        return pl.BlockSpec((rows, GROUP_W), lambda i: (i, cb))

    def prev(cb):
        return pl.BlockSpec((BLK, GROUP_W), lambda i: (jnp.maximum(i * nbk - 1, 0), cb))

    def nxt(cb):
        return pl.BlockSpec((BLK, GROUP_W), lambda i: (jnp.minimum((i + 1) * nbk, nblk - 1), cb))

    (qa, qc), (ka, kc_), (va, vc_) = q, k, v
    return _pcall(name, kern, (t // rows,),
                  [cur(qc), nxt(qc), cur(kc_), prev(kc_), cur(vc_), prev(vc_), cur(0), nxt(0), cur(0), nxt(0), cur(0), nxt(0)],
                  [pl.BlockSpec((rows, GROUP_W), lambda i: (i, 0))] * 3, [jax.ShapeDtypeStruct((t, GROUP_W), F32)] * 3,
                  [pltpu.VMEM((rows + BLK, GROUP_W), BF)] * 2, ("parallel",),
                  (qa, qa, ka, ka, va, va, da, da, dl, dl, lse, lse), side)


CLS_TILE = 512


def _cls_block(t, tile, dil, dtype):
    if dil == 1:
        return pl.BlockSpec((tile, GROUP_W), lambda i: (i, 0)), jax.ShapeDtypeStruct((t, GROUP_W), dtype)
    return (pl.BlockSpec((dil, tile // dil, GROUP_W), lambda i: (0, i, 0)),
            jax.ShapeDtypeStruct((dil, t // dil, GROUP_W), dtype))


def _head_scratch(tile):
    return pltpu.VMEM((tile, HEAD_DIM), F32)


def _rope_split(z, c_tab, s_tab, t):
    tile = min(CLS_TILE, t)
    n_heads = ATTN_WIDTH // HEAD_DIM

    def kern(zq_ref, zk_ref, zv_ref, c_ref, s_ref, *rest):
        outs, scr = rest[:9], rest[9]
        c, s = c_ref[...], s_ref[...]
        for which, z_ref in enumerate((zq_ref, zk_ref, zv_ref)):
            for h in range(n_heads):
                g, hs = h // HEADS_PER_GROUP, _head_lanes(h % HEADS_PER_GROUP)
                val = z_ref[:, h * HEAD_DIM:(h + 1) * HEAD_DIM].astype(F32)
                if which < 2:
                    val = _rot(val, c, s)
                if g == 0:
                    outs[which][:, hs] = val.astype(BF)
                    continue
                scr[...] = val
                dil = DILATIONS[g]
                for r in range(dil):
                    outs[3 * g + which][r, :, hs] = scr[pl.ds(r, tile // dil, stride=dil), :].astype(BF)

    blocks = [_cls_block(t, tile, DILATIONS[g], BF) for g in range(3) for _ in range(3)]
    zspec = lambda cb: pl.BlockSpec((tile, ATTN_WIDTH), lambda i: (i, cb))
    tab = pl.BlockSpec((tile, HEAD_DIM), lambda i: (i, 0))
    return _pcall("rope", kern, (t // tile,), [zspec(0), zspec(1), zspec(2), tab, tab], [b[0] for b in blocks],
                  [b[1] for b in blocks], [_head_scratch(tile)], ("parallel",), (z, z, z, c_tab, s_tab))


def _merge_classes(outs, lses, t):
    tile = min(CLS_TILE, t)

    def kern(o0, l0, o1, l1, o2, l2, attn_ref, lse_ref, s_o1, s_l1, s_o2, s_l2):
        for h in range(HEADS_PER_GROUP):
            hs = _head_lanes(h)
            for src, dst, dil in ((o1, s_o1, DILATIONS[1]), (l1, s_l1, DILATIONS[1]), (o2, s_o2, DILATIONS[2]),
                                  (l2, s_l2, DILATIONS[2])):
                for r in range(dil):
                    dst[pl.ds(r, tile // dil, stride=dil), :] = src[r, :, hs]
            attn_ref[:, hs], lse_ref[:, hs] = _merge_fn(o0[:, hs], s_o1[...], s_o2[...], l0[:, hs], s_l1[...], s_l2[...])

    blocks = [_cls_block(t, tile, DILATIONS[g], F32) for g in range(3)]
    args = []
    for g in range(3):
        args += [outs[g].reshape(blocks[g][1].shape), lses[g].reshape(blocks[g][1].shape)]
    tok = pl.BlockSpec((tile, GROUP_W), lambda i: (i, 0))
    return _pcall("attn_merge", kern, (t // tile,), [blocks[g][0] for g in range(3) for _ in range(2)], [tok, tok],
                  [jax.ShapeDtypeStruct((t, GROUP_W), F32)] * 2, [_head_scratch(tile)] * 4, ("parallel",), args)


def _attn_bwd_prep(dattn, attn, lse, t):
    tile = min(CLS_TILE, t)

    def kern(da_ref, at_ref, ls_ref, dl0, da1, dl1, ls1, da2, dl2, ls2, s_da, s_dl, s_ls):
        for h in range(HEADS_PER_GROUP):
            hs = _head_lanes(h)
            da = da_ref[:, hs]
            s_da[...] = da
            s_dl[...] = jnp.broadcast_to(jnp.sum(da * at_ref[:, hs], axis=1, keepdims=True), (tile, HEAD_DIM))
            s_ls[...] = ls_ref[:, hs]
            dl0[:, hs] = s_dl[...]
            for oda, odl, ols, dil in ((da1, dl1, ls1, DILATIONS[1]), (da2, dl2, ls2, DILATIONS[2])):
                for r in range(dil):
                    rows = pl.ds(r, tile // dil, stride=dil)
                    oda[r, :, hs] = s_da[rows, :].astype(BF)
                    odl[r, :, hs] = s_dl[rows, :]
                    ols[r, :, hs] = s_ls[rows, :]

    tok = pl.BlockSpec((tile, GROUP_W), lambda i: (i, 0))
    blocks = [(tok, jax.ShapeDtypeStruct((t, GROUP_W), F32))]
    for g in (1, 2):
        blocks += [_cls_block(t, tile, DILATIONS[g], BF), _cls_block(t, tile, DILATIONS[g], F32),
                   _cls_block(t, tile, DILATIONS[g], F32)]
    res = _pcall("attn_bwd_prep", kern, (t // tile,), [tok, tok, tok], [b[0] for b in blocks], [b[1] for b in blocks],
                 [_head_scratch(tile)] * 3, ("parallel",), (dattn, attn, lse))
    flat = [a.reshape(t, GROUP_W) for a in res]
    return flat[0], flat[1:4], flat[4:7]


def _rope_bwd_join(dqs, dks, dvs, c_tab, s_tab, dglu_a, dglu_b, dgate, t, side=None):
    tile = min(256, t)
    n_heads = ATTN_WIDTH // HEAD_DIM

    def kern(q0, q1, q2, k0, k1, k2, v0, v1, v2, c_ref, s_ref, ga_ref, gb_ref, gt_ref, dz_ref, scr):
        c, s = c_ref[...], -s_ref[...]
        for which, srcs in enumerate(((q0, q1, q2), (k0, k1, k2), (v0, v1, v2))):
            for h in range(n_heads):
                g, hs = h // HEADS_PER_GROUP, _head_lanes(h % HEADS_PER_GROUP)
                if g == 0:
                    val = srcs[0][:, hs]
                else:
                    dil = DILATIONS[g]
                    for r in range(dil):
                        scr[pl.ds(r, tile // dil, stride=dil), :] = srcs[g][r, :, hs]
                    val = scr[...]
                if which < 2:
                    val = _rot(val, c, s)
                col = which * ATTN_WIDTH + h * HEAD_DIM
                dz_ref[:, col:col + HEAD_DIM] = val.astype(BF)
        dz_ref[:, GLU_A_COL:GLU_B_COL] = ga_ref[...]
        dz_ref[:, GLU_B_COL:GATE_A_COL] = gb_ref[...]
        dz_ref[:, GATE_A_COL:] = gt_ref[...]

    blocks = [_cls_block(t, tile, DILATIONS[g], F32) for g in range(3)]
    args = [a.reshape(blocks[g][1].shape) for grp in (dqs, dks, dvs) for g, a in enumerate(grp)]
    tab = pl.BlockSpec((tile, HEAD_DIM), lambda i: (i, 0))
    row = lambda w: pl.BlockSpec((tile, w), lambda i: (i, 0))
    return _pcall("rope_bwd", kern, (t // tile,), [blocks[g][0] for _ in range(3) for g in range(3)]
                  + [tab, tab, row(CONV_CH), row(CONV_CH), row(2 * D_MODEL)], [row(IN_WIDTH)],
                  [jax.ShapeDtypeStruct((t, IN_WIDTH), BF)], [_head_scratch(tile)], ("parallel",),
                  (*args, c_tab, s_tab, dglu_a, dglu_b, dgate), side)


def _cross_probs(qh, kh):
    s = _dot_nt(qh, kh) * CROSS_SCALE
    e = jnp.exp(s - jnp.max(s, axis=1, keepdims=True))
    return e, jnp.sum(e, axis=1, keepdims=True)


def _cross_fwd(cq, ckv, t):
    rows = min(512, t)

    def kern(q_ref, kv_ref, o_ref):
        for h in range(CROSS_HEADS):
            hs = slice(h * CROSS_HEAD_DIM, (h + 1) * CROSS_HEAD_DIM)
            vs = slice(D_MODEL + h * CROSS_HEAD_DIM, D_MODEL + (h + 1) * CROSS_HEAD_DIM)
            e, tot = _cross_probs(q_ref[:, hs], kv_ref[:, hs])
            o_ref[:, hs] = (_dot(e.astype(BF), kv_ref[:, vs]) / tot).astype(BF)

    return pl.pallas_call(
        kern, name="cross_fwd", grid=(t // rows,),
        in_specs=[pl.BlockSpec((rows, D_MODEL), lambda i: (i, 0)), pl.BlockSpec((N_MEM, 2 * D_MODEL), lambda i: (0, 0))],
        out_specs=pl.BlockSpec((rows, D_MODEL), lambda i: (i, 0)),
        out_shape=jax.ShapeDtypeStruct((t, D_MODEL), BF),
        compiler_params=_params(("parallel",)),
    )(cq, ckv)


def _cross_bwd(cq, ckv, dco, t):
    rows = min(512, t)

    def kern(q_ref, kv_ref, do_ref, dq_ref, dkv_ref):
        @pl.when(pl.program_id(0) == 0)
        def _():
            dkv_ref[...] = jnp.zeros_like(dkv_ref)
        for h in range(CROSS_HEADS):
            hs = slice(h * CROSS_HEAD_DIM, (h + 1) * CROSS_HEAD_DIM)
            vs = slice(D_MODEL + h * CROSS_HEAD_DIM, D_MODEL + (h + 1) * CROSS_HEAD_DIM)
            qh, kh, vh, doh = q_ref[:, hs], kv_ref[:, hs], kv_ref[:, vs], do_ref[:, hs]
            e, tot = _cross_probs(qh, kh)
            p = e / tot
            dp = _dot_nt(doh, vh)
            ds = (p * (dp - jnp.sum(p * dp, axis=1, keepdims=True)) * CROSS_SCALE).astype(BF)
            dq_ref[:, hs] = _dot(ds, kh).astype(BF)
            dkv_ref[:, hs] += _dot_tn(ds, qh)
            dkv_ref[:, vs] += _dot_tn(p.astype(BF), doh)

    return pl.pallas_call(
        kern, name="cross_bwd", grid=(t // rows,),
        in_specs=[pl.BlockSpec((rows, D_MODEL), lambda i: (i, 0)), pl.BlockSpec((N_MEM, 2 * D_MODEL), lambda i: (0, 0)),
                  pl.BlockSpec((rows, D_MODEL), lambda i: (i, 0))],
        out_specs=[pl.BlockSpec((rows, D_MODEL), lambda i: (i, 0)), pl.BlockSpec((N_MEM, 2 * D_MODEL), lambda i: (0, 0))],
        out_shape=[jax.ShapeDtypeStruct((t, D_MODEL), BF), jax.ShapeDtypeStruct((N_MEM, 2 * D_MODEL), F32)],
        compiler_params=_params(("arbitrary",)),
    )(cq, ckv, dco)


CONV_TILE = 512
CONV_CHUNK = 128
HALO = 32


def _conv_fwd(z, w32, bias, t):
    tile = min(CONV_TILE, t)
    a_cb, b_cb = GLU_A_COL // LANES, GLU_B_COL // LANES
    hb = tile // HALO

    def kern(a_ref, b_ref, ah_ref, bh_ref, w_ref, bias_ref, o_ref, g_scr):
        i = pl.program_id(1)
        g_scr[HALO:, :] = _glu(a_ref[...], b_ref[...])
        g_scr[:HALO, :] = _glu(ah_ref[...], bh_ref[...]) * jnp.where(i > 0, 1.0, 0.0)
        for c in range(tile // CONV_CHUNK):
            acc = jnp.broadcast_to(bias_ref[...], (CONV_CHUNK, LANES))
            for j in range(CONV_K):
                lo = c * CONV_CHUNK + HALO - (CONV_K - 1) + j
                acc = acc + w_ref[j:j + 1, :] * g_scr[lo:lo + CONV_CHUNK, :]
            o_ref[c * CONV_CHUNK:(c + 1) * CONV_CHUNK, :] = acc

    def cur(cb):
        return pl.BlockSpec((tile, LANES), lambda j, i: (i, cb + j))

    def prev(cb):
        return pl.BlockSpec((HALO, LANES), lambda j, i: (jnp.maximum(i * hb - 1, 0), cb + j))

    return pl.pallas_call(
        kern, name="conv_fwd", grid=(CONV_CH // LANES, t // tile),
        in_specs=[cur(a_cb), cur(b_cb), prev(a_cb), prev(b_cb),
                  pl.BlockSpec((CONV_KP, LANES), lambda j, i: (0, j)), pl.BlockSpec((1, LANES), lambda j, i: (0, j))],
        out_specs=pl.BlockSpec((tile, LANES), lambda j, i: (i, j)),
        out_shape=jax.ShapeDtypeStruct((t, CONV_CH), F32),
        scratch_shapes=[pltpu.VMEM((tile + HALO, LANES), F32)],
        compiler_params=_params(("parallel", "parallel")),
    )(z, z, z, z, w32, bias)


def _conv_bwd(z, dc1, w32, t, side=None):
    tile = min(CONV_TILE, t)
    a_cb, b_cb = GLU_A_COL // LANES, GLU_B_COL // LANES
    hb = tile // HALO
    n_tiles = t // tile
    n_chunks = tile // CONV_CHUNK

    def kern(a_ref, b_ref, ah_ref, bh_ref, d_ref, dn_ref, w_ref, da_ref, db_ref, dw_ref, g_scr, d_scr):
        i = pl.program_id(1)
        g_scr[HALO:, :] = _glu(a_ref[...], b_ref[...])
        g_scr[:HALO, :] = _glu(ah_ref[...], bh_ref[...]) * jnp.where(i > 0, 1.0, 0.0)
        d_scr[:tile, :] = d_ref[...]
        d_scr[tile:, :] = dn_ref[...] * jnp.where(i < n_tiles - 1, 1.0, 0.0)

        @pl.when(i == 0)
        def _():
            dw_ref[...] = jnp.zeros_like(dw_ref)

        for c in range(n_chunks):
            cs = slice(c * CONV_CHUNK, (c + 1) * CONV_CHUNK)
            acc = jnp.zeros((CONV_CHUNK, LANES), F32)
            for j in range(CONV_K):
                lo = c * CONV_CHUNK + (CONV_K - 1) - j
                acc = acc + w_ref[j:j + 1, :] * d_scr[lo:lo + CONV_CHUNK, :]
            sgc = _sig(b_ref[cs, :].astype(F32))
            da_ref[cs, :] = (acc * sgc).astype(BF)
            db_ref[cs, :] = (acc * a_ref[cs, :].astype(F32) * sgc * (1.0 - sgc)).astype(BF)
        for j in range(CONV_K):
            tot = jnp.zeros((1, LANES), F32)
            for c in range(n_chunks):
                lo = c * CONV_CHUNK + HALO - (CONV_K - 1) + j
                tot = tot + jnp.sum(d_ref[c * CONV_CHUNK:(c + 1) * CONV_CHUNK, :] * g_scr[lo:lo + CONV_CHUNK, :],
                                    axis=0, keepdims=True)
            dw_ref[j:j + 1, :] += tot
        dw_ref[CONV_K:CONV_KP, :] += jnp.sum(d_ref[...], axis=0, keepdims=True)

    def cur(cb):
        return pl.BlockSpec((tile, LANES), lambda j, i: (i, cb + j))

    def prev(cb):
        return pl.BlockSpec((HALO, LANES), lambda j, i: (jnp.maximum(i * hb - 1, 0), cb + j))

    return _pcall(
        "conv_bwd", kern, (CONV_CH // LANES, n_tiles),
        [cur(a_cb), cur(b_cb), prev(a_cb), prev(b_cb), cur(0),
         pl.BlockSpec((HALO, LANES), lambda j, i: (jnp.minimum((i + 1) * hb, t // HALO - 1), j)),
         pl.BlockSpec((CONV_KP, LANES), lambda j, i: (0, j))],
        [pl.BlockSpec((tile, LANES), lambda j, i: (i, j)), pl.BlockSpec((tile, LANES), lambda j, i: (i, j)),
         pl.BlockSpec((CONV_KP, LANES), lambda j, i: (0, j))],
        [jax.ShapeDtypeStruct((t, CONV_CH), BF), jax.ShapeDtypeStruct((t, CONV_CH), BF),
         jax.ShapeDtypeStruct((CONV_KP, CONV_CH), F32)],
        [pltpu.VMEM((tile + HALO, LANES), F32), pltpu.VMEM((tile + HALO, LANES), F32)],
        ("parallel", "arbitrary"), (z, z, z, z, dc1, dc1, w32), side)


def _adam_fn(w, g, m, v):
    m = ADAM_B1 * m + (1.0 - ADAM_B1) * g
    v = ADAM_B2 * v + (1.0 - ADAM_B2) * (g * g)
    m_hat = m / (1.0 - ADAM_B1 ** ADAM_STEP)
    v_hat = v / (1.0 - ADAM_B2 ** ADAM_STEP)
    delta = -ADAM_LR * (m_hat / (jnp.sqrt(v_hat) + ADAM_EPS) + ADAM_WD * w)
    return delta, m, v


def _adam(name, w, g, m, v):
    rows, cols = w.shape
    tile = _ew_tile(rows, cols)
    return _rowcall(name, _adam_fn, rows, tile, [(a, cols, 0) for a in (w, g, m, v)], [], [(cols, F32)] * 3)


def _place():
    x, y, c = lax.axis_index("x"), lax.axis_index("y"), lax.axis_index("c")
    chips = [(1 - x, y), (x, 1 - y), (1 - x, 1 - y)]
    return x, y, c, chips


def _gather_side(shards):
    nw = len(shards)
    chip = 2 * lax.axis_index("x") + lax.axis_index("y")
    staged = [lax.dynamic_update_index_in_dim(jnp.zeros((4,) + s.shape, s.dtype), s, chip, 0) for s in shards]

    def build(_, outs, send_sems, recv_sems):
        x, y, c, chips = _place()
        me = 2 * x + y
        sibling = (x, y, 1 - c)

        def half(w, lead, h):
            n = shards[w].shape[0] // 2
            return outs[w].at[lead, pl.ds(h * n, n)]

        def copy(w, k, part, to):
            return pltpu.make_async_remote_copy(src_ref=part, dst_ref=part, send_sem=send_sems.at[6 * w + k],
                                                recv_sem=recv_sems.at[6 * w + k], device_id=to, device_id_type=MESH)

        def start():
            for w in range(nw):
                for k, (px, py) in enumerate(chips):
                    copy(w, k, half(w, me, c), (px, py, c)).start()

        def finish():
            for w in range(nw):
                for k, (px, py) in enumerate(chips):
                    landed = half(w, 2 * px + py, c)
                    copy(w, k, landed, (px, py, c)).wait_recv()
                    copy(w, 3 + k, landed, sibling).start()
            for w in range(nw):
                for k, (px, py) in enumerate(chips):
                    copy(w, 3 + k, half(w, 2 * px + py, 1 - c), sibling).wait_recv()
            for w in range(nw):
                for k, (px, py) in enumerate(chips):
                    copy(w, k, half(w, me, c), (px, py, c)).wait_send()
                    copy(w, 3 + k, half(w, 2 * px + py, c), sibling).wait_send()

        return start, finish

    return _Side(staged, [jax.ShapeDtypeStruct((4,) + s.shape, s.dtype) for s in shards], 6 * nw, build,
                 aliases={w: w for w in range(nw)})


def _gather8(name, v, side=None):
    rows = v.shape[0]

    def body(v_ref, all_ref, sum_ref, send_sems, recv_sems):
        x, y, c, _ = _place()
        me = 4 * x + 2 * y + c
        all_ref[me] = v_ref[...]
        copies = []
        for k in range(1, 8):
            px, py, pc = x ^ (k >> 2), y ^ ((k >> 1) & 1), c ^ (k & 1)
            copies.append(pltpu.make_async_remote_copy(
                src_ref=v_ref, dst_ref=all_ref.at[me], send_sem=send_sems.at[k - 1], recv_sem=recv_sems.at[k - 1],
                device_id=(px, py, pc), device_id_type=MESH))
            copies[-1].start()
        for k in range(1, 8):
            px, py, pc = x ^ (k >> 2), y ^ ((k >> 1) & 1), c ^ (k & 1)
            theirs = all_ref.at[4 * px + 2 * py + pc]
            pltpu.make_async_remote_copy(
                src_ref=theirs, dst_ref=theirs, send_sem=send_sems.at[k - 1], recv_sem=recv_sems.at[k - 1],
                device_id=(px, py, pc), device_id_type=MESH).wait_recv()
        for cp in copies:
            cp.wait_send()
        tot = all_ref[0]
        for d in range(1, 8):
            tot = tot + all_ref[d]
        sum_ref[...] = tot

    vm = pl.BlockSpec(memory_space=pltpu.VMEM)
    return _pcall(name, body, (), [vm], [vm, vm],
                  [jax.ShapeDtypeStruct((8, rows, LANES), F32), jax.ShapeDtypeStruct((rows, LANES), F32)],
                  [pltpu.SemaphoreType.DMA((7,)), pltpu.SemaphoreType.DMA((7,))], None, (v,), side)


def _region(ref, col_sharded, shape, j, h):
    r, ccols = shape
    if col_sharded:
        return ref.at[pl.ds(h * (r // 2), r // 2), pl.ds(j * (ccols // 4), ccols // 4)]
    n = r // 8
    return ref.at[pl.ds((2 * j + h) * n, n), :]


def _region_shape(col_sharded, shape):
    r, ccols = shape
    return (r // 2, ccols // 4) if col_sharded else (r // 8, ccols)


def _exchange(copies):
    def build(ins, outs, send_sems, recv_sems):
        def start():
            for cp in copies(ins, outs, send_sems, recv_sems):
                cp.start()

        def finish():
            for cp in copies(ins, outs, send_sems, recv_sems):
                cp.wait()

        return start, finish
    return build


def _swap_side(grads, kinds):
    nw = len(grads)

    def copies(ins, theirs, send_sems, recv_sems):
        x, y, c, _ = _place()
        return [pltpu.make_async_remote_copy(
            src_ref=_region(ins[w], kinds[w], grads[w].shape, j, 1 - c), dst_ref=theirs[w].at[j],
            send_sem=send_sems.at[4 * w + j], recv_sem=recv_sems.at[4 * w + j], device_id=(x, y, 1 - c), device_id_type=MESH)
            for w in range(nw) for j in range(4)]

    shapes = [jax.ShapeDtypeStruct((4,) + _region_shape(kinds[w], grads[w].shape), F32) for w in range(nw)]
    return _Side(grads, shapes, 4 * nw, _exchange(copies))


def _kept_halves(grad, col_sharded, c):
    r, ccols = grad.shape
    if col_sharded:
        slab = lax.dynamic_slice_in_dim(grad, c * (r // 2), r // 2, axis=0)
        return slab.reshape(r // 2, 4, ccols // 4).transpose(1, 0, 2)
    return lax.dynamic_index_in_dim(grad.reshape(4, 2, r // 8, ccols), c, axis=1, keepdims=False)


def _scatter_side(parts):
    nw = len(parts)

    def copies(ins, outs, send_sems, recv_sems):
        x, y, c, chips = _place()
        return [pltpu.make_async_remote_copy(
            src_ref=ins[w].at[2 * px + py], dst_ref=outs[w].at[k], send_sem=send_sems.at[3 * w + k],
            recv_sem=recv_sems.at[3 * w + k], device_id=(px, py, c), device_id_type=MESH)
            for w in range(nw) for k, (px, py) in enumerate(chips)]

    shapes = [jax.ShapeDtypeStruct((3,) + p.shape[1:], p.dtype) for p in parts]
    return _Side(parts, shapes, 3 * nw, _exchange(copies))


def _share_side(halves):
    nw = len(halves)

    def copies(ins, outs, send_sems, recv_sems):
        x, y, c, _ = _place()
        return [pltpu.make_async_remote_copy(
            src_ref=ins[w], dst_ref=outs[w], send_sem=send_sems.at[w], recv_sem=recv_sems.at[w],
            device_id=(x, y, 1 - c), device_id_type=MESH) for w in range(nw)]

    return _Side(halves, [jax.ShapeDtypeStruct(h.shape, F32) for h in halves], nw, _exchange(copies))


def _ew_tile(rows, cols):
    limit = max(8, (256 * 1024) // cols)
    return max(d for d in range(8, min(rows, limit) + 1, 8) if rows % d == 0)


def _pair_sums(names, grads, theirs):
    c = lax.axis_index("c")
    parts, parts_bf = {}, {}
    for n, other in zip(names, theirs):
        _, rr, cc = other.shape
        a = _kept_halves(grads[n], COL_SHARDED[n], c).reshape(4 * rr, cc)
        p, pb = _rowcall(f"grad_pair_sum_{n}", lambda u, v: (u + v, u + v), 4 * rr, _ew_tile(4 * rr, cc),
                         [(a, cc, 0), (other.reshape(4 * rr, cc), cc, 0)], [], [(cc, F32), (cc, BF)])
        parts[n], parts_bf[n] = p.reshape(4, rr, cc), pb.reshape(4, rr, cc)
    return parts, parts_bf


def _chip_sums(names, parts, landed):
    chip = 2 * lax.axis_index("x") + lax.axis_index("y")
    halves = {}
    for n, got in zip(names, landed):
        _, rr, cc = got.shape
        tile = _ew_tile(rr, cc)
        own = lax.dynamic_index_in_dim(parts[n], chip, axis=0, keepdims=False)
        flat = got.reshape(3 * rr, cc)
        nb = rr // tile
        (halves[n],) = _rowcall(f"grad_chip_sum_{n}", lambda own, k0, k1, k2: (((own + k0) + k1) + k2,), rr, tile,
                                [(own, cc, 0), (flat, cc, 0, 0), (flat, cc, 0, nb), (flat, cc, 0, 2 * nb)], [], [(cc, F32)])
    return halves


def _both_halves(mine, other):
    c = lax.axis_index("c")
    return jnp.where(c == 0, jnp.concatenate([mine, other], axis=0), jnp.concatenate([other, mine], axis=0))


BIG = ("w_in", "w_attn_proj", "w_conv_proj", "w_out", "w_cq", "w_ckv", "w_co", "w_up", "w_down")
COL_SHARDED = {"w_in": True, "w_attn_proj": True, "w_conv_proj": True, "w_out": False, "w_cq": False,
               "w_ckv": True, "w_co": False, "w_up": True, "w_down": False}
SMALL = ("g_mix", "b_gate", "conv_b", "conv_ln_g", "conv_ln_b", "g_cross", "g_mem", "g_mlp", "g_final")
ORDER = ("g_mix", "w_in", "b_gate", "conv_w", "conv_b", "conv_ln_g", "conv_ln_b", "w_attn_proj", "w_conv_proj", "w_out",
         "g_cross", "g_mem", "w_cq", "w_ckv", "w_co", "g_mlp", "w_up", "w_down", "g_final")


def _pad_rows(flat, rows):
    return jnp.pad(flat, (0, rows * LANES - flat.shape[0])).reshape(rows, LANES)


REST = tuple(n for n in BIG if n != "w_in")
WAVE_MLP = ("w_down", "w_up")
WAVE_MID = ("w_co", "w_cq", "w_ckv", "w_out", "w_attn_proj", "w_conv_proj")


def _local_step(x, mem, tgt, shards, small, conv_w_full):
    t = x.shape[0]
    tr = 256
    c_tab, s_tab = _rope_tables(t)
    row = lambda v: v.reshape(1, -1)
    g_mix, g_cross, g_mem, g_mlp, g_final = (row(small[n]) for n in ("g_mix", "g_cross", "g_mem", "g_mlp", "g_final"))
    b_gate, conv_b, ln_g, ln_b = (row(small[n]) for n in ("b_gate", "conv_b", "conv_ln_g", "conv_ln_b"))
    w32 = jnp.pad(conv_w_full, ((0, CONV_KP - CONV_K), (0, 0)))

    (u,), (w_in_all,) = _rowcall("mix_norm", lambda a, g: (_rms(a, g),), t, tr, [(x, D_MODEL, 0)], [g_mix], [(D_MODEL, BF)],
                                 side=_gather_side([shards["w_in"]]))
    z, gathered = _mm("in_proj", u, w_in_all, out_dtypes=(BF,), side=_gather_side([shards[n] for n in REST]))
    wfull = {"w_in": w_in_all}
    for n, g in zip(REST, gathered):
        wfull[n] = g if COL_SHARDED[n] else g.reshape(1, 4 * g.shape[1], g.shape[2])
    qkv = _rope_split(z, c_tab, s_tab, t)
    qkv_cls, outs, lses = [], [], []
    for g, dil in enumerate(DILATIONS):
        ops = tuple((a.reshape(t, GROUP_W), 0) for a in qkv[3 * g:3 * g + 3])
        qkv_cls.append(ops)
        o_g, l_g = _attn_fwd(f"attn_fwd_{g}", *ops, t, dil)
        outs.append(o_g)
        lses.append(l_g)
    attn, lse = _merge_classes(outs, lses, t)
    y_attn = _mm("attn_proj", attn, wfull["w_attn_proj"])
    c1 = _conv_fwd(z, w32, conv_b, t)
    (c2,) = _rowcall("conv_ln_silu", _ln_silu_fn, t, tr, [(c1, CONV_CH, 0)], [ln_g, ln_b], [(CONV_CH, BF)])
    y_conv = _mm("conv_proj", c2, wfull["w_conv_proj"])
    gate_ins = [(z, D_MODEL, GATE_A_COL // D_MODEL), (z, D_MODEL, GATE_B_COL // D_MODEL)]
    (merged,) = _rowcall("gate", _gate_fn, t, tr, [(y_attn, D_MODEL, 0), (y_conv, D_MODEL, 0)] + gate_ins, [b_gate],
                         [(D_MODEL, BF)])
    x1, uq = _mm("out_proj", merged, wfull["w_out"], extras=(x,), params=(g_cross,), epi=_res_norm_epi, out_dtypes=(F32, BF))
    (mn,) = _rowcall("mem_norm", lambda a, g: (_rms(a, g),), N_MEM, N_MEM, [(mem, D_MODEL, 0)], [g_mem], [(D_MODEL, BF)])
    cq = _mm("cross_q", uq, wfull["w_cq"], out_dtypes=(BF,))
    ckv = _mm("cross_kv", mn, wfull["w_ckv"], out_dtypes=(BF,))
    co = _cross_fwd(cq, ckv, t)
    x2, um = _mm("cross_out", co, wfull["w_co"], extras=(x1,), params=(g_mlp,), epi=_res_norm_epi, out_dtypes=(F32, BF))
    hact = _mm("mlp_up", um, wfull["w_up"], out_dtypes=(BF,), epi=lambda acc: (jnp.square(jnp.maximum(acc, 0.0)),))
    d3, d3b, loss_row, dg_final = _mm("mlp_down", hact, wfull["w_down"], extras=(x2, tgt), params=(g_final,), epi=_final_epi,
                                      out_dtypes=(F32, BF), sums=[(1, LANES), (1, D_MODEL)])

    gw = {}
    dhp = _mm("mlp_down_bwd", d3b, wfull["w_down"], nt=True, extras=(hact,), out_dtypes=(BF,),
              epi=lambda acc, h: (acc * 2.0 * jnp.sqrt(h.astype(F32)),))
    gw["w_down"] = _mm_tn("mlp_down_wgrad", hact, d3b)
    gw["w_up"] = _mm_tn("mlp_up_wgrad", um, dhp)

    def swap_of(names):
        return _swap_side([gw[n] for n in names], [COL_SHARDED[n] for n in names])

    (d2, d2b, dg_mlp), theirs = _mm("mlp_up_bwd", dhp, wfull["w_up"], nt=True, extras=(x2, d3), params=(g_mlp,),
                                    epi=_norm_bwd_epi, out_dtypes=(F32, BF), sums=[(1, D_MODEL)], side=swap_of(WAVE_MLP))
    parts_mlp, parts_bf_mlp = _pair_sums(WAVE_MLP, gw, theirs)

    dco = _mm("cross_out_bwd", d2b, wfull["w_co"], nt=True, out_dtypes=(BF,))
    gw["w_co"] = _mm_tn("cross_out_wgrad", co, d2b)
    dcq, dckv = _cross_bwd(cq, ckv, dco, t)
    gw["w_cq"] = _mm_tn("cross_q_wgrad", uq, dcq)
    d1, d1b, dg_cross = _mm("cross_q_bwd", dcq, wfull["w_cq"], nt=True, extras=(x1, d2), params=(g_cross,), epi=_norm_bwd_epi,
                            out_dtypes=(F32, BF), sums=[(1, D_MODEL)])
    gw["w_ckv"] = _mm_tn("cross_kv_wgrad", mn, dckv, tk=N_MEM)
    dmn = _mm("cross_kv_bwd", dckv, wfull["w_ckv"], nt=True)
    (dg_mem,) = _rowcall("mem_norm_bwd", lambda a, dn, g: (_rms_bwd(a, g, dn)[1],), N_MEM, N_MEM,
                         [(mem, D_MODEL, 0), (dmn, D_MODEL, 0)], [g_mem], [], accs=[(1, D_MODEL)])

    dmerged = _mm("out_proj_bwd", d1b, wfull["w_out"], nt=True)
    gw["w_out"] = _mm_tn("out_proj_wgrad", merged, d1b)
    dya, dyc, dgate, dbg = _rowcall("gate_bwd", _gate_bwd_fn, t, tr,
                                    [(dmerged, D_MODEL, 0), (y_attn, D_MODEL, 0), (y_conv, D_MODEL, 0)] + gate_ins, [b_gate],
                                    [(D_MODEL, BF), (D_MODEL, BF), (2 * D_MODEL, BF)], accs=[(1, 2 * D_MODEL)])
    gw["w_attn_proj"] = _mm_tn("attn_proj_wgrad", attn, dya)
    dattn = _mm("attn_proj_bwd", dya, wfull["w_attn_proj"], nt=True)
    gw["w_conv_proj"] = _mm_tn("conv_proj_wgrad", c2, dyc)
    dc2 = _mm("conv_proj_bwd", dyc, wfull["w_conv_proj"], nt=True)
    (dc1, dlng, dlnb), theirs = _rowcall("conv_ln_silu_bwd", _ln_silu_bwd_fn, t, tr, [(c1, CONV_CH, 0), (dc2, CONV_CH, 0)],
                                         [ln_g, ln_b], [(CONV_CH, F32)], accs=[(1, CONV_CH), (1, CONV_CH)],
                                         side=swap_of(WAVE_MID))
    parts_mid, parts_bf_mid = _pair_sums(WAVE_MID, gw, theirs)
    (dglu_a, dglu_b, dconv), landed = _conv_bwd(z, dc1, w32, t, side=_scatter_side([parts_bf_mlp[n] for n in WAVE_MLP]))
    halves_mlp = _chip_sums(WAVE_MLP, parts_mlp, landed)

    dl0, cls1, cls2 = _attn_bwd_prep(dattn, attn, lse, t)
    dqs, dks, dvs = [], [], []
    for g, (dil, (da_c, dl_c, ls_c)) in enumerate(zip(DILATIONS, ((dattn, dl0, lse), cls1, cls2))):
        res = _attn_bwd(f"attn_bwd_{g}", *qkv_cls[g], da_c, dl_c, ls_c, t, dil,
                        side=_share_side([halves_mlp[n] for n in WAVE_MLP]) if g == 0 else None)
        if g == 0:
            res, others = res
            gshard = {n: _both_halves(halves_mlp[n], o) for n, o in zip(WAVE_MLP, others)}
        dqs.append(res[0])
        dks.append(res[1])
        dvs.append(res[2])
    (dz,), landed = _rope_bwd_join(dqs, dks, dvs, c_tab, s_tab, dglu_a, dglu_b, dgate, t,
                                   side=_scatter_side([parts_bf_mid[n] for n in WAVE_MID]))
    halves_mid = _chip_sums(WAVE_MID, parts_mid, landed)

    gw_in_full, others = _mm_tn("in_proj_wgrad", u, dz, side=_share_side([halves_mid[n] for n in WAVE_MID]))
    gshard.update({n: _both_halves(halves_mid[n], o) for n, o in zip(WAVE_MID, others)})
    gw_in = {"w_in": gw_in_full}
    du_a, theirs = _mm("in_proj_bwd_a", dz, wfull["w_in"], nt=True, k_part=(0, 2), side=_swap_side([gw_in["w_in"]], [True]))
    parts, parts_bf = _pair_sums(("w_in",), gw_in, theirs)
    (gx, dg_mix), landed_in = _mm("in_proj_bwd_b", dz, wfull["w_in"], nt=True, k_part=(1, 2), extras=(du_a, x, d1),
                                  params=(g_mix,), epi=lambda acc, first, a, dres, g: _norm_bwd_epi(acc + first, a, dres, g)[1:],
                                  sums=[(1, D_MODEL)], side=_scatter_side([parts_bf["w_in"]]))
    halves = _chip_sums(("w_in",), parts, landed_in)

    gsmall = {"g_mix": dg_mix, "b_gate": dbg, "conv_b": dconv[CONV_K:CONV_K + 1], "conv_ln_g": dlng, "conv_ln_b": dlnb,
              "g_cross": dg_cross, "g_mem": dg_mem, "g_mlp": dg_mlp, "g_final": dg_final, "conv_w": dconv[:CONV_K]}
    return loss_row, gx, gshard, gsmall, halves["w_in"]


def kernel(x, mem, g_mix, w_in, b_gate, conv_w, conv_b, conv_ln_g, conv_ln_b, w_attn_proj, w_conv_proj, w_out, g_cross, g_mem, w_cq, w_ckv, w_co, g_mlp, w_up, w_down, g_final, loss_target, m_g_mix, m_w_in, m_b_gate, m_conv_w, m_conv_b, m_conv_ln_g, m_conv_ln_b, m_w_attn_proj, m_w_conv_proj, m_w_out, m_g_cross, m_g_mem, m_w_cq, m_w_ckv, m_w_co, m_g_mlp, m_w_up, m_w_down, m_g_final, v_g_mix, v_w_in, v_b_gate, v_conv_w, v_conv_b, v_conv_ln_g, v_conv_ln_b, v_w_attn_proj, v_w_conv_proj, v_w_out, v_g_cross, v_g_mem, v_w_cq, v_w_ckv, v_w_co, v_g_mlp, v_w_up, v_w_down, v_g_final):
    w = dict(g_mix=g_mix, w_in=w_in, b_gate=b_gate, conv_w=conv_w, conv_b=conv_b, conv_ln_g=conv_ln_g, conv_ln_b=conv_ln_b,
             w_attn_proj=w_attn_proj, w_conv_proj=w_conv_proj, w_out=w_out, g_cross=g_cross, g_mem=g_mem, w_cq=w_cq,
             w_ckv=w_ckv, w_co=w_co, g_mlp=g_mlp, w_up=w_up, w_down=w_down, g_final=g_final)
    mo = dict(g_mix=m_g_mix, w_in=m_w_in, b_gate=m_b_gate, conv_w=m_conv_w, conv_b=m_conv_b, conv_ln_g=m_conv_ln_g,
              conv_ln_b=m_conv_ln_b, w_attn_proj=m_w_attn_proj, w_conv_proj=m_w_conv_proj, w_out=m_w_out, g_cross=m_g_cross,
              g_mem=m_g_mem, w_cq=m_w_cq, w_ckv=m_w_ckv, w_co=m_w_co, g_mlp=m_g_mlp, w_up=m_w_up, w_down=m_w_down,
              g_final=m_g_final)
    vo = dict(g_mix=v_g_mix, w_in=v_w_in, b_gate=v_b_gate, conv_w=v_conv_w, conv_b=v_conv_b, conv_ln_g=v_conv_ln_g,
              conv_ln_b=v_conv_ln_b, w_attn_proj=v_w_attn_proj, w_conv_proj=v_w_conv_proj, w_out=v_w_out, g_cross=v_g_cross,
              g_mem=v_g_mem, w_cq=v_w_cq, w_ckv=v_w_ckv, w_co=v_w_co, g_mlp=v_g_mlp, w_up=v_w_up, w_down=v_w_down,
              g_final=v_g_final)
    shapes = {n: w[n].shape for n in ORDER}
    two_d = lambda a: a.reshape(a.shape[-2], a.shape[-1])
    chip = 2 * lax.axis_index("x") + lax.axis_index("y")

    shards = {n: two_d(w[n]).astype(BF) for n in BIG}
    cw_rows = 48
    cw_all, _ = _gather8("gather_conv_w", _pad_rows(conv_w.reshape(-1), cw_rows))
    cw_shard = CONV_K * (CONV_CH // 4)
    conv_w_full = jnp.concatenate(
        [cw_all[2 * j].reshape(-1)[:cw_shard].reshape(CONV_K, CONV_CH // 4) for j in range(4)], axis=1)

    small = {n: w[n] for n in SMALL}
    loss_row, gx, gshard, gsmall, half_w_in = _local_step(two_d(x), two_d(mem), two_d(loss_target), shards, small,
                                                          conv_w_full)
    loss = lax.psum(loss_row[0, 0], ("x", "y", "c"))

    small_names = SMALL + ("conv_w",)
    flat = jnp.concatenate([gsmall[n].reshape(-1) for n in small_names])
    sm_rows = -(-flat.shape[0] // (8 * LANES)) * 8
    (_, sm_sum), others = _gather8("reduce_small_grads", _pad_rows(flat, sm_rows), side=_share_side([half_w_in]))
    gshard["w_in"] = _both_halves(half_w_in, others[0])
    sm_sum = sm_sum.reshape(-1)
    off = 0
    for n in small_names:
        size = gsmall[n].size
        gshard[n] = sm_sum[off:off + size].reshape(gsmall[n].shape)
        off += size
    gshard["conv_w"] = lax.dynamic_slice_in_dim(gshard["conv_w"], chip * (CONV_CH // 4), CONV_CH // 4, axis=1)

    grads, deltas, new_m, new_v = {}, {}, {}, {}
    for n in BIG:
        d, m2, v2 = _adam(f"adamw_{n}", two_d(w[n]), gshard[n], two_d(mo[n]), two_d(vo[n]))
        grads[n], deltas[n], new_m[n], new_v[n] = (a.reshape(shapes[n]) for a in (gshard[n], d, m2, v2))
    pack = lambda src: jnp.concatenate([src[n].reshape(-1) for n in small_names])
    n_small = sum(w[n].size for n in small_names)
    ad_rows = -(-n_small // (8 * LANES)) * 8
    d, m2, v2 = _adam("adamw_small", *[_pad_rows(pack(src), ad_rows) for src in (w, gshard, mo, vo)])
    off = 0
    for n in small_names:
        size = w[n].size
        grads[n] = gshard[n].reshape(shapes[n])
        deltas[n], new_m[n], new_v[n] = (a.reshape(-1)[off:off + size].reshape(shapes[n]) for a in (d, m2, v2))
        off += size

    return (loss, gx.reshape(x.shape), *[grads[n] for n in ORDER], *[deltas[n] for n in ORDER],
            *[new_m[n] for n in ORDER], *[new_v[n] for n in ORDER])
```

```python
import functools

import jax
import jax.numpy as jnp
from jax import lax
from jax.experimental import pallas as pl
from jax.experimental.pallas import tpu as pltpu

F32 = jnp.float32
BF = jnp.bfloat16

D_MODEL = 1024
N_MEM = 256
HEAD_DIM = 128
HEADS_PER_GROUP = 4
DILATIONS = (1, 4, 16)
BLK = 128
GROUP_W = HEADS_PER_GROUP * HEAD_DIM
ATTN_WIDTH = 3 * GROUP_W
ROT_DIM = 32
ROPE_THETA = 500000.0
CONV_CH = 768
CONV_K = 31
CONV_KP = 32
IN_WIDTH = 8192
CROSS_HEADS = 4
CROSS_HEAD_DIM = 256
D_FF = 4096
EPS = 1e-6
ATTN_SCALE = HEAD_DIM ** -0.5
CROSS_SCALE = CROSS_HEAD_DIM ** -0.5
NEG = -1e30

ADAM_LR = 0.001
ADAM_B1 = 0.9
ADAM_B2 = 0.999
ADAM_EPS = 1e-08
ADAM_WD = 0.01
ADAM_STEP = 10

LANES = 128
VMEM_LIMIT = 56 * 1024 * 1024
MESH = pl.DeviceIdType.MESH
ANY = pl.BlockSpec(memory_space=pl.ANY)

GLU_A_COL = 3 * ATTN_WIDTH
GLU_B_COL = GLU_A_COL + CONV_CH
GATE_A_COL = GLU_B_COL + CONV_CH
GATE_B_COL = GATE_A_COL + D_MODEL


def _params(sem=None):
    return pltpu.CompilerParams(dimension_semantics=sem, vmem_limit_bytes=VMEM_LIMIT)


def _dot(a, b):
    return lax.dot_general(a, b, (((1,), (0,)), ((), ())), preferred_element_type=F32)


def _dot_nt(a, b):
    return lax.dot_general(a, b, (((1,), (1,)), ((), ())), preferred_element_type=F32)


def _dot_tn(a, b):
    return lax.dot_general(a, b, (((0,), (0,)), ((), ())), preferred_element_type=F32)


def _sig(x):
    return 1.0 / (1.0 + jnp.exp(-x))


def _glu(a, b):
    return a.astype(F32) * _sig(b.astype(F32))


class _Side:
    def __init__(self, arrays, out_shapes, n_sems, build, aliases=None):
        self.arrays, self.out_shapes, self.n_sems, self.build = list(arrays), list(out_shapes), n_sems, build
        self.aliases = aliases or {}


def _pcall(name, kern, grid, in_specs, out_specs, out_shape, scratch_shapes, sem, args, side=None):
    in_specs, out_specs, out_shape, scratch_shapes = list(in_specs), list(out_specs), list(out_shape), list(scratch_shapes)
    if side is None:
        return pl.pallas_call(kern, name=name, grid=grid, in_specs=in_specs, out_specs=out_specs, out_shape=out_shape,
                              scratch_shapes=scratch_shapes, compiler_params=_params(sem))(*args)
    ni, no, nsc = len(in_specs), len(out_specs), len(scratch_shapes)
    nsi, nso = len(side.arrays), len(side.out_shapes)

    def wrapped(*refs):
        ins, side_ins = refs[:ni], refs[ni:ni + nsi]
        outs, side_outs = refs[ni + nsi:ni + nsi + no], refs[ni + nsi + no:ni + nsi + no + nso]
        scratch = refs[ni + nsi + no + nso:ni + nsi + no + nso + nsc]
        send_sems, recv_sems = refs[-2:]
        start, finish = side.build(side_ins, side_outs, send_sems, recv_sems)
        if grid:
            first = functools.reduce(jnp.logical_and, [pl.program_id(a) == 0 for a in range(len(grid))])
            last = functools.reduce(jnp.logical_and, [pl.program_id(a) == g - 1 for a, g in enumerate(grid)])
            pl.when(first)(start)
            kern(*ins, *outs, *scratch)
            pl.when(last)(finish)
        else:
            start()
            kern(*ins, *outs, *scratch)
            finish()

    res = pl.pallas_call(
        wrapped, name=name, grid=grid, in_specs=in_specs + [ANY] * nsi, out_specs=out_specs + [ANY] * nso,
        out_shape=out_shape + side.out_shapes,
        scratch_shapes=scratch_shapes + [pltpu.SemaphoreType.DMA((side.n_sems,)), pltpu.SemaphoreType.DMA((side.n_sems,))],
        input_output_aliases={ni + k: no + v for k, v in side.aliases.items()},
        compiler_params=_params(("arbitrary",) * len(grid) if grid else None),
    )(*args, *side.arrays)
    return res[:no], res[no:]


def _rowcall(name, fn, n_rows, tile, ins, params, outs, accs=(), side=None):
    tile = min(tile, n_rows)
    ni, npar, no, na = len(ins), len(params), len(outs), len(accs)

    def kern(*refs):
        in_refs = refs[:ni + npar]
        o_refs = refs[ni + npar:ni + npar + no]
        a_refs = refs[ni + npar + no:]
        vals = fn(*[r[...] for r in in_refs])
        for r, v in zip(o_refs, vals[:no]):
            r[...] = v.astype(r.dtype)
        if na:
            @pl.when(pl.program_id(0) == 0)
            def _():
                for r in a_refs:
                    r[...] = jnp.zeros_like(r)
            for r, v in zip(a_refs, vals[no:]):
                r[...] += v

    in_specs = []
    arrays = []
    for spec in ins:
        arr, width, cb = spec[0], spec[1], spec[2]
        rb = spec[3] if len(spec) > 3 else 0
        in_specs.append(pl.BlockSpec((tile, width), functools.partial(lambda i, cb, rb: (i + rb, cb), cb=cb, rb=rb)))
        arrays.append(arr)
    for p in params:
        in_specs.append(pl.BlockSpec(p.shape, lambda i: (0, 0)))
        arrays.append(p)
    out_specs = [pl.BlockSpec((tile, w), lambda i: (i, 0)) for w, _ in outs]
    out_specs += [pl.BlockSpec(s, lambda i: (0, 0)) for s in accs]
    out_shape = [jax.ShapeDtypeStruct((n_rows, w), dt) for w, dt in outs]
    out_shape += [jax.ShapeDtypeStruct(s, F32) for s in accs]
    return _pcall(name, kern, (n_rows // tile,), in_specs, out_specs, out_shape, [],
                  ("arbitrary",) if na else ("parallel",), arrays, side)


def _mm(name, a, w3, *, nt=False, extras=(), params=(), epi=None, out_dtypes=(F32,), sums=(), tm=None, tn=None, tk=None,
        k_part=(0, 1), side=None):
    m, ka = a.shape
    ns, r, cs = w3.shape
    if not nt:
        k_dim, n = r, ns * cs
        tn = tn or min(cs, 1024)
        tk = tk or min(k_dim, 1024)
    else:
        k_dim, n = ns * cs, r
        tn = tn or min(r, 1024)
        tk = tk or min(cs, 1024)
    assert ka == k_dim, (name, a.shape, w3.shape)
    assert not sums or tn == n, name
    nk = k_dim // tk // k_part[1]
    k0 = k_part[0] * nk
    if not nt:
        nbs = cs // tn
        w_spec = pl.BlockSpec((None, tk, tn), lambda i, j, k: (j // nbs, k + k0, j % nbs))
    else:
        kbs = cs // tk
        w_spec = pl.BlockSpec((None, tn, tk), lambda i, j, k: ((k + k0) // kbs, j, (k + k0) % kbs))
    tm = tm or min(m, 1024)
    ne, no, nsum = len(extras) + len(params), len(out_dtypes), len(sums)

    def kern(a_ref, w_ref, *rest):
        e_refs = rest[:ne]
        o_refs = rest[ne:ne + no]
        s_refs = rest[ne + no:ne + no + nsum]

        def part():
            av = a_ref[...].astype(BF)
            return _dot_nt(av, w_ref[...]) if nt else _dot(av, w_ref[...])

        def finish(res):
            vals = epi(res, *[e[...] for e in e_refs]) if epi else (res,)
            for o, v in zip(o_refs, vals[:no]):
                o[...] = v.astype(o.dtype)
            for sr, v in zip(s_refs, vals[no:]):
                sr[...] += v

        if nsum:
            @pl.when(jnp.logical_and(pl.program_id(0) == 0, pl.program_id(2) == 0))
            def _():
                for sr in s_refs:
                    sr[...] = jnp.zeros_like(sr)

        if nk == 1:
            finish(part())
            return
        acc = rest[ne + no + nsum]
        k = pl.program_id(2)

        @pl.when(k == 0)
        def _():
            acc[...] = part()

        @pl.when(jnp.logical_and(k > 0, k < nk - 1))
        def _():
            acc[...] += part()

        @pl.when(k == nk - 1)
        def _():
            finish(acc[...] + part())

    in_specs = [pl.BlockSpec((tm, tk), lambda i, j, k: (i, k + k0)), w_spec]
    split = lambda items: [(it if isinstance(it, tuple) else (it, None)) for it in items]
    extras, params = split(extras), split(params)
    in_specs += [pl.BlockSpec((tm, tn), functools.partial(lambda i, j, k, off: (i, j + off), off=off or 0)) for _, off in extras]
    in_specs += [pl.BlockSpec(p.shape, lambda i, j, k: (0, 0)) if off is None else
                 pl.BlockSpec((p.shape[0], tn), functools.partial(lambda i, j, k, off: (0, j + off), off=off)) for p, off in params]
    extras, params = [e for e, _ in extras], [p for p, _ in params]
    out_specs = [pl.BlockSpec((tm, tn), lambda i, j, k: (i, j)) for _ in out_dtypes]
    out_specs += [pl.BlockSpec(sh, lambda i, j, k: (0, 0)) for sh in sums]
    out_shape = [jax.ShapeDtypeStruct((m, n), dt) for dt in out_dtypes] + [jax.ShapeDtypeStruct(sh, F32) for sh in sums]
    res = _pcall(name, kern, (m // tm, n // tn, nk), in_specs, out_specs, out_shape,
                 [pltpu.VMEM((tm, tn), F32)] if nk > 1 else [],
                 ("arbitrary",) * 3 if nsum else ("parallel", "parallel", "arbitrary"), (a, w3, *extras, *params), side)
    main = res[0] if side is not None else res
    main = main[0] if no + nsum == 1 else main
    return (main, res[1]) if side is not None else main


def _mm_tn(name, a, b, tm=None, tn=None, tk=None, side=None):
    t, ka = a.shape
    _, n = b.shape
    tm = tm or min(ka, 1024)
    tn = tn or min(n, 1024)
    tk = tk or min(t, 1024)

    def kern(a_ref, b_ref, o_ref):
        def part():
            return _dot_tn(a_ref[...].astype(BF), b_ref[...].astype(BF))

        @pl.when(pl.program_id(2) == 0)
        def _():
            o_ref[...] = part()

        @pl.when(pl.program_id(2) > 0)
        def _():
            o_ref[...] += part()

    res = _pcall(name, kern, (ka // tm, n // tn, t // tk),
                 [pl.BlockSpec((tk, tm), lambda i, j, k: (k, i)), pl.BlockSpec((tk, tn), lambda i, j, k: (k, j))],
                 [pl.BlockSpec((tm, tn), lambda i, j, k: (i, j))], [jax.ShapeDtypeStruct((ka, n), F32)], [],
                 ("parallel", "parallel", "arbitrary"), (a, b), side)
    return (res[0][0], res[1]) if side is not None else res[0]


def _rms(x, g):
    return x * lax.rsqrt(jnp.mean(x * x, axis=-1, keepdims=True) + EPS) * g


def _rms_bwd(x, g, dy):
    r = lax.rsqrt(jnp.mean(x * x, axis=-1, keepdims=True) + EPS)
    xh = x * r
    dxh = dy * g
    dx = r * (dxh - xh * jnp.mean(dxh * xh, axis=-1, keepdims=True))
    return dx, jnp.sum(dy * xh, axis=0, keepdims=True)


def _rot(t, c, s):
    lane = lax.broadcasted_iota(jnp.int32, t.shape, 1)
    swapped = jnp.where(lane < ROT_DIM // 2, pltpu.roll(t, HEAD_DIM - ROT_DIM // 2, 1), pltpu.roll(t, ROT_DIM // 2, 1))
    return t * c + swapped * s


def _rope_tables(t):
    half = ROT_DIM // 2
    pos = jnp.arange(t, dtype=F32)
    inv_freq = ROPE_THETA ** (-jnp.arange(0, ROT_DIM, 2, dtype=F32) / ROT_DIM)
    ang = pos[:, None] * inv_freq[None, :]
    cos, sin = jnp.cos(ang), jnp.sin(ang)
    ones = jnp.ones((t, HEAD_DIM - ROT_DIM), F32)
    c_tab = jnp.concatenate([cos, cos, ones], axis=1)
    s_tab = jnp.concatenate([-sin, sin, 0.0 * ones], axis=1)
    return c_tab, s_tab


def _merge_fn(o0, o1, o2, l0, l1, l2):
    m = jnp.maximum(jnp.maximum(l0, l1), l2)
    e0, e1, e2 = jnp.exp(l0 - m), jnp.exp(l1 - m), jnp.exp(l2 - m)
    tot = e0 + e1 + e2
    return (e0 * o0 + e1 * o1 + e2 * o2) / tot, m + jnp.log(tot)


def _ln_parts(c1):
    mu = jnp.mean(c1, axis=-1, keepdims=True)
    xc = c1 - mu
    r = lax.rsqrt(jnp.mean(xc * xc, axis=-1, keepdims=True) + EPS)
    return xc * r, r


def _ln_silu_fn(c1, g, b):
    xh, _ = _ln_parts(c1)
    yl = xh * g + b
    return (yl * _sig(yl),)


def _ln_silu_bwd_fn(c1, dout, g, b):
    xh, r = _ln_parts(c1)
    yl = xh * g + b
    s = _sig(yl)
    dyl = dout * (s + yl * s * (1.0 - s))
    dxh = dyl * g
    dx = r * (dxh - jnp.mean(dxh, axis=-1, keepdims=True) - xh * jnp.mean(dxh * xh, axis=-1, keepdims=True))
    return dx, jnp.sum(dyl * xh, axis=0, keepdims=True), jnp.sum(dyl, axis=0, keepdims=True)


def _gate_epi(yc, ya, ga, gb, ba, bb):
    return _sig(ga + ba) * ya + _sig(gb + bb) * yc, yc


def _gate_bwd_epi(dm, ya, yc, ga, gb, bg):
    sa = _sig(ga + bg[:, :D_MODEL])
    sb = _sig(gb + bg[:, D_MODEL:])
    dga = dm * ya * sa * (1.0 - sa)
    dgb = dm * yc * sb * (1.0 - sb)
    return dm * sa, dm * sb, dga, dgb, jnp.sum(dga, axis=0, keepdims=True), jnp.sum(dgb, axis=0, keepdims=True)


def _res_norm_epi(acc, res, g):
    xn = res + acc
    return xn, _rms(xn, g)


def _norm_bwd_epi(acc, a, dres, g):
    dx, dg = _rms_bwd(a, g, acc)
    return dres + dx, dres + dx, dg


def _final_epi(acc, res, tgt, g):
    return _final_fn(res + acc, tgt, g)


def _final_fn(x3, tgt, g):
    err = _rms(x3, g) - tgt
    lrow = jnp.sum(err * err, axis=-1, keepdims=True) * (0.5 / D_MODEL)
    lsum = jnp.sum(lrow, axis=0, keepdims=True)
    dx, dg = _rms_bwd(x3, g, err * (1.0 / D_MODEL))
    return dx, dx, jnp.broadcast_to(lsum, (1, LANES)), dg


def _attn_geometry(t, dil):
    cls = t // dil
    rows = min(4 * BLK, cls)
    return rows, rows // BLK, cls // rows


def _head_lanes(h):
    return slice(h * HEAD_DIM, (h + 1) * HEAD_DIM)


def _band_mask():
    row = lax.broadcasted_iota(jnp.int32, (BLK, 2 * BLK), 0)
    col = lax.broadcasted_iota(jnp.int32, (BLK, 2 * BLK), 1)
    return jnp.logical_and(col >= row, col <= row + BLK), col


def _stage_window(scr, halo_ref, cur_ref):
    scr[:BLK, :] = halo_ref[...]
    scr[BLK:, :] = cur_ref[...]


def _attn_fwd(name, q, k, v, t, dil):
    rows, nbk, spc = _attn_geometry(t, dil)

    def kern(q_ref, k_ref, kh_ref, v_ref, vh_ref, o_ref, l_ref, k_scr, v_scr):
        i = pl.program_id(0)
        first_shift = jnp.where(i % spc == 0, BLK, 0)
        _stage_window(k_scr, kh_ref, k_ref)
        _stage_window(v_scr, vh_ref, v_ref)
        band, col = _band_mask()
        band_first = jnp.logical_and(band, col >= first_shift)
        for h in range(HEADS_PER_GROUP):
            hs = _head_lanes(h)
            for b in range(nbk):
                rs, win = slice(b * BLK, (b + 1) * BLK), slice(b * BLK, (b + 2) * BLK)
                s = jnp.where(band_first if b == 0 else band, _dot_nt(q_ref[rs, hs], k_scr[win, hs]) * ATTN_SCALE, NEG)
                m = jnp.max(s, axis=1, keepdims=True)
                p = jnp.exp(s - m)
                tot = jnp.sum(p, axis=1, keepdims=True)
                o_ref[rs, hs] = _dot(p.astype(BF), v_scr[win, hs]) / tot
                l_ref[rs, hs] = jnp.broadcast_to(m + jnp.log(tot), (BLK, HEAD_DIM))

    def cur(cb):
        return pl.BlockSpec((rows, GROUP_W), lambda i: (i, cb))

    def halo(cb):
        return pl.BlockSpec((BLK, GROUP_W), lambda i: (jnp.maximum(i * nbk - 1, 0), cb))

    (qa, qc), (ka, kc_), (va, vc_) = q, k, v
    return _pcall(name, kern, (t // rows,), [cur(qc), cur(kc_), halo(kc_), cur(vc_), halo(vc_)],
                  [pl.BlockSpec((rows, GROUP_W), lambda i: (i, 0))] * 2, [jax.ShapeDtypeStruct((t, GROUP_W), F32)] * 2,
                  [pltpu.VMEM((rows + BLK, GROUP_W), BF)] * 2, ("parallel",), (qa, ka, ka, va, va))


def _attn_bwd(name, q, k, v, da, dl, lse, t, dil, side=None):
    rows, nbk, spc = _attn_geometry(t, dil)
    nblk = t // BLK

    def kern(q_ref, qn_ref, k_ref, kh_ref, v_ref, vh_ref, da_ref, dan_ref, dl_ref, dln_ref, ls_ref, lsn_ref,
             dq_ref, dk_ref, dv_ref, k_scr, v_scr):
        i = pl.program_id(0)
        first_shift = jnp.where(i % spc == 0, BLK, 0)
        next_shift = jnp.where((i + 1) % spc == 0, BLK, 0)
        _stage_window(k_scr, kh_ref, k_ref)
        _stage_window(v_scr, vh_ref, v_ref)
        band, col = _band_mask()
        band_first = jnp.logical_and(band, col >= first_shift)
        row1 = lax.broadcasted_iota(jnp.int32, (BLK, BLK), 0)
        col1 = lax.broadcasted_iota(jnp.int32, (BLK, BLK), 1)
        for h in range(HEADS_PER_GROUP):
            hs = _head_lanes(h)
            for b in range(nbk):
                rs, win = slice(b * BLK, (b + 1) * BLK), slice(b * BLK, (b + 2) * BLK)
                qb, dab = q_ref[rs, hs], da_ref[rs, hs].astype(BF)
                kw, vw = k_scr[win, hs], v_scr[win, hs]
                p = jnp.where(band_first if b == 0 else band,
                              jnp.exp(_dot_nt(qb, kw) * ATTN_SCALE - ls_ref[rs, hs][:, :1]), 0.0)
                ds = (p * (_dot_nt(dab, vw) - dl_ref[rs, hs][:, :1]) * ATTN_SCALE).astype(BF)
                dq_ref[rs, hs] = _dot(ds, kw)
                dkw, dvw = _dot_tn(ds, qb), _dot_tn(p.astype(BF), dab)
                if b >= 1:
                    ps = slice((b - 1) * BLK, b * BLK)
                    dk_ref[ps, hs] = pend_k + dkw[:BLK]
                    dv_ref[ps, hs] = pend_v + dvw[:BLK]
                pend_k, pend_v = dkw[BLK:], dvw[BLK:]
            ls_rows = slice((nbk - 1) * BLK, nbk * BLK)
            last = slice(nbk * BLK, (nbk + 1) * BLK)
            qb, dab = qn_ref[:, hs], dan_ref[:, hs].astype(BF)
            kp, vp = k_scr[last, hs], v_scr[last, hs]
            p = jnp.where(col1 >= row1 + next_shift, jnp.exp(_dot_nt(qb, kp) * ATTN_SCALE - lsn_ref[:, hs]), 0.0)
            ds = (p * (_dot_nt(dab, vp) - dln_ref[:, hs]) * ATTN_SCALE).astype(BF)
            dk_ref[ls_rows, hs] = pend_k + _dot_tn(ds, qb)
            dv_ref[ls_rows, hs] = pend_v + _dot_tn(p.astype(BF), dab)

    def cur(cb):
        return pl.BlockSpec((rows, GROUP_W), lambda i: (i, cb))

    def prev(cb):
        return pl.BlockSpec((BLK, GROUP_W), lambda i: (jnp.maximum(i * nbk - 1, 0), cb))

    def nxt(cb):
        return pl.BlockSpec((BLK, GROUP_W), lambda i: (jnp.minimum((i + 1) * nbk, nblk - 1), cb))

    (qa, qc), (ka, kc_), (va, vc_) = q, k, v
    return _pcall(name, kern, (t // rows,),
                  [cur(qc), nxt(qc), cur(kc_), prev(kc_), cur(vc_), prev(vc_), cur(0), nxt(0), cur(0), nxt(0), cur(0), nxt(0)],
                  [pl.BlockSpec((rows, GROUP_W), lambda i: (i, 0))] * 3, [jax.ShapeDtypeStruct((t, GROUP_W), F32)] * 3,
                  [pltpu.VMEM((rows + BLK, GROUP_W), BF)] * 2, ("parallel",),
                  (qa, qa, ka, ka, va, va, da, da, dl, dl, lse, lse), side)


CLS_TILE = 512


def _cls_block(t, tile, dil, dtype):
    if dil == 1:
        return pl.BlockSpec((tile, GROUP_W), lambda i: (i, 0)), jax.ShapeDtypeStruct((t, GROUP_W), dtype)
    return (pl.BlockSpec((dil, tile // dil, GROUP_W), lambda i: (0, i, 0)),
            jax.ShapeDtypeStruct((dil, t // dil, GROUP_W), dtype))


def _head_scratch(tile):
    return pltpu.VMEM((tile, HEAD_DIM), F32)


def _rope_split(z, c_tab, s_tab, t):
    tile = min(CLS_TILE, t)
    n_heads = ATTN_WIDTH // HEAD_DIM

    def kern(zq_ref, zk_ref, zv_ref, c_ref, s_ref, *rest):
        outs, scr = rest[:9], rest[9]
        c, s = c_ref[...], s_ref[...]
        for which, z_ref in enumerate((zq_ref, zk_ref, zv_ref)):
            for h in range(n_heads):
                g, hs = h // HEADS_PER_GROUP, _head_lanes(h % HEADS_PER_GROUP)
                val = z_ref[:, h * HEAD_DIM:(h + 1) * HEAD_DIM].astype(F32)
                if which < 2:
                    val = _rot(val, c, s)
                if g == 0:
                    outs[which][:, hs] = val.astype(BF)
                    continue
                scr[...] = val
                dil = DILATIONS[g]
                for r in range(dil):
                    outs[3 * g + which][r, :, hs] = scr[pl.ds(r, tile // dil, stride=dil), :].astype(BF)

    blocks = [_cls_block(t, tile, DILATIONS[g], BF) for g in range(3) for _ in range(3)]
    zspec = lambda cb: pl.BlockSpec((tile, ATTN_WIDTH), lambda i: (i, cb))
    tab = pl.BlockSpec((tile, HEAD_DIM), lambda i: (i, 0))
    return _pcall("rope", kern, (t // tile,), [zspec(0), zspec(1), zspec(2), tab, tab], [b[0] for b in blocks],
                  [b[1] for b in blocks], [_head_scratch(tile)], ("parallel",), (z, z, z, c_tab, s_tab))


def _merge_classes(outs, lses, t):
    tile = min(CLS_TILE, t)

    def kern(o0, l0, o1, l1, o2, l2, attn_ref, lse_ref, s_o1, s_l1, s_o2, s_l2):
        for h in range(HEADS_PER_GROUP):
            hs = _head_lanes(h)
            for src, dst, dil in ((o1, s_o1, DILATIONS[1]), (l1, s_l1, DILATIONS[1]), (o2, s_o2, DILATIONS[2]),
                                  (l2, s_l2, DILATIONS[2])):
                for r in range(dil):
                    dst[pl.ds(r, tile // dil, stride=dil), :] = src[r, :, hs]
            attn_ref[:, hs], lse_ref[:, hs] = _merge_fn(o0[:, hs], s_o1[...], s_o2[...], l0[:, hs], s_l1[...], s_l2[...])

    blocks = [_cls_block(t, tile, DILATIONS[g], F32) for g in range(3)]
    args = []
    for g in range(3):
        args += [outs[g].reshape(blocks[g][1].shape), lses[g].reshape(blocks[g][1].shape)]
    tok = pl.BlockSpec((tile, GROUP_W), lambda i: (i, 0))
    return _pcall("attn_merge", kern, (t // tile,), [blocks[g][0] for g in range(3) for _ in range(2)], [tok, tok],
                  [jax.ShapeDtypeStruct((t, GROUP_W), F32)] * 2, [_head_scratch(tile)] * 4, ("parallel",), args)


def _attn_bwd_prep(dattn, attn, lse, t):
    tile = min(CLS_TILE, t)

    def kern(da_ref, at_ref, ls_ref, dl0, da1, dl1, ls1, da2, dl2, ls2, s_da, s_dl, s_ls):
        for h in range(HEADS_PER_GROUP):
            hs = _head_lanes(h)
            da = da_ref[:, hs]
            s_da[...] = da
            s_dl[...] = jnp.broadcast_to(jnp.sum(da * at_ref[:, hs], axis=1, keepdims=True), (tile, HEAD_DIM))
            s_ls[...] = ls_ref[:, hs]
            dl0[:, hs] = s_dl[...]
            for oda, odl, ols, dil in ((da1, dl1, ls1, DILATIONS[1]), (da2, dl2, ls2, DILATIONS[2])):
                for r in range(dil):
                    rows = pl.ds(r, tile // dil, stride=dil)
                    oda[r, :, hs] = s_da[rows, :].astype(BF)
                    odl[r, :, hs] = s_dl[rows, :]
                    ols[r, :, hs] = s_ls[rows, :]

    tok = pl.BlockSpec((tile, GROUP_W), lambda i: (i, 0))
    blocks = [(tok, jax.ShapeDtypeStruct((t, GROUP_W), F32))]
    for g in (1, 2):
        blocks += [_cls_block(t, tile, DILATIONS[g], BF), _cls_block(t, tile, DILATIONS[g], F32),
                   _cls_block(t, tile, DILATIONS[g], F32)]
    res = _pcall("attn_bwd_prep", kern, (t // tile,), [tok, tok, tok], [b[0] for b in blocks], [b[1] for b in blocks],
                 [_head_scratch(tile)] * 3, ("parallel",), (dattn, attn, lse))
    flat = [a.reshape(t, GROUP_W) for a in res]
    return flat[0], flat[1:4], flat[4:7]


def _rope_bwd_join(dqs, dks, dvs, c_tab, s_tab, tail, t, side=None):
    tile = min(256, t)
    n_heads = ATTN_WIDTH // HEAD_DIM

    def kern(q0, q1, q2, k0, k1, k2, v0, v1, v2, c_ref, s_ref, ga_ref, gb_ref, gta_ref, gtb_ref, dz_ref, scr):
        c, s = c_ref[...], -s_ref[...]
        for which, srcs in enumerate(((q0, q1, q2), (k0, k1, k2), (v0, v1, v2))):
            for h in range(n_heads):
                g, hs = h // HEADS_PER_GROUP, _head_lanes(h % HEADS_PER_GROUP)
                if g == 0:
                    val = srcs[0][:, hs]
                else:
                    dil = DILATIONS[g]
                    for r in range(dil):
                        scr[pl.ds(r, tile // dil, stride=dil), :] = srcs[g][r, :, hs]
                    val = scr[...]
                if which < 2:
                    val = _rot(val, c, s)
                col = which * ATTN_WIDTH + h * HEAD_DIM
                dz_ref[:, col:col + HEAD_DIM] = val.astype(BF)
        dz_ref[:, GLU_A_COL:GLU_B_COL] = ga_ref[...]
        dz_ref[:, GLU_B_COL:GATE_A_COL] = gb_ref[...]
        dz_ref[:, GATE_A_COL:GATE_B_COL] = gta_ref[...]
        dz_ref[:, GATE_B_COL:] = gtb_ref[...]

    blocks = [_cls_block(t, tile, DILATIONS[g], F32) for g in range(3)]
    args = [a.reshape(blocks[g][1].shape) for grp in (dqs, dks, dvs) for g, a in enumerate(grp)]
    tab = pl.BlockSpec((tile, HEAD_DIM), lambda i: (i, 0))
    row = lambda w: pl.BlockSpec((tile, w), lambda i: (i, 0))
    return _pcall("rope_bwd", kern, (t // tile,), [blocks[g][0] for _ in range(3) for g in range(3)]
                  + [tab, tab, row(CONV_CH), row(CONV_CH), row(D_MODEL), row(D_MODEL)], [row(IN_WIDTH)],
                  [jax.ShapeDtypeStruct((t, IN_WIDTH), BF)], [_head_scratch(tile)], ("parallel",),
                  (*args, c_tab, s_tab, *tail), side)


def _cross_probs(qh, kh):
    s = _dot_nt(qh, kh) * CROSS_SCALE
    e = jnp.exp(s - jnp.max(s, axis=1, keepdims=True))
    return e, jnp.sum(e, axis=1, keepdims=True)


def _cross_fwd(cq, ckv, t):
    rows = min(512, t)

    def kern(q_ref, kv_ref, o_ref):
        for h in range(CROSS_HEADS):
            hs = slice(h * CROSS_HEAD_DIM, (h + 1) * CROSS_HEAD_DIM)
            vs = slice(D_MODEL + h * CROSS_HEAD_DIM, D_MODEL + (h + 1) * CROSS_HEAD_DIM)
            e, tot = _cross_probs(q_ref[:, hs], kv_ref[:, hs])
            o_ref[:, hs] = (_dot(e.astype(BF), kv_ref[:, vs]) / tot).astype(BF)

    return pl.pallas_call(
        kern, name="cross_fwd", grid=(t // rows,),
        in_specs=[pl.BlockSpec((rows, D_MODEL), lambda i: (i, 0)), pl.BlockSpec((N_MEM, 2 * D_MODEL), lambda i: (0, 0))],
        out_specs=pl.BlockSpec((rows, D_MODEL), lambda i: (i, 0)),
        out_shape=jax.ShapeDtypeStruct((t, D_MODEL), BF),
        compiler_params=_params(("parallel",)),
    )(cq, ckv)


def _cross_bwd(cq, ckv, dco, t):
    rows = min(512, t)

    def kern(q_ref, kv_ref, do_ref, dq_ref, dkv_ref):
        @pl.when(pl.program_id(0) == 0)
        def _():
            dkv_ref[...] = jnp.zeros_like(dkv_ref)
        for h in range(CROSS_HEADS):
            hs = slice(h * CROSS_HEAD_DIM, (h + 1) * CROSS_HEAD_DIM)
            vs = slice(D_MODEL + h * CROSS_HEAD_DIM, D_MODEL + (h + 1) * CROSS_HEAD_DIM)
            qh, kh, vh, doh = q_ref[:, hs], kv_ref[:, hs], kv_ref[:, vs], do_ref[:, hs]
            e, tot = _cross_probs(qh, kh)
            p = e / tot
            dp = _dot_nt(doh, vh)
            ds = (p * (dp - jnp.sum(p * dp, axis=1, keepdims=True)) * CROSS_SCALE).astype(BF)
            dq_ref[:, hs] = _dot(ds, kh).astype(BF)
            dkv_ref[:, hs] += _dot_tn(ds, qh)
            dkv_ref[:, vs] += _dot_tn(p.astype(BF), doh)

    return pl.pallas_call(
        kern, name="cross_bwd", grid=(t // rows,),
        in_specs=[pl.BlockSpec((rows, D_MODEL), lambda i: (i, 0)), pl.BlockSpec((N_MEM, 2 * D_MODEL), lambda i: (0, 0)),
                  pl.BlockSpec((rows, D_MODEL), lambda i: (i, 0))],
        out_specs=[pl.BlockSpec((rows, D_MODEL), lambda i: (i, 0)), pl.BlockSpec((N_MEM, 2 * D_MODEL), lambda i: (0, 0))],
        out_shape=[jax.ShapeDtypeStruct((t, D_MODEL), BF), jax.ShapeDtypeStruct((N_MEM, 2 * D_MODEL), F32)],
        compiler_params=_params(("arbitrary",)),
    )(cq, ckv, dco)


CONV_TILE = 512
CONV_CHUNK = 128
HALO = 32


def _conv_fwd(z, w32, bias, t):
    tile = min(CONV_TILE, t)
    a_cb, b_cb = GLU_A_COL // LANES, GLU_B_COL // LANES
    hb = tile // HALO

    def kern(a_ref, b_ref, ah_ref, bh_ref, w_ref, bias_ref, o_ref, g_scr):
        i = pl.program_id(1)
        g_scr[HALO:, :] = _glu(a_ref[...], b_ref[...])
        g_scr[:HALO, :] = _glu(ah_ref[...], bh_ref[...]) * jnp.where(i > 0, 1.0, 0.0)
        for c in range(tile // CONV_CHUNK):
            acc = jnp.broadcast_to(bias_ref[...], (CONV_CHUNK, LANES))
            for j in range(CONV_K):
                lo = c * CONV_CHUNK + HALO - (CONV_K - 1) + j
                acc = acc + w_ref[j:j + 1, :] * g_scr[lo:lo + CONV_CHUNK, :]
            o_ref[c * CONV_CHUNK:(c + 1) * CONV_CHUNK, :] = acc

    def cur(cb):
        return pl.BlockSpec((tile, LANES), lambda j, i: (i, cb + j))

    def prev(cb):
        return pl.BlockSpec((HALO, LANES), lambda j, i: (jnp.maximum(i * hb - 1, 0), cb + j))

    return pl.pallas_call(
        kern, name="conv_fwd", grid=(CONV_CH // LANES, t // tile),
        in_specs=[cur(a_cb), cur(b_cb), prev(a_cb), prev(b_cb),
                  pl.BlockSpec((CONV_KP, LANES), lambda j, i: (0, j)), pl.BlockSpec((1, LANES), lambda j, i: (0, j))],
        out_specs=pl.BlockSpec((tile, LANES), lambda j, i: (i, j)),
        out_shape=jax.ShapeDtypeStruct((t, CONV_CH), F32),
        scratch_shapes=[pltpu.VMEM((tile + HALO, LANES), F32)],
        compiler_params=_params(("parallel", "parallel")),
    )(z, z, z, z, w32, bias)


def _conv_bwd(z, dc1, w32, t, side=None):
    tile = min(CONV_TILE, t)
    a_cb, b_cb = GLU_A_COL // LANES, GLU_B_COL // LANES
    hb = tile // HALO
    n_tiles = t // tile
    n_chunks = tile // CONV_CHUNK

    def kern(a_ref, b_ref, ah_ref, bh_ref, d_ref, dn_ref, w_ref, da_ref, db_ref, dw_ref, g_scr, d_scr):
        i = pl.program_id(1)
        g_scr[HALO:, :] = _glu(a_ref[...], b_ref[...])
        g_scr[:HALO, :] = _glu(ah_ref[...], bh_ref[...]) * jnp.where(i > 0, 1.0, 0.0)
        d_scr[:tile, :] = d_ref[...]
        d_scr[tile:, :] = dn_ref[...] * jnp.where(i < n_tiles - 1, 1.0, 0.0)

        @pl.when(i == 0)
        def _():
            dw_ref[...] = jnp.zeros_like(dw_ref)

        for c in range(n_chunks):
            cs = slice(c * CONV_CHUNK, (c + 1) * CONV_CHUNK)
            acc = jnp.zeros((CONV_CHUNK, LANES), F32)
            for j in range(CONV_K):
                lo = c * CONV_CHUNK + (CONV_K - 1) - j
                acc = acc + w_ref[j:j + 1, :] * d_scr[lo:lo + CONV_CHUNK, :]
            sgc = _sig(b_ref[cs, :].astype(F32))
            da_ref[cs, :] = (acc * sgc).astype(BF)
            db_ref[cs, :] = (acc * a_ref[cs, :].astype(F32) * sgc * (1.0 - sgc)).astype(BF)
        for j in range(CONV_K):
            tot = jnp.zeros((1, LANES), F32)
            for c in range(n_chunks):
                lo = c * CONV_CHUNK + HALO - (CONV_K - 1) + j
                tot = tot + jnp.sum(d_ref[c * CONV_CHUNK:(c + 1) * CONV_CHUNK, :] * g_scr[lo:lo + CONV_CHUNK, :],
                                    axis=0, keepdims=True)
            dw_ref[j:j + 1, :] += tot
        dw_ref[CONV_K:CONV_KP, :] += jnp.sum(d_ref[...], axis=0, keepdims=True)

    def cur(cb):
        return pl.BlockSpec((tile, LANES), lambda j, i: (i, cb + j))

    def prev(cb):
        return pl.BlockSpec((HALO, LANES), lambda j, i: (jnp.maximum(i * hb - 1, 0), cb + j))

    return _pcall(
        "conv_bwd", kern, (CONV_CH // LANES, n_tiles),
        [cur(a_cb), cur(b_cb), prev(a_cb), prev(b_cb), cur(0),
         pl.BlockSpec((HALO, LANES), lambda j, i: (jnp.minimum((i + 1) * hb, t // HALO - 1), j)),
         pl.BlockSpec((CONV_KP, LANES), lambda j, i: (0, j))],
        [pl.BlockSpec((tile, LANES), lambda j, i: (i, j)), pl.BlockSpec((tile, LANES), lambda j, i: (i, j)),
         pl.BlockSpec((CONV_KP, LANES), lambda j, i: (0, j))],
        [jax.ShapeDtypeStruct((t, CONV_CH), BF), jax.ShapeDtypeStruct((t, CONV_CH), BF),
         jax.ShapeDtypeStruct((CONV_KP, CONV_CH), F32)],
        [pltpu.VMEM((tile + HALO, LANES), F32), pltpu.VMEM((tile + HALO, LANES), F32)],
        ("parallel", "arbitrary"), (z, z, z, z, dc1, dc1, w32), side)


def _adam_fn(w, g, m, v):
    m = ADAM_B1 * m + (1.0 - ADAM_B1) * g
    v = ADAM_B2 * v + (1.0 - ADAM_B2) * (g * g)
    m_hat = m / (1.0 - ADAM_B1 ** ADAM_STEP)
    v_hat = v / (1.0 - ADAM_B2 ** ADAM_STEP)
    delta = -ADAM_LR * (m_hat / (jnp.sqrt(v_hat) + ADAM_EPS) + ADAM_WD * w)
    return delta, m, v


def _adam(name, w, g, m, v):
    rows, cols = w.shape
    tile = _ew_tile(rows, cols)
    return _rowcall(name, _adam_fn, rows, tile, [(a, cols, 0) for a in (w, g, m, v)], [], [(cols, F32)] * 3)


def _place():
    x, y, c = lax.axis_index("x"), lax.axis_index("y"), lax.axis_index("c")
    chips = [(1 - x, y), (x, 1 - y), (1 - x, 1 - y)]
    return x, y, c, chips


def _gather_side(shards):
    nw = len(shards)
    chip = 2 * lax.axis_index("x") + lax.axis_index("y")
    staged = [lax.dynamic_update_index_in_dim(jnp.zeros((4,) + s.shape, s.dtype), s, chip, 0) for s in shards]

    def build(_, outs, send_sems, recv_sems):
        x, y, c, chips = _place()
        me = 2 * x + y
        sibling = (x, y, 1 - c)

        def half(w, lead, h):
            n = shards[w].shape[0] // 2
            return outs[w].at[lead, pl.ds(h * n, n)]

        def copy(w, k, part, to):
            return pltpu.make_async_remote_copy(src_ref=part, dst_ref=part, send_sem=send_sems.at[6 * w + k],
                                                recv_sem=recv_sems.at[6 * w + k], device_id=to, device_id_type=MESH)

        def start():
            for w in range(nw):
                for k, (px, py) in enumerate(chips):
                    copy(w, k, half(w, me, c), (px, py, c)).start()

        def finish():
            for w in range(nw):
                for k, (px, py) in enumerate(chips):
                    landed = half(w, 2 * px + py, c)
                    copy(w, k, landed, (px, py, c)).wait_recv()
                    copy(w, 3 + k, landed, sibling).start()
            for w in range(nw):
                for k, (px, py) in enumerate(chips):
                    copy(w, 3 + k, half(w, 2 * px + py, 1 - c), sibling).wait_recv()
            for w in range(nw):
                for k, (px, py) in enumerate(chips):
                    copy(w, k, half(w, me, c), (px, py, c)).wait_send()
                    copy(w, 3 + k, half(w, 2 * px + py, c), sibling).wait_send()

        return start, finish

    return _Side(staged, [jax.ShapeDtypeStruct((4,) + s.shape, s.dtype) for s in shards], 6 * nw, build,
                 aliases={w: w for w in range(nw)})


def _gather8(name, v, side=None):
    rows = v.shape[0]

    def body(v_ref, all_ref, sum_ref, send_sems, recv_sems):
        x, y, c, _ = _place()
        me = 4 * x + 2 * y + c
        all_ref[me] = v_ref[...]
        copies = []
        for k in range(1, 8):
            px, py, pc = x ^ (k >> 2), y ^ ((k >> 1) & 1), c ^ (k & 1)
            copies.append(pltpu.make_async_remote_copy(
                src_ref=v_ref, dst_ref=all_ref.at[me], send_sem=send_sems.at[k - 1], recv_sem=recv_sems.at[k - 1],
                device_id=(px, py, pc), device_id_type=MESH))
            copies[-1].start()
        for k in range(1, 8):
            px, py, pc = x ^ (k >> 2), y ^ ((k >> 1) & 1), c ^ (k & 1)
            theirs = all_ref.at[4 * px + 2 * py + pc]
            pltpu.make_async_remote_copy(
                src_ref=theirs, dst_ref=theirs, send_sem=send_sems.at[k - 1], recv_sem=recv_sems.at[k - 1],
                device_id=(px, py, pc), device_id_type=MESH).wait_recv()
        for cp in copies:
            cp.wait_send()
        tot = all_ref[0]
        for d in range(1, 8):
            tot = tot + all_ref[d]
        sum_ref[...] = tot

    vm = pl.BlockSpec(memory_space=pltpu.VMEM)
    return _pcall(name, body, (), [vm], [vm, vm],
                  [jax.ShapeDtypeStruct((8, rows, LANES), F32), jax.ShapeDtypeStruct((rows, LANES), F32)],
                  [pltpu.SemaphoreType.DMA((7,)), pltpu.SemaphoreType.DMA((7,))], None, (v,), side)


def _region(ref, col_sharded, shape, j, h):
    r, ccols = shape
    if col_sharded:
        return ref.at[pl.ds(h * (r // 2), r // 2), pl.ds(j * (ccols // 4), ccols // 4)]
    n = r // 8
    return ref.at[pl.ds((2 * j + h) * n, n), :]


def _region_shape(col_sharded, shape):
    r, ccols = shape
    return (r // 2, ccols // 4) if col_sharded else (r // 8, ccols)


def _exchange(copies):
    def build(ins, outs, send_sems, recv_sems):
        def start():
            for cp in copies(ins, outs, send_sems, recv_sems):
                cp.start()

        def finish():
            for cp in copies(ins, outs, send_sems, recv_sems):
                cp.wait()

        return start, finish
    return build


def _swap_side(grads, kinds):
    nw = len(grads)

    def copies(ins, theirs, send_sems, recv_sems):
        x, y, c, _ = _place()
        return [pltpu.make_async_remote_copy(
            src_ref=_region(ins[w], kinds[w], grads[w].shape, j, 1 - c), dst_ref=theirs[w].at[j],
            send_sem=send_sems.at[4 * w + j], recv_sem=recv_sems.at[4 * w + j], device_id=(x, y, 1 - c), device_id_type=MESH)
            for w in range(nw) for j in range(4)]

    shapes = [jax.ShapeDtypeStruct((4,) + _region_shape(kinds[w], grads[w].shape), F32) for w in range(nw)]
    return _Side(grads, shapes, 4 * nw, _exchange(copies))


def _kept_halves(grad, col_sharded, c):
    r, ccols = grad.shape
    if col_sharded:
        slab = lax.dynamic_slice_in_dim(grad, c * (r // 2), r // 2, axis=0)
        return slab.reshape(r // 2, 4, ccols // 4).transpose(1, 0, 2)
    return lax.dynamic_index_in_dim(grad.reshape(4, 2, r // 8, ccols), c, axis=1, keepdims=False)


def _scatter_side(parts):
    nw = len(parts)

    def copies(ins, outs, send_sems, recv_sems):
        x, y, c, chips = _place()
        return [pltpu.make_async_remote_copy(
            src_ref=ins[w].at[2 * px + py], dst_ref=outs[w].at[k], send_sem=send_sems.at[3 * w + k],
            recv_sem=recv_sems.at[3 * w + k], device_id=(px, py, c), device_id_type=MESH)
            for w in range(nw) for k, (px, py) in enumerate(chips)]

    shapes = [jax.ShapeDtypeStruct((3,) + p.shape[1:], p.dtype) for p in parts]
    return _Side(parts, shapes, 3 * nw, _exchange(copies))


def _share_side(halves):
    nw = len(halves)

    def copies(ins, outs, send_sems, recv_sems):
        x, y, c, _ = _place()
        return [pltpu.make_async_remote_copy(
            src_ref=ins[w], dst_ref=outs[w].at[c], send_sem=send_sems.at[w], recv_sem=recv_sems.at[w],
            device_id=(x, y, 1 - c), device_id_type=MESH) for w in range(nw)]

    return _Side(halves, [jax.ShapeDtypeStruct((2,) + h.shape, F32) for h in halves], nw, _exchange(copies))


def _ew_tile(rows, cols):
    limit = max(8, (256 * 1024) // cols)
    return max(d for d in range(8, min(rows, limit) + 1, 8) if rows % d == 0)


def _pair_sums(names, grads, theirs):
    c = lax.axis_index("c")
    parts, parts_bf = {}, {}
    for n, other in zip(names, theirs):
        _, rr, cc = other.shape
        a = _kept_halves(grads[n], COL_SHARDED[n], c).reshape(4 * rr, cc)
        p, pb = _rowcall(f"grad_pair_sum_{n}", lambda u, v: (u + v, u + v), 4 * rr, _ew_tile(4 * rr, cc),
                         [(a, cc, 0), (other.reshape(4 * rr, cc), cc, 0)], [], [(cc, F32), (cc, BF)])
        parts[n], parts_bf[n] = p.reshape(4, rr, cc), pb.reshape(4, rr, cc)
    return parts, parts_bf


def _chip_sums(names, parts, landed):
    chip = 2 * lax.axis_index("x") + lax.axis_index("y")
    halves = {}
    for n, got in zip(names, landed):
        _, rr, cc = got.shape
        tile = _ew_tile(rr, cc)
        own = lax.dynamic_index_in_dim(parts[n], chip, axis=0, keepdims=False)
        flat = got.reshape(3 * rr, cc)
        nb = rr // tile
        (halves[n],) = _rowcall(f"grad_chip_sum_{n}", lambda own, k0, k1, k2: (((own + k0) + k1) + k2,), rr, tile,
                                [(own, cc, 0), (flat, cc, 0, 0), (flat, cc, 0, nb), (flat, cc, 0, 2 * nb)], [], [(cc, F32)])
    return halves


def _both_halves(mine, shared):
    both = lax.dynamic_update_index_in_dim(shared, mine, lax.axis_index("c"), 0)
    return both.reshape(2 * mine.shape[0], mine.shape[1])


BIG = ("w_in", "w_attn_proj", "w_conv_proj", "w_out", "w_cq", "w_ckv", "w_co", "w_up", "w_down")
COL_SHARDED = {"w_in": True, "w_attn_proj": True, "w_conv_proj": True, "w_out": False, "w_cq": False,
               "w_ckv": True, "w_co": False, "w_up": True, "w_down": False}
SMALL = ("g_mix", "b_gate", "conv_b", "conv_ln_g", "conv_ln_b", "g_cross", "g_mem", "g_mlp", "g_final")
ORDER = ("g_mix", "w_in", "b_gate", "conv_w", "conv_b", "conv_ln_g", "conv_ln_b", "w_attn_proj", "w_conv_proj", "w_out",
         "g_cross", "g_mem", "w_cq", "w_ckv", "w_co", "g_mlp", "w_up", "w_down", "g_final")


def _pad_rows(flat, rows):
    return jnp.pad(flat, (0, rows * LANES - flat.shape[0])).reshape(rows, LANES)


REST = tuple(n for n in BIG if n != "w_in")
WAVE_MLP = ("w_down", "w_up")
WAVE_MID = ("w_co", "w_cq", "w_ckv", "w_out", "w_attn_proj", "w_conv_proj")


def _local_step(x, mem, tgt, shards, small, conv_w_full):
    t = x.shape[0]
    tr = 256
    c_tab, s_tab = _rope_tables(t)
    row = lambda v: v.reshape(1, -1)
    g_mix, g_cross, g_mem, g_mlp, g_final = (row(small[n]) for n in ("g_mix", "g_cross", "g_mem", "g_mlp", "g_final"))
    b_gate, conv_b, ln_g, ln_b = (row(small[n]) for n in ("b_gate", "conv_b", "conv_ln_g", "conv_ln_b"))
    w32 = jnp.pad(conv_w_full, ((0, CONV_KP - CONV_K), (0, 0)))

    (u,), (w_in_all,) = _rowcall("mix_norm", lambda a, g: (_rms(a, g),), t, tr, [(x, D_MODEL, 0)], [g_mix], [(D_MODEL, BF)],
                                 side=_gather_side([shards["w_in"]]))
    z, gathered = _mm("in_proj", u, w_in_all, out_dtypes=(BF,), side=_gather_side([shards[n] for n in REST]))
    wfull = {"w_in": w_in_all}
    for n, g in zip(REST, gathered):
        wfull[n] = g if COL_SHARDED[n] else g.reshape(1, 4 * g.shape[1], g.shape[2])
    qkv = _rope_split(z, c_tab, s_tab, t)
    qkv_cls, outs, lses = [], [], []
    for g, dil in enumerate(DILATIONS):
        ops = tuple((a.reshape(t, GROUP_W), 0) for a in qkv[3 * g:3 * g + 3])
        qkv_cls.append(ops)
        o_g, l_g = _attn_fwd(f"attn_fwd_{g}", *ops, t, dil)
        outs.append(o_g)
        lses.append(l_g)
    attn, lse = _merge_classes(outs, lses, t)
    y_attn = _mm("attn_proj", attn, wfull["w_attn_proj"])
    c1 = _conv_fwd(z, w32, conv_b, t)
    (c2,) = _rowcall("conv_ln_silu", _ln_silu_fn, t, tr, [(c1, CONV_CH, 0)], [ln_g, ln_b], [(CONV_CH, BF)])
    tn_cp = wfull["w_conv_proj"].shape[2]
    merged, y_conv = _mm("conv_proj", c2, wfull["w_conv_proj"], epi=_gate_epi, out_dtypes=(BF, F32),
                         extras=(y_attn, (z, GATE_A_COL // tn_cp), (z, GATE_B_COL // tn_cp)),
                         params=((b_gate, 0), (b_gate, D_MODEL // tn_cp)))
    x1, uq = _mm("out_proj", merged, wfull["w_out"], extras=(x,), params=(g_cross,), epi=_res_norm_epi, out_dtypes=(F32, BF))
    (mn,) = _rowcall("mem_norm", lambda a, g: (_rms(a, g),), N_MEM, N_MEM, [(mem, D_MODEL, 0)], [g_mem], [(D_MODEL, BF)])
    cq = _mm("cross_q", uq, wfull["w_cq"], out_dtypes=(BF,))
    ckv = _mm("cross_kv", mn, wfull["w_ckv"], out_dtypes=(BF,))
    co = _cross_fwd(cq, ckv, t)
    x2, um = _mm("cross_out", co, wfull["w_co"], extras=(x1,), params=(g_mlp,), epi=_res_norm_epi, out_dtypes=(F32, BF))
    hact = _mm("mlp_up", um, wfull["w_up"], out_dtypes=(BF,), epi=lambda acc: (jnp.square(jnp.maximum(acc, 0.0)),))
    d3, d3b, loss_row, dg_final = _mm("mlp_down", hact, wfull["w_down"], extras=(x2, tgt), params=(g_final,), epi=_final_epi,
                                      out_dtypes=(F32, BF), sums=[(1, LANES), (1, D_MODEL)])

    gw = {}
    dhp = _mm("mlp_down_bwd", d3b, wfull["w_down"], nt=True, extras=(hact,), out_dtypes=(BF,),
              epi=lambda acc, h: (acc * 2.0 * jnp.sqrt(h.astype(F32)),))
    gw["w_down"] = _mm_tn("mlp_down_wgrad", hact, d3b)
    gw["w_up"] = _mm_tn("mlp_up_wgrad", um, dhp)

    def swap_of(names):
        return _swap_side([gw[n] for n in names], [COL_SHARDED[n] for n in names])

    (d2, d2b, dg_mlp), theirs = _mm("mlp_up_bwd", dhp, wfull["w_up"], nt=True, extras=(x2, d3), params=(g_mlp,),
                                    epi=_norm_bwd_epi, out_dtypes=(F32, BF), sums=[(1, D_MODEL)], side=swap_of(WAVE_MLP))
    parts_mlp, parts_bf_mlp = _pair_sums(WAVE_MLP, gw, theirs)

    dco = _mm("cross_out_bwd", d2b, wfull["w_co"], nt=True, out_dtypes=(BF,))
    gw["w_co"] = _mm_tn("cross_out_wgrad", co, d2b)
    dcq, dckv = _cross_bwd(cq, ckv, dco, t)
    gw["w_cq"] = _mm_tn("cross_q_wgrad", uq, dcq)
    d1, d1b, dg_cross = _mm("cross_q_bwd", dcq, wfull["w_cq"], nt=True, extras=(x1, d2), params=(g_cross,), epi=_norm_bwd_epi,
                            out_dtypes=(F32, BF), sums=[(1, D_MODEL)])
    gw["w_ckv"] = _mm_tn("cross_kv_wgrad", mn, dckv, tk=N_MEM)
    dmn = _mm("cross_kv_bwd", dckv, wfull["w_ckv"], nt=True)
    (dg_mem,) = _rowcall("mem_norm_bwd", lambda a, dn, g: (_rms_bwd(a, g, dn)[1],), N_MEM, N_MEM,
                         [(mem, D_MODEL, 0), (dmn, D_MODEL, 0)], [g_mem], [], accs=[(1, D_MODEL)])

    dya, dyc, dgate_a, dgate_b, dbg_a, dbg_b = _mm(
        "out_proj_bwd", d1b, wfull["w_out"], nt=True, tm=min(512, t), epi=_gate_bwd_epi, out_dtypes=(BF,) * 4,
        extras=(y_attn, y_conv, (z, GATE_A_COL // D_MODEL), (z, GATE_B_COL // D_MODEL)), params=(b_gate,),
        sums=[(1, D_MODEL), (1, D_MODEL)])
    dbg = jnp.concatenate([dbg_a, dbg_b], axis=1)
    gw["w_out"] = _mm_tn("out_proj_wgrad", merged, d1b)
    gw["w_attn_proj"] = _mm_tn("attn_proj_wgrad", attn, dya)
    dattn = _mm("attn_proj_bwd", dya, wfull["w_attn_proj"], nt=True)
    gw["w_conv_proj"] = _mm_tn("conv_proj_wgrad", c2, dyc)
    dc2 = _mm("conv_proj_bwd", dyc, wfull["w_conv_proj"], nt=True)
    (dc1, dlng, dlnb), theirs = _rowcall("conv_ln_silu_bwd", _ln_silu_bwd_fn, t, tr, [(c1, CONV_CH, 0), (dc2, CONV_CH, 0)],
                                         [ln_g, ln_b], [(CONV_CH, F32)], accs=[(1, CONV_CH), (1, CONV_CH)],
                                         side=swap_of(WAVE_MID))
    parts_mid, parts_bf_mid = _pair_sums(WAVE_MID, gw, theirs)
    (dglu_a, dglu_b, dconv), landed = _conv_bwd(z, dc1, w32, t, side=_scatter_side([parts_bf_mlp[n] for n in WAVE_MLP]))
    halves_mlp = _chip_sums(WAVE_MLP, parts_mlp, landed)

    dl0, cls1, cls2 = _attn_bwd_prep(dattn, attn, lse, t)
    dqs, dks, dvs = [], [], []
    for g, (dil, (da_c, dl_c, ls_c)) in enumerate(zip(DILATIONS, ((dattn, dl0, lse), cls1, cls2))):
        res = _attn_bwd(f"attn_bwd_{g}", *qkv_cls[g], da_c, dl_c, ls_c, t, dil,
                        side=_share_side([halves_mlp[n] for n in WAVE_MLP]) if g == 0 else None)
        if g == 0:
            res, others = res
            gshard = {n: _both_halves(halves_mlp[n], o) for n, o in zip(WAVE_MLP, others)}
        dqs.append(res[0])
        dks.append(res[1])
        dvs.append(res[2])
    (dz,), landed = _rope_bwd_join(dqs, dks, dvs, c_tab, s_tab, (dglu_a, dglu_b, dgate_a, dgate_b), t,
                                   side=_scatter_side([parts_bf_mid[n] for n in WAVE_MID]))
    halves_mid = _chip_sums(WAVE_MID, parts_mid, landed)

    gw_in_full, others = _mm_tn("in_proj_wgrad", u, dz, side=_share_side([halves_mid[n] for n in WAVE_MID]))
    gshard.update({n: _both_halves(halves_mid[n], o) for n, o in zip(WAVE_MID, others)})
    gw_in = {"w_in": gw_in_full}
    du_a, theirs = _mm("in_proj_bwd_a", dz, wfull["w_in"], nt=True, k_part=(0, 2), side=_swap_side([gw_in["w_in"]], [True]))
    parts, parts_bf = _pair_sums(("w_in",), gw_in, theirs)
    (gx, dg_mix), landed_in = _mm("in_proj_bwd_b", dz, wfull["w_in"], nt=True, k_part=(1, 2), extras=(du_a, x, d1),
                                  params=(g_mix,), epi=lambda acc, first, a, dres, g: _norm_bwd_epi(acc + first, a, dres, g)[1:],
                                  sums=[(1, D_MODEL)], side=_scatter_side([parts_bf["w_in"]]))
    halves = _chip_sums(("w_in",), parts, landed_in)

    gsmall = {"g_mix": dg_mix, "b_gate": dbg, "conv_b": dconv[CONV_K:CONV_K + 1], "conv_ln_g": dlng, "conv_ln_b": dlnb,
              "g_cross": dg_cross, "g_mem": dg_mem, "g_mlp": dg_mlp, "g_final": dg_final, "conv_w": dconv[:CONV_K]}
    return loss_row, gx, gshard, gsmall, halves["w_in"]


def kernel(x, mem, g_mix, w_in, b_gate, conv_w, conv_b, conv_ln_g, conv_ln_b, w_attn_proj, w_conv_proj, w_out, g_cross, g_mem, w_cq, w_ckv, w_co, g_mlp, w_up, w_down, g_final, loss_target, m_g_mix, m_w_in, m_b_gate, m_conv_w, m_conv_b, m_conv_ln_g, m_conv_ln_b, m_w_attn_proj, m_w_conv_proj, m_w_out, m_g_cross, m_g_mem, m_w_cq, m_w_ckv, m_w_co, m_g_mlp, m_w_up, m_w_down, m_g_final, v_g_mix, v_w_in, v_b_gate, v_conv_w, v_conv_b, v_conv_ln_g, v_conv_ln_b, v_w_attn_proj, v_w_conv_proj, v_w_out, v_g_cross, v_g_mem, v_w_cq, v_w_ckv, v_w_co, v_g_mlp, v_w_up, v_w_down, v_g_final):
    w = dict(g_mix=g_mix, w_in=w_in, b_gate=b_gate, conv_w=conv_w, conv_b=conv_b, conv_ln_g=conv_ln_g, conv_ln_b=conv_ln_b,
             w_attn_proj=w_attn_proj, w_conv_proj=w_conv_proj, w_out=w_out, g_cross=g_cross, g_mem=g_mem, w_cq=w_cq,
             w_ckv=w_ckv, w_co=w_co, g_mlp=g_mlp, w_up=w_up, w_down=w_down, g_final=g_final)
    mo = dict(g_mix=m_g_mix, w_in=m_w_in, b_gate=m_b_gate, conv_w=m_conv_w, conv_b=m_conv_b, conv_ln_g=m_conv_ln_g,
              conv_ln_b=m_conv_ln_b, w_attn_proj=m_w_attn_proj, w_conv_proj=m_w_conv_proj, w_out=m_w_out, g_cross=m_g_cross,
              g_mem=m_g_mem, w_cq=m_w_cq, w_ckv=m_w_ckv, w_co=m_w_co, g_mlp=m_g_mlp, w_up=m_w_up, w_down=m_w_down,
              g_final=m_g_final)
    vo = dict(g_mix=v_g_mix, w_in=v_w_in, b_gate=v_b_gate, conv_w=v_conv_w, conv_b=v_conv_b, conv_ln_g=v_conv_ln_g,
              conv_ln_b=v_conv_ln_b, w_attn_proj=v_w_attn_proj, w_conv_proj=v_w_conv_proj, w_out=v_w_out, g_cross=v_g_cross,
              g_mem=v_g_mem, w_cq=v_w_cq, w_ckv=v_w_ckv, w_co=v_w_co, g_mlp=v_g_mlp, w_up=v_w_up, w_down=v_w_down,
              g_final=v_g_final)
    shapes = {n: w[n].shape for n in ORDER}
    two_d = lambda a: a.reshape(a.shape[-2], a.shape[-1])
    chip = 2 * lax.axis_index("x") + lax.axis_index("y")

    shards = {n: two_d(w[n]).astype(BF) for n in BIG}
    cw_rows = 48
    cw_all, _ = _gather8("gather_conv_w", _pad_rows(conv_w.reshape(-1), cw_rows))
    cw_shard = CONV_K * (CONV_CH // 4)
    conv_w_full = jnp.concatenate(
        [cw_all[2 * j].reshape(-1)[:cw_shard].reshape(CONV_K, CONV_CH // 4) for j in range(4)], axis=1)

    small = {n: w[n] for n in SMALL}
    loss_row, gx, gshard, gsmall, half_w_in = _local_step(two_d(x), two_d(mem), two_d(loss_target), shards, small,
                                                          conv_w_full)
    loss = lax.psum(loss_row[0, 0], ("x", "y", "c"))

    small_names = SMALL + ("conv_w",)
    flat = jnp.concatenate([gsmall[n].reshape(-1) for n in small_names])
    sm_rows = -(-flat.shape[0] // (8 * LANES)) * 8
    (_, sm_sum), others = _gather8("reduce_small_grads", _pad_rows(flat, sm_rows), side=_share_side([half_w_in]))
    gshard["w_in"] = _both_halves(half_w_in, others[0])
    sm_sum = sm_sum.reshape(-1)
    off = 0
    for n in small_names:
        size = gsmall[n].size
        gshard[n] = sm_sum[off:off + size].reshape(gsmall[n].shape)
        off += size
    gshard["conv_w"] = lax.dynamic_slice_in_dim(gshard["conv_w"], chip * (CONV_CH // 4), CONV_CH // 4, axis=1)

    grads, deltas, new_m, new_v = {}, {}, {}, {}
    for n in BIG:
        d, m2, v2 = _adam(f"adamw_{n}", two_d(w[n]), gshard[n], two_d(mo[n]), two_d(vo[n]))
        grads[n], deltas[n], new_m[n], new_v[n] = (a.reshape(shapes[n]) for a in (gshard[n], d, m2, v2))
    pack = lambda src: jnp.concatenate([src[n].reshape(-1) for n in small_names])
    n_small = sum(w[n].size for n in small_names)
    ad_rows = -(-n_small // (8 * LANES)) * 8
    d, m2, v2 = _adam("adamw_small", *[_pad_rows(pack(src), ad_rows) for src in (w, gshard, mo, vo)])
    off = 0
    for n in small_names:
        size = w[n].size
        grads[n] = gshard[n].reshape(shapes[n])
        deltas[n], new_m[n], new_v[n] = (a.reshape(-1)[off:off + size].reshape(shapes[n]) for a in (d, m2, v2))
        off += size

    return (loss, gx.reshape(x.shape), *[grads[n] for n in ORDER], *[deltas[n] for n in ORDER],
            *[new_m[n] for n in ORDER], *[new_v[n] for n in ORDER])
```

```python
import functools

import jax
import jax.numpy as jnp
from jax import lax
from jax.experimental import pallas as pl
from jax.experimental.pallas import tpu as pltpu

F32 = jnp.float32
BF = jnp.bfloat16

D_MODEL = 1024
N_MEM = 256
HEAD_DIM = 128
HEADS_PER_GROUP = 4
DILATIONS = (1, 4, 16)
BLK = 128
GROUP_W = HEADS_PER_GROUP * HEAD_DIM
ATTN_WIDTH = 3 * GROUP_W
ROT_DIM = 32
ROPE_THETA = 500000.0
CONV_CH = 768
CONV_K = 31
CONV_KP = 32
IN_WIDTH = 8192
CROSS_HEADS = 4
CROSS_HEAD_DIM = 256
D_FF = 4096
EPS = 1e-6
ATTN_SCALE = HEAD_DIM ** -0.5
CROSS_SCALE = CROSS_HEAD_DIM ** -0.5
NEG = -1e30

ADAM_LR = 0.001
ADAM_B1 = 0.9
ADAM_B2 = 0.999
ADAM_EPS = 1e-08
ADAM_WD = 0.01
ADAM_STEP = 10

LANES = 128
VMEM_LIMIT = 56 * 1024 * 1024
MESH = pl.DeviceIdType.MESH
ANY = pl.BlockSpec(memory_space=pl.ANY)

GLU_A_COL = 3 * ATTN_WIDTH
GLU_B_COL = GLU_A_COL + CONV_CH
GATE_A_COL = GLU_B_COL + CONV_CH
GATE_B_COL = GATE_A_COL + D_MODEL


def _params(sem=None):
    return pltpu.CompilerParams(dimension_semantics=sem, vmem_limit_bytes=VMEM_LIMIT)


def _dot(a, b):
    return lax.dot_general(a, b, (((1,), (0,)), ((), ())), preferred_element_type=F32)


def _dot_nt(a, b):
    return lax.dot_general(a, b, (((1,), (1,)), ((), ())), preferred_element_type=F32)


def _dot_tn(a, b):
    return lax.dot_general(a, b, (((0,), (0,)), ((), ())), preferred_element_type=F32)


def _sig(x):
    return 1.0 / (1.0 + jnp.exp(-x))


def _glu(a, b):
    return a.astype(F32) * _sig(b.astype(F32))


class _Side:
    def __init__(self, arrays, out_shapes, n_sems, build, aliases=None):
        self.arrays, self.out_shapes, self.n_sems, self.build = list(arrays), list(out_shapes), n_sems, build
        self.aliases = aliases or {}


def _pcall(name, kern, grid, in_specs, out_specs, out_shape, scratch_shapes, sem, args, side=None):
    in_specs, out_specs, out_shape, scratch_shapes = list(in_specs), list(out_specs), list(out_shape), list(scratch_shapes)
    if side is None:
        return pl.pallas_call(kern, name=name, grid=grid, in_specs=in_specs, out_specs=out_specs, out_shape=out_shape,
                              scratch_shapes=scratch_shapes, compiler_params=_params(sem))(*args)
    ni, no, nsc = len(in_specs), len(out_specs), len(scratch_shapes)
    nsi, nso = len(side.arrays), len(side.out_shapes)

    def wrapped(*refs):
        ins, side_ins = refs[:ni], refs[ni:ni + nsi]
        outs, side_outs = refs[ni + nsi:ni + nsi + no], refs[ni + nsi + no:ni + nsi + no + nso]
        scratch = refs[ni + nsi + no + nso:ni + nsi + no + nso + nsc]
        send_sems, recv_sems = refs[-2:]
        start, finish = side.build(side_ins, side_outs, send_sems, recv_sems)
        if grid:
            first = functools.reduce(jnp.logical_and, [pl.program_id(a) == 0 for a in range(len(grid))])
            last = functools.reduce(jnp.logical_and, [pl.program_id(a) == g - 1 for a, g in enumerate(grid)])
            pl.when(first)(start)
            kern(*ins, *outs, *scratch)
            pl.when(last)(finish)
        else:
            start()
            kern(*ins, *outs, *scratch)
            finish()

    res = pl.pallas_call(
        wrapped, name=name, grid=grid, in_specs=in_specs + [ANY] * nsi, out_specs=out_specs + [ANY] * nso,
        out_shape=out_shape + side.out_shapes,
        scratch_shapes=scratch_shapes + [pltpu.SemaphoreType.DMA((side.n_sems,)), pltpu.SemaphoreType.DMA((side.n_sems,))],
        input_output_aliases={ni + k: no + v for k, v in side.aliases.items()},
        compiler_params=_params(("arbitrary",) * len(grid) if grid else None),
    )(*args, *side.arrays)
    return res[:no], res[no:]


def _rowcall(name, fn, n_rows, tile, ins, params, outs, accs=(), side=None):
    tile = min(tile, n_rows)
    ni, npar, no, na = len(ins), len(params), len(outs), len(accs)

    def kern(*refs):
        in_refs = refs[:ni + npar]
        o_refs = refs[ni + npar:ni + npar + no]
        a_refs = refs[ni + npar + no:]
        vals = fn(*[r[...] for r in in_refs])
        for r, v in zip(o_refs, vals[:no]):
            r[...] = v.astype(r.dtype)
        if na:
            @pl.when(pl.program_id(0) == 0)
            def _():
                for r in a_refs:
                    r[...] = jnp.zeros_like(r)
            for r, v in zip(a_refs, vals[no:]):
                r[...] += v

    in_specs = []
    arrays = []
    for spec in ins:
        arr, width, cb = spec[0], spec[1], spec[2]
        rb = spec[3] if len(spec) > 3 else 0
        in_specs.append(pl.BlockSpec((tile, width), functools.partial(lambda i, cb, rb: (i + rb, cb), cb=cb, rb=rb)))
        arrays.append(arr)
    for p in params:
        in_specs.append(pl.BlockSpec(p.shape, lambda i: (0, 0)))
        arrays.append(p)
    out_specs = [pl.BlockSpec((tile, w), lambda i: (i, 0)) for w, _ in outs]
    out_specs += [pl.BlockSpec(s, lambda i: (0, 0)) for s in accs]
    out_shape = [jax.ShapeDtypeStruct((n_rows, w), dt) for w, dt in outs]
    out_shape += [jax.ShapeDtypeStruct(s, F32) for s in accs]
    return _pcall(name, kern, (n_rows // tile,), in_specs, out_specs, out_shape, [],
                  ("arbitrary",) if na else ("parallel",), arrays, side)


def _mm(name, a, w3, *, nt=False, extras=(), params=(), epi=None, out_dtypes=(F32,), sums=(), tm=None, tn=None, tk=None,
        k_part=(0, 1), side=None):
    m, ka = a.shape
    ns, r, cs = w3.shape
    if not nt:
        k_dim, n = r, ns * cs
        tn = tn or min(cs, 1024)
        tk = tk or min(k_dim, 1024)
    else:
        k_dim, n = ns * cs, r
        tn = tn or min(r, 1024)
        tk = tk or min(cs, 1024)
    assert ka == k_dim, (name, a.shape, w3.shape)
    assert not sums or tn == n, name
    nk = k_dim // tk // k_part[1]
    k0 = k_part[0] * nk
    if not nt:
        nbs = cs // tn
        w_spec = pl.BlockSpec((None, tk, tn), lambda i, j, k: (j // nbs, k + k0, j % nbs))
    else:
        kbs = cs // tk
        w_spec = pl.BlockSpec((None, tn, tk), lambda i, j, k: ((k + k0) // kbs, j, (k + k0) % kbs))
    tm = tm or min(m, 1024)
    ne, no, nsum = len(extras) + len(params), len(out_dtypes), len(sums)

    def kern(a_ref, w_ref, *rest):
        e_refs = rest[:ne]
        o_refs = rest[ne:ne + no]
        s_refs = rest[ne + no:ne + no + nsum]

        def part():
            av = a_ref[...].astype(BF)
            return _dot_nt(av, w_ref[...]) if nt else _dot(av, w_ref[...])

        def finish(res):
            vals = epi(res, *[e[...] for e in e_refs]) if epi else (res,)
            for o, v in zip(o_refs, vals[:no]):
                o[...] = v.astype(o.dtype)
            for sr, v in zip(s_refs, vals[no:]):
                sr[...] += v

        if nsum:
            @pl.when(jnp.logical_and(pl.program_id(0) == 0, pl.program_id(2) == 0))
            def _():
                for sr in s_refs:
                    sr[...] = jnp.zeros_like(sr)

        if nk == 1:
            finish(part())
            return
        acc = rest[ne + no + nsum]
        k = pl.program_id(2)

        @pl.when(k == 0)
        def _():
            acc[...] = part()

        @pl.when(jnp.logical_and(k > 0, k < nk - 1))
        def _():
            acc[...] += part()

        @pl.when(k == nk - 1)
        def _():
            finish(acc[...] + part())

    in_specs = [pl.BlockSpec((tm, tk), lambda i, j, k: (i, k + k0)), w_spec]
    split = lambda items: [(it if isinstance(it, tuple) else (it, None)) for it in items]
    extras, params = split(extras), split(params)
    in_specs += [pl.BlockSpec((tm, tn), functools.partial(lambda i, j, k, off: (i, j + off), off=off or 0)) for _, off in extras]
    in_specs += [pl.BlockSpec(p.shape, lambda i, j, k: (0, 0)) if off is None else
                 pl.BlockSpec((p.shape[0], tn), functools.partial(lambda i, j, k, off: (0, j + off), off=off)) for p, off in params]
    extras, params = [e for e, _ in extras], [p for p, _ in params]
    out_specs = [pl.BlockSpec((tm, tn), lambda i, j, k: (i, j)) for _ in out_dtypes]
    out_specs += [pl.BlockSpec(sh, lambda i, j, k: (0, 0)) for sh in sums]
    out_shape = [jax.ShapeDtypeStruct((m, n), dt) for dt in out_dtypes] + [jax.ShapeDtypeStruct(sh, F32) for sh in sums]
    res = _pcall(name, kern, (m // tm, n // tn, nk), in_specs, out_specs, out_shape,
                 [pltpu.VMEM((tm, tn), F32)] if nk > 1 else [],
                 ("arbitrary",) * 3 if nsum else ("parallel", "parallel", "arbitrary"), (a, w3, *extras, *params), side)
    main = res[0] if side is not None else res
    main = main[0] if no + nsum == 1 else main
    return (main, res[1]) if side is not None else main


def _mm_tn(name, a, b, tm=None, tn=None, tk=None, side=None):
    t, ka = a.shape
    _, n = b.shape
    tm = tm or min(ka, 1024)
    tn = tn or min(n, 1024)
    tk = tk or min(t, 1024)

    def kern(a_ref, b_ref, o_ref):
        def part():
            return _dot_tn(a_ref[...].astype(BF), b_ref[...].astype(BF))

        @pl.when(pl.program_id(2) == 0)
        def _():
            o_ref[...] = part()

        @pl.when(pl.program_id(2) > 0)
        def _():
            o_ref[...] += part()

    res = _pcall(name, kern, (ka // tm, n // tn, t // tk),
                 [pl.BlockSpec((tk, tm), lambda i, j, k: (k, i)), pl.BlockSpec((tk, tn), lambda i, j, k: (k, j))],
                 [pl.BlockSpec((tm, tn), lambda i, j, k: (i, j))], [jax.ShapeDtypeStruct((ka, n), F32)], [],
                 ("parallel", "parallel", "arbitrary"), (a, b), side)
    return (res[0][0], res[1]) if side is not None else res[0]


def _rms(x, g):
    return x * lax.rsqrt(jnp.mean(x * x, axis=-1, keepdims=True) + EPS) * g


def _rms_bwd(x, g, dy):
    r = lax.rsqrt(jnp.mean(x * x, axis=-1, keepdims=True) + EPS)
    xh = x * r
    dxh = dy * g
    dx = r * (dxh - xh * jnp.mean(dxh * xh, axis=-1, keepdims=True))
    return dx, jnp.sum(dy * xh, axis=0, keepdims=True)


def _rot(t, c, s):
    lane = lax.broadcasted_iota(jnp.int32, t.shape, 1)
    swapped = jnp.where(lane < ROT_DIM // 2, pltpu.roll(t, HEAD_DIM - ROT_DIM // 2, 1), pltpu.roll(t, ROT_DIM // 2, 1))
    return t * c + swapped * s


def _rope_tables(t):
    half = ROT_DIM // 2
    pos = jnp.arange(t, dtype=F32)
    inv_freq = ROPE_THETA ** (-jnp.arange(0, ROT_DIM, 2, dtype=F32) / ROT_DIM)
    ang = pos[:, None] * inv_freq[None, :]
    cos, sin = jnp.cos(ang), jnp.sin(ang)
    ones = jnp.ones((t, HEAD_DIM - ROT_DIM), F32)
    c_tab = jnp.concatenate([cos, cos, ones], axis=1)
    s_tab = jnp.concatenate([-sin, sin, 0.0 * ones], axis=1)
    return c_tab, s_tab


def _merge_fn(o0, o1, o2, l0, l1, l2):
    m = jnp.maximum(jnp.maximum(l0, l1), l2)
    e0, e1, e2 = jnp.exp(l0 - m), jnp.exp(l1 - m), jnp.exp(l2 - m)
    tot = e0 + e1 + e2
    return (e0 * o0 + e1 * o1 + e2 * o2) / tot, m + jnp.log(tot)


def _ln_parts(c1):
    mu = jnp.mean(c1, axis=-1, keepdims=True)
    xc = c1 - mu
    r = lax.rsqrt(jnp.mean(xc * xc, axis=-1, keepdims=True) + EPS)
    return xc * r, r


def _ln_silu_fn(c1, g, b):
    xh, _ = _ln_parts(c1)
    yl = xh * g + b
    return (yl * _sig(yl),)


def _ln_silu_bwd_fn(c1, dout, g, b):
    xh, r = _ln_parts(c1)
    yl = xh * g + b
    s = _sig(yl)
    dyl = dout * (s + yl * s * (1.0 - s))
    dxh = dyl * g
    dx = r * (dxh - jnp.mean(dxh, axis=-1, keepdims=True) - xh * jnp.mean(dxh * xh, axis=-1, keepdims=True))
    return dx, jnp.sum(dyl * xh, axis=0, keepdims=True), jnp.sum(dyl, axis=0, keepdims=True)


def _gate_epi(yc, ya, ga, gb, ba, bb):
    return _sig(ga + ba) * ya + _sig(gb + bb) * yc, yc


def _gate_bwd_epi(dm, ya, yc, ga, gb, bg):
    sa = _sig(ga + bg[:, :D_MODEL])
    sb = _sig(gb + bg[:, D_MODEL:])
    dga = dm * ya * sa * (1.0 - sa)
    dgb = dm * yc * sb * (1.0 - sb)
    return dm * sa, dm * sb, dga, dgb, jnp.sum(dga, axis=0, keepdims=True), jnp.sum(dgb, axis=0, keepdims=True)


def _res_norm_epi(acc, res, g):
    xn = res + acc
    return xn, _rms(xn, g)


def _norm_bwd_epi(acc, a, dres, g):
    dx, dg = _rms_bwd(a, g, acc)
    return dres + dx, dres + dx, dg


def _final_epi(acc, res, tgt, g):
    return _final_fn(res + acc, tgt, g)


def _final_fn(x3, tgt, g):
    err = _rms(x3, g) - tgt
    lrow = jnp.sum(err * err, axis=-1, keepdims=True) * (0.5 / D_MODEL)
    lsum = jnp.sum(lrow, axis=0, keepdims=True)
    dx, dg = _rms_bwd(x3, g, err * (1.0 / D_MODEL))
    return dx, dx, jnp.broadcast_to(lsum, (1, LANES)), dg


def _attn_geometry(t, dil):
    cls = t // dil
    rows = min(4 * BLK, cls)
    return rows, rows // BLK, cls // rows


def _head_lanes(h):
    return slice(h * HEAD_DIM, (h + 1) * HEAD_DIM)


def _band_mask():
    row = lax.broadcasted_iota(jnp.int32, (BLK, 2 * BLK), 0)
    col = lax.broadcasted_iota(jnp.int32, (BLK, 2 * BLK), 1)
    return jnp.logical_and(col >= row, col <= row + BLK), col


def _stage_window(scr, halo_ref, cur_ref):
    scr[:BLK, :] = halo_ref[...]
    scr[BLK:, :] = cur_ref[...]


def _attn_fwd(name, q, k, v, t, dil):
    rows, nbk, spc = _attn_geometry(t, dil)

    def kern(q_ref, k_ref, kh_ref, v_ref, vh_ref, o_ref, l_ref, k_scr, v_scr):
        i = pl.program_id(0)
        first_shift = jnp.where(i % spc == 0, BLK, 0)
        _stage_window(k_scr, kh_ref, k_ref)
        _stage_window(v_scr, vh_ref, v_ref)
        band, col = _band_mask()
        band_first = jnp.logical_and(band, col >= first_shift)
        for h in range(HEADS_PER_GROUP):
            hs = _head_lanes(h)
            for b in range(nbk):
                rs, win = slice(b * BLK, (b + 1) * BLK), slice(b * BLK, (b + 2) * BLK)
                s = jnp.where(band_first if b == 0 else band, _dot_nt(q_ref[rs, hs], k_scr[win, hs]) * ATTN_SCALE, NEG)
                m = jnp.max(s, axis=1, keepdims=True)
                p = jnp.exp(s - m)
                tot = jnp.sum(p, axis=1, keepdims=True)
                o_ref[rs, hs] = _dot(p.astype(BF), v_scr[win, hs]) / tot
                l_ref[rs, hs] = jnp.broadcast_to(m + jnp.log(tot), (BLK, HEAD_DIM))

    def cur(cb):
        return pl.BlockSpec((rows, GROUP_W), lambda i: (i, cb))

    def halo(cb):
        return pl.BlockSpec((BLK, GROUP_W), lambda i: (jnp.maximum(i * nbk - 1, 0), cb))

    (qa, qc), (ka, kc_), (va, vc_) = q, k, v
    return _pcall(name, kern, (t // rows,), [cur(qc), cur(kc_), halo(kc_), cur(vc_), halo(vc_)],
                  [pl.BlockSpec((rows, GROUP_W), lambda i: (i, 0))] * 2, [jax.ShapeDtypeStruct((t, GROUP_W), F32)] * 2,
                  [pltpu.VMEM((rows + BLK, GROUP_W), BF)] * 2, ("parallel",), (qa, ka, ka, va, va))


def _attn_bwd(name, q, k, v, da, dl, lse, t, dil, side=None):
    rows, nbk, spc = _attn_geometry(t, dil)
    nblk = t // BLK

    def kern(q_ref, qn_ref, k_ref, kh_ref, v_ref, vh_ref, da_ref, dan_ref, dl_ref, dln_ref, ls_ref, lsn_ref,
             dq_ref, dk_ref, dv_ref, k_scr, v_scr):
        i = pl.program_id(0)
        first_shift = jnp.where(i % spc == 0, BLK, 0)
        next_shift = jnp.where((i + 1) % spc == 0, BLK, 0)
        _stage_window(k_scr, kh_ref, k_ref)
        _stage_window(v_scr, vh_ref, v_ref)
        band, col = _band_mask()
        band_first = jnp.logical_and(band, col >= first_shift)
        row1 = lax.broadcasted_iota(jnp.int32, (BLK, BLK), 0)
        col1 = lax.broadcasted_iota(jnp.int32, (BLK, BLK), 1)
        for h in range(HEADS_PER_GROUP):
            hs = _head_lanes(h)
            for b in range(nbk):
                rs, win = slice(b * BLK, (b + 1) * BLK), slice(b * BLK, (b + 2) * BLK)
                qb, dab = q_ref[rs, hs], da_ref[rs, hs].astype(BF)
                kw, vw = k_scr[win, hs], v_scr[win, hs]
                p = jnp.where(band_first if b == 0 else band,
                              jnp.exp(_dot_nt(qb, kw) * ATTN_SCALE - ls_ref[rs, hs][:, :1]), 0.0)
                ds = (p * (_dot_nt(dab, vw) - dl_ref[rs, hs][:, :1]) * ATTN_SCALE).astype(BF)
                dq_ref[rs, hs] = _dot(ds, kw).astype(BF)
                dkw, dvw = _dot_tn(ds, qb), _dot_tn(p.astype(BF), dab)
                if b >= 1:
                    ps = slice((b - 1) * BLK, b * BLK)
                    dk_ref[ps, hs] = (pend_k + dkw[:BLK]).astype(BF)
                    dv_ref[ps, hs] = (pend_v + dvw[:BLK]).astype(BF)
                pend_k, pend_v = dkw[BLK:], dvw[BLK:]
            ls_rows = slice((nbk - 1) * BLK, nbk * BLK)
            last = slice(nbk * BLK, (nbk + 1) * BLK)
            qb, dab = qn_ref[:, hs], dan_ref[:, hs].astype(BF)
            kp, vp = k_scr[last, hs], v_scr[last, hs]
            p = jnp.where(col1 >= row1 + next_shift, jnp.exp(_dot_nt(qb, kp) * ATTN_SCALE - lsn_ref[:, hs]), 0.0)
            ds = (p * (_dot_nt(dab, vp) - dln_ref[:, hs]) * ATTN_SCALE).astype(BF)
            dk_ref[ls_rows, hs] = (pend_k + _dot_tn(ds, qb)).astype(BF)
            dv_ref[ls_rows, hs] = (pend_v + _dot_tn(p.astype(BF), dab)).astype(BF)

    def cur(cb):
        return pl.BlockSpec((rows, GROUP_W), lambda i: (i, cb))

    def prev(cb):
        return pl.BlockSpec((BLK, GROUP_W), lambda i: (jnp.maximum(i * nbk - 1, 0), cb))

    def nxt(cb):
        return pl.BlockSpec((BLK, GROUP_W), lambda i: (jnp.minimum((i + 1) * nbk, nblk - 1), cb))

    (qa, qc), (ka, kc_), (va, vc_) = q, k, v
    return _pcall(name, kern, (t // rows,),
                  [cur(qc), nxt(qc), cur(kc_), prev(kc_), cur(vc_), prev(vc_), cur(0), nxt(0), cur(0), nxt(0), cur(0), nxt(0)],
                  [pl.BlockSpec((rows, GROUP_W), lambda i: (i, 0))] * 3, [jax.ShapeDtypeStruct((t, GROUP_W), BF)] * 3,
                  [pltpu.VMEM((rows + BLK, GROUP_W), BF)] * 2, ("parallel",),
                  (qa, qa, ka, ka, va, va, da, da, dl, dl, lse, lse), side)


CLS_TILE = 512


def _cls_block(t, tile, dil, dtype):
    if dil == 1:
        return pl.BlockSpec((tile, GROUP_W), lambda i: (i, 0)), jax.ShapeDtypeStruct((t, GROUP_W), dtype)
    return (pl.BlockSpec((dil, tile // dil, GROUP_W), lambda i: (0, i, 0)),
            jax.ShapeDtypeStruct((dil, t // dil, GROUP_W), dtype))


def _head_scratch(tile):
    return pltpu.VMEM((tile, HEAD_DIM), F32)


def _rope_split(z, c_tab, s_tab, t):
    tile = min(CLS_TILE, t)
    n_heads = ATTN_WIDTH // HEAD_DIM

    def kern(zq_ref, zk_ref, zv_ref, c_ref, s_ref, *rest):
        outs, scr = rest[:9], rest[9]
        c, s = c_ref[...], s_ref[...]
        for which, z_ref in enumerate((zq_ref, zk_ref, zv_ref)):
            for h in range(n_heads):
                g, hs = h // HEADS_PER_GROUP, _head_lanes(h % HEADS_PER_GROUP)
                val = z_ref[:, h * HEAD_DIM:(h + 1) * HEAD_DIM].astype(F32)
                if which < 2:
                    val = _rot(val, c, s)
                if g == 0:
                    outs[which][:, hs] = val.astype(BF)
                    continue
                scr[...] = val
                dil = DILATIONS[g]
                for r in range(dil):
                    outs[3 * g + which][r, :, hs] = scr[pl.ds(r, tile // dil, stride=dil), :].astype(BF)

    blocks = [_cls_block(t, tile, DILATIONS[g], BF) for g in range(3) for _ in range(3)]
    zspec = lambda cb: pl.BlockSpec((tile, ATTN_WIDTH), lambda i: (i, cb))
    tab = pl.BlockSpec((tile, HEAD_DIM), lambda i: (i, 0))
    return _pcall("rope", kern, (t // tile,), [zspec(0), zspec(1), zspec(2), tab, tab], [b[0] for b in blocks],
                  [b[1] for b in blocks], [_head_scratch(tile)], ("parallel",), (z, z, z, c_tab, s_tab))


def _merge_classes(outs, lses, t):
    tile = min(CLS_TILE, t)

    def kern(o0, l0, o1, l1, o2, l2, attn_ref, lse_ref, s_o1, s_l1, s_o2, s_l2):
        for h in range(HEADS_PER_GROUP):
            hs = _head_lanes(h)
            for src, dst, dil in ((o1, s_o1, DILATIONS[1]), (l1, s_l1, DILATIONS[1]), (o2, s_o2, DILATIONS[2]),
                                  (l2, s_l2, DILATIONS[2])):
                for r in range(dil):
                    dst[pl.ds(r, tile // dil, stride=dil), :] = src[r, :, hs]
            attn_ref[:, hs], lse_ref[:, hs] = _merge_fn(o0[:, hs], s_o1[...], s_o2[...], l0[:, hs], s_l1[...], s_l2[...])

    blocks = [_cls_block(t, tile, DILATIONS[g], F32) for g in range(3)]
    args = []
    for g in range(3):
        args += [outs[g].reshape(blocks[g][1].shape), lses[g].reshape(blocks[g][1].shape)]
    tok = pl.BlockSpec((tile, GROUP_W), lambda i: (i, 0))
    return _pcall("attn_merge", kern, (t // tile,), [blocks[g][0] for g in range(3) for _ in range(2)], [tok, tok],
                  [jax.ShapeDtypeStruct((t, GROUP_W), F32)] * 2, [_head_scratch(tile)] * 4, ("parallel",), args)


def _attn_bwd_prep(dattn, attn, lse, t):
    tile = min(CLS_TILE, t)

    def kern(da_ref, at_ref, ls_ref, dl0, da1, dl1, ls1, da2, dl2, ls2, s_da, s_dl, s_ls):
        for h in range(HEADS_PER_GROUP):
            hs = _head_lanes(h)
            da = da_ref[:, hs]
            s_da[...] = da
            s_dl[...] = jnp.broadcast_to(jnp.sum(da * at_ref[:, hs], axis=1, keepdims=True), (tile, HEAD_DIM))
            s_ls[...] = ls_ref[:, hs]
            dl0[:, hs] = s_dl[...]
            for oda, odl, ols, dil in ((da1, dl1, ls1, DILATIONS[1]), (da2, dl2, ls2, DILATIONS[2])):
                for r in range(dil):
                    rows = pl.ds(r, tile // dil, stride=dil)
                    oda[r, :, hs] = s_da[rows, :].astype(BF)
                    odl[r, :, hs] = s_dl[rows, :]
                    ols[r, :, hs] = s_ls[rows, :]

    tok = pl.BlockSpec((tile, GROUP_W), lambda i: (i, 0))
    blocks = [(tok, jax.ShapeDtypeStruct((t, GROUP_W), F32))]
    for g in (1, 2):
        blocks += [_cls_block(t, tile, DILATIONS[g], BF), _cls_block(t, tile, DILATIONS[g], F32),
                   _cls_block(t, tile, DILATIONS[g], F32)]
    res = _pcall("attn_bwd_prep", kern, (t // tile,), [tok, tok, tok], [b[0] for b in blocks], [b[1] for b in blocks],
                 [_head_scratch(tile)] * 3, ("parallel",), (dattn, attn, lse))
    flat = [a.reshape(t, GROUP_W) for a in res]
    return flat[0], flat[1:4], flat[4:7]


def _rope_bwd_join(dqs, dks, dvs, c_tab, s_tab, tail, t, side=None):
    tile = min(256, t)
    n_heads = ATTN_WIDTH // HEAD_DIM

    def kern(q0, q1, q2, k0, k1, k2, v0, v1, v2, c_ref, s_ref, ga_ref, gb_ref, gta_ref, gtb_ref, dz_ref, scr):
        c, s = c_ref[...], -s_ref[...]
        for which, srcs in enumerate(((q0, q1, q2), (k0, k1, k2), (v0, v1, v2))):
            for h in range(n_heads):
                g, hs = h // HEADS_PER_GROUP, _head_lanes(h % HEADS_PER_GROUP)
                if g == 0:
                    val = srcs[0][:, hs].astype(F32)
                else:
                    dil = DILATIONS[g]
                    for r in range(dil):
                        scr[pl.ds(r, tile // dil, stride=dil), :] = srcs[g][r, :, hs].astype(F32)
                    val = scr[...]
                if which < 2:
                    val = _rot(val, c, s)
                col = which * ATTN_WIDTH + h * HEAD_DIM
                dz_ref[:, col:col + HEAD_DIM] = val.astype(BF)
        dz_ref[:, GLU_A_COL:GLU_B_COL] = ga_ref[...]
        dz_ref[:, GLU_B_COL:GATE_A_COL] = gb_ref[...]
        dz_ref[:, GATE_A_COL:GATE_B_COL] = gta_ref[...]
        dz_ref[:, GATE_B_COL:] = gtb_ref[...]

    blocks = [_cls_block(t, tile, DILATIONS[g], BF) for g in range(3)]
    args = [a.reshape(blocks[g][1].shape) for grp in (dqs, dks, dvs) for g, a in enumerate(grp)]
    tab = pl.BlockSpec((tile, HEAD_DIM), lambda i: (i, 0))
    row = lambda w: pl.BlockSpec((tile, w), lambda i: (i, 0))
    return _pcall("rope_bwd", kern, (t // tile,), [blocks[g][0] for _ in range(3) for g in range(3)]
                  + [tab, tab, row(CONV_CH), row(CONV_CH), row(D_MODEL), row(D_MODEL)], [row(IN_WIDTH)],
                  [jax.ShapeDtypeStruct((t, IN_WIDTH), BF)], [_head_scratch(tile)], ("parallel",),
                  (*args, c_tab, s_tab, *tail), side)


def _cross_probs(qh, kh):
    s = _dot_nt(qh, kh) * CROSS_SCALE
    e = jnp.exp(s - jnp.max(s, axis=1, keepdims=True))
    return e, jnp.sum(e, axis=1, keepdims=True)


def _cross_fwd(cq, ckv, t):
    rows = min(512, t)

    def kern(q_ref, kv_ref, o_ref):
        for h in range(CROSS_HEADS):
            hs = slice(h * CROSS_HEAD_DIM, (h + 1) * CROSS_HEAD_DIM)
            vs = slice(D_MODEL + h * CROSS_HEAD_DIM, D_MODEL + (h + 1) * CROSS_HEAD_DIM)
            e, tot = _cross_probs(q_ref[:, hs], kv_ref[:, hs])
            o_ref[:, hs] = (_dot(e.astype(BF), kv_ref[:, vs]) / tot).astype(BF)

    return pl.pallas_call(
        kern, name="cross_fwd", grid=(t // rows,),
        in_specs=[pl.BlockSpec((rows, D_MODEL), lambda i: (i, 0)), pl.BlockSpec((N_MEM, 2 * D_MODEL), lambda i: (0, 0))],
        out_specs=pl.BlockSpec((rows, D_MODEL), lambda i: (i, 0)),
        out_shape=jax.ShapeDtypeStruct((t, D_MODEL), BF),
        compiler_params=_params(("parallel",)),
    )(cq, ckv)


def _cross_bwd(cq, ckv, dco, t):
    rows = min(512, t)

    def kern(q_ref, kv_ref, do_ref, dq_ref, dkv_ref):
        @pl.when(pl.program_id(0) == 0)
        def _():
            dkv_ref[...] = jnp.zeros_like(dkv_ref)
        for h in range(CROSS_HEADS):
            hs = slice(h * CROSS_HEAD_DIM, (h + 1) * CROSS_HEAD_DIM)
            vs = slice(D_MODEL + h * CROSS_HEAD_DIM, D_MODEL + (h + 1) * CROSS_HEAD_DIM)
            qh, kh, vh, doh = q_ref[:, hs], kv_ref[:, hs], kv_ref[:, vs], do_ref[:, hs]
            e, tot = _cross_probs(qh, kh)
            p = e / tot
            dp = _dot_nt(doh, vh)
            ds = (p * (dp - jnp.sum(p * dp, axis=1, keepdims=True)) * CROSS_SCALE).astype(BF)
            dq_ref[:, hs] = _dot(ds, kh).astype(BF)
            dkv_ref[:, hs] += _dot_tn(ds, qh)
            dkv_ref[:, vs] += _dot_tn(p.astype(BF), doh)

    return pl.pallas_call(
        kern, name="cross_bwd", grid=(t // rows,),
        in_specs=[pl.BlockSpec((rows, D_MODEL), lambda i: (i, 0)), pl.BlockSpec((N_MEM, 2 * D_MODEL), lambda i: (0, 0)),
                  pl.BlockSpec((rows, D_MODEL), lambda i: (i, 0))],
        out_specs=[pl.BlockSpec((rows, D_MODEL), lambda i: (i, 0)), pl.BlockSpec((N_MEM, 2 * D_MODEL), lambda i: (0, 0))],
        out_shape=[jax.ShapeDtypeStruct((t, D_MODEL), BF), jax.ShapeDtypeStruct((N_MEM, 2 * D_MODEL), F32)],
        compiler_params=_params(("arbitrary",)),
    )(cq, ckv, dco)


CONV_TILE = 512
CONV_CHUNK = 128
HALO = 32


def _conv_fwd(z, w32, bias, t):
    tile = min(CONV_TILE, t)
    a_cb, b_cb = GLU_A_COL // LANES, GLU_B_COL // LANES
    hb = tile // HALO

    def kern(a_ref, b_ref, ah_ref, bh_ref, w_ref, bias_ref, o_ref, g_scr):
        i = pl.program_id(1)
        g_scr[HALO:, :] = _glu(a_ref[...], b_ref[...])
        g_scr[:HALO, :] = _glu(ah_ref[...], bh_ref[...]) * jnp.where(i > 0, 1.0, 0.0)
        for c in range(tile // CONV_CHUNK):
            acc = jnp.broadcast_to(bias_ref[...], (CONV_CHUNK, LANES))
            for j in range(CONV_K):
                lo = c * CONV_CHUNK + HALO - (CONV_K - 1) + j
                acc = acc + w_ref[j:j + 1, :] * g_scr[lo:lo + CONV_CHUNK, :]
            o_ref[c * CONV_CHUNK:(c + 1) * CONV_CHUNK, :] = acc

    def cur(cb):
        return pl.BlockSpec((tile, LANES), lambda j, i: (i, cb + j))

    def prev(cb):
        return pl.BlockSpec((HALO, LANES), lambda j, i: (jnp.maximum(i * hb - 1, 0), cb + j))

    return pl.pallas_call(
        kern, name="conv_fwd", grid=(CONV_CH // LANES, t // tile),
        in_specs=[cur(a_cb), cur(b_cb), prev(a_cb), prev(b_cb),
                  pl.BlockSpec((CONV_KP, LANES), lambda j, i: (0, j)), pl.BlockSpec((1, LANES), lambda j, i: (0, j))],
        out_specs=pl.BlockSpec((tile, LANES), lambda j, i: (i, j)),
        out_shape=jax.ShapeDtypeStruct((t, CONV_CH), F32),
        scratch_shapes=[pltpu.VMEM((tile + HALO, LANES), F32)],
        compiler_params=_params(("parallel", "parallel")),
    )(z, z, z, z, w32, bias)


def _conv_bwd(z, dc1, w32, t, side=None):
    tile = min(CONV_TILE, t)
    a_cb, b_cb = GLU_A_COL // LANES, GLU_B_COL // LANES
    hb = tile // HALO
    n_tiles = t // tile
    n_chunks = tile // CONV_CHUNK

    def kern(a_ref, b_ref, ah_ref, bh_ref, d_ref, dn_ref, w_ref, da_ref, db_ref, dw_ref, g_scr, d_scr):
        i = pl.program_id(1)
        g_scr[HALO:, :] = _glu(a_ref[...], b_ref[...])
        g_scr[:HALO, :] = _glu(ah_ref[...], bh_ref[...]) * jnp.where(i > 0, 1.0, 0.0)
        d_scr[:tile, :] = d_ref[...]
        d_scr[tile:, :] = dn_ref[...] * jnp.where(i < n_tiles - 1, 1.0, 0.0)

        @pl.when(i == 0)
        def _():
            dw_ref[...] = jnp.zeros_like(dw_ref)

        for c in range(n_chunks):
            cs = slice(c * CONV_CHUNK, (c + 1) * CONV_CHUNK)
            acc = jnp.zeros((CONV_CHUNK, LANES), F32)
            for j in range(CONV_K):
                lo = c * CONV_CHUNK + (CONV_K - 1) - j
                acc = acc + w_ref[j:j + 1, :] * d_scr[lo:lo + CONV_CHUNK, :]
            sgc = _sig(b_ref[cs, :].astype(F32))
            da_ref[cs, :] = (acc * sgc).astype(BF)
            db_ref[cs, :] = (acc * a_ref[cs, :].astype(F32) * sgc * (1.0 - sgc)).astype(BF)
        for j in range(CONV_K):
            tot = jnp.zeros((1, LANES), F32)
            for c in range(n_chunks):
                lo = c * CONV_CHUNK + HALO - (CONV_K - 1) + j
                tot = tot + jnp.sum(d_ref[c * CONV_CHUNK:(c + 1) * CONV_CHUNK, :] * g_scr[lo:lo + CONV_CHUNK, :],
                                    axis=0, keepdims=True)
            dw_ref[j:j + 1, :] += tot
        dw_ref[CONV_K:CONV_KP, :] += jnp.sum(d_ref[...], axis=0, keepdims=True)

    def cur(cb):
        return pl.BlockSpec((tile, LANES), lambda j, i: (i, cb + j))

    def prev(cb):
        return pl.BlockSpec((HALO, LANES), lambda j, i: (jnp.maximum(i * hb - 1, 0), cb + j))

    return _pcall(
        "conv_bwd", kern, (CONV_CH // LANES, n_tiles),
        [cur(a_cb), cur(b_cb), prev(a_cb), prev(b_cb), cur(0),
         pl.BlockSpec((HALO, LANES), lambda j, i: (jnp.minimum((i + 1) * hb, t // HALO - 1), j)),
         pl.BlockSpec((CONV_KP, LANES), lambda j, i: (0, j))],
        [pl.BlockSpec((tile, LANES), lambda j, i: (i, j)), pl.BlockSpec((tile, LANES), lambda j, i: (i, j)),
         pl.BlockSpec((CONV_KP, LANES), lambda j, i: (0, j))],
        [jax.ShapeDtypeStruct((t, CONV_CH), BF), jax.ShapeDtypeStruct((t, CONV_CH), BF),
         jax.ShapeDtypeStruct((CONV_KP, CONV_CH), F32)],
        [pltpu.VMEM((tile + HALO, LANES), F32), pltpu.VMEM((tile + HALO, LANES), F32)],
        ("parallel", "arbitrary"), (z, z, z, z, dc1, dc1, w32), side)


def _adam_fn(w, g, m, v):
    m = ADAM_B1 * m + (1.0 - ADAM_B1) * g
    v = ADAM_B2 * v + (1.0 - ADAM_B2) * (g * g)
    m_hat = m / (1.0 - ADAM_B1 ** ADAM_STEP)
    v_hat = v / (1.0 - ADAM_B2 ** ADAM_STEP)
    delta = -ADAM_LR * (m_hat / (jnp.sqrt(v_hat) + ADAM_EPS) + ADAM_WD * w)
    return delta, m, v


def _adam(name, w, g, m, v):
    rows, cols = w.shape
    tile = _ew_tile(rows, cols)
    return _rowcall(name, _adam_fn, rows, tile, [(a, cols, 0) for a in (w, g, m, v)], [], [(cols, F32)] * 3)


def _place():
    x, y, c = lax.axis_index("x"), lax.axis_index("y"), lax.axis_index("c")
    chips = [(1 - x, y), (x, 1 - y), (1 - x, 1 - y)]
    return x, y, c, chips


def _gather_side(shards):
    nw = len(shards)
    chip = 2 * lax.axis_index("x") + lax.axis_index("y")
    staged = [lax.dynamic_update_index_in_dim(jnp.zeros((4,) + s.shape, s.dtype), s, chip, 0) for s in shards]

    def build(_, outs, send_sems, recv_sems):
        x, y, c, chips = _place()
        me = 2 * x + y
        sibling = (x, y, 1 - c)

        def half(w, lead, h):
            n = shards[w].shape[0] // 2
            return outs[w].at[lead, pl.ds(h * n, n)]

        def copy(w, k, part, to):
            return pltpu.make_async_remote_copy(src_ref=part, dst_ref=part, send_sem=send_sems.at[6 * w + k],
                                                recv_sem=recv_sems.at[6 * w + k], device_id=to, device_id_type=MESH)

        def start():
            for w in range(nw):
                for k, (px, py) in enumerate(chips):
                    copy(w, k, half(w, me, c), (px, py, c)).start()

        def finish():
            for w in range(nw):
                for k, (px, py) in enumerate(chips):
                    landed = half(w, 2 * px + py, c)
                    copy(w, k, landed, (px, py, c)).wait_recv()
                    copy(w, 3 + k, landed, sibling).start()
            for w in range(nw):
                for k, (px, py) in enumerate(chips):
                    copy(w, 3 + k, half(w, 2 * px + py, 1 - c), sibling).wait_recv()
            for w in range(nw):
                for k, (px, py) in enumerate(chips):
                    copy(w, k, half(w, me, c), (px, py, c)).wait_send()
                    copy(w, 3 + k, half(w, 2 * px + py, c), sibling).wait_send()

        return start, finish

    return _Side(staged, [jax.ShapeDtypeStruct((4,) + s.shape, s.dtype) for s in shards], 6 * nw, build,
                 aliases={w: w for w in range(nw)})


def _gather8(name, v, side=None):
    rows = v.shape[0]

    def body(v_ref, all_ref, sum_ref, send_sems, recv_sems):
        x, y, c, _ = _place()
        me = 4 * x + 2 * y + c
        all_ref[me] = v_ref[...]
        copies = []
        for k in range(1, 8):
            px, py, pc = x ^ (k >> 2), y ^ ((k >> 1) & 1), c ^ (k & 1)
            copies.append(pltpu.make_async_remote_copy(
                src_ref=v_ref, dst_ref=all_ref.at[me], send_sem=send_sems.at[k - 1], recv_sem=recv_sems.at[k - 1],
                device_id=(px, py, pc), device_id_type=MESH))
            copies[-1].start()
        for k in range(1, 8):
            px, py, pc = x ^ (k >> 2), y ^ ((k >> 1) & 1), c ^ (k & 1)
            theirs = all_ref.at[4 * px + 2 * py + pc]
            pltpu.make_async_remote_copy(
                src_ref=theirs, dst_ref=theirs, send_sem=send_sems.at[k - 1], recv_sem=recv_sems.at[k - 1],
                device_id=(px, py, pc), device_id_type=MESH).wait_recv()
        for cp in copies:
            cp.wait_send()
        tot = all_ref[0]
        for d in range(1, 8):
            tot = tot + all_ref[d]
        sum_ref[...] = tot

    vm = pl.BlockSpec(memory_space=pltpu.VMEM)
    return _pcall(name, body, (), [vm], [vm, vm],
                  [jax.ShapeDtypeStruct((8, rows, LANES), F32), jax.ShapeDtypeStruct((rows, LANES), F32)],
                  [pltpu.SemaphoreType.DMA((7,)), pltpu.SemaphoreType.DMA((7,))], None, (v,), side)


def _region(ref, col_sharded, shape, j, h):
    r, ccols = shape
    if col_sharded:
        return ref.at[pl.ds(h * (r // 2), r // 2), pl.ds(j * (ccols // 4), ccols // 4)]
    n = r // 8
    return ref.at[pl.ds((2 * j + h) * n, n), :]


def _region_shape(col_sharded, shape):
    r, ccols = shape
    return (r // 2, ccols // 4) if col_sharded else (r // 8, ccols)


def _exchange(copies):
    def build(ins, outs, send_sems, recv_sems):
        def start():
            for cp in copies(ins, outs, send_sems, recv_sems):
                cp.start()

        def finish():
            for cp in copies(ins, outs, send_sems, recv_sems):
                cp.wait()

        return start, finish
    return build


def _swap_side(grads, kinds):
    nw = len(grads)

    def copies(ins, theirs, send_sems, recv_sems):
        x, y, c, _ = _place()
        return [pltpu.make_async_remote_copy(
            src_ref=_region(ins[w], kinds[w], grads[w].shape, j, 1 - c), dst_ref=theirs[w].at[j],
            send_sem=send_sems.at[4 * w + j], recv_sem=recv_sems.at[4 * w + j], device_id=(x, y, 1 - c), device_id_type=MESH)
            for w in range(nw) for j in range(4)]

    shapes = [jax.ShapeDtypeStruct((4,) + _region_shape(kinds[w], grads[w].shape), F32) for w in range(nw)]
    return _Side(grads, shapes, 4 * nw, _exchange(copies))


def _scatter_side(parts):
    nw = len(parts)

    def copies(ins, outs, send_sems, recv_sems):
        x, y, c, chips = _place()
        return [pltpu.make_async_remote_copy(
            src_ref=ins[w].at[2 * px + py], dst_ref=outs[w].at[k], send_sem=send_sems.at[3 * w + k],
            recv_sem=recv_sems.at[3 * w + k], device_id=(px, py, c), device_id_type=MESH)
            for w in range(nw) for k, (px, py) in enumerate(chips)]

    shapes = [jax.ShapeDtypeStruct((3,) + p.shape[1:], p.dtype) for p in parts]
    return _Side(parts, shapes, 3 * nw, _exchange(copies))


def _share_side(halves):
    nw = len(halves)

    def copies(ins, outs, send_sems, recv_sems):
        x, y, c, _ = _place()
        return [pltpu.make_async_remote_copy(
            src_ref=ins[w], dst_ref=outs[w].at[c], send_sem=send_sems.at[w], recv_sem=recv_sems.at[w],
            device_id=(x, y, 1 - c), device_id_type=MESH) for w in range(nw)]

    return _Side(halves, [jax.ShapeDtypeStruct((2,) + h.shape, F32) for h in halves], nw, _exchange(copies))


EW_BLOCK = 512 * 1024


def _ew_tile(rows, cols):
    limit = max(8, EW_BLOCK // cols)
    return max(d for d in range(8, min(rows, limit) + 1, 8) if rows % d == 0)


def _indexed_sum(name, fn, grid, in_specs, out_specs, out_shape, index, arrays):
    def kern(_, *refs):
        n_in = len(in_specs)
        vals = fn(*[r[...] for r in refs[:n_in]])
        for r, v in zip(refs[n_in:], vals):
            r[...] = v.astype(r.dtype)

    return pl.pallas_call(
        kern, name=name, out_shape=out_shape,
        grid_spec=pltpu.PrefetchScalarGridSpec(num_scalar_prefetch=1, grid=grid, in_specs=in_specs, out_specs=out_specs),
        compiler_params=_params(("parallel",) * len(grid)),
    )(index.astype(jnp.int32).reshape(1), *arrays)


def _pair_sums(names, grads, theirs):
    parts, parts_bf = {}, {}
    for n, other in zip(names, theirs):
        _, rr, cc = other.shape
        tile = _ew_tile(rr, cc)
        nb = rr // tile
        if COL_SHARDED[n]:
            mine = pl.BlockSpec((tile, cc), lambda j, i, c: (c[0] * nb + i, j))
        else:
            mine = pl.BlockSpec((tile, cc), lambda j, i, c: ((2 * j + c[0]) * nb + i, 0))
        flat = pl.BlockSpec((tile, cc), lambda j, i, c: (j * nb + i, 0))
        p, pb = _indexed_sum(f"grad_pair_sum_{n}", lambda u, v: (u + v, u + v), (4, nb), [mine, flat], [flat, flat],
                             [jax.ShapeDtypeStruct((4 * rr, cc), F32), jax.ShapeDtypeStruct((4 * rr, cc), BF)],
                             lax.axis_index("c"), (grads[n], other.reshape(4 * rr, cc)))
        parts[n], parts_bf[n] = p.reshape(4, rr, cc), pb.reshape(4, rr, cc)
    return parts, parts_bf


def _chip_sums(names, parts, landed):
    halves = {}
    for n, got in zip(names, landed):
        _, rr, cc = got.shape
        tile = _ew_tile(rr, cc)
        nb = rr // tile
        own = pl.BlockSpec((tile, cc), lambda i, chip: (chip[0] * nb + i, 0))
        peer = lambda k: pl.BlockSpec((tile, cc), lambda i, chip: (k * nb + i, 0))
        halves[n] = _indexed_sum(f"grad_chip_sum_{n}", lambda o, k0, k1, k2: (((o + k0) + k1) + k2,), (nb,),
                                 [own, peer(0), peer(1), peer(2)], [pl.BlockSpec((tile, cc), lambda i, chip: (i, 0))],
                                 [jax.ShapeDtypeStruct((rr, cc), F32)], 2 * lax.axis_index("x") + lax.axis_index("y"),
                                 (parts[n].reshape(4 * rr, cc),) + (got.reshape(3 * rr, cc),) * 3)[0]
    return halves


def _both_halves(mine, shared):
    both = lax.dynamic_update_index_in_dim(shared, mine, lax.axis_index("c"), 0)
    return both.reshape(2 * mine.shape[0], mine.shape[1])


BIG = ("w_in", "w_attn_proj", "w_conv_proj", "w_out", "w_cq", "w_ckv", "w_co", "w_up", "w_down")
COL_SHARDED = {"w_in": True, "w_attn_proj": True, "w_conv_proj": True, "w_out": False, "w_cq": False,
               "w_ckv": True, "w_co": False, "w_up": True, "w_down": False}
SMALL = ("g_mix", "b_gate", "conv_b", "conv_ln_g", "conv_ln_b", "g_cross", "g_mem", "g_mlp", "g_final")
ORDER = ("g_mix", "w_in", "b_gate", "conv_w", "conv_b", "conv_ln_g", "conv_ln_b", "w_attn_proj", "w_conv_proj", "w_out",
         "g_cross", "g_mem", "w_cq", "w_ckv", "w_co", "g_mlp", "w_up", "w_down", "g_final")


def _pad_rows(flat, rows):
    return jnp.pad(flat, (0, rows * LANES - flat.shape[0])).reshape(rows, LANES)


REST = tuple(n for n in BIG if n != "w_in")
WAVE_MLP = ("w_down", "w_up")
WAVE_MID = ("w_co", "w_cq", "w_ckv", "w_out", "w_attn_proj", "w_conv_proj")


def _local_step(x, mem, tgt, shards, small, conv_w_full):
    t = x.shape[0]
    tr = 256
    c_tab, s_tab = _rope_tables(t)
    row = lambda v: v.reshape(1, -1)
    g_mix, g_cross, g_mem, g_mlp, g_final = (row(small[n]) for n in ("g_mix", "g_cross", "g_mem", "g_mlp", "g_final"))
    b_gate, conv_b, ln_g, ln_b = (row(small[n]) for n in ("b_gate", "conv_b", "conv_ln_g", "conv_ln_b"))
    w32 = jnp.pad(conv_w_full, ((0, CONV_KP - CONV_K), (0, 0)))

    (u,), (w_in_all,) = _rowcall("mix_norm", lambda a, g: (_rms(a, g),), t, tr, [(x, D_MODEL, 0)], [g_mix], [(D_MODEL, BF)],
                                 side=_gather_side([shards["w_in"]]))
    z, gathered = _mm("in_proj", u, w_in_all, out_dtypes=(BF,), side=_gather_side([shards[n] for n in REST]))
    wfull = {"w_in": w_in_all}
    for n, g in zip(REST, gathered):
        wfull[n] = g if COL_SHARDED[n] else g.reshape(1, 4 * g.shape[1], g.shape[2])
    qkv = _rope_split(z, c_tab, s_tab, t)
    qkv_cls, outs, lses = [], [], []
    for g, dil in enumerate(DILATIONS):
        ops = tuple((a.reshape(t, GROUP_W), 0) for a in qkv[3 * g:3 * g + 3])
        qkv_cls.append(ops)
        o_g, l_g = _attn_fwd(f"attn_fwd_{g}", *ops, t, dil)
        outs.append(o_g)
        lses.append(l_g)
    attn, lse = _merge_classes(outs, lses, t)
    y_attn = _mm("attn_proj", attn, wfull["w_attn_proj"])
    c1 = _conv_fwd(z, w32, conv_b, t)
    (c2,) = _rowcall("conv_ln_silu", _ln_silu_fn, t, tr, [(c1, CONV_CH, 0)], [ln_g, ln_b], [(CONV_CH, BF)])
    tn_cp = wfull["w_conv_proj"].shape[2]
    merged, y_conv = _mm("conv_proj", c2, wfull["w_conv_proj"], epi=_gate_epi, out_dtypes=(BF, F32),
                         extras=(y_attn, (z, GATE_A_COL // tn_cp), (z, GATE_B_COL // tn_cp)),
                         params=((b_gate, 0), (b_gate, D_MODEL // tn_cp)))
    x1, uq = _mm("out_proj", merged, wfull["w_out"], extras=(x,), params=(g_cross,), epi=_res_norm_epi, out_dtypes=(F32, BF))
    (mn,) = _rowcall("mem_norm", lambda a, g: (_rms(a, g),), N_MEM, N_MEM, [(mem, D_MODEL, 0)], [g_mem], [(D_MODEL, BF)])
    cq = _mm("cross_q", uq, wfull["w_cq"], out_dtypes=(BF,))
    ckv = _mm("cross_kv", mn, wfull["w_ckv"], out_dtypes=(BF,))
    co = _cross_fwd(cq, ckv, t)
    x2, um = _mm("cross_out", co, wfull["w_co"], extras=(x1,), params=(g_mlp,), epi=_res_norm_epi, out_dtypes=(F32, BF))
    hact = _mm("mlp_up", um, wfull["w_up"], out_dtypes=(BF,), epi=lambda acc: (jnp.square(jnp.maximum(acc, 0.0)),))
    d3, d3b, loss_row, dg_final = _mm("mlp_down", hact, wfull["w_down"], extras=(x2, tgt), params=(g_final,), epi=_final_epi,
                                      out_dtypes=(F32, BF), sums=[(1, LANES), (1, D_MODEL)])

    gw = {}
    dhp = _mm("mlp_down_bwd", d3b, wfull["w_down"], nt=True, extras=(hact,), out_dtypes=(BF,),
              epi=lambda acc, h: (acc * 2.0 * jnp.sqrt(h.astype(F32)),))
    gw["w_down"] = _mm_tn("mlp_down_wgrad", hact, d3b)
    gw["w_up"] = _mm_tn("mlp_up_wgrad", um, dhp)

    def swap_of(names):
        return _swap_side([gw[n] for n in names], [COL_SHARDED[n] for n in names])

    (d2, d2b, dg_mlp), theirs = _mm("mlp_up_bwd", dhp, wfull["w_up"], nt=True, extras=(x2, d3), params=(g_mlp,),
                                    epi=_norm_bwd_epi, out_dtypes=(F32, BF), sums=[(1, D_MODEL)], side=swap_of(WAVE_MLP))
    parts_mlp, parts_bf_mlp = _pair_sums(WAVE_MLP, gw, theirs)

    dco = _mm("cross_out_bwd", d2b, wfull["w_co"], nt=True, out_dtypes=(BF,))
    gw["w_co"] = _mm_tn("cross_out_wgrad", co, d2b)
    dcq, dckv = _cross_bwd(cq, ckv, dco, t)
    gw["w_cq"] = _mm_tn("cross_q_wgrad", uq, dcq)
    d1, d1b, dg_cross = _mm("cross_q_bwd", dcq, wfull["w_cq"], nt=True, extras=(x1, d2), params=(g_cross,), epi=_norm_bwd_epi,
                            out_dtypes=(F32, BF), sums=[(1, D_MODEL)])
    gw["w_ckv"] = _mm_tn("cross_kv_wgrad", mn, dckv, tk=N_MEM)
    dmn = _mm("cross_kv_bwd", dckv, wfull["w_ckv"], nt=True)
    (dg_mem,) = _rowcall("mem_norm_bwd", lambda a, dn, g: (_rms_bwd(a, g, dn)[1],), N_MEM, N_MEM,
                         [(mem, D_MODEL, 0), (dmn, D_MODEL, 0)], [g_mem], [], accs=[(1, D_MODEL)])

    dya, dyc, dgate_a, dgate_b, dbg_a, dbg_b = _mm(
        "out_proj_bwd", d1b, wfull["w_out"], nt=True, tm=min(512, t), epi=_gate_bwd_epi, out_dtypes=(BF,) * 4,
        extras=(y_attn, y_conv, (z, GATE_A_COL // D_MODEL), (z, GATE_B_COL // D_MODEL)), params=(b_gate,),
        sums=[(1, D_MODEL), (1, D_MODEL)])
    dbg = jnp.concatenate([dbg_a, dbg_b], axis=1)
    gw["w_out"] = _mm_tn("out_proj_wgrad", merged, d1b)
    gw["w_attn_proj"] = _mm_tn("attn_proj_wgrad", attn, dya)
    dattn = _mm("attn_proj_bwd", dya, wfull["w_attn_proj"], nt=True)
    gw["w_conv_proj"] = _mm_tn("conv_proj_wgrad", c2, dyc)
    dc2 = _mm("conv_proj_bwd", dyc, wfull["w_conv_proj"], nt=True)
    (dc1, dlng, dlnb), theirs = _rowcall("conv_ln_silu_bwd", _ln_silu_bwd_fn, t, tr, [(c1, CONV_CH, 0), (dc2, CONV_CH, 0)],
                                         [ln_g, ln_b], [(CONV_CH, F32)], accs=[(1, CONV_CH), (1, CONV_CH)],
                                         side=swap_of(WAVE_MID))
    parts_mid, parts_bf_mid = _pair_sums(WAVE_MID, gw, theirs)
    (dglu_a, dglu_b, dconv), landed = _conv_bwd(z, dc1, w32, t, side=_scatter_side([parts_bf_mlp[n] for n in WAVE_MLP]))
    halves_mlp = _chip_sums(WAVE_MLP, parts_mlp, landed)

    dl0, cls1, cls2 = _attn_bwd_prep(dattn, attn, lse, t)
    dqs, dks, dvs = [], [], []
    for g, (dil, (da_c, dl_c, ls_c)) in enumerate(zip(DILATIONS, ((dattn, dl0, lse), cls1, cls2))):
        res = _attn_bwd(f"attn_bwd_{g}", *qkv_cls[g], da_c, dl_c, ls_c, t, dil,
                        side=_share_side([halves_mlp[n] for n in WAVE_MLP]) if g == 0 else None)
        if g == 0:
            res, others = res
            gshard = {n: _both_halves(halves_mlp[n], o) for n, o in zip(WAVE_MLP, others)}
        dqs.append(res[0])
        dks.append(res[1])
        dvs.append(res[2])
    (dz,), landed = _rope_bwd_join(dqs, dks, dvs, c_tab, s_tab, (dglu_a, dglu_b, dgate_a, dgate_b), t,
                                   side=_scatter_side([parts_bf_mid[n] for n in WAVE_MID]))
    halves_mid = _chip_sums(WAVE_MID, parts_mid, landed)

    gw_in_full, others = _mm_tn("in_proj_wgrad", u, dz, side=_share_side([halves_mid[n] for n in WAVE_MID]))
    gshard.update({n: _both_halves(halves_mid[n], o) for n, o in zip(WAVE_MID, others)})
    gw_in = {"w_in": gw_in_full}
    du_a, theirs = _mm("in_proj_bwd_a", dz, wfull["w_in"], nt=True, k_part=(0, 2), side=_swap_side([gw_in["w_in"]], [True]))
    parts, parts_bf = _pair_sums(("w_in",), gw_in, theirs)
    (gx, dg_mix), landed_in = _mm("in_proj_bwd_b", dz, wfull["w_in"], nt=True, k_part=(1, 2), extras=(du_a, x, d1),
                                  params=(g_mix,), epi=lambda acc, first, a, dres, g: _norm_bwd_epi(acc + first, a, dres, g)[1:],
                                  sums=[(1, D_MODEL)], side=_scatter_side([parts_bf["w_in"]]))
    halves = _chip_sums(("w_in",), parts, landed_in)

    gsmall = {"g_mix": dg_mix, "b_gate": dbg, "conv_b": dconv[CONV_K:CONV_K + 1], "conv_ln_g": dlng, "conv_ln_b": dlnb,
              "g_cross": dg_cross, "g_mem": dg_mem, "g_mlp": dg_mlp, "g_final": dg_final, "conv_w": dconv[:CONV_K]}
    return loss_row, gx, gshard, gsmall, halves["w_in"]


def kernel(x, mem, g_mix, w_in, b_gate, conv_w, conv_b, conv_ln_g, conv_ln_b, w_attn_proj, w_conv_proj, w_out, g_cross, g_mem, w_cq, w_ckv, w_co, g_mlp, w_up, w_down, g_final, loss_target, m_g_mix, m_w_in, m_b_gate, m_conv_w, m_conv_b, m_conv_ln_g, m_conv_ln_b, m_w_attn_proj, m_w_conv_proj, m_w_out, m_g_cross, m_g_mem, m_w_cq, m_w_ckv, m_w_co, m_g_mlp, m_w_up, m_w_down, m_g_final, v_g_mix, v_w_in, v_b_gate, v_conv_w, v_conv_b, v_conv_ln_g, v_conv_ln_b, v_w_attn_proj, v_w_conv_proj, v_w_out, v_g_cross, v_g_mem, v_w_cq, v_w_ckv, v_w_co, v_g_mlp, v_w_up, v_w_down, v_g_final):
    w = dict(g_mix=g_mix, w_in=w_in, b_gate=b_gate, conv_w=conv_w, conv_b=conv_b, conv_ln_g=conv_ln_g, conv_ln_b=conv_ln_b,
             w_attn_proj=w_attn_proj, w_conv_proj=w_conv_proj, w_out=w_out, g_cross=g_cross, g_mem=g_mem, w_cq=w_cq,
             w_ckv=w_ckv, w_co=w_co, g_mlp=g_mlp, w_up=w_up, w_down=w_down, g_final=g_final)
    mo = dict(g_mix=m_g_mix, w_in=m_w_in, b_gate=m_b_gate, conv_w=m_conv_w, conv_b=m_conv_b, conv_ln_g=m_conv_ln_g,
              conv_ln_b=m_conv_ln_b, w_attn_proj=m_w_attn_proj, w_conv_proj=m_w_conv_proj, w_out=m_w_out, g_cross=m_g_cross,
              g_mem=m_g_mem, w_cq=m_w_cq, w_ckv=m_w_ckv, w_co=m_w_co, g_mlp=m_g_mlp, w_up=m_w_up, w_down=m_w_down,
              g_final=m_g_final)
    vo = dict(g_mix=v_g_mix, w_in=v_w_in, b_gate=v_b_gate, conv_w=v_conv_w, conv_b=v_conv_b, conv_ln_g=v_conv_ln_g,
              conv_ln_b=v_conv_ln_b, w_attn_proj=v_w_attn_proj, w_conv_proj=v_w_conv_proj, w_out=v_w_out, g_cross=v_g_cross,
              g_mem=v_g_mem, w_cq=v_w_cq, w_ckv=v_w_ckv, w_co=v_w_co, g_mlp=v_g_mlp, w_up=v_w_up, w_down=v_w_down,
              g_final=v_g_final)
    shapes = {n: w[n].shape for n in ORDER}
    two_d = lambda a: a.reshape(a.shape[-2], a.shape[-1])
    chip = 2 * lax.axis_index("x") + lax.axis_index("y")

    shards = {n: two_d(w[n]).astype(BF) for n in BIG}
    cw_rows = 48
    cw_all, _ = _gather8("gather_conv_w", _pad_rows(conv_w.reshape(-1), cw_rows))
    cw_shard = CONV_K * (CONV_CH // 4)
    conv_w_full = jnp.concatenate(
        [cw_all[2 * j].reshape(-1)[:cw_shard].reshape(CONV_K, CONV_CH // 4) for j in range(4)], axis=1)

    small = {n: w[n] for n in SMALL}
    loss_row, gx, gshard, gsmall, half_w_in = _local_step(two_d(x), two_d(mem), two_d(loss_target), shards, small,
                                                          conv_w_full)
    loss = lax.psum(loss_row[0, 0], ("x", "y", "c"))

    small_names = SMALL + ("conv_w",)
    flat = jnp.concatenate([gsmall[n].reshape(-1) for n in small_names])
    sm_rows = -(-flat.shape[0] // (8 * LANES)) * 8
    (_, sm_sum), others = _gather8("reduce_small_grads", _pad_rows(flat, sm_rows), side=_share_side([half_w_in]))
    gshard["w_in"] = _both_halves(half_w_in, others[0])
    sm_sum = sm_sum.reshape(-1)
    off = 0
    for n in small_names:
        size = gsmall[n].size
        gshard[n] = sm_sum[off:off + size].reshape(gsmall[n].shape)
        off += size
    gshard["conv_w"] = lax.dynamic_slice_in_dim(gshard["conv_w"], chip * (CONV_CH // 4), CONV_CH // 4, axis=1)

    grads, deltas, new_m, new_v = {}, {}, {}, {}
    for n in BIG:
        d, m2, v2 = _adam(f"adamw_{n}", two_d(w[n]), gshard[n], two_d(mo[n]), two_d(vo[n]))
        grads[n], deltas[n], new_m[n], new_v[n] = (a.reshape(shapes[n]) for a in (gshard[n], d, m2, v2))
    pack = lambda src: jnp.concatenate([src[n].reshape(-1) for n in small_names])
    n_small = sum(w[n].size for n in small_names)
    ad_rows = -(-n_small // (8 * LANES)) * 8
    d, m2, v2 = _adam("adamw_small", *[_pad_rows(pack(src), ad_rows) for src in (w, gshard, mo, vo)])
    off = 0
    for n in small_names:
        size = w[n].size
        grads[n] = gshard[n].reshape(shapes[n])
        deltas[n], new_m[n], new_v[n] = (a.reshape(-1)[off:off + size].reshape(shapes[n]) for a in (d, m2, v2))
        off += size

    return (loss, gx.reshape(x.shape), *[grads[n] for n in ORDER], *[deltas[n] for n in ORDER],
            *[new_m[n] for n in ORDER], *[new_v[n] for n in ORDER])
```

```python
import functools

import jax
import jax.numpy as jnp
from jax import lax
from jax.experimental import pallas as pl
from jax.experimental.pallas import tpu as pltpu

F32 = jnp.float32
BF = jnp.bfloat16

D_MODEL = 1024
N_MEM = 256
HEAD_DIM = 128
HEADS_PER_GROUP = 4
DILATIONS = (1, 4, 16)
BLK = 128
GROUP_W = HEADS_PER_GROUP * HEAD_DIM
ATTN_WIDTH = 3 * GROUP_W
ROT_DIM = 32
ROPE_THETA = 500000.0
CONV_CH = 768
CONV_K = 31
CONV_KP = 32
IN_WIDTH = 8192
CROSS_HEADS = 4
CROSS_HEAD_DIM = 256
D_FF = 4096
EPS = 1e-6
ATTN_SCALE = HEAD_DIM ** -0.5
CROSS_SCALE = CROSS_HEAD_DIM ** -0.5
NEG = -1e30

ADAM_LR = 0.001
ADAM_B1 = 0.9
ADAM_B2 = 0.999
ADAM_EPS = 1e-08
ADAM_WD = 0.01
ADAM_STEP = 10

LANES = 128
VMEM_LIMIT = 56 * 1024 * 1024
MESH = pl.DeviceIdType.MESH
ANY = pl.BlockSpec(memory_space=pl.ANY)

GLU_A_COL = 3 * ATTN_WIDTH
GLU_B_COL = GLU_A_COL + CONV_CH
GATE_A_COL = GLU_B_COL + CONV_CH
GATE_B_COL = GATE_A_COL + D_MODEL


def _params(sem=None):
    return pltpu.CompilerParams(dimension_semantics=sem, vmem_limit_bytes=VMEM_LIMIT)


def _dot(a, b):
    return lax.dot_general(a, b, (((1,), (0,)), ((), ())), preferred_element_type=F32)


def _dot_nt(a, b):
    return lax.dot_general(a, b, (((1,), (1,)), ((), ())), preferred_element_type=F32)


def _dot_tn(a, b):
    return lax.dot_general(a, b, (((0,), (0,)), ((), ())), preferred_element_type=F32)


def _sig(x):
    return 1.0 / (1.0 + jnp.exp(-x))


def _glu(a, b):
    return a.astype(F32) * _sig(b.astype(F32))


class _Side:
    def __init__(self, arrays, out_shapes, n_sems, build, aliases=None):
        self.arrays, self.out_shapes, self.n_sems, self.build = list(arrays), list(out_shapes), n_sems, build
        self.aliases = aliases or {}


def _pcall(name, kern, grid, in_specs, out_specs, out_shape, scratch_shapes, sem, args, side=None):
    in_specs, out_specs, out_shape, scratch_shapes = list(in_specs), list(out_specs), list(out_shape), list(scratch_shapes)
    if side is None:
        return pl.pallas_call(kern, name=name, grid=grid, in_specs=in_specs, out_specs=out_specs, out_shape=out_shape,
                              scratch_shapes=scratch_shapes, compiler_params=_params(sem))(*args)
    ni, no, nsc = len(in_specs), len(out_specs), len(scratch_shapes)
    nsi, nso = len(side.arrays), len(side.out_shapes)

    def wrapped(*refs):
        ins, side_ins = refs[:ni], refs[ni:ni + nsi]
        outs, side_outs = refs[ni + nsi:ni + nsi + no], refs[ni + nsi + no:ni + nsi + no + nso]
        scratch = refs[ni + nsi + no + nso:ni + nsi + no + nso + nsc]
        send_sems, recv_sems = refs[-2:]
        start, finish = side.build(side_ins, side_outs, send_sems, recv_sems)
        if grid:
            first = functools.reduce(jnp.logical_and, [pl.program_id(a) == 0 for a in range(len(grid))])
            last = functools.reduce(jnp.logical_and, [pl.program_id(a) == g - 1 for a, g in enumerate(grid)])
            pl.when(first)(start)
            kern(*ins, *outs, *scratch)
            pl.when(last)(finish)
        else:
            start()
            kern(*ins, *outs, *scratch)
            finish()

    res = pl.pallas_call(
        wrapped, name=name, grid=grid, in_specs=in_specs + [ANY] * nsi, out_specs=out_specs + [ANY] * nso,
        out_shape=out_shape + side.out_shapes,
        scratch_shapes=scratch_shapes + [pltpu.SemaphoreType.DMA((side.n_sems,)), pltpu.SemaphoreType.DMA((side.n_sems,))],
        input_output_aliases={ni + k: no + v for k, v in side.aliases.items()},
        compiler_params=_params(("arbitrary",) * len(grid) if grid else None),
    )(*args, *side.arrays)
    return res[:no], res[no:]


def _rowcall(name, fn, n_rows, tile, ins, params, outs, accs=(), side=None):
    tile = min(tile, n_rows)
    ni, npar, no, na = len(ins), len(params), len(outs), len(accs)

    def kern(*refs):
        in_refs = refs[:ni + npar]
        o_refs = refs[ni + npar:ni + npar + no]
        a_refs = refs[ni + npar + no:]
        vals = fn(*[r[...] for r in in_refs])
        for r, v in zip(o_refs, vals[:no]):
            r[...] = v.astype(r.dtype)
        if na:
            @pl.when(pl.program_id(0) == 0)
            def _():
                for r in a_refs:
                    r[...] = jnp.zeros_like(r)
            for r, v in zip(a_refs, vals[no:]):
                r[...] += v

    in_specs = []
    arrays = []
    for spec in ins:
        arr, width, cb = spec[0], spec[1], spec[2]
        rb = spec[3] if len(spec) > 3 else 0
        in_specs.append(pl.BlockSpec((tile, width), functools.partial(lambda i, cb, rb: (i + rb, cb), cb=cb, rb=rb)))
        arrays.append(arr)
    for p in params:
        in_specs.append(pl.BlockSpec(p.shape, lambda i: (0, 0)))
        arrays.append(p)
    out_specs = [pl.BlockSpec((tile, w), lambda i: (i, 0)) for w, _ in outs]
    out_specs += [pl.BlockSpec(s, lambda i: (0, 0)) for s in accs]
    out_shape = [jax.ShapeDtypeStruct((n_rows, w), dt) for w, dt in outs]
    out_shape += [jax.ShapeDtypeStruct(s, F32) for s in accs]
    return _pcall(name, kern, (n_rows // tile,), in_specs, out_specs, out_shape, [],
                  ("arbitrary",) if na else ("parallel",), arrays, side)


def _mm(name, a, w3, *, nt=False, extras=(), params=(), epi=None, out_dtypes=(F32,), sums=(), tm=None, tn=None, tk=None,
        k_part=(0, 1), side=None):
    m, ka = a.shape
    ns, r, cs = w3.shape
    if not nt:
        k_dim, n = r, ns * cs
        tn = tn or min(cs, 1024)
        tk = tk or min(k_dim, 1024)
    else:
        k_dim, n = ns * cs, r
        tn = tn or min(r, 1024)
        tk = tk or min(cs, 1024)
    assert ka == k_dim, (name, a.shape, w3.shape)
    assert not sums or tn == n, name
    nk = k_dim // tk // k_part[1]
    k0 = k_part[0] * nk
    if not nt:
        nbs = cs // tn
        w_spec = pl.BlockSpec((None, tk, tn), lambda i, j, k: (j // nbs, k + k0, j % nbs))
    else:
        kbs = cs // tk
        w_spec = pl.BlockSpec((None, tn, tk), lambda i, j, k: ((k + k0) // kbs, j, (k + k0) % kbs))
    tm = tm or min(m, 1024)
    ne, no, nsum = len(extras) + len(params), len(out_dtypes), len(sums)

    def kern(a_ref, w_ref, *rest):
        e_refs = rest[:ne]
        o_refs = rest[ne:ne + no]
        s_refs = rest[ne + no:ne + no + nsum]

        def part():
            av = a_ref[...].astype(BF)
            return _dot_nt(av, w_ref[...]) if nt else _dot(av, w_ref[...])

        def finish(res):
            vals = epi(res, *[e[...] for e in e_refs]) if epi else (res,)
            for o, v in zip(o_refs, vals[:no]):
                o[...] = v.astype(o.dtype)
            for sr, v in zip(s_refs, vals[no:]):
                sr[...] += v

        if nsum:
            @pl.when(jnp.logical_and(pl.program_id(0) == 0, pl.program_id(2) == 0))
            def _():
                for sr in s_refs:
                    sr[...] = jnp.zeros_like(sr)

        if nk == 1:
            finish(part())
            return
        acc = rest[ne + no + nsum]
        k = pl.program_id(2)

        @pl.when(k == 0)
        def _():
            acc[...] = part()

        @pl.when(jnp.logical_and(k > 0, k < nk - 1))
        def _():
            acc[...] += part()

        @pl.when(k == nk - 1)
        def _():
            finish(acc[...] + part())

    in_specs = [pl.BlockSpec((tm, tk), lambda i, j, k: (i, k + k0)), w_spec]
    split = lambda items: [(it if isinstance(it, tuple) else (it, None)) for it in items]
    extras, params = split(extras), split(params)
    in_specs += [pl.BlockSpec((tm, tn), functools.partial(lambda i, j, k, off: (i, j + off), off=off or 0)) for _, off in extras]
    in_specs += [pl.BlockSpec(p.shape, lambda i, j, k: (0, 0)) if off is None else
                 pl.BlockSpec((p.shape[0], tn), functools.partial(lambda i, j, k, off: (0, j + off), off=off)) for p, off in params]
    extras, params = [e for e, _ in extras], [p for p, _ in params]
    out_specs = [pl.BlockSpec((tm, tn), lambda i, j, k: (i, j)) for _ in out_dtypes]
    out_specs += [pl.BlockSpec(sh, lambda i, j, k: (0, 0)) for sh in sums]
    out_shape = [jax.ShapeDtypeStruct((m, n), dt) for dt in out_dtypes] + [jax.ShapeDtypeStruct(sh, F32) for sh in sums]
    res = _pcall(name, kern, (m // tm, n // tn, nk), in_specs, out_specs, out_shape,
                 [pltpu.VMEM((tm, tn), F32)] if nk > 1 else [],
                 ("arbitrary",) * 3 if nsum else ("parallel", "parallel", "arbitrary"), (a, w3, *extras, *params), side)
    main = res[0] if side is not None else res
    main = main[0] if no + nsum == 1 else main
    return (main, res[1]) if side is not None else main


def _mm_tn(name, a, b, tm=None, tn=None, tk=None, side=None):
    t, ka = a.shape
    _, n = b.shape
    tm = tm or min(ka, 1024)
    tn = tn or min(n, 1024)
    tk = tk or min(t, 1024)

    def kern(a_ref, b_ref, o_ref):
        def part():
            return _dot_tn(a_ref[...].astype(BF), b_ref[...].astype(BF))

        @pl.when(pl.program_id(2) == 0)
        def _():
            o_ref[...] = part()

        @pl.when(pl.program_id(2) > 0)
        def _():
            o_ref[...] += part()

    res = _pcall(name, kern, (ka // tm, n // tn, t // tk),
                 [pl.BlockSpec((tk, tm), lambda i, j, k: (k, i)), pl.BlockSpec((tk, tn), lambda i, j, k: (k, j))],
                 [pl.BlockSpec((tm, tn), lambda i, j, k: (i, j))], [jax.ShapeDtypeStruct((ka, n), F32)], [],
                 ("parallel", "parallel", "arbitrary"), (a, b), side)
    return (res[0][0], res[1]) if side is not None else res[0]


def _rms(x, g):
    return x * lax.rsqrt(jnp.mean(x * x, axis=-1, keepdims=True) + EPS) * g


def _rms_bwd(x, g, dy):
    r = lax.rsqrt(jnp.mean(x * x, axis=-1, keepdims=True) + EPS)
    xh = x * r
    dxh = dy * g
    dx = r * (dxh - xh * jnp.mean(dxh * xh, axis=-1, keepdims=True))
    return dx, jnp.sum(dy * xh, axis=0, keepdims=True)


def _rot(t, c, s):
    lane = lax.broadcasted_iota(jnp.int32, t.shape, 1)
    swapped = jnp.where(lane < ROT_DIM // 2, pltpu.roll(t, HEAD_DIM - ROT_DIM // 2, 1), pltpu.roll(t, ROT_DIM // 2, 1))
    return t * c + swapped * s


def _rope_tables(t):
    half = ROT_DIM // 2
    pos = jnp.arange(t, dtype=F32)
    inv_freq = ROPE_THETA ** (-jnp.arange(0, ROT_DIM, 2, dtype=F32) / ROT_DIM)
    ang = pos[:, None] * inv_freq[None, :]
    cos, sin = jnp.cos(ang), jnp.sin(ang)
    ones = jnp.ones((t, HEAD_DIM - ROT_DIM), F32)
    c_tab = jnp.concatenate([cos, cos, ones], axis=1)
    s_tab = jnp.concatenate([-sin, sin, 0.0 * ones], axis=1)
    return c_tab, s_tab


def _merge_fn(o0, o1, o2, l0, l1, l2):
    m = jnp.maximum(jnp.maximum(l0, l1), l2)
    e0, e1, e2 = jnp.exp(l0 - m), jnp.exp(l1 - m), jnp.exp(l2 - m)
    tot = e0 + e1 + e2
    return (e0 * o0 + e1 * o1 + e2 * o2) / tot, m + jnp.log(tot)


def _ln_parts(c1):
    mu = jnp.mean(c1, axis=-1, keepdims=True)
    xc = c1 - mu
    r = lax.rsqrt(jnp.mean(xc * xc, axis=-1, keepdims=True) + EPS)
    return xc * r, r


def _ln_silu_fn(c1, g, b):
    xh, _ = _ln_parts(c1)
    yl = xh * g + b
    return (yl * _sig(yl),)


def _ln_silu_bwd_fn(c1, dout, g, b):
    xh, r = _ln_parts(c1)
    yl = xh * g + b
    s = _sig(yl)
    dyl = dout * (s + yl * s * (1.0 - s))
    dxh = dyl * g
    dx = r * (dxh - jnp.mean(dxh, axis=-1, keepdims=True) - xh * jnp.mean(dxh * xh, axis=-1, keepdims=True))
    return dx, jnp.sum(dyl * xh, axis=0, keepdims=True), jnp.sum(dyl, axis=0, keepdims=True)


def _gate_epi(yc, ya, ga, gb, ba, bb):
    return _sig(ga + ba) * ya + _sig(gb + bb) * yc, yc


def _gate_bwd_epi(dm, ya, yc, ga, gb, bg):
    sa = _sig(ga + bg[:, :D_MODEL])
    sb = _sig(gb + bg[:, D_MODEL:])
    dga = dm * ya * sa * (1.0 - sa)
    dgb = dm * yc * sb * (1.0 - sb)
    return dm * sa, dm * sb, dga, dgb, jnp.sum(dga, axis=0, keepdims=True), jnp.sum(dgb, axis=0, keepdims=True)


def _res_norm_epi(acc, res, g):
    xn = res + acc
    return xn, _rms(xn, g)


def _norm_bwd_epi(acc, a, dres, g):
    dx, dg = _rms_bwd(a, g, acc)
    return dres + dx, dres + dx, dg


def _final_epi(acc, res, tgt, g):
    return _final_fn(res + acc, tgt, g)


def _final_fn(x3, tgt, g):
    err = _rms(x3, g) - tgt
    lrow = jnp.sum(err * err, axis=-1, keepdims=True) * (0.5 / D_MODEL)
    lsum = jnp.sum(lrow, axis=0, keepdims=True)
    dx, dg = _rms_bwd(x3, g, err * (1.0 / D_MODEL))
    return dx, dx, jnp.broadcast_to(lsum, (1, LANES)), dg


def _attn_geometry(t, dil):
    cls = t // dil
    rows = min(4 * BLK, cls)
    return rows, rows // BLK, cls // rows


def _head_lanes(h):
    return slice(h * HEAD_DIM, (h + 1) * HEAD_DIM)


def _band_mask():
    row = lax.broadcasted_iota(jnp.int32, (BLK, 2 * BLK), 0)
    col = lax.broadcasted_iota(jnp.int32, (BLK, 2 * BLK), 1)
    return jnp.logical_and(col >= row, col <= row + BLK), col


def _stage_window(scr, halo_ref, cur_ref):
    scr[:BLK, :] = halo_ref[...]
    scr[BLK:, :] = cur_ref[...]


def _attn_fwd(name, q, k, v, t, dil):
    rows, nbk, spc = _attn_geometry(t, dil)

    def kern(q_ref, k_ref, kh_ref, v_ref, vh_ref, o_ref, l_ref, k_scr, v_scr):
        i = pl.program_id(0)
        first_shift = jnp.where(i % spc == 0, BLK, 0)
        _stage_window(k_scr, kh_ref, k_ref)
        _stage_window(v_scr, vh_ref, v_ref)
        band, col = _band_mask()
        band_first = jnp.logical_and(band, col >= first_shift)
        for h in range(HEADS_PER_GROUP):
            hs = _head_lanes(h)
            for b in range(nbk):
                rs, win = slice(b * BLK, (b + 1) * BLK), slice(b * BLK, (b + 2) * BLK)
                s = jnp.where(band_first if b == 0 else band, _dot_nt(q_ref[rs, hs], k_scr[win, hs]) * ATTN_SCALE, NEG)
                m = jnp.max(s, axis=1, keepdims=True)
                p = jnp.exp(s - m)
                tot = jnp.sum(p, axis=1, keepdims=True)
                o_ref[rs, hs] = _dot(p.astype(BF), v_scr[win, hs]) / tot
                l_ref[rs, hs] = jnp.broadcast_to(m + jnp.log(tot), (BLK, HEAD_DIM))

    def cur(cb):
        return pl.BlockSpec((rows, GROUP_W), lambda i: (i, cb))

    def halo(cb):
        return pl.BlockSpec((BLK, GROUP_W), lambda i: (jnp.maximum(i * nbk - 1, 0), cb))

    (qa, qc), (ka, kc_), (va, vc_) = q, k, v
    return _pcall(name, kern, (t // rows,), [cur(qc), cur(kc_), halo(kc_), cur(vc_), halo(vc_)],
                  [pl.BlockSpec((rows, GROUP_W), lambda i: (i, 0))] * 2, [jax.ShapeDtypeStruct((t, GROUP_W), F32)] * 2,
                  [pltpu.VMEM((rows + BLK, GROUP_W), BF)] * 2, ("parallel",), (qa, ka, ka, va, va))


def _attn_bwd(name, q, k, v, da, dl, lse, t, dil, side=None):
    rows, nbk, spc = _attn_geometry(t, dil)
    nblk = t // BLK

    def kern(q_ref, qn_ref, k_ref, kh_ref, v_ref, vh_ref, da_ref, dan_ref, dl_ref, dln_ref, ls_ref, lsn_ref,
             dq_ref, dk_ref, dv_ref, k_scr, v_scr):
        i = pl.program_id(0)
        first_shift = jnp.where(i % spc == 0, BLK, 0)
        next_shift = jnp.where((i + 1) % spc == 0, BLK, 0)
        _stage_window(k_scr, kh_ref, k_ref)
        _stage_window(v_scr, vh_ref, v_ref)
        band, col = _band_mask()
        band_first = jnp.logical_and(band, col >= first_shift)
        row1 = lax.broadcasted_iota(jnp.int32, (BLK, BLK), 0)
        col1 = lax.broadcasted_iota(jnp.int32, (BLK, BLK), 1)
        pend_k, pend_v = [None] * HEADS_PER_GROUP, [None] * HEADS_PER_GROUP
        for b in range(nbk):
            rs, win = slice(b * BLK, (b + 1) * BLK), slice(b * BLK, (b + 2) * BLK)
            for h in range(HEADS_PER_GROUP):
                hs = _head_lanes(h)
                qb, dab = q_ref[rs, hs], da_ref[rs, hs].astype(BF)
                kw, vw = k_scr[win, hs], v_scr[win, hs]
                p = jnp.where(band_first if b == 0 else band,
                              jnp.exp(_dot_nt(qb, kw) * ATTN_SCALE - ls_ref[rs, hs][:, :1]), 0.0)
                ds = (p * (_dot_nt(dab, vw) - dl_ref[rs, hs][:, :1]) * ATTN_SCALE).astype(BF)
                dq_ref[rs, hs] = _dot(ds, kw).astype(BF)
                dkw, dvw = _dot_tn(ds, qb), _dot_tn(p.astype(BF), dab)
                if b >= 1:
                    ps = slice((b - 1) * BLK, b * BLK)
                    dk_ref[ps, hs] = (pend_k[h] + dkw[:BLK]).astype(BF)
                    dv_ref[ps, hs] = (pend_v[h] + dvw[:BLK]).astype(BF)
                pend_k[h], pend_v[h] = dkw[BLK:], dvw[BLK:]
        ls_rows = slice((nbk - 1) * BLK, nbk * BLK)
        last = slice(nbk * BLK, (nbk + 1) * BLK)
        for h in range(HEADS_PER_GROUP):
            hs = _head_lanes(h)
            qb, dab = qn_ref[:, hs], dan_ref[:, hs].astype(BF)
            kp, vp = k_scr[last, hs], v_scr[last, hs]
            p = jnp.where(col1 >= row1 + next_shift, jnp.exp(_dot_nt(qb, kp) * ATTN_SCALE - lsn_ref[:, hs]), 0.0)
            ds = (p * (_dot_nt(dab, vp) - dln_ref[:, hs]) * ATTN_SCALE).astype(BF)
            dk_ref[ls_rows, hs] = (pend_k[h] + _dot_tn(ds, qb)).astype(BF)
            dv_ref[ls_rows, hs] = (pend_v[h] + _dot_tn(p.astype(BF), dab)).astype(BF)

    def cur(cb):
        return pl.BlockSpec((rows, GROUP_W), lambda i: (i, cb))

    def prev(cb):
        return pl.BlockSpec((BLK, GROUP_W), lambda i: (jnp.maximum(i * nbk - 1, 0), cb))

    def nxt(cb):
        return pl.BlockSpec((BLK, GROUP_W), lambda i: (jnp.minimum((i + 1) * nbk, nblk - 1), cb))

    (qa, qc), (ka, kc_), (va, vc_) = q, k, v
    return _pcall(name, kern, (t // rows,),
                  [cur(qc), nxt(qc), cur(kc_), prev(kc_), cur(vc_), prev(vc_), cur(0), nxt(0), cur(0), nxt(0), cur(0), nxt(0)],
                  [pl.BlockSpec((rows, GROUP_W), lambda i: (i, 0))] * 3, [jax.ShapeDtypeStruct((t, GROUP_W), BF)] * 3,
                  [pltpu.VMEM((rows + BLK, GROUP_W), BF)] * 2, ("parallel",),
                  (qa, qa, ka, ka, va, va, da, da, dl, dl, lse, lse), side)


CLS_TILE = 512


def _cls_block(t, tile, dil, dtype):
    if dil == 1:
        return pl.BlockSpec((tile, GROUP_W), lambda i: (i, 0)), jax.ShapeDtypeStruct((t, GROUP_W), dtype)
    return (pl.BlockSpec((dil, tile // dil, GROUP_W), lambda i: (0, i, 0)),
            jax.ShapeDtypeStruct((dil, t // dil, GROUP_W), dtype))


def _head_scratch(tile):
    return pltpu.VMEM((tile, HEAD_DIM), F32)


def _rope_split(z, c_tab, s_tab, t):
    tile = min(CLS_TILE, t)
    n_heads = ATTN_WIDTH // HEAD_DIM

    def kern(zq_ref, zk_ref, zv_ref, c_ref, s_ref, *rest):
        outs, scr = rest[:9], rest[9]
        c, s = c_ref[...], s_ref[...]
        for which, z_ref in enumerate((zq_ref, zk_ref, zv_ref)):
            for h in range(n_heads):
                g, hs = h // HEADS_PER_GROUP, _head_lanes(h % HEADS_PER_GROUP)
                val = z_ref[:, h * HEAD_DIM:(h + 1) * HEAD_DIM].astype(F32)
                if which < 2:
                    val = _rot(val, c, s)
                if g == 0:
                    outs[which][:, hs] = val.astype(BF)
                    continue
                scr[...] = val
                dil = DILATIONS[g]
                for r in range(dil):
                    outs[3 * g + which][r, :, hs] = scr[pl.ds(r, tile // dil, stride=dil), :].astype(BF)

    blocks = [_cls_block(t, tile, DILATIONS[g], BF) for g in range(3) for _ in range(3)]
    zspec = lambda cb: pl.BlockSpec((tile, ATTN_WIDTH), lambda i: (i, cb))
    tab = pl.BlockSpec((tile, HEAD_DIM), lambda i: (i, 0))
    return _pcall("rope", kern, (t // tile,), [zspec(0), zspec(1), zspec(2), tab, tab], [b[0] for b in blocks],
                  [b[1] for b in blocks], [_head_scratch(tile)], ("parallel",), (z, z, z, c_tab, s_tab))


def _merge_classes(outs, lses, t):
    tile = min(CLS_TILE, t)

    def kern(o0, l0, o1, l1, o2, l2, attn_ref, lse_ref, s_o1, s_l1, s_o2, s_l2):
        for h in range(HEADS_PER_GROUP):
            hs = _head_lanes(h)
            for src, dst, dil in ((o1, s_o1, DILATIONS[1]), (l1, s_l1, DILATIONS[1]), (o2, s_o2, DILATIONS[2]),
                                  (l2, s_l2, DILATIONS[2])):
                for r in range(dil):
                    dst[pl.ds(r, tile // dil, stride=dil), :] = src[r, :, hs]
            attn_ref[:, hs], lse_ref[:, hs] = _merge_fn(o0[:, hs], s_o1[...], s_o2[...], l0[:, hs], s_l1[...], s_l2[...])

    blocks = [_cls_block(t, tile, DILATIONS[g], F32) for g in range(3)]
    args = []
    for g in range(3):
        args += [outs[g].reshape(blocks[g][1].shape), lses[g].reshape(blocks[g][1].shape)]
    tok = pl.BlockSpec((tile, GROUP_W), lambda i: (i, 0))
    return _pcall("attn_merge", kern, (t // tile,), [blocks[g][0] for g in range(3) for _ in range(2)], [tok, tok],
                  [jax.ShapeDtypeStruct((t, GROUP_W), F32)] * 2, [_head_scratch(tile)] * 4, ("parallel",), args)


def _attn_bwd_prep(dattn, attn, lse, t):
    tile = min(CLS_TILE, t)

    def kern(da_ref, at_ref, ls_ref, dl0, da1, dl1, ls1, da2, dl2, ls2, s_da, s_dl, s_ls):
        for h in range(HEADS_PER_GROUP):
            hs = _head_lanes(h)
            da = da_ref[:, hs]
            s_da[...] = da
            s_dl[...] = jnp.broadcast_to(jnp.sum(da * at_ref[:, hs], axis=1, keepdims=True), (tile, HEAD_DIM))
            s_ls[...] = ls_ref[:, hs]
            dl0[:, hs] = s_dl[...]
            for oda, odl, ols, dil in ((da1, dl1, ls1, DILATIONS[1]), (da2, dl2, ls2, DILATIONS[2])):
                for r in range(dil):
                    rows = pl.ds(r, tile // dil, stride=dil)
                    oda[r, :, hs] = s_da[rows, :].astype(BF)
                    odl[r, :, hs] = s_dl[rows, :]
                    ols[r, :, hs] = s_ls[rows, :]

    tok = pl.BlockSpec((tile, GROUP_W), lambda i: (i, 0))
    blocks = [(tok, jax.ShapeDtypeStruct((t, GROUP_W), F32))]
    for g in (1, 2):
        blocks += [_cls_block(t, tile, DILATIONS[g], BF), _cls_block(t, tile, DILATIONS[g], F32),
                   _cls_block(t, tile, DILATIONS[g], F32)]
    res = _pcall("attn_bwd_prep", kern, (t // tile,), [tok, tok, tok], [b[0] for b in blocks], [b[1] for b in blocks],
                 [_head_scratch(tile)] * 3, ("parallel",), (dattn, attn, lse))
    flat = [a.reshape(t, GROUP_W) for a in res]
    return flat[0], flat[1:4], flat[4:7]


def _rope_bwd_join(dqs, dks, dvs, c_tab, s_tab, tail, t, side=None):
    tile = min(256, t)
    n_heads = ATTN_WIDTH // HEAD_DIM

    def kern(q0, q1, q2, k0, k1, k2, v0, v1, v2, c_ref, s_ref, ga_ref, gb_ref, gta_ref, gtb_ref, dz_ref, scr):
        c, s = c_ref[...], -s_ref[...]
        for which, srcs in enumerate(((q0, q1, q2), (k0, k1, k2), (v0, v1, v2))):
            for h in range(n_heads):
                g, hs = h // HEADS_PER_GROUP, _head_lanes(h % HEADS_PER_GROUP)
                if g == 0:
                    val = srcs[0][:, hs].astype(F32)
                else:
                    dil = DILATIONS[g]
                    for r in range(dil):
                        scr[pl.ds(r, tile // dil, stride=dil), :] = srcs[g][r, :, hs].astype(F32)
                    val = scr[...]
                if which < 2:
                    val = _rot(val, c, s)
                col = which * ATTN_WIDTH + h * HEAD_DIM
                dz_ref[:, col:col + HEAD_DIM] = val.astype(BF)
        dz_ref[:, GLU_A_COL:GLU_B_COL] = ga_ref[...]
        dz_ref[:, GLU_B_COL:GATE_A_COL] = gb_ref[...]
        dz_ref[:, GATE_A_COL:GATE_B_COL] = gta_ref[...]
        dz_ref[:, GATE_B_COL:] = gtb_ref[...]

    blocks = [_cls_block(t, tile, DILATIONS[g], BF) for g in range(3)]
    args = [a.reshape(blocks[g][1].shape) for grp in (dqs, dks, dvs) for g, a in enumerate(grp)]
    tab = pl.BlockSpec((tile, HEAD_DIM), lambda i: (i, 0))
    row = lambda w: pl.BlockSpec((tile, w), lambda i: (i, 0))
    return _pcall("rope_bwd", kern, (t // tile,), [blocks[g][0] for _ in range(3) for g in range(3)]
                  + [tab, tab, row(CONV_CH), row(CONV_CH), row(D_MODEL), row(D_MODEL)], [row(IN_WIDTH)],
                  [jax.ShapeDtypeStruct((t, IN_WIDTH), BF)], [_head_scratch(tile)], ("parallel",),
                  (*args, c_tab, s_tab, *tail), side)


def _cross_probs(qh, kh):
    s = _dot_nt(qh, kh) * CROSS_SCALE
    e = jnp.exp(s - jnp.max(s, axis=1, keepdims=True))
    return e, jnp.sum(e, axis=1, keepdims=True)


def _cross_fwd(cq, ckv, t):
    rows = min(512, t)

    def kern(q_ref, kv_ref, o_ref):
        for h in range(CROSS_HEADS):
            hs = slice(h * CROSS_HEAD_DIM, (h + 1) * CROSS_HEAD_DIM)
            vs = slice(D_MODEL + h * CROSS_HEAD_DIM, D_MODEL + (h + 1) * CROSS_HEAD_DIM)
            e, tot = _cross_probs(q_ref[:, hs], kv_ref[:, hs])
            o_ref[:, hs] = (_dot(e.astype(BF), kv_ref[:, vs]) / tot).astype(BF)

    return pl.pallas_call(
        kern, name="cross_fwd", grid=(t // rows,),
        in_specs=[pl.BlockSpec((rows, D_MODEL), lambda i: (i, 0)), pl.BlockSpec((N_MEM, 2 * D_MODEL), lambda i: (0, 0))],
        out_specs=pl.BlockSpec((rows, D_MODEL), lambda i: (i, 0)),
        out_shape=jax.ShapeDtypeStruct((t, D_MODEL), BF),
        compiler_params=_params(("parallel",)),
    )(cq, ckv)


def _cross_bwd(cq, ckv, dco, t):
    rows = min(512, t)

    def kern(q_ref, kv_ref, do_ref, dq_ref, dkv_ref):
        @pl.when(pl.program_id(0) == 0)
        def _():
            dkv_ref[...] = jnp.zeros_like(dkv_ref)
        for h in range(CROSS_HEADS):
            hs = slice(h * CROSS_HEAD_DIM, (h + 1) * CROSS_HEAD_DIM)
            vs = slice(D_MODEL + h * CROSS_HEAD_DIM, D_MODEL + (h + 1) * CROSS_HEAD_DIM)
            qh, kh, vh, doh = q_ref[:, hs], kv_ref[:, hs], kv_ref[:, vs], do_ref[:, hs]
            e, tot = _cross_probs(qh, kh)
            p = e / tot
            dp = _dot_nt(doh, vh)
            ds = (p * (dp - jnp.sum(p * dp, axis=1, keepdims=True)) * CROSS_SCALE).astype(BF)
            dq_ref[:, hs] = _dot(ds, kh).astype(BF)
            dkv_ref[:, hs] += _dot_tn(ds, qh)
            dkv_ref[:, vs] += _dot_tn(p.astype(BF), doh)

    return pl.pallas_call(
        kern, name="cross_bwd", grid=(t // rows,),
        in_specs=[pl.BlockSpec((rows, D_MODEL), lambda i: (i, 0)), pl.BlockSpec((N_MEM, 2 * D_MODEL), lambda i: (0, 0)),
                  pl.BlockSpec((rows, D_MODEL), lambda i: (i, 0))],
        out_specs=[pl.BlockSpec((rows, D_MODEL), lambda i: (i, 0)), pl.BlockSpec((N_MEM, 2 * D_MODEL), lambda i: (0, 0))],
        out_shape=[jax.ShapeDtypeStruct((t, D_MODEL), BF), jax.ShapeDtypeStruct((N_MEM, 2 * D_MODEL), F32)],
        compiler_params=_params(("arbitrary",)),
    )(cq, ckv, dco)


CONV_TILE = 512
CONV_CHUNK = 128
HALO = 32


def _conv_fwd(z, w32, bias, t):
    tile = min(CONV_TILE, t)
    a_cb, b_cb = GLU_A_COL // LANES, GLU_B_COL // LANES
    hb = tile // HALO

    def kern(a_ref, b_ref, ah_ref, bh_ref, w_ref, bias_ref, o_ref, g_scr):
        i = pl.program_id(1)
        g_scr[HALO:, :] = _glu(a_ref[...], b_ref[...])
        g_scr[:HALO, :] = _glu(ah_ref[...], bh_ref[...]) * jnp.where(i > 0, 1.0, 0.0)
        for c in range(tile // CONV_CHUNK):
            acc = jnp.broadcast_to(bias_ref[...], (CONV_CHUNK, LANES))
            for j in range(CONV_K):
                lo = c * CONV_CHUNK + HALO - (CONV_K - 1) + j
                acc = acc + w_ref[j:j + 1, :] * g_scr[lo:lo + CONV_CHUNK, :]
            o_ref[c * CONV_CHUNK:(c + 1) * CONV_CHUNK, :] = acc

    def cur(cb):
        return pl.BlockSpec((tile, LANES), lambda j, i: (i, cb + j))

    def prev(cb):
        return pl.BlockSpec((HALO, LANES), lambda j, i: (jnp.maximum(i * hb - 1, 0), cb + j))

    return pl.pallas_call(
        kern, name="conv_fwd", grid=(CONV_CH // LANES, t // tile),
        in_specs=[cur(a_cb), cur(b_cb), prev(a_cb), prev(b_cb),
                  pl.BlockSpec((CONV_KP, LANES), lambda j, i: (0, j)), pl.BlockSpec((1, LANES), lambda j, i: (0, j))],
        out_specs=pl.BlockSpec((tile, LANES), lambda j, i: (i, j)),
        out_shape=jax.ShapeDtypeStruct((t, CONV_CH), F32),
        scratch_shapes=[pltpu.VMEM((tile + HALO, LANES), F32)],
        compiler_params=_params(("parallel", "parallel")),
    )(z, z, z, z, w32, bias)


def _conv_bwd(z, dc1, w32, t, side=None):
    tile = min(CONV_TILE, t)
    a_cb, b_cb = GLU_A_COL // LANES, GLU_B_COL // LANES
    hb = tile // HALO
    n_tiles = t // tile
    n_chunks = tile // CONV_CHUNK

    def kern(a_ref, b_ref, ah_ref, bh_ref, d_ref, dn_ref, w_ref, da_ref, db_ref, dw_ref, g_scr, d_scr):
        i = pl.program_id(1)
        g_scr[HALO:, :] = _glu(a_ref[...], b_ref[...])
        g_scr[:HALO, :] = _glu(ah_ref[...], bh_ref[...]) * jnp.where(i > 0, 1.0, 0.0)
        d_scr[:tile, :] = d_ref[...]
        d_scr[tile:, :] = dn_ref[...] * jnp.where(i < n_tiles - 1, 1.0, 0.0)

        @pl.when(i == 0)
        def _():
            dw_ref[...] = jnp.zeros_like(dw_ref)

        for c in range(n_chunks):
            cs = slice(c * CONV_CHUNK, (c + 1) * CONV_CHUNK)
            acc = jnp.zeros((CONV_CHUNK, LANES), F32)
            for j in range(CONV_K):
                lo = c * CONV_CHUNK + (CONV_K - 1) - j
                acc = acc + w_ref[j:j + 1, :] * d_scr[lo:lo + CONV_CHUNK, :]
            sgc = _sig(b_ref[cs, :].astype(F32))
            da_ref[cs, :] = (acc * sgc).astype(BF)
            db_ref[cs, :] = (acc * a_ref[cs, :].astype(F32) * sgc * (1.0 - sgc)).astype(BF)
        for j in range(CONV_K):
            tot = jnp.zeros((1, LANES), F32)
            for c in range(n_chunks):
                lo = c * CONV_CHUNK + HALO - (CONV_K - 1) + j
                tot = tot + jnp.sum(d_ref[c * CONV_CHUNK:(c + 1) * CONV_CHUNK, :] * g_scr[lo:lo + CONV_CHUNK, :],
                                    axis=0, keepdims=True)
            dw_ref[j:j + 1, :] += tot
        dw_ref[CONV_K:CONV_KP, :] += jnp.sum(d_ref[...], axis=0, keepdims=True)

    def cur(cb):
        return pl.BlockSpec((tile, LANES), lambda j, i: (i, cb + j))

    def prev(cb):
        return pl.BlockSpec((HALO, LANES), lambda j, i: (jnp.maximum(i * hb - 1, 0), cb + j))

    return _pcall(
        "conv_bwd", kern, (CONV_CH // LANES, n_tiles),
        [cur(a_cb), cur(b_cb), prev(a_cb), prev(b_cb), cur(0),
         pl.BlockSpec((HALO, LANES), lambda j, i: (jnp.minimum((i + 1) * hb, t // HALO - 1), j)),
         pl.BlockSpec((CONV_KP, LANES), lambda j, i: (0, j))],
        [pl.BlockSpec((tile, LANES), lambda j, i: (i, j)), pl.BlockSpec((tile, LANES), lambda j, i: (i, j)),
         pl.BlockSpec((CONV_KP, LANES), lambda j, i: (0, j))],
        [jax.ShapeDtypeStruct((t, CONV_CH), BF), jax.ShapeDtypeStruct((t, CONV_CH), BF),
         jax.ShapeDtypeStruct((CONV_KP, CONV_CH), F32)],
        [pltpu.VMEM((tile + HALO, LANES), F32), pltpu.VMEM((tile + HALO, LANES), F32)],
        ("parallel", "arbitrary"), (z, z, z, z, dc1, dc1, w32), side)


def _adam_fn(w, g, m, v):
    m = ADAM_B1 * m + (1.0 - ADAM_B1) * g
    v = ADAM_B2 * v + (1.0 - ADAM_B2) * (g * g)
    m_hat = m / (1.0 - ADAM_B1 ** ADAM_STEP)
    v_hat = v / (1.0 - ADAM_B2 ** ADAM_STEP)
    delta = -ADAM_LR * (m_hat / (jnp.sqrt(v_hat) + ADAM_EPS) + ADAM_WD * w)
    return delta, m, v


def _adam(name, w, g, m, v):
    rows, cols = w.shape
    tile = _ew_tile(rows, cols)
    return _rowcall(name, _adam_fn, rows, tile, [(a, cols, 0) for a in (w, g, m, v)], [], [(cols, F32)] * 3)


def _place():
    x, y, c = lax.axis_index("x"), lax.axis_index("y"), lax.axis_index("c")
    chips = [(1 - x, y), (x, 1 - y), (1 - x, 1 - y)]
    return x, y, c, chips


def _gather_side(shards):
    nw = len(shards)
    chip = 2 * lax.axis_index("x") + lax.axis_index("y")
    staged = [lax.dynamic_update_index_in_dim(jnp.zeros((4,) + s.shape, s.dtype), s, chip, 0) for s in shards]

    def build(_, outs, send_sems, recv_sems):
        x, y, c, chips = _place()
        me = 2 * x + y
        sibling = (x, y, 1 - c)

        def half(w, lead, h):
            n = shards[w].shape[0] // 2
            return outs[w].at[lead, pl.ds(h * n, n)]

        def copy(w, k, part, to):
            return pltpu.make_async_remote_copy(src_ref=part, dst_ref=part, send_sem=send_sems.at[6 * w + k],
                                                recv_sem=recv_sems.at[6 * w + k], device_id=to, device_id_type=MESH)

        def start():
            for w in range(nw):
                for k, (px, py) in enumerate(chips):
                    copy(w, k, half(w, me, c), (px, py, c)).start()

        def finish():
            for w in range(nw):
                for k, (px, py) in enumerate(chips):
                    landed = half(w, 2 * px + py, c)
                    copy(w, k, landed, (px, py, c)).wait_recv()
                    copy(w, 3 + k, landed, sibling).start()
            for w in range(nw):
                for k, (px, py) in enumerate(chips):
                    copy(w, 3 + k, half(w, 2 * px + py, 1 - c), sibling).wait_recv()
            for w in range(nw):
                for k, (px, py) in enumerate(chips):
                    copy(w, k, half(w, me, c), (px, py, c)).wait_send()
                    copy(w, 3 + k, half(w, 2 * px + py, c), sibling).wait_send()

        return start, finish

    return _Side(staged, [jax.ShapeDtypeStruct((4,) + s.shape, s.dtype) for s in shards], 6 * nw, build,
                 aliases={w: w for w in range(nw)})


def _gather8(name, v, side=None):
    rows = v.shape[0]

    def body(v_ref, all_ref, sum_ref, send_sems, recv_sems):
        x, y, c, _ = _place()
        me = 4 * x + 2 * y + c
        all_ref[me] = v_ref[...]
        copies = []
        for k in range(1, 8):
            px, py, pc = x ^ (k >> 2), y ^ ((k >> 1) & 1), c ^ (k & 1)
            copies.append(pltpu.make_async_remote_copy(
                src_ref=v_ref, dst_ref=all_ref.at[me], send_sem=send_sems.at[k - 1], recv_sem=recv_sems.at[k - 1],
                device_id=(px, py, pc), device_id_type=MESH))
            copies[-1].start()
        for k in range(1, 8):
            px, py, pc = x ^ (k >> 2), y ^ ((k >> 1) & 1), c ^ (k & 1)
            theirs = all_ref.at[4 * px + 2 * py + pc]
            pltpu.make_async_remote_copy(
                src_ref=theirs, dst_ref=theirs, send_sem=send_sems.at[k - 1], recv_sem=recv_sems.at[k - 1],
                device_id=(px, py, pc), device_id_type=MESH).wait_recv()
        for cp in copies:
            cp.wait_send()
        tot = all_ref[0]
        for d in range(1, 8):
            tot = tot + all_ref[d]
        sum_ref[...] = tot

    vm = pl.BlockSpec(memory_space=pltpu.VMEM)
    return _pcall(name, body, (), [vm], [vm, vm],
                  [jax.ShapeDtypeStruct((8, rows, LANES), F32), jax.ShapeDtypeStruct((rows, LANES), F32)],
                  [pltpu.SemaphoreType.DMA((7,)), pltpu.SemaphoreType.DMA((7,))], None, (v,), side)


def _region(ref, col_sharded, shape, j, h):
    r, ccols = shape
    if col_sharded:
        return ref.at[pl.ds(h * (r // 2), r // 2), pl.ds(j * (ccols // 4), ccols // 4)]
    n = r // 8
    return ref.at[pl.ds((2 * j + h) * n, n), :]


def _region_shape(col_sharded, shape):
    r, ccols = shape
    return (r // 2, ccols // 4) if col_sharded else (r // 8, ccols)


def _exchange(copies):
    def build(ins, outs, send_sems, recv_sems):
        def start():
            for cp in copies(ins, outs, send_sems, recv_sems):
                cp.start()

        def finish():
            for cp in copies(ins, outs, send_sems, recv_sems):
                cp.wait()

        return start, finish
    return build


def _swap_side(grads, kinds):
    nw = len(grads)

    def copies(ins, theirs, send_sems, recv_sems):
        x, y, c, _ = _place()
        return [pltpu.make_async_remote_copy(
            src_ref=_region(ins[w], kinds[w], grads[w].shape, j, 1 - c), dst_ref=theirs[w].at[j],
            send_sem=send_sems.at[4 * w + j], recv_sem=recv_sems.at[4 * w + j], device_id=(x, y, 1 - c), device_id_type=MESH)
            for w in range(nw) for j in range(4)]

    shapes = [jax.ShapeDtypeStruct((4,) + _region_shape(kinds[w], grads[w].shape), F32) for w in range(nw)]
    return _Side(grads, shapes, 4 * nw, _exchange(copies))


def _scatter_side(parts):
    nw = len(parts)

    def copies(ins, outs, send_sems, recv_sems):
        x, y, c, chips = _place()
        return [pltpu.make_async_remote_copy(
            src_ref=ins[w].at[2 * px + py], dst_ref=outs[w].at[k], send_sem=send_sems.at[3 * w + k],
            recv_sem=recv_sems.at[3 * w + k], device_id=(px, py, c), device_id_type=MESH)
            for w in range(nw) for k, (px, py) in enumerate(chips)]

    shapes = [jax.ShapeDtypeStruct((3,) + p.shape[1:], p.dtype) for p in parts]
    return _Side(parts, shapes, 3 * nw, _exchange(copies))


def _share_side(halves):
    nw = len(halves)

    def copies(ins, outs, send_sems, recv_sems):
        x, y, c, _ = _place()
        return [pltpu.make_async_remote_copy(
            src_ref=ins[w], dst_ref=outs[w].at[c], send_sem=send_sems.at[w], recv_sem=recv_sems.at[w],
            device_id=(x, y, 1 - c), device_id_type=MESH) for w in range(nw)]

    return _Side(halves, [jax.ShapeDtypeStruct((2,) + h.shape, F32) for h in halves], nw, _exchange(copies))


EW_BLOCK = 512 * 1024


def _ew_tile(rows, cols):
    limit = max(8, EW_BLOCK // cols)
    return max(d for d in range(8, min(rows, limit) + 1, 8) if rows % d == 0)


def _indexed_sum(name, fn, grid, in_specs, out_specs, out_shape, index, arrays):
    def kern(_, *refs):
        n_in = len(in_specs)
        vals = fn(*[r[...] for r in refs[:n_in]])
        for r, v in zip(refs[n_in:], vals):
            r[...] = v.astype(r.dtype)

    return pl.pallas_call(
        kern, name=name, out_shape=out_shape,
        grid_spec=pltpu.PrefetchScalarGridSpec(num_scalar_prefetch=1, grid=grid, in_specs=in_specs, out_specs=out_specs),
        compiler_params=_params(("parallel",) * len(grid)),
    )(index.astype(jnp.int32).reshape(1), *arrays)


def _pair_sums(names, grads, theirs):
    parts, parts_bf = {}, {}
    for n, other in zip(names, theirs):
        _, rr, cc = other.shape
        tile = _ew_tile(rr, cc)
        nb = rr // tile
        if COL_SHARDED[n]:
            mine = pl.BlockSpec((tile, cc), lambda j, i, c: (c[0] * nb + i, j))
        else:
            mine = pl.BlockSpec((tile, cc), lambda j, i, c: ((2 * j + c[0]) * nb + i, 0))
        flat = pl.BlockSpec((tile, cc), lambda j, i, c: (j * nb + i, 0))
        p, pb = _indexed_sum(f"grad_pair_sum_{n}", lambda u, v: (u + v, u + v), (4, nb), [mine, flat], [flat, flat],
                             [jax.ShapeDtypeStruct((4 * rr, cc), F32), jax.ShapeDtypeStruct((4 * rr, cc), BF)],
                             lax.axis_index("c"), (grads[n], other.reshape(4 * rr, cc)))
        parts[n], parts_bf[n] = p.reshape(4, rr, cc), pb.reshape(4, rr, cc)
    return parts, parts_bf


def _chip_sums(names, parts, landed):
    halves = {}
    for n, got in zip(names, landed):
        _, rr, cc = got.shape
        tile = _ew_tile(rr, cc)
        nb = rr // tile
        own = pl.BlockSpec((tile, cc), lambda i, chip: (chip[0] * nb + i, 0))
        peer = lambda k: pl.BlockSpec((tile, cc), lambda i, chip: (k * nb + i, 0))
        halves[n] = _indexed_sum(f"grad_chip_sum_{n}", lambda o, k0, k1, k2: (((o + k0) + k1) + k2,), (nb,),
                                 [own, peer(0), peer(1), peer(2)], [pl.BlockSpec((tile, cc), lambda i, chip: (i, 0))],
                                 [jax.ShapeDtypeStruct((rr, cc), F32)], 2 * lax.axis_index("x") + lax.axis_index("y"),
                                 (parts[n].reshape(4 * rr, cc),) + (got.reshape(3 * rr, cc),) * 3)[0]
    return halves


def _both_halves(mine, shared):
    both = lax.dynamic_update_index_in_dim(shared, mine, lax.axis_index("c"), 0)
    return both.reshape(2 * mine.shape[0], mine.shape[1])


BIG = ("w_in", "w_attn_proj", "w_conv_proj", "w_out", "w_cq", "w_ckv", "w_co", "w_up", "w_down")
COL_SHARDED = {"w_in": True, "w_attn_proj": True, "w_conv_proj": True, "w_out": False, "w_cq": False,
               "w_ckv": True, "w_co": False, "w_up": True, "w_down": False}
SMALL = ("g_mix", "b_gate", "conv_b", "conv_ln_g", "conv_ln_b", "g_cross", "g_mem", "g_mlp", "g_final")
ORDER = ("g_mix", "w_in", "b_gate", "conv_w", "conv_b", "conv_ln_g", "conv_ln_b", "w_attn_proj", "w_conv_proj", "w_out",
         "g_cross", "g_mem", "w_cq", "w_ckv", "w_co", "g_mlp", "w_up", "w_down", "g_final")


def _pad_rows(flat, rows):
    return jnp.pad(flat, (0, rows * LANES - flat.shape[0])).reshape(rows, LANES)


REST = tuple(n for n in BIG if n != "w_in")
WAVE_MLP = ("w_down", "w_up")
WAVE_MID = ("w_co", "w_cq", "w_ckv", "w_out", "w_attn_proj", "w_conv_proj")


def _local_step(x, mem, tgt, shards, small, conv_w_full):
    t = x.shape[0]
    tr = 512
    c_tab, s_tab = _rope_tables(t)
    row = lambda v: v.reshape(1, -1)
    g_mix, g_cross, g_mem, g_mlp, g_final = (row(small[n]) for n in ("g_mix", "g_cross", "g_mem", "g_mlp", "g_final"))
    b_gate, conv_b, ln_g, ln_b = (row(small[n]) for n in ("b_gate", "conv_b", "conv_ln_g", "conv_ln_b"))
    w32 = jnp.pad(conv_w_full, ((0, CONV_KP - CONV_K), (0, 0)))

    (u,), (w_in_all,) = _rowcall("mix_norm", lambda a, g: (_rms(a, g),), t, tr, [(x, D_MODEL, 0)], [g_mix], [(D_MODEL, BF)],
                                 side=_gather_side([shards["w_in"]]))
    z, gathered = _mm("in_proj", u, w_in_all, out_dtypes=(BF,), side=_gather_side([shards[n] for n in REST]))
    wfull = {"w_in": w_in_all}
    for n, g in zip(REST, gathered):
        wfull[n] = g if COL_SHARDED[n] else g.reshape(1, 4 * g.shape[1], g.shape[2])
    qkv = _rope_split(z, c_tab, s_tab, t)
    qkv_cls, outs, lses = [], [], []
    for g, dil in enumerate(DILATIONS):
        ops = tuple((a.reshape(t, GROUP_W), 0) for a in qkv[3 * g:3 * g + 3])
        qkv_cls.append(ops)
        o_g, l_g = _attn_fwd(f"attn_fwd_{g}", *ops, t, dil)
        outs.append(o_g)
        lses.append(l_g)
    attn, lse = _merge_classes(outs, lses, t)
    y_attn = _mm("attn_proj", attn, wfull["w_attn_proj"])
    c1 = _conv_fwd(z, w32, conv_b, t)
    (c2,) = _rowcall("conv_ln_silu", _ln_silu_fn, t, tr, [(c1, CONV_CH, 0)], [ln_g, ln_b], [(CONV_CH, BF)])
    tn_cp = wfull["w_conv_proj"].shape[2]
    merged, y_conv = _mm("conv_proj", c2, wfull["w_conv_proj"], epi=_gate_epi, out_dtypes=(BF, F32),
                         extras=(y_attn, (z, GATE_A_COL // tn_cp), (z, GATE_B_COL // tn_cp)),
                         params=((b_gate, 0), (b_gate, D_MODEL // tn_cp)))
    x1, uq = _mm("out_proj", merged, wfull["w_out"], extras=(x,), params=(g_cross,), epi=_res_norm_epi, out_dtypes=(F32, BF))
    (mn,) = _rowcall("mem_norm", lambda a, g: (_rms(a, g),), N_MEM, N_MEM, [(mem, D_MODEL, 0)], [g_mem], [(D_MODEL, BF)])
    cq = _mm("cross_q", uq, wfull["w_cq"], out_dtypes=(BF,))
    ckv = _mm("cross_kv", mn, wfull["w_ckv"], out_dtypes=(BF,))
    co = _cross_fwd(cq, ckv, t)
    x2, um = _mm("cross_out", co, wfull["w_co"], extras=(x1,), params=(g_mlp,), epi=_res_norm_epi, out_dtypes=(F32, BF))
    hact = _mm("mlp_up", um, wfull["w_up"], out_dtypes=(BF,), epi=lambda acc: (jnp.square(jnp.maximum(acc, 0.0)),))
    d3, d3b, loss_row, dg_final = _mm("mlp_down", hact, wfull["w_down"], extras=(x2, tgt), params=(g_final,), epi=_final_epi,
                                      out_dtypes=(F32, BF), sums=[(1, LANES), (1, D_MODEL)])

    gw = {}
    dhp = _mm("mlp_down_bwd", d3b, wfull["w_down"], nt=True, extras=(hact,), out_dtypes=(BF,),
              epi=lambda acc, h: (acc * 2.0 * jnp.sqrt(h.astype(F32)),))
    gw["w_down"] = _mm_tn("mlp_down_wgrad", hact, d3b)
    gw["w_up"] = _mm_tn("mlp_up_wgrad", um, dhp)

    def swap_of(names):
        return _swap_side([gw[n] for n in names], [COL_SHARDED[n] for n in names])

    (d2, d2b, dg_mlp), theirs = _mm("mlp_up_bwd", dhp, wfull["w_up"], nt=True, extras=(x2, d3), params=(g_mlp,),
                                    epi=_norm_bwd_epi, out_dtypes=(F32, BF), sums=[(1, D_MODEL)], side=swap_of(WAVE_MLP))
    parts_mlp, parts_bf_mlp = _pair_sums(WAVE_MLP, gw, theirs)

    dco = _mm("cross_out_bwd", d2b, wfull["w_co"], nt=True, out_dtypes=(BF,))
    gw["w_co"] = _mm_tn("cross_out_wgrad", co, d2b)
    dcq, dckv = _cross_bwd(cq, ckv, dco, t)
    gw["w_cq"] = _mm_tn("cross_q_wgrad", uq, dcq)
    d1, d1b, dg_cross = _mm("cross_q_bwd", dcq, wfull["w_cq"], nt=True, extras=(x1, d2), params=(g_cross,), epi=_norm_bwd_epi,
                            out_dtypes=(F32, BF), sums=[(1, D_MODEL)])
    gw["w_ckv"] = _mm_tn("cross_kv_wgrad", mn, dckv, tk=N_MEM)
    dmn = _mm("cross_kv_bwd", dckv, wfull["w_ckv"], nt=True)
    (dg_mem,) = _rowcall("mem_norm_bwd", lambda a, dn, g: (_rms_bwd(a, g, dn)[1],), N_MEM, N_MEM,
                         [(mem, D_MODEL, 0), (dmn, D_MODEL, 0)], [g_mem], [], accs=[(1, D_MODEL)])

    dya, dyc, dgate_a, dgate_b, dbg_a, dbg_b = _mm(
        "out_proj_bwd", d1b, wfull["w_out"], nt=True, tm=min(512, t), epi=_gate_bwd_epi, out_dtypes=(BF,) * 4,
        extras=(y_attn, y_conv, (z, GATE_A_COL // D_MODEL), (z, GATE_B_COL // D_MODEL)), params=(b_gate,),
        sums=[(1, D_MODEL), (1, D_MODEL)])
    dbg = jnp.concatenate([dbg_a, dbg_b], axis=1)
    gw["w_out"] = _mm_tn("out_proj_wgrad", merged, d1b)
    gw["w_attn_proj"] = _mm_tn("attn_proj_wgrad", attn, dya)
    dattn = _mm("attn_proj_bwd", dya, wfull["w_attn_proj"], nt=True)
    gw["w_conv_proj"] = _mm_tn("conv_proj_wgrad", c2, dyc)
    dc2 = _mm("conv_proj_bwd", dyc, wfull["w_conv_proj"], nt=True)
    (dc1, dlng, dlnb), theirs = _rowcall("conv_ln_silu_bwd", _ln_silu_bwd_fn, t, tr, [(c1, CONV_CH, 0), (dc2, CONV_CH, 0)],
                                         [ln_g, ln_b], [(CONV_CH, F32)], accs=[(1, CONV_CH), (1, CONV_CH)],
                                         side=swap_of(WAVE_MID))
    parts_mid, parts_bf_mid = _pair_sums(WAVE_MID, gw, theirs)
    (dglu_a, dglu_b, dconv), landed = _conv_bwd(z, dc1, w32, t, side=_scatter_side([parts_bf_mlp[n] for n in WAVE_MLP]))
    halves_mlp = _chip_sums(WAVE_MLP, parts_mlp, landed)

    dl0, cls1, cls2 = _attn_bwd_prep(dattn, attn, lse, t)
    dqs, dks, dvs = [], [], []
    for g, (dil, (da_c, dl_c, ls_c)) in enumerate(zip(DILATIONS, ((dattn, dl0, lse), cls1, cls2))):
        res = _attn_bwd(f"attn_bwd_{g}", *qkv_cls[g], da_c, dl_c, ls_c, t, dil,
                        side=_share_side([halves_mlp[n] for n in WAVE_MLP]) if g == 0 else None)
        if g == 0:
            res, others = res
            gshard = {n: _both_halves(halves_mlp[n], o) for n, o in zip(WAVE_MLP, others)}
        dqs.append(res[0])
        dks.append(res[1])
        dvs.append(res[2])
    (dz,), landed = _rope_bwd_join(dqs, dks, dvs, c_tab, s_tab, (dglu_a, dglu_b, dgate_a, dgate_b), t,
                                   side=_scatter_side([parts_bf_mid[n] for n in WAVE_MID]))
    halves_mid = _chip_sums(WAVE_MID, parts_mid, landed)

    gw_in_full, others = _mm_tn("in_proj_wgrad", u, dz, side=_share_side([halves_mid[n] for n in WAVE_MID]))
    gshard.update({n: _both_halves(halves_mid[n], o) for n, o in zip(WAVE_MID, others)})
    gw_in = {"w_in": gw_in_full}
    du_a, theirs = _mm("in_proj_bwd_a", dz, wfull["w_in"], nt=True, k_part=(0, 2), side=_swap_side([gw_in["w_in"]], [True]))
    parts, parts_bf = _pair_sums(("w_in",), gw_in, theirs)
    (gx, dg_mix), landed_in = _mm("in_proj_bwd_b", dz, wfull["w_in"], nt=True, k_part=(1, 2), extras=(du_a, x, d1),
                                  params=(g_mix,), epi=lambda acc, first, a, dres, g: _norm_bwd_epi(acc + first, a, dres, g)[1:],
                                  sums=[(1, D_MODEL)], side=_scatter_side([parts_bf["w_in"]]))
    halves = _chip_sums(("w_in",), parts, landed_in)

    gsmall = {"g_mix": dg_mix, "b_gate": dbg, "conv_b": dconv[CONV_K:CONV_K + 1], "conv_ln_g": dlng, "conv_ln_b": dlnb,
              "g_cross": dg_cross, "g_mem": dg_mem, "g_mlp": dg_mlp, "g_final": dg_final, "conv_w": dconv[:CONV_K]}
    return loss_row, gx, gshard, gsmall, halves["w_in"]


def kernel(x, mem, g_mix, w_in, b_gate, conv_w, conv_b, conv_ln_g, conv_ln_b, w_attn_proj, w_conv_proj, w_out, g_cross, g_mem, w_cq, w_ckv, w_co, g_mlp, w_up, w_down, g_final, loss_target, m_g_mix, m_w_in, m_b_gate, m_conv_w, m_conv_b, m_conv_ln_g, m_conv_ln_b, m_w_attn_proj, m_w_conv_proj, m_w_out, m_g_cross, m_g_mem, m_w_cq, m_w_ckv, m_w_co, m_g_mlp, m_w_up, m_w_down, m_g_final, v_g_mix, v_w_in, v_b_gate, v_conv_w, v_conv_b, v_conv_ln_g, v_conv_ln_b, v_w_attn_proj, v_w_conv_proj, v_w_out, v_g_cross, v_g_mem, v_w_cq, v_w_ckv, v_w_co, v_g_mlp, v_w_up, v_w_down, v_g_final):
    w = dict(g_mix=g_mix, w_in=w_in, b_gate=b_gate, conv_w=conv_w, conv_b=conv_b, conv_ln_g=conv_ln_g, conv_ln_b=conv_ln_b,
             w_attn_proj=w_attn_proj, w_conv_proj=w_conv_proj, w_out=w_out, g_cross=g_cross, g_mem=g_mem, w_cq=w_cq,
             w_ckv=w_ckv, w_co=w_co, g_mlp=g_mlp, w_up=w_up, w_down=w_down, g_final=g_final)
    mo = dict(g_mix=m_g_mix, w_in=m_w_in, b_gate=m_b_gate, conv_w=m_conv_w, conv_b=m_conv_b, conv_ln_g=m_conv_ln_g,
              conv_ln_b=m_conv_ln_b, w_attn_proj=m_w_attn_proj, w_conv_proj=m_w_conv_proj, w_out=m_w_out, g_cross=m_g_cross,
              g_mem=m_g_mem, w_cq=m_w_cq, w_ckv=m_w_ckv, w_co=m_w_co, g_mlp=m_g_mlp, w_up=m_w_up, w_down=m_w_down,
              g_final=m_g_final)
    vo = dict(g_mix=v_g_mix, w_in=v_w_in, b_gate=v_b_gate, conv_w=v_conv_w, conv_b=v_conv_b, conv_ln_g=v_conv_ln_g,
              conv_ln_b=v_conv_ln_b, w_attn_proj=v_w_attn_proj, w_conv_proj=v_w_conv_proj, w_out=v_w_out, g_cross=v_g_cross,
              g_mem=v_g_mem, w_cq=v_w_cq, w_ckv=v_w_ckv, w_co=v_w_co, g_mlp=v_g_mlp, w_up=v_w_up, w_down=v_w_down,
              g_final=v_g_final)
    shapes = {n: w[n].shape for n in ORDER}
    two_d = lambda a: a.reshape(a.shape[-2], a.shape[-1])
    chip = 2 * lax.axis_index("x") + lax.axis_index("y")

    shards = {n: two_d(w[n]).astype(BF) for n in BIG}
    cw_rows = 48
    cw_all, _ = _gather8("gather_conv_w", _pad_rows(conv_w.reshape(-1), cw_rows))
    cw_shard = CONV_K * (CONV_CH // 4)
    conv_w_full = jnp.concatenate(
        [cw_all[2 * j].reshape(-1)[:cw_shard].reshape(CONV_K, CONV_CH // 4) for j in range(4)], axis=1)

    small = {n: w[n] for n in SMALL}
    loss_row, gx, gshard, gsmall, half_w_in = _local_step(two_d(x), two_d(mem), two_d(loss_target), shards, small,
                                                          conv_w_full)
    loss = lax.psum(loss_row[0, 0], ("x", "y", "c"))

    small_names = SMALL + ("conv_w",)
    flat = jnp.concatenate([gsmall[n].reshape(-1) for n in small_names])
    sm_rows = -(-flat.shape[0] // (8 * LANES)) * 8
    (_, sm_sum), others = _gather8("reduce_small_grads", _pad_rows(flat, sm_rows), side=_share_side([half_w_in]))
    gshard["w_in"] = _both_halves(half_w_in, others[0])
    sm_sum = sm_sum.reshape(-1)
    off = 0
    for n in small_names:
        size = gsmall[n].size
        gshard[n] = sm_sum[off:off + size].reshape(gsmall[n].shape)
        off += size
    gshard["conv_w"] = lax.dynamic_slice_in_dim(gshard["conv_w"], chip * (CONV_CH // 4), CONV_CH // 4, axis=1)

    grads, deltas, new_m, new_v = {}, {}, {}, {}
    for n in BIG:
        d, m2, v2 = _adam(f"adamw_{n}", two_d(w[n]), gshard[n], two_d(mo[n]), two_d(vo[n]))
        grads[n], deltas[n], new_m[n], new_v[n] = (a.reshape(shapes[n]) for a in (gshard[n], d, m2, v2))
    pack = lambda src: jnp.concatenate([src[n].reshape(-1) for n in small_names])
    n_small = sum(w[n].size for n in small_names)
    ad_rows = -(-n_small // (8 * LANES)) * 8
    d, m2, v2 = _adam("adamw_small", *[_pad_rows(pack(src), ad_rows) for src in (w, gshard, mo, vo)])
    off = 0
    for n in small_names:
        size = w[n].size
        grads[n] = gshard[n].reshape(shapes[n])
        deltas[n], new_m[n], new_v[n] = (a.reshape(-1)[off:off + size].reshape(shapes[n]) for a in (d, m2, v2))
        off += size

    return (loss, gx.reshape(x.shape), *[grads[n] for n in ORDER], *[deltas[n] for n in ORDER],
            *[new_m[n] for n in ORDER], *[new_v[n] for n in ORDER])
```

```python
import functools

import jax
import jax.numpy as jnp
from jax import lax
from jax.experimental import pallas as pl
from jax.experimental.pallas import tpu as pltpu

F32 = jnp.float32
BF = jnp.bfloat16

D_MODEL = 1024
N_MEM = 256
HEAD_DIM = 128
HEADS_PER_GROUP = 4
DILATIONS = (1, 4, 16)
BLK = 128
GROUP_W = HEADS_PER_GROUP * HEAD_DIM
ATTN_WIDTH = 3 * GROUP_W
ROT_DIM = 32
ROPE_THETA = 500000.0
CONV_CH = 768
CONV_K = 31
CONV_KP = 32
IN_WIDTH = 8192
CROSS_HEADS = 4
CROSS_HEAD_DIM = 256
D_FF = 4096
EPS = 1e-6
ATTN_SCALE = HEAD_DIM ** -0.5
CROSS_SCALE = CROSS_HEAD_DIM ** -0.5
NEG = -1e30

ADAM_LR = 0.001
ADAM_B1 = 0.9
ADAM_B2 = 0.999
ADAM_EPS = 1e-08
ADAM_WD = 0.01
ADAM_STEP = 10

LANES = 128
SUBLANES = 8
VMEM_LIMIT = 56 * 1024 * 1024
MESH = pl.DeviceIdType.MESH
ANY = pl.BlockSpec(memory_space=pl.ANY)

GLU_A_COL = 3 * ATTN_WIDTH
GLU_B_COL = GLU_A_COL + CONV_CH
GATE_A_COL = GLU_B_COL + CONV_CH
GATE_B_COL = GATE_A_COL + D_MODEL


def _params(sem=None):
    return pltpu.CompilerParams(dimension_semantics=sem, vmem_limit_bytes=VMEM_LIMIT)


def _dot(a, b):
    return lax.dot_general(a, b, (((1,), (0,)), ((), ())), preferred_element_type=F32)


def _dot_nt(a, b):
    return lax.dot_general(a, b, (((1,), (1,)), ((), ())), preferred_element_type=F32)


def _dot_tn(a, b):
    return lax.dot_general(a, b, (((0,), (0,)), ((), ())), preferred_element_type=F32)


def _sig(x):
    return 1.0 / (1.0 + jnp.exp(-x))


def _glu(a, b):
    return a.astype(F32) * _sig(b.astype(F32))


class _Side:
    def __init__(self, arrays, out_shapes, n_sems, build, aliases=None):
        self.arrays, self.out_shapes, self.n_sems, self.build = list(arrays), list(out_shapes), n_sems, build
        self.aliases = aliases or {}


def _pcall(name, kern, grid, in_specs, out_specs, out_shape, scratch_shapes, sem, args, side=None):
    in_specs, out_specs, out_shape, scratch_shapes = list(in_specs), list(out_specs), list(out_shape), list(scratch_shapes)
    if side is None:
        return pl.pallas_call(kern, name=name, grid=grid, in_specs=in_specs, out_specs=out_specs, out_shape=out_shape,
                              scratch_shapes=scratch_shapes, compiler_params=_params(sem))(*args)
    ni, no, nsc = len(in_specs), len(out_specs), len(scratch_shapes)
    nsi, nso = len(side.arrays), len(side.out_shapes)

    def wrapped(*refs):
        ins, side_ins = refs[:ni], refs[ni:ni + nsi]
        outs, side_outs = refs[ni + nsi:ni + nsi + no], refs[ni + nsi + no:ni + nsi + no + nso]
        scratch = refs[ni + nsi + no + nso:ni + nsi + no + nso + nsc]
        send_sems, recv_sems = refs[-2:]
        start, finish = side.build(side_ins, side_outs, send_sems, recv_sems)
        if grid:
            first = functools.reduce(jnp.logical_and, [pl.program_id(a) == 0 for a in range(len(grid))])
            last = functools.reduce(jnp.logical_and, [pl.program_id(a) == g - 1 for a, g in enumerate(grid)])
            pl.when(first)(start)
            kern(*ins, *outs, *scratch)
            pl.when(last)(finish)
        else:
            start()
            kern(*ins, *outs, *scratch)
            finish()

    res = pl.pallas_call(
        wrapped, name=name, grid=grid, in_specs=in_specs + [ANY] * nsi, out_specs=out_specs + [ANY] * nso,
        out_shape=out_shape + side.out_shapes,
        scratch_shapes=scratch_shapes + [pltpu.SemaphoreType.DMA((side.n_sems,)), pltpu.SemaphoreType.DMA((side.n_sems,))],
        input_output_aliases={ni + k: no + v for k, v in side.aliases.items()},
        compiler_params=_params(("arbitrary",) * len(grid) if grid else None),
    )(*args, *side.arrays)
    return res[:no], res[no:]


def _rowcall(name, fn, n_rows, tile, ins, params, outs, accs=(), side=None):
    tile = min(tile, n_rows)
    ni, npar, no, na = len(ins), len(params), len(outs), len(accs)

    def kern(*refs):
        in_refs = refs[:ni + npar]
        o_refs = refs[ni + npar:ni + npar + no]
        a_refs = refs[ni + npar + no:]
        vals = fn(*[r[...] for r in in_refs])
        for r, v in zip(o_refs, vals[:no]):
            r[...] = v.astype(r.dtype)
        if na:
            @pl.when(pl.program_id(0) == 0)
            def _():
                for r in a_refs:
                    r[...] = jnp.zeros_like(r)
            for r, v in zip(a_refs, vals[no:]):
                r[...] += v

    in_specs = []
    arrays = []
    for spec in ins:
        arr, width, cb = spec[0], spec[1], spec[2]
        rb = spec[3] if len(spec) > 3 else 0
        in_specs.append(pl.BlockSpec((tile, width), functools.partial(lambda i, cb, rb: (i + rb, cb), cb=cb, rb=rb)))
        arrays.append(arr)
    for p in params:
        in_specs.append(pl.BlockSpec(p.shape, lambda i: (0, 0)))
        arrays.append(p)
    out_specs = [pl.BlockSpec((tile, w), lambda i: (i, 0)) for w, _ in outs]
    out_specs += [pl.BlockSpec(s, lambda i: (0, 0)) for s in accs]
    out_shape = [jax.ShapeDtypeStruct((n_rows, w), dt) for w, dt in outs]
    out_shape += [jax.ShapeDtypeStruct(s, F32) for s in accs]
    return _pcall(name, kern, (n_rows // tile,), in_specs, out_specs, out_shape, [],
                  ("arbitrary",) if na else ("parallel",), arrays, side)


def _mm(name, a, w3, *, nt=False, extras=(), params=(), epi=None, out_dtypes=(F32,), sums=(), tm=None, tn=None, tk=None,
        k_part=(0, 1), side=None):
    m, ka = a.shape
    ns, r, cs = w3.shape
    if not nt:
        k_dim, n = r, ns * cs
        tn = tn or min(cs, 1024)
        tk = tk or min(k_dim, 1024)
    else:
        k_dim, n = ns * cs, r
        tn = tn or min(r, 1024)
        tk = tk or min(cs, 1024)
    assert ka == k_dim, (name, a.shape, w3.shape)
    assert not sums or tn == n, name
    nk = k_dim // tk // k_part[1]
    k0 = k_part[0] * nk
    if not nt:
        nbs = cs // tn
        w_spec = pl.BlockSpec((None, tk, tn), lambda i, j, k: (j // nbs, k + k0, j % nbs))
    else:
        kbs = cs // tk
        w_spec = pl.BlockSpec((None, tn, tk), lambda i, j, k: ((k + k0) // kbs, j, (k + k0) % kbs))
    tm = tm or min(m, 1024)
    ne, no, nsum = len(extras) + len(params), len(out_dtypes), len(sums)

    def kern(a_ref, w_ref, *rest):
        e_refs = rest[:ne]
        o_refs = rest[ne:ne + no]
        s_refs = rest[ne + no:ne + no + nsum]

        def part():
            av = a_ref[...].astype(BF)
            return _dot_nt(av, w_ref[...]) if nt else _dot(av, w_ref[...])

        def finish(res):
            vals = epi(res, *[e[...] for e in e_refs]) if epi else (res,)
            for o, v in zip(o_refs, vals[:no]):
                o[...] = v.astype(o.dtype)
            for sr, v in zip(s_refs, vals[no:]):
                sr[...] += v

        if nsum:
            @pl.when(jnp.logical_and(pl.program_id(0) == 0, pl.program_id(2) == 0))
            def _():
                for sr in s_refs:
                    sr[...] = jnp.zeros_like(sr)

        if nk == 1:
            finish(part())
            return
        acc = rest[ne + no + nsum]
        k = pl.program_id(2)

        @pl.when(k == 0)
        def _():
            acc[...] = part()

        @pl.when(jnp.logical_and(k > 0, k < nk - 1))
        def _():
            acc[...] += part()

        @pl.when(k == nk - 1)
        def _():
            finish(acc[...] + part())

    in_specs = [pl.BlockSpec((tm, tk), lambda i, j, k: (i, k + k0)), w_spec]
    split = lambda items: [(it if isinstance(it, tuple) else (it, None)) for it in items]
    extras, params = split(extras), split(params)
    in_specs += [pl.BlockSpec((tm, tn), functools.partial(lambda i, j, k, off: (i, j + off), off=off or 0)) for _, off in extras]
    in_specs += [pl.BlockSpec(p.shape, lambda i, j, k: (0, 0)) if off is None else
                 pl.BlockSpec((p.shape[0], tn), functools.partial(lambda i, j, k, off: (0, j + off), off=off)) for p, off in params]
    extras, params = [e for e, _ in extras], [p for p, _ in params]
    out_specs = [pl.BlockSpec((tm, tn), lambda i, j, k: (i, j)) for _ in out_dtypes]
    out_specs += [pl.BlockSpec(sh, lambda i, j, k: (0, 0)) for sh in sums]
    out_shape = [jax.ShapeDtypeStruct((m, n), dt) for dt in out_dtypes] + [jax.ShapeDtypeStruct(sh, F32) for sh in sums]
    res = _pcall(name, kern, (m // tm, n // tn, nk), in_specs, out_specs, out_shape,
                 [pltpu.VMEM((tm, tn), F32)] if nk > 1 else [],
                 ("arbitrary",) * 3 if nsum else ("parallel", "parallel", "arbitrary"), (a, w3, *extras, *params), side)
    main = res[0] if side is not None else res
    main = main[0] if no + nsum == 1 else main
    return (main, res[1]) if side is not None else main


def _mm_tn(name, a, b, tm=None, tn=None, tk=None, side=None):
    t, ka = a.shape
    _, n = b.shape
    tm = tm or min(ka, 1024)
    tn = tn or min(n, 1024)
    tk = tk or min(t, 1024)

    def kern(a_ref, b_ref, o_ref):
        def part():
            return _dot_tn(a_ref[...].astype(BF), b_ref[...].astype(BF))

        @pl.when(pl.program_id(2) == 0)
        def _():
            o_ref[...] = part()

        @pl.when(pl.program_id(2) > 0)
        def _():
            o_ref[...] += part()

    res = _pcall(name, kern, (ka // tm, n // tn, t // tk),
                 [pl.BlockSpec((tk, tm), lambda i, j, k: (k, i)), pl.BlockSpec((tk, tn), lambda i, j, k: (k, j))],
                 [pl.BlockSpec((tm, tn), lambda i, j, k: (i, j))], [jax.ShapeDtypeStruct((ka, n), F32)], [],
                 ("parallel", "parallel", "arbitrary"), (a, b), side)
    return (res[0][0], res[1]) if side is not None else res[0]


def _rms(x, g):
    return x * lax.rsqrt(jnp.mean(x * x, axis=-1, keepdims=True) + EPS) * g


def _rms_bwd(x, g, dy):
    r = lax.rsqrt(jnp.mean(x * x, axis=-1, keepdims=True) + EPS)
    xh = x * r
    dxh = dy * g
    dx = r * (dxh - xh * jnp.mean(dxh * xh, axis=-1, keepdims=True))
    return dx, jnp.sum(dy * xh, axis=0, keepdims=True)


def _rot(t, c, s):
    lane = lax.broadcasted_iota(jnp.int32, t.shape, 1)
    swapped = jnp.where(lane < ROT_DIM // 2, pltpu.roll(t, HEAD_DIM - ROT_DIM // 2, 1), pltpu.roll(t, ROT_DIM // 2, 1))
    return t * c + swapped * s


def _rope_tables(t):
    half = ROT_DIM // 2
    pos = jnp.arange(t, dtype=F32)
    inv_freq = ROPE_THETA ** (-jnp.arange(0, ROT_DIM, 2, dtype=F32) / ROT_DIM)
    ang = pos[:, None] * inv_freq[None, :]
    cos, sin = jnp.cos(ang), jnp.sin(ang)
    ones = jnp.ones((t, HEAD_DIM - ROT_DIM), F32)
    c_tab = jnp.concatenate([cos, cos, ones], axis=1)
    s_tab = jnp.concatenate([-sin, sin, 0.0 * ones], axis=1)
    return c_tab, s_tab


def _merge_fn(o0, o1, o2, l0, l1, l2):
    m = jnp.maximum(jnp.maximum(l0, l1), l2)
    e0, e1, e2 = jnp.exp(l0 - m), jnp.exp(l1 - m), jnp.exp(l2 - m)
    tot = e0 + e1 + e2
    return (e0 * o0 + e1 * o1 + e2 * o2) / tot, m + jnp.log(tot)


def _ln_parts(c1):
    mu = jnp.mean(c1, axis=-1, keepdims=True)
    xc = c1 - mu
    r = lax.rsqrt(jnp.mean(xc * xc, axis=-1, keepdims=True) + EPS)
    return xc * r, r


def _ln_silu_fn(c1, g, b):
    xh, _ = _ln_parts(c1)
    yl = xh * g + b
    return (yl * _sig(yl),)


def _ln_silu_bwd_fn(c1, dout, g, b):
    xh, r = _ln_parts(c1)
    yl = xh * g + b
    s = _sig(yl)
    dyl = dout * (s + yl * s * (1.0 - s))
    dxh = dyl * g
    dx = r * (dxh - jnp.mean(dxh, axis=-1, keepdims=True) - xh * jnp.mean(dxh * xh, axis=-1, keepdims=True))
    return dx, jnp.sum(dyl * xh, axis=0, keepdims=True), jnp.sum(dyl, axis=0, keepdims=True)


def _gate_epi(yc, ya, ga, gb, ba, bb):
    return _sig(ga + ba) * ya + _sig(gb + bb) * yc, yc


def _gate_bwd_epi(dm, ya, yc, ga, gb, bg):
    sa = _sig(ga + bg[:, :D_MODEL])
    sb = _sig(gb + bg[:, D_MODEL:])
    dga = dm * ya * sa * (1.0 - sa)
    dgb = dm * yc * sb * (1.0 - sb)
    return dm * sa, dm * sb, dga, dgb, jnp.sum(dga, axis=0, keepdims=True), jnp.sum(dgb, axis=0, keepdims=True)


def _res_norm_epi(acc, res, g):
    xn = res + acc
    return xn, _rms(xn, g)


def _norm_bwd_epi(acc, a, dres, g):
    dx, dg = _rms_bwd(a, g, acc)
    return dres + dx, dres + dx, dg


def _final_epi(acc, res, tgt, g):
    return _final_fn(res + acc, tgt, g)


def _final_fn(x3, tgt, g):
    err = _rms(x3, g) - tgt
    lrow = jnp.sum(err * err, axis=-1, keepdims=True) * (0.5 / D_MODEL)
    lsum = jnp.sum(lrow, axis=0, keepdims=True)
    dx, dg = _rms_bwd(x3, g, err * (1.0 / D_MODEL))
    return dx, dx, jnp.broadcast_to(lsum, (1, LANES)), dg


def _attn_geometry(t, dil):
    cls = t // dil
    rows = min(4 * BLK, cls)
    return rows, rows // BLK, cls // rows


def _head_lanes(h):
    return slice(h * HEAD_DIM, (h + 1) * HEAD_DIM)


def _band_mask():
    row = lax.broadcasted_iota(jnp.int32, (BLK, 2 * BLK), 0)
    col = lax.broadcasted_iota(jnp.int32, (BLK, 2 * BLK), 1)
    return jnp.logical_and(col >= row, col <= row + BLK), col


def _stage_window(scr, halo_ref, cur_ref):
    scr[:BLK, :] = halo_ref[...]
    scr[BLK:, :] = cur_ref[...]


def _attn_fwd(name, q, k, v, t, dil):
    rows, nbk, spc = _attn_geometry(t, dil)

    def kern(q_ref, k_ref, kh_ref, v_ref, vh_ref, o_ref, l_ref, k_scr, v_scr):
        i = pl.program_id(0)
        first_shift = jnp.where(i % spc == 0, BLK, 0)
        _stage_window(k_scr, kh_ref, k_ref)
        _stage_window(v_scr, vh_ref, v_ref)
        band, col = _band_mask()
        band_first = jnp.logical_and(band, col >= first_shift)
        for h in range(HEADS_PER_GROUP):
            hs = _head_lanes(h)
            for b in range(nbk):
                rs, win = slice(b * BLK, (b + 1) * BLK), slice(b * BLK, (b + 2) * BLK)
                s = jnp.where(band_first if b == 0 else band, _dot_nt(q_ref[rs, hs], k_scr[win, hs]) * ATTN_SCALE, NEG)
                m = jnp.max(s, axis=1, keepdims=True)
                p = jnp.exp(s - m)
                tot = jnp.sum(p, axis=1, keepdims=True)
                o_ref[rs, hs] = _dot(p.astype(BF), v_scr[win, hs]) / tot
                l_ref[rs, hs] = jnp.broadcast_to(m + jnp.log(tot), (BLK, HEAD_DIM))

    def cur(cb):
        return pl.BlockSpec((rows, GROUP_W), lambda i: (i, cb))

    def halo(cb):
        return pl.BlockSpec((BLK, GROUP_W), lambda i: (jnp.maximum(i * nbk - 1, 0), cb))

    (qa, qc), (ka, kc_), (va, vc_) = q, k, v
    return _pcall(name, kern, (t // rows,), [cur(qc), cur(kc_), halo(kc_), cur(vc_), halo(vc_)],
                  [pl.BlockSpec((rows, GROUP_W), lambda i: (i, 0))] * 2, [jax.ShapeDtypeStruct((t, GROUP_W), F32)] * 2,
                  [pltpu.VMEM((rows + BLK, GROUP_W), BF)] * 2, ("parallel",), (qa, ka, ka, va, va))


def _attn_bwd(name, q, k, v, da, dl, lse, t, dil, side=None):
    rows, nbk, spc = _attn_geometry(t, dil)
    nblk = t // BLK

    def kern(q_ref, qn_ref, k_ref, kh_ref, v_ref, vh_ref, da_ref, dan_ref, dl_ref, dln_ref, ls_ref, lsn_ref,
             dq_ref, dk_ref, dv_ref, k_scr, v_scr):
        i = pl.program_id(0)
        first_shift = jnp.where(i % spc == 0, BLK, 0)
        next_shift = jnp.where((i + 1) % spc == 0, BLK, 0)
        _stage_window(k_scr, kh_ref, k_ref)
        _stage_window(v_scr, vh_ref, v_ref)
        band, col = _band_mask()
        band_first = jnp.logical_and(band, col >= first_shift)
        row1 = lax.broadcasted_iota(jnp.int32, (BLK, BLK), 0)
        col1 = lax.broadcasted_iota(jnp.int32, (BLK, BLK), 1)
        pend_k, pend_v = [None] * HEADS_PER_GROUP, [None] * HEADS_PER_GROUP
        for b in range(nbk):
            rs, win = slice(b * BLK, (b + 1) * BLK), slice(b * BLK, (b + 2) * BLK)
            for h in range(HEADS_PER_GROUP):
                hs = _head_lanes(h)
                qb, dab = q_ref[rs, hs], da_ref[rs, hs].astype(BF)
                kw, vw = k_scr[win, hs], v_scr[win, hs]
                p = jnp.where(band_first if b == 0 else band,
                              jnp.exp(_dot_nt(qb, kw) * ATTN_SCALE - ls_ref[rs, hs][:, :1]), 0.0)
                ds = (p * (_dot_nt(dab, vw) - dl_ref[rs, hs][:, :1]) * ATTN_SCALE).astype(BF)
                dq_ref[rs, hs] = _dot(ds, kw).astype(BF)
                dkw, dvw = _dot_tn(ds, qb), _dot_tn(p.astype(BF), dab)
                if b >= 1:
                    ps = slice((b - 1) * BLK, b * BLK)
                    dk_ref[ps, hs] = (pend_k[h] + dkw[:BLK]).astype(BF)
                    dv_ref[ps, hs] = (pend_v[h] + dvw[:BLK]).astype(BF)
                pend_k[h], pend_v[h] = dkw[BLK:], dvw[BLK:]
        ls_rows = slice((nbk - 1) * BLK, nbk * BLK)
        last = slice(nbk * BLK, (nbk + 1) * BLK)
        for h in range(HEADS_PER_GROUP):
            hs = _head_lanes(h)
            qb, dab = qn_ref[:, hs], dan_ref[:, hs].astype(BF)
            kp, vp = k_scr[last, hs], v_scr[last, hs]
            p = jnp.where(col1 >= row1 + next_shift, jnp.exp(_dot_nt(qb, kp) * ATTN_SCALE - lsn_ref[:, hs]), 0.0)
            ds = (p * (_dot_nt(dab, vp) - dln_ref[:, hs]) * ATTN_SCALE).astype(BF)
            dk_ref[ls_rows, hs] = (pend_k[h] + _dot_tn(ds, qb)).astype(BF)
            dv_ref[ls_rows, hs] = (pend_v[h] + _dot_tn(p.astype(BF), dab)).astype(BF)

    def cur(cb):
        return pl.BlockSpec((rows, GROUP_W), lambda i: (i, cb))

    def prev(cb):
        return pl.BlockSpec((BLK, GROUP_W), lambda i: (jnp.maximum(i * nbk - 1, 0), cb))

    def nxt(cb):
        return pl.BlockSpec((BLK, GROUP_W), lambda i: (jnp.minimum((i + 1) * nbk, nblk - 1), cb))

    (qa, qc), (ka, kc_), (va, vc_) = q, k, v
    return _pcall(name, kern, (t // rows,),
                  [cur(qc), nxt(qc), cur(kc_), prev(kc_), cur(vc_), prev(vc_), cur(0), nxt(0), cur(0), nxt(0), cur(0), nxt(0)],
                  [pl.BlockSpec((rows, GROUP_W), lambda i: (i, 0))] * 3, [jax.ShapeDtypeStruct((t, GROUP_W), BF)] * 3,
                  [pltpu.VMEM((rows + BLK, GROUP_W), BF)] * 2, ("parallel",),
                  (qa, qa, ka, ka, va, va, da, da, dl, dl, lse, lse), side)


CLS_TILE = 512


def _cls_block(t, tile, dil, dtype):
    if dil == 1:
        return pl.BlockSpec((tile, GROUP_W), lambda i: (i, 0)), jax.ShapeDtypeStruct((t, GROUP_W), dtype)
    return (pl.BlockSpec((dil, tile // dil, GROUP_W), lambda i: (0, i, 0)),
            jax.ShapeDtypeStruct((dil, t // dil, GROUP_W), dtype))


def _head_scratch(tile):
    return pltpu.VMEM((tile, HEAD_DIM), F32)


def _rope_split(z, c_tab, s_tab, t):
    tile = min(CLS_TILE, t)
    n_heads = ATTN_WIDTH // HEAD_DIM

    def kern(zq_ref, zk_ref, zv_ref, c_ref, s_ref, *rest):
        outs, scr = rest[:9], rest[9]
        c, s = c_ref[...], s_ref[...]
        for which, z_ref in enumerate((zq_ref, zk_ref, zv_ref)):
            for h in range(n_heads):
                g, hs = h // HEADS_PER_GROUP, _head_lanes(h % HEADS_PER_GROUP)
                val = z_ref[:, h * HEAD_DIM:(h + 1) * HEAD_DIM].astype(F32)
                if which < 2:
                    val = _rot(val, c, s)
                if g == 0:
                    outs[which][:, hs] = val.astype(BF)
                    continue
                scr[...] = val
                dil = DILATIONS[g]
                for r in range(dil):
                    outs[3 * g + which][r, :, hs] = scr[pl.ds(r, tile // dil, stride=dil), :].astype(BF)

    blocks = [_cls_block(t, tile, DILATIONS[g], BF) for g in range(3) for _ in range(3)]
    zspec = lambda cb: pl.BlockSpec((tile, ATTN_WIDTH), lambda i: (i, cb))
    tab = pl.BlockSpec((tile, HEAD_DIM), lambda i: (i, 0))
    return _pcall("rope", kern, (t // tile,), [zspec(0), zspec(1), zspec(2), tab, tab], [b[0] for b in blocks],
                  [b[1] for b in blocks], [_head_scratch(tile)], ("parallel",), (z, z, z, c_tab, s_tab))


def _merge_classes(outs, lses, t):
    tile = min(CLS_TILE, t)

    def kern(o0, l0, o1, l1, o2, l2, attn_ref, lse_ref, s_o1, s_l1, s_o2, s_l2):
        for h in range(HEADS_PER_GROUP):
            hs = _head_lanes(h)
            for src, dst, dil in ((o1, s_o1, DILATIONS[1]), (l1, s_l1, DILATIONS[1]), (o2, s_o2, DILATIONS[2]),
                                  (l2, s_l2, DILATIONS[2])):
                for r in range(dil):
                    dst[pl.ds(r, tile // dil, stride=dil), :] = src[r, :, hs]
            attn_ref[:, hs], lse_ref[:, hs] = _merge_fn(o0[:, hs], s_o1[...], s_o2[...], l0[:, hs], s_l1[...], s_l2[...])

    blocks = [_cls_block(t, tile, DILATIONS[g], F32) for g in range(3)]
    args = []
    for g in range(3):
        args += [outs[g].reshape(blocks[g][1].shape), lses[g].reshape(blocks[g][1].shape)]
    tok = pl.BlockSpec((tile, GROUP_W), lambda i: (i, 0))
    return _pcall("attn_merge", kern, (t // tile,), [blocks[g][0] for g in range(3) for _ in range(2)], [tok, tok],
                  [jax.ShapeDtypeStruct((t, GROUP_W), F32)] * 2, [_head_scratch(tile)] * 4, ("parallel",), args)


def _attn_bwd_prep(dattn, attn, lse, t):
    tile = min(CLS_TILE, t)

    def kern(da_ref, at_ref, ls_ref, dl0, da1, dl1, ls1, da2, dl2, ls2, s_da, s_dl, s_ls):
        for h in range(HEADS_PER_GROUP):
            hs = _head_lanes(h)
            da = da_ref[:, hs]
            s_da[...] = da
            s_dl[...] = jnp.broadcast_to(jnp.sum(da * at_ref[:, hs], axis=1, keepdims=True), (tile, HEAD_DIM))
            s_ls[...] = ls_ref[:, hs]
            dl0[:, hs] = s_dl[...]
            for oda, odl, ols, dil in ((da1, dl1, ls1, DILATIONS[1]), (da2, dl2, ls2, DILATIONS[2])):
                for r in range(dil):
                    rows = pl.ds(r, tile // dil, stride=dil)
                    oda[r, :, hs] = s_da[rows, :].astype(BF)
                    odl[r, :, hs] = s_dl[rows, :]
                    ols[r, :, hs] = s_ls[rows, :]

    tok = pl.BlockSpec((tile, GROUP_W), lambda i: (i, 0))
    blocks = [(tok, jax.ShapeDtypeStruct((t, GROUP_W), F32))]
    for g in (1, 2):
        blocks += [_cls_block(t, tile, DILATIONS[g], BF), _cls_block(t, tile, DILATIONS[g], F32),
                   _cls_block(t, tile, DILATIONS[g], F32)]
    res = _pcall("attn_bwd_prep", kern, (t // tile,), [tok, tok, tok], [b[0] for b in blocks], [b[1] for b in blocks],
                 [_head_scratch(tile)] * 3, ("parallel",), (dattn, attn, lse))
    flat = [a.reshape(t, GROUP_W) for a in res]
    return flat[0], flat[1:4], flat[4:7]


def _rope_bwd_join(dqs, dks, dvs, c_tab, s_tab, tail, t, side=None):
    tile = min(256, t)
    n_heads = ATTN_WIDTH // HEAD_DIM

    def kern(q0, q1, q2, k0, k1, k2, v0, v1, v2, c_ref, s_ref, ga_ref, gb_ref, gta_ref, gtb_ref, dz_ref, scr):
        c, s = c_ref[...], -s_ref[...]
        for which, srcs in enumerate(((q0, q1, q2), (k0, k1, k2), (v0, v1, v2))):
            for h in range(n_heads):
                g, hs = h // HEADS_PER_GROUP, _head_lanes(h % HEADS_PER_GROUP)
                if g == 0:
                    val = srcs[0][:, hs].astype(F32)
                else:
                    dil = DILATIONS[g]
                    for r in range(dil):
                        scr[pl.ds(r, tile // dil, stride=dil), :] = srcs[g][r, :, hs].astype(F32)
                    val = scr[...]
                if which < 2:
                    val = _rot(val, c, s)
                col = which * ATTN_WIDTH + h * HEAD_DIM
                dz_ref[:, col:col + HEAD_DIM] = val.astype(BF)
        dz_ref[:, GLU_A_COL:GLU_B_COL] = ga_ref[...]
        dz_ref[:, GLU_B_COL:GATE_A_COL] = gb_ref[...]
        dz_ref[:, GATE_A_COL:GATE_B_COL] = gta_ref[...]
        dz_ref[:, GATE_B_COL:] = gtb_ref[...]

    blocks = [_cls_block(t, tile, DILATIONS[g], BF) for g in range(3)]
    args = [a.reshape(blocks[g][1].shape) for grp in (dqs, dks, dvs) for g, a in enumerate(grp)]
    tab = pl.BlockSpec((tile, HEAD_DIM), lambda i: (i, 0))
    row = lambda w: pl.BlockSpec((tile, w), lambda i: (i, 0))
    return _pcall("rope_bwd", kern, (t // tile,), [blocks[g][0] for _ in range(3) for g in range(3)]
                  + [tab, tab, row(CONV_CH), row(CONV_CH), row(D_MODEL), row(D_MODEL)], [row(IN_WIDTH)],
                  [jax.ShapeDtypeStruct((t, IN_WIDTH), BF)], [_head_scratch(tile)], ("parallel",),
                  (*args, c_tab, s_tab, *tail), side)


def _cross_probs(qh, kh):
    s = _dot_nt(qh, kh) * CROSS_SCALE
    e = jnp.exp(s - jnp.max(s, axis=1, keepdims=True))
    return e, jnp.sum(e, axis=1, keepdims=True)


def _cross_fwd(cq, ckv, t):
    rows = min(512, t)

    def kern(q_ref, kv_ref, o_ref):
        for h in range(CROSS_HEADS):
            hs = slice(h * CROSS_HEAD_DIM, (h + 1) * CROSS_HEAD_DIM)
            vs = slice(D_MODEL + h * CROSS_HEAD_DIM, D_MODEL + (h + 1) * CROSS_HEAD_DIM)
            e, tot = _cross_probs(q_ref[:, hs], kv_ref[:, hs])
            o_ref[:, hs] = (_dot(e.astype(BF), kv_ref[:, vs]) / tot).astype(BF)

    return pl.pallas_call(
        kern, name="cross_fwd", grid=(t // rows,),
        in_specs=[pl.BlockSpec((rows, D_MODEL), lambda i: (i, 0)), pl.BlockSpec((N_MEM, 2 * D_MODEL), lambda i: (0, 0))],
        out_specs=pl.BlockSpec((rows, D_MODEL), lambda i: (i, 0)),
        out_shape=jax.ShapeDtypeStruct((t, D_MODEL), BF),
        compiler_params=_params(("parallel",)),
    )(cq, ckv)


def _cross_bwd(cq, ckv, dco, t):
    rows = min(512, t)

    def kern(q_ref, kv_ref, do_ref, dq_ref, dkv_ref):
        @pl.when(pl.program_id(0) == 0)
        def _():
            dkv_ref[...] = jnp.zeros_like(dkv_ref)
        for h in range(CROSS_HEADS):
            hs = slice(h * CROSS_HEAD_DIM, (h + 1) * CROSS_HEAD_DIM)
            vs = slice(D_MODEL + h * CROSS_HEAD_DIM, D_MODEL + (h + 1) * CROSS_HEAD_DIM)
            qh, kh, vh, doh = q_ref[:, hs], kv_ref[:, hs], kv_ref[:, vs], do_ref[:, hs]
            e, tot = _cross_probs(qh, kh)
            p = e / tot
            dp = _dot_nt(doh, vh)
            ds = (p * (dp - jnp.sum(p * dp, axis=1, keepdims=True)) * CROSS_SCALE).astype(BF)
            dq_ref[:, hs] = _dot(ds, kh).astype(BF)
            dkv_ref[:, hs] += _dot_tn(ds, qh)
            dkv_ref[:, vs] += _dot_tn(p.astype(BF), doh)

    return pl.pallas_call(
        kern, name="cross_bwd", grid=(t // rows,),
        in_specs=[pl.BlockSpec((rows, D_MODEL), lambda i: (i, 0)), pl.BlockSpec((N_MEM, 2 * D_MODEL), lambda i: (0, 0)),
                  pl.BlockSpec((rows, D_MODEL), lambda i: (i, 0))],
        out_specs=[pl.BlockSpec((rows, D_MODEL), lambda i: (i, 0)), pl.BlockSpec((N_MEM, 2 * D_MODEL), lambda i: (0, 0))],
        out_shape=[jax.ShapeDtypeStruct((t, D_MODEL), BF), jax.ShapeDtypeStruct((N_MEM, 2 * D_MODEL), F32)],
        compiler_params=_params(("arbitrary",)),
    )(cq, ckv, dco)


CONV_TILE = 512
CONV_CHUNK = 128
HALO = 32


def _conv_fwd(z, w32, bias, t, side=None):
    tile = min(CONV_TILE, t)
    a_cb, b_cb = GLU_A_COL // LANES, GLU_B_COL // LANES
    hb = tile // HALO

    def kern(a_ref, b_ref, ah_ref, bh_ref, w_ref, bias_ref, o_ref, g_scr):
        i = pl.program_id(1)
        g_scr[HALO:, :] = _glu(a_ref[...], b_ref[...])
        g_scr[:HALO, :] = _glu(ah_ref[...], bh_ref[...]) * jnp.where(i > 0, 1.0, 0.0)
        for c in range(tile // CONV_CHUNK):
            acc = jnp.broadcast_to(bias_ref[...], (CONV_CHUNK, LANES))
            for j in range(CONV_K):
                lo = c * CONV_CHUNK + HALO - (CONV_K - 1) + j
                acc = acc + w_ref[j:j + 1, :] * g_scr[lo:lo + CONV_CHUNK, :]
            o_ref[c * CONV_CHUNK:(c + 1) * CONV_CHUNK, :] = acc

    def cur(cb):
        return pl.BlockSpec((tile, LANES), lambda j, i: (i, cb + j))

    def prev(cb):
        return pl.BlockSpec((HALO, LANES), lambda j, i: (jnp.maximum(i * hb - 1, 0), cb + j))

    return _pcall("conv_fwd", kern, (CONV_CH // LANES, t // tile),
                  [cur(a_cb), cur(b_cb), prev(a_cb), prev(b_cb),
                   pl.BlockSpec((CONV_KP, LANES), lambda j, i: (0, j)), pl.BlockSpec((1, LANES), lambda j, i: (0, j))],
                  [pl.BlockSpec((tile, LANES), lambda j, i: (i, j))], [jax.ShapeDtypeStruct((t, CONV_CH), F32)],
                  [pltpu.VMEM((tile + HALO, LANES), F32)], ("parallel", "parallel"), (z, z, z, z, w32, bias), side)


def _conv_bwd(z, dc1, w32, t, side=None):
    tile = min(CONV_TILE, t)
    a_cb, b_cb = GLU_A_COL // LANES, GLU_B_COL // LANES
    hb = tile // HALO
    n_tiles = t // tile
    n_chunks = tile // CONV_CHUNK

    def kern(a_ref, b_ref, ah_ref, bh_ref, d_ref, dn_ref, w_ref, da_ref, db_ref, dw_ref, g_scr, d_scr):
        i = pl.program_id(1)
        g_scr[HALO:, :] = _glu(a_ref[...], b_ref[...])
        g_scr[:HALO, :] = _glu(ah_ref[...], bh_ref[...]) * jnp.where(i > 0, 1.0, 0.0)
        d_scr[:tile, :] = d_ref[...]
        d_scr[tile:, :] = dn_ref[...] * jnp.where(i < n_tiles - 1, 1.0, 0.0)

        @pl.when(i == 0)
        def _():
            dw_ref[...] = jnp.zeros_like(dw_ref)

        for c in range(n_chunks):
            cs = slice(c * CONV_CHUNK, (c + 1) * CONV_CHUNK)
            acc = jnp.zeros((CONV_CHUNK, LANES), F32)
            for j in range(CONV_K):
                lo = c * CONV_CHUNK + (CONV_K - 1) - j
                acc = acc + w_ref[j:j + 1, :] * d_scr[lo:lo + CONV_CHUNK, :]
            sgc = _sig(b_ref[cs, :].astype(F32))
            da_ref[cs, :] = (acc * sgc).astype(BF)
            db_ref[cs, :] = (acc * a_ref[cs, :].astype(F32) * sgc * (1.0 - sgc)).astype(BF)
        for j in range(CONV_K):
            tot = jnp.zeros((SUBLANES, LANES), F32)
            for c in range(n_chunks):
                lo = c * CONV_CHUNK + HALO - (CONV_K - 1) + j
                prod = d_ref[c * CONV_CHUNK:(c + 1) * CONV_CHUNK, :] * g_scr[lo:lo + CONV_CHUNK, :]
                tot = tot + jnp.sum(prod.reshape(CONV_CHUNK // SUBLANES, SUBLANES, LANES), axis=0)
            dw_ref[j:j + 1, :] += jnp.sum(tot, axis=0, keepdims=True)
        dw_ref[CONV_K:CONV_KP, :] += jnp.sum(d_ref[...], axis=0, keepdims=True)

    def cur(cb):
        return pl.BlockSpec((tile, LANES), lambda j, i: (i, cb + j))

    def prev(cb):
        return pl.BlockSpec((HALO, LANES), lambda j, i: (jnp.maximum(i * hb - 1, 0), cb + j))

    return _pcall(
        "conv_bwd", kern, (CONV_CH // LANES, n_tiles),
        [cur(a_cb), cur(b_cb), prev(a_cb), prev(b_cb), cur(0),
         pl.BlockSpec((HALO, LANES), lambda j, i: (jnp.minimum((i + 1) * hb, t // HALO - 1), j)),
         pl.BlockSpec((CONV_KP, LANES), lambda j, i: (0, j))],
        [pl.BlockSpec((tile, LANES), lambda j, i: (i, j)), pl.BlockSpec((tile, LANES), lambda j, i: (i, j)),
         pl.BlockSpec((CONV_KP, LANES), lambda j, i: (0, j))],
        [jax.ShapeDtypeStruct((t, CONV_CH), BF), jax.ShapeDtypeStruct((t, CONV_CH), BF),
         jax.ShapeDtypeStruct((CONV_KP, CONV_CH), F32)],
        [pltpu.VMEM((tile + HALO, LANES), F32), pltpu.VMEM((tile + HALO, LANES), F32)],
        ("parallel", "arbitrary"), (z, z, z, z, dc1, dc1, w32), side)


def _adam_fn(w, g, m, v):
    m = ADAM_B1 * m + (1.0 - ADAM_B1) * g
    v = ADAM_B2 * v + (1.0 - ADAM_B2) * (g * g)
    m_hat = m / (1.0 - ADAM_B1 ** ADAM_STEP)
    v_hat = v / (1.0 - ADAM_B2 ** ADAM_STEP)
    delta = -ADAM_LR * (m_hat / (jnp.sqrt(v_hat) + ADAM_EPS) + ADAM_WD * w)
    return delta, m, v


def _adam(name, w, g, m, v):
    rows, cols = w.shape
    tile = _ew_tile(rows, cols)
    return _rowcall(name, _adam_fn, rows, tile, [(a, cols, 0) for a in (w, g, m, v)], [], [(cols, F32)] * 3)


def _place():
    x, y, c = lax.axis_index("x"), lax.axis_index("y"), lax.axis_index("c")
    chips = [(1 - x, y), (x, 1 - y), (1 - x, 1 - y)]
    return x, y, c, chips


def _gather_side(shards):
    nw = len(shards)
    chip = 2 * lax.axis_index("x") + lax.axis_index("y")
    staged = [lax.dynamic_update_index_in_dim(jnp.zeros((4,) + s.shape, s.dtype), s, chip, 0) for s in shards]

    def build(_, outs, send_sems, recv_sems):
        x, y, c, chips = _place()
        me = 2 * x + y
        sibling = (x, y, 1 - c)

        def half(w, lead, h):
            n = shards[w].shape[0] // 2
            return outs[w].at[lead, pl.ds(h * n, n)]

        def copy(w, k, part, to):
            return pltpu.make_async_remote_copy(src_ref=part, dst_ref=part, send_sem=send_sems.at[6 * w + k],
                                                recv_sem=recv_sems.at[6 * w + k], device_id=to, device_id_type=MESH)

        def start():
            for w in range(nw):
                for k, (px, py) in enumerate(chips):
                    copy(w, k, half(w, me, c), (px, py, c)).start()

        def finish():
            for w in range(nw):
                for k, (px, py) in enumerate(chips):
                    landed = half(w, 2 * px + py, c)
                    copy(w, k, landed, (px, py, c)).wait_recv()
                    copy(w, 3 + k, landed, sibling).start()
            for w in range(nw):
                for k, (px, py) in enumerate(chips):
                    copy(w, 3 + k, half(w, 2 * px + py, 1 - c), sibling).wait_recv()
            for w in range(nw):
                for k, (px, py) in enumerate(chips):
                    copy(w, k, half(w, me, c), (px, py, c)).wait_send()
                    copy(w, 3 + k, half(w, 2 * px + py, c), sibling).wait_send()

        return start, finish

    return _Side(staged, [jax.ShapeDtypeStruct((4,) + s.shape, s.dtype) for s in shards], 6 * nw, build,
                 aliases={w: w for w in range(nw)})


def _gather8(name, v, side=None):
    rows = v.shape[0]

    def body(v_ref, all_ref, sum_ref, send_sems, recv_sems):
        x, y, c, _ = _place()
        me = 4 * x + 2 * y + c
        all_ref[me] = v_ref[...]
        copies = []
        for k in range(1, 8):
            px, py, pc = x ^ (k >> 2), y ^ ((k >> 1) & 1), c ^ (k & 1)
            copies.append(pltpu.make_async_remote_copy(
                src_ref=v_ref, dst_ref=all_ref.at[me], send_sem=send_sems.at[k - 1], recv_sem=recv_sems.at[k - 1],
                device_id=(px, py, pc), device_id_type=MESH))
            copies[-1].start()
        for k in range(1, 8):
            px, py, pc = x ^ (k >> 2), y ^ ((k >> 1) & 1), c ^ (k & 1)
            theirs = all_ref.at[4 * px + 2 * py + pc]
            pltpu.make_async_remote_copy(
                src_ref=theirs, dst_ref=theirs, send_sem=send_sems.at[k - 1], recv_sem=recv_sems.at[k - 1],
                device_id=(px, py, pc), device_id_type=MESH).wait_recv()
        for cp in copies:
            cp.wait_send()
        tot = all_ref[0]
        for d in range(1, 8):
            tot = tot + all_ref[d]
        sum_ref[...] = tot

    vm = pl.BlockSpec(memory_space=pltpu.VMEM)
    return _pcall(name, body, (), [vm], [vm, vm],
                  [jax.ShapeDtypeStruct((8, rows, LANES), F32), jax.ShapeDtypeStruct((rows, LANES), F32)],
                  [pltpu.SemaphoreType.DMA((7,)), pltpu.SemaphoreType.DMA((7,))], None, (v,), side)


def _region(ref, col_sharded, shape, j, h):
    r, ccols = shape
    if col_sharded:
        return ref.at[pl.ds(h * (r // 2), r // 2), pl.ds(j * (ccols // 4), ccols // 4)]
    n = r // 8
    return ref.at[pl.ds((2 * j + h) * n, n), :]


def _region_shape(col_sharded, shape):
    r, ccols = shape
    return (r // 2, ccols // 4) if col_sharded else (r // 8, ccols)


def _exchange(copies):
    def build(ins, outs, send_sems, recv_sems):
        def start():
            for cp in copies(ins, outs, send_sems, recv_sems):
                cp.start()

        def finish():
            for cp in copies(ins, outs, send_sems, recv_sems):
                cp.wait()

        return start, finish
    return build


def _swap_side(grads, kinds):
    nw = len(grads)

    def copies(ins, theirs, send_sems, recv_sems):
        x, y, c, _ = _place()
        return [pltpu.make_async_remote_copy(
            src_ref=_region(ins[w], kinds[w], grads[w].shape, j, 1 - c), dst_ref=theirs[w].at[j],
            send_sem=send_sems.at[4 * w + j], recv_sem=recv_sems.at[4 * w + j], device_id=(x, y, 1 - c), device_id_type=MESH)
            for w in range(nw) for j in range(4)]

    shapes = [jax.ShapeDtypeStruct((4,) + _region_shape(kinds[w], grads[w].shape), F32) for w in range(nw)]
    return _Side(grads, shapes, 4 * nw, _exchange(copies))


def _scatter_side(parts):
    nw = len(parts)

    def copies(ins, outs, send_sems, recv_sems):
        x, y, c, chips = _place()
        return [pltpu.make_async_remote_copy(
            src_ref=ins[w].at[2 * px + py], dst_ref=outs[w].at[k], send_sem=send_sems.at[3 * w + k],
            recv_sem=recv_sems.at[3 * w + k], device_id=(px, py, c), device_id_type=MESH)
            for w in range(nw) for k, (px, py) in enumerate(chips)]

    shapes = [jax.ShapeDtypeStruct((3,) + p.shape[1:], p.dtype) for p in parts]
    return _Side(parts, shapes, 3 * nw, _exchange(copies))


def _share_side(halves):
    nw = len(halves)

    def copies(ins, outs, send_sems, recv_sems):
        x, y, c, _ = _place()
        return [pltpu.make_async_remote_copy(
            src_ref=ins[w], dst_ref=outs[w].at[c], send_sem=send_sems.at[w], recv_sem=recv_sems.at[w],
            device_id=(x, y, 1 - c), device_id_type=MESH) for w in range(nw)]

    return _Side(halves, [jax.ShapeDtypeStruct((2,) + h.shape, F32) for h in halves], nw, _exchange(copies))


EW_BLOCK = 512 * 1024


def _ew_tile(rows, cols):
    limit = max(8, EW_BLOCK // cols)
    return max(d for d in range(8, min(rows, limit) + 1, 8) if rows % d == 0)


def _indexed_sum(name, fn, grid, in_specs, out_specs, out_shape, index, arrays):
    def kern(_, *refs):
        n_in = len(in_specs)
        vals = fn(*[r[...] for r in refs[:n_in]])
        for r, v in zip(refs[n_in:], vals):
            r[...] = v.astype(r.dtype)

    return pl.pallas_call(
        kern, name=name, out_shape=out_shape,
        grid_spec=pltpu.PrefetchScalarGridSpec(num_scalar_prefetch=1, grid=grid, in_specs=in_specs, out_specs=out_specs),
        compiler_params=_params(("parallel",) * len(grid)),
    )(index.astype(jnp.int32).reshape(1), *arrays)


def _pair_sums(names, grads, theirs):
    parts, parts_bf = {}, {}
    for n, other in zip(names, theirs):
        _, rr, cc = other.shape
        tile = _ew_tile(rr, cc)
        nb = rr // tile
        if COL_SHARDED[n]:
            mine = pl.BlockSpec((tile, cc), lambda j, i, c: (c[0] * nb + i, j))
        else:
            mine = pl.BlockSpec((tile, cc), lambda j, i, c: ((2 * j + c[0]) * nb + i, 0))
        flat = pl.BlockSpec((tile, cc), lambda j, i, c: (j * nb + i, 0))
        p, pb = _indexed_sum(f"grad_pair_sum_{n}", lambda u, v: (u + v, u + v), (4, nb), [mine, flat], [flat, flat],
                             [jax.ShapeDtypeStruct((4 * rr, cc), F32), jax.ShapeDtypeStruct((4 * rr, cc), BF)],
                             lax.axis_index("c"), (grads[n], other.reshape(4 * rr, cc)))
        parts[n], parts_bf[n] = p.reshape(4, rr, cc), pb.reshape(4, rr, cc)
    return parts, parts_bf


def _chip_sums(names, parts, landed):
    halves = {}
    for n, got in zip(names, landed):
        _, rr, cc = got.shape
        tile = _ew_tile(rr, cc)
        nb = rr // tile
        own = pl.BlockSpec((tile, cc), lambda i, chip: (chip[0] * nb + i, 0))
        peer = lambda k: pl.BlockSpec((tile, cc), lambda i, chip: (k * nb + i, 0))
        halves[n] = _indexed_sum(f"grad_chip_sum_{n}", lambda o, k0, k1, k2: (((o + k0) + k1) + k2,), (nb,),
                                 [own, peer(0), peer(1), peer(2)], [pl.BlockSpec((tile, cc), lambda i, chip: (i, 0))],
                                 [jax.ShapeDtypeStruct((rr, cc), F32)], 2 * lax.axis_index("x") + lax.axis_index("y"),
                                 (parts[n].reshape(4 * rr, cc),) + (got.reshape(3 * rr, cc),) * 3)[0]
    return halves


def _both_halves(mine, shared):
    both = lax.dynamic_update_index_in_dim(shared, mine, lax.axis_index("c"), 0)
    return both.reshape(2 * mine.shape[0], mine.shape[1])


BIG = ("w_in", "w_attn_proj", "w_conv_proj", "w_out", "w_cq", "w_ckv", "w_co", "w_up", "w_down")
COL_SHARDED = {"w_in": True, "w_attn_proj": True, "w_conv_proj": True, "w_out": False, "w_cq": False,
               "w_ckv": True, "w_co": False, "w_up": True, "w_down": False}
SMALL = ("g_mix", "b_gate", "conv_b", "conv_ln_g", "conv_ln_b", "g_cross", "g_mem", "g_mlp", "g_final")
ORDER = ("g_mix", "w_in", "b_gate", "conv_w", "conv_b", "conv_ln_g", "conv_ln_b", "w_attn_proj", "w_conv_proj", "w_out",
         "g_cross", "g_mem", "w_cq", "w_ckv", "w_co", "g_mlp", "w_up", "w_down", "g_final")


def _pad_rows(flat, rows):
    return jnp.pad(flat, (0, rows * LANES - flat.shape[0])).reshape(rows, LANES)


WAVE_MLP = ("w_down", "w_up")
WAVE_MID = ("w_co", "w_cq", "w_ckv", "w_out", "w_attn_proj", "w_conv_proj")


def _local_step(x, mem, tgt, shards, small, conv_w_full):
    t = x.shape[0]
    tr = 512
    c_tab, s_tab = _rope_tables(t)
    row = lambda v: v.reshape(1, -1)
    g_mix, g_cross, g_mem, g_mlp, g_final = (row(small[n]) for n in ("g_mix", "g_cross", "g_mem", "g_mlp", "g_final"))
    b_gate, conv_b, ln_g, ln_b = (row(small[n]) for n in ("b_gate", "conv_b", "conv_ln_g", "conv_ln_b"))
    w32 = jnp.pad(conv_w_full, ((0, CONV_KP - CONV_K), (0, 0)))

    (u,), (w_in_all,) = _rowcall("mix_norm", lambda a, g: (_rms(a, g),), t, tr, [(x, D_MODEL, 0)], [g_mix], [(D_MODEL, BF)],
                                 side=_gather_side([shards["w_in"]]))
    wfull = {"w_in": w_in_all}

    def keep(names, gathered):
        for n, g in zip(names, gathered):
            wfull[n] = g if COL_SHARDED[n] else g.reshape(1, 4 * g.shape[1], g.shape[2])

    z, gathered = _mm("in_proj", u, w_in_all, out_dtypes=(BF,), side=_gather_side([shards[n] for n in WAVE_MID]))
    keep(WAVE_MID, gathered)
    (c1,), gathered = _conv_fwd(z, w32, conv_b, t, side=_gather_side([shards[n] for n in WAVE_MLP]))
    keep(WAVE_MLP, gathered)
    qkv = _rope_split(z, c_tab, s_tab, t)
    qkv_cls, outs, lses = [], [], []
    for g, dil in enumerate(DILATIONS):
        ops = tuple((a.reshape(t, GROUP_W), 0) for a in qkv[3 * g:3 * g + 3])
        qkv_cls.append(ops)
        o_g, l_g = _attn_fwd(f"attn_fwd_{g}", *ops, t, dil)
        outs.append(o_g)
        lses.append(l_g)
    attn, lse = _merge_classes(outs, lses, t)
    y_attn = _mm("attn_proj", attn, wfull["w_attn_proj"])
    (c2,) = _rowcall("conv_ln_silu", _ln_silu_fn, t, tr, [(c1, CONV_CH, 0)], [ln_g, ln_b], [(CONV_CH, BF)])
    tn_cp = wfull["w_conv_proj"].shape[2]
    merged, y_conv = _mm("conv_proj", c2, wfull["w_conv_proj"], epi=_gate_epi, out_dtypes=(BF, F32),
                         extras=(y_attn, (z, GATE_A_COL // tn_cp), (z, GATE_B_COL // tn_cp)),
                         params=((b_gate, 0), (b_gate, D_MODEL // tn_cp)))
    x1, uq = _mm("out_proj", merged, wfull["w_out"], extras=(x,), params=(g_cross,), epi=_res_norm_epi, out_dtypes=(F32, BF))
    (mn,) = _rowcall("mem_norm", lambda a, g: (_rms(a, g),), N_MEM, N_MEM, [(mem, D_MODEL, 0)], [g_mem], [(D_MODEL, BF)])
    cq = _mm("cross_q", uq, wfull["w_cq"], out_dtypes=(BF,))
    ckv = _mm("cross_kv", mn, wfull["w_ckv"], out_dtypes=(BF,))
    co = _cross_fwd(cq, ckv, t)
    x2, um = _mm("cross_out", co, wfull["w_co"], extras=(x1,), params=(g_mlp,), epi=_res_norm_epi, out_dtypes=(F32, BF))
    hact = _mm("mlp_up", um, wfull["w_up"], out_dtypes=(BF,), epi=lambda acc: (jnp.square(jnp.maximum(acc, 0.0)),))
    d3, d3b, loss_row, dg_final = _mm("mlp_down", hact, wfull["w_down"], extras=(x2, tgt), params=(g_final,), epi=_final_epi,
                                      out_dtypes=(F32, BF), sums=[(1, LANES), (1, D_MODEL)])

    gw = {}
    dhp = _mm("mlp_down_bwd", d3b, wfull["w_down"], nt=True, extras=(hact,), out_dtypes=(BF,),
              epi=lambda acc, h: (acc * 2.0 * jnp.sqrt(h.astype(F32)),))
    gw["w_down"] = _mm_tn("mlp_down_wgrad", hact, d3b)
    gw["w_up"] = _mm_tn("mlp_up_wgrad", um, dhp)

    def swap_of(names):
        return _swap_side([gw[n] for n in names], [COL_SHARDED[n] for n in names])

    (d2, d2b, dg_mlp), theirs = _mm("mlp_up_bwd", dhp, wfull["w_up"], nt=True, extras=(x2, d3), params=(g_mlp,),
                                    epi=_norm_bwd_epi, out_dtypes=(F32, BF), sums=[(1, D_MODEL)], side=swap_of(WAVE_MLP))
    parts_mlp, parts_bf_mlp = _pair_sums(WAVE_MLP, gw, theirs)

    dco = _mm("cross_out_bwd", d2b, wfull["w_co"], nt=True, out_dtypes=(BF,))
    gw["w_co"] = _mm_tn("cross_out_wgrad", co, d2b)
    dcq, dckv = _cross_bwd(cq, ckv, dco, t)
    gw["w_cq"] = _mm_tn("cross_q_wgrad", uq, dcq)
    d1, d1b, dg_cross = _mm("cross_q_bwd", dcq, wfull["w_cq"], nt=True, extras=(x1, d2), params=(g_cross,), epi=_norm_bwd_epi,
                            out_dtypes=(F32, BF), sums=[(1, D_MODEL)])
    gw["w_ckv"] = _mm_tn("cross_kv_wgrad", mn, dckv, tk=N_MEM)
    dmn = _mm("cross_kv_bwd", dckv, wfull["w_ckv"], nt=True)
    (dg_mem,) = _rowcall("mem_norm_bwd", lambda a, dn, g: (_rms_bwd(a, g, dn)[1],), N_MEM, N_MEM,
                         [(mem, D_MODEL, 0), (dmn, D_MODEL, 0)], [g_mem], [], accs=[(1, D_MODEL)])

    dya, dyc, dgate_a, dgate_b, dbg_a, dbg_b = _mm(
        "out_proj_bwd", d1b, wfull["w_out"], nt=True, tm=min(512, t), epi=_gate_bwd_epi, out_dtypes=(BF,) * 4,
        extras=(y_attn, y_conv, (z, GATE_A_COL // D_MODEL), (z, GATE_B_COL // D_MODEL)), params=(b_gate,),
        sums=[(1, D_MODEL), (1, D_MODEL)])
    dbg = jnp.concatenate([dbg_a, dbg_b], axis=1)
    gw["w_out"] = _mm_tn("out_proj_wgrad", merged, d1b)
    gw["w_attn_proj"] = _mm_tn("attn_proj_wgrad", attn, dya)
    dattn = _mm("attn_proj_bwd", dya, wfull["w_attn_proj"], nt=True)
    gw["w_conv_proj"] = _mm_tn("conv_proj_wgrad", c2, dyc)
    dc2 = _mm("conv_proj_bwd", dyc, wfull["w_conv_proj"], nt=True)
    (dc1, dlng, dlnb), theirs = _rowcall("conv_ln_silu_bwd", _ln_silu_bwd_fn, t, tr, [(c1, CONV_CH, 0), (dc2, CONV_CH, 0)],
                                         [ln_g, ln_b], [(CONV_CH, F32)], accs=[(1, CONV_CH), (1, CONV_CH)],
                                         side=swap_of(WAVE_MID))
    parts_mid, parts_bf_mid = _pair_sums(WAVE_MID, gw, theirs)
    (dglu_a, dglu_b, dconv), landed = _conv_bwd(z, dc1, w32, t, side=_scatter_side([parts_bf_mlp[n] for n in WAVE_MLP]))
    halves_mlp = _chip_sums(WAVE_MLP, parts_mlp, landed)

    dl0, cls1, cls2 = _attn_bwd_prep(dattn, attn, lse, t)
    dqs, dks, dvs = [], [], []
    for g, (dil, (da_c, dl_c, ls_c)) in enumerate(zip(DILATIONS, ((dattn, dl0, lse), cls1, cls2))):
        res = _attn_bwd(f"attn_bwd_{g}", *qkv_cls[g], da_c, dl_c, ls_c, t, dil,
                        side=_share_side([halves_mlp[n] for n in WAVE_MLP]) if g == 0 else None)
        if g == 0:
            res, others = res
            gshard = {n: _both_halves(halves_mlp[n], o) for n, o in zip(WAVE_MLP, others)}
        dqs.append(res[0])
        dks.append(res[1])
        dvs.append(res[2])
    (dz,), landed = _rope_bwd_join(dqs, dks, dvs, c_tab, s_tab, (dglu_a, dglu_b, dgate_a, dgate_b), t,
                                   side=_scatter_side([parts_bf_mid[n] for n in WAVE_MID]))
    halves_mid = _chip_sums(WAVE_MID, parts_mid, landed)

    gw_in_full, others = _mm_tn("in_proj_wgrad", u, dz, side=_share_side([halves_mid[n] for n in WAVE_MID]))
    gshard.update({n: _both_halves(halves_mid[n], o) for n, o in zip(WAVE_MID, others)})
    gw_in = {"w_in": gw_in_full}
    du_a, theirs = _mm("in_proj_bwd_a", dz, wfull["w_in"], nt=True, k_part=(0, 2), side=_swap_side([gw_in["w_in"]], [True]))
    parts, parts_bf = _pair_sums(("w_in",), gw_in, theirs)
    (gx, dg_mix), landed_in = _mm("in_proj_bwd_b", dz, wfull["w_in"], nt=True, k_part=(1, 2), extras=(du_a, x, d1),
                                  params=(g_mix,), epi=lambda acc, first, a, dres, g: _norm_bwd_epi(acc + first, a, dres, g)[1:],
                                  sums=[(1, D_MODEL)], side=_scatter_side([parts_bf["w_in"]]))
    halves = _chip_sums(("w_in",), parts, landed_in)

    gsmall = {"g_mix": dg_mix, "b_gate": dbg, "conv_b": dconv[CONV_K:CONV_K + 1], "conv_ln_g": dlng, "conv_ln_b": dlnb,
              "g_cross": dg_cross, "g_mem": dg_mem, "g_mlp": dg_mlp, "g_final": dg_final, "conv_w": dconv[:CONV_K]}
    return loss_row, gx, gshard, gsmall, halves["w_in"]


def kernel(x, mem, g_mix, w_in, b_gate, conv_w, conv_b, conv_ln_g, conv_ln_b, w_attn_proj, w_conv_proj, w_out, g_cross, g_mem, w_cq, w_ckv, w_co, g_mlp, w_up, w_down, g_final, loss_target, m_g_mix, m_w_in, m_b_gate, m_conv_w, m_conv_b, m_conv_ln_g, m_conv_ln_b, m_w_attn_proj, m_w_conv_proj, m_w_out, m_g_cross, m_g_mem, m_w_cq, m_w_ckv, m_w_co, m_g_mlp, m_w_up, m_w_down, m_g_final, v_g_mix, v_w_in, v_b_gate, v_conv_w, v_conv_b, v_conv_ln_g, v_conv_ln_b, v_w_attn_proj, v_w_conv_proj, v_w_out, v_g_cross, v_g_mem, v_w_cq, v_w_ckv, v_w_co, v_g_mlp, v_w_up, v_w_down, v_g_final):
    w = dict(g_mix=g_mix, w_in=w_in, b_gate=b_gate, conv_w=conv_w, conv_b=conv_b, conv_ln_g=conv_ln_g, conv_ln_b=conv_ln_b,
             w_attn_proj=w_attn_proj, w_conv_proj=w_conv_proj, w_out=w_out, g_cross=g_cross, g_mem=g_mem, w_cq=w_cq,
             w_ckv=w_ckv, w_co=w_co, g_mlp=g_mlp, w_up=w_up, w_down=w_down, g_final=g_final)
    mo = dict(g_mix=m_g_mix, w_in=m_w_in, b_gate=m_b_gate, conv_w=m_conv_w, conv_b=m_conv_b, conv_ln_g=m_conv_ln_g,
              conv_ln_b=m_conv_ln_b, w_attn_proj=m_w_attn_proj, w_conv_proj=m_w_conv_proj, w_out=m_w_out, g_cross=m_g_cross,
              g_mem=m_g_mem, w_cq=m_w_cq, w_ckv=m_w_ckv, w_co=m_w_co, g_mlp=m_g_mlp, w_up=m_w_up, w_down=m_w_down,
              g_final=m_g_final)
    vo = dict(g_mix=v_g_mix, w_in=v_w_in, b_gate=v_b_gate, conv_w=v_conv_w, conv_b=v_conv_b, conv_ln_g=v_conv_ln_g,
              conv_ln_b=v_conv_ln_b, w_attn_proj=v_w_attn_proj, w_conv_proj=v_w_conv_proj, w_out=v_w_out, g_cross=v_g_cross,
              g_mem=v_g_mem, w_cq=v_w_cq, w_ckv=v_w_ckv, w_co=v_w_co, g_mlp=v_g_mlp, w_up=v_w_up, w_down=v_w_down,
              g_final=v_g_final)
    shapes = {n: w[n].shape for n in ORDER}
    two_d = lambda a: a.reshape(a.shape[-2], a.shape[-1])
    chip = 2 * lax.axis_index("x") + lax.axis_index("y")

    shards = {n: two_d(w[n]).astype(BF) for n in BIG}
    cw_rows = 48
    cw_all, _ = _gather8("gather_conv_w", _pad_rows(conv_w.reshape(-1), cw_rows))
    cw_shard = CONV_K * (CONV_CH // 4)
    conv_w_full = jnp.concatenate(
        [cw_all[2 * j].reshape(-1)[:cw_shard].reshape(CONV_K, CONV_CH // 4) for j in range(4)], axis=1)

    small = {n: w[n] for n in SMALL}
    loss_row, gx, gshard, gsmall, half_w_in = _local_step(two_d(x), two_d(mem), two_d(loss_target), shards, small,
                                                          conv_w_full)
    loss = lax.psum(loss_row[0, 0], ("x", "y", "c"))

    small_names = SMALL + ("conv_w",)
    flat = jnp.concatenate([gsmall[n].reshape(-1) for n in small_names])
    sm_rows = -(-flat.shape[0] // (8 * LANES)) * 8
    (_, sm_sum), others = _gather8("reduce_small_grads", _pad_rows(flat, sm_rows), side=_share_side([half_w_in]))
    gshard["w_in"] = _both_halves(half_w_in, others[0])
    sm_sum = sm_sum.reshape(-1)
    off = 0
    for n in small_names:
        size = gsmall[n].size
        gshard[n] = sm_sum[off:off + size].reshape(gsmall[n].shape)
        off += size
    gshard["conv_w"] = lax.dynamic_slice_in_dim(gshard["conv_w"], chip * (CONV_CH // 4), CONV_CH // 4, axis=1)

    grads, deltas, new_m, new_v = {}, {}, {}, {}
    for n in BIG:
        d, m2, v2 = _adam(f"adamw_{n}", two_d(w[n]), gshard[n], two_d(mo[n]), two_d(vo[n]))
        grads[n], deltas[n], new_m[n], new_v[n] = (a.reshape(shapes[n]) for a in (gshard[n], d, m2, v2))
    pack = lambda src: jnp.concatenate([src[n].reshape(-1) for n in small_names])
    n_small = sum(w[n].size for n in small_names)
    ad_rows = -(-n_small // (8 * LANES)) * 8
    d, m2, v2 = _adam("adamw_small", *[_pad_rows(pack(src), ad_rows) for src in (w, gshard, mo, vo)])
    off = 0
    for n in small_names:
        size = w[n].size
        grads[n] = gshard[n].reshape(shapes[n])
        deltas[n], new_m[n], new_v[n] = (a.reshape(-1)[off:off + size].reshape(shapes[n]) for a in (d, m2, v2))
        off += size

    return (loss, gx.reshape(x.shape), *[grads[n] for n in ORDER], *[deltas[n] for n in ORDER],
            *[new_m[n] for n in ORDER], *[new_v[n] for n in ORDER])
```

```python
import functools

import jax
import jax.numpy as jnp
from jax import lax
from jax.experimental import pallas as pl
from jax.experimental.pallas import tpu as pltpu

F32 = jnp.float32
BF = jnp.bfloat16

D_MODEL = 1024
N_MEM = 256
HEAD_DIM = 128
HEADS_PER_GROUP = 4
DILATIONS = (1, 4, 16)
BLK = 128
GROUP_W = HEADS_PER_GROUP * HEAD_DIM
ATTN_WIDTH = 3 * GROUP_W
ROT_DIM = 32
ROPE_THETA = 500000.0
CONV_CH = 768
CONV_K = 31
CONV_KP = 32
IN_WIDTH = 8192
CROSS_HEADS = 4
CROSS_HEAD_DIM = 256
D_FF = 4096
EPS = 1e-6
ATTN_SCALE = HEAD_DIM ** -0.5
CROSS_SCALE = CROSS_HEAD_DIM ** -0.5
NEG = -1e30

ADAM_LR = 0.001
ADAM_B1 = 0.9
ADAM_B2 = 0.999
ADAM_EPS = 1e-08
ADAM_WD = 0.01
ADAM_STEP = 10

LANES = 128
SUBLANES = 8
VMEM_LIMIT = 56 * 1024 * 1024
MESH = pl.DeviceIdType.MESH
ANY = pl.BlockSpec(memory_space=pl.ANY)

GLU_A_COL = 3 * ATTN_WIDTH
GLU_B_COL = GLU_A_COL + CONV_CH
GATE_A_COL = GLU_B_COL + CONV_CH
GATE_B_COL = GATE_A_COL + D_MODEL


def _params(sem=None):
    return pltpu.CompilerParams(dimension_semantics=sem, vmem_limit_bytes=VMEM_LIMIT)


def _dot(a, b):
    return lax.dot_general(a, b, (((1,), (0,)), ((), ())), preferred_element_type=F32)


def _dot_nt(a, b):
    return lax.dot_general(a, b, (((1,), (1,)), ((), ())), preferred_element_type=F32)


def _dot_tn(a, b):
    return lax.dot_general(a, b, (((0,), (0,)), ((), ())), preferred_element_type=F32)


def _sig(x):
    return 1.0 / (1.0 + jnp.exp(-x))


def _glu(a, b):
    return a.astype(F32) * _sig(b.astype(F32))


class _Side:
    def __init__(self, arrays, out_shapes, n_sems, build, aliases=None):
        self.arrays, self.out_shapes, self.n_sems, self.build = list(arrays), list(out_shapes), n_sems, build
        self.aliases = aliases or {}


def _pcall(name, kern, grid, in_specs, out_specs, out_shape, scratch_shapes, sem, args, side=None):
    in_specs, out_specs, out_shape, scratch_shapes = list(in_specs), list(out_specs), list(out_shape), list(scratch_shapes)
    if side is None:
        return pl.pallas_call(kern, name=name, grid=grid, in_specs=in_specs, out_specs=out_specs, out_shape=out_shape,
                              scratch_shapes=scratch_shapes, compiler_params=_params(sem))(*args)
    ni, no, nsc = len(in_specs), len(out_specs), len(scratch_shapes)
    nsi, nso = len(side.arrays), len(side.out_shapes)

    def wrapped(*refs):
        ins, side_ins = refs[:ni], refs[ni:ni + nsi]
        outs, side_outs = refs[ni + nsi:ni + nsi + no], refs[ni + nsi + no:ni + nsi + no + nso]
        scratch = refs[ni + nsi + no + nso:ni + nsi + no + nso + nsc]
        send_sems, recv_sems = refs[-2:]
        start, finish = side.build(side_ins, side_outs, send_sems, recv_sems)
        if grid:
            first = functools.reduce(jnp.logical_and, [pl.program_id(a) == 0 for a in range(len(grid))])
            last = functools.reduce(jnp.logical_and, [pl.program_id(a) == g - 1 for a, g in enumerate(grid)])
            pl.when(first)(start)
            kern(*ins, *outs, *scratch)
            pl.when(last)(finish)
        else:
            start()
            kern(*ins, *outs, *scratch)
            finish()

    res = pl.pallas_call(
        wrapped, name=name, grid=grid, in_specs=in_specs + [ANY] * nsi, out_specs=out_specs + [ANY] * nso,
        out_shape=out_shape + side.out_shapes,
        scratch_shapes=scratch_shapes + [pltpu.SemaphoreType.DMA((side.n_sems,)), pltpu.SemaphoreType.DMA((side.n_sems,))],
        input_output_aliases={ni + k: no + v for k, v in side.aliases.items()},
        compiler_params=_params(("arbitrary",) * len(grid) if grid else None),
    )(*args, *side.arrays)
    return res[:no], res[no:]


def _rowcall(name, fn, n_rows, tile, ins, params, outs, accs=(), side=None):
    tile = min(tile, n_rows)
    ni, npar, no, na = len(ins), len(params), len(outs), len(accs)

    def kern(*refs):
        in_refs = refs[:ni + npar]
        o_refs = refs[ni + npar:ni + npar + no]
        a_refs = refs[ni + npar + no:]
        vals = fn(*[r[...] for r in in_refs])
        for r, v in zip(o_refs, vals[:no]):
            r[...] = v.astype(r.dtype)
        if na:
            @pl.when(pl.program_id(0) == 0)
            def _():
                for r in a_refs:
                    r[...] = jnp.zeros_like(r)
            for r, v in zip(a_refs, vals[no:]):
                r[...] += v

    in_specs = []
    arrays = []
    for spec in ins:
        arr, width, cb = spec[0], spec[1], spec[2]
        rb = spec[3] if len(spec) > 3 else 0
        in_specs.append(pl.BlockSpec((tile, width), functools.partial(lambda i, cb, rb: (i + rb, cb), cb=cb, rb=rb)))
        arrays.append(arr)
    for p in params:
        in_specs.append(pl.BlockSpec(p.shape, lambda i: (0, 0)))
        arrays.append(p)
    out_specs = [pl.BlockSpec((tile, w), lambda i: (i, 0)) for w, _ in outs]
    out_specs += [pl.BlockSpec(s, lambda i: (0, 0)) for s in accs]
    out_shape = [jax.ShapeDtypeStruct((n_rows, w), dt) for w, dt in outs]
    out_shape += [jax.ShapeDtypeStruct(s, F32) for s in accs]
    return _pcall(name, kern, (n_rows // tile,), in_specs, out_specs, out_shape, [],
                  ("arbitrary",) if na else ("parallel",), arrays, side)


def _mm(name, a, w3, *, nt=False, extras=(), params=(), epi=None, out_dtypes=(F32,), sums=(), tm=None, tn=None, tk=None,
        k_part=(0, 1), side=None):
    m, ka = a.shape
    ns, r, cs = w3.shape
    if not nt:
        k_dim, n = r, ns * cs
        tn = tn or min(cs, 1024)
        tk = tk or min(k_dim, 1024)
    else:
        k_dim, n = ns * cs, r
        tn = tn or min(r, 1024)
        tk = tk or min(cs, 1024)
    assert ka == k_dim, (name, a.shape, w3.shape)
    assert not sums or tn == n, name
    nk = k_dim // tk // k_part[1]
    k0 = k_part[0] * nk
    if not nt:
        nbs = cs // tn
        w_spec = pl.BlockSpec((None, tk, tn), lambda i, j, k: (j // nbs, k + k0, j % nbs))
    else:
        kbs = cs // tk
        w_spec = pl.BlockSpec((None, tn, tk), lambda i, j, k: ((k + k0) // kbs, j, (k + k0) % kbs))
    tm = tm or min(m, 1024)
    ne, no, nsum = len(extras) + len(params), len(out_dtypes), len(sums)

    def kern(a_ref, w_ref, *rest):
        e_refs = rest[:ne]
        o_refs = rest[ne:ne + no]
        s_refs = rest[ne + no:ne + no + nsum]

        def part():
            av = a_ref[...].astype(BF)
            return _dot_nt(av, w_ref[...]) if nt else _dot(av, w_ref[...])

        def finish(res):
            vals = epi(res, *[e[...] for e in e_refs]) if epi else (res,)
            for o, v in zip(o_refs, vals[:no]):
                o[...] = v.astype(o.dtype)
            for sr, v in zip(s_refs, vals[no:]):
                sr[...] += v

        if nsum:
            @pl.when(jnp.logical_and(pl.program_id(0) == 0, pl.program_id(2) == 0))
            def _():
                for sr in s_refs:
                    sr[...] = jnp.zeros_like(sr)

        if nk == 1:
            finish(part())
            return
        acc = rest[ne + no + nsum]
        k = pl.program_id(2)

        @pl.when(k == 0)
        def _():
            acc[...] = part()

        @pl.when(jnp.logical_and(k > 0, k < nk - 1))
        def _():
            acc[...] += part()

        @pl.when(k == nk - 1)
        def _():
            finish(acc[...] + part())

    in_specs = [pl.BlockSpec((tm, tk), lambda i, j, k: (i, k + k0)), w_spec]
    split = lambda items: [(it if isinstance(it, tuple) else (it, None)) for it in items]
    extras, params = split(extras), split(params)
    in_specs += [pl.BlockSpec((tm, tn), functools.partial(lambda i, j, k, off: (i, j + off), off=off or 0)) for _, off in extras]
    in_specs += [pl.BlockSpec(p.shape, lambda i, j, k: (0, 0)) if off is None else
                 pl.BlockSpec((p.shape[0], tn), functools.partial(lambda i, j, k, off: (0, j + off), off=off)) for p, off in params]
    extras, params = [e for e, _ in extras], [p for p, _ in params]
    out_specs = [pl.BlockSpec((tm, tn), lambda i, j, k: (i, j)) for _ in out_dtypes]
    out_specs += [pl.BlockSpec(sh, lambda i, j, k: (0, 0)) for sh in sums]
    out_shape = [jax.ShapeDtypeStruct((m, n), dt) for dt in out_dtypes] + [jax.ShapeDtypeStruct(sh, F32) for sh in sums]
    res = _pcall(name, kern, (m // tm, n // tn, nk), in_specs, out_specs, out_shape,
                 [pltpu.VMEM((tm, tn), F32)] if nk > 1 else [],
                 ("arbitrary",) * 3 if nsum else ("parallel", "parallel", "arbitrary"), (a, w3, *extras, *params), side)
    main = res[0] if side is not None else res
    main = main[0] if no + nsum == 1 else main
    return (main, res[1]) if side is not None else main


def _mm_tn(name, a, b, tm=None, tn=None, tk=None, side=None):
    t, ka = a.shape
    _, n = b.shape
    tm = tm or min(ka, 1024)
    tn = tn or min(n, 1024)
    tk = tk or min(t, 1024)

    def kern(a_ref, b_ref, o_ref):
        def part():
            return _dot_tn(a_ref[...].astype(BF), b_ref[...].astype(BF))

        @pl.when(pl.program_id(2) == 0)
        def _():
            o_ref[...] = part()

        @pl.when(pl.program_id(2) > 0)
        def _():
            o_ref[...] += part()

    res = _pcall(name, kern, (ka // tm, n // tn, t // tk),
                 [pl.BlockSpec((tk, tm), lambda i, j, k: (k, i)), pl.BlockSpec((tk, tn), lambda i, j, k: (k, j))],
                 [pl.BlockSpec((tm, tn), lambda i, j, k: (i, j))], [jax.ShapeDtypeStruct((ka, n), F32)], [],
                 ("parallel", "parallel", "arbitrary"), (a, b), side)
    return (res[0][0], res[1]) if side is not None else res[0]


def _rms(x, g):
    return x * lax.rsqrt(jnp.mean(x * x, axis=-1, keepdims=True) + EPS) * g


def _rms_bwd(x, g, dy):
    r = lax.rsqrt(jnp.mean(x * x, axis=-1, keepdims=True) + EPS)
    xh = x * r
    dxh = dy * g
    dx = r * (dxh - xh * jnp.mean(dxh * xh, axis=-1, keepdims=True))
    return dx, jnp.sum(dy * xh, axis=0, keepdims=True)


def _rot(t, c, s):
    lane = lax.broadcasted_iota(jnp.int32, t.shape, 1)
    swapped = jnp.where(lane < ROT_DIM // 2, pltpu.roll(t, HEAD_DIM - ROT_DIM // 2, 1), pltpu.roll(t, ROT_DIM // 2, 1))
    return t * c + swapped * s


def _rope_tables(t):
    half = ROT_DIM // 2
    pos = jnp.arange(t, dtype=F32)
    inv_freq = ROPE_THETA ** (-jnp.arange(0, ROT_DIM, 2, dtype=F32) / ROT_DIM)
    ang = pos[:, None] * inv_freq[None, :]
    cos, sin = jnp.cos(ang), jnp.sin(ang)
    ones = jnp.ones((t, HEAD_DIM - ROT_DIM), F32)
    c_tab = jnp.concatenate([cos, cos, ones], axis=1)
    s_tab = jnp.concatenate([-sin, sin, 0.0 * ones], axis=1)
    return c_tab, s_tab


def _merge_fn(o0, o1, o2, l0, l1, l2):
    m = jnp.maximum(jnp.maximum(l0, l1), l2)
    e0, e1, e2 = jnp.exp(l0 - m), jnp.exp(l1 - m), jnp.exp(l2 - m)
    tot = e0 + e1 + e2
    return (e0 * o0 + e1 * o1 + e2 * o2) / tot, m + jnp.log(tot)


def _ln_parts(c1):
    mu = jnp.mean(c1, axis=-1, keepdims=True)
    xc = c1 - mu
    r = lax.rsqrt(jnp.mean(xc * xc, axis=-1, keepdims=True) + EPS)
    return xc * r, r


def _ln_silu_fn(c1, g, b):
    xh, _ = _ln_parts(c1)
    yl = xh * g + b
    return (yl * _sig(yl),)


def _ln_silu_bwd_fn(c1, dout, g, b):
    xh, r = _ln_parts(c1)
    yl = xh * g + b
    s = _sig(yl)
    dyl = dout * (s + yl * s * (1.0 - s))
    dxh = dyl * g
    dx = r * (dxh - jnp.mean(dxh, axis=-1, keepdims=True) - xh * jnp.mean(dxh * xh, axis=-1, keepdims=True))
    return dx, jnp.sum(dyl * xh, axis=0, keepdims=True), jnp.sum(dyl, axis=0, keepdims=True)


def _gate_epi(yc, ya, ga, gb, ba, bb):
    return _sig(ga + ba) * ya + _sig(gb + bb) * yc, yc


def _gate_bwd_epi(dm, ya, yc, ga, gb, bg):
    sa = _sig(ga + bg[:, :D_MODEL])
    sb = _sig(gb + bg[:, D_MODEL:])
    dga = dm * ya * sa * (1.0 - sa)
    dgb = dm * yc * sb * (1.0 - sb)
    return dm * sa, dm * sb, dga, dgb, jnp.sum(dga, axis=0, keepdims=True), jnp.sum(dgb, axis=0, keepdims=True)


def _res_norm_epi(acc, res, g):
    xn = res + acc
    return xn, _rms(xn, g)


def _norm_bwd_epi(acc, a, dres, g):
    dx, dg = _rms_bwd(a, g, acc)
    return dres + dx, dres + dx, dg


def _final_epi(acc, res, tgt, g):
    return _final_fn(res + acc, tgt, g)


def _final_fn(x3, tgt, g):
    err = _rms(x3, g) - tgt
    lrow = jnp.sum(err * err, axis=-1, keepdims=True) * (0.5 / D_MODEL)
    lsum = jnp.sum(lrow, axis=0, keepdims=True)
    dx, dg = _rms_bwd(x3, g, err * (1.0 / D_MODEL))
    return dx, dx, jnp.broadcast_to(lsum, (1, LANES)), dg


def _attn_geometry(t, dil):
    cls = t // dil
    rows = min(8 * BLK, cls)
    return rows, rows // BLK, cls // rows


def _head_lanes(h):
    return slice(h * HEAD_DIM, (h + 1) * HEAD_DIM)


def _band_mask():
    row = lax.broadcasted_iota(jnp.int32, (BLK, 2 * BLK), 0)
    col = lax.broadcasted_iota(jnp.int32, (BLK, 2 * BLK), 1)
    return jnp.logical_and(col >= row, col <= row + BLK), col


def _stage_window(scr, halo_ref, cur_ref):
    scr[:BLK, :] = halo_ref[...]
    scr[BLK:, :] = cur_ref[...]


def _attn_fwd(name, q, k, v, t, dil):
    rows, nbk, spc = _attn_geometry(t, dil)

    def kern(q_ref, k_ref, kh_ref, v_ref, vh_ref, o_ref, l_ref, k_scr, v_scr):
        i = pl.program_id(0)
        first_shift = jnp.where(i % spc == 0, BLK, 0)
        _stage_window(k_scr, kh_ref, k_ref)
        _stage_window(v_scr, vh_ref, v_ref)
        band, col = _band_mask()
        band_first = jnp.logical_and(band, col >= first_shift)
        for h in range(HEADS_PER_GROUP):
            hs = _head_lanes(h)
            for b in range(nbk):
                rs, win = slice(b * BLK, (b + 1) * BLK), slice(b * BLK, (b + 2) * BLK)
                s = jnp.where(band_first if b == 0 else band, _dot_nt(q_ref[rs, hs], k_scr[win, hs]) * ATTN_SCALE, NEG)
                m = jnp.max(s, axis=1, keepdims=True)
                p = jnp.exp(s - m)
                tot = jnp.sum(p, axis=1, keepdims=True)
                o_ref[rs, hs] = _dot(p.astype(BF), v_scr[win, hs]) / tot
                l_ref[rs, hs] = jnp.broadcast_to(m + jnp.log(tot), (BLK, HEAD_DIM))

    def cur(cb):
        return pl.BlockSpec((rows, GROUP_W), lambda i: (i, cb))

    def halo(cb):
        return pl.BlockSpec((BLK, GROUP_W), lambda i: (jnp.maximum(i * nbk - 1, 0), cb))

    (qa, qc), (ka, kc_), (va, vc_) = q, k, v
    return _pcall(name, kern, (t // rows,), [cur(qc), cur(kc_), halo(kc_), cur(vc_), halo(vc_)],
                  [pl.BlockSpec((rows, GROUP_W), lambda i: (i, 0))] * 2, [jax.ShapeDtypeStruct((t, GROUP_W), F32)] * 2,
                  [pltpu.VMEM((rows + BLK, GROUP_W), BF)] * 2, ("parallel",), (qa, ka, ka, va, va))


def _attn_bwd(name, q, k, v, da, dl, lse, t, dil, side=None):
    rows, nbk, spc = _attn_geometry(t, dil)
    nblk = t // BLK

    def kern(q_ref, qn_ref, k_ref, kh_ref, v_ref, vh_ref, da_ref, dan_ref, dl_ref, dln_ref, ls_ref, lsn_ref,
             dq_ref, dk_ref, dv_ref, k_scr, v_scr):
        i = pl.program_id(0)
        first_shift = jnp.where(i % spc == 0, BLK, 0)
        next_shift = jnp.where((i + 1) % spc == 0, BLK, 0)
        _stage_window(k_scr, kh_ref, k_ref)
        _stage_window(v_scr, vh_ref, v_ref)
        band, col = _band_mask()
        band_first = jnp.logical_and(band, col >= first_shift)
        row1 = lax.broadcasted_iota(jnp.int32, (BLK, BLK), 0)
        col1 = lax.broadcasted_iota(jnp.int32, (BLK, BLK), 1)
        pend_k, pend_v = [None] * HEADS_PER_GROUP, [None] * HEADS_PER_GROUP
        for b in range(nbk):
            rs, win = slice(b * BLK, (b + 1) * BLK), slice(b * BLK, (b + 2) * BLK)
            for h in range(HEADS_PER_GROUP):
                hs = _head_lanes(h)
                qb, dab = q_ref[rs, hs], da_ref[rs, hs].astype(BF)
                kw, vw = k_scr[win, hs], v_scr[win, hs]
                p = jnp.where(band_first if b == 0 else band,
                              jnp.exp(_dot_nt(qb, kw) * ATTN_SCALE - ls_ref[rs, hs][:, :1]), 0.0)
                ds = (p * (_dot_nt(dab, vw) - dl_ref[rs, hs][:, :1]) * ATTN_SCALE).astype(BF)
                dq_ref[rs, hs] = _dot(ds, kw).astype(BF)
                dkw, dvw = _dot_tn(ds, qb), _dot_tn(p.astype(BF), dab)
                if b >= 1:
                    ps = slice((b - 1) * BLK, b * BLK)
                    dk_ref[ps, hs] = (pend_k[h] + dkw[:BLK]).astype(BF)
                    dv_ref[ps, hs] = (pend_v[h] + dvw[:BLK]).astype(BF)
                pend_k[h], pend_v[h] = dkw[BLK:], dvw[BLK:]
        ls_rows = slice((nbk - 1) * BLK, nbk * BLK)
        last = slice(nbk * BLK, (nbk + 1) * BLK)
        for h in range(HEADS_PER_GROUP):
            hs = _head_lanes(h)
            qb, dab = qn_ref[:, hs], dan_ref[:, hs].astype(BF)
            kp, vp = k_scr[last, hs], v_scr[last, hs]
            p = jnp.where(col1 >= row1 + next_shift, jnp.exp(_dot_nt(qb, kp) * ATTN_SCALE - lsn_ref[:, hs]), 0.0)
            ds = (p * (_dot_nt(dab, vp) - dln_ref[:, hs]) * ATTN_SCALE).astype(BF)
            dk_ref[ls_rows, hs] = (pend_k[h] + _dot_tn(ds, qb)).astype(BF)
            dv_ref[ls_rows, hs] = (pend_v[h] + _dot_tn(p.astype(BF), dab)).astype(BF)

    def cur(cb):
        return pl.BlockSpec((rows, GROUP_W), lambda i: (i, cb))

    def prev(cb):
        return pl.BlockSpec((BLK, GROUP_W), lambda i: (jnp.maximum(i * nbk - 1, 0), cb))

    def nxt(cb):
        return pl.BlockSpec((BLK, GROUP_W), lambda i: (jnp.minimum((i + 1) * nbk, nblk - 1), cb))

    (qa, qc), (ka, kc_), (va, vc_) = q, k, v
    return _pcall(name, kern, (t // rows,),
                  [cur(qc), nxt(qc), cur(kc_), prev(kc_), cur(vc_), prev(vc_), cur(0), nxt(0), cur(0), nxt(0), cur(0), nxt(0)],
                  [pl.BlockSpec((rows, GROUP_W), lambda i: (i, 0))] * 3, [jax.ShapeDtypeStruct((t, GROUP_W), BF)] * 3,
                  [pltpu.VMEM((rows + BLK, GROUP_W), BF)] * 2, ("parallel",),
                  (qa, qa, ka, ka, va, va, da, da, dl, dl, lse, lse), side)


CLS_TILE = 512


def _cls_block(t, tile, dil, dtype):
    if dil == 1:
        return pl.BlockSpec((tile, GROUP_W), lambda i: (i, 0)), jax.ShapeDtypeStruct((t, GROUP_W), dtype)
    return (pl.BlockSpec((dil, tile // dil, GROUP_W), lambda i: (0, i, 0)),
            jax.ShapeDtypeStruct((dil, t // dil, GROUP_W), dtype))


def _head_scratch(tile):
    return pltpu.VMEM((tile, HEAD_DIM), F32)


def _rope_split(z, c_tab, s_tab, t):
    tile = min(CLS_TILE, t)
    n_heads = ATTN_WIDTH // HEAD_DIM

    def kern(zq_ref, zk_ref, zv_ref, c_ref, s_ref, *rest):
        outs, scr = rest[:9], rest[9]
        c, s = c_ref[...], s_ref[...]
        for which, z_ref in enumerate((zq_ref, zk_ref, zv_ref)):
            for h in range(n_heads):
                g, hs = h // HEADS_PER_GROUP, _head_lanes(h % HEADS_PER_GROUP)
                val = z_ref[:, h * HEAD_DIM:(h + 1) * HEAD_DIM].astype(F32)
                if which < 2:
                    val = _rot(val, c, s)
                if g == 0:
                    outs[which][:, hs] = val.astype(BF)
                    continue
                scr[...] = val
                dil = DILATIONS[g]
                for r in range(dil):
                    outs[3 * g + which][r, :, hs] = scr[pl.ds(r, tile // dil, stride=dil), :].astype(BF)

    blocks = [_cls_block(t, tile, DILATIONS[g], BF) for g in range(3) for _ in range(3)]
    zspec = lambda cb: pl.BlockSpec((tile, ATTN_WIDTH), lambda i: (i, cb))
    tab = pl.BlockSpec((tile, HEAD_DIM), lambda i: (i, 0))
    return _pcall("rope", kern, (t // tile,), [zspec(0), zspec(1), zspec(2), tab, tab], [b[0] for b in blocks],
                  [b[1] for b in blocks], [_head_scratch(tile)], ("parallel",), (z, z, z, c_tab, s_tab))


def _merge_classes(outs, lses, t):
    tile = min(CLS_TILE, t)

    def kern(o0, l0, o1, l1, o2, l2, attn_ref, lse_ref, s_o1, s_l1, s_o2, s_l2):
        for h in range(HEADS_PER_GROUP):
            hs = _head_lanes(h)
            for src, dst, dil in ((o1, s_o1, DILATIONS[1]), (l1, s_l1, DILATIONS[1]), (o2, s_o2, DILATIONS[2]),
                                  (l2, s_l2, DILATIONS[2])):
                for r in range(dil):
                    dst[pl.ds(r, tile // dil, stride=dil), :] = src[r, :, hs]
            attn_ref[:, hs], lse_ref[:, hs] = _merge_fn(o0[:, hs], s_o1[...], s_o2[...], l0[:, hs], s_l1[...], s_l2[...])

    blocks = [_cls_block(t, tile, DILATIONS[g], F32) for g in range(3)]
    args = []
    for g in range(3):
        args += [outs[g].reshape(blocks[g][1].shape), lses[g].reshape(blocks[g][1].shape)]
    tok = pl.BlockSpec((tile, GROUP_W), lambda i: (i, 0))
    return _pcall("attn_merge", kern, (t // tile,), [blocks[g][0] for g in range(3) for _ in range(2)], [tok, tok],
                  [jax.ShapeDtypeStruct((t, GROUP_W), F32)] * 2, [_head_scratch(tile)] * 4, ("parallel",), args)


def _attn_bwd_prep(dattn, attn, lse, t):
    tile = min(CLS_TILE, t)

    def kern(da_ref, at_ref, ls_ref, dl0, da1, dl1, ls1, da2, dl2, ls2, s_da, s_dl, s_ls):
        for h in range(HEADS_PER_GROUP):
            hs = _head_lanes(h)
            da = da_ref[:, hs]
            s_da[...] = da
            s_dl[...] = jnp.broadcast_to(jnp.sum(da * at_ref[:, hs], axis=1, keepdims=True), (tile, HEAD_DIM))
            s_ls[...] = ls_ref[:, hs]
            dl0[:, hs] = s_dl[...]
            for oda, odl, ols, dil in ((da1, dl1, ls1, DILATIONS[1]), (da2, dl2, ls2, DILATIONS[2])):
                for r in range(dil):
                    rows = pl.ds(r, tile // dil, stride=dil)
                    oda[r, :, hs] = s_da[rows, :].astype(BF)
                    odl[r, :, hs] = s_dl[rows, :]
                    ols[r, :, hs] = s_ls[rows, :]

    tok = pl.BlockSpec((tile, GROUP_W), lambda i: (i, 0))
    blocks = [(tok, jax.ShapeDtypeStruct((t, GROUP_W), F32))]
    for g in (1, 2):
        blocks += [_cls_block(t, tile, DILATIONS[g], BF), _cls_block(t, tile, DILATIONS[g], F32),
                   _cls_block(t, tile, DILATIONS[g], F32)]
    res = _pcall("attn_bwd_prep", kern, (t // tile,), [tok, tok, tok], [b[0] for b in blocks], [b[1] for b in blocks],
                 [_head_scratch(tile)] * 3, ("parallel",), (dattn, attn, lse))
    flat = [a.reshape(t, GROUP_W) for a in res]
    return flat[0], flat[1:4], flat[4:7]


def _rope_bwd_join(dqs, dks, dvs, c_tab, s_tab, tail, t, side=None):
    tile = min(256, t)
    n_heads = ATTN_WIDTH // HEAD_DIM

    def kern(q0, q1, q2, k0, k1, k2, v0, v1, v2, c_ref, s_ref, ga_ref, gb_ref, gta_ref, gtb_ref, dz_ref, scr):
        c, s = c_ref[...], -s_ref[...]
        for which, srcs in enumerate(((q0, q1, q2), (k0, k1, k2), (v0, v1, v2))):
            for h in range(n_heads):
                g, hs = h // HEADS_PER_GROUP, _head_lanes(h % HEADS_PER_GROUP)
                if g == 0:
                    val = srcs[0][:, hs].astype(F32)
                else:
                    dil = DILATIONS[g]
                    for r in range(dil):
                        scr[pl.ds(r, tile // dil, stride=dil), :] = srcs[g][r, :, hs].astype(F32)
                    val = scr[...]
                if which < 2:
                    val = _rot(val, c, s)
                col = which * ATTN_WIDTH + h * HEAD_DIM
                dz_ref[:, col:col + HEAD_DIM] = val.astype(BF)
        dz_ref[:, GLU_A_COL:GLU_B_COL] = ga_ref[...]
        dz_ref[:, GLU_B_COL:GATE_A_COL] = gb_ref[...]
        dz_ref[:, GATE_A_COL:GATE_B_COL] = gta_ref[...]
        dz_ref[:, GATE_B_COL:] = gtb_ref[...]

    blocks = [_cls_block(t, tile, DILATIONS[g], BF) for g in range(3)]
    args = [a.reshape(blocks[g][1].shape) for grp in (dqs, dks, dvs) for g, a in enumerate(grp)]
    tab = pl.BlockSpec((tile, HEAD_DIM), lambda i: (i, 0))
    row = lambda w: pl.BlockSpec((tile, w), lambda i: (i, 0))
    return _pcall("rope_bwd", kern, (t // tile,), [blocks[g][0] for _ in range(3) for g in range(3)]
                  + [tab, tab, row(CONV_CH), row(CONV_CH), row(D_MODEL), row(D_MODEL)], [row(IN_WIDTH)],
                  [jax.ShapeDtypeStruct((t, IN_WIDTH), BF)], [_head_scratch(tile)], ("parallel",),
                  (*args, c_tab, s_tab, *tail), side)


def _cross_probs(qh, kh):
    s = _dot_nt(qh, kh) * CROSS_SCALE
    e = jnp.exp(s - jnp.max(s, axis=1, keepdims=True))
    return e, jnp.sum(e, axis=1, keepdims=True)


def _cross_fwd(cq, ckv, t):
    rows = min(512, t)

    def kern(q_ref, kv_ref, o_ref):
        for h in range(CROSS_HEADS):
            hs = slice(h * CROSS_HEAD_DIM, (h + 1) * CROSS_HEAD_DIM)
            vs = slice(D_MODEL + h * CROSS_HEAD_DIM, D_MODEL + (h + 1) * CROSS_HEAD_DIM)
            e, tot = _cross_probs(q_ref[:, hs], kv_ref[:, hs])
            o_ref[:, hs] = (_dot(e.astype(BF), kv_ref[:, vs]) / tot).astype(BF)

    return pl.pallas_call(
        kern, name="cross_fwd", grid=(t // rows,),
        in_specs=[pl.BlockSpec((rows, D_MODEL), lambda i: (i, 0)), pl.BlockSpec((N_MEM, 2 * D_MODEL), lambda i: (0, 0))],
        out_specs=pl.BlockSpec((rows, D_MODEL), lambda i: (i, 0)),
        out_shape=jax.ShapeDtypeStruct((t, D_MODEL), BF),
        compiler_params=_params(("parallel",)),
    )(cq, ckv)


def _cross_bwd(cq, ckv, dco, t):
    rows = min(512, t)

    def kern(q_ref, kv_ref, do_ref, dq_ref, dkv_ref):
        @pl.when(pl.program_id(0) == 0)
        def _():
            dkv_ref[...] = jnp.zeros_like(dkv_ref)
        for h in range(CROSS_HEADS):
            hs = slice(h * CROSS_HEAD_DIM, (h + 1) * CROSS_HEAD_DIM)
            vs = slice(D_MODEL + h * CROSS_HEAD_DIM, D_MODEL + (h + 1) * CROSS_HEAD_DIM)
            qh, kh, vh, doh = q_ref[:, hs], kv_ref[:, hs], kv_ref[:, vs], do_ref[:, hs]
            e, tot = _cross_probs(qh, kh)
            p = e / tot
            dp = _dot_nt(doh, vh)
            ds = (p * (dp - jnp.sum(p * dp, axis=1, keepdims=True)) * CROSS_SCALE).astype(BF)
            dq_ref[:, hs] = _dot(ds, kh).astype(BF)
            dkv_ref[:, hs] += _dot_tn(ds, qh)
            dkv_ref[:, vs] += _dot_tn(p.astype(BF), doh)

    return pl.pallas_call(
        kern, name="cross_bwd", grid=(t // rows,),
        in_specs=[pl.BlockSpec((rows, D_MODEL), lambda i: (i, 0)), pl.BlockSpec((N_MEM, 2 * D_MODEL), lambda i: (0, 0)),
                  pl.BlockSpec((rows, D_MODEL), lambda i: (i, 0))],
        out_specs=[pl.BlockSpec((rows, D_MODEL), lambda i: (i, 0)), pl.BlockSpec((N_MEM, 2 * D_MODEL), lambda i: (0, 0))],
        out_shape=[jax.ShapeDtypeStruct((t, D_MODEL), BF), jax.ShapeDtypeStruct((N_MEM, 2 * D_MODEL), F32)],
        compiler_params=_params(("arbitrary",)),
    )(cq, ckv, dco)


CONV_TILE = 512
CONV_CHUNK = 128
HALO = 32


def _conv_fwd(z, w32, bias, t, side=None):
    tile = min(CONV_TILE, t)
    a_cb, b_cb = GLU_A_COL // LANES, GLU_B_COL // LANES
    hb = tile // HALO

    def kern(a_ref, b_ref, ah_ref, bh_ref, w_ref, bias_ref, o_ref, g_scr):
        i = pl.program_id(1)
        g_scr[HALO:, :] = _glu(a_ref[...], b_ref[...])
        g_scr[:HALO, :] = _glu(ah_ref[...], bh_ref[...]) * jnp.where(i > 0, 1.0, 0.0)
        for c in range(tile // CONV_CHUNK):
            acc = jnp.broadcast_to(bias_ref[...], (CONV_CHUNK, LANES))
            for j in range(CONV_K):
                lo = c * CONV_CHUNK + HALO - (CONV_K - 1) + j
                acc = acc + w_ref[j:j + 1, :] * g_scr[lo:lo + CONV_CHUNK, :]
            o_ref[c * CONV_CHUNK:(c + 1) * CONV_CHUNK, :] = acc

    def cur(cb):
        return pl.BlockSpec((tile, LANES), lambda j, i: (i, cb + j))

    def prev(cb):
        return pl.BlockSpec((HALO, LANES), lambda j, i: (jnp.maximum(i * hb - 1, 0), cb + j))

    return _pcall("conv_fwd", kern, (CONV_CH // LANES, t // tile),
                  [cur(a_cb), cur(b_cb), prev(a_cb), prev(b_cb),
                   pl.BlockSpec((CONV_KP, LANES), lambda j, i: (0, j)), pl.BlockSpec((1, LANES), lambda j, i: (0, j))],
                  [pl.BlockSpec((tile, LANES), lambda j, i: (i, j))], [jax.ShapeDtypeStruct((t, CONV_CH), F32)],
                  [pltpu.VMEM((tile + HALO, LANES), F32)], ("parallel", "parallel"), (z, z, z, z, w32, bias), side)


def _conv_bwd(z, dc1, w32, t, side=None):
    tile = min(CONV_TILE, t)
    a_cb, b_cb = GLU_A_COL // LANES, GLU_B_COL // LANES
    hb = tile // HALO
    n_tiles = t // tile
    n_chunks = tile // CONV_CHUNK

    def kern(a_ref, b_ref, ah_ref, bh_ref, d_ref, dn_ref, w_ref, da_ref, db_ref, dw_ref, g_scr, d_scr):
        i = pl.program_id(1)
        g_scr[HALO:, :] = _glu(a_ref[...], b_ref[...])
        g_scr[:HALO, :] = _glu(ah_ref[...], bh_ref[...]) * jnp.where(i > 0, 1.0, 0.0)
        d_scr[:tile, :] = d_ref[...]
        d_scr[tile:, :] = dn_ref[...] * jnp.where(i < n_tiles - 1, 1.0, 0.0)

        @pl.when(i == 0)
        def _():
            dw_ref[...] = jnp.zeros_like(dw_ref)

        for c in range(n_chunks):
            cs = slice(c * CONV_CHUNK, (c + 1) * CONV_CHUNK)
            acc = jnp.zeros((CONV_CHUNK, LANES), F32)
            for j in range(CONV_K):
                lo = c * CONV_CHUNK + (CONV_K - 1) - j
                acc = acc + w_ref[j:j + 1, :] * d_scr[lo:lo + CONV_CHUNK, :]
            sgc = _sig(b_ref[cs, :].astype(F32))
            da_ref[cs, :] = (acc * sgc).astype(BF)
            db_ref[cs, :] = (acc * a_ref[cs, :].astype(F32) * sgc * (1.0 - sgc)).astype(BF)
        for j in range(CONV_K):
            tot = jnp.zeros((SUBLANES, LANES), F32)
            for c in range(n_chunks):
                lo = c * CONV_CHUNK + HALO - (CONV_K - 1) + j
                prod = d_ref[c * CONV_CHUNK:(c + 1) * CONV_CHUNK, :] * g_scr[lo:lo + CONV_CHUNK, :]
                tot = tot + jnp.sum(prod.reshape(CONV_CHUNK // SUBLANES, SUBLANES, LANES), axis=0)
            dw_ref[j:j + 1, :] += jnp.sum(tot, axis=0, keepdims=True)
        dw_ref[CONV_K:CONV_KP, :] += jnp.sum(d_ref[...], axis=0, keepdims=True)

    def cur(cb):
        return pl.BlockSpec((tile, LANES), lambda j, i: (i, cb + j))

    def prev(cb):
        return pl.BlockSpec((HALO, LANES), lambda j, i: (jnp.maximum(i * hb - 1, 0), cb + j))

    return _pcall(
        "conv_bwd", kern, (CONV_CH // LANES, n_tiles),
        [cur(a_cb), cur(b_cb), prev(a_cb), prev(b_cb), cur(0),
         pl.BlockSpec((HALO, LANES), lambda j, i: (jnp.minimum((i + 1) * hb, t // HALO - 1), j)),
         pl.BlockSpec((CONV_KP, LANES), lambda j, i: (0, j))],
        [pl.BlockSpec((tile, LANES), lambda j, i: (i, j)), pl.BlockSpec((tile, LANES), lambda j, i: (i, j)),
         pl.BlockSpec((CONV_KP, LANES), lambda j, i: (0, j))],
        [jax.ShapeDtypeStruct((t, CONV_CH), BF), jax.ShapeDtypeStruct((t, CONV_CH), BF),
         jax.ShapeDtypeStruct((CONV_KP, CONV_CH), F32)],
        [pltpu.VMEM((tile + HALO, LANES), F32), pltpu.VMEM((tile + HALO, LANES), F32)],
        ("parallel", "arbitrary"), (z, z, z, z, dc1, dc1, w32), side)


def _adam_fn(w, g, m, v):
    m = ADAM_B1 * m + (1.0 - ADAM_B1) * g
    v = ADAM_B2 * v + (1.0 - ADAM_B2) * (g * g)
    m_hat = m / (1.0 - ADAM_B1 ** ADAM_STEP)
    v_hat = v / (1.0 - ADAM_B2 ** ADAM_STEP)
    delta = -ADAM_LR * (m_hat / (jnp.sqrt(v_hat) + ADAM_EPS) + ADAM_WD * w)
    return delta, m, v


def _adam(name, w, g, m, v):
    rows, cols = w.shape
    tile = _ew_tile(rows, cols)
    return _rowcall(name, _adam_fn, rows, tile, [(a, cols, 0) for a in (w, g, m, v)], [], [(cols, F32)] * 3)


def _place():
    x, y, c = lax.axis_index("x"), lax.axis_index("y"), lax.axis_index("c")
    chips = [(1 - x, y), (x, 1 - y), (1 - x, 1 - y)]
    return x, y, c, chips


def _gather_side(shards):
    nw = len(shards)
    chip = 2 * lax.axis_index("x") + lax.axis_index("y")
    staged = [lax.dynamic_update_index_in_dim(jnp.zeros((4,) + s.shape, s.dtype), s, chip, 0) for s in shards]

    def build(_, outs, send_sems, recv_sems):
        x, y, c, chips = _place()
        me = 2 * x + y
        sibling = (x, y, 1 - c)

        def half(w, lead, h):
            n = shards[w].shape[0] // 2
            return outs[w].at[lead, pl.ds(h * n, n)]

        def copy(w, k, part, to):
            return pltpu.make_async_remote_copy(src_ref=part, dst_ref=part, send_sem=send_sems.at[6 * w + k],
                                                recv_sem=recv_sems.at[6 * w + k], device_id=to, device_id_type=MESH)

        def start():
            for w in range(nw):
                for k, (px, py) in enumerate(chips):
                    copy(w, k, half(w, me, c), (px, py, c)).start()

        def finish():
            for w in range(nw):
                for k, (px, py) in enumerate(chips):
                    landed = half(w, 2 * px + py, c)
                    copy(w, k, landed, (px, py, c)).wait_recv()
                    copy(w, 3 + k, landed, sibling).start()
            for w in range(nw):
                for k, (px, py) in enumerate(chips):
                    copy(w, 3 + k, half(w, 2 * px + py, 1 - c), sibling).wait_recv()
            for w in range(nw):
                for k, (px, py) in enumerate(chips):
                    copy(w, k, half(w, me, c), (px, py, c)).wait_send()
                    copy(w, 3 + k, half(w, 2 * px + py, c), sibling).wait_send()

        return start, finish

    return _Side(staged, [jax.ShapeDtypeStruct((4,) + s.shape, s.dtype) for s in shards], 6 * nw, build,
                 aliases={w: w for w in range(nw)})


def _gather8(name, v, side=None):
    rows = v.shape[0]

    def body(v_ref, all_ref, sum_ref, send_sems, recv_sems):
        x, y, c, _ = _place()
        me = 4 * x + 2 * y + c
        all_ref[me] = v_ref[...]
        copies = []
        for k in range(1, 8):
            px, py, pc = x ^ (k >> 2), y ^ ((k >> 1) & 1), c ^ (k & 1)
            copies.append(pltpu.make_async_remote_copy(
                src_ref=v_ref, dst_ref=all_ref.at[me], send_sem=send_sems.at[k - 1], recv_sem=recv_sems.at[k - 1],
                device_id=(px, py, pc), device_id_type=MESH))
            copies[-1].start()
        for k in range(1, 8):
            px, py, pc = x ^ (k >> 2), y ^ ((k >> 1) & 1), c ^ (k & 1)
            theirs = all_ref.at[4 * px + 2 * py + pc]
            pltpu.make_async_remote_copy(
                src_ref=theirs, dst_ref=theirs, send_sem=send_sems.at[k - 1], recv_sem=recv_sems.at[k - 1],
                device_id=(px, py, pc), device_id_type=MESH).wait_recv()
        for cp in copies:
            cp.wait_send()
        tot = all_ref[0]
        for d in range(1, 8):
            tot = tot + all_ref[d]
        sum_ref[...] = tot

    vm = pl.BlockSpec(memory_space=pltpu.VMEM)
    return _pcall(name, body, (), [vm], [vm, vm],
                  [jax.ShapeDtypeStruct((8, rows, LANES), F32), jax.ShapeDtypeStruct((rows, LANES), F32)],
                  [pltpu.SemaphoreType.DMA((7,)), pltpu.SemaphoreType.DMA((7,))], None, (v,), side)


def _region(ref, col_sharded, shape, j, h):
    r, ccols = shape
    if col_sharded:
        return ref.at[pl.ds(h * (r // 2), r // 2), pl.ds(j * (ccols // 4), ccols // 4)]
    n = r // 8
    return ref.at[pl.ds((2 * j + h) * n, n), :]


def _region_shape(col_sharded, shape):
    r, ccols = shape
    return (r // 2, ccols // 4) if col_sharded else (r // 8, ccols)


def _exchange(copies):
    def build(ins, outs, send_sems, recv_sems):
        def start():
            for cp in copies(ins, outs, send_sems, recv_sems):
                cp.start()

        def finish():
            for cp in copies(ins, outs, send_sems, recv_sems):
                cp.wait()

        return start, finish
    return build


def _swap_side(grads, kinds):
    nw = len(grads)

    def copies(ins, theirs, send_sems, recv_sems):
        x, y, c, _ = _place()
        return [pltpu.make_async_remote_copy(
            src_ref=_region(ins[w], kinds[w], grads[w].shape, j, 1 - c), dst_ref=theirs[w].at[j],
            send_sem=send_sems.at[4 * w + j], recv_sem=recv_sems.at[4 * w + j], device_id=(x, y, 1 - c), device_id_type=MESH)
            for w in range(nw) for j in range(4)]

    shapes = [jax.ShapeDtypeStruct((4,) + _region_shape(kinds[w], grads[w].shape), F32) for w in range(nw)]
    return _Side(grads, shapes, 4 * nw, _exchange(copies))


def _scatter_side(parts):
    nw = len(parts)

    def copies(ins, outs, send_sems, recv_sems):
        x, y, c, chips = _place()
        return [pltpu.make_async_remote_copy(
            src_ref=ins[w].at[2 * px + py], dst_ref=outs[w].at[k], send_sem=send_sems.at[3 * w + k],
            recv_sem=recv_sems.at[3 * w + k], device_id=(px, py, c), device_id_type=MESH)
            for w in range(nw) for k, (px, py) in enumerate(chips)]

    shapes = [jax.ShapeDtypeStruct((3,) + p.shape[1:], p.dtype) for p in parts]
    return _Side(parts, shapes, 3 * nw, _exchange(copies))


def _share_side(halves):
    nw = len(halves)

    def copies(ins, outs, send_sems, recv_sems):
        x, y, c, _ = _place()
        return [pltpu.make_async_remote_copy(
            src_ref=ins[w], dst_ref=outs[w].at[c], send_sem=send_sems.at[w], recv_sem=recv_sems.at[w],
            device_id=(x, y, 1 - c), device_id_type=MESH) for w in range(nw)]

    return _Side(halves, [jax.ShapeDtypeStruct((2,) + h.shape, F32) for h in halves], nw, _exchange(copies))


EW_BLOCK = 512 * 1024


def _ew_tile(rows, cols):
    limit = max(8, EW_BLOCK // cols)
    return max(d for d in range(8, min(rows, limit) + 1, 8) if rows % d == 0)


def _indexed_sum(name, fn, grid, in_specs, out_specs, out_shape, index, arrays):
    def kern(_, *refs):
        n_in = len(in_specs)
        vals = fn(*[r[...] for r in refs[:n_in]])
        for r, v in zip(refs[n_in:], vals):
            r[...] = v.astype(r.dtype)

    return pl.pallas_call(
        kern, name=name, out_shape=out_shape,
        grid_spec=pltpu.PrefetchScalarGridSpec(num_scalar_prefetch=1, grid=grid, in_specs=in_specs, out_specs=out_specs),
        compiler_params=_params(("parallel",) * len(grid)),
    )(index.astype(jnp.int32).reshape(1), *arrays)


def _pair_sums(names, grads, theirs):
    parts, parts_bf = {}, {}
    for n, other in zip(names, theirs):
        _, rr, cc = other.shape
        tile = _ew_tile(rr, cc)
        nb = rr // tile
        if COL_SHARDED[n]:
            mine = pl.BlockSpec((tile, cc), lambda j, i, c: (c[0] * nb + i, j))
        else:
            mine = pl.BlockSpec((tile, cc), lambda j, i, c: ((2 * j + c[0]) * nb + i, 0))
        flat = pl.BlockSpec((tile, cc), lambda j, i, c: (j * nb + i, 0))
        p, pb = _indexed_sum(f"grad_pair_sum_{n}", lambda u, v: (u + v, u + v), (4, nb), [mine, flat], [flat, flat],
                             [jax.ShapeDtypeStruct((4 * rr, cc), F32), jax.ShapeDtypeStruct((4 * rr, cc), BF)],
                             lax.axis_index("c"), (grads[n], other.reshape(4 * rr, cc)))
        parts[n], parts_bf[n] = p.reshape(4, rr, cc), pb.reshape(4, rr, cc)
    return parts, parts_bf


def _chip_sums(names, parts, landed):
    halves = {}
    for n, got in zip(names, landed):
        _, rr, cc = got.shape
        tile = _ew_tile(rr, cc)
        nb = rr // tile
        own = pl.BlockSpec((tile, cc), lambda i, chip: (chip[0] * nb + i, 0))
        peer = lambda k: pl.BlockSpec((tile, cc), lambda i, chip: (k * nb + i, 0))
        halves[n] = _indexed_sum(f"grad_chip_sum_{n}", lambda o, k0, k1, k2: (((o + k0) + k1) + k2,), (nb,),
                                 [own, peer(0), peer(1), peer(2)], [pl.BlockSpec((tile, cc), lambda i, chip: (i, 0))],
                                 [jax.ShapeDtypeStruct((rr, cc), F32)], 2 * lax.axis_index("x") + lax.axis_index("y"),
                                 (parts[n].reshape(4 * rr, cc),) + (got.reshape(3 * rr, cc),) * 3)[0]
    return halves


def _both_halves(mine, shared):
    both = lax.dynamic_update_index_in_dim(shared, mine, lax.axis_index("c"), 0)
    return both.reshape(2 * mine.shape[0], mine.shape[1])


BIG = ("w_in", "w_attn_proj", "w_conv_proj", "w_out", "w_cq", "w_ckv", "w_co", "w_up", "w_down")
COL_SHARDED = {"w_in": True, "w_attn_proj": True, "w_conv_proj": True, "w_out": False, "w_cq": False,
               "w_ckv": True, "w_co": False, "w_up": True, "w_down": False}
SMALL = ("g_mix", "b_gate", "conv_b", "conv_ln_g", "conv_ln_b", "g_cross", "g_mem", "g_mlp", "g_final")
ORDER = ("g_mix", "w_in", "b_gate", "conv_w", "conv_b", "conv_ln_g", "conv_ln_b", "w_attn_proj", "w_conv_proj", "w_out",
         "g_cross", "g_mem", "w_cq", "w_ckv", "w_co", "g_mlp", "w_up", "w_down", "g_final")


def _pad_rows(flat, rows):
    return jnp.pad(flat, (0, rows * LANES - flat.shape[0])).reshape(rows, LANES)


WAVE_MLP = ("w_down", "w_up")
WAVE_MID = ("w_co", "w_cq", "w_ckv", "w_out", "w_attn_proj", "w_conv_proj")


def _local_step(x, mem, tgt, shards, small, conv_w_full):
    t = x.shape[0]
    tr = 512
    c_tab, s_tab = _rope_tables(t)
    row = lambda v: v.reshape(1, -1)
    g_mix, g_cross, g_mem, g_mlp, g_final = (row(small[n]) for n in ("g_mix", "g_cross", "g_mem", "g_mlp", "g_final"))
    b_gate, conv_b, ln_g, ln_b = (row(small[n]) for n in ("b_gate", "conv_b", "conv_ln_g", "conv_ln_b"))
    w32 = jnp.pad(conv_w_full, ((0, CONV_KP - CONV_K), (0, 0)))

    (u,), (w_in_all,) = _rowcall("mix_norm", lambda a, g: (_rms(a, g),), t, tr, [(x, D_MODEL, 0)], [g_mix], [(D_MODEL, BF)],
                                 side=_gather_side([shards["w_in"]]))
    wfull = {"w_in": w_in_all}

    def keep(names, gathered):
        for n, g in zip(names, gathered):
            wfull[n] = g if COL_SHARDED[n] else g.reshape(1, 4 * g.shape[1], g.shape[2])

    z, gathered = _mm("in_proj", u, w_in_all, out_dtypes=(BF,), side=_gather_side([shards[n] for n in WAVE_MID]))
    keep(WAVE_MID, gathered)
    (c1,), gathered = _conv_fwd(z, w32, conv_b, t, side=_gather_side([shards[n] for n in WAVE_MLP]))
    keep(WAVE_MLP, gathered)
    qkv = _rope_split(z, c_tab, s_tab, t)
    qkv_cls, outs, lses = [], [], []
    for g, dil in enumerate(DILATIONS):
        ops = tuple((a.reshape(t, GROUP_W), 0) for a in qkv[3 * g:3 * g + 3])
        qkv_cls.append(ops)
        o_g, l_g = _attn_fwd(f"attn_fwd_{g}", *ops, t, dil)
        outs.append(o_g)
        lses.append(l_g)
    attn, lse = _merge_classes(outs, lses, t)
    y_attn = _mm("attn_proj", attn, wfull["w_attn_proj"])
    (c2,) = _rowcall("conv_ln_silu", _ln_silu_fn, t, tr, [(c1, CONV_CH, 0)], [ln_g, ln_b], [(CONV_CH, BF)])
    tn_cp = wfull["w_conv_proj"].shape[2]
    merged, y_conv = _mm("conv_proj", c2, wfull["w_conv_proj"], epi=_gate_epi, out_dtypes=(BF, F32),
                         extras=(y_attn, (z, GATE_A_COL // tn_cp), (z, GATE_B_COL // tn_cp)),
                         params=((b_gate, 0), (b_gate, D_MODEL // tn_cp)))
    x1, uq = _mm("out_proj", merged, wfull["w_out"], extras=(x,), params=(g_cross,), epi=_res_norm_epi, out_dtypes=(F32, BF))
    (mn,) = _rowcall("mem_norm", lambda a, g: (_rms(a, g),), N_MEM, N_MEM, [(mem, D_MODEL, 0)], [g_mem], [(D_MODEL, BF)])
    cq = _mm("cross_q", uq, wfull["w_cq"], out_dtypes=(BF,))
    ckv = _mm("cross_kv", mn, wfull["w_ckv"], out_dtypes=(BF,))
    co = _cross_fwd(cq, ckv, t)
    x2, um = _mm("cross_out", co, wfull["w_co"], extras=(x1,), params=(g_mlp,), epi=_res_norm_epi, out_dtypes=(F32, BF))
    hact = _mm("mlp_up", um, wfull["w_up"], out_dtypes=(BF,), epi=lambda acc: (jnp.square(jnp.maximum(acc, 0.0)),))
    d3, d3b, loss_row, dg_final = _mm("mlp_down", hact, wfull["w_down"], extras=(x2, tgt), params=(g_final,), epi=_final_epi,
                                      out_dtypes=(F32, BF), sums=[(1, LANES), (1, D_MODEL)])

    gw = {}
    dhp = _mm("mlp_down_bwd", d3b, wfull["w_down"], nt=True, extras=(hact,), out_dtypes=(BF,),
              epi=lambda acc, h: (acc * 2.0 * jnp.sqrt(h.astype(F32)),))
    gw["w_down"] = _mm_tn("mlp_down_wgrad", hact, d3b)
    gw["w_up"] = _mm_tn("mlp_up_wgrad", um, dhp)

    def swap_of(names):
        return _swap_side([gw[n] for n in names], [COL_SHARDED[n] for n in names])

    (d2, d2b, dg_mlp), theirs = _mm("mlp_up_bwd", dhp, wfull["w_up"], nt=True, extras=(x2, d3), params=(g_mlp,),
                                    epi=_norm_bwd_epi, out_dtypes=(F32, BF), sums=[(1, D_MODEL)], side=swap_of(WAVE_MLP))
    parts_mlp, parts_bf_mlp = _pair_sums(WAVE_MLP, gw, theirs)

    dco = _mm("cross_out_bwd", d2b, wfull["w_co"], nt=True, out_dtypes=(BF,))
    gw["w_co"] = _mm_tn("cross_out_wgrad", co, d2b)
    dcq, dckv = _cross_bwd(cq, ckv, dco, t)
    gw["w_cq"] = _mm_tn("cross_q_wgrad", uq, dcq)
    d1, d1b, dg_cross = _mm("cross_q_bwd", dcq, wfull["w_cq"], nt=True, extras=(x1, d2), params=(g_cross,), epi=_norm_bwd_epi,
                            out_dtypes=(F32, BF), sums=[(1, D_MODEL)])
    gw["w_ckv"] = _mm_tn("cross_kv_wgrad", mn, dckv, tk=N_MEM)
    dmn = _mm("cross_kv_bwd", dckv, wfull["w_ckv"], nt=True)
    (dg_mem,) = _rowcall("mem_norm_bwd", lambda a, dn, g: (_rms_bwd(a, g, dn)[1],), N_MEM, N_MEM,
                         [(mem, D_MODEL, 0), (dmn, D_MODEL, 0)], [g_mem], [], accs=[(1, D_MODEL)])

    dya, dyc, dgate_a, dgate_b, dbg_a, dbg_b = _mm(
        "out_proj_bwd", d1b, wfull["w_out"], nt=True, tm=min(512, t), epi=_gate_bwd_epi, out_dtypes=(BF,) * 4,
        extras=(y_attn, y_conv, (z, GATE_A_COL // D_MODEL), (z, GATE_B_COL // D_MODEL)), params=(b_gate,),
        sums=[(1, D_MODEL), (1, D_MODEL)])
    dbg = jnp.concatenate([dbg_a, dbg_b], axis=1)
    gw["w_out"] = _mm_tn("out_proj_wgrad", merged, d1b)
    gw["w_attn_proj"] = _mm_tn("attn_proj_wgrad", attn, dya)
    dattn = _mm("attn_proj_bwd", dya, wfull["w_attn_proj"], nt=True)
    gw["w_conv_proj"] = _mm_tn("conv_proj_wgrad", c2, dyc)
    dc2 = _mm("conv_proj_bwd", dyc, wfull["w_conv_proj"], nt=True)
    (dc1, dlng, dlnb), theirs = _rowcall("conv_ln_silu_bwd", _ln_silu_bwd_fn, t, tr, [(c1, CONV_CH, 0), (dc2, CONV_CH, 0)],
                                         [ln_g, ln_b], [(CONV_CH, F32)], accs=[(1, CONV_CH), (1, CONV_CH)],
                                         side=swap_of(WAVE_MID))
    parts_mid, parts_bf_mid = _pair_sums(WAVE_MID, gw, theirs)
    (dglu_a, dglu_b, dconv), landed = _conv_bwd(z, dc1, w32, t, side=_scatter_side([parts_bf_mlp[n] for n in WAVE_MLP]))
    halves_mlp = _chip_sums(WAVE_MLP, parts_mlp, landed)

    dl0, cls1, cls2 = _attn_bwd_prep(dattn, attn, lse, t)
    dqs, dks, dvs = [], [], []
    for g, (dil, (da_c, dl_c, ls_c)) in enumerate(zip(DILATIONS, ((dattn, dl0, lse), cls1, cls2))):
        res = _attn_bwd(f"attn_bwd_{g}", *qkv_cls[g], da_c, dl_c, ls_c, t, dil,
                        side=_share_side([halves_mlp[n] for n in WAVE_MLP]) if g == 0 else None)
        if g == 0:
            res, others = res
            gshard = {n: _both_halves(halves_mlp[n], o) for n, o in zip(WAVE_MLP, others)}
        dqs.append(res[0])
        dks.append(res[1])
        dvs.append(res[2])
    (dz,), landed = _rope_bwd_join(dqs, dks, dvs, c_tab, s_tab, (dglu_a, dglu_b, dgate_a, dgate_b), t,
                                   side=_scatter_side([parts_bf_mid[n] for n in WAVE_MID]))
    halves_mid = _chip_sums(WAVE_MID, parts_mid, landed)

    gw_in_full, others = _mm_tn("in_proj_wgrad", u, dz, side=_share_side([halves_mid[n] for n in WAVE_MID]))
    gshard.update({n: _both_halves(halves_mid[n], o) for n, o in zip(WAVE_MID, others)})
    gw_in = {"w_in": gw_in_full}
    du_a, theirs = _mm("in_proj_bwd_a", dz, wfull["w_in"], nt=True, k_part=(0, 2), side=_swap_side([gw_in["w_in"]], [True]))
    parts, parts_bf = _pair_sums(("w_in",), gw_in, theirs)
    (gx, dg_mix), landed_in = _mm("in_proj_bwd_b", dz, wfull["w_in"], nt=True, k_part=(1, 2), extras=(du_a, x, d1),
                                  params=(g_mix,), epi=lambda acc, first, a, dres, g: _norm_bwd_epi(acc + first, a, dres, g)[1:],
                                  sums=[(1, D_MODEL)], side=_scatter_side([parts_bf["w_in"]]))
    halves = _chip_sums(("w_in",), parts, landed_in)

    gsmall = {"g_mix": dg_mix, "b_gate": dbg, "conv_b": dconv[CONV_K:CONV_K + 1], "conv_ln_g": dlng, "conv_ln_b": dlnb,
              "g_cross": dg_cross, "g_mem": dg_mem, "g_mlp": dg_mlp, "g_final": dg_final, "conv_w": dconv[:CONV_K]}
    return loss_row, gx, gshard, gsmall, halves["w_in"]


def kernel(x, mem, g_mix, w_in, b_gate, conv_w, conv_b, conv_ln_g, conv_ln_b, w_attn_proj, w_conv_proj, w_out, g_cross, g_mem, w_cq, w_ckv, w_co, g_mlp, w_up, w_down, g_final, loss_target, m_g_mix, m_w_in, m_b_gate, m_conv_w, m_conv_b, m_conv_ln_g, m_conv_ln_b, m_w_attn_proj, m_w_conv_proj, m_w_out, m_g_cross, m_g_mem, m_w_cq, m_w_ckv, m_w_co, m_g_mlp, m_w_up, m_w_down, m_g_final, v_g_mix, v_w_in, v_b_gate, v_conv_w, v_conv_b, v_conv_ln_g, v_conv_ln_b, v_w_attn_proj, v_w_conv_proj, v_w_out, v_g_cross, v_g_mem, v_w_cq, v_w_ckv, v_w_co, v_g_mlp, v_w_up, v_w_down, v_g_final):
    w = dict(g_mix=g_mix, w_in=w_in, b_gate=b_gate, conv_w=conv_w, conv_b=conv_b, conv_ln_g=conv_ln_g, conv_ln_b=conv_ln_b,
             w_attn_proj=w_attn_proj, w_conv_proj=w_conv_proj, w_out=w_out, g_cross=g_cross, g_mem=g_mem, w_cq=w_cq,
             w_ckv=w_ckv, w_co=w_co, g_mlp=g_mlp, w_up=w_up, w_down=w_down, g_final=g_final)
    mo = dict(g_mix=m_g_mix, w_in=m_w_in, b_gate=m_b_gate, conv_w=m_conv_w, conv_b=m_conv_b, conv_ln_g=m_conv_ln_g,
              conv_ln_b=m_conv_ln_b, w_attn_proj=m_w_attn_proj, w_conv_proj=m_w_conv_proj, w_out=m_w_out, g_cross=m_g_cross,
              g_mem=m_g_mem, w_cq=m_w_cq, w_ckv=m_w_ckv, w_co=m_w_co, g_mlp=m_g_mlp, w_up=m_w_up, w_down=m_w_down,
              g_final=m_g_final)
    vo = dict(g_mix=v_g_mix, w_in=v_w_in, b_gate=v_b_gate, conv_w=v_conv_w, conv_b=v_conv_b, conv_ln_g=v_conv_ln_g,
              conv_ln_b=v_conv_ln_b, w_attn_proj=v_w_attn_proj, w_conv_proj=v_w_conv_proj, w_out=v_w_out, g_cross=v_g_cross,
              g_mem=v_g_mem, w_cq=v_w_cq, w_ckv=v_w_ckv, w_co=v_w_co, g_mlp=v_g_mlp, w_up=v_w_up, w_down=v_w_down,
              g_final=v_g_final)
    shapes = {n: w[n].shape for n in ORDER}
    two_d = lambda a: a.reshape(a.shape[-2], a.shape[-1])
    chip = 2 * lax.axis_index("x") + lax.axis_index("y")

    shards = {n: two_d(w[n]).astype(BF) for n in BIG}
    cw_rows = 48
    cw_all, _ = _gather8("gather_conv_w", _pad_rows(conv_w.reshape(-1), cw_rows))
    cw_shard = CONV_K * (CONV_CH // 4)
    conv_w_full = jnp.concatenate(
        [cw_all[2 * j].reshape(-1)[:cw_shard].reshape(CONV_K, CONV_CH // 4) for j in range(4)], axis=1)

    small = {n: w[n] for n in SMALL}
    loss_row, gx, gshard, gsmall, half_w_in = _local_step(two_d(x), two_d(mem), two_d(loss_target), shards, small,
                                                          conv_w_full)
    loss = lax.psum(loss_row[0, 0], ("x", "y", "c"))

    small_names = SMALL + ("conv_w",)
    flat = jnp.concatenate([gsmall[n].reshape(-1) for n in small_names])
    sm_rows = -(-flat.shape[0] // (8 * LANES)) * 8
    (_, sm_sum), others = _gather8("reduce_small_grads", _pad_rows(flat, sm_rows), side=_share_side([half_w_in]))
    gshard["w_in"] = _both_halves(half_w_in, others[0])
    sm_sum = sm_sum.reshape(-1)
    off = 0
    for n in small_names:
        size = gsmall[n].size
        gshard[n] = sm_sum[off:off + size].reshape(gsmall[n].shape)
        off += size
    gshard["conv_w"] = lax.dynamic_slice_in_dim(gshard["conv_w"], chip * (CONV_CH // 4), CONV_CH // 4, axis=1)

    grads, deltas, new_m, new_v = {}, {}, {}, {}
    for n in BIG:
        d, m2, v2 = _adam(f"adamw_{n}", two_d(w[n]), gshard[n], two_d(mo[n]), two_d(vo[n]))
        grads[n], deltas[n], new_m[n], new_v[n] = (a.reshape(shapes[n]) for a in (gshard[n], d, m2, v2))
    pack = lambda src: jnp.concatenate([src[n].reshape(-1) for n in small_names])
    n_small = sum(w[n].size for n in small_names)
    ad_rows = -(-n_small // (8 * LANES)) * 8
    d, m2, v2 = _adam("adamw_small", *[_pad_rows(pack(src), ad_rows) for src in (w, gshard, mo, vo)])
    off = 0
    for n in small_names:
        size = w[n].size
        grads[n] = gshard[n].reshape(shapes[n])
        deltas[n], new_m[n], new_v[n] = (a.reshape(-1)[off:off + size].reshape(shapes[n]) for a in (d, m2, v2))
        off += size

    return (loss, gx.reshape(x.shape), *[grads[n] for n in ORDER], *[deltas[n] for n in ORDER],
            *[new_m[n] for n in ORDER], *[new_v[n] for n in ORDER])
```

```python
import functools

import jax
import jax.numpy as jnp
from jax import lax
from jax.experimental import pallas as pl
from jax.experimental.pallas import tpu as pltpu

F32 = jnp.float32
BF = jnp.bfloat16

D_MODEL = 1024
N_MEM = 256
HEAD_DIM = 128
HEADS_PER_GROUP = 4
DILATIONS = (1, 4, 16)
BLK = 128
GROUP_W = HEADS_PER_GROUP * HEAD_DIM
ATTN_WIDTH = 3 * GROUP_W
ROT_DIM = 32
ROPE_THETA = 500000.0
CONV_CH = 768
CONV_K = 31
CONV_KP = 32
IN_WIDTH = 8192
CROSS_HEADS = 4
CROSS_HEAD_DIM = 256
D_FF = 4096
EPS = 1e-6
ATTN_SCALE = HEAD_DIM ** -0.5
CROSS_SCALE = CROSS_HEAD_DIM ** -0.5
NEG = -1e30

ADAM_LR = 0.001
ADAM_B1 = 0.9
ADAM_B2 = 0.999
ADAM_EPS = 1e-08
ADAM_WD = 0.01
ADAM_STEP = 10

LANES = 128
SUBLANES = 8
VMEM_LIMIT = 56 * 1024 * 1024
MESH = pl.DeviceIdType.MESH
ANY = pl.BlockSpec(memory_space=pl.ANY)

GLU_A_COL = 3 * ATTN_WIDTH
GLU_B_COL = GLU_A_COL + CONV_CH
GATE_A_COL = GLU_B_COL + CONV_CH
GATE_B_COL = GATE_A_COL + D_MODEL


def _params(sem=None):
    return pltpu.CompilerParams(dimension_semantics=sem, vmem_limit_bytes=VMEM_LIMIT)


def _dot(a, b):
    return lax.dot_general(a, b, (((1,), (0,)), ((), ())), preferred_element_type=F32)


def _dot_nt(a, b):
    return lax.dot_general(a, b, (((1,), (1,)), ((), ())), preferred_element_type=F32)


def _dot_tn(a, b):
    return lax.dot_general(a, b, (((0,), (0,)), ((), ())), preferred_element_type=F32)


def _sig(x):
    return 1.0 / (1.0 + jnp.exp(-x))


def _glu(a, b):
    return a.astype(F32) * _sig(b.astype(F32))


class _Side:
    def __init__(self, arrays, out_shapes, n_sems, build, aliases=None):
        self.arrays, self.out_shapes, self.n_sems, self.build = list(arrays), list(out_shapes), n_sems, build
        self.aliases = aliases or {}


def _pcall(name, kern, grid, in_specs, out_specs, out_shape, scratch_shapes, sem, args, side=None):
    in_specs, out_specs, out_shape, scratch_shapes = list(in_specs), list(out_specs), list(out_shape), list(scratch_shapes)
    if side is None:
        return pl.pallas_call(kern, name=name, grid=grid, in_specs=in_specs, out_specs=out_specs, out_shape=out_shape,
                              scratch_shapes=scratch_shapes, compiler_params=_params(sem))(*args)
    ni, no, nsc = len(in_specs), len(out_specs), len(scratch_shapes)
    nsi, nso = len(side.arrays), len(side.out_shapes)

    def wrapped(*refs):
        ins, side_ins = refs[:ni], refs[ni:ni + nsi]
        outs, side_outs = refs[ni + nsi:ni + nsi + no], refs[ni + nsi + no:ni + nsi + no + nso]
        scratch = refs[ni + nsi + no + nso:ni + nsi + no + nso + nsc]
        send_sems, recv_sems = refs[-2:]
        start, finish = side.build(side_ins, side_outs, send_sems, recv_sems)
        if grid:
            first = functools.reduce(jnp.logical_and, [pl.program_id(a) == 0 for a in range(len(grid))])
            last = functools.reduce(jnp.logical_and, [pl.program_id(a) == g - 1 for a, g in enumerate(grid)])
            pl.when(first)(start)
            kern(*ins, *outs, *scratch)
            pl.when(last)(finish)
        else:
            start()
            kern(*ins, *outs, *scratch)
            finish()

    res = pl.pallas_call(
        wrapped, name=name, grid=grid, in_specs=in_specs + [ANY] * nsi, out_specs=out_specs + [ANY] * nso,
        out_shape=out_shape + side.out_shapes,
        scratch_shapes=scratch_shapes + [pltpu.SemaphoreType.DMA((side.n_sems,)), pltpu.SemaphoreType.DMA((side.n_sems,))],
        input_output_aliases={ni + k: no + v for k, v in side.aliases.items()},
        compiler_params=_params(("arbitrary",) * len(grid) if grid else None),
    )(*args, *side.arrays)
    return res[:no], res[no:]


def _rowcall(name, fn, n_rows, tile, ins, params, outs, accs=(), side=None):
    tile = min(tile, n_rows)
    ni, npar, no, na = len(ins), len(params), len(outs), len(accs)

    def kern(*refs):
        in_refs = refs[:ni + npar]
        o_refs = refs[ni + npar:ni + npar + no]
        a_refs = refs[ni + npar + no:]
        vals = fn(*[r[...] for r in in_refs])
        for r, v in zip(o_refs, vals[:no]):
            r[...] = v.astype(r.dtype)
        if na:
            @pl.when(pl.program_id(0) == 0)
            def _():
                for r in a_refs:
                    r[...] = jnp.zeros_like(r)
            for r, v in zip(a_refs, vals[no:]):
                r[...] += v

    in_specs = []
    arrays = []
    for spec in ins:
        arr, width, cb = spec[0], spec[1], spec[2]
        rb = spec[3] if len(spec) > 3 else 0
        in_specs.append(pl.BlockSpec((tile, width), functools.partial(lambda i, cb, rb: (i + rb, cb), cb=cb, rb=rb)))
        arrays.append(arr)
    for p in params:
        in_specs.append(pl.BlockSpec(p.shape, lambda i: (0, 0)))
        arrays.append(p)
    out_specs = [pl.BlockSpec((tile, w), lambda i: (i, 0)) for w, _ in outs]
    out_specs += [pl.BlockSpec(s, lambda i: (0, 0)) for s in accs]
    out_shape = [jax.ShapeDtypeStruct((n_rows, w), dt) for w, dt in outs]
    out_shape += [jax.ShapeDtypeStruct(s, F32) for s in accs]
    return _pcall(name, kern, (n_rows // tile,), in_specs, out_specs, out_shape, [],
                  ("arbitrary",) if na else ("parallel",), arrays, side)


def _mm(name, a, w3, *, nt=False, extras=(), params=(), epi=None, out_dtypes=(F32,), sums=(), tm=None, tn=None, tk=None,
        k_part=(0, 1), side=None):
    m, ka = a.shape
    ns, r, cs = w3.shape
    if not nt:
        k_dim, n = r, ns * cs
        tn = tn or min(cs, 1024)
        tk = tk or min(k_dim, 1024)
    else:
        k_dim, n = ns * cs, r
        tn = tn or min(r, 1024)
        tk = tk or min(cs, 1024)
    assert ka == k_dim, (name, a.shape, w3.shape)
    assert not sums or tn == n, name
    nk = k_dim // tk // k_part[1]
    k0 = k_part[0] * nk
    if not nt:
        nbs = cs // tn
        w_spec = pl.BlockSpec((None, tk, tn), lambda i, j, k: (j // nbs, k + k0, j % nbs))
    else:
        kbs = cs // tk
        w_spec = pl.BlockSpec((None, tn, tk), lambda i, j, k: ((k + k0) // kbs, j, (k + k0) % kbs))
    tm = tm or min(m, 1024)
    ne, no, nsum = len(extras) + len(params), len(out_dtypes), len(sums)

    def kern(a_ref, w_ref, *rest):
        e_refs = rest[:ne]
        o_refs = rest[ne:ne + no]
        s_refs = rest[ne + no:ne + no + nsum]

        def part():
            av = a_ref[...].astype(BF)
            return _dot_nt(av, w_ref[...]) if nt else _dot(av, w_ref[...])

        def finish(res):
            vals = epi(res, *[e[...] for e in e_refs]) if epi else (res,)
            for o, v in zip(o_refs, vals[:no]):
                o[...] = v.astype(o.dtype)
            for sr, v in zip(s_refs, vals[no:]):
                sr[...] += v

        if nsum:
            @pl.when(jnp.logical_and(pl.program_id(0) == 0, pl.program_id(2) == 0))
            def _():
                for sr in s_refs:
                    sr[...] = jnp.zeros_like(sr)

        if nk == 1:
            finish(part())
            return
        acc = rest[ne + no + nsum]
        k = pl.program_id(2)

        @pl.when(k == 0)
        def _():
            acc[...] = part()

        @pl.when(jnp.logical_and(k > 0, k < nk - 1))
        def _():
            acc[...] += part()

        @pl.when(k == nk - 1)
        def _():
            finish(acc[...] + part())

    in_specs = [pl.BlockSpec((tm, tk), lambda i, j, k: (i, k + k0)), w_spec]
    split = lambda items: [(it if isinstance(it, tuple) else (it, None)) for it in items]
    extras, params = split(extras), split(params)
    in_specs += [pl.BlockSpec((tm, tn), functools.partial(lambda i, j, k, off: (i, j + off), off=off or 0)) for _, off in extras]
    in_specs += [pl.BlockSpec(p.shape, lambda i, j, k: (0, 0)) if off is None else
                 pl.BlockSpec((p.shape[0], tn), functools.partial(lambda i, j, k, off: (0, j + off), off=off)) for p, off in params]
    extras, params = [e for e, _ in extras], [p for p, _ in params]
    out_specs = [pl.BlockSpec((tm, tn), lambda i, j, k: (i, j)) for _ in out_dtypes]
    out_specs += [pl.BlockSpec(sh, lambda i, j, k: (0, 0)) for sh in sums]
    out_shape = [jax.ShapeDtypeStruct((m, n), dt) for dt in out_dtypes] + [jax.ShapeDtypeStruct(sh, F32) for sh in sums]
    res = _pcall(name, kern, (m // tm, n // tn, nk), in_specs, out_specs, out_shape,
                 [pltpu.VMEM((tm, tn), F32)] if nk > 1 else [],
                 ("arbitrary",) * 3 if nsum else ("parallel", "parallel", "arbitrary"), (a, w3, *extras, *params), side)
    main = res[0] if side is not None else res
    main = main[0] if no + nsum == 1 else main
    return (main, res[1]) if side is not None else main


def _mm_tn(name, a, b, tm=None, tn=None, tk=None, side=None):
    t, ka = a.shape
    _, n = b.shape
    tm = tm or min(ka, 1024)
    tn = tn or min(n, 2048)
    tk = tk or min(t, 1024)

    def kern(a_ref, b_ref, o_ref):
        def part():
            return _dot_tn(a_ref[...].astype(BF), b_ref[...].astype(BF))

        @pl.when(pl.program_id(2) == 0)
        def _():
            o_ref[...] = part()

        @pl.when(pl.program_id(2) > 0)
        def _():
            o_ref[...] += part()

    res = _pcall(name, kern, (ka // tm, n // tn, t // tk),
                 [pl.BlockSpec((tk, tm), lambda i, j, k: (k, i)), pl.BlockSpec((tk, tn), lambda i, j, k: (k, j))],
                 [pl.BlockSpec((tm, tn), lambda i, j, k: (i, j))], [jax.ShapeDtypeStruct((ka, n), F32)], [],
                 ("parallel", "parallel", "arbitrary"), (a, b), side)
    return (res[0][0], res[1]) if side is not None else res[0]


def _rms(x, g):
    return x * lax.rsqrt(jnp.mean(x * x, axis=-1, keepdims=True) + EPS) * g


def _rms_bwd(x, g, dy):
    r = lax.rsqrt(jnp.mean(x * x, axis=-1, keepdims=True) + EPS)
    xh = x * r
    dxh = dy * g
    dx = r * (dxh - xh * jnp.mean(dxh * xh, axis=-1, keepdims=True))
    return dx, jnp.sum(dy * xh, axis=0, keepdims=True)


def _rot(t, c, s):
    lane = lax.broadcasted_iota(jnp.int32, t.shape, 1)
    swapped = jnp.where(lane < ROT_DIM // 2, pltpu.roll(t, HEAD_DIM - ROT_DIM // 2, 1), pltpu.roll(t, ROT_DIM // 2, 1))
    return t * c + swapped * s


def _rope_tables(t):
    half = ROT_DIM // 2
    pos = jnp.arange(t, dtype=F32)
    inv_freq = ROPE_THETA ** (-jnp.arange(0, ROT_DIM, 2, dtype=F32) / ROT_DIM)
    ang = pos[:, None] * inv_freq[None, :]
    cos, sin = jnp.cos(ang), jnp.sin(ang)
    ones = jnp.ones((t, HEAD_DIM - ROT_DIM), F32)
    c_tab = jnp.concatenate([cos, cos, ones], axis=1)
    s_tab = jnp.concatenate([-sin, sin, 0.0 * ones], axis=1)
    return c_tab, s_tab


def _merge_fn(o0, o1, o2, l0, l1, l2):
    m = jnp.maximum(jnp.maximum(l0, l1), l2)
    e0, e1, e2 = jnp.exp(l0 - m), jnp.exp(l1 - m), jnp.exp(l2 - m)
    tot = e0 + e1 + e2
    return (e0 * o0 + e1 * o1 + e2 * o2) / tot, m + jnp.log(tot)


def _ln_parts(c1):
    mu = jnp.mean(c1, axis=-1, keepdims=True)
    xc = c1 - mu
    r = lax.rsqrt(jnp.mean(xc * xc, axis=-1, keepdims=True) + EPS)
    return xc * r, r


def _ln_silu_fn(c1, g, b):
    xh, _ = _ln_parts(c1)
    yl = xh * g + b
    return (yl * _sig(yl),)


def _ln_silu_bwd_fn(c1, dout, g, b):
    xh, r = _ln_parts(c1)
    yl = xh * g + b
    s = _sig(yl)
    dyl = dout * (s + yl * s * (1.0 - s))
    dxh = dyl * g
    dx = r * (dxh - jnp.mean(dxh, axis=-1, keepdims=True) - xh * jnp.mean(dxh * xh, axis=-1, keepdims=True))
    return dx, jnp.sum(dyl * xh, axis=0, keepdims=True), jnp.sum(dyl, axis=0, keepdims=True)


def _gate_epi(yc, ya, ga, gb, ba, bb):
    return _sig(ga + ba) * ya + _sig(gb + bb) * yc, yc


def _gate_bwd_epi(dm, ya, yc, ga, gb, bg):
    sa = _sig(ga + bg[:, :D_MODEL])
    sb = _sig(gb + bg[:, D_MODEL:])
    dga = dm * ya * sa * (1.0 - sa)
    dgb = dm * yc * sb * (1.0 - sb)
    return dm * sa, dm * sb, dga, dgb, jnp.sum(dga, axis=0, keepdims=True), jnp.sum(dgb, axis=0, keepdims=True)


def _res_norm_epi(acc, res, g):
    xn = res + acc
    return xn, _rms(xn, g)


def _norm_bwd_epi(acc, a, dres, g):
    dx, dg = _rms_bwd(a, g, acc)
    return dres + dx, dres + dx, dg


def _final_epi(acc, res, tgt, g):
    return _final_fn(res + acc, tgt, g)


def _final_fn(x3, tgt, g):
    err = _rms(x3, g) - tgt
    lrow = jnp.sum(err * err, axis=-1, keepdims=True) * (0.5 / D_MODEL)
    lsum = jnp.sum(lrow, axis=0, keepdims=True)
    dx, dg = _rms_bwd(x3, g, err * (1.0 / D_MODEL))
    return dx, dx, jnp.broadcast_to(lsum, (1, LANES)), dg


def _attn_geometry(t, dil):
    cls = t // dil
    rows = min(8 * BLK, cls)
    return rows, rows // BLK, cls // rows


def _head_lanes(h):
    return slice(h * HEAD_DIM, (h + 1) * HEAD_DIM)


def _band_mask():
    row = lax.broadcasted_iota(jnp.int32, (BLK, 2 * BLK), 0)
    col = lax.broadcasted_iota(jnp.int32, (BLK, 2 * BLK), 1)
    return jnp.logical_and(col >= row, col <= row + BLK), col


def _stage_window(scr, halo_ref, cur_ref):
    scr[:BLK, :] = halo_ref[...]
    scr[BLK:, :] = cur_ref[...]


def _attn_fwd(name, q, k, v, t, dil):
    rows, nbk, spc = _attn_geometry(t, dil)

    def kern(q_ref, k_ref, kh_ref, v_ref, vh_ref, o_ref, l_ref, k_scr, v_scr):
        i = pl.program_id(0)
        first_shift = jnp.where(i % spc == 0, BLK, 0)
        _stage_window(k_scr, kh_ref, k_ref)
        _stage_window(v_scr, vh_ref, v_ref)
        band, col = _band_mask()
        band_first = jnp.logical_and(band, col >= first_shift)
        for h in range(HEADS_PER_GROUP):
            hs = _head_lanes(h)
            for b in range(nbk):
                rs, win = slice(b * BLK, (b + 1) * BLK), slice(b * BLK, (b + 2) * BLK)
                s = jnp.where(band_first if b == 0 else band, _dot_nt(q_ref[rs, hs], k_scr[win, hs]) * ATTN_SCALE, NEG)
                m = jnp.max(s, axis=1, keepdims=True)
                p = jnp.exp(s - m)
                tot = jnp.sum(p, axis=1, keepdims=True)
                o_ref[rs, hs] = _dot(p.astype(BF), v_scr[win, hs]) / tot
                l_ref[rs, hs] = jnp.broadcast_to(m + jnp.log(tot), (BLK, HEAD_DIM))

    def cur(cb):
        return pl.BlockSpec((rows, GROUP_W), lambda i: (i, cb))

    def halo(cb):
        return pl.BlockSpec((BLK, GROUP_W), lambda i: (jnp.maximum(i * nbk - 1, 0), cb))

    (qa, qc), (ka, kc_), (va, vc_) = q, k, v
    return _pcall(name, kern, (t // rows,), [cur(qc), cur(kc_), halo(kc_), cur(vc_), halo(vc_)],
                  [pl.BlockSpec((rows, GROUP_W), lambda i: (i, 0))] * 2, [jax.ShapeDtypeStruct((t, GROUP_W), F32)] * 2,
                  [pltpu.VMEM((rows + BLK, GROUP_W), BF)] * 2, ("parallel",), (qa, ka, ka, va, va))


def _attn_bwd(name, q, k, v, da, dl, lse, t, dil, side=None):
    rows, nbk, spc = _attn_geometry(t, dil)
    nblk = t // BLK

    def kern(q_ref, qn_ref, k_ref, kh_ref, v_ref, vh_ref, da_ref, dan_ref, dl_ref, dln_ref, ls_ref, lsn_ref,
             dq_ref, dk_ref, dv_ref, k_scr, v_scr):
        i = pl.program_id(0)
        first_shift = jnp.where(i % spc == 0, BLK, 0)
        next_shift = jnp.where((i + 1) % spc == 0, BLK, 0)
        _stage_window(k_scr, kh_ref, k_ref)
        _stage_window(v_scr, vh_ref, v_ref)
        band, col = _band_mask()
        band_first = jnp.logical_and(band, col >= first_shift)
        row1 = lax.broadcasted_iota(jnp.int32, (BLK, BLK), 0)
        col1 = lax.broadcasted_iota(jnp.int32, (BLK, BLK), 1)
        pend_k, pend_v = [None] * HEADS_PER_GROUP, [None] * HEADS_PER_GROUP
        for b in range(nbk):
            rs, win = slice(b * BLK, (b + 1) * BLK), slice(b * BLK, (b + 2) * BLK)
            for h in range(HEADS_PER_GROUP):
                hs = _head_lanes(h)
                qb, dab = q_ref[rs, hs], da_ref[rs, hs].astype(BF)
                kw, vw = k_scr[win, hs], v_scr[win, hs]
                p = jnp.where(band_first if b == 0 else band,
                              jnp.exp(_dot_nt(qb, kw) * ATTN_SCALE - ls_ref[rs, hs][:, :1]), 0.0)
                ds = (p * (_dot_nt(dab, vw) - dl_ref[rs, hs][:, :1]) * ATTN_SCALE).astype(BF)
                dq_ref[rs, hs] = _dot(ds, kw).astype(BF)
                dkw, dvw = _dot_tn(ds, qb), _dot_tn(p.astype(BF), dab)
                if b >= 1:
                    ps = slice((b - 1) * BLK, b * BLK)
                    dk_ref[ps, hs] = (pend_k[h] + dkw[:BLK]).astype(BF)
                    dv_ref[ps, hs] = (pend_v[h] + dvw[:BLK]).astype(BF)
                pend_k[h], pend_v[h] = dkw[BLK:], dvw[BLK:]
        ls_rows = slice((nbk - 1) * BLK, nbk * BLK)
        last = slice(nbk * BLK, (nbk + 1) * BLK)
        for h in range(HEADS_PER_GROUP):
            hs = _head_lanes(h)
            qb, dab = qn_ref[:, hs], dan_ref[:, hs].astype(BF)
            kp, vp = k_scr[last, hs], v_scr[last, hs]
            p = jnp.where(col1 >= row1 + next_shift, jnp.exp(_dot_nt(qb, kp) * ATTN_SCALE - lsn_ref[:, hs]), 0.0)
            ds = (p * (_dot_nt(dab, vp) - dln_ref[:, hs]) * ATTN_SCALE).astype(BF)
            dk_ref[ls_rows, hs] = (pend_k[h] + _dot_tn(ds, qb)).astype(BF)
            dv_ref[ls_rows, hs] = (pend_v[h] + _dot_tn(p.astype(BF), dab)).astype(BF)

    def cur(cb):
        return pl.BlockSpec((rows, GROUP_W), lambda i: (i, cb))

    def prev(cb):
        return pl.BlockSpec((BLK, GROUP_W), lambda i: (jnp.maximum(i * nbk - 1, 0), cb))

    def nxt(cb):
        return pl.BlockSpec((BLK, GROUP_W), lambda i: (jnp.minimum((i + 1) * nbk, nblk - 1), cb))

    (qa, qc), (ka, kc_), (va, vc_) = q, k, v
    return _pcall(name, kern, (t // rows,),
                  [cur(qc), nxt(qc), cur(kc_), prev(kc_), cur(vc_), prev(vc_), cur(0), nxt(0), cur(0), nxt(0), cur(0), nxt(0)],
                  [pl.BlockSpec((rows, GROUP_W), lambda i: (i, 0))] * 3, [jax.ShapeDtypeStruct((t, GROUP_W), BF)] * 3,
                  [pltpu.VMEM((rows + BLK, GROUP_W), BF)] * 2, ("parallel",),
                  (qa, qa, ka, ka, va, va, da, da, dl, dl, lse, lse), side)


CLS_TILE = 512


def _cls_block(t, tile, dil, dtype):
    if dil == 1:
        return pl.BlockSpec((tile, GROUP_W), lambda i: (i, 0)), jax.ShapeDtypeStruct((t, GROUP_W), dtype)
    return (pl.BlockSpec((dil, tile // dil, GROUP_W), lambda i: (0, i, 0)),
            jax.ShapeDtypeStruct((dil, t // dil, GROUP_W), dtype))


def _head_scratch(tile):
    return pltpu.VMEM((tile, HEAD_DIM), F32)


def _rope_split(z, c_tab, s_tab, t):
    tile = min(CLS_TILE, t)
    n_heads = ATTN_WIDTH // HEAD_DIM

    def kern(zq_ref, zk_ref, zv_ref, c_ref, s_ref, *rest):
        outs, scr = rest[:9], rest[9]
        c, s = c_ref[...], s_ref[...]
        for which, z_ref in enumerate((zq_ref, zk_ref, zv_ref)):
            for h in range(n_heads):
                g, hs = h // HEADS_PER_GROUP, _head_lanes(h % HEADS_PER_GROUP)
                val = z_ref[:, h * HEAD_DIM:(h + 1) * HEAD_DIM].astype(F32)
                if which < 2:
                    val = _rot(val, c, s)
                if g == 0:
                    outs[which][:, hs] = val.astype(BF)
                    continue
                scr[...] = val
                dil = DILATIONS[g]
                for r in range(dil):
                    outs[3 * g + which][r, :, hs] = scr[pl.ds(r, tile // dil, stride=dil), :].astype(BF)

    blocks = [_cls_block(t, tile, DILATIONS[g], BF) for g in range(3) for _ in range(3)]
    zspec = lambda cb: pl.BlockSpec((tile, ATTN_WIDTH), lambda i: (i, cb))
    tab = pl.BlockSpec((tile, HEAD_DIM), lambda i: (i, 0))
    return _pcall("rope", kern, (t // tile,), [zspec(0), zspec(1), zspec(2), tab, tab], [b[0] for b in blocks],
                  [b[1] for b in blocks], [_head_scratch(tile)], ("parallel",), (z, z, z, c_tab, s_tab))


def _merge_classes(outs, lses, t):
    tile = min(CLS_TILE, t)

    def kern(o0, l0, o1, l1, o2, l2, attn_ref, lse_ref, s_o1, s_l1, s_o2, s_l2):
        for h in range(HEADS_PER_GROUP):
            hs = _head_lanes(h)
            for src, dst, dil in ((o1, s_o1, DILATIONS[1]), (l1, s_l1, DILATIONS[1]), (o2, s_o2, DILATIONS[2]),
                                  (l2, s_l2, DILATIONS[2])):
                for r in range(dil):
                    dst[pl.ds(r, tile // dil, stride=dil), :] = src[r, :, hs]
            attn_ref[:, hs], lse_ref[:, hs] = _merge_fn(o0[:, hs], s_o1[...], s_o2[...], l0[:, hs], s_l1[...], s_l2[...])

    blocks = [_cls_block(t, tile, DILATIONS[g], F32) for g in range(3)]
    args = []
    for g in range(3):
        args += [outs[g].reshape(blocks[g][1].shape), lses[g].reshape(blocks[g][1].shape)]
    tok = pl.BlockSpec((tile, GROUP_W), lambda i: (i, 0))
    return _pcall("attn_merge", kern, (t // tile,), [blocks[g][0] for g in range(3) for _ in range(2)], [tok, tok],
                  [jax.ShapeDtypeStruct((t, GROUP_W), F32)] * 2, [_head_scratch(tile)] * 4, ("parallel",), args)


def _attn_bwd_prep(dattn, attn, lse, t):
    tile = min(CLS_TILE, t)

    def kern(da_ref, at_ref, ls_ref, dl0, da1, dl1, ls1, da2, dl2, ls2, s_da, s_dl, s_ls):
        for h in range(HEADS_PER_GROUP):
            hs = _head_lanes(h)
            da = da_ref[:, hs]
            s_da[...] = da
            s_dl[...] = jnp.broadcast_to(jnp.sum(da * at_ref[:, hs], axis=1, keepdims=True), (tile, HEAD_DIM))
            s_ls[...] = ls_ref[:, hs]
            dl0[:, hs] = s_dl[...]
            for oda, odl, ols, dil in ((da1, dl1, ls1, DILATIONS[1]), (da2, dl2, ls2, DILATIONS[2])):
                for r in range(dil):
                    rows = pl.ds(r, tile // dil, stride=dil)
                    oda[r, :, hs] = s_da[rows, :].astype(BF)
                    odl[r, :, hs] = s_dl[rows, :]
                    ols[r, :, hs] = s_ls[rows, :]

    tok = pl.BlockSpec((tile, GROUP_W), lambda i: (i, 0))
    blocks = [(tok, jax.ShapeDtypeStruct((t, GROUP_W), F32))]
    for g in (1, 2):
        blocks += [_cls_block(t, tile, DILATIONS[g], BF), _cls_block(t, tile, DILATIONS[g], F32),
                   _cls_block(t, tile, DILATIONS[g], F32)]
    res = _pcall("attn_bwd_prep", kern, (t // tile,), [tok, tok, tok], [b[0] for b in blocks], [b[1] for b in blocks],
                 [_head_scratch(tile)] * 3, ("parallel",), (dattn, attn, lse))
    flat = [a.reshape(t, GROUP_W) for a in res]
    return flat[0], flat[1:4], flat[4:7]


def _rope_bwd_join(dqs, dks, dvs, c_tab, s_tab, tail, t, side=None):
    tile = min(256, t)
    n_heads = ATTN_WIDTH // HEAD_DIM

    def kern(q0, q1, q2, k0, k1, k2, v0, v1, v2, c_ref, s_ref, ga_ref, gb_ref, gta_ref, gtb_ref, dz_ref, scr):
        c, s = c_ref[...], -s_ref[...]
        for which, srcs in enumerate(((q0, q1, q2), (k0, k1, k2), (v0, v1, v2))):
            for h in range(n_heads):
                g, hs = h // HEADS_PER_GROUP, _head_lanes(h % HEADS_PER_GROUP)
                if g == 0:
                    val = srcs[0][:, hs].astype(F32)
                else:
                    dil = DILATIONS[g]
                    for r in range(dil):
                        scr[pl.ds(r, tile // dil, stride=dil), :] = srcs[g][r, :, hs].astype(F32)
                    val = scr[...]
                if which < 2:
                    val = _rot(val, c, s)
                col = which * ATTN_WIDTH + h * HEAD_DIM
                dz_ref[:, col:col + HEAD_DIM] = val.astype(BF)
        dz_ref[:, GLU_A_COL:GLU_B_COL] = ga_ref[...]
        dz_ref[:, GLU_B_COL:GATE_A_COL] = gb_ref[...]
        dz_ref[:, GATE_A_COL:GATE_B_COL] = gta_ref[...]
        dz_ref[:, GATE_B_COL:] = gtb_ref[...]

    blocks = [_cls_block(t, tile, DILATIONS[g], BF) for g in range(3)]
    args = [a.reshape(blocks[g][1].shape) for grp in (dqs, dks, dvs) for g, a in enumerate(grp)]
    tab = pl.BlockSpec((tile, HEAD_DIM), lambda i: (i, 0))
    row = lambda w: pl.BlockSpec((tile, w), lambda i: (i, 0))
    return _pcall("rope_bwd", kern, (t // tile,), [blocks[g][0] for _ in range(3) for g in range(3)]
                  + [tab, tab, row(CONV_CH), row(CONV_CH), row(D_MODEL), row(D_MODEL)], [row(IN_WIDTH)],
                  [jax.ShapeDtypeStruct((t, IN_WIDTH), BF)], [_head_scratch(tile)], ("parallel",),
                  (*args, c_tab, s_tab, *tail), side)


def _cross_probs(qh, kh):
    s = _dot_nt(qh, kh) * CROSS_SCALE
    e = jnp.exp(s - jnp.max(s, axis=1, keepdims=True))
    return e, jnp.sum(e, axis=1, keepdims=True)


def _cross_fwd(cq, ckv, t):
    rows = min(512, t)

    def kern(q_ref, kv_ref, o_ref):
        for h in range(CROSS_HEADS):
            hs = slice(h * CROSS_HEAD_DIM, (h + 1) * CROSS_HEAD_DIM)
            vs = slice(D_MODEL + h * CROSS_HEAD_DIM, D_MODEL + (h + 1) * CROSS_HEAD_DIM)
            e, tot = _cross_probs(q_ref[:, hs], kv_ref[:, hs])
            o_ref[:, hs] = (_dot(e.astype(BF), kv_ref[:, vs]) / tot).astype(BF)

    return pl.pallas_call(
        kern, name="cross_fwd", grid=(t // rows,),
        in_specs=[pl.BlockSpec((rows, D_MODEL), lambda i: (i, 0)), pl.BlockSpec((N_MEM, 2 * D_MODEL), lambda i: (0, 0))],
        out_specs=pl.BlockSpec((rows, D_MODEL), lambda i: (i, 0)),
        out_shape=jax.ShapeDtypeStruct((t, D_MODEL), BF),
        compiler_params=_params(("parallel",)),
    )(cq, ckv)


def _cross_bwd(cq, ckv, dco, t):
    rows = min(512, t)

    def kern(q_ref, kv_ref, do_ref, dq_ref, dkv_ref):
        @pl.when(pl.program_id(0) == 0)
        def _():
            dkv_ref[...] = jnp.zeros_like(dkv_ref)
        for h in range(CROSS_HEADS):
            hs = slice(h * CROSS_HEAD_DIM, (h + 1) * CROSS_HEAD_DIM)
            vs = slice(D_MODEL + h * CROSS_HEAD_DIM, D_MODEL + (h + 1) * CROSS_HEAD_DIM)
            qh, kh, vh, doh = q_ref[:, hs], kv_ref[:, hs], kv_ref[:, vs], do_ref[:, hs]
            e, tot = _cross_probs(qh, kh)
            p = e / tot
            dp = _dot_nt(doh, vh)
            ds = (p * (dp - jnp.sum(p * dp, axis=1, keepdims=True)) * CROSS_SCALE).astype(BF)
            dq_ref[:, hs] = _dot(ds, kh).astype(BF)
            dkv_ref[:, hs] += _dot_tn(ds, qh)
            dkv_ref[:, vs] += _dot_tn(p.astype(BF), doh)

    return pl.pallas_call(
        kern, name="cross_bwd", grid=(t // rows,),
        in_specs=[pl.BlockSpec((rows, D_MODEL), lambda i: (i, 0)), pl.BlockSpec((N_MEM, 2 * D_MODEL), lambda i: (0, 0)),
                  pl.BlockSpec((rows, D_MODEL), lambda i: (i, 0))],
        out_specs=[pl.BlockSpec((rows, D_MODEL), lambda i: (i, 0)), pl.BlockSpec((N_MEM, 2 * D_MODEL), lambda i: (0, 0))],
        out_shape=[jax.ShapeDtypeStruct((t, D_MODEL), BF), jax.ShapeDtypeStruct((N_MEM, 2 * D_MODEL), F32)],
        compiler_params=_params(("arbitrary",)),
    )(cq, ckv, dco)


CONV_TILE = 512
CONV_CHUNK = 128
HALO = 32


def _conv_fwd(z, w32, bias, t, side=None):
    tile = min(CONV_TILE, t)
    a_cb, b_cb = GLU_A_COL // LANES, GLU_B_COL // LANES
    hb = tile // HALO

    def kern(a_ref, b_ref, ah_ref, bh_ref, w_ref, bias_ref, o_ref, g_scr):
        i = pl.program_id(1)
        g_scr[HALO:, :] = _glu(a_ref[...], b_ref[...])
        g_scr[:HALO, :] = _glu(ah_ref[...], bh_ref[...]) * jnp.where(i > 0, 1.0, 0.0)
        for c in range(tile // CONV_CHUNK):
            acc = jnp.broadcast_to(bias_ref[...], (CONV_CHUNK, LANES))
            for j in range(CONV_K):
                lo = c * CONV_CHUNK + HALO - (CONV_K - 1) + j
                acc = acc + w_ref[j:j + 1, :] * g_scr[lo:lo + CONV_CHUNK, :]
            o_ref[c * CONV_CHUNK:(c + 1) * CONV_CHUNK, :] = acc

    def cur(cb):
        return pl.BlockSpec((tile, LANES), lambda j, i: (i, cb + j))

    def prev(cb):
        return pl.BlockSpec((HALO, LANES), lambda j, i: (jnp.maximum(i * hb - 1, 0), cb + j))

    return _pcall("conv_fwd", kern, (CONV_CH // LANES, t // tile),
                  [cur(a_cb), cur(b_cb), prev(a_cb), prev(b_cb),
                   pl.BlockSpec((CONV_KP, LANES), lambda j, i: (0, j)), pl.BlockSpec((1, LANES), lambda j, i: (0, j))],
                  [pl.BlockSpec((tile, LANES), lambda j, i: (i, j))], [jax.ShapeDtypeStruct((t, CONV_CH), F32)],
                  [pltpu.VMEM((tile + HALO, LANES), F32)], ("parallel", "parallel"), (z, z, z, z, w32, bias), side)


def _conv_bwd(z, dc1, w32, t, side=None):
    tile = min(CONV_TILE, t)
    a_cb, b_cb = GLU_A_COL // LANES, GLU_B_COL // LANES
    hb = tile // HALO
    n_tiles = t // tile
    n_chunks = tile // CONV_CHUNK

    def kern(a_ref, b_ref, ah_ref, bh_ref, d_ref, dn_ref, w_ref, da_ref, db_ref, dw_ref, g_scr, d_scr):
        i = pl.program_id(1)
        g_scr[HALO:, :] = _glu(a_ref[...], b_ref[...])
        g_scr[:HALO, :] = _glu(ah_ref[...], bh_ref[...]) * jnp.where(i > 0, 1.0, 0.0)
        d_scr[:tile, :] = d_ref[...]
        d_scr[tile:, :] = dn_ref[...] * jnp.where(i < n_tiles - 1, 1.0, 0.0)

        @pl.when(i == 0)
        def _():
            dw_ref[...] = jnp.zeros_like(dw_ref)

        for c in range(n_chunks):
            cs = slice(c * CONV_CHUNK, (c + 1) * CONV_CHUNK)
            acc = jnp.zeros((CONV_CHUNK, LANES), F32)
            for j in range(CONV_K):
                lo = c * CONV_CHUNK + (CONV_K - 1) - j
                acc = acc + w_ref[j:j + 1, :] * d_scr[lo:lo + CONV_CHUNK, :]
            sgc = _sig(b_ref[cs, :].astype(F32))
            da_ref[cs, :] = (acc * sgc).astype(BF)
            db_ref[cs, :] = (acc * a_ref[cs, :].astype(F32) * sgc * (1.0 - sgc)).astype(BF)
        for j in range(CONV_K):
            tot = jnp.zeros((SUBLANES, LANES), F32)
            for c in range(n_chunks):
                lo = c * CONV_CHUNK + HALO - (CONV_K - 1) + j
                prod = d_ref[c * CONV_CHUNK:(c + 1) * CONV_CHUNK, :] * g_scr[lo:lo + CONV_CHUNK, :]
                tot = tot + jnp.sum(prod.reshape(CONV_CHUNK // SUBLANES, SUBLANES, LANES), axis=0)
            dw_ref[j:j + 1, :] += jnp.sum(tot, axis=0, keepdims=True)
        dw_ref[CONV_K:CONV_KP, :] += jnp.sum(d_ref[...], axis=0, keepdims=True)

    def cur(cb):
        return pl.BlockSpec((tile, LANES), lambda j, i: (i, cb + j))

    def prev(cb):
        return pl.BlockSpec((HALO, LANES), lambda j, i: (jnp.maximum(i * hb - 1, 0), cb + j))

    return _pcall(
        "conv_bwd", kern, (CONV_CH // LANES, n_tiles),
        [cur(a_cb), cur(b_cb), prev(a_cb), prev(b_cb), cur(0),
         pl.BlockSpec((HALO, LANES), lambda j, i: (jnp.minimum((i + 1) * hb, t // HALO - 1), j)),
         pl.BlockSpec((CONV_KP, LANES), lambda j, i: (0, j))],
        [pl.BlockSpec((tile, LANES), lambda j, i: (i, j)), pl.BlockSpec((tile, LANES), lambda j, i: (i, j)),
         pl.BlockSpec((CONV_KP, LANES), lambda j, i: (0, j))],
        [jax.ShapeDtypeStruct((t, CONV_CH), BF), jax.ShapeDtypeStruct((t, CONV_CH), BF),
         jax.ShapeDtypeStruct((CONV_KP, CONV_CH), F32)],
        [pltpu.VMEM((tile + HALO, LANES), F32), pltpu.VMEM((tile + HALO, LANES), F32)],
        ("parallel", "arbitrary"), (z, z, z, z, dc1, dc1, w32), side)


def _adam_fn(w, g, m, v):
    m = ADAM_B1 * m + (1.0 - ADAM_B1) * g
    v = ADAM_B2 * v + (1.0 - ADAM_B2) * (g * g)
    m_hat = m / (1.0 - ADAM_B1 ** ADAM_STEP)
    v_hat = v / (1.0 - ADAM_B2 ** ADAM_STEP)
    delta = -ADAM_LR * (m_hat / (jnp.sqrt(v_hat) + ADAM_EPS) + ADAM_WD * w)
    return delta, m, v


def _adam(name, w, g, m, v):
    rows, cols = w.shape
    tile = _ew_tile(rows, cols)
    return _rowcall(name, _adam_fn, rows, tile, [(a, cols, 0) for a in (w, g, m, v)], [], [(cols, F32)] * 3)


def _place():
    x, y, c = lax.axis_index("x"), lax.axis_index("y"), lax.axis_index("c")
    chips = [(1 - x, y), (x, 1 - y), (1 - x, 1 - y)]
    return x, y, c, chips


def _gather_side(shards):
    nw = len(shards)
    chip = 2 * lax.axis_index("x") + lax.axis_index("y")
    staged = [lax.dynamic_update_index_in_dim(jnp.zeros((4,) + s.shape, s.dtype), s, chip, 0) for s in shards]

    def build(_, outs, send_sems, recv_sems):
        x, y, c, chips = _place()
        me = 2 * x + y
        sibling = (x, y, 1 - c)

        def half(w, lead, h):
            n = shards[w].shape[0] // 2
            return outs[w].at[lead, pl.ds(h * n, n)]

        def copy(w, k, part, to):
            return pltpu.make_async_remote_copy(src_ref=part, dst_ref=part, send_sem=send_sems.at[6 * w + k],
                                                recv_sem=recv_sems.at[6 * w + k], device_id=to, device_id_type=MESH)

        def start():
            for w in range(nw):
                for k, (px, py) in enumerate(chips):
                    copy(w, k, half(w, me, c), (px, py, c)).start()

        def finish():
            for w in range(nw):
                for k, (px, py) in enumerate(chips):
                    landed = half(w, 2 * px + py, c)
                    copy(w, k, landed, (px, py, c)).wait_recv()
                    copy(w, 3 + k, landed, sibling).start()
            for w in range(nw):
                for k, (px, py) in enumerate(chips):
                    copy(w, 3 + k, half(w, 2 * px + py, 1 - c), sibling).wait_recv()
            for w in range(nw):
                for k, (px, py) in enumerate(chips):
                    copy(w, k, half(w, me, c), (px, py, c)).wait_send()
                    copy(w, 3 + k, half(w, 2 * px + py, c), sibling).wait_send()

        return start, finish

    return _Side(staged, [jax.ShapeDtypeStruct((4,) + s.shape, s.dtype) for s in shards], 6 * nw, build,
                 aliases={w: w for w in range(nw)})


def _gather8(name, v, side=None):
    rows = v.shape[0]

    def body(v_ref, all_ref, sum_ref, send_sems, recv_sems):
        x, y, c, _ = _place()
        me = 4 * x + 2 * y + c
        all_ref[me] = v_ref[...]
        copies = []
        for k in range(1, 8):
            px, py, pc = x ^ (k >> 2), y ^ ((k >> 1) & 1), c ^ (k & 1)
            copies.append(pltpu.make_async_remote_copy(
                src_ref=v_ref, dst_ref=all_ref.at[me], send_sem=send_sems.at[k - 1], recv_sem=recv_sems.at[k - 1],
                device_id=(px, py, pc), device_id_type=MESH))
            copies[-1].start()
        for k in range(1, 8):
            px, py, pc = x ^ (k >> 2), y ^ ((k >> 1) & 1), c ^ (k & 1)
            theirs = all_ref.at[4 * px + 2 * py + pc]
            pltpu.make_async_remote_copy(
                src_ref=theirs, dst_ref=theirs, send_sem=send_sems.at[k - 1], recv_sem=recv_sems.at[k - 1],
                device_id=(px, py, pc), device_id_type=MESH).wait_recv()
        for cp in copies:
            cp.wait_send()
        tot = all_ref[0]
        for d in range(1, 8):
            tot = tot + all_ref[d]
        sum_ref[...] = tot

    vm = pl.BlockSpec(memory_space=pltpu.VMEM)
    return _pcall(name, body, (), [vm], [vm, vm],
                  [jax.ShapeDtypeStruct((8, rows, LANES), F32), jax.ShapeDtypeStruct((rows, LANES), F32)],
                  [pltpu.SemaphoreType.DMA((7,)), pltpu.SemaphoreType.DMA((7,))], None, (v,), side)


def _region(ref, col_sharded, shape, j, h):
    r, ccols = shape
    if col_sharded:
        return ref.at[pl.ds(h * (r // 2), r // 2), pl.ds(j * (ccols // 4), ccols // 4)]
    n = r // 8
    return ref.at[pl.ds((2 * j + h) * n, n), :]


def _region_shape(col_sharded, shape):
    r, ccols = shape
    return (r // 2, ccols // 4) if col_sharded else (r // 8, ccols)


def _exchange(copies):
    def build(ins, outs, send_sems, recv_sems):
        def start():
            for cp in copies(ins, outs, send_sems, recv_sems):
                cp.start()

        def finish():
            for cp in copies(ins, outs, send_sems, recv_sems):
                cp.wait()

        return start, finish
    return build


def _swap_side(grads, kinds):
    nw = len(grads)

    def copies(ins, theirs, send_sems, recv_sems):
        x, y, c, _ = _place()
        return [pltpu.make_async_remote_copy(
            src_ref=_region(ins[w], kinds[w], grads[w].shape, j, 1 - c), dst_ref=theirs[w].at[j],
            send_sem=send_sems.at[4 * w + j], recv_sem=recv_sems.at[4 * w + j], device_id=(x, y, 1 - c), device_id_type=MESH)
            for w in range(nw) for j in range(4)]

    shapes = [jax.ShapeDtypeStruct((4,) + _region_shape(kinds[w], grads[w].shape), F32) for w in range(nw)]
    return _Side(grads, shapes, 4 * nw, _exchange(copies))


def _scatter_side(parts):
    nw = len(parts)

    def copies(ins, outs, send_sems, recv_sems):
        x, y, c, chips = _place()
        return [pltpu.make_async_remote_copy(
            src_ref=ins[w].at[2 * px + py], dst_ref=outs[w].at[k], send_sem=send_sems.at[3 * w + k],
            recv_sem=recv_sems.at[3 * w + k], device_id=(px, py, c), device_id_type=MESH)
            for w in range(nw) for k, (px, py) in enumerate(chips)]

    shapes = [jax.ShapeDtypeStruct((3,) + p.shape[1:], p.dtype) for p in parts]
    return _Side(parts, shapes, 3 * nw, _exchange(copies))


def _share_side(halves):
    nw = len(halves)

    def copies(ins, outs, send_sems, recv_sems):
        x, y, c, _ = _place()
        return [pltpu.make_async_remote_copy(
            src_ref=ins[w], dst_ref=outs[w].at[c], send_sem=send_sems.at[w], recv_sem=recv_sems.at[w],
            device_id=(x, y, 1 - c), device_id_type=MESH) for w in range(nw)]

    return _Side(halves, [jax.ShapeDtypeStruct((2,) + h.shape, F32) for h in halves], nw, _exchange(copies))


EW_BLOCK = 512 * 1024


def _ew_tile(rows, cols):
    limit = max(8, EW_BLOCK // cols)
    return max(d for d in range(8, min(rows, limit) + 1, 8) if rows % d == 0)


def _indexed_sum(name, fn, grid, in_specs, out_specs, out_shape, index, arrays):
    def kern(_, *refs):
        n_in = len(in_specs)
        vals = fn(*[r[...] for r in refs[:n_in]])
        for r, v in zip(refs[n_in:], vals):
            r[...] = v.astype(r.dtype)

    return pl.pallas_call(
        kern, name=name, out_shape=out_shape,
        grid_spec=pltpu.PrefetchScalarGridSpec(num_scalar_prefetch=1, grid=grid, in_specs=in_specs, out_specs=out_specs),
        compiler_params=_params(("parallel",) * len(grid)),
    )(index.astype(jnp.int32).reshape(1), *arrays)


def _pair_sums(names, grads, theirs):
    parts, parts_bf = {}, {}
    for n, other in zip(names, theirs):
        _, rr, cc = other.shape
        tile = _ew_tile(rr, cc)
        nb = rr // tile
        if COL_SHARDED[n]:
            mine = pl.BlockSpec((tile, cc), lambda j, i, c: (c[0] * nb + i, j))
        else:
            mine = pl.BlockSpec((tile, cc), lambda j, i, c: ((2 * j + c[0]) * nb + i, 0))
        flat = pl.BlockSpec((tile, cc), lambda j, i, c: (j * nb + i, 0))
        p, pb = _indexed_sum(f"grad_pair_sum_{n}", lambda u, v: (u + v, u + v), (4, nb), [mine, flat], [flat, flat],
                             [jax.ShapeDtypeStruct((4 * rr, cc), F32), jax.ShapeDtypeStruct((4 * rr, cc), BF)],
                             lax.axis_index("c"), (grads[n], other.reshape(4 * rr, cc)))
        parts[n], parts_bf[n] = p.reshape(4, rr, cc), pb.reshape(4, rr, cc)
    return parts, parts_bf


def _chip_sums(names, parts, landed):
    halves = {}
    for n, got in zip(names, landed):
        _, rr, cc = got.shape
        tile = _ew_tile(rr, cc)
        nb = rr // tile
        own = pl.BlockSpec((tile, cc), lambda i, chip: (chip[0] * nb + i, 0))
        peer = lambda k: pl.BlockSpec((tile, cc), lambda i, chip: (k * nb + i, 0))
        halves[n] = _indexed_sum(f"grad_chip_sum_{n}", lambda o, k0, k1, k2: (((o + k0) + k1) + k2,), (nb,),
                                 [own, peer(0), peer(1), peer(2)], [pl.BlockSpec((tile, cc), lambda i, chip: (i, 0))],
                                 [jax.ShapeDtypeStruct((rr, cc), F32)], 2 * lax.axis_index("x") + lax.axis_index("y"),
                                 (parts[n].reshape(4 * rr, cc),) + (got.reshape(3 * rr, cc),) * 3)[0]
    return halves


def _both_halves(mine, shared):
    both = lax.dynamic_update_index_in_dim(shared, mine, lax.axis_index("c"), 0)
    return both.reshape(2 * mine.shape[0], mine.shape[1])


BIG = ("w_in", "w_attn_proj", "w_conv_proj", "w_out", "w_cq", "w_ckv", "w_co", "w_up", "w_down")
COL_SHARDED = {"w_in": True, "w_attn_proj": True, "w_conv_proj": True, "w_out": False, "w_cq": False,
               "w_ckv": True, "w_co": False, "w_up": True, "w_down": False}
SMALL = ("g_mix", "b_gate", "conv_b", "conv_ln_g", "conv_ln_b", "g_cross", "g_mem", "g_mlp", "g_final")
ORDER = ("g_mix", "w_in", "b_gate", "conv_w", "conv_b", "conv_ln_g", "conv_ln_b", "w_attn_proj", "w_conv_proj", "w_out",
         "g_cross", "g_mem", "w_cq", "w_ckv", "w_co", "g_mlp", "w_up", "w_down", "g_final")


def _pad_rows(flat, rows):
    return jnp.pad(flat, (0, rows * LANES - flat.shape[0])).reshape(rows, LANES)


WAVE_MLP = ("w_down", "w_up")
WAVE_MID = ("w_co", "w_cq", "w_ckv", "w_out", "w_attn_proj", "w_conv_proj")


def _local_step(x, mem, tgt, shards, small, conv_w_full):
    t = x.shape[0]
    tr = 512
    c_tab, s_tab = _rope_tables(t)
    row = lambda v: v.reshape(1, -1)
    g_mix, g_cross, g_mem, g_mlp, g_final = (row(small[n]) for n in ("g_mix", "g_cross", "g_mem", "g_mlp", "g_final"))
    b_gate, conv_b, ln_g, ln_b = (row(small[n]) for n in ("b_gate", "conv_b", "conv_ln_g", "conv_ln_b"))
    w32 = jnp.pad(conv_w_full, ((0, CONV_KP - CONV_K), (0, 0)))

    (u,), (w_in_all,) = _rowcall("mix_norm", lambda a, g: (_rms(a, g),), t, tr, [(x, D_MODEL, 0)], [g_mix], [(D_MODEL, BF)],
                                 side=_gather_side([shards["w_in"]]))
    wfull = {"w_in": w_in_all}

    def keep(names, gathered):
        for n, g in zip(names, gathered):
            wfull[n] = g if COL_SHARDED[n] else g.reshape(1, 4 * g.shape[1], g.shape[2])

    z, gathered = _mm("in_proj", u, w_in_all, out_dtypes=(BF,), side=_gather_side([shards[n] for n in WAVE_MID]))
    keep(WAVE_MID, gathered)
    (c1,), gathered = _conv_fwd(z, w32, conv_b, t, side=_gather_side([shards[n] for n in WAVE_MLP]))
    keep(WAVE_MLP, gathered)
    qkv = _rope_split(z, c_tab, s_tab, t)
    qkv_cls, outs, lses = [], [], []
    for g, dil in enumerate(DILATIONS):
        ops = tuple((a.reshape(t, GROUP_W), 0) for a in qkv[3 * g:3 * g + 3])
        qkv_cls.append(ops)
        o_g, l_g = _attn_fwd(f"attn_fwd_{g}", *ops, t, dil)
        outs.append(o_g)
        lses.append(l_g)
    attn, lse = _merge_classes(outs, lses, t)
    y_attn = _mm("attn_proj", attn, wfull["w_attn_proj"])
    (c2,) = _rowcall("conv_ln_silu", _ln_silu_fn, t, tr, [(c1, CONV_CH, 0)], [ln_g, ln_b], [(CONV_CH, BF)])
    tn_cp = wfull["w_conv_proj"].shape[2]
    merged, y_conv = _mm("conv_proj", c2, wfull["w_conv_proj"], epi=_gate_epi, out_dtypes=(BF, F32),
                         extras=(y_attn, (z, GATE_A_COL // tn_cp), (z, GATE_B_COL // tn_cp)),
                         params=((b_gate, 0), (b_gate, D_MODEL // tn_cp)))
    x1, uq = _mm("out_proj", merged, wfull["w_out"], extras=(x,), params=(g_cross,), epi=_res_norm_epi, out_dtypes=(F32, BF))
    (mn,) = _rowcall("mem_norm", lambda a, g: (_rms(a, g),), N_MEM, N_MEM, [(mem, D_MODEL, 0)], [g_mem], [(D_MODEL, BF)])
    cq = _mm("cross_q", uq, wfull["w_cq"], out_dtypes=(BF,))
    ckv = _mm("cross_kv", mn, wfull["w_ckv"], out_dtypes=(BF,))
    co = _cross_fwd(cq, ckv, t)
    x2, um = _mm("cross_out", co, wfull["w_co"], extras=(x1,), params=(g_mlp,), epi=_res_norm_epi, out_dtypes=(F32, BF))
    hact = _mm("mlp_up", um, wfull["w_up"], out_dtypes=(BF,), epi=lambda acc: (jnp.square(jnp.maximum(acc, 0.0)),))
    d3, d3b, loss_row, dg_final = _mm("mlp_down", hact, wfull["w_down"], extras=(x2, tgt), params=(g_final,), epi=_final_epi,
                                      out_dtypes=(F32, BF), sums=[(1, LANES), (1, D_MODEL)])

    gw = {}
    dhp = _mm("mlp_down_bwd", d3b, wfull["w_down"], nt=True, extras=(hact,), out_dtypes=(BF,),
              epi=lambda acc, h: (acc * 2.0 * jnp.sqrt(h.astype(F32)),))
    gw["w_down"] = _mm_tn("mlp_down_wgrad", hact, d3b)
    gw["w_up"] = _mm_tn("mlp_up_wgrad", um, dhp)

    def swap_of(names):
        return _swap_side([gw[n] for n in names], [COL_SHARDED[n] for n in names])

    (d2, d2b, dg_mlp), theirs = _mm("mlp_up_bwd", dhp, wfull["w_up"], nt=True, extras=(x2, d3), params=(g_mlp,),
                                    epi=_norm_bwd_epi, out_dtypes=(F32, BF), sums=[(1, D_MODEL)], side=swap_of(WAVE_MLP))
    parts_mlp, parts_bf_mlp = _pair_sums(WAVE_MLP, gw, theirs)

    dco = _mm("cross_out_bwd", d2b, wfull["w_co"], nt=True, out_dtypes=(BF,))
    gw["w_co"] = _mm_tn("cross_out_wgrad", co, d2b)
    dcq, dckv = _cross_bwd(cq, ckv, dco, t)
    gw["w_cq"] = _mm_tn("cross_q_wgrad", uq, dcq)
    d1, d1b, dg_cross = _mm("cross_q_bwd", dcq, wfull["w_cq"], nt=True, extras=(x1, d2), params=(g_cross,), epi=_norm_bwd_epi,
                            out_dtypes=(F32, BF), sums=[(1, D_MODEL)])
    gw["w_ckv"] = _mm_tn("cross_kv_wgrad", mn, dckv, tk=N_MEM)
    dmn = _mm("cross_kv_bwd", dckv, wfull["w_ckv"], nt=True)
    (dg_mem,) = _rowcall("mem_norm_bwd", lambda a, dn, g: (_rms_bwd(a, g, dn)[1],), N_MEM, N_MEM,
                         [(mem, D_MODEL, 0), (dmn, D_MODEL, 0)], [g_mem], [], accs=[(1, D_MODEL)])

    dya, dyc, dgate_a, dgate_b, dbg_a, dbg_b = _mm(
        "out_proj_bwd", d1b, wfull["w_out"], nt=True, tm=min(512, t), epi=_gate_bwd_epi, out_dtypes=(BF,) * 4,
        extras=(y_attn, y_conv, (z, GATE_A_COL // D_MODEL), (z, GATE_B_COL // D_MODEL)), params=(b_gate,),
        sums=[(1, D_MODEL), (1, D_MODEL)])
    dbg = jnp.concatenate([dbg_a, dbg_b], axis=1)
    gw["w_out"] = _mm_tn("out_proj_wgrad", merged, d1b)
    gw["w_attn_proj"] = _mm_tn("attn_proj_wgrad", attn, dya)
    dattn = _mm("attn_proj_bwd", dya, wfull["w_attn_proj"], nt=True)
    gw["w_conv_proj"] = _mm_tn("conv_proj_wgrad", c2, dyc)
    dc2 = _mm("conv_proj_bwd", dyc, wfull["w_conv_proj"], nt=True)
    (dc1, dlng, dlnb), theirs = _rowcall("conv_ln_silu_bwd", _ln_silu_bwd_fn, t, tr, [(c1, CONV_CH, 0), (dc2, CONV_CH, 0)],
                                         [ln_g, ln_b], [(CONV_CH, F32)], accs=[(1, CONV_CH), (1, CONV_CH)],
                                         side=swap_of(WAVE_MID))
    parts_mid, parts_bf_mid = _pair_sums(WAVE_MID, gw, theirs)
    (dglu_a, dglu_b, dconv), landed = _conv_bwd(z, dc1, w32, t, side=_scatter_side([parts_bf_mlp[n] for n in WAVE_MLP]))
    halves_mlp = _chip_sums(WAVE_MLP, parts_mlp, landed)

    dl0, cls1, cls2 = _attn_bwd_prep(dattn, attn, lse, t)
    dqs, dks, dvs = [], [], []
    for g, (dil, (da_c, dl_c, ls_c)) in enumerate(zip(DILATIONS, ((dattn, dl0, lse), cls1, cls2))):
        res = _attn_bwd(f"attn_bwd_{g}", *qkv_cls[g], da_c, dl_c, ls_c, t, dil,
                        side=_share_side([halves_mlp[n] for n in WAVE_MLP]) if g == 0 else None)
        if g == 0:
            res, others = res
            gshard = {n: _both_halves(halves_mlp[n], o) for n, o in zip(WAVE_MLP, others)}
        dqs.append(res[0])
        dks.append(res[1])
        dvs.append(res[2])
    (dz,), landed = _rope_bwd_join(dqs, dks, dvs, c_tab, s_tab, (dglu_a, dglu_b, dgate_a, dgate_b), t,
                                   side=_scatter_side([parts_bf_mid[n] for n in WAVE_MID]))
    halves_mid = _chip_sums(WAVE_MID, parts_mid, landed)

    gw_in_full, others = _mm_tn("in_proj_wgrad", u, dz, side=_share_side([halves_mid[n] for n in WAVE_MID]))
    gshard.update({n: _both_halves(halves_mid[n], o) for n, o in zip(WAVE_MID, others)})
    gw_in = {"w_in": gw_in_full}
    du_a, theirs = _mm("in_proj_bwd_a", dz, wfull["w_in"], nt=True, k_part=(0, 2), side=_swap_side([gw_in["w_in"]], [True]))
    parts, parts_bf = _pair_sums(("w_in",), gw_in, theirs)
    (gx, dg_mix), landed_in = _mm("in_proj_bwd_b", dz, wfull["w_in"], nt=True, k_part=(1, 2), extras=(du_a, x, d1),
                                  params=(g_mix,), epi=lambda acc, first, a, dres, g: _norm_bwd_epi(acc + first, a, dres, g)[1:],
                                  sums=[(1, D_MODEL)], side=_scatter_side([parts_bf["w_in"]]))
    halves = _chip_sums(("w_in",), parts, landed_in)

    gsmall = {"g_mix": dg_mix, "b_gate": dbg, "conv_b": dconv[CONV_K:CONV_K + 1], "conv_ln_g": dlng, "conv_ln_b": dlnb,
              "g_cross": dg_cross, "g_mem": dg_mem, "g_mlp": dg_mlp, "g_final": dg_final, "conv_w": dconv[:CONV_K]}
    return loss_row, gx, gshard, gsmall, halves["w_in"]


def kernel(x, mem, g_mix, w_in, b_gate, conv_w, conv_b, conv_ln_g, conv_ln_b, w_attn_proj, w_conv_proj, w_out, g_cross, g_mem, w_cq, w_ckv, w_co, g_mlp, w_up, w_down, g_final, loss_target, m_g_mix, m_w_in, m_b_gate, m_conv_w, m_conv_b, m_conv_ln_g, m_conv_ln_b, m_w_attn_proj, m_w_conv_proj, m_w_out, m_g_cross, m_g_mem, m_w_cq, m_w_ckv, m_w_co, m_g_mlp, m_w_up, m_w_down, m_g_final, v_g_mix, v_w_in, v_b_gate, v_conv_w, v_conv_b, v_conv_ln_g, v_conv_ln_b, v_w_attn_proj, v_w_conv_proj, v_w_out, v_g_cross, v_g_mem, v_w_cq, v_w_ckv, v_w_co, v_g_mlp, v_w_up, v_w_down, v_g_final):
    w = dict(g_mix=g_mix, w_in=w_in, b_gate=b_gate, conv_w=conv_w, conv_b=conv_b, conv_ln_g=conv_ln_g, conv_ln_b=conv_ln_b,
             w_attn_proj=w_attn_proj, w_conv_proj=w_conv_proj, w_out=w_out, g_cross=g_cross, g_mem=g_mem, w_cq=w_cq,
             w_ckv=w_ckv, w_co=w_co, g_mlp=g_mlp, w_up=w_up, w_down=w_down, g_final=g_final)
    mo = dict(g_mix=m_g_mix, w_in=m_w_in, b_gate=m_b_gate, conv_w=m_conv_w, conv_b=m_conv_b, conv_ln_g=m_conv_ln_g,
              conv_ln_b=m_conv_ln_b, w_attn_proj=m_w_attn_proj, w_conv_proj=m_w_conv_proj, w_out=m_w_out, g_cross=m_g_cross,
              g_mem=m_g_mem, w_cq=m_w_cq, w_ckv=m_w_ckv, w_co=m_w_co, g_mlp=m_g_mlp, w_up=m_w_up, w_down=m_w_down,
              g_final=m_g_final)
    vo = dict(g_mix=v_g_mix, w_in=v_w_in, b_gate=v_b_gate, conv_w=v_conv_w, conv_b=v_conv_b, conv_ln_g=v_conv_ln_g,
              conv_ln_b=v_conv_ln_b, w_attn_proj=v_w_attn_proj, w_conv_proj=v_w_conv_proj, w_out=v_w_out, g_cross=v_g_cross,
              g_mem=v_g_mem, w_cq=v_w_cq, w_ckv=v_w_ckv, w_co=v_w_co, g_mlp=v_g_mlp, w_up=v_w_up, w_down=v_w_down,
              g_final=v_g_final)
    shapes = {n: w[n].shape for n in ORDER}
    two_d = lambda a: a.reshape(a.shape[-2], a.shape[-1])
    chip = 2 * lax.axis_index("x") + lax.axis_index("y")

    shards = {n: two_d(w[n]).astype(BF) for n in BIG}
    cw_rows = 48
    cw_all, _ = _gather8("gather_conv_w", _pad_rows(conv_w.reshape(-1), cw_rows))
    cw_shard = CONV_K * (CONV_CH // 4)
    conv_w_full = jnp.concatenate(
        [cw_all[2 * j].reshape(-1)[:cw_shard].reshape(CONV_K, CONV_CH // 4) for j in range(4)], axis=1)

    small = {n: w[n] for n in SMALL}
    loss_row, gx, gshard, gsmall, half_w_in = _local_step(two_d(x), two_d(mem), two_d(loss_target), shards, small,
                                                          conv_w_full)
    loss = lax.psum(loss_row[0, 0], ("x", "y", "c"))

    small_names = SMALL + ("conv_w",)
    flat = jnp.concatenate([gsmall[n].reshape(-1) for n in small_names])
    sm_rows = -(-flat.shape[0] // (8 * LANES)) * 8
    (_, sm_sum), others = _gather8("reduce_small_grads", _pad_rows(flat, sm_rows), side=_share_side([half_w_in]))
    gshard["w_in"] = _both_halves(half_w_in, others[0])
    sm_sum = sm_sum.reshape(-1)
    off = 0
    for n in small_names:
        size = gsmall[n].size
        gshard[n] = sm_sum[off:off + size].reshape(gsmall[n].shape)
        off += size
    gshard["conv_w"] = lax.dynamic_slice_in_dim(gshard["conv_w"], chip * (CONV_CH // 4), CONV_CH // 4, axis=1)

    grads, deltas, new_m, new_v = {}, {}, {}, {}
    for n in BIG:
        d, m2, v2 = _adam(f"adamw_{n}", two_d(w[n]), gshard[n], two_d(mo[n]), two_d(vo[n]))
        grads[n], deltas[n], new_m[n], new_v[n] = (a.reshape(shapes[n]) for a in (gshard[n], d, m2, v2))
    pack = lambda src: jnp.concatenate([src[n].reshape(-1) for n in small_names])
    n_small = sum(w[n].size for n in small_names)
    ad_rows = -(-n_small // (8 * LANES)) * 8
    d, m2, v2 = _adam("adamw_small", *[_pad_rows(pack(src), ad_rows) for src in (w, gshard, mo, vo)])
    off = 0
    for n in small_names:
        size = w[n].size
        grads[n] = gshard[n].reshape(shapes[n])
        deltas[n], new_m[n], new_v[n] = (a.reshape(-1)[off:off + size].reshape(shapes[n]) for a in (d, m2, v2))
        off += size

    return (loss, gx.reshape(x.shape), *[grads[n] for n in ORDER], *[deltas[n] for n in ORDER],
            *[new_m[n] for n in ORDER], *[new_v[n] for n in ORDER])
```

```python
import functools

import jax
import jax.numpy as jnp
from jax import lax
from jax.experimental import pallas as pl
from jax.experimental.pallas import tpu as pltpu

F32 = jnp.float32
BF = jnp.bfloat16

D_MODEL = 1024
N_MEM = 256
HEAD_DIM = 128
HEADS_PER_GROUP = 4
DILATIONS = (1, 4, 16)
BLK = 128
GROUP_W = HEADS_PER_GROUP * HEAD_DIM
ATTN_WIDTH = 3 * GROUP_W
ROT_DIM = 32
ROPE_THETA = 500000.0
CONV_CH = 768
CONV_K = 31
CONV_KP = 32
IN_WIDTH = 8192
CROSS_HEADS = 4
CROSS_HEAD_DIM = 256
D_FF = 4096
EPS = 1e-6
ATTN_SCALE = HEAD_DIM ** -0.5
CROSS_SCALE = CROSS_HEAD_DIM ** -0.5
NEG = -1e30

ADAM_LR = 0.001
ADAM_B1 = 0.9
ADAM_B2 = 0.999
ADAM_EPS = 1e-08
ADAM_WD = 0.01
ADAM_STEP = 10

LANES = 128
SUBLANES = 8
VMEM_LIMIT = 56 * 1024 * 1024
MESH = pl.DeviceIdType.MESH
ANY = pl.BlockSpec(memory_space=pl.ANY)

GLU_A_COL = 3 * ATTN_WIDTH
GLU_B_COL = GLU_A_COL + CONV_CH
GATE_A_COL = GLU_B_COL + CONV_CH
GATE_B_COL = GATE_A_COL + D_MODEL


def _params(sem=None):
    return pltpu.CompilerParams(dimension_semantics=sem, vmem_limit_bytes=VMEM_LIMIT)


def _dot(a, b):
    return lax.dot_general(a, b, (((1,), (0,)), ((), ())), preferred_element_type=F32)


def _dot_nt(a, b):
    return lax.dot_general(a, b, (((1,), (1,)), ((), ())), preferred_element_type=F32)


def _dot_tn(a, b):
    return lax.dot_general(a, b, (((0,), (0,)), ((), ())), preferred_element_type=F32)


def _sig(x):
    return 1.0 / (1.0 + jnp.exp(-x))


def _glu(a, b):
    return a.astype(F32) * _sig(b.astype(F32))


class _Side:
    def __init__(self, arrays, out_shapes, n_sems, build, aliases=None):
        self.arrays, self.out_shapes, self.n_sems, self.build = list(arrays), list(out_shapes), n_sems, build
        self.aliases = aliases or {}


def _pcall(name, kern, grid, in_specs, out_specs, out_shape, scratch_shapes, sem, args, side=None):
    in_specs, out_specs, out_shape, scratch_shapes = list(in_specs), list(out_specs), list(out_shape), list(scratch_shapes)
    if side is None:
        return pl.pallas_call(kern, name=name, grid=grid, in_specs=in_specs, out_specs=out_specs, out_shape=out_shape,
                              scratch_shapes=scratch_shapes, compiler_params=_params(sem))(*args)
    ni, no, nsc = len(in_specs), len(out_specs), len(scratch_shapes)
    nsi, nso = len(side.arrays), len(side.out_shapes)

    def wrapped(*refs):
        ins, side_ins = refs[:ni], refs[ni:ni + nsi]
        outs, side_outs = refs[ni + nsi:ni + nsi + no], refs[ni + nsi + no:ni + nsi + no + nso]
        scratch = refs[ni + nsi + no + nso:ni + nsi + no + nso + nsc]
        send_sems, recv_sems = refs[-2:]
        start, finish = side.build(side_ins, side_outs, send_sems, recv_sems)
        if grid:
            first = functools.reduce(jnp.logical_and, [pl.program_id(a) == 0 for a in range(len(grid))])
            last = functools.reduce(jnp.logical_and, [pl.program_id(a) == g - 1 for a, g in enumerate(grid)])
            pl.when(first)(start)
            kern(*ins, *outs, *scratch)
            pl.when(last)(finish)
        else:
            start()
            kern(*ins, *outs, *scratch)
            finish()

    res = pl.pallas_call(
        wrapped, name=name, grid=grid, in_specs=in_specs + [ANY] * nsi, out_specs=out_specs + [ANY] * nso,
        out_shape=out_shape + side.out_shapes,
        scratch_shapes=scratch_shapes + [pltpu.SemaphoreType.DMA((side.n_sems,)), pltpu.SemaphoreType.DMA((side.n_sems,))],
        input_output_aliases={ni + k: no + v for k, v in side.aliases.items()},
        compiler_params=_params(("arbitrary",) * len(grid) if grid else None),
    )(*args, *side.arrays)
    return res[:no], res[no:]


def _rowcall(name, fn, n_rows, tile, ins, params, outs, accs=(), side=None):
    tile = min(tile, n_rows)
    ni, npar, no, na = len(ins), len(params), len(outs), len(accs)

    def kern(*refs):
        in_refs = refs[:ni + npar]
        o_refs = refs[ni + npar:ni + npar + no]
        a_refs = refs[ni + npar + no:]
        vals = fn(*[r[...] for r in in_refs])
        for r, v in zip(o_refs, vals[:no]):
            r[...] = v.astype(r.dtype)
        if na:
            @pl.when(pl.program_id(0) == 0)
            def _():
                for r in a_refs:
                    r[...] = jnp.zeros_like(r)
            for r, v in zip(a_refs, vals[no:]):
                r[...] += v

    in_specs = []
    arrays = []
    for spec in ins:
        arr, width, cb = spec[0], spec[1], spec[2]
        rb = spec[3] if len(spec) > 3 else 0
        in_specs.append(pl.BlockSpec((tile, width), functools.partial(lambda i, cb, rb: (i + rb, cb), cb=cb, rb=rb)))
        arrays.append(arr)
    for p in params:
        in_specs.append(pl.BlockSpec(p.shape, lambda i: (0, 0)))
        arrays.append(p)
    out_specs = [pl.BlockSpec((tile, w), lambda i: (i, 0)) for w, _ in outs]
    out_specs += [pl.BlockSpec(s, lambda i: (0, 0)) for s in accs]
    out_shape = [jax.ShapeDtypeStruct((n_rows, w), dt) for w, dt in outs]
    out_shape += [jax.ShapeDtypeStruct(s, F32) for s in accs]
    return _pcall(name, kern, (n_rows // tile,), in_specs, out_specs, out_shape, [],
                  ("arbitrary",) if na else ("parallel",), arrays, side)


def _mm(name, a, w3, *, nt=False, extras=(), params=(), epi=None, out_dtypes=(F32,), sums=(), tm=None, tn=None, tk=None,
        k_part=(0, 1), side=None):
    m, ka = a.shape
    ns, r, cs = w3.shape
    if not nt:
        k_dim, n = r, ns * cs
        tn = tn or min(cs, 2048)
        tk = tk or min(k_dim, 1024)
    else:
        k_dim, n = ns * cs, r
        tn = tn or min(r, 2048)
        tk = tk or min(cs, 1024)
    assert ka == k_dim, (name, a.shape, w3.shape)
    assert not sums or tn == n, name
    nk = k_dim // tk // k_part[1]
    k0 = k_part[0] * nk
    if not nt:
        nbs = cs // tn
        w_spec = pl.BlockSpec((None, tk, tn), lambda i, j, k: (j // nbs, k + k0, j % nbs))
    else:
        kbs = cs // tk
        w_spec = pl.BlockSpec((None, tn, tk), lambda i, j, k: ((k + k0) // kbs, j, (k + k0) % kbs))
    tm = tm or min(m, 1024)
    ne, no, nsum = len(extras) + len(params), len(out_dtypes), len(sums)

    def kern(a_ref, w_ref, *rest):
        e_refs = rest[:ne]
        o_refs = rest[ne:ne + no]
        s_refs = rest[ne + no:ne + no + nsum]

        def part():
            av = a_ref[...].astype(BF)
            return _dot_nt(av, w_ref[...]) if nt else _dot(av, w_ref[...])

        def finish(res):
            vals = epi(res, *[e[...] for e in e_refs]) if epi else (res,)
            for o, v in zip(o_refs, vals[:no]):
                o[...] = v.astype(o.dtype)
            for sr, v in zip(s_refs, vals[no:]):
                sr[...] += v

        if nsum:
            @pl.when(jnp.logical_and(pl.program_id(0) == 0, pl.program_id(2) == 0))
            def _():
                for sr in s_refs:
                    sr[...] = jnp.zeros_like(sr)

        if nk == 1:
            finish(part())
            return
        acc = rest[ne + no + nsum]
        k = pl.program_id(2)

        @pl.when(k == 0)
        def _():
            acc[...] = part()

        @pl.when(jnp.logical_and(k > 0, k < nk - 1))
        def _():
            acc[...] += part()

        @pl.when(k == nk - 1)
        def _():
            finish(acc[...] + part())

    in_specs = [pl.BlockSpec((tm, tk), lambda i, j, k: (i, k + k0)), w_spec]
    split = lambda items: [(it if isinstance(it, tuple) else (it, None)) for it in items]
    extras, params = split(extras), split(params)
    in_specs += [pl.BlockSpec((tm, tn), functools.partial(lambda i, j, k, off: (i, j + off), off=off or 0)) for _, off in extras]
    in_specs += [pl.BlockSpec(p.shape, lambda i, j, k: (0, 0)) if off is None else
                 pl.BlockSpec((p.shape[0], tn), functools.partial(lambda i, j, k, off: (0, j + off), off=off)) for p, off in params]
    extras, params = [e for e, _ in extras], [p for p, _ in params]
    out_specs = [pl.BlockSpec((tm, tn), lambda i, j, k: (i, j)) for _ in out_dtypes]
    out_specs += [pl.BlockSpec(sh, lambda i, j, k: (0, 0)) for sh in sums]
    out_shape = [jax.ShapeDtypeStruct((m, n), dt) for dt in out_dtypes] + [jax.ShapeDtypeStruct(sh, F32) for sh in sums]
    res = _pcall(name, kern, (m // tm, n // tn, nk), in_specs, out_specs, out_shape,
                 [pltpu.VMEM((tm, tn), F32)] if nk > 1 else [],
                 ("arbitrary",) * 3 if nsum else ("parallel", "parallel", "arbitrary"), (a, w3, *extras, *params), side)
    main = res[0] if side is not None else res
    main = main[0] if no + nsum == 1 else main
    return (main, res[1]) if side is not None else main


def _mm_tn(name, a, b, tm=None, tn=None, tk=None, side=None):
    t, ka = a.shape
    _, n = b.shape
    tm = tm or min(ka, 2048)
    tn = tn or min(n, 2048)
    tk = tk or min(t, 1024)

    def kern(a_ref, b_ref, o_ref):
        def part():
            return _dot_tn(a_ref[...].astype(BF), b_ref[...].astype(BF))

        @pl.when(pl.program_id(2) == 0)
        def _():
            o_ref[...] = part()

        @pl.when(pl.program_id(2) > 0)
        def _():
            o_ref[...] += part()

    res = _pcall(name, kern, (ka // tm, n // tn, t // tk),
                 [pl.BlockSpec((tk, tm), lambda i, j, k: (k, i)), pl.BlockSpec((tk, tn), lambda i, j, k: (k, j))],
                 [pl.BlockSpec((tm, tn), lambda i, j, k: (i, j))], [jax.ShapeDtypeStruct((ka, n), F32)], [],
                 ("parallel", "parallel", "arbitrary"), (a, b), side)
    return (res[0][0], res[1]) if side is not None else res[0]


def _rms(x, g):
    return x * lax.rsqrt(jnp.mean(x * x, axis=-1, keepdims=True) + EPS) * g


def _rms_bwd(x, g, dy):
    r = lax.rsqrt(jnp.mean(x * x, axis=-1, keepdims=True) + EPS)
    xh = x * r
    dxh = dy * g
    dx = r * (dxh - xh * jnp.mean(dxh * xh, axis=-1, keepdims=True))
    return dx, jnp.sum(dy * xh, axis=0, keepdims=True)


def _rot(t, c, s):
    lane = lax.broadcasted_iota(jnp.int32, t.shape, 1)
    swapped = jnp.where(lane < ROT_DIM // 2, pltpu.roll(t, HEAD_DIM - ROT_DIM // 2, 1), pltpu.roll(t, ROT_DIM // 2, 1))
    return t * c + swapped * s


def _rope_tables(t):
    half = ROT_DIM // 2
    pos = jnp.arange(t, dtype=F32)
    inv_freq = ROPE_THETA ** (-jnp.arange(0, ROT_DIM, 2, dtype=F32) / ROT_DIM)
    ang = pos[:, None] * inv_freq[None, :]
    cos, sin = jnp.cos(ang), jnp.sin(ang)
    ones = jnp.ones((t, HEAD_DIM - ROT_DIM), F32)
    c_tab = jnp.concatenate([cos, cos, ones], axis=1)
    s_tab = jnp.concatenate([-sin, sin, 0.0 * ones], axis=1)
    return c_tab, s_tab


def _merge_fn(o0, o1, o2, l0, l1, l2):
    m = jnp.maximum(jnp.maximum(l0, l1), l2)
    e0, e1, e2 = jnp.exp(l0 - m), jnp.exp(l1 - m), jnp.exp(l2 - m)
    tot = e0 + e1 + e2
    return (e0 * o0 + e1 * o1 + e2 * o2) / tot, m + jnp.log(tot)


def _ln_parts(c1):
    mu = jnp.mean(c1, axis=-1, keepdims=True)
    xc = c1 - mu
    r = lax.rsqrt(jnp.mean(xc * xc, axis=-1, keepdims=True) + EPS)
    return xc * r, r


def _ln_silu_fn(c1, g, b):
    xh, _ = _ln_parts(c1)
    yl = xh * g + b
    return (yl * _sig(yl),)


def _ln_silu_bwd_fn(c1, dout, g, b):
    xh, r = _ln_parts(c1)
    yl = xh * g + b
    s = _sig(yl)
    dyl = dout * (s + yl * s * (1.0 - s))
    dxh = dyl * g
    dx = r * (dxh - jnp.mean(dxh, axis=-1, keepdims=True) - xh * jnp.mean(dxh * xh, axis=-1, keepdims=True))
    return dx, jnp.sum(dyl * xh, axis=0, keepdims=True), jnp.sum(dyl, axis=0, keepdims=True)


def _gate_epi(yc, ya, ga, gb, ba, bb):
    return _sig(ga + ba) * ya + _sig(gb + bb) * yc, yc


def _gate_bwd_epi(dm, ya, yc, ga, gb, bg):
    sa = _sig(ga + bg[:, :D_MODEL])
    sb = _sig(gb + bg[:, D_MODEL:])
    dga = dm * ya * sa * (1.0 - sa)
    dgb = dm * yc * sb * (1.0 - sb)
    return dm * sa, dm * sb, dga, dgb, jnp.sum(dga, axis=0, keepdims=True), jnp.sum(dgb, axis=0, keepdims=True)


def _res_norm_epi(acc, res, g):
    xn = res + acc
    return xn, _rms(xn, g)


def _norm_bwd_epi(acc, a, dres, g):
    dx, dg = _rms_bwd(a, g, acc)
    return dres + dx, dres + dx, dg


def _final_epi(acc, res, tgt, g):
    return _final_fn(res + acc, tgt, g)


def _final_fn(x3, tgt, g):
    err = _rms(x3, g) - tgt
    lrow = jnp.sum(err * err, axis=-1, keepdims=True) * (0.5 / D_MODEL)
    lsum = jnp.sum(lrow, axis=0, keepdims=True)
    dx, dg = _rms_bwd(x3, g, err * (1.0 / D_MODEL))
    return dx, dx, jnp.broadcast_to(lsum, (1, LANES)), dg


def _attn_geometry(t, dil):
    cls = t // dil
    rows = min(8 * BLK, cls)
    return rows, rows // BLK, cls // rows


def _head_lanes(h):
    return slice(h * HEAD_DIM, (h + 1) * HEAD_DIM)


def _band_mask():
    row = lax.broadcasted_iota(jnp.int32, (BLK, 2 * BLK), 0)
    col = lax.broadcasted_iota(jnp.int32, (BLK, 2 * BLK), 1)
    return jnp.logical_and(col >= row, col <= row + BLK), col


def _stage_window(scr, halo_ref, cur_ref):
    scr[:BLK, :] = halo_ref[...]
    scr[BLK:, :] = cur_ref[...]


def _attn_fwd(name, q, k, v, t, dil):
    rows, nbk, spc = _attn_geometry(t, dil)

    def kern(q_ref, k_ref, kh_ref, v_ref, vh_ref, o_ref, l_ref, k_scr, v_scr):
        i = pl.program_id(0)
        first_shift = jnp.where(i % spc == 0, BLK, 0)
        _stage_window(k_scr, kh_ref, k_ref)
        _stage_window(v_scr, vh_ref, v_ref)
        band, col = _band_mask()
        band_first = jnp.logical_and(band, col >= first_shift)
        for h in range(HEADS_PER_GROUP):
            hs = _head_lanes(h)
            for b in range(nbk):
                rs, win = slice(b * BLK, (b + 1) * BLK), slice(b * BLK, (b + 2) * BLK)
                s = jnp.where(band_first if b == 0 else band, _dot_nt(q_ref[rs, hs], k_scr[win, hs]) * ATTN_SCALE, NEG)
                m = jnp.max(s, axis=1, keepdims=True)
                p = jnp.exp(s - m)
                tot = jnp.sum(p, axis=1, keepdims=True)
                o_ref[rs, hs] = _dot(p.astype(BF), v_scr[win, hs]) / tot
                l_ref[rs, hs] = jnp.broadcast_to(m + jnp.log(tot), (BLK, HEAD_DIM))

    def cur(cb):
        return pl.BlockSpec((rows, GROUP_W), lambda i: (i, cb))

    def halo(cb):
        return pl.BlockSpec((BLK, GROUP_W), lambda i: (jnp.maximum(i * nbk - 1, 0), cb))

    (qa, qc), (ka, kc_), (va, vc_) = q, k, v
    return _pcall(name, kern, (t // rows,), [cur(qc), cur(kc_), halo(kc_), cur(vc_), halo(vc_)],
                  [pl.BlockSpec((rows, GROUP_W), lambda i: (i, 0))] * 2, [jax.ShapeDtypeStruct((t, GROUP_W), F32)] * 2,
                  [pltpu.VMEM((rows + BLK, GROUP_W), BF)] * 2, ("parallel",), (qa, ka, ka, va, va))


def _attn_bwd(name, q, k, v, da, dl, lse, t, dil, side=None):
    rows, nbk, spc = _attn_geometry(t, dil)
    nblk = t // BLK

    def kern(q_ref, qn_ref, k_ref, kh_ref, v_ref, vh_ref, da_ref, dan_ref, dl_ref, dln_ref, ls_ref, lsn_ref,
             dq_ref, dk_ref, dv_ref, k_scr, v_scr):
        i = pl.program_id(0)
        first_shift = jnp.where(i % spc == 0, BLK, 0)
        next_shift = jnp.where((i + 1) % spc == 0, BLK, 0)
        _stage_window(k_scr, kh_ref, k_ref)
        _stage_window(v_scr, vh_ref, v_ref)
        band, col = _band_mask()
        band_first = jnp.logical_and(band, col >= first_shift)
        row1 = lax.broadcasted_iota(jnp.int32, (BLK, BLK), 0)
        col1 = lax.broadcasted_iota(jnp.int32, (BLK, BLK), 1)
        pend_k, pend_v = [None] * HEADS_PER_GROUP, [None] * HEADS_PER_GROUP
        for b in range(nbk):
            rs, win = slice(b * BLK, (b + 1) * BLK), slice(b * BLK, (b + 2) * BLK)
            for h in range(HEADS_PER_GROUP):
                hs = _head_lanes(h)
                qb, dab = q_ref[rs, hs], da_ref[rs, hs].astype(BF)
                kw, vw = k_scr[win, hs], v_scr[win, hs]
                p = jnp.where(band_first if b == 0 else band,
                              jnp.exp(_dot_nt(qb, kw) * ATTN_SCALE - ls_ref[rs, hs][:, :1]), 0.0)
                ds = (p * (_dot_nt(dab, vw) - dl_ref[rs, hs][:, :1]) * ATTN_SCALE).astype(BF)
                dq_ref[rs, hs] = _dot(ds, kw).astype(BF)
                dkw, dvw = _dot_tn(ds, qb), _dot_tn(p.astype(BF), dab)
                if b >= 1:
                    ps = slice((b - 1) * BLK, b * BLK)
                    dk_ref[ps, hs] = (pend_k[h] + dkw[:BLK]).astype(BF)
                    dv_ref[ps, hs] = (pend_v[h] + dvw[:BLK]).astype(BF)
                pend_k[h], pend_v[h] = dkw[BLK:], dvw[BLK:]
        ls_rows = slice((nbk - 1) * BLK, nbk * BLK)
        last = slice(nbk * BLK, (nbk + 1) * BLK)
        for h in range(HEADS_PER_GROUP):
            hs = _head_lanes(h)
            qb, dab = qn_ref[:, hs], dan_ref[:, hs].astype(BF)
            kp, vp = k_scr[last, hs], v_scr[last, hs]
            p = jnp.where(col1 >= row1 + next_shift, jnp.exp(_dot_nt(qb, kp) * ATTN_SCALE - lsn_ref[:, hs]), 0.0)
            ds = (p * (_dot_nt(dab, vp) - dln_ref[:, hs]) * ATTN_SCALE).astype(BF)
            dk_ref[ls_rows, hs] = (pend_k[h] + _dot_tn(ds, qb)).astype(BF)
            dv_ref[ls_rows, hs] = (pend_v[h] + _dot_tn(p.astype(BF), dab)).astype(BF)

    def cur(cb):
        return pl.BlockSpec((rows, GROUP_W), lambda i: (i, cb))

    def prev(cb):
        return pl.BlockSpec((BLK, GROUP_W), lambda i: (jnp.maximum(i * nbk - 1, 0), cb))

    def nxt(cb):
        return pl.BlockSpec((BLK, GROUP_W), lambda i: (jnp.minimum((i + 1) * nbk, nblk - 1), cb))

    (qa, qc), (ka, kc_), (va, vc_) = q, k, v
    return _pcall(name, kern, (t // rows,),
                  [cur(qc), nxt(qc), cur(kc_), prev(kc_), cur(vc_), prev(vc_), cur(0), nxt(0), cur(0), nxt(0), cur(0), nxt(0)],
                  [pl.BlockSpec((rows, GROUP_W), lambda i: (i, 0))] * 3, [jax.ShapeDtypeStruct((t, GROUP_W), BF)] * 3,
                  [pltpu.VMEM((rows + BLK, GROUP_W), BF)] * 2, ("parallel",),
                  (qa, qa, ka, ka, va, va, da, da, dl, dl, lse, lse), side)


CLS_TILE = 512


def _cls_block(t, tile, dil, dtype):
    if dil == 1:
        return pl.BlockSpec((tile, GROUP_W), lambda i: (i, 0)), jax.ShapeDtypeStruct((t, GROUP_W), dtype)
    return (pl.BlockSpec((dil, tile // dil, GROUP_W), lambda i: (0, i, 0)),
            jax.ShapeDtypeStruct((dil, t // dil, GROUP_W), dtype))


def _head_scratch(tile):
    return pltpu.VMEM((tile, HEAD_DIM), F32)


def _rope_split(z, c_tab, s_tab, t):
    tile = min(CLS_TILE, t)
    n_heads = ATTN_WIDTH // HEAD_DIM

    def kern(zq_ref, zk_ref, zv_ref, c_ref, s_ref, *rest):
        outs, scr = rest[:9], rest[9]
        c, s = c_ref[...], s_ref[...]
        for which, z_ref in enumerate((zq_ref, zk_ref, zv_ref)):
            for h in range(n_heads):
                g, hs = h // HEADS_PER_GROUP, _head_lanes(h % HEADS_PER_GROUP)
                val = z_ref[:, h * HEAD_DIM:(h + 1) * HEAD_DIM].astype(F32)
                if which < 2:
                    val = _rot(val, c, s)
                if g == 0:
                    outs[which][:, hs] = val.astype(BF)
                    continue
                scr[...] = val
                dil = DILATIONS[g]
                for r in range(dil):
                    outs[3 * g + which][r, :, hs] = scr[pl.ds(r, tile // dil, stride=dil), :].astype(BF)

    blocks = [_cls_block(t, tile, DILATIONS[g], BF) for g in range(3) for _ in range(3)]
    zspec = lambda cb: pl.BlockSpec((tile, ATTN_WIDTH), lambda i: (i, cb))
    tab = pl.BlockSpec((tile, HEAD_DIM), lambda i: (i, 0))
    return _pcall("rope", kern, (t // tile,), [zspec(0), zspec(1), zspec(2), tab, tab], [b[0] for b in blocks],
                  [b[1] for b in blocks], [_head_scratch(tile)], ("parallel",), (z, z, z, c_tab, s_tab))


def _merge_classes(outs, lses, t):
    tile = min(CLS_TILE, t)

    def kern(o0, l0, o1, l1, o2, l2, attn_ref, lse_ref, s_o1, s_l1, s_o2, s_l2):
        for h in range(HEADS_PER_GROUP):
            hs = _head_lanes(h)
            for src, dst, dil in ((o1, s_o1, DILATIONS[1]), (l1, s_l1, DILATIONS[1]), (o2, s_o2, DILATIONS[2]),
                                  (l2, s_l2, DILATIONS[2])):
                for r in range(dil):
                    dst[pl.ds(r, tile // dil, stride=dil), :] = src[r, :, hs]
            attn_ref[:, hs], lse_ref[:, hs] = _merge_fn(o0[:, hs], s_o1[...], s_o2[...], l0[:, hs], s_l1[...], s_l2[...])

    blocks = [_cls_block(t, tile, DILATIONS[g], F32) for g in range(3)]
    args = []
    for g in range(3):
        args += [outs[g].reshape(blocks[g][1].shape), lses[g].reshape(blocks[g][1].shape)]
    tok = pl.BlockSpec((tile, GROUP_W), lambda i: (i, 0))
    return _pcall("attn_merge", kern, (t // tile,), [blocks[g][0] for g in range(3) for _ in range(2)], [tok, tok],
                  [jax.ShapeDtypeStruct((t, GROUP_W), F32)] * 2, [_head_scratch(tile)] * 4, ("parallel",), args)


def _attn_bwd_prep(dattn, attn, lse, t):
    tile = min(CLS_TILE, t)

    def kern(da_ref, at_ref, ls_ref, dl0, da1, dl1, ls1, da2, dl2, ls2, s_da, s_dl, s_ls):
        for h in range(HEADS_PER_GROUP):
            hs = _head_lanes(h)
            da = da_ref[:, hs]
            s_da[...] = da
            s_dl[...] = jnp.broadcast_to(jnp.sum(da * at_ref[:, hs], axis=1, keepdims=True), (tile, HEAD_DIM))
            s_ls[...] = ls_ref[:, hs]
            dl0[:, hs] = s_dl[...]
            for oda, odl, ols, dil in ((da1, dl1, ls1, DILATIONS[1]), (da2, dl2, ls2, DILATIONS[2])):
                for r in range(dil):
                    rows = pl.ds(r, tile // dil, stride=dil)
                    oda[r, :, hs] = s_da[rows, :].astype(BF)
                    odl[r, :, hs] = s_dl[rows, :]
                    ols[r, :, hs] = s_ls[rows, :]

    tok = pl.BlockSpec((tile, GROUP_W), lambda i: (i, 0))
    blocks = [(tok, jax.ShapeDtypeStruct((t, GROUP_W), F32))]
    for g in (1, 2):
        blocks += [_cls_block(t, tile, DILATIONS[g], BF), _cls_block(t, tile, DILATIONS[g], F32),
                   _cls_block(t, tile, DILATIONS[g], F32)]
    res = _pcall("attn_bwd_prep", kern, (t // tile,), [tok, tok, tok], [b[0] for b in blocks], [b[1] for b in blocks],
                 [_head_scratch(tile)] * 3, ("parallel",), (dattn, attn, lse))
    flat = [a.reshape(t, GROUP_W) for a in res]
    return flat[0], flat[1:4], flat[4:7]


def _rope_bwd_join(dqs, dks, dvs, c_tab, s_tab, tail, t, side=None):
    tile = min(256, t)
    n_heads = ATTN_WIDTH // HEAD_DIM

    def kern(q0, q1, q2, k0, k1, k2, v0, v1, v2, c_ref, s_ref, ga_ref, gb_ref, gta_ref, gtb_ref, dz_ref, scr):
        c, s = c_ref[...], -s_ref[...]
        for which, srcs in enumerate(((q0, q1, q2), (k0, k1, k2), (v0, v1, v2))):
            for h in range(n_heads):
                g, hs = h // HEADS_PER_GROUP, _head_lanes(h % HEADS_PER_GROUP)
                if g == 0:
                    val = srcs[0][:, hs].astype(F32)
                else:
                    dil = DILATIONS[g]
                    for r in range(dil):
                        scr[pl.ds(r, tile // dil, stride=dil), :] = srcs[g][r, :, hs].astype(F32)
                    val = scr[...]
                if which < 2:
                    val = _rot(val, c, s)
                col = which * ATTN_WIDTH + h * HEAD_DIM
                dz_ref[:, col:col + HEAD_DIM] = val.astype(BF)
        dz_ref[:, GLU_A_COL:GLU_B_COL] = ga_ref[...]
        dz_ref[:, GLU_B_COL:GATE_A_COL] = gb_ref[...]
        dz_ref[:, GATE_A_COL:GATE_B_COL] = gta_ref[...]
        dz_ref[:, GATE_B_COL:] = gtb_ref[...]

    blocks = [_cls_block(t, tile, DILATIONS[g], BF) for g in range(3)]
    args = [a.reshape(blocks[g][1].shape) for grp in (dqs, dks, dvs) for g, a in enumerate(grp)]
    tab = pl.BlockSpec((tile, HEAD_DIM), lambda i: (i, 0))
    row = lambda w: pl.BlockSpec((tile, w), lambda i: (i, 0))
    return _pcall("rope_bwd", kern, (t // tile,), [blocks[g][0] for _ in range(3) for g in range(3)]
                  + [tab, tab, row(CONV_CH), row(CONV_CH), row(D_MODEL), row(D_MODEL)], [row(IN_WIDTH)],
                  [jax.ShapeDtypeStruct((t, IN_WIDTH), BF)], [_head_scratch(tile)], ("parallel",),
                  (*args, c_tab, s_tab, *tail), side)


def _cross_probs(qh, kh):
    s = _dot_nt(qh, kh) * CROSS_SCALE
    e = jnp.exp(s - jnp.max(s, axis=1, keepdims=True))
    return e, jnp.sum(e, axis=1, keepdims=True)


def _cross_fwd(cq, ckv, t):
    rows = min(512, t)

    def kern(q_ref, kv_ref, o_ref):
        for h in range(CROSS_HEADS):
            hs = slice(h * CROSS_HEAD_DIM, (h + 1) * CROSS_HEAD_DIM)
            vs = slice(D_MODEL + h * CROSS_HEAD_DIM, D_MODEL + (h + 1) * CROSS_HEAD_DIM)
            e, tot = _cross_probs(q_ref[:, hs], kv_ref[:, hs])
            o_ref[:, hs] = (_dot(e.astype(BF), kv_ref[:, vs]) / tot).astype(BF)

    return pl.pallas_call(
        kern, name="cross_fwd", grid=(t // rows,),
        in_specs=[pl.BlockSpec((rows, D_MODEL), lambda i: (i, 0)), pl.BlockSpec((N_MEM, 2 * D_MODEL), lambda i: (0, 0))],
        out_specs=pl.BlockSpec((rows, D_MODEL), lambda i: (i, 0)),
        out_shape=jax.ShapeDtypeStruct((t, D_MODEL), BF),
        compiler_params=_params(("parallel",)),
    )(cq, ckv)


def _cross_bwd(cq, ckv, dco, t):
    rows = min(512, t)

    def kern(q_ref, kv_ref, do_ref, dq_ref, dkv_ref):
        @pl.when(pl.program_id(0) == 0)
        def _():
            dkv_ref[...] = jnp.zeros_like(dkv_ref)
        for h in range(CROSS_HEADS):
            hs = slice(h * CROSS_HEAD_DIM, (h + 1) * CROSS_HEAD_DIM)
            vs = slice(D_MODEL + h * CROSS_HEAD_DIM, D_MODEL + (h + 1) * CROSS_HEAD_DIM)
            qh, kh, vh, doh = q_ref[:, hs], kv_ref[:, hs], kv_ref[:, vs], do_ref[:, hs]
            e, tot = _cross_probs(qh, kh)
            p = e / tot
            dp = _dot_nt(doh, vh)
            ds = (p * (dp - jnp.sum(p * dp, axis=1, keepdims=True)) * CROSS_SCALE).astype(BF)
            dq_ref[:, hs] = _dot(ds, kh).astype(BF)
            dkv_ref[:, hs] += _dot_tn(ds, qh)
            dkv_ref[:, vs] += _dot_tn(p.astype(BF), doh)

    return pl.pallas_call(
        kern, name="cross_bwd", grid=(t // rows,),
        in_specs=[pl.BlockSpec((rows, D_MODEL), lambda i: (i, 0)), pl.BlockSpec((N_MEM, 2 * D_MODEL), lambda i: (0, 0)),
                  pl.BlockSpec((rows, D_MODEL), lambda i: (i, 0))],
        out_specs=[pl.BlockSpec((rows, D_MODEL), lambda i: (i, 0)), pl.BlockSpec((N_MEM, 2 * D_MODEL), lambda i: (0, 0))],
        out_shape=[jax.ShapeDtypeStruct((t, D_MODEL), BF), jax.ShapeDtypeStruct((N_MEM, 2 * D_MODEL), F32)],
        compiler_params=_params(("arbitrary",)),
    )(cq, ckv, dco)


CONV_TILE = 512
CONV_CHUNK = 128
HALO = 32


def _conv_fwd(z, w32, bias, t, side=None):
    tile = min(CONV_TILE, t)
    a_cb, b_cb = GLU_A_COL // LANES, GLU_B_COL // LANES
    hb = tile // HALO

    def kern(a_ref, b_ref, ah_ref, bh_ref, w_ref, bias_ref, o_ref, g_scr):
        i = pl.program_id(1)
        g_scr[HALO:, :] = _glu(a_ref[...], b_ref[...])
        g_scr[:HALO, :] = _glu(ah_ref[...], bh_ref[...]) * jnp.where(i > 0, 1.0, 0.0)
        for c in range(tile // CONV_CHUNK):
            acc = jnp.broadcast_to(bias_ref[...], (CONV_CHUNK, LANES))
            for j in range(CONV_K):
                lo = c * CONV_CHUNK + HALO - (CONV_K - 1) + j
                acc = acc + w_ref[j:j + 1, :] * g_scr[lo:lo + CONV_CHUNK, :]
            o_ref[c * CONV_CHUNK:(c + 1) * CONV_CHUNK, :] = acc

    def cur(cb):
        return pl.BlockSpec((tile, LANES), lambda j, i: (i, cb + j))

    def prev(cb):
        return pl.BlockSpec((HALO, LANES), lambda j, i: (jnp.maximum(i * hb - 1, 0), cb + j))

    return _pcall("conv_fwd", kern, (CONV_CH // LANES, t // tile),
                  [cur(a_cb), cur(b_cb), prev(a_cb), prev(b_cb),
                   pl.BlockSpec((CONV_KP, LANES), lambda j, i: (0, j)), pl.BlockSpec((1, LANES), lambda j, i: (0, j))],
                  [pl.BlockSpec((tile, LANES), lambda j, i: (i, j))], [jax.ShapeDtypeStruct((t, CONV_CH), F32)],
                  [pltpu.VMEM((tile + HALO, LANES), F32)], ("parallel", "parallel"), (z, z, z, z, w32, bias), side)


def _conv_bwd(z, dc1, w32, t, side=None):
    tile = min(CONV_TILE, t)
    a_cb, b_cb = GLU_A_COL // LANES, GLU_B_COL // LANES
    hb = tile // HALO
    n_tiles = t // tile
    n_chunks = tile // CONV_CHUNK

    def kern(a_ref, b_ref, ah_ref, bh_ref, d_ref, dn_ref, w_ref, da_ref, db_ref, dw_ref, g_scr, d_scr):
        i = pl.program_id(1)
        g_scr[HALO:, :] = _glu(a_ref[...], b_ref[...])
        g_scr[:HALO, :] = _glu(ah_ref[...], bh_ref[...]) * jnp.where(i > 0, 1.0, 0.0)
        d_scr[:tile, :] = d_ref[...]
        d_scr[tile:, :] = dn_ref[...] * jnp.where(i < n_tiles - 1, 1.0, 0.0)

        @pl.when(i == 0)
        def _():
            dw_ref[...] = jnp.zeros_like(dw_ref)

        for c in range(n_chunks):
            cs = slice(c * CONV_CHUNK, (c + 1) * CONV_CHUNK)
            acc = jnp.zeros((CONV_CHUNK, LANES), F32)
            for j in range(CONV_K):
                lo = c * CONV_CHUNK + (CONV_K - 1) - j
                acc = acc + w_ref[j:j + 1, :] * d_scr[lo:lo + CONV_CHUNK, :]
            sgc = _sig(b_ref[cs, :].astype(F32))
            da_ref[cs, :] = (acc * sgc).astype(BF)
            db_ref[cs, :] = (acc * a_ref[cs, :].astype(F32) * sgc * (1.0 - sgc)).astype(BF)
        for j in range(CONV_K):
            tot = jnp.zeros((SUBLANES, LANES), F32)
            for c in range(n_chunks):
                lo = c * CONV_CHUNK + HALO - (CONV_K - 1) + j
                prod = d_ref[c * CONV_CHUNK:(c + 1) * CONV_CHUNK, :] * g_scr[lo:lo + CONV_CHUNK, :]
                tot = tot + jnp.sum(prod.reshape(CONV_CHUNK // SUBLANES, SUBLANES, LANES), axis=0)
            dw_ref[j:j + 1, :] += jnp.sum(tot, axis=0, keepdims=True)
        dw_ref[CONV_K:CONV_KP, :] += jnp.sum(d_ref[...], axis=0, keepdims=True)

    def cur(cb):
        return pl.BlockSpec((tile, LANES), lambda j, i: (i, cb + j))

    def prev(cb):
        return pl.BlockSpec((HALO, LANES), lambda j, i: (jnp.maximum(i * hb - 1, 0), cb + j))

    return _pcall(
        "conv_bwd", kern, (CONV_CH // LANES, n_tiles),
        [cur(a_cb), cur(b_cb), prev(a_cb), prev(b_cb), cur(0),
         pl.BlockSpec((HALO, LANES), lambda j, i: (jnp.minimum((i + 1) * hb, t // HALO - 1), j)),
         pl.BlockSpec((CONV_KP, LANES), lambda j, i: (0, j))],
        [pl.BlockSpec((tile, LANES), lambda j, i: (i, j)), pl.BlockSpec((tile, LANES), lambda j, i: (i, j)),
         pl.BlockSpec((CONV_KP, LANES), lambda j, i: (0, j))],
        [jax.ShapeDtypeStruct((t, CONV_CH), BF), jax.ShapeDtypeStruct((t, CONV_CH), BF),
         jax.ShapeDtypeStruct((CONV_KP, CONV_CH), F32)],
        [pltpu.VMEM((tile + HALO, LANES), F32), pltpu.VMEM((tile + HALO, LANES), F32)],
        ("parallel", "arbitrary"), (z, z, z, z, dc1, dc1, w32), side)


def _adam_fn(w, g, m, v):
    m = ADAM_B1 * m + (1.0 - ADAM_B1) * g
    v = ADAM_B2 * v + (1.0 - ADAM_B2) * (g * g)
    m_hat = m / (1.0 - ADAM_B1 ** ADAM_STEP)
    v_hat = v / (1.0 - ADAM_B2 ** ADAM_STEP)
    delta = -ADAM_LR * (m_hat / (jnp.sqrt(v_hat) + ADAM_EPS) + ADAM_WD * w)
    return delta, m, v


def _adam(name, w, g, m, v):
    rows, cols = w.shape
    tile = _ew_tile(rows, cols)
    return _rowcall(name, _adam_fn, rows, tile, [(a, cols, 0) for a in (w, g, m, v)], [], [(cols, F32)] * 3)


def _place():
    x, y, c = lax.axis_index("x"), lax.axis_index("y"), lax.axis_index("c")
    chips = [(1 - x, y), (x, 1 - y), (1 - x, 1 - y)]
    return x, y, c, chips


def _gather_side(shards):
    nw = len(shards)
    chip = 2 * lax.axis_index("x") + lax.axis_index("y")
    staged = [lax.dynamic_update_index_in_dim(jnp.zeros((4,) + s.shape, s.dtype), s, chip, 0) for s in shards]

    def build(_, outs, send_sems, recv_sems):
        x, y, c, chips = _place()
        me = 2 * x + y
        sibling = (x, y, 1 - c)

        def half(w, lead, h):
            n = shards[w].shape[0] // 2
            return outs[w].at[lead, pl.ds(h * n, n)]

        def copy(w, k, part, to):
            return pltpu.make_async_remote_copy(src_ref=part, dst_ref=part, send_sem=send_sems.at[6 * w + k],
                                                recv_sem=recv_sems.at[6 * w + k], device_id=to, device_id_type=MESH)

        def start():
            for w in range(nw):
                for k, (px, py) in enumerate(chips):
                    copy(w, k, half(w, me, c), (px, py, c)).start()

        def finish():
            for w in range(nw):
                for k, (px, py) in enumerate(chips):
                    landed = half(w, 2 * px + py, c)
                    copy(w, k, landed, (px, py, c)).wait_recv()
                    copy(w, 3 + k, landed, sibling).start()
            for w in range(nw):
                for k, (px, py) in enumerate(chips):
                    copy(w, 3 + k, half(w, 2 * px + py, 1 - c), sibling).wait_recv()
            for w in range(nw):
                for k, (px, py) in enumerate(chips):
                    copy(w, k, half(w, me, c), (px, py, c)).wait_send()
                    copy(w, 3 + k, half(w, 2 * px + py, c), sibling).wait_send()

        return start, finish

    return _Side(staged, [jax.ShapeDtypeStruct((4,) + s.shape, s.dtype) for s in shards], 6 * nw, build,
                 aliases={w: w for w in range(nw)})


def _gather8(name, v, side=None):
    rows = v.shape[0]

    def body(v_ref, all_ref, sum_ref, send_sems, recv_sems):
        x, y, c, _ = _place()
        me = 4 * x + 2 * y + c
        all_ref[me] = v_ref[...]
        copies = []
        for k in range(1, 8):
            px, py, pc = x ^ (k >> 2), y ^ ((k >> 1) & 1), c ^ (k & 1)
            copies.append(pltpu.make_async_remote_copy(
                src_ref=v_ref, dst_ref=all_ref.at[me], send_sem=send_sems.at[k - 1], recv_sem=recv_sems.at[k - 1],
                device_id=(px, py, pc), device_id_type=MESH))
            copies[-1].start()
        for k in range(1, 8):
            px, py, pc = x ^ (k >> 2), y ^ ((k >> 1) & 1), c ^ (k & 1)
            theirs = all_ref.at[4 * px + 2 * py + pc]
            pltpu.make_async_remote_copy(
                src_ref=theirs, dst_ref=theirs, send_sem=send_sems.at[k - 1], recv_sem=recv_sems.at[k - 1],
                device_id=(px, py, pc), device_id_type=MESH).wait_recv()
        for cp in copies:
            cp.wait_send()
        tot = all_ref[0]
        for d in range(1, 8):
            tot = tot + all_ref[d]
        sum_ref[...] = tot

    vm = pl.BlockSpec(memory_space=pltpu.VMEM)
    return _pcall(name, body, (), [vm], [vm, vm],
                  [jax.ShapeDtypeStruct((8, rows, LANES), F32), jax.ShapeDtypeStruct((rows, LANES), F32)],
                  [pltpu.SemaphoreType.DMA((7,)), pltpu.SemaphoreType.DMA((7,))], None, (v,), side)


def _region(ref, col_sharded, shape, j, h):
    r, ccols = shape
    if col_sharded:
        return ref.at[pl.ds(h * (r // 2), r // 2), pl.ds(j * (ccols // 4), ccols // 4)]
    n = r // 8
    return ref.at[pl.ds((2 * j + h) * n, n), :]


def _region_shape(col_sharded, shape):
    r, ccols = shape
    return (r // 2, ccols // 4) if col_sharded else (r // 8, ccols)


def _exchange(copies):
    def build(ins, outs, send_sems, recv_sems):
        def start():
            for cp in copies(ins, outs, send_sems, recv_sems):
                cp.start()

        def finish():
            for cp in copies(ins, outs, send_sems, recv_sems):
                cp.wait()

        return start, finish
    return build


def _swap_side(grads, kinds):
    nw = len(grads)

    def copies(ins, theirs, send_sems, recv_sems):
        x, y, c, _ = _place()
        return [pltpu.make_async_remote_copy(
            src_ref=_region(ins[w], kinds[w], grads[w].shape, j, 1 - c), dst_ref=theirs[w].at[j],
            send_sem=send_sems.at[4 * w + j], recv_sem=recv_sems.at[4 * w + j], device_id=(x, y, 1 - c), device_id_type=MESH)
            for w in range(nw) for j in range(4)]

    shapes = [jax.ShapeDtypeStruct((4,) + _region_shape(kinds[w], grads[w].shape), F32) for w in range(nw)]
    return _Side(grads, shapes, 4 * nw, _exchange(copies))


def _scatter_side(parts):
    nw = len(parts)

    def copies(ins, outs, send_sems, recv_sems):
        x, y, c, chips = _place()
        return [pltpu.make_async_remote_copy(
            src_ref=ins[w].at[2 * px + py], dst_ref=outs[w].at[k], send_sem=send_sems.at[3 * w + k],
            recv_sem=recv_sems.at[3 * w + k], device_id=(px, py, c), device_id_type=MESH)
            for w in range(nw) for k, (px, py) in enumerate(chips)]

    shapes = [jax.ShapeDtypeStruct((3,) + p.shape[1:], p.dtype) for p in parts]
    return _Side(parts, shapes, 3 * nw, _exchange(copies))


def _share_side(halves):
    nw = len(halves)

    def copies(ins, outs, send_sems, recv_sems):
        x, y, c, _ = _place()
        return [pltpu.make_async_remote_copy(
            src_ref=ins[w], dst_ref=outs[w].at[c], send_sem=send_sems.at[w], recv_sem=recv_sems.at[w],
            device_id=(x, y, 1 - c), device_id_type=MESH) for w in range(nw)]

    return _Side(halves, [jax.ShapeDtypeStruct((2,) + h.shape, F32) for h in halves], nw, _exchange(copies))


EW_BLOCK = 512 * 1024


def _ew_tile(rows, cols):
    limit = max(8, EW_BLOCK // cols)
    return max(d for d in range(8, min(rows, limit) + 1, 8) if rows % d == 0)


def _indexed_sum(name, fn, grid, in_specs, out_specs, out_shape, index, arrays):
    def kern(_, *refs):
        n_in = len(in_specs)
        vals = fn(*[r[...] for r in refs[:n_in]])
        for r, v in zip(refs[n_in:], vals):
            r[...] = v.astype(r.dtype)

    return pl.pallas_call(
        kern, name=name, out_shape=out_shape,
        grid_spec=pltpu.PrefetchScalarGridSpec(num_scalar_prefetch=1, grid=grid, in_specs=in_specs, out_specs=out_specs),
        compiler_params=_params(("parallel",) * len(grid)),
    )(index.astype(jnp.int32).reshape(1), *arrays)


def _pair_sums(names, grads, theirs):
    parts, parts_bf = {}, {}
    for n, other in zip(names, theirs):
        _, rr, cc = other.shape
        tile = _ew_tile(rr, cc)
        nb = rr // tile
        if COL_SHARDED[n]:
            mine = pl.BlockSpec((tile, cc), lambda j, i, c: (c[0] * nb + i, j))
        else:
            mine = pl.BlockSpec((tile, cc), lambda j, i, c: ((2 * j + c[0]) * nb + i, 0))
        flat = pl.BlockSpec((tile, cc), lambda j, i, c: (j * nb + i, 0))
        p, pb = _indexed_sum(f"grad_pair_sum_{n}", lambda u, v: (u + v, u + v), (4, nb), [mine, flat], [flat, flat],
                             [jax.ShapeDtypeStruct((4 * rr, cc), F32), jax.ShapeDtypeStruct((4 * rr, cc), BF)],
                             lax.axis_index("c"), (grads[n], other.reshape(4 * rr, cc)))
        parts[n], parts_bf[n] = p.reshape(4, rr, cc), pb.reshape(4, rr, cc)
    return parts, parts_bf


def _chip_sums(names, parts, landed):
    halves = {}
    for n, got in zip(names, landed):
        _, rr, cc = got.shape
        tile = _ew_tile(rr, cc)
        nb = rr // tile
        own = pl.BlockSpec((tile, cc), lambda i, chip: (chip[0] * nb + i, 0))
        peer = lambda k: pl.BlockSpec((tile, cc), lambda i, chip: (k * nb + i, 0))
        halves[n] = _indexed_sum(f"grad_chip_sum_{n}", lambda o, k0, k1, k2: (((o + k0) + k1) + k2,), (nb,),
                                 [own, peer(0), peer(1), peer(2)], [pl.BlockSpec((tile, cc), lambda i, chip: (i, 0))],
                                 [jax.ShapeDtypeStruct((rr, cc), F32)], 2 * lax.axis_index("x") + lax.axis_index("y"),
                                 (parts[n].reshape(4 * rr, cc),) + (got.reshape(3 * rr, cc),) * 3)[0]
    return halves


def _both_halves(mine, shared):
    both = lax.dynamic_update_index_in_dim(shared, mine, lax.axis_index("c"), 0)
    return both.reshape(2 * mine.shape[0], mine.shape[1])


BIG = ("w_in", "w_attn_proj", "w_conv_proj", "w_out", "w_cq", "w_ckv", "w_co", "w_up", "w_down")
COL_SHARDED = {"w_in": True, "w_attn_proj": True, "w_conv_proj": True, "w_out": False, "w_cq": False,
               "w_ckv": True, "w_co": False, "w_up": True, "w_down": False}
SMALL = ("g_mix", "b_gate", "conv_b", "conv_ln_g", "conv_ln_b", "g_cross", "g_mem", "g_mlp", "g_final")
ORDER = ("g_mix", "w_in", "b_gate", "conv_w", "conv_b", "conv_ln_g", "conv_ln_b", "w_attn_proj", "w_conv_proj", "w_out",
         "g_cross", "g_mem", "w_cq", "w_ckv", "w_co", "g_mlp", "w_up", "w_down", "g_final")


def _pad_rows(flat, rows):
    return jnp.pad(flat, (0, rows * LANES - flat.shape[0])).reshape(rows, LANES)


WAVE_MLP = ("w_down", "w_up")
WAVE_MID = ("w_co", "w_cq", "w_ckv", "w_out", "w_attn_proj", "w_conv_proj")


def _local_step(x, mem, tgt, shards, small, conv_w_full):
    t = x.shape[0]
    tr = 512
    c_tab, s_tab = _rope_tables(t)
    row = lambda v: v.reshape(1, -1)
    g_mix, g_cross, g_mem, g_mlp, g_final = (row(small[n]) for n in ("g_mix", "g_cross", "g_mem", "g_mlp", "g_final"))
    b_gate, conv_b, ln_g, ln_b = (row(small[n]) for n in ("b_gate", "conv_b", "conv_ln_g", "conv_ln_b"))
    w32 = jnp.pad(conv_w_full, ((0, CONV_KP - CONV_K), (0, 0)))

    (u,), (w_in_all,) = _rowcall("mix_norm", lambda a, g: (_rms(a, g),), t, tr, [(x, D_MODEL, 0)], [g_mix], [(D_MODEL, BF)],
                                 side=_gather_side([shards["w_in"]]))
    wfull = {"w_in": w_in_all}

    def keep(names, gathered):
        for n, g in zip(names, gathered):
            wfull[n] = g if COL_SHARDED[n] else g.reshape(1, 4 * g.shape[1], g.shape[2])

    z, gathered = _mm("in_proj", u, w_in_all, out_dtypes=(BF,), side=_gather_side([shards[n] for n in WAVE_MID]))
    keep(WAVE_MID, gathered)
    (c1,), gathered = _conv_fwd(z, w32, conv_b, t, side=_gather_side([shards[n] for n in WAVE_MLP]))
    keep(WAVE_MLP, gathered)
    qkv = _rope_split(z, c_tab, s_tab, t)
    qkv_cls, outs, lses = [], [], []
    for g, dil in enumerate(DILATIONS):
        ops = tuple((a.reshape(t, GROUP_W), 0) for a in qkv[3 * g:3 * g + 3])
        qkv_cls.append(ops)
        o_g, l_g = _attn_fwd(f"attn_fwd_{g}", *ops, t, dil)
        outs.append(o_g)
        lses.append(l_g)
    attn, lse = _merge_classes(outs, lses, t)
    y_attn = _mm("attn_proj", attn, wfull["w_attn_proj"])
    (c2,) = _rowcall("conv_ln_silu", _ln_silu_fn, t, tr, [(c1, CONV_CH, 0)], [ln_g, ln_b], [(CONV_CH, BF)])
    tn_cp = wfull["w_conv_proj"].shape[2]
    merged, y_conv = _mm("conv_proj", c2, wfull["w_conv_proj"], epi=_gate_epi, out_dtypes=(BF, F32),
                         extras=(y_attn, (z, GATE_A_COL // tn_cp), (z, GATE_B_COL // tn_cp)),
                         params=((b_gate, 0), (b_gate, D_MODEL // tn_cp)))
    x1, uq = _mm("out_proj", merged, wfull["w_out"], extras=(x,), params=(g_cross,), epi=_res_norm_epi, out_dtypes=(F32, BF))
    (mn,) = _rowcall("mem_norm", lambda a, g: (_rms(a, g),), N_MEM, N_MEM, [(mem, D_MODEL, 0)], [g_mem], [(D_MODEL, BF)])
    cq = _mm("cross_q", uq, wfull["w_cq"], out_dtypes=(BF,))
    ckv = _mm("cross_kv", mn, wfull["w_ckv"], out_dtypes=(BF,))
    co = _cross_fwd(cq, ckv, t)
    x2, um = _mm("cross_out", co, wfull["w_co"], extras=(x1,), params=(g_mlp,), epi=_res_norm_epi, out_dtypes=(F32, BF))
    hact = _mm("mlp_up", um, wfull["w_up"], out_dtypes=(BF,), epi=lambda acc: (jnp.square(jnp.maximum(acc, 0.0)),))
    d3, d3b, loss_row, dg_final = _mm("mlp_down", hact, wfull["w_down"], extras=(x2, tgt), params=(g_final,), epi=_final_epi,
                                      out_dtypes=(F32, BF), sums=[(1, LANES), (1, D_MODEL)])

    gw = {}
    dhp = _mm("mlp_down_bwd", d3b, wfull["w_down"], nt=True, extras=(hact,), out_dtypes=(BF,),
              epi=lambda acc, h: (acc * 2.0 * jnp.sqrt(h.astype(F32)),))
    gw["w_down"] = _mm_tn("mlp_down_wgrad", hact, d3b)
    gw["w_up"] = _mm_tn("mlp_up_wgrad", um, dhp)

    def swap_of(names):
        return _swap_side([gw[n] for n in names], [COL_SHARDED[n] for n in names])

    (d2, d2b, dg_mlp), theirs = _mm("mlp_up_bwd", dhp, wfull["w_up"], nt=True, extras=(x2, d3), params=(g_mlp,),
                                    epi=_norm_bwd_epi, out_dtypes=(F32, BF), sums=[(1, D_MODEL)], side=swap_of(WAVE_MLP))
    parts_mlp, parts_bf_mlp = _pair_sums(WAVE_MLP, gw, theirs)

    dco = _mm("cross_out_bwd", d2b, wfull["w_co"], nt=True, out_dtypes=(BF,))
    gw["w_co"] = _mm_tn("cross_out_wgrad", co, d2b)
    dcq, dckv = _cross_bwd(cq, ckv, dco, t)
    gw["w_cq"] = _mm_tn("cross_q_wgrad", uq, dcq)
    d1, d1b, dg_cross = _mm("cross_q_bwd", dcq, wfull["w_cq"], nt=True, extras=(x1, d2), params=(g_cross,), epi=_norm_bwd_epi,
                            out_dtypes=(F32, BF), sums=[(1, D_MODEL)])
    gw["w_ckv"] = _mm_tn("cross_kv_wgrad", mn, dckv, tk=N_MEM)
    dmn = _mm("cross_kv_bwd", dckv, wfull["w_ckv"], nt=True)
    (dg_mem,) = _rowcall("mem_norm_bwd", lambda a, dn, g: (_rms_bwd(a, g, dn)[1],), N_MEM, N_MEM,
                         [(mem, D_MODEL, 0), (dmn, D_MODEL, 0)], [g_mem], [], accs=[(1, D_MODEL)])

    dya, dyc, dgate_a, dgate_b, dbg_a, dbg_b = _mm(
        "out_proj_bwd", d1b, wfull["w_out"], nt=True, tm=min(512, t), epi=_gate_bwd_epi, out_dtypes=(BF,) * 4,
        extras=(y_attn, y_conv, (z, GATE_A_COL // D_MODEL), (z, GATE_B_COL // D_MODEL)), params=(b_gate,),
        sums=[(1, D_MODEL), (1, D_MODEL)])
    dbg = jnp.concatenate([dbg_a, dbg_b], axis=1)
    gw["w_out"] = _mm_tn("out_proj_wgrad", merged, d1b)
    gw["w_attn_proj"] = _mm_tn("attn_proj_wgrad", attn, dya)
    dattn = _mm("attn_proj_bwd", dya, wfull["w_attn_proj"], nt=True)
    gw["w_conv_proj"] = _mm_tn("conv_proj_wgrad", c2, dyc)
    dc2 = _mm("conv_proj_bwd", dyc, wfull["w_conv_proj"], nt=True)
    (dc1, dlng, dlnb), theirs = _rowcall("conv_ln_silu_bwd", _ln_silu_bwd_fn, t, tr, [(c1, CONV_CH, 0), (dc2, CONV_CH, 0)],
                                         [ln_g, ln_b], [(CONV_CH, F32)], accs=[(1, CONV_CH), (1, CONV_CH)],
                                         side=swap_of(WAVE_MID))
    parts_mid, parts_bf_mid = _pair_sums(WAVE_MID, gw, theirs)
    (dglu_a, dglu_b, dconv), landed = _conv_bwd(z, dc1, w32, t, side=_scatter_side([parts_bf_mlp[n] for n in WAVE_MLP]))
    halves_mlp = _chip_sums(WAVE_MLP, parts_mlp, landed)

    dl0, cls1, cls2 = _attn_bwd_prep(dattn, attn, lse, t)
    dqs, dks, dvs = [], [], []
    for g, (dil, (da_c, dl_c, ls_c)) in enumerate(zip(DILATIONS, ((dattn, dl0, lse), cls1, cls2))):
        res = _attn_bwd(f"attn_bwd_{g}", *qkv_cls[g], da_c, dl_c, ls_c, t, dil,
                        side=_share_side([halves_mlp[n] for n in WAVE_MLP]) if g == 0 else None)
        if g == 0:
            res, others = res
            gshard = {n: _both_halves(halves_mlp[n], o) for n, o in zip(WAVE_MLP, others)}
        dqs.append(res[0])
        dks.append(res[1])
        dvs.append(res[2])
    (dz,), landed = _rope_bwd_join(dqs, dks, dvs, c_tab, s_tab, (dglu_a, dglu_b, dgate_a, dgate_b), t,
                                   side=_scatter_side([parts_bf_mid[n] for n in WAVE_MID]))
    halves_mid = _chip_sums(WAVE_MID, parts_mid, landed)

    gw_in_full, others = _mm_tn("in_proj_wgrad", u, dz, side=_share_side([halves_mid[n] for n in WAVE_MID]))
    gshard.update({n: _both_halves(halves_mid[n], o) for n, o in zip(WAVE_MID, others)})
    gw_in = {"w_in": gw_in_full}
    du_a, theirs = _mm("in_proj_bwd_a", dz, wfull["w_in"], nt=True, k_part=(0, 2), side=_swap_side([gw_in["w_in"]], [True]))
    parts, parts_bf = _pair_sums(("w_in",), gw_in, theirs)
    (gx, dg_mix), landed_in = _mm("in_proj_bwd_b", dz, wfull["w_in"], nt=True, k_part=(1, 2), extras=(du_a, x, d1),
                                  params=(g_mix,), epi=lambda acc, first, a, dres, g: _norm_bwd_epi(acc + first, a, dres, g)[1:],
                                  sums=[(1, D_MODEL)], side=_scatter_side([parts_bf["w_in"]]))
    halves = _chip_sums(("w_in",), parts, landed_in)

    gsmall = {"g_mix": dg_mix, "b_gate": dbg, "conv_b": dconv[CONV_K:CONV_K + 1], "conv_ln_g": dlng, "conv_ln_b": dlnb,
              "g_cross": dg_cross, "g_mem": dg_mem, "g_mlp": dg_mlp, "g_final": dg_final, "conv_w": dconv[:CONV_K]}
    return loss_row, gx, gshard, gsmall, halves["w_in"]


def kernel(x, mem, g_mix, w_in, b_gate, conv_w, conv_b, conv_ln_g, conv_ln_b, w_attn_proj, w_conv_proj, w_out, g_cross, g_mem, w_cq, w_ckv, w_co, g_mlp, w_up, w_down, g_final, loss_target, m_g_mix, m_w_in, m_b_gate, m_conv_w, m_conv_b, m_conv_ln_g, m_conv_ln_b, m_w_attn_proj, m_w_conv_proj, m_w_out, m_g_cross, m_g_mem, m_w_cq, m_w_ckv, m_w_co, m_g_mlp, m_w_up, m_w_down, m_g_final, v_g_mix, v_w_in, v_b_gate, v_conv_w, v_conv_b, v_conv_ln_g, v_conv_ln_b, v_w_attn_proj, v_w_conv_proj, v_w_out, v_g_cross, v_g_mem, v_w_cq, v_w_ckv, v_w_co, v_g_mlp, v_w_up, v_w_down, v_g_final):
    w = dict(g_mix=g_mix, w_in=w_in, b_gate=b_gate, conv_w=conv_w, conv_b=conv_b, conv_ln_g=conv_ln_g, conv_ln_b=conv_ln_b,
             w_attn_proj=w_attn_proj, w_conv_proj=w_conv_proj, w_out=w_out, g_cross=g_cross, g_mem=g_mem, w_cq=w_cq,
             w_ckv=w_ckv, w_co=w_co, g_mlp=g_mlp, w_up=w_up, w_down=w_down, g_final=g_final)
    mo = dict(g_mix=m_g_mix, w_in=m_w_in, b_gate=m_b_gate, conv_w=m_conv_w, conv_b=m_conv_b, conv_ln_g=m_conv_ln_g,
              conv_ln_b=m_conv_ln_b, w_attn_proj=m_w_attn_proj, w_conv_proj=m_w_conv_proj, w_out=m_w_out, g_cross=m_g_cross,
              g_mem=m_g_mem, w_cq=m_w_cq, w_ckv=m_w_ckv, w_co=m_w_co, g_mlp=m_g_mlp, w_up=m_w_up, w_down=m_w_down,
              g_final=m_g_final)
    vo = dict(g_mix=v_g_mix, w_in=v_w_in, b_gate=v_b_gate, conv_w=v_conv_w, conv_b=v_conv_b, conv_ln_g=v_conv_ln_g,
              conv_ln_b=v_conv_ln_b, w_attn_proj=v_w_attn_proj, w_conv_proj=v_w_conv_proj, w_out=v_w_out, g_cross=v_g_cross,
              g_mem=v_g_mem, w_cq=v_w_cq, w_ckv=v_w_ckv, w_co=v_w_co, g_mlp=v_g_mlp, w_up=v_w_up, w_down=v_w_down,
              g_final=v_g_final)
    shapes = {n: w[n].shape for n in ORDER}
    two_d = lambda a: a.reshape(a.shape[-2], a.shape[-1])
    chip = 2 * lax.axis_index("x") + lax.axis_index("y")

    shards = {n: two_d(w[n]).astype(BF) for n in BIG}
    cw_rows = 48
    cw_all, _ = _gather8("gather_conv_w", _pad_rows(conv_w.reshape(-1), cw_rows))
    cw_shard = CONV_K * (CONV_CH // 4)
    conv_w_full = jnp.concatenate(
        [cw_all[2 * j].reshape(-1)[:cw_shard].reshape(CONV_K, CONV_CH // 4) for j in range(4)], axis=1)

    small = {n: w[n] for n in SMALL}
    loss_row, gx, gshard, gsmall, half_w_in = _local_step(two_d(x), two_d(mem), two_d(loss_target), shards, small,
                                                          conv_w_full)
    loss = lax.psum(loss_row[0, 0], ("x", "y", "c"))

    small_names = SMALL + ("conv_w",)
    flat = jnp.concatenate([gsmall[n].reshape(-1) for n in small_names])
    sm_rows = -(-flat.shape[0] // (8 * LANES)) * 8
    (_, sm_sum), others = _gather8("reduce_small_grads", _pad_rows(flat, sm_rows), side=_share_side([half_w_in]))
    gshard["w_in"] = _both_halves(half_w_in, others[0])
    sm_sum = sm_sum.reshape(-1)
    off = 0
    for n in small_names:
        size = gsmall[n].size
        gshard[n] = sm_sum[off:off + size].reshape(gsmall[n].shape)
        off += size
    gshard["conv_w"] = lax.dynamic_slice_in_dim(gshard["conv_w"], chip * (CONV_CH // 4), CONV_CH // 4, axis=1)

    grads, deltas, new_m, new_v = {}, {}, {}, {}
    for n in BIG:
        d, m2, v2 = _adam(f"adamw_{n}", two_d(w[n]), gshard[n], two_d(mo[n]), two_d(vo[n]))
        grads[n], deltas[n], new_m[n], new_v[n] = (a.reshape(shapes[n]) for a in (gshard[n], d, m2, v2))
    pack = lambda src: jnp.concatenate([src[n].reshape(-1) for n in small_names])
    n_small = sum(w[n].size for n in small_names)
    ad_rows = -(-n_small // (8 * LANES)) * 8
    d, m2, v2 = _adam("adamw_small", *[_pad_rows(pack(src), ad_rows) for src in (w, gshard, mo, vo)])
    off = 0
    for n in small_names:
        size = w[n].size
        grads[n] = gshard[n].reshape(shapes[n])
        deltas[n], new_m[n], new_v[n] = (a.reshape(-1)[off:off + size].reshape(shapes[n]) for a in (d, m2, v2))
        off += size

    return (loss, gx.reshape(x.shape), *[grads[n] for n in ORDER], *[deltas[n] for n in ORDER],
            *[new_m[n] for n in ORDER], *[new_v[n] for n in ORDER])
```

```python
import functools

import jax
import jax.numpy as jnp
from jax import lax
from jax.experimental import pallas as pl
from jax.experimental.pallas import tpu as pltpu

F32 = jnp.float32
BF = jnp.bfloat16

D_MODEL = 1024
N_MEM = 256
HEAD_DIM = 128
HEADS_PER_GROUP = 4
DILATIONS = (1, 4, 16)
BLK = 128
GROUP_W = HEADS_PER_GROUP * HEAD_DIM
ATTN_WIDTH = 3 * GROUP_W
ROT_DIM = 32
ROPE_THETA = 500000.0
CONV_CH = 768
CONV_K = 31
CONV_KP = 32
IN_WIDTH = 8192
CROSS_HEADS = 4
CROSS_HEAD_DIM = 256
D_FF = 4096
EPS = 1e-6
ATTN_SCALE = HEAD_DIM ** -0.5
CROSS_SCALE = CROSS_HEAD_DIM ** -0.5
NEG = -1e30

ADAM_LR = 0.001
ADAM_B1 = 0.9
ADAM_B2 = 0.999
ADAM_EPS = 1e-08
ADAM_WD = 0.01
ADAM_STEP = 10

LANES = 128
SUBLANES = 8
VMEM_LIMIT = 56 * 1024 * 1024
MESH = pl.DeviceIdType.MESH
ANY = pl.BlockSpec(memory_space=pl.ANY)

GLU_A_COL = 3 * ATTN_WIDTH
GLU_B_COL = GLU_A_COL + CONV_CH
GATE_A_COL = GLU_B_COL + CONV_CH
GATE_B_COL = GATE_A_COL + D_MODEL


def _params(sem=None):
    return pltpu.CompilerParams(dimension_semantics=sem, vmem_limit_bytes=VMEM_LIMIT)


def _dot(a, b):
    return lax.dot_general(a, b, (((1,), (0,)), ((), ())), preferred_element_type=F32)


def _dot_nt(a, b):
    return lax.dot_general(a, b, (((1,), (1,)), ((), ())), preferred_element_type=F32)


def _dot_tn(a, b):
    return lax.dot_general(a, b, (((0,), (0,)), ((), ())), preferred_element_type=F32)


def _sig(x):
    return 1.0 / (1.0 + jnp.exp(-x))


def _glu(a, b):
    return a.astype(F32) * _sig(b.astype(F32))


class _Side:
    def __init__(self, arrays, out_shapes, n_sems, build, aliases=None):
        self.arrays, self.out_shapes, self.n_sems, self.build = list(arrays), list(out_shapes), n_sems, build
        self.aliases = aliases or {}


def _pcall(name, kern, grid, in_specs, out_specs, out_shape, scratch_shapes, sem, args, side=None):
    in_specs, out_specs, out_shape, scratch_shapes = list(in_specs), list(out_specs), list(out_shape), list(scratch_shapes)
    if side is None:
        return pl.pallas_call(kern, name=name, grid=grid, in_specs=in_specs, out_specs=out_specs, out_shape=out_shape,
                              scratch_shapes=scratch_shapes, compiler_params=_params(sem))(*args)
    ni, no, nsc = len(in_specs), len(out_specs), len(scratch_shapes)
    nsi, nso = len(side.arrays), len(side.out_shapes)

    def wrapped(*refs):
        ins, side_ins = refs[:ni], refs[ni:ni + nsi]
        outs, side_outs = refs[ni + nsi:ni + nsi + no], refs[ni + nsi + no:ni + nsi + no + nso]
        scratch = refs[ni + nsi + no + nso:ni + nsi + no + nso + nsc]
        send_sems, recv_sems = refs[-2:]
        start, finish = side.build(side_ins, side_outs, send_sems, recv_sems)
        if grid:
            first = functools.reduce(jnp.logical_and, [pl.program_id(a) == 0 for a in range(len(grid))])
            last = functools.reduce(jnp.logical_and, [pl.program_id(a) == g - 1 for a, g in enumerate(grid)])
            pl.when(first)(start)
            kern(*ins, *outs, *scratch)
            pl.when(last)(finish)
        else:
            start()
            kern(*ins, *outs, *scratch)
            finish()

    res = pl.pallas_call(
        wrapped, name=name, grid=grid, in_specs=in_specs + [ANY] * nsi, out_specs=out_specs + [ANY] * nso,
        out_shape=out_shape + side.out_shapes,
        scratch_shapes=scratch_shapes + [pltpu.SemaphoreType.DMA((side.n_sems,)), pltpu.SemaphoreType.DMA((side.n_sems,))],
        input_output_aliases={ni + k: no + v for k, v in side.aliases.items()},
        compiler_params=_params(("arbitrary",) * len(grid) if grid else None),
    )(*args, *side.arrays)
    return res[:no], res[no:]


def _rowcall(name, fn, n_rows, tile, ins, params, outs, accs=(), side=None):
    tile = min(tile, n_rows)
    ni, npar, no, na = len(ins), len(params), len(outs), len(accs)

    def kern(*refs):
        in_refs = refs[:ni + npar]
        o_refs = refs[ni + npar:ni + npar + no]
        a_refs = refs[ni + npar + no:]
        vals = fn(*[r[...] for r in in_refs])
        for r, v in zip(o_refs, vals[:no]):
            r[...] = v.astype(r.dtype)
        if na:
            @pl.when(pl.program_id(0) == 0)
            def _():
                for r in a_refs:
                    r[...] = jnp.zeros_like(r)
            for r, v in zip(a_refs, vals[no:]):
                r[...] += v

    in_specs = []
    arrays = []
    for spec in ins:
        arr, width, cb = spec[0], spec[1], spec[2]
        rb = spec[3] if len(spec) > 3 else 0
        in_specs.append(pl.BlockSpec((tile, width), functools.partial(lambda i, cb, rb: (i + rb, cb), cb=cb, rb=rb)))
        arrays.append(arr)
    for p in params:
        in_specs.append(pl.BlockSpec(p.shape, lambda i: (0, 0)))
        arrays.append(p)
    out_specs = [pl.BlockSpec((tile, w), lambda i: (i, 0)) for w, _ in outs]
    out_specs += [pl.BlockSpec(s, lambda i: (0, 0)) for s in accs]
    out_shape = [jax.ShapeDtypeStruct((n_rows, w), dt) for w, dt in outs]
    out_shape += [jax.ShapeDtypeStruct(s, F32) for s in accs]
    return _pcall(name, kern, (n_rows // tile,), in_specs, out_specs, out_shape, [],
                  ("arbitrary",) if na else ("parallel",), arrays, side)


def _mm(name, a, w3, *, nt=False, extras=(), params=(), epi=None, out_dtypes=(F32,), sums=(), tm=None, tn=None, tk=None,
        k_part=(0, 1), side=None):
    m, ka = a.shape
    ns, r, cs = w3.shape
    if not nt:
        k_dim, n = r, ns * cs
        tn = tn or min(cs, 2048)
        tk = tk or min(k_dim, 1024)
    else:
        k_dim, n = ns * cs, r
        tn = tn or min(r, 2048)
        tk = tk or min(cs, 1024)
    assert ka == k_dim, (name, a.shape, w3.shape)
    assert not sums or tn == n, name
    nk = k_dim // tk // k_part[1]
    k0 = k_part[0] * nk
    if not nt:
        nbs = cs // tn
        w_spec = pl.BlockSpec((None, tk, tn), lambda i, j, k: (j // nbs, k + k0, j % nbs))
    else:
        kbs = cs // tk
        w_spec = pl.BlockSpec((None, tn, tk), lambda i, j, k: ((k + k0) // kbs, j, (k + k0) % kbs))
    tm = tm or min(m, 1024)
    ne, no, nsum = len(extras) + len(params), len(out_dtypes), len(sums)

    def kern(a_ref, w_ref, *rest):
        e_refs = rest[:ne]
        o_refs = rest[ne:ne + no]
        s_refs = rest[ne + no:ne + no + nsum]

        def part():
            av = a_ref[...].astype(BF)
            return _dot_nt(av, w_ref[...]) if nt else _dot(av, w_ref[...])

        def finish(res):
            vals = epi(res, *[e[...] for e in e_refs]) if epi else (res,)
            for o, v in zip(o_refs, vals[:no]):
                o[...] = v.astype(o.dtype)
            for sr, v in zip(s_refs, vals[no:]):
                sr[...] += v

        if nsum:
            @pl.when(jnp.logical_and(pl.program_id(0) == 0, pl.program_id(2) == 0))
            def _():
                for sr in s_refs:
                    sr[...] = jnp.zeros_like(sr)

        if nk == 1:
            finish(part())
            return
        acc = rest[ne + no + nsum]
        k = pl.program_id(2)

        @pl.when(k == 0)
        def _():
            acc[...] = part()

        @pl.when(jnp.logical_and(k > 0, k < nk - 1))
        def _():
            acc[...] += part()

        @pl.when(k == nk - 1)
        def _():
            finish(acc[...] + part())

    in_specs = [pl.BlockSpec((tm, tk), lambda i, j, k: (i, k + k0)), w_spec]
    split = lambda items: [(it if isinstance(it, tuple) else (it, None)) for it in items]
    extras, params = split(extras), split(params)
    in_specs += [pl.BlockSpec((tm, tn), functools.partial(lambda i, j, k, off: (i, j + off), off=off or 0)) for _, off in extras]
    in_specs += [pl.BlockSpec(p.shape, lambda i, j, k: (0, 0)) if off is None else
                 pl.BlockSpec((p.shape[0], tn), functools.partial(lambda i, j, k, off: (0, j + off), off=off)) for p, off in params]
    extras, params = [e for e, _ in extras], [p for p, _ in params]
    out_specs = [pl.BlockSpec((tm, tn), lambda i, j, k: (i, j)) for _ in out_dtypes]
    out_specs += [pl.BlockSpec(sh, lambda i, j, k: (0, 0)) for sh in sums]
    out_shape = [jax.ShapeDtypeStruct((m, n), dt) for dt in out_dtypes] + [jax.ShapeDtypeStruct(sh, F32) for sh in sums]
    res = _pcall(name, kern, (m // tm, n // tn, nk), in_specs, out_specs, out_shape,
                 [pltpu.VMEM((tm, tn), F32)] if nk > 1 else [],
                 ("arbitrary",) * 3 if nsum else ("parallel", "parallel", "arbitrary"), (a, w3, *extras, *params), side)
    main = res[0] if side is not None else res
    main = main[0] if no + nsum == 1 else main
    return (main, res[1]) if side is not None else main


def _mm_tn(name, a, b, tm=None, tn=None, tk=None, side=None):
    t, ka = a.shape
    _, n = b.shape
    tm = tm or min(ka, 2048)
    tn = tn or min(n, 2048)
    tk = tk or min(t, 1024)

    def kern(a_ref, b_ref, o_ref):
        def part():
            return _dot_tn(a_ref[...].astype(BF), b_ref[...].astype(BF))

        @pl.when(pl.program_id(2) == 0)
        def _():
            o_ref[...] = part()

        @pl.when(pl.program_id(2) > 0)
        def _():
            o_ref[...] += part()

    res = _pcall(name, kern, (ka // tm, n // tn, t // tk),
                 [pl.BlockSpec((tk, tm), lambda i, j, k: (k, i)), pl.BlockSpec((tk, tn), lambda i, j, k: (k, j))],
                 [pl.BlockSpec((tm, tn), lambda i, j, k: (i, j))], [jax.ShapeDtypeStruct((ka, n), F32)], [],
                 ("parallel", "parallel", "arbitrary"), (a, b), side)
    return (res[0][0], res[1]) if side is not None else res[0]


def _rms(x, g):
    return x * lax.rsqrt(jnp.mean(x * x, axis=-1, keepdims=True) + EPS) * g


def _rms_bwd(x, g, dy):
    r = lax.rsqrt(jnp.mean(x * x, axis=-1, keepdims=True) + EPS)
    xh = x * r
    dxh = dy * g
    dx = r * (dxh - xh * jnp.mean(dxh * xh, axis=-1, keepdims=True))
    return dx, jnp.sum(dy * xh, axis=0, keepdims=True)


def _rot(t, c, s):
    lane = lax.broadcasted_iota(jnp.int32, t.shape, 1)
    swapped = jnp.where(lane < ROT_DIM // 2, pltpu.roll(t, HEAD_DIM - ROT_DIM // 2, 1), pltpu.roll(t, ROT_DIM // 2, 1))
    return t * c + swapped * s


def _rope_tables(t):
    half = ROT_DIM // 2
    pos = jnp.arange(t, dtype=F32)
    inv_freq = ROPE_THETA ** (-jnp.arange(0, ROT_DIM, 2, dtype=F32) / ROT_DIM)
    ang = pos[:, None] * inv_freq[None, :]
    cos, sin = jnp.cos(ang), jnp.sin(ang)
    ones = jnp.ones((t, HEAD_DIM - ROT_DIM), F32)
    c_tab = jnp.concatenate([cos, cos, ones], axis=1)
    s_tab = jnp.concatenate([-sin, sin, 0.0 * ones], axis=1)
    return c_tab, s_tab


def _merge_fn(o0, o1, o2, l0, l1, l2):
    m = jnp.maximum(jnp.maximum(l0, l1), l2)
    e0, e1, e2 = jnp.exp(l0 - m), jnp.exp(l1 - m), jnp.exp(l2 - m)
    tot = e0 + e1 + e2
    return (e0 * o0 + e1 * o1 + e2 * o2) / tot, m + jnp.log(tot)


def _ln_parts(c1):
    mu = jnp.mean(c1, axis=-1, keepdims=True)
    xc = c1 - mu
    r = lax.rsqrt(jnp.mean(xc * xc, axis=-1, keepdims=True) + EPS)
    return xc * r, r


def _ln_silu_fn(c1, g, b):
    xh, _ = _ln_parts(c1)
    yl = xh * g + b
    return (yl * _sig(yl),)


def _ln_silu_bwd_fn(c1, dout, g, b):
    xh, r = _ln_parts(c1)
    yl = xh * g + b
    s = _sig(yl)
    dyl = dout * (s + yl * s * (1.0 - s))
    dxh = dyl * g
    dx = r * (dxh - jnp.mean(dxh, axis=-1, keepdims=True) - xh * jnp.mean(dxh * xh, axis=-1, keepdims=True))
    return dx, jnp.sum(dyl * xh, axis=0, keepdims=True), jnp.sum(dyl, axis=0, keepdims=True)


def _gate_epi(yc, ya, ga, gb, ba, bb):
    return _sig(ga + ba) * ya + _sig(gb + bb) * yc, yc


def _gate_bwd_epi(dm, ya, yc, ga, gb, bg):
    sa = _sig(ga + bg[:, :D_MODEL])
    sb = _sig(gb + bg[:, D_MODEL:])
    dga = dm * ya * sa * (1.0 - sa)
    dgb = dm * yc * sb * (1.0 - sb)
    return dm * sa, dm * sb, dga, dgb, jnp.sum(dga, axis=0, keepdims=True), jnp.sum(dgb, axis=0, keepdims=True)


def _res_norm_epi(acc, res, g):
    xn = res + acc
    return xn, _rms(xn, g)


def _norm_bwd_epi(acc, a, dres, g):
    dx, dg = _rms_bwd(a, g, acc)
    return dres + dx, dres + dx, dg


def _final_epi(acc, res, tgt, g):
    return _final_fn(res + acc, tgt, g)


def _final_fn(x3, tgt, g):
    err = _rms(x3, g) - tgt
    lrow = jnp.sum(err * err, axis=-1, keepdims=True) * (0.5 / D_MODEL)
    lsum = jnp.sum(lrow, axis=0, keepdims=True)
    dx, dg = _rms_bwd(x3, g, err * (1.0 / D_MODEL))
    return dx, dx, jnp.broadcast_to(lsum, (1, LANES)), dg


def _attn_geometry(t, dil):
    cls = t // dil
    rows = min(8 * BLK, cls)
    return rows, rows // BLK, cls // rows


def _head_lanes(h):
    return slice(h * HEAD_DIM, (h + 1) * HEAD_DIM)


def _band_mask():
    row = lax.broadcasted_iota(jnp.int32, (BLK, 2 * BLK), 0)
    col = lax.broadcasted_iota(jnp.int32, (BLK, 2 * BLK), 1)
    return jnp.logical_and(col >= row, col <= row + BLK), col


def _stage_window(scr, halo_ref, cur_ref):
    scr[:BLK, :] = halo_ref[...]
    scr[BLK:, :] = cur_ref[...]


def _attn_fwd(name, q, k, v, t, dil):
    rows, nbk, spc = _attn_geometry(t, dil)

    def kern(q_ref, k_ref, kh_ref, v_ref, vh_ref, o_ref, l_ref, k_scr, v_scr):
        i = pl.program_id(0)
        first_shift = jnp.where(i % spc == 0, BLK, 0)
        _stage_window(k_scr, kh_ref, k_ref)
        _stage_window(v_scr, vh_ref, v_ref)
        band, col = _band_mask()
        band_first = jnp.logical_and(band, col >= first_shift)
        for h in range(HEADS_PER_GROUP):
            hs = _head_lanes(h)
            for b in range(nbk):
                rs, win = slice(b * BLK, (b + 1) * BLK), slice(b * BLK, (b + 2) * BLK)
                s = jnp.where(band_first if b == 0 else band, _dot_nt(q_ref[rs, hs], k_scr[win, hs]) * ATTN_SCALE, NEG)
                m = jnp.max(s, axis=1, keepdims=True)
                p = jnp.exp(s - m)
                tot = jnp.sum(p, axis=1, keepdims=True)
                o_ref[rs, hs] = _dot(p.astype(BF), v_scr[win, hs]) / tot
                l_ref[rs, hs] = jnp.broadcast_to(m + jnp.log(tot), (BLK, HEAD_DIM))

    def cur(cb):
        return pl.BlockSpec((rows, GROUP_W), lambda i: (i, cb))

    def halo(cb):
        return pl.BlockSpec((BLK, GROUP_W), lambda i: (jnp.maximum(i * nbk - 1, 0), cb))

    (qa, qc), (ka, kc_), (va, vc_) = q, k, v
    return _pcall(name, kern, (t // rows,), [cur(qc), cur(kc_), halo(kc_), cur(vc_), halo(vc_)],
                  [pl.BlockSpec((rows, GROUP_W), lambda i: (i, 0))] * 2, [jax.ShapeDtypeStruct((t, GROUP_W), F32)] * 2,
                  [pltpu.VMEM((rows + BLK, GROUP_W), BF)] * 2, ("parallel",), (qa, ka, ka, va, va))


def _attn_bwd(name, q, k, v, da, dl, lse, t, dil, side=None):
    rows, nbk, spc = _attn_geometry(t, dil)
    nblk = t // BLK

    def kern(q_ref, qn_ref, k_ref, kh_ref, v_ref, vh_ref, da_ref, dan_ref, dl_ref, dln_ref, ls_ref, lsn_ref,
             dq_ref, dk_ref, dv_ref, k_scr, v_scr):
        i = pl.program_id(0)
        first_shift = jnp.where(i % spc == 0, BLK, 0)
        next_shift = jnp.where((i + 1) % spc == 0, BLK, 0)
        _stage_window(k_scr, kh_ref, k_ref)
        _stage_window(v_scr, vh_ref, v_ref)
        band, col = _band_mask()
        band_first = jnp.logical_and(band, col >= first_shift)
        row1 = lax.broadcasted_iota(jnp.int32, (BLK, BLK), 0)
        col1 = lax.broadcasted_iota(jnp.int32, (BLK, BLK), 1)
        pend_k, pend_v = [None] * HEADS_PER_GROUP, [None] * HEADS_PER_GROUP
        for b in range(nbk):
            rs, win = slice(b * BLK, (b + 1) * BLK), slice(b * BLK, (b + 2) * BLK)
            for h in range(HEADS_PER_GROUP):
                hs = _head_lanes(h)
                qb, dab = q_ref[rs, hs], da_ref[rs, hs].astype(BF)
                kw, vw = k_scr[win, hs], v_scr[win, hs]
                p = jnp.where(band_first if b == 0 else band,
                              jnp.exp(_dot_nt(qb, kw) * ATTN_SCALE - ls_ref[rs, hs][:, :1]), 0.0)
                ds = (p * (_dot_nt(dab, vw) - dl_ref[rs, hs][:, :1]) * ATTN_SCALE).astype(BF)
                dq_ref[rs, hs] = _dot(ds, kw).astype(BF)
                dkw, dvw = _dot_tn(ds, qb), _dot_tn(p.astype(BF), dab)
                if b >= 1:
                    ps = slice((b - 1) * BLK, b * BLK)
                    dk_ref[ps, hs] = (pend_k[h] + dkw[:BLK]).astype(BF)
                    dv_ref[ps, hs] = (pend_v[h] + dvw[:BLK]).astype(BF)
                pend_k[h], pend_v[h] = dkw[BLK:], dvw[BLK:]
        ls_rows = slice((nbk - 1) * BLK, nbk * BLK)
        last = slice(nbk * BLK, (nbk + 1) * BLK)
        for h in range(HEADS_PER_GROUP):
            hs = _head_lanes(h)
            qb, dab = qn_ref[:, hs], dan_ref[:, hs].astype(BF)
            kp, vp = k_scr[last, hs], v_scr[last, hs]
            p = jnp.where(col1 >= row1 + next_shift, jnp.exp(_dot_nt(qb, kp) * ATTN_SCALE - lsn_ref[:, hs]), 0.0)
            ds = (p * (_dot_nt(dab, vp) - dln_ref[:, hs]) * ATTN_SCALE).astype(BF)
            dk_ref[ls_rows, hs] = (pend_k[h] + _dot_tn(ds, qb)).astype(BF)
            dv_ref[ls_rows, hs] = (pend_v[h] + _dot_tn(p.astype(BF), dab)).astype(BF)

    def cur(cb):
        return pl.BlockSpec((rows, GROUP_W), lambda i: (i, cb))

    def prev(cb):
        return pl.BlockSpec((BLK, GROUP_W), lambda i: (jnp.maximum(i * nbk - 1, 0), cb))

    def nxt(cb):
        return pl.BlockSpec((BLK, GROUP_W), lambda i: (jnp.minimum((i + 1) * nbk, nblk - 1), cb))

    (qa, qc), (ka, kc_), (va, vc_) = q, k, v
    return _pcall(name, kern, (t // rows,),
                  [cur(qc), nxt(qc), cur(kc_), prev(kc_), cur(vc_), prev(vc_), cur(0), nxt(0), cur(0), nxt(0), cur(0), nxt(0)],
                  [pl.BlockSpec((rows, GROUP_W), lambda i: (i, 0))] * 3, [jax.ShapeDtypeStruct((t, GROUP_W), BF)] * 3,
                  [pltpu.VMEM((rows + BLK, GROUP_W), BF)] * 2, ("parallel",),
                  (qa, qa, ka, ka, va, va, da, da, dl, dl, lse, lse), side)


CLS_TILE = 512


def _cls_block(t, tile, dil, dtype):
    if dil == 1:
        return pl.BlockSpec((tile, GROUP_W), lambda i: (i, 0)), jax.ShapeDtypeStruct((t, GROUP_W), dtype)
    return (pl.BlockSpec((dil, tile // dil, GROUP_W), lambda i: (0, i, 0)),
            jax.ShapeDtypeStruct((dil, t // dil, GROUP_W), dtype))


def _head_scratch(tile):
    return pltpu.VMEM((tile, HEAD_DIM), F32)


def _rope_split(z, c_tab, s_tab, t):
    tile = min(CLS_TILE, t)
    n_heads = ATTN_WIDTH // HEAD_DIM

    def kern(zq_ref, zk_ref, zv_ref, c_ref, s_ref, *rest):
        outs, scr = rest[:9], rest[9]
        c, s = c_ref[...], s_ref[...]
        for which, z_ref in enumerate((zq_ref, zk_ref, zv_ref)):
            for h in range(n_heads):
                g, hs = h // HEADS_PER_GROUP, _head_lanes(h % HEADS_PER_GROUP)
                val = z_ref[:, h * HEAD_DIM:(h + 1) * HEAD_DIM].astype(F32)
                if which < 2:
                    val = _rot(val, c, s)
                if g == 0:
                    outs[which][:, hs] = val.astype(BF)
                    continue
                scr[...] = val
                dil = DILATIONS[g]
                for r in range(dil):
                    outs[3 * g + which][r, :, hs] = scr[pl.ds(r, tile // dil, stride=dil), :].astype(BF)

    blocks = [_cls_block(t, tile, DILATIONS[g], BF) for g in range(3) for _ in range(3)]
    zspec = lambda cb: pl.BlockSpec((tile, ATTN_WIDTH), lambda i: (i, cb))
    tab = pl.BlockSpec((tile, HEAD_DIM), lambda i: (i, 0))
    return _pcall("rope", kern, (t // tile,), [zspec(0), zspec(1), zspec(2), tab, tab], [b[0] for b in blocks],
                  [b[1] for b in blocks], [_head_scratch(tile)], ("parallel",), (z, z, z, c_tab, s_tab))


def _merge_classes(outs, lses, t):
    tile = min(CLS_TILE, t)

    def kern(o0, l0, o1, l1, o2, l2, attn_ref, lse_ref, s_o1, s_l1, s_o2, s_l2):
        for h in range(HEADS_PER_GROUP):
            hs = _head_lanes(h)
            for src, dst, dil in ((o1, s_o1, DILATIONS[1]), (l1, s_l1, DILATIONS[1]), (o2, s_o2, DILATIONS[2]),
                                  (l2, s_l2, DILATIONS[2])):
                for r in range(dil):
                    dst[pl.ds(r, tile // dil, stride=dil), :] = src[r, :, hs]
            attn_ref[:, hs], lse_ref[:, hs] = _merge_fn(o0[:, hs], s_o1[...], s_o2[...], l0[:, hs], s_l1[...], s_l2[...])

    blocks = [_cls_block(t, tile, DILATIONS[g], F32) for g in range(3)]
    args = []
    for g in range(3):
        args += [outs[g].reshape(blocks[g][1].shape), lses[g].reshape(blocks[g][1].shape)]
    tok = pl.BlockSpec((tile, GROUP_W), lambda i: (i, 0))
    return _pcall("attn_merge", kern, (t // tile,), [blocks[g][0] for g in range(3) for _ in range(2)], [tok, tok],
                  [jax.ShapeDtypeStruct((t, GROUP_W), F32)] * 2, [_head_scratch(tile)] * 4, ("parallel",), args)


def _attn_bwd_prep(dattn, attn, lse, t):
    tile = min(CLS_TILE, t)

    def kern(da_ref, at_ref, ls_ref, dl0, da1, dl1, ls1, da2, dl2, ls2, s_da, s_dl, s_ls):
        for h in range(HEADS_PER_GROUP):
            hs = _head_lanes(h)
            da = da_ref[:, hs]
            s_da[...] = da
            s_dl[...] = jnp.broadcast_to(jnp.sum(da * at_ref[:, hs], axis=1, keepdims=True), (tile, HEAD_DIM))
            s_ls[...] = ls_ref[:, hs]
            dl0[:, hs] = s_dl[...]
            for oda, odl, ols, dil in ((da1, dl1, ls1, DILATIONS[1]), (da2, dl2, ls2, DILATIONS[2])):
                for r in range(dil):
                    rows = pl.ds(r, tile // dil, stride=dil)
                    oda[r, :, hs] = s_da[rows, :].astype(BF)
                    odl[r, :, hs] = s_dl[rows, :]
                    ols[r, :, hs] = s_ls[rows, :]

    tok = pl.BlockSpec((tile, GROUP_W), lambda i: (i, 0))
    blocks = [(tok, jax.ShapeDtypeStruct((t, GROUP_W), F32))]
    for g in (1, 2):
        blocks += [_cls_block(t, tile, DILATIONS[g], BF), _cls_block(t, tile, DILATIONS[g], F32),
                   _cls_block(t, tile, DILATIONS[g], F32)]
    res = _pcall("attn_bwd_prep", kern, (t // tile,), [tok, tok, tok], [b[0] for b in blocks], [b[1] for b in blocks],
                 [_head_scratch(tile)] * 3, ("parallel",), (dattn, attn, lse))
    flat = [a.reshape(t, GROUP_W) for a in res]
    return flat[0], flat[1:4], flat[4:7]


def _rope_bwd_join(dqs, dks, dvs, c_tab, s_tab, tail, t, side=None):
    tile = min(256, t)
    n_heads = ATTN_WIDTH // HEAD_DIM

    def kern(q0, q1, q2, k0, k1, k2, v0, v1, v2, c_ref, s_ref, ga_ref, gb_ref, gta_ref, gtb_ref, dz_ref, scr):
        c, s = c_ref[...], -s_ref[...]
        for which, srcs in enumerate(((q0, q1, q2), (k0, k1, k2), (v0, v1, v2))):
            for h in range(n_heads):
                g, hs = h // HEADS_PER_GROUP, _head_lanes(h % HEADS_PER_GROUP)
                if g == 0:
                    val = srcs[0][:, hs].astype(F32)
                else:
                    dil = DILATIONS[g]
                    for r in range(dil):
                        scr[pl.ds(r, tile // dil, stride=dil), :] = srcs[g][r, :, hs].astype(F32)
                    val = scr[...]
                if which < 2:
                    val = _rot(val, c, s)
                col = which * ATTN_WIDTH + h * HEAD_DIM
                dz_ref[:, col:col + HEAD_DIM] = val.astype(BF)
        dz_ref[:, GLU_A_COL:GLU_B_COL] = ga_ref[...]
        dz_ref[:, GLU_B_COL:GATE_A_COL] = gb_ref[...]
        dz_ref[:, GATE_A_COL:GATE_B_COL] = gta_ref[...]
        dz_ref[:, GATE_B_COL:] = gtb_ref[...]

    blocks = [_cls_block(t, tile, DILATIONS[g], BF) for g in range(3)]
    args = [a.reshape(blocks[g][1].shape) for grp in (dqs, dks, dvs) for g, a in enumerate(grp)]
    tab = pl.BlockSpec((tile, HEAD_DIM), lambda i: (i, 0))
    row = lambda w: pl.BlockSpec((tile, w), lambda i: (i, 0))
    return _pcall("rope_bwd", kern, (t // tile,), [blocks[g][0] for _ in range(3) for g in range(3)]
                  + [tab, tab, row(CONV_CH), row(CONV_CH), row(D_MODEL), row(D_MODEL)], [row(IN_WIDTH)],
                  [jax.ShapeDtypeStruct((t, IN_WIDTH), BF)], [_head_scratch(tile)], ("parallel",),
                  (*args, c_tab, s_tab, *tail), side)


def _cross_probs(qh, kh):
    s = _dot_nt(qh, kh) * CROSS_SCALE
    e = jnp.exp(s - jnp.max(s, axis=1, keepdims=True))
    return e, jnp.sum(e, axis=1, keepdims=True)


def _cross_fwd(cq, ckv, t):
    rows = min(512, t)

    def kern(q_ref, kv_ref, o_ref):
        for h in range(CROSS_HEADS):
            hs = slice(h * CROSS_HEAD_DIM, (h + 1) * CROSS_HEAD_DIM)
            vs = slice(D_MODEL + h * CROSS_HEAD_DIM, D_MODEL + (h + 1) * CROSS_HEAD_DIM)
            e, tot = _cross_probs(q_ref[:, hs], kv_ref[:, hs])
            o_ref[:, hs] = (_dot(e.astype(BF), kv_ref[:, vs]) / tot).astype(BF)

    return pl.pallas_call(
        kern, name="cross_fwd", grid=(t // rows,),
        in_specs=[pl.BlockSpec((rows, D_MODEL), lambda i: (i, 0)), pl.BlockSpec((N_MEM, 2 * D_MODEL), lambda i: (0, 0))],
        out_specs=pl.BlockSpec((rows, D_MODEL), lambda i: (i, 0)),
        out_shape=jax.ShapeDtypeStruct((t, D_MODEL), BF),
        compiler_params=_params(("parallel",)),
    )(cq, ckv)


def _cross_bwd(cq, ckv, dco, t):
    rows = min(512, t)

    def kern(q_ref, kv_ref, do_ref, dq_ref, dkv_ref):
        @pl.when(pl.program_id(0) == 0)
        def _():
            dkv_ref[...] = jnp.zeros_like(dkv_ref)
        for h in range(CROSS_HEADS):
            hs = slice(h * CROSS_HEAD_DIM, (h + 1) * CROSS_HEAD_DIM)
            vs = slice(D_MODEL + h * CROSS_HEAD_DIM, D_MODEL + (h + 1) * CROSS_HEAD_DIM)
            qh, kh, vh, doh = q_ref[:, hs], kv_ref[:, hs], kv_ref[:, vs], do_ref[:, hs]
            e, tot = _cross_probs(qh, kh)
            p = e / tot
            dp = _dot_nt(doh, vh)
            ds = (p * (dp - jnp.sum(p * dp, axis=1, keepdims=True)) * CROSS_SCALE).astype(BF)
            dq_ref[:, hs] = _dot(ds, kh).astype(BF)
            dkv_ref[:, hs] += _dot_tn(ds, qh)
            dkv_ref[:, vs] += _dot_tn(p.astype(BF), doh)

    return pl.pallas_call(
        kern, name="cross_bwd", grid=(t // rows,),
        in_specs=[pl.BlockSpec((rows, D_MODEL), lambda i: (i, 0)), pl.BlockSpec((N_MEM, 2 * D_MODEL), lambda i: (0, 0)),
                  pl.BlockSpec((rows, D_MODEL), lambda i: (i, 0))],
        out_specs=[pl.BlockSpec((rows, D_MODEL), lambda i: (i, 0)), pl.BlockSpec((N_MEM, 2 * D_MODEL), lambda i: (0, 0))],
        out_shape=[jax.ShapeDtypeStruct((t, D_MODEL), BF), jax.ShapeDtypeStruct((N_MEM, 2 * D_MODEL), F32)],
        compiler_params=_params(("arbitrary",)),
    )(cq, ckv, dco)


CONV_TILE = 512
CONV_CHUNK = 128
HALO = 32


def _conv_fwd(z, w32, bias, t, side=None):
    tile = min(CONV_TILE, t)
    a_cb, b_cb = GLU_A_COL // LANES, GLU_B_COL // LANES
    hb = tile // HALO

    def kern(a_ref, b_ref, ah_ref, bh_ref, w_ref, bias_ref, o_ref, g_scr):
        i = pl.program_id(1)
        g_scr[HALO:, :] = _glu(a_ref[...], b_ref[...])
        g_scr[:HALO, :] = _glu(ah_ref[...], bh_ref[...]) * jnp.where(i > 0, 1.0, 0.0)
        for c in range(tile // CONV_CHUNK):
            acc = jnp.broadcast_to(bias_ref[...], (CONV_CHUNK, LANES))
            for j in range(CONV_K):
                lo = c * CONV_CHUNK + HALO - (CONV_K - 1) + j
                acc = acc + w_ref[j:j + 1, :] * g_scr[lo:lo + CONV_CHUNK, :]
            o_ref[c * CONV_CHUNK:(c + 1) * CONV_CHUNK, :] = acc

    def cur(cb):
        return pl.BlockSpec((tile, LANES), lambda j, i: (i, cb + j))

    def prev(cb):
        return pl.BlockSpec((HALO, LANES), lambda j, i: (jnp.maximum(i * hb - 1, 0), cb + j))

    return _pcall("conv_fwd", kern, (CONV_CH // LANES, t // tile),
                  [cur(a_cb), cur(b_cb), prev(a_cb), prev(b_cb),
                   pl.BlockSpec((CONV_KP, LANES), lambda j, i: (0, j)), pl.BlockSpec((1, LANES), lambda j, i: (0, j))],
                  [pl.BlockSpec((tile, LANES), lambda j, i: (i, j))], [jax.ShapeDtypeStruct((t, CONV_CH), F32)],
                  [pltpu.VMEM((tile + HALO, LANES), F32)], ("parallel", "parallel"), (z, z, z, z, w32, bias), side)


def _conv_bwd(z, dc1, w32, t, side=None):
    tile = min(CONV_TILE, t)
    a_cb, b_cb = GLU_A_COL // LANES, GLU_B_COL // LANES
    hb = tile // HALO
    n_tiles = t // tile
    n_chunks = tile // CONV_CHUNK

    def kern(a_ref, b_ref, ah_ref, bh_ref, d_ref, dn_ref, w_ref, da_ref, db_ref, dw_ref, g_scr, d_scr):
        i = pl.program_id(1)
        g_scr[HALO:, :] = _glu(a_ref[...], b_ref[...])
        g_scr[:HALO, :] = _glu(ah_ref[...], bh_ref[...]) * jnp.where(i > 0, 1.0, 0.0)
        d_scr[:tile, :] = d_ref[...]
        d_scr[tile:, :] = dn_ref[...] * jnp.where(i < n_tiles - 1, 1.0, 0.0)

        @pl.when(i == 0)
        def _():
            dw_ref[...] = jnp.zeros_like(dw_ref)

        for c in range(n_chunks):
            cs = slice(c * CONV_CHUNK, (c + 1) * CONV_CHUNK)
            acc = jnp.zeros((CONV_CHUNK, LANES), F32)
            for j in range(CONV_K):
                lo = c * CONV_CHUNK + (CONV_K - 1) - j
                acc = acc + w_ref[j:j + 1, :] * d_scr[lo:lo + CONV_CHUNK, :]
            sgc = _sig(b_ref[cs, :].astype(F32))
            da_ref[cs, :] = (acc * sgc).astype(BF)
            db_ref[cs, :] = (acc * a_ref[cs, :].astype(F32) * sgc * (1.0 - sgc)).astype(BF)
        for j in range(CONV_K):
            tot = jnp.zeros((SUBLANES, LANES), F32)
            for c in range(n_chunks):
                lo = c * CONV_CHUNK + HALO - (CONV_K - 1) + j
                prod = d_ref[c * CONV_CHUNK:(c + 1) * CONV_CHUNK, :] * g_scr[lo:lo + CONV_CHUNK, :]
                tot = tot + jnp.sum(prod.reshape(CONV_CHUNK // SUBLANES, SUBLANES, LANES), axis=0)
            dw_ref[j:j + 1, :] += jnp.sum(tot, axis=0, keepdims=True)
        dw_ref[CONV_K:CONV_KP, :] += jnp.sum(d_ref[...], axis=0, keepdims=True)

    def cur(cb):
        return pl.BlockSpec((tile, LANES), lambda j, i: (i, cb + j))

    def prev(cb):
        return pl.BlockSpec((HALO, LANES), lambda j, i: (jnp.maximum(i * hb - 1, 0), cb + j))

    return _pcall(
        "conv_bwd", kern, (CONV_CH // LANES, n_tiles),
        [cur(a_cb), cur(b_cb), prev(a_cb), prev(b_cb), cur(0),
         pl.BlockSpec((HALO, LANES), lambda j, i: (jnp.minimum((i + 1) * hb, t // HALO - 1), j)),
         pl.BlockSpec((CONV_KP, LANES), lambda j, i: (0, j))],
        [pl.BlockSpec((tile, LANES), lambda j, i: (i, j)), pl.BlockSpec((tile, LANES), lambda j, i: (i, j)),
         pl.BlockSpec((CONV_KP, LANES), lambda j, i: (0, j))],
        [jax.ShapeDtypeStruct((t, CONV_CH), BF), jax.ShapeDtypeStruct((t, CONV_CH), BF),
         jax.ShapeDtypeStruct((CONV_KP, CONV_CH), F32)],
        [pltpu.VMEM((tile + HALO, LANES), F32), pltpu.VMEM((tile + HALO, LANES), F32)],
        ("parallel", "arbitrary"), (z, z, z, z, dc1, dc1, w32), side)


def _adam_fn(w, g, m, v):
    m = ADAM_B1 * m + (1.0 - ADAM_B1) * g
    v = ADAM_B2 * v + (1.0 - ADAM_B2) * (g * g)
    m_hat = m / (1.0 - ADAM_B1 ** ADAM_STEP)
    v_hat = v / (1.0 - ADAM_B2 ** ADAM_STEP)
    delta = -ADAM_LR * (m_hat / (jnp.sqrt(v_hat) + ADAM_EPS) + ADAM_WD * w)
    return delta, m, v


def _adam(name, w, g, m, v):
    rows, cols = w.shape
    tile = _ew_tile(rows, cols)
    return _rowcall(name, _adam_fn, rows, tile, [(a, cols, 0) for a in (w, g, m, v)], [], [(cols, F32)] * 3)


def _place():
    x, y, c = lax.axis_index("x"), lax.axis_index("y"), lax.axis_index("c")
    chips = [(1 - x, y), (x, 1 - y), (1 - x, 1 - y)]
    return x, y, c, chips


def _gather_side(shards):
    nw = len(shards)
    chip = 2 * lax.axis_index("x") + lax.axis_index("y")
    staged = [lax.dynamic_update_index_in_dim(jnp.zeros((4,) + s.shape, s.dtype), s, chip, 0) for s in shards]

    def build(_, outs, send_sems, recv_sems):
        x, y, c, chips = _place()
        me = 2 * x + y
        sibling = (x, y, 1 - c)

        def half(w, lead, h):
            n = shards[w].shape[0] // 2
            return outs[w].at[lead, pl.ds(h * n, n)]

        def copy(w, k, part, to):
            return pltpu.make_async_remote_copy(src_ref=part, dst_ref=part, send_sem=send_sems.at[6 * w + k],
                                                recv_sem=recv_sems.at[6 * w + k], device_id=to, device_id_type=MESH)

        def start():
            for w in range(nw):
                for k, (px, py) in enumerate(chips):
                    copy(w, k, half(w, me, c), (px, py, c)).start()

        def finish():
            for w in range(nw):
                for k, (px, py) in enumerate(chips):
                    landed = half(w, 2 * px + py, c)
                    copy(w, k, landed, (px, py, c)).wait_recv()
                    copy(w, 3 + k, landed, sibling).start()
            for w in range(nw):
                for k, (px, py) in enumerate(chips):
                    copy(w, 3 + k, half(w, 2 * px + py, 1 - c), sibling).wait_recv()
            for w in range(nw):
                for k, (px, py) in enumerate(chips):
                    copy(w, k, half(w, me, c), (px, py, c)).wait_send()
                    copy(w, 3 + k, half(w, 2 * px + py, c), sibling).wait_send()

        return start, finish

    return _Side(staged, [jax.ShapeDtypeStruct((4,) + s.shape, s.dtype) for s in shards], 6 * nw, build,
                 aliases={w: w for w in range(nw)})


def _gather8(name, v, side=None):
    rows = v.shape[0]

    def body(v_ref, all_ref, sum_ref, send_sems, recv_sems):
        x, y, c, _ = _place()
        me = 4 * x + 2 * y + c
        all_ref[me] = v_ref[...]
        copies = []
        for k in range(1, 8):
            px, py, pc = x ^ (k >> 2), y ^ ((k >> 1) & 1), c ^ (k & 1)
            copies.append(pltpu.make_async_remote_copy(
                src_ref=v_ref, dst_ref=all_ref.at[me], send_sem=send_sems.at[k - 1], recv_sem=recv_sems.at[k - 1],
                device_id=(px, py, pc), device_id_type=MESH))
            copies[-1].start()
        for k in range(1, 8):
            px, py, pc = x ^ (k >> 2), y ^ ((k >> 1) & 1), c ^ (k & 1)
            theirs = all_ref.at[4 * px + 2 * py + pc]
            pltpu.make_async_remote_copy(
                src_ref=theirs, dst_ref=theirs, send_sem=send_sems.at[k - 1], recv_sem=recv_sems.at[k - 1],
                device_id=(px, py, pc), device_id_type=MESH).wait_recv()
        for cp in copies:
            cp.wait_send()
        tot = all_ref[0]
        for d in range(1, 8):
            tot = tot + all_ref[d]
        sum_ref[...] = tot

    vm = pl.BlockSpec(memory_space=pltpu.VMEM)
    return _pcall(name, body, (), [vm], [vm, vm],
                  [jax.ShapeDtypeStruct((8, rows, LANES), F32), jax.ShapeDtypeStruct((rows, LANES), F32)],
                  [pltpu.SemaphoreType.DMA((7,)), pltpu.SemaphoreType.DMA((7,))], None, (v,), side)


def _region(ref, col_sharded, shape, j, h):
    r, ccols = shape
    if col_sharded:
        return ref.at[pl.ds(h * (r // 2), r // 2), pl.ds(j * (ccols // 4), ccols // 4)]
    n = r // 8
    return ref.at[pl.ds((2 * j + h) * n, n), :]


def _region_shape(col_sharded, shape):
    r, ccols = shape
    return (r // 2, ccols // 4) if col_sharded else (r // 8, ccols)


def _exchange(copies):
    def build(ins, outs, send_sems, recv_sems):
        def start():
            for cp in copies(ins, outs, send_sems, recv_sems):
                cp.start()

        def finish():
            for cp in copies(ins, outs, send_sems, recv_sems):
                cp.wait()

        return start, finish
    return build


def _swap_side(grads, kinds):
    nw = len(grads)

    def copies(ins, theirs, send_sems, recv_sems):
        x, y, c, _ = _place()
        return [pltpu.make_async_remote_copy(
            src_ref=_region(ins[w], kinds[w], grads[w].shape, j, 1 - c), dst_ref=theirs[w].at[j],
            send_sem=send_sems.at[4 * w + j], recv_sem=recv_sems.at[4 * w + j], device_id=(x, y, 1 - c), device_id_type=MESH)
            for w in range(nw) for j in range(4)]

    shapes = [jax.ShapeDtypeStruct((4,) + _region_shape(kinds[w], grads[w].shape), F32) for w in range(nw)]
    return _Side(grads, shapes, 4 * nw, _exchange(copies))


def _scatter_side(parts):
    nw = len(parts)

    def copies(ins, outs, send_sems, recv_sems):
        x, y, c, chips = _place()
        return [pltpu.make_async_remote_copy(
            src_ref=ins[w].at[2 * px + py], dst_ref=outs[w].at[k], send_sem=send_sems.at[3 * w + k],
            recv_sem=recv_sems.at[3 * w + k], device_id=(px, py, c), device_id_type=MESH)
            for w in range(nw) for k, (px, py) in enumerate(chips)]

    shapes = [jax.ShapeDtypeStruct((3,) + p.shape[1:], p.dtype) for p in parts]
    return _Side(parts, shapes, 3 * nw, _exchange(copies))


def _share_side(halves):
    nw = len(halves)

    def copies(ins, outs, send_sems, recv_sems):
        x, y, c, _ = _place()
        return [pltpu.make_async_remote_copy(
            src_ref=ins[w], dst_ref=outs[w].at[c], send_sem=send_sems.at[w], recv_sem=recv_sems.at[w],
            device_id=(x, y, 1 - c), device_id_type=MESH) for w in range(nw)]

    return _Side(halves, [jax.ShapeDtypeStruct((2,) + h.shape, F32) for h in halves], nw, _exchange(copies))


EW_BLOCK = 512 * 1024


def _ew_tile(rows, cols):
    limit = max(8, EW_BLOCK // cols)
    return max(d for d in range(8, min(rows, limit) + 1, 8) if rows % d == 0)


def _indexed_sum(name, fn, grid, in_specs, out_specs, out_shape, index, arrays):
    def kern(_, *refs):
        n_in = len(in_specs)
        vals = fn(*[r[...] for r in refs[:n_in]])
        for r, v in zip(refs[n_in:], vals):
            r[...] = v.astype(r.dtype)

    return pl.pallas_call(
        kern, name=name, out_shape=out_shape,
        grid_spec=pltpu.PrefetchScalarGridSpec(num_scalar_prefetch=1, grid=grid, in_specs=in_specs, out_specs=out_specs),
        compiler_params=_params(("parallel",) * len(grid)),
    )(index.astype(jnp.int32).reshape(1), *arrays)


def _pair_sums(names, grads, theirs):
    parts, parts_bf = {}, {}
    for n, other in zip(names, theirs):
        _, rr, cc = other.shape
        tile = _ew_tile(rr, cc)
        nb = rr // tile
        if COL_SHARDED[n]:
            mine = pl.BlockSpec((tile, cc), lambda j, i, c: (c[0] * nb + i, j))
        else:
            mine = pl.BlockSpec((tile, cc), lambda j, i, c: ((2 * j + c[0]) * nb + i, 0))
        flat = pl.BlockSpec((tile, cc), lambda j, i, c: (j * nb + i, 0))
        p, pb = _indexed_sum(f"grad_pair_sum_{n}", lambda u, v: (u + v, u + v), (4, nb), [mine, flat], [flat, flat],
                             [jax.ShapeDtypeStruct((4 * rr, cc), F32), jax.ShapeDtypeStruct((4 * rr, cc), BF)],
                             lax.axis_index("c"), (grads[n], other.reshape(4 * rr, cc)))
        parts[n], parts_bf[n] = p.reshape(4, rr, cc), pb.reshape(4, rr, cc)
    return parts, parts_bf


def _chip_sums(names, parts, landed):
    halves = {}
    for n, got in zip(names, landed):
        _, rr, cc = got.shape
        tile = _ew_tile(rr, cc)
        nb = rr // tile
        own = pl.BlockSpec((tile, cc), lambda i, chip: (chip[0] * nb + i, 0))
        peer = lambda k: pl.BlockSpec((tile, cc), lambda i, chip: (k * nb + i, 0))
        halves[n] = _indexed_sum(f"grad_chip_sum_{n}", lambda o, k0, k1, k2: (((o + k0) + k1) + k2,), (nb,),
                                 [own, peer(0), peer(1), peer(2)], [pl.BlockSpec((tile, cc), lambda i, chip: (i, 0))],
                                 [jax.ShapeDtypeStruct((rr, cc), F32)], 2 * lax.axis_index("x") + lax.axis_index("y"),
                                 (parts[n].reshape(4 * rr, cc),) + (got.reshape(3 * rr, cc),) * 3)[0]
    return halves


def _both_halves(mine, shared):
    both = lax.dynamic_update_index_in_dim(shared, mine, lax.axis_index("c"), 0)
    return both.reshape(2 * mine.shape[0], mine.shape[1])


BIG = ("w_in", "w_attn_proj", "w_conv_proj", "w_out", "w_cq", "w_ckv", "w_co", "w_up", "w_down")
COL_SHARDED = {"w_in": True, "w_attn_proj": True, "w_conv_proj": True, "w_out": False, "w_cq": False,
               "w_ckv": True, "w_co": False, "w_up": True, "w_down": False}
SMALL = ("g_mix", "b_gate", "conv_b", "conv_ln_g", "conv_ln_b", "g_cross", "g_mem", "g_mlp", "g_final")
ORDER = ("g_mix", "w_in", "b_gate", "conv_w", "conv_b", "conv_ln_g", "conv_ln_b", "w_attn_proj", "w_conv_proj", "w_out",
         "g_cross", "g_mem", "w_cq", "w_ckv", "w_co", "g_mlp", "w_up", "w_down", "g_final")


def _pad_rows(flat, rows):
    return jnp.pad(flat, (0, rows * LANES - flat.shape[0])).reshape(rows, LANES)


WAVE_MLP = ("w_down", "w_up")
WAVE_MID = ("w_co", "w_cq", "w_ckv", "w_out", "w_attn_proj", "w_conv_proj")


def _local_step(x, mem, tgt, shards, small, conv_w_full):
    t = x.shape[0]
    tr = 512
    c_tab, s_tab = _rope_tables(t)
    row = lambda v: v.reshape(1, -1)
    g_mix, g_cross, g_mem, g_mlp, g_final = (row(small[n]) for n in ("g_mix", "g_cross", "g_mem", "g_mlp", "g_final"))
    b_gate, conv_b, ln_g, ln_b = (row(small[n]) for n in ("b_gate", "conv_b", "conv_ln_g", "conv_ln_b"))
    w32 = jnp.pad(conv_w_full, ((0, CONV_KP - CONV_K), (0, 0)))

    (u,), (w_in_all,) = _rowcall("mix_norm", lambda a, g: (_rms(a, g),), t, tr, [(x, D_MODEL, 0)], [g_mix], [(D_MODEL, BF)],
                                 side=_gather_side([shards["w_in"]]))
    wfull = {"w_in": w_in_all}

    def keep(names, gathered):
        for n, g in zip(names, gathered):
            wfull[n] = g if COL_SHARDED[n] else g.reshape(1, 4 * g.shape[1], g.shape[2])

    z, gathered = _mm("in_proj", u, w_in_all, out_dtypes=(BF,), tm=min(2048, t),
                      side=_gather_side([shards[n] for n in WAVE_MID]))
    keep(WAVE_MID, gathered)
    (c1,), gathered = _conv_fwd(z, w32, conv_b, t, side=_gather_side([shards[n] for n in WAVE_MLP]))
    keep(WAVE_MLP, gathered)
    qkv = _rope_split(z, c_tab, s_tab, t)
    qkv_cls, outs, lses = [], [], []
    for g, dil in enumerate(DILATIONS):
        ops = tuple((a.reshape(t, GROUP_W), 0) for a in qkv[3 * g:3 * g + 3])
        qkv_cls.append(ops)
        o_g, l_g = _attn_fwd(f"attn_fwd_{g}", *ops, t, dil)
        outs.append(o_g)
        lses.append(l_g)
    attn, lse = _merge_classes(outs, lses, t)
    y_attn = _mm("attn_proj", attn, wfull["w_attn_proj"])
    (c2,) = _rowcall("conv_ln_silu", _ln_silu_fn, t, tr, [(c1, CONV_CH, 0)], [ln_g, ln_b], [(CONV_CH, BF)])
    tn_cp = wfull["w_conv_proj"].shape[2]
    merged, y_conv = _mm("conv_proj", c2, wfull["w_conv_proj"], epi=_gate_epi, out_dtypes=(BF, F32),
                         extras=(y_attn, (z, GATE_A_COL // tn_cp), (z, GATE_B_COL // tn_cp)),
                         params=((b_gate, 0), (b_gate, D_MODEL // tn_cp)))
    x1, uq = _mm("out_proj", merged, wfull["w_out"], extras=(x,), params=(g_cross,), epi=_res_norm_epi, out_dtypes=(F32, BF))
    (mn,) = _rowcall("mem_norm", lambda a, g: (_rms(a, g),), N_MEM, N_MEM, [(mem, D_MODEL, 0)], [g_mem], [(D_MODEL, BF)])
    cq = _mm("cross_q", uq, wfull["w_cq"], out_dtypes=(BF,))
    ckv = _mm("cross_kv", mn, wfull["w_ckv"], out_dtypes=(BF,))
    co = _cross_fwd(cq, ckv, t)
    x2, um = _mm("cross_out", co, wfull["w_co"], extras=(x1,), params=(g_mlp,), epi=_res_norm_epi, out_dtypes=(F32, BF))
    hact = _mm("mlp_up", um, wfull["w_up"], out_dtypes=(BF,), tm=min(2048, t),
               epi=lambda acc: (jnp.square(jnp.maximum(acc, 0.0)),))
    d3, d3b, loss_row, dg_final = _mm("mlp_down", hact, wfull["w_down"], extras=(x2, tgt), params=(g_final,), epi=_final_epi,
                                      out_dtypes=(F32, BF), sums=[(1, LANES), (1, D_MODEL)])

    gw = {}
    dhp = _mm("mlp_down_bwd", d3b, wfull["w_down"], nt=True, extras=(hact,), out_dtypes=(BF,),
              epi=lambda acc, h: (acc * 2.0 * jnp.sqrt(h.astype(F32)),))
    gw["w_down"] = _mm_tn("mlp_down_wgrad", hact, d3b)
    gw["w_up"] = _mm_tn("mlp_up_wgrad", um, dhp)

    def swap_of(names):
        return _swap_side([gw[n] for n in names], [COL_SHARDED[n] for n in names])

    (d2, d2b, dg_mlp), theirs = _mm("mlp_up_bwd", dhp, wfull["w_up"], nt=True, extras=(x2, d3), params=(g_mlp,),
                                    epi=_norm_bwd_epi, out_dtypes=(F32, BF), sums=[(1, D_MODEL)], side=swap_of(WAVE_MLP))
    parts_mlp, parts_bf_mlp = _pair_sums(WAVE_MLP, gw, theirs)

    dco = _mm("cross_out_bwd", d2b, wfull["w_co"], nt=True, out_dtypes=(BF,))
    gw["w_co"] = _mm_tn("cross_out_wgrad", co, d2b)
    dcq, dckv = _cross_bwd(cq, ckv, dco, t)
    gw["w_cq"] = _mm_tn("cross_q_wgrad", uq, dcq)
    d1, d1b, dg_cross = _mm("cross_q_bwd", dcq, wfull["w_cq"], nt=True, extras=(x1, d2), params=(g_cross,), epi=_norm_bwd_epi,
                            out_dtypes=(F32, BF), sums=[(1, D_MODEL)])
    gw["w_ckv"] = _mm_tn("cross_kv_wgrad", mn, dckv, tk=N_MEM)
    dmn = _mm("cross_kv_bwd", dckv, wfull["w_ckv"], nt=True)
    (dg_mem,) = _rowcall("mem_norm_bwd", lambda a, dn, g: (_rms_bwd(a, g, dn)[1],), N_MEM, N_MEM,
                         [(mem, D_MODEL, 0), (dmn, D_MODEL, 0)], [g_mem], [], accs=[(1, D_MODEL)])

    dya, dyc, dgate_a, dgate_b, dbg_a, dbg_b = _mm(
        "out_proj_bwd", d1b, wfull["w_out"], nt=True, tm=min(512, t), epi=_gate_bwd_epi, out_dtypes=(BF,) * 4,
        extras=(y_attn, y_conv, (z, GATE_A_COL // D_MODEL), (z, GATE_B_COL // D_MODEL)), params=(b_gate,),
        sums=[(1, D_MODEL), (1, D_MODEL)])
    dbg = jnp.concatenate([dbg_a, dbg_b], axis=1)
    gw["w_out"] = _mm_tn("out_proj_wgrad", merged, d1b)
    gw["w_attn_proj"] = _mm_tn("attn_proj_wgrad", attn, dya)
    dattn = _mm("attn_proj_bwd", dya, wfull["w_attn_proj"], nt=True)
    gw["w_conv_proj"] = _mm_tn("conv_proj_wgrad", c2, dyc)
    dc2 = _mm("conv_proj_bwd", dyc, wfull["w_conv_proj"], nt=True)
    (dc1, dlng, dlnb), theirs = _rowcall("conv_ln_silu_bwd", _ln_silu_bwd_fn, t, tr, [(c1, CONV_CH, 0), (dc2, CONV_CH, 0)],
                                         [ln_g, ln_b], [(CONV_CH, F32)], accs=[(1, CONV_CH), (1, CONV_CH)],
                                         side=swap_of(WAVE_MID))
    parts_mid, parts_bf_mid = _pair_sums(WAVE_MID, gw, theirs)
    (dglu_a, dglu_b, dconv), landed = _conv_bwd(z, dc1, w32, t, side=_scatter_side([parts_bf_mlp[n] for n in WAVE_MLP]))
    halves_mlp = _chip_sums(WAVE_MLP, parts_mlp, landed)

    dl0, cls1, cls2 = _attn_bwd_prep(dattn, attn, lse, t)
    dqs, dks, dvs = [], [], []
    for g, (dil, (da_c, dl_c, ls_c)) in enumerate(zip(DILATIONS, ((dattn, dl0, lse), cls1, cls2))):
        res = _attn_bwd(f"attn_bwd_{g}", *qkv_cls[g], da_c, dl_c, ls_c, t, dil,
                        side=_share_side([halves_mlp[n] for n in WAVE_MLP]) if g == 0 else None)
        if g == 0:
            res, others = res
            gshard = {n: _both_halves(halves_mlp[n], o) for n, o in zip(WAVE_MLP, others)}
        dqs.append(res[0])
        dks.append(res[1])
        dvs.append(res[2])
    (dz,), landed = _rope_bwd_join(dqs, dks, dvs, c_tab, s_tab, (dglu_a, dglu_b, dgate_a, dgate_b), t,
                                   side=_scatter_side([parts_bf_mid[n] for n in WAVE_MID]))
    halves_mid = _chip_sums(WAVE_MID, parts_mid, landed)

    gw_in_full, others = _mm_tn("in_proj_wgrad", u, dz, side=_share_side([halves_mid[n] for n in WAVE_MID]))
    gshard.update({n: _both_halves(halves_mid[n], o) for n, o in zip(WAVE_MID, others)})
    gw_in = {"w_in": gw_in_full}
    du_a, theirs = _mm("in_proj_bwd_a", dz, wfull["w_in"], nt=True, k_part=(0, 2), side=_swap_side([gw_in["w_in"]], [True]))
    parts, parts_bf = _pair_sums(("w_in",), gw_in, theirs)
    (gx, dg_mix), landed_in = _mm("in_proj_bwd_b", dz, wfull["w_in"], nt=True, k_part=(1, 2), extras=(du_a, x, d1),
                                  params=(g_mix,), epi=lambda acc, first, a, dres, g: _norm_bwd_epi(acc + first, a, dres, g)[1:],
                                  sums=[(1, D_MODEL)], side=_scatter_side([parts_bf["w_in"]]))
    halves = _chip_sums(("w_in",), parts, landed_in)

    gsmall = {"g_mix": dg_mix, "b_gate": dbg, "conv_b": dconv[CONV_K:CONV_K + 1], "conv_ln_g": dlng, "conv_ln_b": dlnb,
              "g_cross": dg_cross, "g_mem": dg_mem, "g_mlp": dg_mlp, "g_final": dg_final, "conv_w": dconv[:CONV_K]}
    return loss_row, gx, gshard, gsmall, halves["w_in"]


def kernel(x, mem, g_mix, w_in, b_gate, conv_w, conv_b, conv_ln_g, conv_ln_b, w_attn_proj, w_conv_proj, w_out, g_cross, g_mem, w_cq, w_ckv, w_co, g_mlp, w_up, w_down, g_final, loss_target, m_g_mix, m_w_in, m_b_gate, m_conv_w, m_conv_b, m_conv_ln_g, m_conv_ln_b, m_w_attn_proj, m_w_conv_proj, m_w_out, m_g_cross, m_g_mem, m_w_cq, m_w_ckv, m_w_co, m_g_mlp, m_w_up, m_w_down, m_g_final, v_g_mix, v_w_in, v_b_gate, v_conv_w, v_conv_b, v_conv_ln_g, v_conv_ln_b, v_w_attn_proj, v_w_conv_proj, v_w_out, v_g_cross, v_g_mem, v_w_cq, v_w_ckv, v_w_co, v_g_mlp, v_w_up, v_w_down, v_g_final):
    w = dict(g_mix=g_mix, w_in=w_in, b_gate=b_gate, conv_w=conv_w, conv_b=conv_b, conv_ln_g=conv_ln_g, conv_ln_b=conv_ln_b,
             w_attn_proj=w_attn_proj, w_conv_proj=w_conv_proj, w_out=w_out, g_cross=g_cross, g_mem=g_mem, w_cq=w_cq,
             w_ckv=w_ckv, w_co=w_co, g_mlp=g_mlp, w_up=w_up, w_down=w_down, g_final=g_final)
    mo = dict(g_mix=m_g_mix, w_in=m_w_in, b_gate=m_b_gate, conv_w=m_conv_w, conv_b=m_conv_b, conv_ln_g=m_conv_ln_g,
              conv_ln_b=m_conv_ln_b, w_attn_proj=m_w_attn_proj, w_conv_proj=m_w_conv_proj, w_out=m_w_out, g_cross=m_g_cross,
              g_mem=m_g_mem, w_cq=m_w_cq, w_ckv=m_w_ckv, w_co=m_w_co, g_mlp=m_g_mlp, w_up=m_w_up, w_down=m_w_down,
              g_final=m_g_final)
    vo = dict(g_mix=v_g_mix, w_in=v_w_in, b_gate=v_b_gate, conv_w=v_conv_w, conv_b=v_conv_b, conv_ln_g=v_conv_ln_g,
              conv_ln_b=v_conv_ln_b, w_attn_proj=v_w_attn_proj, w_conv_proj=v_w_conv_proj, w_out=v_w_out, g_cross=v_g_cross,
              g_mem=v_g_mem, w_cq=v_w_cq, w_ckv=v_w_ckv, w_co=v_w_co, g_mlp=v_g_mlp, w_up=v_w_up, w_down=v_w_down,
              g_final=v_g_final)
    shapes = {n: w[n].shape for n in ORDER}
    two_d = lambda a: a.reshape(a.shape[-2], a.shape[-1])
    chip = 2 * lax.axis_index("x") + lax.axis_index("y")

    shards = {n: two_d(w[n]).astype(BF) for n in BIG}
    cw_rows = 48
    cw_all, _ = _gather8("gather_conv_w", _pad_rows(conv_w.reshape(-1), cw_rows))
    cw_shard = CONV_K * (CONV_CH // 4)
    conv_w_full = jnp.concatenate(
        [cw_all[2 * j].reshape(-1)[:cw_shard].reshape(CONV_K, CONV_CH // 4) for j in range(4)], axis=1)

    small = {n: w[n] for n in SMALL}
    loss_row, gx, gshard, gsmall, half_w_in = _local_step(two_d(x), two_d(mem), two_d(loss_target), shards, small,
                                                          conv_w_full)
    loss = lax.psum(loss_row[0, 0], ("x", "y", "c"))

    small_names = SMALL + ("conv_w",)
    flat = jnp.concatenate([gsmall[n].reshape(-1) for n in small_names])
    sm_rows = -(-flat.shape[0] // (8 * LANES)) * 8
    (_, sm_sum), others = _gather8("reduce_small_grads", _pad_rows(flat, sm_rows), side=_share_side([half_w_in]))
    gshard["w_in"] = _both_halves(half_w_in, others[0])
    sm_sum = sm_sum.reshape(-1)
    off = 0
    for n in small_names:
        size = gsmall[n].size
        gshard[n] = sm_sum[off:off + size].reshape(gsmall[n].shape)
        off += size
    gshard["conv_w"] = lax.dynamic_slice_in_dim(gshard["conv_w"], chip * (CONV_CH // 4), CONV_CH // 4, axis=1)

    grads, deltas, new_m, new_v = {}, {}, {}, {}
    for n in BIG:
        d, m2, v2 = _adam(f"adamw_{n}", two_d(w[n]), gshard[n], two_d(mo[n]), two_d(vo[n]))
        grads[n], deltas[n], new_m[n], new_v[n] = (a.reshape(shapes[n]) for a in (gshard[n], d, m2, v2))
    pack = lambda src: jnp.concatenate([src[n].reshape(-1) for n in small_names])
    n_small = sum(w[n].size for n in small_names)
    ad_rows = -(-n_small // (8 * LANES)) * 8
    d, m2, v2 = _adam("adamw_small", *[_pad_rows(pack(src), ad_rows) for src in (w, gshard, mo, vo)])
    off = 0
    for n in small_names:
        size = w[n].size
        grads[n] = gshard[n].reshape(shapes[n])
        deltas[n], new_m[n], new_v[n] = (a.reshape(-1)[off:off + size].reshape(shapes[n]) for a in (d, m2, v2))
        off += size

    return (loss, gx.reshape(x.shape), *[grads[n] for n in ORDER], *[deltas[n] for n in ORDER],
            *[new_m[n] for n in ORDER], *[new_v[n] for n in ORDER])
```

```python
import functools

import jax
import jax.numpy as jnp
from jax import lax
from jax.experimental import pallas as pl
from jax.experimental.pallas import tpu as pltpu

F32 = jnp.float32
BF = jnp.bfloat16

D_MODEL = 1024
N_MEM = 256
HEAD_DIM = 128
HEADS_PER_GROUP = 4
DILATIONS = (1, 4, 16)
BLK = 128
GROUP_W = HEADS_PER_GROUP * HEAD_DIM
ATTN_WIDTH = 3 * GROUP_W
ROT_DIM = 32
ROPE_THETA = 500000.0
CONV_CH = 768
CONV_K = 31
CONV_KP = 32
IN_WIDTH = 8192
CROSS_HEADS = 4
CROSS_HEAD_DIM = 256
D_FF = 4096
EPS = 1e-6
ATTN_SCALE = HEAD_DIM ** -0.5
CROSS_SCALE = CROSS_HEAD_DIM ** -0.5
NEG = -1e30

ADAM_LR = 0.001
ADAM_B1 = 0.9
ADAM_B2 = 0.999
ADAM_EPS = 1e-08
ADAM_WD = 0.01
ADAM_STEP = 10

LANES = 128
SUBLANES = 8
VMEM_LIMIT = 56 * 1024 * 1024
MESH = pl.DeviceIdType.MESH
ANY = pl.BlockSpec(memory_space=pl.ANY)

GLU_A_COL = 3 * ATTN_WIDTH
GLU_B_COL = GLU_A_COL + CONV_CH
GATE_A_COL = GLU_B_COL + CONV_CH
GATE_B_COL = GATE_A_COL + D_MODEL


def _params(sem=None):
    return pltpu.CompilerParams(dimension_semantics=sem, vmem_limit_bytes=VMEM_LIMIT)


def _dot(a, b):
    return lax.dot_general(a, b, (((1,), (0,)), ((), ())), preferred_element_type=F32)


def _dot_nt(a, b):
    return lax.dot_general(a, b, (((1,), (1,)), ((), ())), preferred_element_type=F32)


def _dot_tn(a, b):
    return lax.dot_general(a, b, (((0,), (0,)), ((), ())), preferred_element_type=F32)


def _sig(x):
    return 1.0 / (1.0 + jnp.exp(-x))


def _glu(a, b):
    return a.astype(F32) * _sig(b.astype(F32))


class _Side:
    def __init__(self, arrays, out_shapes, n_sems, build, aliases=None):
        self.arrays, self.out_shapes, self.n_sems, self.build = list(arrays), list(out_shapes), n_sems, build
        self.aliases = aliases or {}


def _pcall(name, kern, grid, in_specs, out_specs, out_shape, scratch_shapes, sem, args, side=None):
    in_specs, out_specs, out_shape, scratch_shapes = list(in_specs), list(out_specs), list(out_shape), list(scratch_shapes)
    if side is None:
        return pl.pallas_call(kern, name=name, grid=grid, in_specs=in_specs, out_specs=out_specs, out_shape=out_shape,
                              scratch_shapes=scratch_shapes, compiler_params=_params(sem))(*args)
    ni, no, nsc = len(in_specs), len(out_specs), len(scratch_shapes)
    nsi, nso = len(side.arrays), len(side.out_shapes)

    def wrapped(*refs):
        ins, side_ins = refs[:ni], refs[ni:ni + nsi]
        outs, side_outs = refs[ni + nsi:ni + nsi + no], refs[ni + nsi + no:ni + nsi + no + nso]
        scratch = refs[ni + nsi + no + nso:ni + nsi + no + nso + nsc]
        send_sems, recv_sems = refs[-2:]
        start, finish = side.build(side_ins, side_outs, send_sems, recv_sems)
        if grid:
            first = functools.reduce(jnp.logical_and, [pl.program_id(a) == 0 for a in range(len(grid))])
            last = functools.reduce(jnp.logical_and, [pl.program_id(a) == g - 1 for a, g in enumerate(grid)])
            pl.when(first)(start)
            kern(*ins, *outs, *scratch)
            pl.when(last)(finish)
        else:
            start()
            kern(*ins, *outs, *scratch)
            finish()

    res = pl.pallas_call(
        wrapped, name=name, grid=grid, in_specs=in_specs + [ANY] * nsi, out_specs=out_specs + [ANY] * nso,
        out_shape=out_shape + side.out_shapes,
        scratch_shapes=scratch_shapes + [pltpu.SemaphoreType.DMA((side.n_sems,)), pltpu.SemaphoreType.DMA((side.n_sems,))],
        input_output_aliases={ni + k: no + v for k, v in side.aliases.items()},
        compiler_params=_params(("arbitrary",) * len(grid) if grid else None),
    )(*args, *side.arrays)
    return res[:no], res[no:]


def _rowcall(name, fn, n_rows, tile, ins, params, outs, accs=(), side=None):
    tile = min(tile, n_rows)
    ni, npar, no, na = len(ins), len(params), len(outs), len(accs)

    def kern(*refs):
        in_refs = refs[:ni + npar]
        o_refs = refs[ni + npar:ni + npar + no]
        a_refs = refs[ni + npar + no:]
        vals = fn(*[r[...] for r in in_refs])
        for r, v in zip(o_refs, vals[:no]):
            r[...] = v.astype(r.dtype)
        if na:
            @pl.when(pl.program_id(0) == 0)
            def _():
                for r in a_refs:
                    r[...] = jnp.zeros_like(r)
            for r, v in zip(a_refs, vals[no:]):
                r[...] += v

    in_specs = []
    arrays = []
    for spec in ins:
        arr, width, cb = spec[0], spec[1], spec[2]
        rb = spec[3] if len(spec) > 3 else 0
        in_specs.append(pl.BlockSpec((tile, width), functools.partial(lambda i, cb, rb: (i + rb, cb), cb=cb, rb=rb)))
        arrays.append(arr)
    for p in params:
        in_specs.append(pl.BlockSpec(p.shape, lambda i: (0, 0)))
        arrays.append(p)
    out_specs = [pl.BlockSpec((tile, w), lambda i: (i, 0)) for w, _ in outs]
    out_specs += [pl.BlockSpec(s, lambda i: (0, 0)) for s in accs]
    out_shape = [jax.ShapeDtypeStruct((n_rows, w), dt) for w, dt in outs]
    out_shape += [jax.ShapeDtypeStruct(s, F32) for s in accs]
    return _pcall(name, kern, (n_rows // tile,), in_specs, out_specs, out_shape, [],
                  ("arbitrary",) if na else ("parallel",), arrays, side)


def _mm(name, a, w3, *, nt=False, extras=(), params=(), epi=None, out_dtypes=(F32,), sums=(), tm=None, tn=None, tk=None,
        k_part=(0, 1), side=None):
    m, ka = a.shape
    ns, r, cs = w3.shape
    if not nt:
        k_dim, n = r, ns * cs
        tn = tn or min(cs, 2048)
        tk = tk or min(k_dim, 1024)
    else:
        k_dim, n = ns * cs, r
        tn = tn or min(r, 2048)
        tk = tk or min(cs, 1024)
    assert ka == k_dim, (name, a.shape, w3.shape)
    assert not sums or tn == n, name
    nk = k_dim // tk // k_part[1]
    k0 = k_part[0] * nk
    if not nt:
        nbs = cs // tn
        w_spec = pl.BlockSpec((None, tk, tn), lambda i, j, k: (j // nbs, k + k0, j % nbs))
    else:
        kbs = cs // tk
        w_spec = pl.BlockSpec((None, tn, tk), lambda i, j, k: ((k + k0) // kbs, j, (k + k0) % kbs))
    tm = tm or min(m, 1024)
    ne, no, nsum = len(extras) + len(params), len(out_dtypes), len(sums)

    def kern(a_ref, w_ref, *rest):
        e_refs = rest[:ne]
        o_refs = rest[ne:ne + no]
        s_refs = rest[ne + no:ne + no + nsum]

        def part():
            av = a_ref[...].astype(BF)
            return _dot_nt(av, w_ref[...]) if nt else _dot(av, w_ref[...])

        def finish(res):
            vals = epi(res, *[e[...] for e in e_refs]) if epi else (res,)
            for o, v in zip(o_refs, vals[:no]):
                o[...] = v.astype(o.dtype)
            for sr, v in zip(s_refs, vals[no:]):
                sr[...] += v

        if nsum:
            @pl.when(jnp.logical_and(pl.program_id(0) == 0, pl.program_id(2) == 0))
            def _():
                for sr in s_refs:
                    sr[...] = jnp.zeros_like(sr)

        if nk == 1:
            finish(part())
            return
        acc = rest[ne + no + nsum]
        k = pl.program_id(2)

        @pl.when(k == 0)
        def _():
            acc[...] = part()

        @pl.when(jnp.logical_and(k > 0, k < nk - 1))
        def _():
            acc[...] += part()

        @pl.when(k == nk - 1)
        def _():
            finish(acc[...] + part())

    in_specs = [pl.BlockSpec((tm, tk), lambda i, j, k: (i, k + k0)), w_spec]
    split = lambda items: [(it if isinstance(it, tuple) else (it, None)) for it in items]
    extras, params = split(extras), split(params)
    in_specs += [pl.BlockSpec((tm, tn), functools.partial(lambda i, j, k, off: (i, j + off), off=off or 0)) for _, off in extras]
    in_specs += [pl.BlockSpec(p.shape, lambda i, j, k: (0, 0)) if off is None else
                 pl.BlockSpec((p.shape[0], tn), functools.partial(lambda i, j, k, off: (0, j + off), off=off)) for p, off in params]
    extras, params = [e for e, _ in extras], [p for p, _ in params]
    out_specs = [pl.BlockSpec((tm, tn), lambda i, j, k: (i, j)) for _ in out_dtypes]
    out_specs += [pl.BlockSpec(sh, lambda i, j, k: (0, 0)) for sh in sums]
    out_shape = [jax.ShapeDtypeStruct((m, n), dt) for dt in out_dtypes] + [jax.ShapeDtypeStruct(sh, F32) for sh in sums]
    res = _pcall(name, kern, (m // tm, n // tn, nk), in_specs, out_specs, out_shape,
                 [pltpu.VMEM((tm, tn), F32)] if nk > 1 else [],
                 ("arbitrary",) * 3 if nsum else ("parallel", "parallel", "arbitrary"), (a, w3, *extras, *params), side)
    main = res[0] if side is not None else res
    main = main[0] if no + nsum == 1 else main
    return (main, res[1]) if side is not None else main


def _mm_tn(name, a, b, tm=None, tn=None, tk=None, side=None):
    t, ka = a.shape
    _, n = b.shape
    tm = tm or min(ka, 2048)
    tn = tn or min(n, 2048)
    tk = tk or min(t, 1024)

    def kern(a_ref, b_ref, o_ref):
        def part():
            return _dot_tn(a_ref[...].astype(BF), b_ref[...].astype(BF))

        @pl.when(pl.program_id(2) == 0)
        def _():
            o_ref[...] = part()

        @pl.when(pl.program_id(2) > 0)
        def _():
            o_ref[...] += part()

    res = _pcall(name, kern, (ka // tm, n // tn, t // tk),
                 [pl.BlockSpec((tk, tm), lambda i, j, k: (k, i)), pl.BlockSpec((tk, tn), lambda i, j, k: (k, j))],
                 [pl.BlockSpec((tm, tn), lambda i, j, k: (i, j))], [jax.ShapeDtypeStruct((ka, n), F32)], [],
                 ("parallel", "parallel", "arbitrary"), (a, b), side)
    return (res[0][0], res[1]) if side is not None else res[0]


def _rms(x, g):
    return x * lax.rsqrt(jnp.mean(x * x, axis=-1, keepdims=True) + EPS) * g


def _rms_bwd(x, g, dy):
    r = lax.rsqrt(jnp.mean(x * x, axis=-1, keepdims=True) + EPS)
    xh = x * r
    dxh = dy * g
    dx = r * (dxh - xh * jnp.mean(dxh * xh, axis=-1, keepdims=True))
    return dx, jnp.sum(dy * xh, axis=0, keepdims=True)


def _rot(t, c, s):
    lane = lax.broadcasted_iota(jnp.int32, t.shape, 1)
    swapped = jnp.where(lane < ROT_DIM // 2, pltpu.roll(t, HEAD_DIM - ROT_DIM // 2, 1), pltpu.roll(t, ROT_DIM // 2, 1))
    return t * c + swapped * s


def _rope_tables(t):
    half = ROT_DIM // 2
    pos = jnp.arange(t, dtype=F32)
    inv_freq = ROPE_THETA ** (-jnp.arange(0, ROT_DIM, 2, dtype=F32) / ROT_DIM)
    ang = pos[:, None] * inv_freq[None, :]
    cos, sin = jnp.cos(ang), jnp.sin(ang)
    ones = jnp.ones((t, HEAD_DIM - ROT_DIM), F32)
    c_tab = jnp.concatenate([cos, cos, ones], axis=1)
    s_tab = jnp.concatenate([-sin, sin, 0.0 * ones], axis=1)
    return c_tab, s_tab


def _merge_fn(o0, o1, o2, l0, l1, l2):
    m = jnp.maximum(jnp.maximum(l0, l1), l2)
    e0, e1, e2 = jnp.exp(l0 - m), jnp.exp(l1 - m), jnp.exp(l2 - m)
    tot = e0 + e1 + e2
    return (e0 * o0 + e1 * o1 + e2 * o2) / tot, m + jnp.log(tot)


def _ln_parts(c1):
    mu = jnp.mean(c1, axis=-1, keepdims=True)
    xc = c1 - mu
    r = lax.rsqrt(jnp.mean(xc * xc, axis=-1, keepdims=True) + EPS)
    return xc * r, r


def _ln_silu_fn(c1, g, b):
    xh, _ = _ln_parts(c1)
    yl = xh * g + b
    return (yl * _sig(yl),)


def _ln_silu_bwd_fn(c1, dout, g, b):
    xh, r = _ln_parts(c1)
    yl = xh * g + b
    s = _sig(yl)
    dyl = dout * (s + yl * s * (1.0 - s))
    dxh = dyl * g
    dx = r * (dxh - jnp.mean(dxh, axis=-1, keepdims=True) - xh * jnp.mean(dxh * xh, axis=-1, keepdims=True))
    return dx, jnp.sum(dyl * xh, axis=0, keepdims=True), jnp.sum(dyl, axis=0, keepdims=True)


def _gate_epi(yc, ya, ga, gb, ba, bb):
    return _sig(ga + ba) * ya + _sig(gb + bb) * yc, yc


def _gate_bwd_epi(dm, ya, yc, ga, gb, bg):
    sa = _sig(ga + bg[:, :D_MODEL])
    sb = _sig(gb + bg[:, D_MODEL:])
    dga = dm * ya * sa * (1.0 - sa)
    dgb = dm * yc * sb * (1.0 - sb)
    return dm * sa, dm * sb, dga, dgb, jnp.sum(dga, axis=0, keepdims=True), jnp.sum(dgb, axis=0, keepdims=True)


def _res_norm_epi(acc, res, g):
    xn = res + acc
    return xn, _rms(xn, g)


def _norm_bwd_epi(acc, a, dres, g):
    dx, dg = _rms_bwd(a, g, acc)
    return dres + dx, dres + dx, dg


def _final_epi(acc, res, tgt, g):
    return _final_fn(res + acc, tgt, g)


def _final_fn(x3, tgt, g):
    err = _rms(x3, g) - tgt
    lrow = jnp.sum(err * err, axis=-1, keepdims=True) * (0.5 / D_MODEL)
    lsum = jnp.sum(lrow, axis=0, keepdims=True)
    dx, dg = _rms_bwd(x3, g, err * (1.0 / D_MODEL))
    return dx, dx, jnp.broadcast_to(lsum, (1, LANES)), dg


def _attn_geometry(t, dil):
    cls = t // dil
    rows = min(8 * BLK, cls)
    return rows, rows // BLK, cls // rows


def _head_lanes(h):
    return slice(h * HEAD_DIM, (h + 1) * HEAD_DIM)


def _band_mask():
    row = lax.broadcasted_iota(jnp.int32, (BLK, 2 * BLK), 0)
    col = lax.broadcasted_iota(jnp.int32, (BLK, 2 * BLK), 1)
    return jnp.logical_and(col >= row, col <= row + BLK), col


def _stage_window(scr, halo_ref, cur_ref):
    scr[:BLK, :] = halo_ref[...]
    scr[BLK:, :] = cur_ref[...]


def _attn_fwd(name, q, k, v, t, dil):
    rows, nbk, spc = _attn_geometry(t, dil)

    def kern(q_ref, k_ref, kh_ref, v_ref, vh_ref, o_ref, l_ref, k_scr, v_scr):
        i = pl.program_id(0)
        first_shift = jnp.where(i % spc == 0, BLK, 0)
        _stage_window(k_scr, kh_ref, k_ref)
        _stage_window(v_scr, vh_ref, v_ref)
        band, col = _band_mask()
        band_first = jnp.logical_and(band, col >= first_shift)
        for h in range(HEADS_PER_GROUP):
            hs = _head_lanes(h)
            for b in range(nbk):
                rs, win = slice(b * BLK, (b + 1) * BLK), slice(b * BLK, (b + 2) * BLK)
                s = jnp.where(band_first if b == 0 else band, _dot_nt(q_ref[rs, hs], k_scr[win, hs]) * ATTN_SCALE, NEG)
                m = jnp.max(s, axis=1, keepdims=True)
                p = jnp.exp(s - m)
                tot = jnp.sum(p, axis=1, keepdims=True)
                o_ref[rs, hs] = _dot(p.astype(BF), v_scr[win, hs]) / tot
                l_ref[rs, hs] = jnp.broadcast_to(m + jnp.log(tot), (BLK, HEAD_DIM))

    def cur(cb):
        return pl.BlockSpec((rows, GROUP_W), lambda i: (i, cb))

    def halo(cb):
        return pl.BlockSpec((BLK, GROUP_W), lambda i: (jnp.maximum(i * nbk - 1, 0), cb))

    (qa, qc), (ka, kc_), (va, vc_) = q, k, v
    return _pcall(name, kern, (t // rows,), [cur(qc), cur(kc_), halo(kc_), cur(vc_), halo(vc_)],
                  [pl.BlockSpec((rows, GROUP_W), lambda i: (i, 0))] * 2, [jax.ShapeDtypeStruct((t, GROUP_W), F32)] * 2,
                  [pltpu.VMEM((rows + BLK, GROUP_W), BF)] * 2, ("parallel",), (qa, ka, ka, va, va))


def _attn_bwd(name, q, k, v, da, dl, lse, t, dil, side=None):
    rows, nbk, spc = _attn_geometry(t, dil)
    nblk = t // BLK

    def kern(q_ref, qn_ref, k_ref, kh_ref, v_ref, vh_ref, da_ref, dan_ref, dl_ref, dln_ref, ls_ref, lsn_ref,
             dq_ref, dk_ref, dv_ref, k_scr, v_scr):
        i = pl.program_id(0)
        first_shift = jnp.where(i % spc == 0, BLK, 0)
        next_shift = jnp.where((i + 1) % spc == 0, BLK, 0)
        _stage_window(k_scr, kh_ref, k_ref)
        _stage_window(v_scr, vh_ref, v_ref)
        band, col = _band_mask()
        band_first = jnp.logical_and(band, col >= first_shift)
        row1 = lax.broadcasted_iota(jnp.int32, (BLK, BLK), 0)
        col1 = lax.broadcasted_iota(jnp.int32, (BLK, BLK), 1)
        pend_k, pend_v = [None] * HEADS_PER_GROUP, [None] * HEADS_PER_GROUP
        for b in range(nbk):
            rs, win = slice(b * BLK, (b + 1) * BLK), slice(b * BLK, (b + 2) * BLK)
            for h in range(HEADS_PER_GROUP):
                hs = _head_lanes(h)
                qb, dab = q_ref[rs, hs], da_ref[rs, hs].astype(BF)
                kw, vw = k_scr[win, hs], v_scr[win, hs]
                p = jnp.where(band_first if b == 0 else band,
                              jnp.exp(_dot_nt(qb, kw) * ATTN_SCALE - ls_ref[rs, hs][:, :1]), 0.0)
                ds = (p * (_dot_nt(dab, vw) - dl_ref[rs, hs][:, :1]) * ATTN_SCALE).astype(BF)
                dq_ref[rs, hs] = _dot(ds, kw).astype(BF)
                dkw, dvw = _dot_tn(ds, qb), _dot_tn(p.astype(BF), dab)
                if b >= 1:
                    ps = slice((b - 1) * BLK, b * BLK)
                    dk_ref[ps, hs] = (pend_k[h] + dkw[:BLK]).astype(BF)
                    dv_ref[ps, hs] = (pend_v[h] + dvw[:BLK]).astype(BF)
                pend_k[h], pend_v[h] = dkw[BLK:], dvw[BLK:]
        ls_rows = slice((nbk - 1) * BLK, nbk * BLK)
        last = slice(nbk * BLK, (nbk + 1) * BLK)
        for h in range(HEADS_PER_GROUP):
            hs = _head_lanes(h)
            qb, dab = qn_ref[:, hs], dan_ref[:, hs].astype(BF)
            kp, vp = k_scr[last, hs], v_scr[last, hs]
            p = jnp.where(col1 >= row1 + next_shift, jnp.exp(_dot_nt(qb, kp) * ATTN_SCALE - lsn_ref[:, hs]), 0.0)
            ds = (p * (_dot_nt(dab, vp) - dln_ref[:, hs]) * ATTN_SCALE).astype(BF)
            dk_ref[ls_rows, hs] = (pend_k[h] + _dot_tn(ds, qb)).astype(BF)
            dv_ref[ls_rows, hs] = (pend_v[h] + _dot_tn(p.astype(BF), dab)).astype(BF)

    def cur(cb):
        return pl.BlockSpec((rows, GROUP_W), lambda i: (i, cb))

    def prev(cb):
        return pl.BlockSpec((BLK, GROUP_W), lambda i: (jnp.maximum(i * nbk - 1, 0), cb))

    def nxt(cb):
        return pl.BlockSpec((BLK, GROUP_W), lambda i: (jnp.minimum((i + 1) * nbk, nblk - 1), cb))

    (qa, qc), (ka, kc_), (va, vc_) = q, k, v
    return _pcall(name, kern, (t // rows,),
                  [cur(qc), nxt(qc), cur(kc_), prev(kc_), cur(vc_), prev(vc_), cur(0), nxt(0), cur(0), nxt(0), cur(0), nxt(0)],
                  [pl.BlockSpec((rows, GROUP_W), lambda i: (i, 0))] * 3, [jax.ShapeDtypeStruct((t, GROUP_W), BF)] * 3,
                  [pltpu.VMEM((rows + BLK, GROUP_W), BF)] * 2, ("parallel",),
                  (qa, qa, ka, ka, va, va, da, da, dl, dl, lse, lse), side)


CLS_TILE = 512


def _cls_block(t, tile, dil, dtype):
    if dil == 1:
        return pl.BlockSpec((tile, GROUP_W), lambda i: (i, 0)), jax.ShapeDtypeStruct((t, GROUP_W), dtype)
    return (pl.BlockSpec((dil, tile // dil, GROUP_W), lambda i: (0, i, 0)),
            jax.ShapeDtypeStruct((dil, t // dil, GROUP_W), dtype))


def _head_scratch(tile):
    return pltpu.VMEM((tile, HEAD_DIM), F32)


def _rope_split(z, c_tab, s_tab, t):
    tile = min(CLS_TILE, t)
    n_heads = ATTN_WIDTH // HEAD_DIM

    def kern(zq_ref, zk_ref, zv_ref, c_ref, s_ref, *rest):
        outs, scr = rest[:9], rest[9]
        c, s = c_ref[...], s_ref[...]
        for which, z_ref in enumerate((zq_ref, zk_ref, zv_ref)):
            for h in range(n_heads):
                g, hs = h // HEADS_PER_GROUP, _head_lanes(h % HEADS_PER_GROUP)
                val = z_ref[:, h * HEAD_DIM:(h + 1) * HEAD_DIM].astype(F32)
                if which < 2:
                    val = _rot(val, c, s)
                if g == 0:
                    outs[which][:, hs] = val.astype(BF)
                    continue
                scr[...] = val
                dil = DILATIONS[g]
                for r in range(dil):
                    outs[3 * g + which][r, :, hs] = scr[pl.ds(r, tile // dil, stride=dil), :].astype(BF)

    blocks = [_cls_block(t, tile, DILATIONS[g], BF) for g in range(3) for _ in range(3)]
    zspec = lambda cb: pl.BlockSpec((tile, ATTN_WIDTH), lambda i: (i, cb))
    tab = pl.BlockSpec((tile, HEAD_DIM), lambda i: (i, 0))
    return _pcall("rope", kern, (t // tile,), [zspec(0), zspec(1), zspec(2), tab, tab], [b[0] for b in blocks],
                  [b[1] for b in blocks], [_head_scratch(tile)], ("parallel",), (z, z, z, c_tab, s_tab))


def _merge_classes(outs, lses, t):
    tile = min(CLS_TILE, t)

    def kern(o0, l0, o1, l1, o2, l2, attn_ref, lse_ref, attn_bf_ref, s_o1, s_l1, s_o2, s_l2):
        for h in range(HEADS_PER_GROUP):
            hs = _head_lanes(h)
            for src, dst, dil in ((o1, s_o1, DILATIONS[1]), (l1, s_l1, DILATIONS[1]), (o2, s_o2, DILATIONS[2]),
                                  (l2, s_l2, DILATIONS[2])):
                for r in range(dil):
                    dst[pl.ds(r, tile // dil, stride=dil), :] = src[r, :, hs]
            merged, lse_tot = _merge_fn(o0[:, hs], s_o1[...], s_o2[...], l0[:, hs], s_l1[...], s_l2[...])
            attn_ref[:, hs], lse_ref[:, hs] = merged, lse_tot
            attn_bf_ref[:, hs] = merged.astype(BF)

    blocks = [_cls_block(t, tile, DILATIONS[g], F32) for g in range(3)]
    args = []
    for g in range(3):
        args += [outs[g].reshape(blocks[g][1].shape), lses[g].reshape(blocks[g][1].shape)]
    tok = pl.BlockSpec((tile, GROUP_W), lambda i: (i, 0))
    return _pcall("attn_merge", kern, (t // tile,), [blocks[g][0] for g in range(3) for _ in range(2)], [tok, tok, tok],
                  [jax.ShapeDtypeStruct((t, GROUP_W), F32)] * 2 + [jax.ShapeDtypeStruct((t, GROUP_W), BF)],
                  [_head_scratch(tile)] * 4, ("parallel",), args)


def _attn_bwd_prep(dattn, attn, lse, t):
    tile = min(CLS_TILE, t)

    def kern(da_ref, at_ref, ls_ref, dl0, da1, dl1, ls1, da2, dl2, ls2, s_da, s_dl, s_ls):
        for h in range(HEADS_PER_GROUP):
            hs = _head_lanes(h)
            da = da_ref[:, hs]
            s_da[...] = da
            s_dl[...] = jnp.broadcast_to(jnp.sum(da * at_ref[:, hs], axis=1, keepdims=True), (tile, HEAD_DIM))
            s_ls[...] = ls_ref[:, hs]
            dl0[:, hs] = s_dl[...]
            for oda, odl, ols, dil in ((da1, dl1, ls1, DILATIONS[1]), (da2, dl2, ls2, DILATIONS[2])):
                for r in range(dil):
                    rows = pl.ds(r, tile // dil, stride=dil)
                    oda[r, :, hs] = s_da[rows, :].astype(BF)
                    odl[r, :, hs] = s_dl[rows, :]
                    ols[r, :, hs] = s_ls[rows, :]

    tok = pl.BlockSpec((tile, GROUP_W), lambda i: (i, 0))
    blocks = [(tok, jax.ShapeDtypeStruct((t, GROUP_W), F32))]
    for g in (1, 2):
        blocks += [_cls_block(t, tile, DILATIONS[g], BF), _cls_block(t, tile, DILATIONS[g], F32),
                   _cls_block(t, tile, DILATIONS[g], F32)]
    res = _pcall("attn_bwd_prep", kern, (t // tile,), [tok, tok, tok], [b[0] for b in blocks], [b[1] for b in blocks],
                 [_head_scratch(tile)] * 3, ("parallel",), (dattn, attn, lse))
    flat = [a.reshape(t, GROUP_W) for a in res]
    return flat[0], flat[1:4], flat[4:7]


def _rope_bwd_join(dqs, dks, dvs, c_tab, s_tab, tail, t, side=None):
    tile = min(256, t)
    n_heads = ATTN_WIDTH // HEAD_DIM

    def kern(q0, q1, q2, k0, k1, k2, v0, v1, v2, c_ref, s_ref, ga_ref, gb_ref, gta_ref, gtb_ref, dz_ref, scr):
        c, s = c_ref[...], -s_ref[...]
        for which, srcs in enumerate(((q0, q1, q2), (k0, k1, k2), (v0, v1, v2))):
            for h in range(n_heads):
                g, hs = h // HEADS_PER_GROUP, _head_lanes(h % HEADS_PER_GROUP)
                if g == 0:
                    val = srcs[0][:, hs].astype(F32)
                else:
                    dil = DILATIONS[g]
                    for r in range(dil):
                        scr[pl.ds(r, tile // dil, stride=dil), :] = srcs[g][r, :, hs].astype(F32)
                    val = scr[...]
                if which < 2:
                    val = _rot(val, c, s)
                col = which * ATTN_WIDTH + h * HEAD_DIM
                dz_ref[:, col:col + HEAD_DIM] = val.astype(BF)
        dz_ref[:, GLU_A_COL:GLU_B_COL] = ga_ref[...]
        dz_ref[:, GLU_B_COL:GATE_A_COL] = gb_ref[...]
        dz_ref[:, GATE_A_COL:GATE_B_COL] = gta_ref[...]
        dz_ref[:, GATE_B_COL:] = gtb_ref[...]

    blocks = [_cls_block(t, tile, DILATIONS[g], BF) for g in range(3)]
    args = [a.reshape(blocks[g][1].shape) for grp in (dqs, dks, dvs) for g, a in enumerate(grp)]
    tab = pl.BlockSpec((tile, HEAD_DIM), lambda i: (i, 0))
    row = lambda w: pl.BlockSpec((tile, w), lambda i: (i, 0))
    return _pcall("rope_bwd", kern, (t // tile,), [blocks[g][0] for _ in range(3) for g in range(3)]
                  + [tab, tab, row(CONV_CH), row(CONV_CH), row(D_MODEL), row(D_MODEL)], [row(IN_WIDTH)],
                  [jax.ShapeDtypeStruct((t, IN_WIDTH), BF)], [_head_scratch(tile)], ("parallel",),
                  (*args, c_tab, s_tab, *tail), side)


def _cross_probs(qh, kh):
    s = _dot_nt(qh, kh) * CROSS_SCALE
    e = jnp.exp(s - jnp.max(s, axis=1, keepdims=True))
    return e, jnp.sum(e, axis=1, keepdims=True)


def _cross_fwd(cq, ckv, t):
    rows = min(512, t)

    def kern(q_ref, kv_ref, o_ref):
        for h in range(CROSS_HEADS):
            hs = slice(h * CROSS_HEAD_DIM, (h + 1) * CROSS_HEAD_DIM)
            vs = slice(D_MODEL + h * CROSS_HEAD_DIM, D_MODEL + (h + 1) * CROSS_HEAD_DIM)
            e, tot = _cross_probs(q_ref[:, hs], kv_ref[:, hs])
            o_ref[:, hs] = (_dot(e.astype(BF), kv_ref[:, vs]) / tot).astype(BF)

    return pl.pallas_call(
        kern, name="cross_fwd", grid=(t // rows,),
        in_specs=[pl.BlockSpec((rows, D_MODEL), lambda i: (i, 0)), pl.BlockSpec((N_MEM, 2 * D_MODEL), lambda i: (0, 0))],
        out_specs=pl.BlockSpec((rows, D_MODEL), lambda i: (i, 0)),
        out_shape=jax.ShapeDtypeStruct((t, D_MODEL), BF),
        compiler_params=_params(("parallel",)),
    )(cq, ckv)


def _cross_bwd(cq, ckv, dco, t):
    rows = min(512, t)

    def kern(q_ref, kv_ref, do_ref, dq_ref, dkv_ref):
        @pl.when(pl.program_id(0) == 0)
        def _():
            dkv_ref[...] = jnp.zeros_like(dkv_ref)
        for h in range(CROSS_HEADS):
            hs = slice(h * CROSS_HEAD_DIM, (h + 1) * CROSS_HEAD_DIM)
            vs = slice(D_MODEL + h * CROSS_HEAD_DIM, D_MODEL + (h + 1) * CROSS_HEAD_DIM)
            qh, kh, vh, doh = q_ref[:, hs], kv_ref[:, hs], kv_ref[:, vs], do_ref[:, hs]
            e, tot = _cross_probs(qh, kh)
            p = e / tot
            dp = _dot_nt(doh, vh)
            ds = (p * (dp - jnp.sum(p * dp, axis=1, keepdims=True)) * CROSS_SCALE).astype(BF)
            dq_ref[:, hs] = _dot(ds, kh).astype(BF)
            dkv_ref[:, hs] += _dot_tn(ds, qh)
            dkv_ref[:, vs] += _dot_tn(p.astype(BF), doh)

    return pl.pallas_call(
        kern, name="cross_bwd", grid=(t // rows,),
        in_specs=[pl.BlockSpec((rows, D_MODEL), lambda i: (i, 0)), pl.BlockSpec((N_MEM, 2 * D_MODEL), lambda i: (0, 0)),
                  pl.BlockSpec((rows, D_MODEL), lambda i: (i, 0))],
        out_specs=[pl.BlockSpec((rows, D_MODEL), lambda i: (i, 0)), pl.BlockSpec((N_MEM, 2 * D_MODEL), lambda i: (0, 0))],
        out_shape=[jax.ShapeDtypeStruct((t, D_MODEL), BF), jax.ShapeDtypeStruct((N_MEM, 2 * D_MODEL), F32)],
        compiler_params=_params(("arbitrary",)),
    )(cq, ckv, dco)


CONV_TILE = 512
CONV_CHUNK = 128
HALO = 32


def _conv_fwd(z, w32, bias, t, side=None):
    tile = min(CONV_TILE, t)
    a_cb, b_cb = GLU_A_COL // LANES, GLU_B_COL // LANES
    hb = tile // HALO

    def kern(a_ref, b_ref, ah_ref, bh_ref, w_ref, bias_ref, o_ref, g_scr):
        i = pl.program_id(1)
        g_scr[HALO:, :] = _glu(a_ref[...], b_ref[...])
        g_scr[:HALO, :] = _glu(ah_ref[...], bh_ref[...]) * jnp.where(i > 0, 1.0, 0.0)
        for c in range(tile // CONV_CHUNK):
            acc = jnp.broadcast_to(bias_ref[...], (CONV_CHUNK, LANES))
            for j in range(CONV_K):
                lo = c * CONV_CHUNK + HALO - (CONV_K - 1) + j
                acc = acc + w_ref[j:j + 1, :] * g_scr[lo:lo + CONV_CHUNK, :]
            o_ref[c * CONV_CHUNK:(c + 1) * CONV_CHUNK, :] = acc

    def cur(cb):
        return pl.BlockSpec((tile, LANES), lambda j, i: (i, cb + j))

    def prev(cb):
        return pl.BlockSpec((HALO, LANES), lambda j, i: (jnp.maximum(i * hb - 1, 0), cb + j))

    return _pcall("conv_fwd", kern, (CONV_CH // LANES, t // tile),
                  [cur(a_cb), cur(b_cb), prev(a_cb), prev(b_cb),
                   pl.BlockSpec((CONV_KP, LANES), lambda j, i: (0, j)), pl.BlockSpec((1, LANES), lambda j, i: (0, j))],
                  [pl.BlockSpec((tile, LANES), lambda j, i: (i, j))], [jax.ShapeDtypeStruct((t, CONV_CH), F32)],
                  [pltpu.VMEM((tile + HALO, LANES), F32)], ("parallel", "parallel"), (z, z, z, z, w32, bias), side)


def _conv_bwd(z, dc1, w32, t, side=None):
    tile = min(CONV_TILE, t)
    a_cb, b_cb = GLU_A_COL // LANES, GLU_B_COL // LANES
    hb = tile // HALO
    n_tiles = t // tile
    n_chunks = tile // CONV_CHUNK

    def kern(a_ref, b_ref, ah_ref, bh_ref, d_ref, dn_ref, w_ref, da_ref, db_ref, dw_ref, g_scr, d_scr):
        i = pl.program_id(1)
        g_scr[HALO:, :] = _glu(a_ref[...], b_ref[...])
        g_scr[:HALO, :] = _glu(ah_ref[...], bh_ref[...]) * jnp.where(i > 0, 1.0, 0.0)
        d_scr[:tile, :] = d_ref[...]
        d_scr[tile:, :] = dn_ref[...] * jnp.where(i < n_tiles - 1, 1.0, 0.0)

        @pl.when(i == 0)
        def _():
            dw_ref[...] = jnp.zeros_like(dw_ref)

        for c in range(n_chunks):
            cs = slice(c * CONV_CHUNK, (c + 1) * CONV_CHUNK)
            acc = jnp.zeros((CONV_CHUNK, LANES), F32)
            for j in range(CONV_K):
                lo = c * CONV_CHUNK + (CONV_K - 1) - j
                acc = acc + w_ref[j:j + 1, :] * d_scr[lo:lo + CONV_CHUNK, :]
            sgc = _sig(b_ref[cs, :].astype(F32))
            da_ref[cs, :] = (acc * sgc).astype(BF)
            db_ref[cs, :] = (acc * a_ref[cs, :].astype(F32) * sgc * (1.0 - sgc)).astype(BF)
        for j in range(CONV_K):
            tot = jnp.zeros((SUBLANES, LANES), F32)
            for c in range(n_chunks):
                lo = c * CONV_CHUNK + HALO - (CONV_K - 1) + j
                prod = d_ref[c * CONV_CHUNK:(c + 1) * CONV_CHUNK, :] * g_scr[lo:lo + CONV_CHUNK, :]
                tot = tot + jnp.sum(prod.reshape(CONV_CHUNK // SUBLANES, SUBLANES, LANES), axis=0)
            dw_ref[j:j + 1, :] += jnp.sum(tot, axis=0, keepdims=True)
        dw_ref[CONV_K:CONV_KP, :] += jnp.sum(d_ref[...], axis=0, keepdims=True)

    def cur(cb):
        return pl.BlockSpec((tile, LANES), lambda j, i: (i, cb + j))

    def prev(cb):
        return pl.BlockSpec((HALO, LANES), lambda j, i: (jnp.maximum(i * hb - 1, 0), cb + j))

    return _pcall(
        "conv_bwd", kern, (CONV_CH // LANES, n_tiles),
        [cur(a_cb), cur(b_cb), prev(a_cb), prev(b_cb), cur(0),
         pl.BlockSpec((HALO, LANES), lambda j, i: (jnp.minimum((i + 1) * hb, t // HALO - 1), j)),
         pl.BlockSpec((CONV_KP, LANES), lambda j, i: (0, j))],
        [pl.BlockSpec((tile, LANES), lambda j, i: (i, j)), pl.BlockSpec((tile, LANES), lambda j, i: (i, j)),
         pl.BlockSpec((CONV_KP, LANES), lambda j, i: (0, j))],
        [jax.ShapeDtypeStruct((t, CONV_CH), BF), jax.ShapeDtypeStruct((t, CONV_CH), BF),
         jax.ShapeDtypeStruct((CONV_KP, CONV_CH), F32)],
        [pltpu.VMEM((tile + HALO, LANES), F32), pltpu.VMEM((tile + HALO, LANES), F32)],
        ("parallel", "arbitrary"), (z, z, z, z, dc1, dc1, w32), side)


def _adam_fn(w, g, m, v):
    m = ADAM_B1 * m + (1.0 - ADAM_B1) * g
    v = ADAM_B2 * v + (1.0 - ADAM_B2) * (g * g)
    m_hat = m / (1.0 - ADAM_B1 ** ADAM_STEP)
    v_hat = v / (1.0 - ADAM_B2 ** ADAM_STEP)
    delta = -ADAM_LR * (m_hat / (jnp.sqrt(v_hat) + ADAM_EPS) + ADAM_WD * w)
    return delta, m, v


def _adam(name, w, g, m, v):
    rows, cols = w.shape
    tile = _ew_tile(rows, cols)
    return _rowcall(name, _adam_fn, rows, tile, [(a, cols, 0) for a in (w, g, m, v)], [], [(cols, F32)] * 3)


def _place():
    x, y, c = lax.axis_index("x"), lax.axis_index("y"), lax.axis_index("c")
    chips = [(1 - x, y), (x, 1 - y), (1 - x, 1 - y)]
    return x, y, c, chips


def _gather_side(shards):
    nw = len(shards)
    chip = 2 * lax.axis_index("x") + lax.axis_index("y")
    staged = [lax.dynamic_update_index_in_dim(jnp.zeros((4,) + s.shape, s.dtype), s, chip, 0) for s in shards]

    def build(_, outs, send_sems, recv_sems):
        x, y, c, chips = _place()
        me = 2 * x + y
        sibling = (x, y, 1 - c)

        def half(w, lead, h):
            n = shards[w].shape[0] // 2
            return outs[w].at[lead, pl.ds(h * n, n)]

        def copy(w, k, part, to):
            return pltpu.make_async_remote_copy(src_ref=part, dst_ref=part, send_sem=send_sems.at[6 * w + k],
                                                recv_sem=recv_sems.at[6 * w + k], device_id=to, device_id_type=MESH)

        def start():
            for w in range(nw):
                for k, (px, py) in enumerate(chips):
                    copy(w, k, half(w, me, c), (px, py, c)).start()

        def finish():
            for w in range(nw):
                for k, (px, py) in enumerate(chips):
                    landed = half(w, 2 * px + py, c)
                    copy(w, k, landed, (px, py, c)).wait_recv()
                    copy(w, 3 + k, landed, sibling).start()
            for w in range(nw):
                for k, (px, py) in enumerate(chips):
                    copy(w, 3 + k, half(w, 2 * px + py, 1 - c), sibling).wait_recv()
            for w in range(nw):
                for k, (px, py) in enumerate(chips):
                    copy(w, k, half(w, me, c), (px, py, c)).wait_send()
                    copy(w, 3 + k, half(w, 2 * px + py, c), sibling).wait_send()

        return start, finish

    return _Side(staged, [jax.ShapeDtypeStruct((4,) + s.shape, s.dtype) for s in shards], 6 * nw, build,
                 aliases={w: w for w in range(nw)})


def _gather8(name, v, side=None):
    rows = v.shape[0]

    def body(v_ref, all_ref, sum_ref, send_sems, recv_sems):
        x, y, c, _ = _place()
        me = 4 * x + 2 * y + c
        all_ref[me] = v_ref[...]
        copies = []
        for k in range(1, 8):
            px, py, pc = x ^ (k >> 2), y ^ ((k >> 1) & 1), c ^ (k & 1)
            copies.append(pltpu.make_async_remote_copy(
                src_ref=v_ref, dst_ref=all_ref.at[me], send_sem=send_sems.at[k - 1], recv_sem=recv_sems.at[k - 1],
                device_id=(px, py, pc), device_id_type=MESH))
            copies[-1].start()
        for k in range(1, 8):
            px, py, pc = x ^ (k >> 2), y ^ ((k >> 1) & 1), c ^ (k & 1)
            theirs = all_ref.at[4 * px + 2 * py + pc]
            pltpu.make_async_remote_copy(
                src_ref=theirs, dst_ref=theirs, send_sem=send_sems.at[k - 1], recv_sem=recv_sems.at[k - 1],
                device_id=(px, py, pc), device_id_type=MESH).wait_recv()
        for cp in copies:
            cp.wait_send()
        tot = all_ref[0]
        for d in range(1, 8):
            tot = tot + all_ref[d]
        sum_ref[...] = tot

    vm = pl.BlockSpec(memory_space=pltpu.VMEM)
    return _pcall(name, body, (), [vm], [vm, vm],
                  [jax.ShapeDtypeStruct((8, rows, LANES), F32), jax.ShapeDtypeStruct((rows, LANES), F32)],
                  [pltpu.SemaphoreType.DMA((7,)), pltpu.SemaphoreType.DMA((7,))], None, (v,), side)


def _region(ref, col_sharded, shape, j, h):
    r, ccols = shape
    if col_sharded:
        return ref.at[pl.ds(h * (r // 2), r // 2), pl.ds(j * (ccols // 4), ccols // 4)]
    n = r // 8
    return ref.at[pl.ds((2 * j + h) * n, n), :]


def _region_shape(col_sharded, shape):
    r, ccols = shape
    return (r // 2, ccols // 4) if col_sharded else (r // 8, ccols)


def _exchange(copies):
    def build(ins, outs, send_sems, recv_sems):
        def start():
            for cp in copies(ins, outs, send_sems, recv_sems):
                cp.start()

        def finish():
            for cp in copies(ins, outs, send_sems, recv_sems):
                cp.wait()

        return start, finish
    return build


def _swap_side(grads, kinds):
    nw = len(grads)

    def copies(ins, theirs, send_sems, recv_sems):
        x, y, c, _ = _place()
        return [pltpu.make_async_remote_copy(
            src_ref=_region(ins[w], kinds[w], grads[w].shape, j, 1 - c), dst_ref=theirs[w].at[j],
            send_sem=send_sems.at[4 * w + j], recv_sem=recv_sems.at[4 * w + j], device_id=(x, y, 1 - c), device_id_type=MESH)
            for w in range(nw) for j in range(4)]

    shapes = [jax.ShapeDtypeStruct((4,) + _region_shape(kinds[w], grads[w].shape), F32) for w in range(nw)]
    return _Side(grads, shapes, 4 * nw, _exchange(copies))


def _scatter_side(parts):
    nw = len(parts)

    def copies(ins, outs, send_sems, recv_sems):
        x, y, c, chips = _place()
        return [pltpu.make_async_remote_copy(
            src_ref=ins[w].at[2 * px + py], dst_ref=outs[w].at[k], send_sem=send_sems.at[3 * w + k],
            recv_sem=recv_sems.at[3 * w + k], device_id=(px, py, c), device_id_type=MESH)
            for w in range(nw) for k, (px, py) in enumerate(chips)]

    shapes = [jax.ShapeDtypeStruct((3,) + p.shape[1:], p.dtype) for p in parts]
    return _Side(parts, shapes, 3 * nw, _exchange(copies))


def _share_side(halves):
    nw = len(halves)

    def copies(ins, outs, send_sems, recv_sems):
        x, y, c, _ = _place()
        return [pltpu.make_async_remote_copy(
            src_ref=ins[w], dst_ref=outs[w].at[c], send_sem=send_sems.at[w], recv_sem=recv_sems.at[w],
            device_id=(x, y, 1 - c), device_id_type=MESH) for w in range(nw)]

    return _Side(halves, [jax.ShapeDtypeStruct((2,) + h.shape, F32) for h in halves], nw, _exchange(copies))


EW_BLOCK = 512 * 1024


def _ew_tile(rows, cols):
    limit = max(8, EW_BLOCK // cols)
    return max(d for d in range(8, min(rows, limit) + 1, 8) if rows % d == 0)


def _indexed_sum(name, fn, grid, in_specs, out_specs, out_shape, index, arrays):
    def kern(_, *refs):
        n_in = len(in_specs)
        vals = fn(*[r[...] for r in refs[:n_in]])
        for r, v in zip(refs[n_in:], vals):
            r[...] = v.astype(r.dtype)

    return pl.pallas_call(
        kern, name=name, out_shape=out_shape,
        grid_spec=pltpu.PrefetchScalarGridSpec(num_scalar_prefetch=1, grid=grid, in_specs=in_specs, out_specs=out_specs),
        compiler_params=_params(("parallel",) * len(grid)),
    )(index.astype(jnp.int32).reshape(1), *arrays)


def _pair_sums(names, grads, theirs):
    parts, parts_bf = {}, {}
    for n, other in zip(names, theirs):
        _, rr, cc = other.shape
        tile = _ew_tile(rr, cc)
        nb = rr // tile
        if COL_SHARDED[n]:
            mine = pl.BlockSpec((tile, cc), lambda j, i, c: (c[0] * nb + i, j))
        else:
            mine = pl.BlockSpec((tile, cc), lambda j, i, c: ((2 * j + c[0]) * nb + i, 0))
        flat = pl.BlockSpec((tile, cc), lambda j, i, c: (j * nb + i, 0))
        p, pb = _indexed_sum(f"grad_pair_sum_{n}", lambda u, v: (u + v, u + v), (4, nb), [mine, flat], [flat, flat],
                             [jax.ShapeDtypeStruct((4 * rr, cc), F32), jax.ShapeDtypeStruct((4 * rr, cc), BF)],
                             lax.axis_index("c"), (grads[n], other.reshape(4 * rr, cc)))
        parts[n], parts_bf[n] = p.reshape(4, rr, cc), pb.reshape(4, rr, cc)
    return parts, parts_bf


def _chip_sums(names, parts, landed):
    halves = {}
    for n, got in zip(names, landed):
        _, rr, cc = got.shape
        tile = _ew_tile(rr, cc)
        nb = rr // tile
        own = pl.BlockSpec((tile, cc), lambda i, chip: (chip[0] * nb + i, 0))
        peer = lambda k: pl.BlockSpec((tile, cc), lambda i, chip: (k * nb + i, 0))
        halves[n] = _indexed_sum(f"grad_chip_sum_{n}", lambda o, k0, k1, k2: (((o + k0) + k1) + k2,), (nb,),
                                 [own, peer(0), peer(1), peer(2)], [pl.BlockSpec((tile, cc), lambda i, chip: (i, 0))],
                                 [jax.ShapeDtypeStruct((rr, cc), F32)], 2 * lax.axis_index("x") + lax.axis_index("y"),
                                 (parts[n].reshape(4 * rr, cc),) + (got.reshape(3 * rr, cc),) * 3)[0]
    return halves


def _both_halves(mine, shared):
    both = lax.dynamic_update_index_in_dim(shared, mine, lax.axis_index("c"), 0)
    return both.reshape(2 * mine.shape[0], mine.shape[1])


BIG = ("w_in", "w_attn_proj", "w_conv_proj", "w_out", "w_cq", "w_ckv", "w_co", "w_up", "w_down")
COL_SHARDED = {"w_in": True, "w_attn_proj": True, "w_conv_proj": True, "w_out": False, "w_cq": False,
               "w_ckv": True, "w_co": False, "w_up": True, "w_down": False}
SMALL = ("g_mix", "b_gate", "conv_b", "conv_ln_g", "conv_ln_b", "g_cross", "g_mem", "g_mlp", "g_final")
ORDER = ("g_mix", "w_in", "b_gate", "conv_w", "conv_b", "conv_ln_g", "conv_ln_b", "w_attn_proj", "w_conv_proj", "w_out",
         "g_cross", "g_mem", "w_cq", "w_ckv", "w_co", "g_mlp", "w_up", "w_down", "g_final")


def _pad_rows(flat, rows):
    return jnp.pad(flat, (0, rows * LANES - flat.shape[0])).reshape(rows, LANES)


WAVE_MLP = ("w_down", "w_up")
WAVE_MID = ("w_co", "w_cq", "w_ckv", "w_out", "w_attn_proj", "w_conv_proj")


def _local_step(x, mem, tgt, shards, small, conv_w_full):
    t = x.shape[0]
    tr = 512
    c_tab, s_tab = _rope_tables(t)
    row = lambda v: v.reshape(1, -1)
    g_mix, g_cross, g_mem, g_mlp, g_final = (row(small[n]) for n in ("g_mix", "g_cross", "g_mem", "g_mlp", "g_final"))
    b_gate, conv_b, ln_g, ln_b = (row(small[n]) for n in ("b_gate", "conv_b", "conv_ln_g", "conv_ln_b"))
    w32 = jnp.pad(conv_w_full, ((0, CONV_KP - CONV_K), (0, 0)))

    (u,), (w_in_all,) = _rowcall("mix_norm", lambda a, g: (_rms(a, g),), t, tr, [(x, D_MODEL, 0)], [g_mix], [(D_MODEL, BF)],
                                 side=_gather_side([shards["w_in"]]))
    wfull = {"w_in": w_in_all}

    def keep(names, gathered):
        for n, g in zip(names, gathered):
            wfull[n] = g if COL_SHARDED[n] else g.reshape(1, 4 * g.shape[1], g.shape[2])

    z, gathered = _mm("in_proj", u, w_in_all, out_dtypes=(BF,), tm=min(2048, t),
                      side=_gather_side([shards[n] for n in WAVE_MID]))
    keep(WAVE_MID, gathered)
    (c1,), gathered = _conv_fwd(z, w32, conv_b, t, side=_gather_side([shards[n] for n in WAVE_MLP]))
    keep(WAVE_MLP, gathered)
    qkv = _rope_split(z, c_tab, s_tab, t)
    qkv_cls, outs, lses = [], [], []
    for g, dil in enumerate(DILATIONS):
        ops = tuple((a.reshape(t, GROUP_W), 0) for a in qkv[3 * g:3 * g + 3])
        qkv_cls.append(ops)
        o_g, l_g = _attn_fwd(f"attn_fwd_{g}", *ops, t, dil)
        outs.append(o_g)
        lses.append(l_g)
    attn, lse, attn_bf = _merge_classes(outs, lses, t)
    y_attn = _mm("attn_proj", attn_bf, wfull["w_attn_proj"])
    (c2,) = _rowcall("conv_ln_silu", _ln_silu_fn, t, tr, [(c1, CONV_CH, 0)], [ln_g, ln_b], [(CONV_CH, BF)])
    tn_cp = wfull["w_conv_proj"].shape[2]
    merged, y_conv = _mm("conv_proj", c2, wfull["w_conv_proj"], epi=_gate_epi, out_dtypes=(BF, F32),
                         extras=(y_attn, (z, GATE_A_COL // tn_cp), (z, GATE_B_COL // tn_cp)),
                         params=((b_gate, 0), (b_gate, D_MODEL // tn_cp)))
    x1, uq = _mm("out_proj", merged, wfull["w_out"], extras=(x,), params=(g_cross,), epi=_res_norm_epi, out_dtypes=(F32, BF))
    (mn,) = _rowcall("mem_norm", lambda a, g: (_rms(a, g),), N_MEM, N_MEM, [(mem, D_MODEL, 0)], [g_mem], [(D_MODEL, BF)])
    cq = _mm("cross_q", uq, wfull["w_cq"], out_dtypes=(BF,))
    ckv = _mm("cross_kv", mn, wfull["w_ckv"], out_dtypes=(BF,))
    co = _cross_fwd(cq, ckv, t)
    x2, um = _mm("cross_out", co, wfull["w_co"], extras=(x1,), params=(g_mlp,), epi=_res_norm_epi, out_dtypes=(F32, BF))
    hact = _mm("mlp_up", um, wfull["w_up"], out_dtypes=(BF,), tm=min(2048, t),
               epi=lambda acc: (jnp.square(jnp.maximum(acc, 0.0)),))
    d3, d3b, loss_row, dg_final = _mm("mlp_down", hact, wfull["w_down"], extras=(x2, tgt), params=(g_final,), epi=_final_epi,
                                      out_dtypes=(F32, BF), sums=[(1, LANES), (1, D_MODEL)])

    gw = {}
    dhp = _mm("mlp_down_bwd", d3b, wfull["w_down"], nt=True, extras=(hact,), out_dtypes=(BF,),
              epi=lambda acc, h: (acc * 2.0 * jnp.sqrt(h.astype(F32)),))
    gw["w_down"] = _mm_tn("mlp_down_wgrad", hact, d3b)
    gw["w_up"] = _mm_tn("mlp_up_wgrad", um, dhp)

    def swap_of(names):
        return _swap_side([gw[n] for n in names], [COL_SHARDED[n] for n in names])

    (d2, d2b, dg_mlp), theirs = _mm("mlp_up_bwd", dhp, wfull["w_up"], nt=True, extras=(x2, d3), params=(g_mlp,),
                                    epi=_norm_bwd_epi, out_dtypes=(F32, BF), sums=[(1, D_MODEL)], side=swap_of(WAVE_MLP))
    parts_mlp, parts_bf_mlp = _pair_sums(WAVE_MLP, gw, theirs)

    dco = _mm("cross_out_bwd", d2b, wfull["w_co"], nt=True, out_dtypes=(BF,))
    gw["w_co"] = _mm_tn("cross_out_wgrad", co, d2b)
    dcq, dckv = _cross_bwd(cq, ckv, dco, t)
    gw["w_cq"] = _mm_tn("cross_q_wgrad", uq, dcq)
    d1, d1b, dg_cross = _mm("cross_q_bwd", dcq, wfull["w_cq"], nt=True, extras=(x1, d2), params=(g_cross,), epi=_norm_bwd_epi,
                            out_dtypes=(F32, BF), sums=[(1, D_MODEL)])
    gw["w_ckv"] = _mm_tn("cross_kv_wgrad", mn, dckv, tk=N_MEM)
    dmn = _mm("cross_kv_bwd", dckv, wfull["w_ckv"], nt=True)
    (dg_mem,) = _rowcall("mem_norm_bwd", lambda a, dn, g: (_rms_bwd(a, g, dn)[1],), N_MEM, N_MEM,
                         [(mem, D_MODEL, 0), (dmn, D_MODEL, 0)], [g_mem], [], accs=[(1, D_MODEL)])

    dya, dyc, dgate_a, dgate_b, dbg_a, dbg_b = _mm(
        "out_proj_bwd", d1b, wfull["w_out"], nt=True, tm=min(512, t), epi=_gate_bwd_epi, out_dtypes=(BF,) * 4,
        extras=(y_attn, y_conv, (z, GATE_A_COL // D_MODEL), (z, GATE_B_COL // D_MODEL)), params=(b_gate,),
        sums=[(1, D_MODEL), (1, D_MODEL)])
    dbg = jnp.concatenate([dbg_a, dbg_b], axis=1)
    gw["w_out"] = _mm_tn("out_proj_wgrad", merged, d1b)
    gw["w_attn_proj"] = _mm_tn("attn_proj_wgrad", attn_bf, dya)
    dattn = _mm("attn_proj_bwd", dya, wfull["w_attn_proj"], nt=True)
    gw["w_conv_proj"] = _mm_tn("conv_proj_wgrad", c2, dyc)
    dc2 = _mm("conv_proj_bwd", dyc, wfull["w_conv_proj"], nt=True)
    (dc1, dlng, dlnb), theirs = _rowcall("conv_ln_silu_bwd", _ln_silu_bwd_fn, t, tr, [(c1, CONV_CH, 0), (dc2, CONV_CH, 0)],
                                         [ln_g, ln_b], [(CONV_CH, F32)], accs=[(1, CONV_CH), (1, CONV_CH)],
                                         side=swap_of(WAVE_MID))
    parts_mid, parts_bf_mid = _pair_sums(WAVE_MID, gw, theirs)
    (dglu_a, dglu_b, dconv), landed = _conv_bwd(z, dc1, w32, t, side=_scatter_side([parts_bf_mlp[n] for n in WAVE_MLP]))
    halves_mlp = _chip_sums(WAVE_MLP, parts_mlp, landed)

    dl0, cls1, cls2 = _attn_bwd_prep(dattn, attn, lse, t)
    dqs, dks, dvs = [], [], []
    for g, (dil, (da_c, dl_c, ls_c)) in enumerate(zip(DILATIONS, ((dattn, dl0, lse), cls1, cls2))):
        res = _attn_bwd(f"attn_bwd_{g}", *qkv_cls[g], da_c, dl_c, ls_c, t, dil,
                        side=_share_side([halves_mlp[n] for n in WAVE_MLP]) if g == 0 else None)
        if g == 0:
            res, others = res
            gshard = {n: _both_halves(halves_mlp[n], o) for n, o in zip(WAVE_MLP, others)}
        dqs.append(res[0])
        dks.append(res[1])
        dvs.append(res[2])
    (dz,), landed = _rope_bwd_join(dqs, dks, dvs, c_tab, s_tab, (dglu_a, dglu_b, dgate_a, dgate_b), t,
                                   side=_scatter_side([parts_bf_mid[n] for n in WAVE_MID]))
    halves_mid = _chip_sums(WAVE_MID, parts_mid, landed)

    gw_in_full, others = _mm_tn("in_proj_wgrad", u, dz, side=_share_side([halves_mid[n] for n in WAVE_MID]))
    gshard.update({n: _both_halves(halves_mid[n], o) for n, o in zip(WAVE_MID, others)})
    gw_in = {"w_in": gw_in_full}
    du_a, theirs = _mm("in_proj_bwd_a", dz, wfull["w_in"], nt=True, k_part=(0, 2), side=_swap_side([gw_in["w_in"]], [True]))
    parts, parts_bf = _pair_sums(("w_in",), gw_in, theirs)
    (gx, dg_mix), landed_in = _mm("in_proj_bwd_b", dz, wfull["w_in"], nt=True, k_part=(1, 2), extras=(du_a, x, d1),
                                  params=(g_mix,), epi=lambda acc, first, a, dres, g: _norm_bwd_epi(acc + first, a, dres, g)[1:],
                                  sums=[(1, D_MODEL)], side=_scatter_side([parts_bf["w_in"]]))
    halves = _chip_sums(("w_in",), parts, landed_in)

    gsmall = {"g_mix": dg_mix, "b_gate": dbg, "conv_b": dconv[CONV_K:CONV_K + 1], "conv_ln_g": dlng, "conv_ln_b": dlnb,
              "g_cross": dg_cross, "g_mem": dg_mem, "g_mlp": dg_mlp, "g_final": dg_final, "conv_w": dconv[:CONV_K]}
    return loss_row, gx, gshard, gsmall, halves["w_in"]


def kernel(x, mem, g_mix, w_in, b_gate, conv_w, conv_b, conv_ln_g, conv_ln_b, w_attn_proj, w_conv_proj, w_out, g_cross, g_mem, w_cq, w_ckv, w_co, g_mlp, w_up, w_down, g_final, loss_target, m_g_mix, m_w_in, m_b_gate, m_conv_w, m_conv_b, m_conv_ln_g, m_conv_ln_b, m_w_attn_proj, m_w_conv_proj, m_w_out, m_g_cross, m_g_mem, m_w_cq, m_w_ckv, m_w_co, m_g_mlp, m_w_up, m_w_down, m_g_final, v_g_mix, v_w_in, v_b_gate, v_conv_w, v_conv_b, v_conv_ln_g, v_conv_ln_b, v_w_attn_proj, v_w_conv_proj, v_w_out, v_g_cross, v_g_mem, v_w_cq, v_w_ckv, v_w_co, v_g_mlp, v_w_up, v_w_down, v_g_final):
    w = dict(g_mix=g_mix, w_in=w_in, b_gate=b_gate, conv_w=conv_w, conv_b=conv_b, conv_ln_g=conv_ln_g, conv_ln_b=conv_ln_b,
             w_attn_proj=w_attn_proj, w_conv_proj=w_conv_proj, w_out=w_out, g_cross=g_cross, g_mem=g_mem, w_cq=w_cq,
             w_ckv=w_ckv, w_co=w_co, g_mlp=g_mlp, w_up=w_up, w_down=w_down, g_final=g_final)
    mo = dict(g_mix=m_g_mix, w_in=m_w_in, b_gate=m_b_gate, conv_w=m_conv_w, conv_b=m_conv_b, conv_ln_g=m_conv_ln_g,
              conv_ln_b=m_conv_ln_b, w_attn_proj=m_w_attn_proj, w_conv_proj=m_w_conv_proj, w_out=m_w_out, g_cross=m_g_cross,
              g_mem=m_g_mem, w_cq=m_w_cq, w_ckv=m_w_ckv, w_co=m_w_co, g_mlp=m_g_mlp, w_up=m_w_up, w_down=m_w_down,
              g_final=m_g_final)
    vo = dict(g_mix=v_g_mix, w_in=v_w_in, b_gate=v_b_gate, conv_w=v_conv_w, conv_b=v_conv_b, conv_ln_g=v_conv_ln_g,
              conv_ln_b=v_conv_ln_b, w_attn_proj=v_w_attn_proj, w_conv_proj=v_w_conv_proj, w_out=v_w_out, g_cross=v_g_cross,
              g_mem=v_g_mem, w_cq=v_w_cq, w_ckv=v_w_ckv, w_co=v_w_co, g_mlp=v_g_mlp, w_up=v_w_up, w_down=v_w_down,
              g_final=v_g_final)
    shapes = {n: w[n].shape for n in ORDER}
    two_d = lambda a: a.reshape(a.shape[-2], a.shape[-1])
    chip = 2 * lax.axis_index("x") + lax.axis_index("y")

    shards = {n: two_d(w[n]).astype(BF) for n in BIG}
    cw_rows = 48
    cw_all, _ = _gather8("gather_conv_w", _pad_rows(conv_w.reshape(-1), cw_rows))
    cw_shard = CONV_K * (CONV_CH // 4)
    conv_w_full = jnp.concatenate(
        [cw_all[2 * j].reshape(-1)[:cw_shard].reshape(CONV_K, CONV_CH // 4) for j in range(4)], axis=1)

    small = {n: w[n] for n in SMALL}
    loss_row, gx, gshard, gsmall, half_w_in = _local_step(two_d(x), two_d(mem), two_d(loss_target), shards, small,
                                                          conv_w_full)
    loss = lax.psum(loss_row[0, 0], ("x", "y", "c"))

    small_names = SMALL + ("conv_w",)
    flat = jnp.concatenate([gsmall[n].reshape(-1) for n in small_names])
    sm_rows = -(-flat.shape[0] // (8 * LANES)) * 8
    (_, sm_sum), others = _gather8("reduce_small_grads", _pad_rows(flat, sm_rows), side=_share_side([half_w_in]))
    gshard["w_in"] = _both_halves(half_w_in, others[0])
    sm_sum = sm_sum.reshape(-1)
    off = 0
    for n in small_names:
        size = gsmall[n].size
        gshard[n] = sm_sum[off:off + size].reshape(gsmall[n].shape)
        off += size
    gshard["conv_w"] = lax.dynamic_slice_in_dim(gshard["conv_w"], chip * (CONV_CH // 4), CONV_CH // 4, axis=1)

    grads, deltas, new_m, new_v = {}, {}, {}, {}
    for n in BIG:
        d, m2, v2 = _adam(f"adamw_{n}", two_d(w[n]), gshard[n], two_d(mo[n]), two_d(vo[n]))
        grads[n], deltas[n], new_m[n], new_v[n] = (a.reshape(shapes[n]) for a in (gshard[n], d, m2, v2))
    pack = lambda src: jnp.concatenate([src[n].reshape(-1) for n in small_names])
    n_small = sum(w[n].size for n in small_names)
    ad_rows = -(-n_small // (8 * LANES)) * 8
    d, m2, v2 = _adam("adamw_small", *[_pad_rows(pack(src), ad_rows) for src in (w, gshard, mo, vo)])
    off = 0
    for n in small_names:
        size = w[n].size
        grads[n] = gshard[n].reshape(shapes[n])
        deltas[n], new_m[n], new_v[n] = (a.reshape(-1)[off:off + size].reshape(shapes[n]) for a in (d, m2, v2))
        off += size

    return (loss, gx.reshape(x.shape), *[grads[n] for n in ORDER], *[deltas[n] for n in ORDER],
            *[new_m[n] for n in ORDER], *[new_v[n] for n in ORDER])
```
